```python
import jax, jax.numpy as jnp
from jax import lax
import numpy as np

D_MODEL = 1024
BATCH = 8
SEQ = 8192
DEPTH = 2

N_MIXERS = 2
N_A_LAYERS = (DEPTH + 1) // 2
N_B_LAYERS = DEPTH // 2
EPS = 1e-6

D_FF = ((8 * D_MODEL // 3 + 127) // 128) * 128

A_DK = 128
A_DV = 128
A_HEADS = D_MODEL // A_DK
A_CONV = 4
A_CHUNK = 64

B_HD = 64
B_HEADS = D_MODEL // B_HD
B_KV_HEADS = 4
B_WINDOW = 128
B_BLOCK = 128

kernel_name = "hybrid_gdn_swa_sink_macaron"


def rmsnorm(x, w):
    xf = x.astype(jnp.float32)
    y = xf * lax.rsqrt(jnp.mean(xf * xf, axis=-1, keepdims=True) + EPS) * w.astype(jnp.float32)
    return y.astype(x.dtype)


def l2norm(x):
    xf = x.astype(jnp.float32)
    return xf * lax.rsqrt(jnp.sum(xf * xf, axis=-1, keepdims=True) + EPS)


def swiglu(h, w_gu, w_down):
    gate, up = jnp.split(h @ w_gu, 2, axis=-1)
    return (jax.nn.silu(gate) * up) @ w_down


def causal_depthwise_conv(x, w):
    C = x.shape[-1]
    return lax.conv_general_dilated(
        x, w[:, None, :].astype(x.dtype), window_strides=(1,), padding=[(A_CONV - 1, 0)],
        dimension_numbers=("NWC", "WIO", "NWC"), feature_group_count=C)


def gated_delta_rule_chunked(q, k, v, g, beta):
    Bsz, T, H, DK = q.shape
    DV = v.shape[-1]
    C = A_CHUNK
    N = T // C
    f32 = jnp.float32

    def chunk(t):
        return t.astype(f32).reshape(Bsz, N, C, H, -1).transpose(0, 3, 1, 2, 4)

    q, k, v = chunk(q), chunk(k), chunk(v)
    g = g.astype(f32).reshape(Bsz, N, C, H).transpose(0, 3, 1, 2)
    beta = beta.astype(f32).reshape(Bsz, N, C, H).transpose(0, 3, 1, 2)
    g = jnp.cumsum(g, axis=-1)

    idx = jnp.arange(C)
    lower_incl = idx[:, None] >= idx[None, :]
    strict = idx[:, None] > idx[None, :]
    decay = jnp.exp(jnp.where(lower_incl, g[..., :, None] - g[..., None, :], -jnp.inf))

    kb = k * beta[..., None]
    L = jnp.where(strict, jnp.einsum("bhnid,bhnjd->bhnij", kb, k) * decay, 0.0)
    rhs = jnp.concatenate([v * beta[..., None], kb * jnp.exp(g)[..., None]], axis=-1)
    sol = lax.linalg.triangular_solve(L, rhs, left_side=True, lower=True,
                                      transpose_a=False, conjugate_a=False, unit_diagonal=True)
    u, w = sol[..., :DV], sol[..., DV:]

    a_qk = jnp.einsum("bhnid,bhnjd->bhnij", q, k) * decay
    q_dec = q * jnp.exp(g)[..., None]
    k_dec = k * jnp.exp(g[..., -1:] - g)[..., None]
    g_last = jnp.exp(g[..., -1])

    xs = (jnp.moveaxis(q_dec, 2, 0), jnp.moveaxis(k_dec, 2, 0), jnp.moveaxis(u, 2, 0),
          jnp.moveaxis(w, 2, 0), jnp.moveaxis(a_qk, 2, 0), jnp.moveaxis(g_last, 2, 0))

    def step(S, inp):
        qd, kd, u_c, w_c, a_c, gl = inp
        v_new = u_c - jnp.einsum("bhck,bhkv->bhcv", w_c, S)
        o = jnp.einsum("bhck,bhkv->bhcv", qd, S) + jnp.einsum("bhij,bhjv->bhiv", a_c, v_new)
        S = S * gl[..., None, None] + jnp.einsum("bhck,bhcv->bhkv", kd, v_new)
        return S, o

    S0 = jnp.zeros((Bsz, H, DK, DV), f32)
    _, o = lax.scan(step, S0, xs)
    return o.transpose(1, 0, 3, 2, 4).reshape(Bsz, T, H, DV)


def mixer_gated_deltanet(h, w_in, w_conv, A_log, dt_bias, out_norm, w_out):
    Bsz, T, _ = h.shape
    HK = A_HEADS * A_DK
    HV = A_HEADS * A_DV
    proj = h @ w_in
    qkv = proj[..., :2 * HK + HV]
    z = proj[..., 2 * HK + HV:2 * HK + 2 * HV]
    b = proj[..., 2 * HK + 2 * HV:2 * HK + 2 * HV + A_HEADS]
    a = proj[..., 2 * HK + 2 * HV + A_HEADS:]
    qkv = jax.nn.silu(causal_depthwise_conv(qkv, w_conv))
    q = l2norm(qkv[..., :HK].reshape(Bsz, T, A_HEADS, A_DK)) * (A_DK ** -0.5)
    k = l2norm(qkv[..., HK:2 * HK].reshape(Bsz, T, A_HEADS, A_DK))
    v = qkv[..., 2 * HK:].reshape(Bsz, T, A_HEADS, A_DV)
    beta = jax.nn.sigmoid(b.astype(jnp.float32))
    g = -jnp.exp(A_log.astype(jnp.float32)) * jax.nn.softplus(a.astype(jnp.float32) + dt_bias.astype(jnp.float32))
    o = gated_delta_rule_chunked(q, k, v, g, beta)
    zf = z.reshape(Bsz, T, A_HEADS, A_DV).astype(jnp.float32)
    o = rmsnorm(o, out_norm) * jax.nn.silu(zf)
    return o.reshape(Bsz, T, HV).astype(h.dtype) @ w_out


def mixer_sliding_window_sinks(h, w_in, b_in, sinks, w_out, b_out):
    Bsz, T, _ = h.shape
    G = B_HEADS // B_KV_HEADS
    NB = T // B_BLOCK
    HQ = B_HEADS * B_HD
    HKV = B_KV_HEADS * B_HD
    proj = h @ w_in + b_in
    q = proj[..., :HQ].reshape(Bsz, NB, B_BLOCK, B_KV_HEADS, G, B_HD)
    k = proj[..., HQ:HQ + HKV].reshape(Bsz, NB, B_BLOCK, B_KV_HEADS, B_HD)
    v = proj[..., HQ + HKV:].reshape(Bsz, NB, B_BLOCK, B_KV_HEADS, B_HD)

    def with_prev(t):
        prev = jnp.concatenate([jnp.zeros_like(t[:, :1]), t[:, :-1]], axis=1)
        return jnp.concatenate([prev, t], axis=2)

    kk, vv = with_prev(k), with_prev(v)
    s = jnp.einsum("bnqhgd,bnkhd->bnhgqk", q, kk).astype(jnp.float32) * (B_HD ** -0.5)
    qi = jnp.arange(B_BLOCK)[:, None]
    kj = jnp.arange(2 * B_BLOCK)[None, :]
    rel = qi + B_BLOCK - kj
    band = (rel >= 0) & (rel < B_WINDOW)
    blk = jnp.arange(NB)[:, None, None]
    valid = band[None] & ((blk > 0) | (kj >= B_BLOCK)[None])
    s = jnp.where(valid[None, :, None, None], s, -jnp.inf)
    sink = jnp.broadcast_to(sinks.astype(jnp.float32).reshape(B_KV_HEADS, G)[None, None, :, :, None, None],
                            s.shape[:-1] + (1,))
    p = jax.nn.softmax(jnp.concatenate([s, sink], axis=-1), axis=-1)[..., :-1]
    o = jnp.einsum("bnhgqk,bnkhd->bnqhgd", p.astype(vv.dtype), vv)
    return o.reshape(Bsz, T, HQ) @ w_out + b_out


def _fwd_setup_inputs(seed: int = 0) -> dict:
    key = jax.random.key(seed)
    ks = iter(jax.random.split(key, 32))
    f32 = jnp.float32
    nrm = lambda shape, scale: jax.random.normal(next(ks), shape, f32) * scale
    gain = lambda shape: 1.0 + 0.02 * jax.random.normal(next(ks), shape, f32)
    D = D_MODEL
    a_in_cols = 2 * A_HEADS * A_DK + 2 * A_HEADS * A_DV + 2 * A_HEADS
    b_in_cols = (B_HEADS + 2 * B_KV_HEADS) * B_HD
    dt = jnp.exp(jax.random.uniform(next(ks), (N_A_LAYERS, A_HEADS), f32, np.log(1e-3), np.log(1e-1)))
    return {
        "x": jax.random.normal(next(ks), (BATCH, SEQ, D), f32),
        "ffn1_norm": gain((DEPTH, D)),
        "ffn1_w_gu": nrm((DEPTH, D, 2 * D_FF), D ** -0.5),
        "ffn1_w_down": nrm((DEPTH, D_FF, D), D_FF ** -0.5),
        "mix_norm": gain((DEPTH, D)),
        "ffn2_norm": gain((DEPTH, D)),
        "ffn2_w_gu": nrm((DEPTH, D, 2 * D_FF), D ** -0.5),
        "ffn2_w_down": nrm((DEPTH, D_FF, D), D_FF ** -0.5),
        "a_w_in": nrm((N_A_LAYERS, D, a_in_cols), D ** -0.5),
        "a_w_conv": nrm((N_A_LAYERS, A_CONV, 2 * A_HEADS * A_DK + A_HEADS * A_DV), A_CONV ** -0.5),
        "a_A_log": jnp.log(jax.random.uniform(next(ks), (N_A_LAYERS, A_HEADS), f32, 1.0, 16.0)),
        "a_dt_bias": dt + jnp.log(-jnp.expm1(-dt)),
        "a_out_norm": gain((N_A_LAYERS, A_DV)),
        "a_w_out": nrm((N_A_LAYERS, A_HEADS * A_DV, D), (A_HEADS * A_DV) ** -0.5),
        "b_w_in": nrm((N_B_LAYERS, D, b_in_cols), D ** -0.5),
        "b_b_in": nrm((N_B_LAYERS, b_in_cols), 0.02),
        "b_sinks": nrm((N_B_LAYERS, B_HEADS), 1.0),
        "b_w_out": nrm((N_B_LAYERS, B_HEADS * B_HD, D), (B_HEADS * B_HD) ** -0.5),
        "b_b_out": nrm((N_B_LAYERS, D), 0.02),
        "final_norm": gain((D,)),
    }


def _fwd_reference(x, ffn1_norm, ffn1_w_gu, ffn1_w_down, mix_norm, ffn2_norm, ffn2_w_gu, ffn2_w_down,
              a_w_in, a_w_conv, a_A_log, a_dt_bias, a_out_norm, a_w_out,
              b_w_in, b_b_in, b_sinks, b_w_out, b_b_out, final_norm):
    for layer in range(DEPTH):
        x = x + 0.5 * swiglu(rmsnorm(x, ffn1_norm[layer]), ffn1_w_gu[layer], ffn1_w_down[layer])
        h = rmsnorm(x, mix_norm[layer])
        j = layer // N_MIXERS
        if layer % N_MIXERS == 0:
            y = mixer_gated_deltanet(h, a_w_in[j], a_w_conv[j], a_A_log[j], a_dt_bias[j],
                                     a_out_norm[j], a_w_out[j])
        else:
            y = mixer_sliding_window_sinks(h, b_w_in[j], b_b_in[j], b_sinks[j], b_w_out[j], b_b_out[j])
        x = x + y
        x = x + 0.5 * swiglu(rmsnorm(x, ffn2_norm[layer]), ffn2_w_gu[layer], ffn2_w_down[layer])
    return rmsnorm(x, final_norm)


import jax as _jax
import jax.numpy as _jnp

TWIN_FORMAT = 'train_step'
FWD_PARAMS = ['x', 'ffn1_norm', 'ffn1_w_gu', 'ffn1_w_down', 'mix_norm', 'ffn2_norm', 'ffn2_w_gu', 'ffn2_w_down', 'a_w_in', 'a_w_conv', 'a_A_log', 'a_dt_bias', 'a_out_norm', 'a_w_out', 'b_w_in', 'b_b_in', 'b_sinks', 'b_w_out', 'b_b_out', 'final_norm']
TWIN_WEIGHTS = ['ffn1_norm', 'ffn1_w_gu', 'ffn1_w_down', 'mix_norm', 'ffn2_norm', 'ffn2_w_gu', 'ffn2_w_down', 'a_w_in', 'a_w_conv', 'a_A_log', 'a_dt_bias', 'a_out_norm', 'a_w_out', 'b_w_in', 'b_b_in', 'b_sinks', 'b_w_out', 'b_b_out', 'final_norm']
TWIN_DIFF_INPUT = 'x'
TWIN_INPUTS = ['x', 'ffn1_norm', 'ffn1_w_gu', 'ffn1_w_down', 'mix_norm', 'ffn2_norm', 'ffn2_w_gu', 'ffn2_w_down', 'a_w_in', 'a_w_conv', 'a_A_log', 'a_dt_bias', 'a_out_norm', 'a_w_out', 'b_w_in', 'b_b_in', 'b_sinks', 'b_w_out', 'b_b_out', 'final_norm', 'loss_target', 'm_ffn1_norm', 'm_ffn1_w_gu', 'm_ffn1_w_down', 'm_mix_norm', 'm_ffn2_norm', 'm_ffn2_w_gu', 'm_ffn2_w_down', 'm_a_w_in', 'm_a_w_conv', 'm_a_A_log', 'm_a_dt_bias', 'm_a_out_norm', 'm_a_w_out', 'm_b_w_in', 'm_b_b_in', 'm_b_sinks', 'm_b_w_out', 'm_b_b_out', 'm_final_norm', 'v_ffn1_norm', 'v_ffn1_w_gu', 'v_ffn1_w_down', 'v_mix_norm', 'v_ffn2_norm', 'v_ffn2_w_gu', 'v_ffn2_w_down', 'v_a_w_in', 'v_a_w_conv', 'v_a_A_log', 'v_a_dt_bias', 'v_a_out_norm', 'v_a_w_out', 'v_b_w_in', 'v_b_b_in', 'v_b_sinks', 'v_b_w_out', 'v_b_b_out', 'v_final_norm']
TWIN_OUTPUTS = ['loss', 'grad_x', 'grad_ffn1_norm', 'grad_ffn1_w_gu', 'grad_ffn1_w_down', 'grad_mix_norm', 'grad_ffn2_norm', 'grad_ffn2_w_gu', 'grad_ffn2_w_down', 'grad_a_w_in', 'grad_a_w_conv', 'grad_a_A_log', 'grad_a_dt_bias', 'grad_a_out_norm', 'grad_a_w_out', 'grad_b_w_in', 'grad_b_b_in', 'grad_b_sinks', 'grad_b_w_out', 'grad_b_b_out', 'grad_final_norm', 'delta_ffn1_norm', 'delta_ffn1_w_gu', 'delta_ffn1_w_down', 'delta_mix_norm', 'delta_ffn2_norm', 'delta_ffn2_w_gu', 'delta_ffn2_w_down', 'delta_a_w_in', 'delta_a_w_conv', 'delta_a_A_log', 'delta_a_dt_bias', 'delta_a_out_norm', 'delta_a_w_out', 'delta_b_w_in', 'delta_b_b_in', 'delta_b_sinks', 'delta_b_w_out', 'delta_b_b_out', 'delta_final_norm', 'new_m_ffn1_norm', 'new_m_ffn1_w_gu', 'new_m_ffn1_w_down', 'new_m_mix_norm', 'new_m_ffn2_norm', 'new_m_ffn2_w_gu', 'new_m_ffn2_w_down', 'new_m_a_w_in', 'new_m_a_w_conv', 'new_m_a_A_log', 'new_m_a_dt_bias', 'new_m_a_out_norm', 'new_m_a_w_out', 'new_m_b_w_in', 'new_m_b_b_in', 'new_m_b_sinks', 'new_m_b_w_out', 'new_m_b_b_out', 'new_m_final_norm', 'new_v_ffn1_norm', 'new_v_ffn1_w_gu', 'new_v_ffn1_w_down', 'new_v_mix_norm', 'new_v_ffn2_norm', 'new_v_ffn2_w_gu', 'new_v_ffn2_w_down', 'new_v_a_w_in', 'new_v_a_w_conv', 'new_v_a_A_log', 'new_v_a_dt_bias', 'new_v_a_out_norm', 'new_v_a_w_out', 'new_v_b_w_in', 'new_v_b_b_in', 'new_v_b_sinks', 'new_v_b_w_out', 'new_v_b_b_out', 'new_v_final_norm']
TWIN_LEAF_KINDS = {'loss': 'loss', 'grad_x': 'grad_x', 'grad_ffn1_norm': 'grad_w', 'grad_ffn1_w_gu': 'grad_w', 'grad_ffn1_w_down': 'grad_w', 'grad_mix_norm': 'grad_w', 'grad_ffn2_norm': 'grad_w', 'grad_ffn2_w_gu': 'grad_w', 'grad_ffn2_w_down': 'grad_w', 'grad_a_w_in': 'grad_w', 'grad_a_w_conv': 'grad_w', 'grad_a_A_log': 'grad_w', 'grad_a_dt_bias': 'grad_w', 'grad_a_out_norm': 'grad_w', 'grad_a_w_out': 'grad_w', 'grad_b_w_in': 'grad_w', 'grad_b_b_in': 'grad_w', 'grad_b_sinks': 'grad_w', 'grad_b_w_out': 'grad_w', 'grad_b_b_out': 'grad_w', 'grad_final_norm': 'grad_w', 'delta_ffn1_norm': 'delta_w', 'delta_ffn1_w_gu': 'delta_w', 'delta_ffn1_w_down': 'delta_w', 'delta_mix_norm': 'delta_w', 'delta_ffn2_norm': 'delta_w', 'delta_ffn2_w_gu': 'delta_w', 'delta_ffn2_w_down': 'delta_w', 'delta_a_w_in': 'delta_w', 'delta_a_w_conv': 'delta_w', 'delta_a_A_log': 'delta_w', 'delta_a_dt_bias': 'delta_w', 'delta_a_out_norm': 'delta_w', 'delta_a_w_out': 'delta_w', 'delta_b_w_in': 'delta_w', 'delta_b_b_in': 'delta_w', 'delta_b_sinks': 'delta_w', 'delta_b_w_out': 'delta_w', 'delta_b_b_out': 'delta_w', 'delta_final_norm': 'delta_w', 'new_m_ffn1_norm': 'new_m', 'new_m_ffn1_w_gu': 'new_m', 'new_m_ffn1_w_down': 'new_m', 'new_m_mix_norm': 'new_m', 'new_m_ffn2_norm': 'new_m', 'new_m_ffn2_w_gu': 'new_m', 'new_m_ffn2_w_down': 'new_m', 'new_m_a_w_in': 'new_m', 'new_m_a_w_conv': 'new_m', 'new_m_a_A_log': 'new_m', 'new_m_a_dt_bias': 'new_m', 'new_m_a_out_norm': 'new_m', 'new_m_a_w_out': 'new_m', 'new_m_b_w_in': 'new_m', 'new_m_b_b_in': 'new_m', 'new_m_b_sinks': 'new_m', 'new_m_b_w_out': 'new_m', 'new_m_b_b_out': 'new_m', 'new_m_final_norm': 'new_m', 'new_v_ffn1_norm': 'new_v', 'new_v_ffn1_w_gu': 'new_v', 'new_v_ffn1_w_down': 'new_v', 'new_v_mix_norm': 'new_v', 'new_v_ffn2_norm': 'new_v', 'new_v_ffn2_w_gu': 'new_v', 'new_v_ffn2_w_down': 'new_v', 'new_v_a_w_in': 'new_v', 'new_v_a_w_conv': 'new_v', 'new_v_a_A_log': 'new_v', 'new_v_a_dt_bias': 'new_v', 'new_v_a_out_norm': 'new_v', 'new_v_a_w_out': 'new_v', 'new_v_b_w_in': 'new_v', 'new_v_b_b_in': 'new_v', 'new_v_b_sinks': 'new_v', 'new_v_b_w_out': 'new_v', 'new_v_b_b_out': 'new_v', 'new_v_final_norm': 'new_v'}


def _forward(args):
    return _fwd_reference(*[args[k] for k in FWD_PARAMS])


def _output_shape():
    def fwd():
        inp = _fwd_setup_inputs(0)
        return _fwd_reference(*[inp[k] for k in FWD_PARAMS])
    out = _jax.eval_shape(fwd)
    return out.shape, out.dtype

N_MICROBATCH = 1
ADAM_LR = 0.001
ADAM_B1 = 0.9
ADAM_B2 = 0.999
ADAM_EPS = 1e-08
ADAM_WD = 0.01
ADAM_STEP = 10
PER_EXAMPLE_BATCH_AXIS = {'x': 0, 'loss_target': 0}
SHARED_INPUTS = []
_WEIGHT_DTYPES = {'ffn1_norm': _jnp.float32, 'ffn1_w_gu': _jnp.float32, 'ffn1_w_down': _jnp.float32, 'mix_norm': _jnp.float32, 'ffn2_norm': _jnp.float32, 'ffn2_w_gu': _jnp.float32, 'ffn2_w_down': _jnp.float32, 'a_w_in': _jnp.float32, 'a_w_conv': _jnp.float32, 'a_A_log': _jnp.float32, 'a_dt_bias': _jnp.float32, 'a_out_norm': _jnp.float32, 'a_w_out': _jnp.float32, 'b_w_in': _jnp.float32, 'b_b_in': _jnp.float32, 'b_sinks': _jnp.float32, 'b_w_out': _jnp.float32, 'b_b_out': _jnp.float32, 'final_norm': _jnp.float32}
MOMENT_SCALE = {'ffn1_norm': 1.166819e-01, 'ffn1_w_gu': 4.912406e-02, 'ffn1_w_down': 8.027279e-02, 'mix_norm': 1.769549e-01, 'ffn2_norm': 8.801480e-02, 'ffn2_w_gu': 3.737294e-02, 'ffn2_w_down': 6.103254e-02, 'a_w_in': 1.141804e-01, 'a_w_conv': 1.062892e-01, 'a_A_log': 1.112471e+00, 'a_dt_bias': 1.079794e+00, 'a_out_norm': 4.121559e-01, 'a_w_out': 1.391805e-01, 'b_w_in': 5.431323e-02, 'b_b_in': 2.841690e-01, 'b_sinks': 2.776178e-02, 'b_w_out': 4.437543e-02, 'b_b_out': 2.644016e-01, 'final_norm': 6.399559e+01}


def _to_microbatches(a, axis):
    t = _jnp.moveaxis(a, axis, 0)
    t = t.reshape((N_MICROBATCH, t.shape[0] // N_MICROBATCH) + t.shape[1:])
    return _jnp.moveaxis(t, 1, axis + 1)


def setup_inputs(seed: int = 0) -> dict:
    inp = _fwd_setup_inputs(seed)
    key = _jax.random.fold_in(_jax.random.key(seed), 7919)
    shape, _ = _output_shape()
    out = dict(inp)
    out["loss_target"] = _jax.random.normal(_jax.random.fold_in(key, 0), shape, _jnp.float32)
    for i, name in enumerate(TWIN_WEIGHTS):
        w = inp[name].astype(_jnp.float32)
        if MOMENT_SCALE is None:
            s = _jnp.sqrt(_jnp.mean(_jnp.square(w)) + 1e-30)
        else:
            s = MOMENT_SCALE[name]
        km, kv = _jax.random.split(_jax.random.fold_in(key, i + 1))
        out[name] = w
        out["m_" + name] = s * _jax.random.normal(km, w.shape, _jnp.float32)
        out["v_" + name] = (s * s) * _jax.random.uniform(kv, w.shape, _jnp.float32, 0.5, 1.5)
    if N_MICROBATCH > 1:
        for name, axis in PER_EXAMPLE_BATCH_AXIS.items():
            out[name] = _to_microbatches(out[name], axis)
    return {'x': out['x'], 'ffn1_norm': out['ffn1_norm'], 'ffn1_w_gu': out['ffn1_w_gu'], 'ffn1_w_down': out['ffn1_w_down'], 'mix_norm': out['mix_norm'], 'ffn2_norm': out['ffn2_norm'], 'ffn2_w_gu': out['ffn2_w_gu'], 'ffn2_w_down': out['ffn2_w_down'], 'a_w_in': out['a_w_in'], 'a_w_conv': out['a_w_conv'], 'a_A_log': out['a_A_log'], 'a_dt_bias': out['a_dt_bias'], 'a_out_norm': out['a_out_norm'], 'a_w_out': out['a_w_out'], 'b_w_in': out['b_w_in'], 'b_b_in': out['b_b_in'], 'b_sinks': out['b_sinks'], 'b_w_out': out['b_w_out'], 'b_b_out': out['b_b_out'], 'final_norm': out['final_norm'], 'loss_target': out['loss_target'], 'm_ffn1_norm': out['m_ffn1_norm'], 'm_ffn1_w_gu': out['m_ffn1_w_gu'], 'm_ffn1_w_down': out['m_ffn1_w_down'], 'm_mix_norm': out['m_mix_norm'], 'm_ffn2_norm': out['m_ffn2_norm'], 'm_ffn2_w_gu': out['m_ffn2_w_gu'], 'm_ffn2_w_down': out['m_ffn2_w_down'], 'm_a_w_in': out['m_a_w_in'], 'm_a_w_conv': out['m_a_w_conv'], 'm_a_A_log': out['m_a_A_log'], 'm_a_dt_bias': out['m_a_dt_bias'], 'm_a_out_norm': out['m_a_out_norm'], 'm_a_w_out': out['m_a_w_out'], 'm_b_w_in': out['m_b_w_in'], 'm_b_b_in': out['m_b_b_in'], 'm_b_sinks': out['m_b_sinks'], 'm_b_w_out': out['m_b_w_out'], 'm_b_b_out': out['m_b_b_out'], 'm_final_norm': out['m_final_norm'], 'v_ffn1_norm': out['v_ffn1_norm'], 'v_ffn1_w_gu': out['v_ffn1_w_gu'], 'v_ffn1_w_down': out['v_ffn1_w_down'], 'v_mix_norm': out['v_mix_norm'], 'v_ffn2_norm': out['v_ffn2_norm'], 'v_ffn2_w_gu': out['v_ffn2_w_gu'], 'v_ffn2_w_down': out['v_ffn2_w_down'], 'v_a_w_in': out['v_a_w_in'], 'v_a_w_conv': out['v_a_w_conv'], 'v_a_A_log': out['v_a_A_log'], 'v_a_dt_bias': out['v_a_dt_bias'], 'v_a_out_norm': out['v_a_out_norm'], 'v_a_w_out': out['v_a_w_out'], 'v_b_w_in': out['v_b_w_in'], 'v_b_b_in': out['v_b_b_in'], 'v_b_sinks': out['v_b_sinks'], 'v_b_w_out': out['v_b_w_out'], 'v_b_b_out': out['v_b_b_out'], 'v_final_norm': out['v_final_norm']}


def _loss(weights, diff, rest, loss_target):
    with _jax.named_scope("forward"):
        args = {**rest, TWIN_DIFF_INPUT: diff, **{k: w.astype(_WEIGHT_DTYPES[k]) for k, w in weights.items()}}
        y = _forward(args)
    with _jax.named_scope("loss_head"):
        err = _jnp.square(y.astype(_jnp.float32) - loss_target)
        return 0.5 * _jnp.sum(_jnp.mean(err, axis=-1)) if err.ndim else 0.5 * err


def _adamw(w, g, m, v):
    m = ADAM_B1 * m + (1.0 - ADAM_B1) * g
    v = ADAM_B2 * v + (1.0 - ADAM_B2) * _jnp.square(g)
    m_hat = m / (1.0 - ADAM_B1 ** ADAM_STEP)
    v_hat = v / (1.0 - ADAM_B2 ** ADAM_STEP)
    delta = -ADAM_LR * (m_hat / (_jnp.sqrt(v_hat) + ADAM_EPS) + ADAM_WD * w)
    return delta, m, v


def reference(x, ffn1_norm, ffn1_w_gu, ffn1_w_down, mix_norm, ffn2_norm, ffn2_w_gu, ffn2_w_down, a_w_in, a_w_conv, a_A_log, a_dt_bias, a_out_norm, a_w_out, b_w_in, b_b_in, b_sinks, b_w_out, b_b_out, final_norm, loss_target, m_ffn1_norm, m_ffn1_w_gu, m_ffn1_w_down, m_mix_norm, m_ffn2_norm, m_ffn2_w_gu, m_ffn2_w_down, m_a_w_in, m_a_w_conv, m_a_A_log, m_a_dt_bias, m_a_out_norm, m_a_w_out, m_b_w_in, m_b_b_in, m_b_sinks, m_b_w_out, m_b_b_out, m_final_norm, v_ffn1_norm, v_ffn1_w_gu, v_ffn1_w_down, v_mix_norm, v_ffn2_norm, v_ffn2_w_gu, v_ffn2_w_down, v_a_w_in, v_a_w_conv, v_a_A_log, v_a_dt_bias, v_a_out_norm, v_a_w_out, v_b_w_in, v_b_b_in, v_b_sinks, v_b_w_out, v_b_b_out, v_final_norm):
    given = dict(x=x, ffn1_norm=ffn1_norm, ffn1_w_gu=ffn1_w_gu, ffn1_w_down=ffn1_w_down, mix_norm=mix_norm, ffn2_norm=ffn2_norm, ffn2_w_gu=ffn2_w_gu, ffn2_w_down=ffn2_w_down, a_w_in=a_w_in, a_w_conv=a_w_conv, a_A_log=a_A_log, a_dt_bias=a_dt_bias, a_out_norm=a_out_norm, a_w_out=a_w_out, b_w_in=b_w_in, b_b_in=b_b_in, b_sinks=b_sinks, b_w_out=b_w_out, b_b_out=b_b_out, final_norm=final_norm, loss_target=loss_target, m_ffn1_norm=m_ffn1_norm, m_ffn1_w_gu=m_ffn1_w_gu, m_ffn1_w_down=m_ffn1_w_down, m_mix_norm=m_mix_norm, m_ffn2_norm=m_ffn2_norm, m_ffn2_w_gu=m_ffn2_w_gu, m_ffn2_w_down=m_ffn2_w_down, m_a_w_in=m_a_w_in, m_a_w_conv=m_a_w_conv, m_a_A_log=m_a_A_log, m_a_dt_bias=m_a_dt_bias, m_a_out_norm=m_a_out_norm, m_a_w_out=m_a_w_out, m_b_w_in=m_b_w_in, m_b_b_in=m_b_b_in, m_b_sinks=m_b_sinks, m_b_w_out=m_b_w_out, m_b_b_out=m_b_b_out, m_final_norm=m_final_norm, v_ffn1_norm=v_ffn1_norm, v_ffn1_w_gu=v_ffn1_w_gu, v_ffn1_w_down=v_ffn1_w_down, v_mix_norm=v_mix_norm, v_ffn2_norm=v_ffn2_norm, v_ffn2_w_gu=v_ffn2_w_gu, v_ffn2_w_down=v_ffn2_w_down, v_a_w_in=v_a_w_in, v_a_w_conv=v_a_w_conv, v_a_A_log=v_a_A_log, v_a_dt_bias=v_a_dt_bias, v_a_out_norm=v_a_out_norm, v_a_w_out=v_a_w_out, v_b_w_in=v_b_w_in, v_b_b_in=v_b_b_in, v_b_sinks=v_b_sinks, v_b_w_out=v_b_w_out, v_b_b_out=v_b_b_out, v_final_norm=v_final_norm)
    weights = {n: given[n] for n in TWIN_WEIGHTS}
    shared = {n: given[n] for n in SHARED_INPUTS}
    per_example = {n: given[n] for n in ['x']}
    grad_fn = _jax.value_and_grad(_loss, argnums=(0, 1))

    def one_microbatch(ex, loss_target):
        ex = dict(ex)
        diff = ex.pop(TWIN_DIFF_INPUT)
        return grad_fn(weights, diff, {**shared, **ex}, loss_target)

    if N_MICROBATCH == 1:
        loss, (grad_w, grad_x) = one_microbatch(per_example, given["loss_target"])
    else:
        def body(carry, xs):
            loss_sum, grad_sum = carry
            l_k, (gw_k, gx_k) = one_microbatch(xs[0], xs[1])
            with _jax.named_scope("update"):
                return (loss_sum + l_k, _jax.tree.map(_jnp.add, grad_sum, gw_k)), gx_k

        init = (_jnp.zeros((), _jnp.float32), _jax.tree.map(_jnp.zeros_like, weights))
        (loss, grad_w), grad_x = _jax.lax.scan(body, init, (per_example, given["loss_target"]))
    with _jax.named_scope("update"):
        delta_w, new_m, new_v = {}, {}, {}
        for n in TWIN_WEIGHTS:
            delta_w[n], new_m[n], new_v[n] = _adamw(weights[n], grad_w[n], given["m_" + n], given["v_" + n])
    return (loss, grad_x, *[grad_w[n] for n in TWIN_WEIGHTS], *[delta_w[n] for n in TWIN_WEIGHTS],
            *[new_m[n] for n in TWIN_WEIGHTS], *[new_v[n] for n in TWIN_WEIGHTS])
```

```python
import jax
import jax.numpy as jnp
from jax import lax
from jax.experimental import pallas as pl
from jax.experimental.pallas import tpu as pltpu

F32 = jnp.float32
BF16 = jnp.bfloat16

D_MODEL = 1024
EPS = 1e-6
FF_BLK = 1408
A_HEADS = 8
A_DK = 128
A_CHUNK = 64
A_IN_COLS = 4112
A_IN_PAD = 4224
B_HEADS = 16
B_KV = 4
B_HD = 64
B_BLK = 128
ADAM_LR, ADAM_B1, ADAM_B2, ADAM_EPS, ADAM_WD, ADAM_STEP = 0.001, 0.9, 0.999, 1e-08, 0.01, 10
MESH = pl.DeviceIdType.MESH
VMEM_LIMIT = 56 * 1024 * 1024


def _params(n_axes):
    return pltpu.CompilerParams(dimension_semantics=("arbitrary",) * n_axes, vmem_limit_bytes=VMEM_LIMIT)


def _sigmoid(x):
    return 1.0 / (1.0 + jnp.exp(-x))


def _dot(a, b, ca, cb):
    return lax.dot_general(a, b, (((ca,), (cb,)), ((), ())), preferred_element_type=F32)


def _dotb(a, b, ca=1, cb=0):
    return _dot(a.astype(BF16), b.astype(BF16), ca, cb)


def _dotx(a, b, ca=1, cb=0):
    return lax.dot_general(a, b, (((ca,), (cb,)), ((), ())), preferred_element_type=F32,
                           precision=lax.Precision.HIGHEST)


def _rms_fwd(name, x, w):
    T, D = x.shape
    tt = min(T, 512)

    def body(x_ref, w_ref, h_ref):
        xv = x_ref[...]
        r = lax.rsqrt(jnp.mean(xv * xv, axis=-1, keepdims=True) + EPS)
        h_ref[...] = (xv * r * w_ref[...]).astype(BF16)

    return pl.pallas_call(
        body, grid=(T // tt,),
        in_specs=[pl.BlockSpec((tt, D), lambda i: (i, 0)), pl.BlockSpec((1, D), lambda i: (0, 0))],
        out_specs=pl.BlockSpec((tt, D), lambda i: (i, 0)),
        out_shape=jax.ShapeDtypeStruct((T, D), BF16), name=name, compiler_params=_params(1))(x, w)


def _rms_bwd_tile(dh, xv, dy, w):
    r = lax.rsqrt(jnp.mean(xv * xv, axis=-1, keepdims=True) + EPS)
    xhat = xv * r
    dxhat = dh * w
    dx = dy + r * (dxhat - xhat * jnp.mean(dxhat * xhat, axis=-1, keepdims=True))
    return dx, jnp.sum(dh * xhat, axis=0, keepdims=True)


def _colsum(name, a):
    T, N = a.shape
    tt = min(T, 512)

    def body(a_ref, o_ref):
        @pl.when(pl.program_id(0) == 0)
        def _():
            o_ref[...] = jnp.zeros_like(o_ref)
        o_ref[...] += jnp.sum(a_ref[...].astype(F32), axis=0, keepdims=True)

    return pl.pallas_call(
        body, grid=(T // tt,), in_specs=[pl.BlockSpec((tt, N), lambda i: (i, 0))],
        out_specs=pl.BlockSpec((1, N), lambda i: (0, 0)),
        out_shape=jax.ShapeDtypeStruct((1, N), F32), name=name, compiler_params=_params(1))(a)


def _matmul(name, a, b, ca, cb, tm, tn, tk, extra_in, outs, epi, order="ji"):
    M, K, N = a.shape[1 - ca], a.shape[ca], b.shape[1 - cb]
    tm, tn, tk = min(tm, M), min(tn, N), min(tk, K)
    assert M % tm == 0 and N % tn == 0 and K % tk == 0, (name, M, N, K, tm, tn, tk)
    ni, nj, nk = M // tm, N // tn, K // tk
    if order == "ji":
        grid = (nj, ni, nk)
        perm = lambda g0, g1, g2: (g1, g0, g2)
    else:
        grid = (ni, nj, nk)
        perm = lambda g0, g1, g2: (g0, g1, g2)

    def wrap(f):
        return lambda g0, g1, g2: f(*perm(g0, g1, g2))

    a_spec = (pl.BlockSpec((tm, tk), wrap(lambda i, j, k: (i, k))) if ca == 1
              else pl.BlockSpec((tk, tm), wrap(lambda i, j, k: (k, i))))
    b_spec = (pl.BlockSpec((tk, tn), wrap(lambda i, j, k: (k, j))) if cb == 0
              else pl.BlockSpec((tn, tk), wrap(lambda i, j, k: (j, k))))
    ne, no = len(extra_in), len(outs)

    def body(*refs):
        a_ref, b_ref = refs[0], refs[1]
        ex, out = refs[2:2 + ne], refs[2 + ne:2 + ne + no]
        i, j, k = perm(pl.program_id(0), pl.program_id(1), pl.program_id(2))
        p = _dotb(a_ref[...], b_ref[...], ca, cb)
        if nk == 1:
            epi(p, ex, out, i, j)
        else:
            acc_ref = refs[-1]

            @pl.when(k == 0)
            def _():
                acc_ref[...] = p

            @pl.when(k > 0)
            def _():
                acc_ref[...] += p

            @pl.when(k == nk - 1)
            def _():
                epi(acc_ref[...], ex, out, i, j)

    return pl.pallas_call(
        body, grid=grid,
        in_specs=[a_spec, b_spec] + [pl.BlockSpec(bs, wrap(f)) for _, bs, f in extra_in],
        out_specs=[pl.BlockSpec(bs, wrap(f)) for _, bs, f in outs],
        out_shape=[s for s, _, _ in outs],
        scratch_shapes=[pltpu.VMEM((tm, tn), F32)] if nk > 1 else [],
        name=name, compiler_params=_params(3))(a, b, *[x for x, _, _ in extra_in])


def _mm_plain(name, a, b, ca, cb, out_dtype, tm=1024, tn=1024, tk=1024, scale=1.0, bias=None):
    M, N = a.shape[1 - ca], b.shape[1 - cb]
    tm, tn = min(tm, M), min(tn, N)
    extra = [] if bias is None else [(bias, (1, tn), lambda i, j, k: (0, j))]

    def epi(acc, ex, out, i, j):
        r = acc * scale if scale != 1.0 else acc
        if bias is not None:
            r = r + ex[0][...]
        out[0][...] = r.astype(out_dtype)

    return _matmul(name, a, b, ca, cb, tm, tn, tk, extra,
                   [(jax.ShapeDtypeStruct((M, N), out_dtype), (tm, tn), lambda i, j, k: (i, j))], epi)[0]


def _mm_residual(name, a, b, x, scale, bias=None, tk=1024):
    M, N = x.shape
    tm, tn = min(512, M), N
    extra = [(x, (tm, tn), lambda i, j, k: (i, j))]
    if bias is not None:
        extra.append((bias, (1, tn), lambda i, j, k: (0, j)))

    def epi(acc, ex, out, i, j):
        r = acc if bias is None else acc + ex[1][...]
        out[0][...] = ex[0][...] + scale * r

    return _matmul(name, a, b, 1, 0, tm, tn, tk, extra,
                   [(jax.ShapeDtypeStruct((M, N), F32), (tm, tn), lambda i, j, k: (i, j))], epi, order="ij")[0]


def _mm_rms_bwd(name, dproj, w_in, x, dy, nw, tk):
    M, N = x.shape
    tm = min(512, M)
    extra = [(x, (tm, N), lambda i, j, k: (i, 0)), (dy, (tm, N), lambda i, j, k: (i, 0)),
             (nw, (1, N), lambda i, j, k: (0, 0))]

    def epi(acc, ex, out, i, j):
        dx, dw = _rms_bwd_tile(acc, ex[0][...], ex[1][...], ex[2][...])
        out[0][...] = dx
        out[1][...] = dx.astype(BF16)

        @pl.when(i == 0)
        def _():
            out[2][...] = dw

        @pl.when(i > 0)
        def _():
            out[2][...] += dw

    return _matmul(name, dproj, w_in, 1, 1, tm, N, tk, extra,
                   [(jax.ShapeDtypeStruct((M, N), F32), (tm, N), lambda i, j, k: (i, 0)),
                    (jax.ShapeDtypeStruct((M, N), BF16), (tm, N), lambda i, j, k: (i, 0)),
                    (jax.ShapeDtypeStruct((1, N), F32), (1, N), lambda i, j, k: (0, 0))], epi, order="ij")


def _ffn_gu(name, h, wgu):
    T, D = h.shape
    tm = min(T, 1024)
    rs = min(tm, 256)

    def body(h_ref, w_ref, gu_ref, act_ref):
        for r in range(tm // rs):
            rows = pl.ds(r * rs, rs)
            hv = h_ref[rows, :]
            g = _dot(hv, w_ref[0, 0], 1, 0)
            u = _dot(hv, w_ref[1, 0], 1, 0)
            gu_ref[0, 0, rows, :] = g.astype(BF16)
            gu_ref[1, 0, rows, :] = u.astype(BF16)
            act_ref[0, rows, :] = (g * _sigmoid(g) * u).astype(BF16)

    return pl.pallas_call(
        body, grid=(2, T // tm),
        in_specs=[pl.BlockSpec((tm, D), lambda j, i: (i, 0)),
                  pl.BlockSpec((2, 1, D, FF_BLK), lambda j, i: (0, j, 0, 0))],
        out_specs=[pl.BlockSpec((2, 1, tm, FF_BLK), lambda j, i: (0, j, i, 0)),
                   pl.BlockSpec((1, tm, FF_BLK), lambda j, i: (j, i, 0))],
        out_shape=[jax.ShapeDtypeStruct((2, 2, T, FF_BLK), BF16), jax.ShapeDtypeStruct((2, T, FF_BLK), BF16)],
        name=name, compiler_params=_params(2))(h, wgu)


def _ffn_down(name, act, wd, x):
    T, D = x.shape
    tm = min(T, 512)

    def body(act_ref, wd_ref, x_ref, o_ref):
        acc = _dot(act_ref[0], wd_ref[0], 1, 0) + _dot(act_ref[1], wd_ref[1], 1, 0)
        o_ref[...] = x_ref[...] + 0.5 * acc

    return pl.pallas_call(
        body, grid=(T // tm,),
        in_specs=[pl.BlockSpec((2, tm, FF_BLK), lambda i: (0, i, 0)),
                  pl.BlockSpec((2, FF_BLK, D), lambda i: (0, 0, 0)),
                  pl.BlockSpec((tm, D), lambda i: (i, 0))],
        out_specs=pl.BlockSpec((tm, D), lambda i: (i, 0)),
        out_shape=jax.ShapeDtypeStruct((T, D), F32), name=name, compiler_params=_params(1))(act, wd, x)


def _ffn_dact(name, dyb, wd, gu):
    T, D = dyb.shape
    tm = min(T, 1024)
    rs = min(tm, 256)

    def body(dy_ref, wd_ref, gu_ref, o_ref):
        for r in range(tm // rs):
            rows = pl.ds(r * rs, rs)
            dact = 0.5 * _dot(dy_ref[rows, :], wd_ref[0], 1, 1)
            g = gu_ref[0, 0, rows, :].astype(F32)
            u = gu_ref[1, 0, rows, :].astype(F32)
            s = _sigmoid(g)
            o_ref[0, 0, rows, :] = (dact * u * (s + g * s * (1.0 - s))).astype(BF16)
            o_ref[1, 0, rows, :] = (dact * g * s).astype(BF16)

    return pl.pallas_call(
        body, grid=(2, T // tm),
        in_specs=[pl.BlockSpec((tm, D), lambda j, i: (i, 0)),
                  pl.BlockSpec((1, FF_BLK, D), lambda j, i: (j, 0, 0)),
                  pl.BlockSpec((2, 1, tm, FF_BLK), lambda j, i: (0, j, i, 0))],
        out_specs=pl.BlockSpec((2, 1, tm, FF_BLK), lambda j, i: (0, j, i, 0)),
        out_shape=jax.ShapeDtypeStruct((2, 2, T, FF_BLK), BF16), name=name, compiler_params=_params(2))(dyb, wd, gu)


def _ffn_dwd(name, act, dyb):
    _, T, _ = act.shape
    D = dyb.shape[1]
    tk = min(T, 1024)

    def body(a_ref, d_ref, o_ref):
        p = 0.5 * _dot(a_ref[0], d_ref[...], 0, 0)

        @pl.when(pl.program_id(1) == 0)
        def _():
            o_ref[0] = p

        @pl.when(pl.program_id(1) > 0)
        def _():
            o_ref[0] += p

    return pl.pallas_call(
        body, grid=(2, T // tk),
        in_specs=[pl.BlockSpec((1, tk, FF_BLK), lambda j, k: (j, k, 0)), pl.BlockSpec((tk, D), lambda j, k: (k, 0))],
        out_specs=pl.BlockSpec((1, FF_BLK, D), lambda j, k: (j, 0, 0)),
        out_shape=jax.ShapeDtypeStruct((2, FF_BLK, D), F32), name=name, compiler_params=_params(2))(act, dyb)


def _ffn_dwgu(name, h, dgu):
    T, D = h.shape
    tk = min(T, 1024)

    def body(h_ref, d_ref, o_ref):
        p = _dot(h_ref[...], d_ref[0, 0], 0, 0)

        @pl.when(pl.program_id(1) == 0)
        def _():
            o_ref[0, 0] = p

        @pl.when(pl.program_id(1) > 0)
        def _():
            o_ref[0, 0] += p

    return pl.pallas_call(
        body, grid=(4, T // tk),
        in_specs=[pl.BlockSpec((tk, D), lambda q, k: (k, 0)),
                  pl.BlockSpec((1, 1, tk, FF_BLK), lambda q, k: (q // 2, q % 2, k, 0))],
        out_specs=pl.BlockSpec((1, 1, D, FF_BLK), lambda q, k: (q // 2, q % 2, 0, 0)),
        out_shape=jax.ShapeDtypeStruct((2, 2, D, FF_BLK), F32), name=name, compiler_params=_params(2))(h, dgu)


def _ffn_dx(name, dgu, wgu, x, dy, nw):
    T, D = x.shape
    tm = min(T, 512)

    def body(d_ref, w_ref, x_ref, dy_ref, nw_ref, dx_ref, dxb_ref, dnw_ref, acc_ref):
        i, k = pl.program_id(0), pl.program_id(1)
        p = _dot(d_ref[0, 0], w_ref[0, 0], 1, 1)

        @pl.when(k == 0)
        def _():
            acc_ref[...] = p

        @pl.when(k > 0)
        def _():
            acc_ref[...] += p

        @pl.when(k == 3)
        def _():
            dx, dw = _rms_bwd_tile(acc_ref[...], x_ref[...], dy_ref[...], nw_ref[...])
            dx_ref[...] = dx
            dxb_ref[...] = dx.astype(BF16)

            @pl.when(i == 0)
            def _():
                dnw_ref[...] = dw

            @pl.when(i > 0)
            def _():
                dnw_ref[...] += dw

    return pl.pallas_call(
        body, grid=(T // tm, 4),
        in_specs=[pl.BlockSpec((1, 1, tm, FF_BLK), lambda i, k: (k // 2, k % 2, i, 0)),
                  pl.BlockSpec((1, 1, D, FF_BLK), lambda i, k: (k // 2, k % 2, 0, 0)),
                  pl.BlockSpec((tm, D), lambda i, k: (i, 0)), pl.BlockSpec((tm, D), lambda i, k: (i, 0)),
                  pl.BlockSpec((1, D), lambda i, k: (0, 0))],
        out_specs=[pl.BlockSpec((tm, D), lambda i, k: (i, 0)), pl.BlockSpec((tm, D), lambda i, k: (i, 0)),
                   pl.BlockSpec((1, D), lambda i, k: (0, 0))],
        out_shape=[jax.ShapeDtypeStruct((T, D), F32), jax.ShapeDtypeStruct((T, D), BF16),
                   jax.ShapeDtypeStruct((1, D), F32)],
        scratch_shapes=[pltpu.VMEM((tm, D), F32)],
        name=name, compiler_params=_params(2))(dgu, wgu, x, dy, nw)


def _ffn_fwd(tag, x, nw, wgu, wd):
    h = _rms_fwd(tag + "_rms", x, nw)
    gu, act = _ffn_gu(tag + "_gu", h, wgu)
    return _ffn_down(tag + "_down", act, wd, x), (h, gu, act)


def _ffn_bwd(tag, dy, dyb, x, nw, wgu, wd, saved):
    h, gu, act = saved
    dgu = _ffn_dact(tag + "_dact", dyb, wd, gu)
    d_wd = _ffn_dwd(tag + "_dwd", act, dyb)
    d_wgu = _ffn_dwgu(tag + "_dwgu", h, dgu)
    dx, dxb, d_nw = _ffn_dx(tag + "_dx", dgu, wgu, x, dy, nw)
    return dx, dxb, d_nw, d_wgu, d_wd


def _conv_taps(cur, halo, w, first, sign):
    tt = cur.shape[0]
    halo = jnp.where(first, 0.0, halo)
    rid = lax.broadcasted_iota(jnp.int32, (8, cur.shape[1]), 0)
    acc = w[3:4, :] * cur
    for s in (1, 2, 3):
        if sign < 0:
            sh = pltpu.roll(cur, s, 0)
            edge = jnp.where(rid < s, pltpu.roll(halo, s, 0), sh[0:8])
            sh = jnp.concatenate([edge, sh[8:]], axis=0) if tt > 8 else edge
        else:
            sh = pltpu.roll(cur, tt - s, 0)
            edge = jnp.where(rid >= 8 - s, pltpu.roll(halo, 8 - s, 0), sh[tt - 8:])
            sh = jnp.concatenate([sh[:tt - 8], edge], axis=0) if tt > 8 else edge
        acc = acc + w[3 - s:4 - s, :] * sh
    return acc


def _gdn_prep(name, proj, wconv, gate_p):
    T = proj.shape[0]
    tt = min(T, 256)
    hb = tt // 8
    nch = tt // A_CHUNK

    def body(cur_ref, halo_ref, ba_ref, w_ref, gp_ref, qkv_ref, bg_ref, gc_ref):
        first = pl.program_id(0) == 0
        for c in range(24):
            cols = pl.ds(c * 128, 128)
            conv = _conv_taps(cur_ref[:, cols], halo_ref[:, cols], w_ref[:, cols], first, -1)
            y = conv * _sigmoid(conv)
            if c < 16:
                y = y * lax.rsqrt(jnp.sum(y * y, axis=-1, keepdims=True) + EPS)
                if c < 8:
                    y = y * (A_DK ** -0.5)
            qkv_ref[:, cols] = y
        ba = ba_ref[...]
        lane = lax.broadcasted_iota(jnp.int32, ba.shape, 1)
        zarg = ba + gp_ref[1:2, :]
        softplus = jnp.maximum(zarg, 0.0) + jnp.log(1.0 + jnp.exp(-jnp.abs(zarg)))
        bg = jnp.where(lane < 8, _sigmoid(ba), jnp.where(lane < 16, gp_ref[0:1, :] * softplus, 0.0))
        bg_ref[...] = bg
        tri = (lax.broadcasted_iota(jnp.int32, (A_CHUNK, A_CHUNK), 0)
               >= lax.broadcasted_iota(jnp.int32, (A_CHUNK, A_CHUNK), 1)).astype(F32)
        for c in range(nch):
            rows = pl.ds(c * A_CHUNK, A_CHUNK)
            gc_ref[rows, :] = _dotx(tri, bg[c * A_CHUNK:(c + 1) * A_CHUNK, :])

    return pl.pallas_call(
        body, grid=(T // tt,),
        in_specs=[pl.BlockSpec((tt, 3072), lambda i: (i, 0)),
                  pl.BlockSpec((8, 3072), lambda i: (jnp.maximum(i * hb - 1, 0), 0)),
                  pl.BlockSpec((tt, 128), lambda i: (i, 32)),
                  pl.BlockSpec((4, 3072), lambda i: (0, 0)),
                  pl.BlockSpec((2, 128), lambda i: (0, 0))],
        out_specs=[pl.BlockSpec((tt, 3072), lambda i: (i, 0)), pl.BlockSpec((tt, 128), lambda i: (i, 0)),
                   pl.BlockSpec((tt, 128), lambda i: (i, 0))],
        out_shape=[jax.ShapeDtypeStruct((T, 3072), F32), jax.ShapeDtypeStruct((T, 128), F32),
                   jax.ShapeDtypeStruct((T, 128), F32)],
        name=name, compiler_params=_params(1))(proj, proj, proj, wconv, gate_p)


def _chunk_masks():
    ri = lax.broadcasted_iota(jnp.int32, (A_CHUNK, A_CHUNK), 0)
    ci = lax.broadcasted_iota(jnp.int32, (A_CHUNK, A_CHUNK), 1)
    return ri >= ci, ri > ci, ri == ci


def _chunk_local(q, k, gcol, grow, bcol):
    incl, strict, _ = _chunk_masks()
    dec = jnp.where(incl, jnp.exp(jnp.where(incl, gcol - grow, 0.0)), 0.0)
    e = jnp.exp(gcol)
    glast = grow[:, A_CHUNK - 1:A_CHUNK]
    f = jnp.exp(glast - gcol)
    gl = jnp.exp(glast)
    kb = k * bcol
    lmat = jnp.where(strict, _dotb(kb, k, 1, 1) * dec, 0.0)
    amat = jnp.where(incl, _dotb(q, k, 1, 1) * dec, 0.0)
    return dec, e, f, gl, kb, lmat, amat


def _unit_lower_inverse(lmat):
    _, _, eye = _chunk_masks()
    t = jnp.where(eye, 1.0, 0.0) - lmat
    lp = _dotx(lmat, lmat)
    for it in range(5):
        t = t + _dotx(t, lp)
        if it < 4:
            lp = _dotx(lp, lp)
    return t


def _gdn_delta_fwd(name, qkv, gcol, grow, bcol):
    T = qkv.shape[0]
    tt = min(T, 512)
    nch = tt // A_CHUNK
    NC = T // A_CHUNK

    def body(q_ref, k_ref, v_ref, gc_ref, gr_ref, bc_ref, o_ref, s_ref, t_ref, u_ref, w_ref, state):
        @pl.when(pl.program_id(1) == 0)
        def _():
            state[...] = jnp.zeros_like(state)

        def chunk(c, carry):
            rows = pl.ds(pl.multiple_of(c * A_CHUNK, A_CHUNK), A_CHUNK)
            q, k, v = q_ref[rows, :], k_ref[rows, :], v_ref[rows, :]
            gcl, bcl, grw = gc_ref[0, rows, :], bc_ref[0, rows, :], gr_ref[0, c]
            dec, e, f, gl, kb, lmat, amat = _chunk_local(q, k, gcl, grw, bcl)
            tinv = _unit_lower_inverse(lmat)
            u = _dotx(tinv, v * bcl)
            w = _dotx(tinv, kb * e)
            s = state[...]
            s_ref[0, c] = s.astype(BF16)
            t_ref[0, c] = tinv
            u_ref[rows, :] = u
            w_ref[rows, :] = w
            vn = u - _dotb(w, s)
            o_ref[rows, :] = _dotb(q * e, s) + _dotb(amat, vn)
            state[...] = s * gl + _dotb(k * f, vn, 0, 0)
            return carry

        lax.fori_loop(0, nch, chunk, 0)

    hd = lambda col: pl.BlockSpec((tt, 128), lambda h, i: (i, col * 8 + h))
    col_spec = pl.BlockSpec((1, tt, 1), lambda h, i: (h, i, 0))
    return pl.pallas_call(
        body, grid=(A_HEADS, T // tt),
        in_specs=[hd(0), hd(1), hd(2), col_spec, pl.BlockSpec((1, nch, 1, A_CHUNK), lambda h, i: (h, i, 0, 0)), col_spec],
        out_specs=[pl.BlockSpec((tt, 128), lambda h, i: (i, h)),
                   pl.BlockSpec((1, nch, 128, 128), lambda h, i: (h, i, 0, 0)),
                   pl.BlockSpec((1, nch, A_CHUNK, A_CHUNK), lambda h, i: (h, i, 0, 0)),
                   pl.BlockSpec((tt, 128), lambda h, i: (i, h)), pl.BlockSpec((tt, 128), lambda h, i: (i, h))],
        out_shape=[jax.ShapeDtypeStruct((T, 1024), F32), jax.ShapeDtypeStruct((A_HEADS, NC, 128, 128), BF16),
                   jax.ShapeDtypeStruct((A_HEADS, NC, A_CHUNK, A_CHUNK), F32),
                   jax.ShapeDtypeStruct((T, 1024), F32), jax.ShapeDtypeStruct((T, 1024), F32)],
        scratch_shapes=[pltpu.VMEM((128, 128), F32)],
        name=name, compiler_params=_params(2))(qkv, qkv, qkv, gcol, grow, bcol)


def _gdn_delta_bwd(name, qkv, gcol, grow, bcol, d_o, s_sv, t_sv, u_sv, w_sv):
    T = qkv.shape[0]
    tt = min(T, 512)
    nch = tt // A_CHUNK
    ni = T // tt

    def body(q_ref, k_ref, v_ref, gc_ref, gr_ref, bc_ref, do_ref, s_ref, t_ref, u_ref, w_ref,
             dq_ref, dk_ref, dv_ref, dg_ref, db_ref, dstate):
        @pl.when(pl.program_id(1) == 0)
        def _():
            dstate[...] = jnp.zeros_like(dstate)

        incl, strict, _ = _chunk_masks()
        upper = (lax.broadcasted_iota(jnp.int32, (A_CHUNK, A_CHUNK), 0)
                 <= lax.broadcasted_iota(jnp.int32, (A_CHUNK, A_CHUNK), 1)).astype(F32)
        last_row = lax.broadcasted_iota(jnp.int32, (A_CHUNK, 1), 0) == A_CHUNK - 1
        ones = jnp.ones((A_CHUNK, 128), F32)

        def chunk(cc, carry):
            c = nch - 1 - cc
            rows = pl.ds(pl.multiple_of(c * A_CHUNK, A_CHUNK), A_CHUNK)
            q, k, v = q_ref[rows, :], k_ref[rows, :], v_ref[rows, :]
            gcl, bcl, grw = gc_ref[0, rows, :], bc_ref[0, rows, :], gr_ref[0, c]
            do, u, w = do_ref[rows, :], u_ref[rows, :], w_ref[rows, :]
            s, tinv = s_ref[0, c], t_ref[0, c]
            dec, e, f, gl, kb, lmat, amat = _chunk_local(q, k, gcl, grw, bcl)
            qd, kd, ke = q * e, k * f, kb * e
            ds = dstate[...]
            vn = u - _dotb(w, s)
            d_qd = _dotb(do, s, 1, 1)
            d_a = jnp.where(incl, _dotb(do, vn, 1, 1), 0.0)
            d_vn = _dotb(amat, do, 0, 0) + _dotb(kd, ds)
            d_kd = _dotb(vn, ds, 1, 1)
            d_gl = jnp.sum(jnp.sum(ds * s.astype(F32), axis=1, keepdims=True), axis=0, keepdims=True)
            d_w = -_dotb(d_vn, s, 1, 1)
            dstate[...] = gl * ds + _dotb(qd, do, 0, 0) - _dotb(w, d_vn, 0, 0)
            d_bv = _dotx(tinv, d_vn, 0, 0)
            d_ke = _dotx(tinv, d_w, 0, 0)
            d_l = -jnp.where(strict, _dotb(d_bv, u, 1, 1) + _dotb(d_ke, w, 1, 1), 0.0)
            d_kk = d_l * dec
            d_qk = d_a * dec
            d_kb = _dotb(d_kk, k)
            dk_ref[rows, :] = (_dotb(d_kk, kb, 0, 0) + _dotb(d_qk, q, 0, 0) + d_kd * f + d_ke * (bcl * e) + d_kb * bcl)
            dq_ref[rows, :] = _dotb(d_qk, k) + d_qd * e
            dv_ref[rows, :] = d_bv * bcl
            rsum = lambda x: jnp.sum(x, axis=1, keepdims=True)
            db_ref[0, rows, :] = rsum(d_ke * k) * e + rsum(d_kb * k) + rsum(d_bv * v)
            m = d_l * lmat + d_a * amat
            r_kd = rsum(d_kd * kd)
            col_m = _dotx(m, ones, 0, 0)[:, 0:1]
            d_gc = rsum(m) - col_m + rsum(d_qd * qd) - r_kd + rsum(d_ke * ke)
            tail = jnp.sum(r_kd, axis=0, keepdims=True) + d_gl * gl
            d_gc = d_gc + jnp.where(last_row, tail, 0.0)
            dg_ref[0, rows, :] = _dotx(upper, d_gc * ones)[:, 0:1]
            return carry

        lax.fori_loop(0, nch, chunk, 0)

    rev = lambda i: ni - 1 - i
    hd = lambda col: pl.BlockSpec((tt, 128), lambda h, i: (rev(i), col * 8 + h))
    hd1 = pl.BlockSpec((tt, 128), lambda h, i: (rev(i), h))
    col_spec = pl.BlockSpec((1, tt, 1), lambda h, i: (h, rev(i), 0))
    return pl.pallas_call(
        body, grid=(A_HEADS, ni),
        in_specs=[hd(0), hd(1), hd(2), col_spec,
                  pl.BlockSpec((1, nch, 1, A_CHUNK), lambda h, i: (h, rev(i), 0, 0)), col_spec, hd1,
                  pl.BlockSpec((1, nch, 128, 128), lambda h, i: (h, rev(i), 0, 0)),
                  pl.BlockSpec((1, nch, A_CHUNK, A_CHUNK), lambda h, i: (h, rev(i), 0, 0)), hd1, hd1],
        out_specs=[hd1, hd1, hd1, col_spec, col_spec],
        out_shape=[jax.ShapeDtypeStruct((T, 1024), F32)] * 3 + [jax.ShapeDtypeStruct((A_HEADS, T, 1), F32)] * 2,
        scratch_shapes=[pltpu.VMEM((128, 128), F32)],
        name=name, compiler_params=_params(2))(qkv, qkv, qkv, gcol, grow, bcol, d_o, s_sv, t_sv, u_sv, w_sv)


def _gdn_gate_fwd(name, o, proj, nw):
    T = o.shape[0]
    tt = min(T, 512)

    def body(o_ref, z_ref, nw_ref, y_ref):
        for h in range(A_HEADS):
            cols = pl.ds(h * 128, 128)
            ov, z = o_ref[:, cols], z_ref[:, cols]
            r = lax.rsqrt(jnp.mean(ov * ov, axis=-1, keepdims=True) + EPS)
            y_ref[:, cols] = (ov * r * nw_ref[...] * (z * _sigmoid(z))).astype(BF16)

    return pl.pallas_call(
        body, grid=(T // tt,),
        in_specs=[pl.BlockSpec((tt, 1024), lambda i: (i, 0)), pl.BlockSpec((tt, 1024), lambda i: (i, 3)),
                  pl.BlockSpec((1, 128), lambda i: (0, 0))],
        out_specs=pl.BlockSpec((tt, 1024), lambda i: (i, 0)),
        out_shape=jax.ShapeDtypeStruct((T, 1024), BF16), name=name, compiler_params=_params(1))(o, proj, nw)


def _gdn_gate_bwd(name, dy2, o, proj, nw):
    T = o.shape[0]
    tt = min(T, 512)

    def body(dy_ref, o_ref, z_ref, nw_ref, do_ref, dz_ref, dnw_ref):
        dnw = jnp.zeros((1, 128), F32)
        for h in range(A_HEADS):
            cols = pl.ds(h * 128, 128)
            dy, ov, z = dy_ref[:, cols], o_ref[:, cols], z_ref[:, cols]
            s = _sigmoid(z)
            sz = z * s
            r = lax.rsqrt(jnp.mean(ov * ov, axis=-1, keepdims=True) + EPS)
            xhat = ov * r
            dn = dy * sz
            dz_ref[:, cols] = dy * (xhat * nw_ref[...]) * (s + z * s * (1.0 - s))
            dxhat = dn * nw_ref[...]
            do_ref[:, cols] = r * (dxhat - xhat * jnp.mean(dxhat * xhat, axis=-1, keepdims=True))
            dnw = dnw + jnp.sum(dn * xhat, axis=0, keepdims=True)

        @pl.when(pl.program_id(0) == 0)
        def _():
            dnw_ref[...] = dnw

        @pl.when(pl.program_id(0) > 0)
        def _():
            dnw_ref[...] += dnw

    blk = lambda c: pl.BlockSpec((tt, 1024), lambda i: (i, c))
    return pl.pallas_call(
        body, grid=(T // tt,),
        in_specs=[blk(0), blk(0), blk(3), pl.BlockSpec((1, 128), lambda i: (0, 0))],
        out_specs=[blk(0), blk(0), pl.BlockSpec((1, 128), lambda i: (0, 0))],
        out_shape=[jax.ShapeDtypeStruct((T, 1024), F32), jax.ShapeDtypeStruct((T, 1024), F32),
                   jax.ShapeDtypeStruct((1, 128), F32)],
        name=name, compiler_params=_params(1))(dy2, o, proj, nw)


def _gdn_prep_bwd1(name, proj, wconv, gate_p, dq, dk, dv, dbg):
    T = proj.shape[0]
    tt = min(T, 256)
    hb = tt // 8

    def body(cur_ref, halo_ref, ba_ref, w_ref, gp_ref, dq_ref, dk_ref, dv_ref, dbg_ref,
             dc_ref, dw_ref, dba_ref, dgp_ref):
        first = pl.program_id(0) == 0
        rid = lax.broadcasted_iota(jnp.int32, (8, 128), 0)
        for c in range(24):
            cols = pl.ds(c * 128, 128)
            cur = cur_ref[:, cols]
            halo = jnp.where(first, 0.0, halo_ref[:, cols])
            conv = _conv_taps(cur, halo_ref[:, cols], w_ref[:, cols], first, -1)
            s = _sigmoid(conv)
            y = conv * s
            if c < 16:
                dref = dq_ref if c < 8 else dk_ref
                dn = dref[:, pl.ds((c % 8) * 128, 128)]
                rinv = lax.rsqrt(jnp.sum(y * y, axis=-1, keepdims=True) + EPS)
                yhat = y * rinv
                dyv = rinv * (dn - yhat * jnp.sum(dn * yhat, axis=-1, keepdims=True))
                if c < 8:
                    dyv = dyv * (A_DK ** -0.5)
            else:
                dyv = dv_ref[:, pl.ds((c - 16) * 128, 128)]
            dc = dyv * (s + conv * s * (1.0 - s))
            dc_ref[:, cols] = dc
            parts = [jnp.sum(dc * cur, axis=0, keepdims=True)]
            for sft in (1, 2, 3):
                sh = pltpu.roll(cur, sft, 0)
                edge = jnp.where(rid < sft, pltpu.roll(halo, sft, 0), sh[0:8])
                sh = jnp.concatenate([edge, sh[8:]], axis=0) if tt > 8 else edge
                parts.append(jnp.sum(dc * sh, axis=0, keepdims=True))
            dwc = jnp.concatenate(parts[::-1], axis=0)

            @pl.when(first)
            def _():
                dw_ref[:, cols] = dwc

            @pl.when(jnp.logical_not(first))
            def _():
                dw_ref[:, cols] += dwc

        ba = ba_ref[...]
        dbg = dbg_ref[...]
        lane = lax.broadcasted_iota(jnp.int32, ba.shape, 1)
        sb = _sigmoid(ba)
        zarg = ba + gp_ref[1:2, :]
        softplus = jnp.maximum(zarg, 0.0) + jnp.log(1.0 + jnp.exp(-jnp.abs(zarg)))
        d_b = dbg * sb * (1.0 - sb)
        d_a = dbg * gp_ref[0:1, :] * _sigmoid(zarg)
        dba_ref[...] = jnp.where(lane < 8, d_b, jnp.where(lane < 16, d_a, 0.0))
        g = gp_ref[0:1, :] * softplus
        in_a = (lane >= 8) & (lane < 16)
        sums = jnp.concatenate([jnp.sum(jnp.where(in_a, dbg * g, 0.0), axis=0, keepdims=True),
                                jnp.sum(jnp.where(in_a, d_a, 0.0), axis=0, keepdims=True)], axis=0)

        @pl.when(first)
        def _():
            dgp_ref[...] = sums

        @pl.when(jnp.logical_not(first))
        def _():
            dgp_ref[...] += sums

    row = lambda w, c=0: pl.BlockSpec((tt, w), lambda i: (i, c))
    return pl.pallas_call(
        body, grid=(T // tt,),
        in_specs=[row(3072), pl.BlockSpec((8, 3072), lambda i: (jnp.maximum(i * hb - 1, 0), 0)), row(128, 32),
                  pl.BlockSpec((4, 3072), lambda i: (0, 0)), pl.BlockSpec((2, 128), lambda i: (0, 0)),
                  row(1024), row(1024), row(1024), row(128)],
        out_specs=[row(3072), pl.BlockSpec((4, 3072), lambda i: (0, 0)), row(128),
                   pl.BlockSpec((2, 128), lambda i: (0, 0))],
        out_shape=[jax.ShapeDtypeStruct((T, 3072), F32), jax.ShapeDtypeStruct((4, 3072), F32),
                   jax.ShapeDtypeStruct((T, 128), F32), jax.ShapeDtypeStruct((2, 128), F32)],
        name=name, compiler_params=_params(1))(proj, proj, proj, wconv, gate_p, dq, dk, dv, dbg)


def _gdn_prep_bwd2(name, dc, wconv, dz, dba):
    T = dc.shape[0]
    tt = min(T, 256)
    hb = tt // 8
    ni = T // tt

    def body(cur_ref, halo_ref, w_ref, dz_ref, dba_ref, o_ref):
        last = pl.program_id(0) == ni - 1
        for c in range(24):
            cols = pl.ds(c * 128, 128)
            o_ref[:, cols] = _conv_taps(cur_ref[:, cols], halo_ref[:, cols], w_ref[:, cols], last, +1)
        o_ref[:, pl.ds(3072, 1024)] = dz_ref[...]
        o_ref[:, pl.ds(4096, 128)] = dba_ref[...]

    return pl.pallas_call(
        body, grid=(ni,),
        in_specs=[pl.BlockSpec((tt, 3072), lambda i: (i, 0)),
                  pl.BlockSpec((8, 3072), lambda i: (jnp.minimum((i + 1) * hb, T // 8 - 1), 0)),
                  pl.BlockSpec((4, 3072), lambda i: (0, 0)),
                  pl.BlockSpec((tt, 1024), lambda i: (i, 0)), pl.BlockSpec((tt, 128), lambda i: (i, 0))],
        out_specs=pl.BlockSpec((tt, A_IN_PAD), lambda i: (i, 0)),
        out_shape=jax.ShapeDtypeStruct((T, A_IN_PAD), F32), name=name, compiler_params=_params(1))(
            dc, dc, wconv, dz, dba)


def _gdn_fwd(x, nw, w_in, wconv, gate_p, out_nw, w_out):
    T = x.shape[0]
    h = _rms_fwd("a_rms", x, nw)
    proj = _mm_plain("a_proj", h, w_in, 1, 0, F32, tn=FF_BLK)
    qkv, bg, gcum = _gdn_prep("a_prep", proj, wconv, gate_p)
    bcol = bg[:, 0:8].T.reshape(A_HEADS, T, 1)
    gcol = gcum[:, 8:16].T.reshape(A_HEADS, T, 1)
    grow = gcol.reshape(A_HEADS, T // A_CHUNK, 1, A_CHUNK)
    o, s_sv, t_sv, u_sv, w_sv = _gdn_delta_fwd("a_delta", qkv, gcol, grow, bcol)
    o2 = _gdn_gate_fwd("a_gate", o, proj, out_nw)
    y = _mm_residual("a_out", o2, w_out, x, 1.0)
    return y, (h, proj, qkv, gcol, grow, bcol, o, s_sv, t_sv, u_sv, w_sv, o2)


def _gdn_bwd(dy, dyb, x, nw, w_in, wconv, gate_p, out_nw, w_out, saved):
    h, proj, qkv, gcol, grow, bcol, o, s_sv, t_sv, u_sv, w_sv, o2 = saved
    T = x.shape[0]
    d_o2 = _mm_plain("a_dout", dyb, w_out, 1, 1, F32)
    d_wout = _mm_plain("a_dwout", o2, dyb, 0, 0, F32)
    d_o, d_z, d_outnw = _gdn_gate_bwd("a_dgate", d_o2, o, proj, out_nw)
    dq, dk, dv, dg, dbeta = _gdn_delta_bwd("a_ddelta", qkv, gcol, grow, bcol, d_o, s_sv, t_sv, u_sv, w_sv)
    dbg = jnp.concatenate([dbeta.reshape(A_HEADS, T).T, dg.reshape(A_HEADS, T).T, jnp.zeros((T, 112), F32)], axis=1)
    dc, d_wconv, dba, dgp = _gdn_prep_bwd1("a_dprep1", proj, wconv, gate_p, dq, dk, dv, dbg)
    dproj = _gdn_prep_bwd2("a_dprep2", dc, wconv, d_z, dba)
    d_win = _mm_plain("a_dwin", h, dproj, 0, 0, F32, tn=FF_BLK)
    dx, dxb, d_nw = _mm_rms_bwd("a_dx", dproj, w_in, x, dy, nw, tk=FF_BLK)
    return dx, dxb, d_nw, d_win, d_wconv, dgp, d_outnw, d_wout


def _swa_masks(n):
    qi = lax.broadcasted_iota(jnp.int32, (B_BLK, B_BLK), 0)
    kj = lax.broadcasted_iota(jnp.int32, (B_BLK, B_BLK), 1)
    return kj > qi + jnp.where(n > 0, 0, B_BLK), kj <= qi


def _swa_fwd(name, q, k, v, sinks):
    T = q.shape[1]
    nb = T // B_BLK
    scale = B_HD ** -0.5

    def body(q_ref, k_ref, v_ref, s_ref, o_ref, l_ref):
        sink = s_ref[0][:, 0:1]

        def block(n, carry):
            m_prev, m_cur = _swa_masks(n)
            cur = pl.ds(pl.multiple_of(n * B_BLK, B_BLK), B_BLK)
            prv = pl.ds(pl.multiple_of(jnp.maximum(n - 1, 0) * B_BLK, B_BLK), B_BLK)
            qb = q_ref[0, cur, :]
            s_p = jnp.where(m_prev, _dot(qb, k_ref[0, prv, :], 1, 1) * scale, -jnp.inf)
            s_c = jnp.where(m_cur, _dot(qb, k_ref[0, cur, :], 1, 1) * scale, -jnp.inf)
            m = jnp.maximum(jnp.maximum(jnp.max(s_p, axis=1, keepdims=True), jnp.max(s_c, axis=1, keepdims=True)), sink)
            p_p, p_c = jnp.exp(s_p - m), jnp.exp(s_c - m)
            den = jnp.sum(p_p, axis=1, keepdims=True) + jnp.sum(p_c, axis=1, keepdims=True) + jnp.exp(sink - m)
            acc = _dotb(p_p, v_ref[0, prv, :]) + _dotb(p_c, v_ref[0, cur, :])
            o_ref[0, cur, :] = (acc / den).astype(BF16)
            l_ref[0, cur, :] = m + jnp.log(den)
            return carry

        lax.fori_loop(0, nb, block, 0)

    return pl.pallas_call(
        body, grid=(B_HEADS,),
        in_specs=[pl.BlockSpec((1, T, B_HD), lambda h: (h, 0, 0)), pl.BlockSpec((1, T, B_HD), lambda h: (h // 4, 0, 0)),
                  pl.BlockSpec((1, T, B_HD), lambda h: (h // 4, 0, 0)), pl.BlockSpec((1, 1, 128), lambda h: (h, 0, 0))],
        out_specs=[pl.BlockSpec((1, T, B_HD), lambda h: (h, 0, 0)), pl.BlockSpec((1, T, 1), lambda h: (h, 0, 0))],
        out_shape=[jax.ShapeDtypeStruct((B_HEADS, T, B_HD), BF16), jax.ShapeDtypeStruct((B_HEADS, T, 1), F32)],
        name=name, compiler_params=_params(1))(q, k, v, sinks)


def _swa_bwd(name, q, k, v, sinks, o, lse, do):
    T = q.shape[1]
    nb = T // B_BLK
    scale = B_HD ** -0.5
    G = B_HEADS // B_KV

    def body(q_ref, k_ref, v_ref, s_ref, o_ref, l_ref, do_ref, dq_ref, dk_ref, dv_ref, ds_ref):
        g = pl.program_id(1)
        sink = s_ref[0][:, 0:1]

        @pl.when(g == 0)
        def _():
            dk_ref[...] = jnp.zeros_like(dk_ref)
            dv_ref[...] = jnp.zeros_like(dv_ref)

        def block(n, dsink):
            m_prev, m_cur = _swa_masks(n)
            cur = pl.ds(pl.multiple_of(n * B_BLK, B_BLK), B_BLK)
            prv = pl.ds(pl.multiple_of(jnp.maximum(n - 1, 0) * B_BLK, B_BLK), B_BLK)
            qb, dob = q_ref[0, cur, :], do_ref[0, cur, :]
            kp, kc, vp, vc = k_ref[0, prv, :], k_ref[0, cur, :], v_ref[0, prv, :], v_ref[0, cur, :]
            lse_b = l_ref[0, cur, :]
            p_p = jnp.where(m_prev, jnp.exp(_dot(qb, kp, 1, 1) * scale - lse_b), 0.0)
            p_c = jnp.where(m_cur, jnp.exp(_dot(qb, kc, 1, 1) * scale - lse_b), 0.0)
            delta = jnp.sum(dob.astype(F32) * o_ref[0, cur, :].astype(F32), axis=1, keepdims=True)
            ds_p = p_p * (_dot(dob, vp, 1, 1) - delta)
            ds_c = p_c * (_dot(dob, vc, 1, 1) - delta)
            dq_ref[0, cur, :] = (_dotb(ds_p, kp) + _dotb(ds_c, kc)) * scale
            dk_ref[0, prv, :] += _dotb(ds_p, qb, 0, 0) * scale
            dk_ref[0, cur, :] += _dotb(ds_c, qb, 0, 0) * scale
            dv_ref[0, prv, :] += _dotb(p_p, dob, 0, 0)
            dv_ref[0, cur, :] += _dotb(p_c, dob, 0, 0)
            return dsink - jnp.sum(jnp.exp(sink - lse_b) * delta, axis=0, keepdims=True)

        dsink = lax.fori_loop(0, nb, block, jnp.zeros((1, 1), F32))
        ds_ref[0] = jnp.broadcast_to(dsink, (1, 128))

    qs = pl.BlockSpec((1, T, B_HD), lambda kv, g: (kv * G + g, 0, 0))
    ks = pl.BlockSpec((1, T, B_HD), lambda kv, g: (kv, 0, 0))
    return pl.pallas_call(
        body, grid=(B_KV, G),
        in_specs=[qs, ks, ks, pl.BlockSpec((1, 1, 128), lambda kv, g: (kv * G + g, 0, 0)), qs,
                  pl.BlockSpec((1, T, 1), lambda kv, g: (kv * G + g, 0, 0)), qs],
        out_specs=[qs, ks, ks, pl.BlockSpec((1, 1, 128), lambda kv, g: (kv * G + g, 0, 0))],
        out_shape=[jax.ShapeDtypeStruct((B_HEADS, T, B_HD), F32), jax.ShapeDtypeStruct((B_KV, T, B_HD), F32),
                   jax.ShapeDtypeStruct((B_KV, T, B_HD), F32), jax.ShapeDtypeStruct((B_HEADS, 1, 128), F32)],
        name=name, compiler_params=_params(2))(q, k, v, sinks, o, lse, do)


def _split_heads(a, n):
    T = a.shape[0]
    return a.reshape(T, n, B_HD).transpose(1, 0, 2)


def _merge_heads(a):
    n, T, _ = a.shape
    return a.transpose(1, 0, 2).reshape(T, n * B_HD)


def _swa_mixer_fwd(x, nw, w_in, b_in, sinks, w_out, b_out):
    h = _rms_fwd("b_rms", x, nw)
    proj = _mm_plain("b_proj", h, w_in, 1, 0, BF16, tn=768, bias=b_in)
    q, k, v = _split_heads(proj[:, :1024], B_HEADS), _split_heads(proj[:, 1024:1280], B_KV), _split_heads(proj[:, 1280:], B_KV)
    o, lse = _swa_fwd("b_attn", q, k, v, sinks)
    om = _merge_heads(o)
    y = _mm_residual("b_out", om, w_out, x, 1.0, bias=b_out)
    return y, (h, q, k, v, o, lse, om)


def _swa_mixer_bwd(dy, dyb, x, nw, w_in, sinks, w_out, saved):
    h, q, k, v, o, lse, om = saved
    d_om = _mm_plain("b_dout", dyb, w_out, 1, 1, BF16)
    d_wout = _mm_plain("b_dwout", om, dyb, 0, 0, F32)
    d_bout = _colsum("b_dbout", dy)
    dq, dk, dv, dsinks = _swa_bwd("b_dattn", q, k, v, sinks, o, lse, _split_heads(d_om, B_HEADS))
    dproj = jnp.concatenate([_merge_heads(dq), _merge_heads(dk), _merge_heads(dv)], axis=1)
    d_bin = _colsum("b_dbin", dproj)
    d_win = _mm_plain("b_dwin", h, dproj, 0, 0, F32, tn=768)
    dx, dxb, d_nw = _mm_rms_bwd("b_dx", dproj, w_in, x, dy, nw, tk=768)
    return dx, dxb, d_nw, d_win, d_bin, dsinks[:, 0, 0], d_wout, d_bout


def _loss_head(name, x, tgt, fw):
    T, D = x.shape
    tt = min(T, 512)

    def body(x_ref, t_ref, w_ref, dx_ref, dxb_ref, loss_ref, dw_ref):
        xv = x_ref[...]
        r = lax.rsqrt(jnp.mean(xv * xv, axis=-1, keepdims=True) + EPS)
        xhat = xv * r
        diff = xhat * w_ref[...] - t_ref[...]
        part = 0.5 * jnp.sum(jnp.mean(diff * diff, axis=-1, keepdims=True), axis=0, keepdims=True)
        dyv = diff * (1.0 / D)
        dxhat = dyv * w_ref[...]
        dx = r * (dxhat - xhat * jnp.mean(dxhat * xhat, axis=-1, keepdims=True))
        dx_ref[...] = dx
        dxb_ref[...] = dx.astype(BF16)
        dw = jnp.sum(dyv * xhat, axis=0, keepdims=True)
        lp = jnp.broadcast_to(part, (1, 128))

        @pl.when(pl.program_id(0) == 0)
        def _():
            loss_ref[...] = lp
            dw_ref[...] = dw

        @pl.when(pl.program_id(0) > 0)
        def _():
            loss_ref[...] += lp
            dw_ref[...] += dw

    row = pl.BlockSpec((tt, D), lambda i: (i, 0))
    return pl.pallas_call(
        body, grid=(T // tt,), in_specs=[row, row, pl.BlockSpec((1, D), lambda i: (0, 0))],
        out_specs=[row, row, pl.BlockSpec((1, 128), lambda i: (0, 0)), pl.BlockSpec((1, D), lambda i: (0, 0))],
        out_shape=[jax.ShapeDtypeStruct((T, D), F32), jax.ShapeDtypeStruct((T, D), BF16),
                   jax.ShapeDtypeStruct((1, 128), F32), jax.ShapeDtypeStruct((1, D), F32)],
        name=name, compiler_params=_params(1))(x, tgt, fw)


def _local_step(x, tgt, wts):
    W = wts
    g = {}
    x1, sv1 = _ffn_fwd("f10", x, W["ffn1_norm"][0:1], W["ffn1_w_gu"][0], W["ffn1_w_down"][0])
    x2, sva = _gdn_fwd(x1, W["mix_norm"][0:1], W["a_w_in"], W["a_w_conv"], W["a_gate_p"], W["a_out_norm"], W["a_w_out"])
    x3, sv3 = _ffn_fwd("f20", x2, W["ffn2_norm"][0:1], W["ffn2_w_gu"][0], W["ffn2_w_down"][0])
    x4, sv4 = _ffn_fwd("f11", x3, W["ffn1_norm"][1:2], W["ffn1_w_gu"][1], W["ffn1_w_down"][1])
    x5, svb = _swa_mixer_fwd(x4, W["mix_norm"][1:2], W["b_w_in"], W["b_b_in"], W["b_sinks"], W["b_w_out"], W["b_b_out"])
    x6, sv6 = _ffn_fwd("f21", x5, W["ffn2_norm"][1:2], W["ffn2_w_gu"][1], W["ffn2_w_down"][1])
    dx, dxb, loss_p, g["final_norm"] = _loss_head("loss_head", x6, tgt, W["final_norm"])

    dx, dxb, n21, gu21, wd21 = _ffn_bwd("f21", dx, dxb, x5, W["ffn2_norm"][1:2], W["ffn2_w_gu"][1], W["ffn2_w_down"][1], sv6)
    dx, dxb, nb, g["b_w_in"], g["b_b_in"], g["b_sinks"], g["b_w_out"], g["b_b_out"] = _swa_mixer_bwd(
        dx, dxb, x4, W["mix_norm"][1:2], W["b_w_in"], W["b_sinks"], W["b_w_out"], svb)
    dx, dxb, n11, gu11, wd11 = _ffn_bwd("f11", dx, dxb, x3, W["ffn1_norm"][1:2], W["ffn1_w_gu"][1], W["ffn1_w_down"][1], sv4)
    dx, dxb, n20, gu20, wd20 = _ffn_bwd("f20", dx, dxb, x2, W["ffn2_norm"][0:1], W["ffn2_w_gu"][0], W["ffn2_w_down"][0], sv3)
    dx, dxb, na, g["a_w_in"], g["a_w_conv"], g["a_gate_p"], g["a_out_norm"], g["a_w_out"] = _gdn_bwd(
        dx, dxb, x1, W["mix_norm"][0:1], W["a_w_in"], W["a_w_conv"], W["a_gate_p"], W["a_out_norm"], W["a_w_out"], sva)
    dx, dxb, n10, gu10, wd10 = _ffn_bwd("f10", dx, dxb, x, W["ffn1_norm"][0:1], W["ffn1_w_gu"][0], W["ffn1_w_down"][0], sv1)

    g["ffn1_norm"] = jnp.concatenate([n10, n11], axis=0)
    g["ffn2_norm"] = jnp.concatenate([n20, n21], axis=0)
    g["mix_norm"] = jnp.concatenate([na, nb], axis=0)
    g["ffn1_w_gu"] = jnp.stack([gu10, gu11])
    g["ffn2_w_gu"] = jnp.stack([gu20, gu21])
    g["ffn1_w_down"] = jnp.stack([wd10, wd11])
    g["ffn2_w_down"] = jnp.stack([wd20, wd21])
    return loss_p, dx, g


PACK = (("ffn1_w_gu", 2816), ("ffn2_w_gu", 2816), ("ffn1_w_down", 1408), ("ffn2_w_down", 1408),
        ("a_w_in", 1028), ("a_w_out", 256), ("b_w_in", 384), ("b_w_out", 256))
PACK_USED = sum(n for _, n in PACK)
PACK_ROWS = 10400
HALF_ROWS = PACK_ROWS // 2
SMALL_SHARD = (8, 512)
HBM = pl.BlockSpec(memory_space=pl.ANY)


def _mesh_pos():
    x, y, c = lax.axis_index("x"), lax.axis_index("y"), lax.axis_index("c")
    return x, y, c, [(1 - x, y), (x, 1 - y), (1 - x, 1 - y)]


def _half(c):
    return pl.ds(pl.multiple_of(c * HALF_ROWS, 16), HALF_ROWS)


def _gather_shards(big, small):
    def body(big_ref, small_ref, gb_ref, gs_ref, send_sems, recv_sems, local_sems):
        x, y, c, chips = _mesh_pos()
        me, sibling = 2 * x + y, (x, y, 1 - c)
        slot = lambda chip: 2 * chip[0] + chip[1]

        def big_copy(k, block, hc, to, src=None):
            dst = gb_ref.at[block, _half(hc)]
            return pltpu.make_async_remote_copy(src_ref=dst if src is None else src, dst_ref=dst,
                                                send_sem=send_sems.at[k], recv_sem=recv_sems.at[k],
                                                device_id=to, device_id_type=MESH)

        def small_copy(k, block, to):
            return pltpu.make_async_remote_copy(src_ref=small_ref, dst_ref=gs_ref.at[block],
                                                send_sem=send_sems.at[k], recv_sem=recv_sems.at[k],
                                                device_id=to, device_id_type=MESH)

        own_big = pltpu.make_async_copy(big_ref, gb_ref.at[me], local_sems.at[0])
        own_small = pltpu.make_async_copy(small_ref, gs_ref.at[me], local_sems.at[1])
        own_big.start()
        own_small.start()
        first = [big_copy(j, me, c, (*chip, c), src=big_ref.at[_half(c)]) for j, chip in enumerate(chips)]
        first += [small_copy(6 + j, me, (*chip, c)) for j, chip in enumerate(chips)]
        for cp in first:
            cp.start()
        passed = [big_copy(3 + j, slot(chip), c, sibling) for j, chip in enumerate(chips)]
        for j, chip in enumerate(chips):
            big_copy(j, slot(chip), c, (x, y, c)).wait_recv()
            passed[j].start()
        for j, chip in enumerate(chips):
            big_copy(3 + j, slot(chip), 1 - c, (x, y, c)).wait_recv()
            small_copy(6 + j, slot(chip), (x, y, c)).wait_recv()
        for cp in first + passed:
            cp.wait_send()
        own_big.wait()
        own_small.wait()

    return pl.pallas_call(
        body, name="gather_shards", in_specs=[HBM, HBM], out_specs=[HBM, HBM],
        out_shape=[jax.ShapeDtypeStruct((4, PACK_ROWS, 1024), BF16), jax.ShapeDtypeStruct((4,) + SMALL_SHARD, F32)],
        scratch_shapes=[pltpu.SemaphoreType.DMA((9,)), pltpu.SemaphoreType.DMA((9,)), pltpu.SemaphoreType.DMA((2,))],)(big, small)


def _pair_exchange(p):
    def body(p_ref, own_ref, sib_ref, send_sems, recv_sems, local_sems):
        x, y, c, _ = _mesh_pos()
        sibling = (x, y, 1 - c)
        keep = [pltpu.make_async_copy(p_ref.at[s, _half(c)], own_ref.at[s], local_sems.at[s]) for s in range(4)]
        send = [pltpu.make_async_remote_copy(src_ref=p_ref.at[s, _half(1 - c)], dst_ref=sib_ref.at[s],
                                             send_sem=send_sems.at[s], recv_sem=recv_sems.at[s],
                                             device_id=sibling, device_id_type=MESH) for s in range(4)]
        for cp in keep + send:
            cp.start()
        for cp in send:
            cp.wait_recv()
        for cp in send:
            cp.wait_send()
        for cp in keep:
            cp.wait()

    shape = jax.ShapeDtypeStruct((4, HALF_ROWS, 1024), BF16)
    return pl.pallas_call(
        body, name="pair_exchange", in_specs=[HBM], out_specs=[HBM, HBM], out_shape=[shape, shape],
        scratch_shapes=[pltpu.SemaphoreType.DMA((4,)), pltpu.SemaphoreType.DMA((4,)), pltpu.SemaphoreType.DMA((4,))],)(p)


def _chip_exchange(cs):
    def body(c_ref, b_ref, send_sems, recv_sems, local_sem):
        x, y, c, chips = _mesh_pos()
        me = 2 * x + y
        slot = lambda chip: 2 * chip[0] + chip[1]
        own = pltpu.make_async_copy(c_ref.at[me], b_ref.at[me], local_sem)
        own.start()

        def copy(j, src_block, dst_block, to):
            return pltpu.make_async_remote_copy(src_ref=c_ref.at[src_block], dst_ref=b_ref.at[dst_block],
                                                send_sem=send_sems.at[j], recv_sem=recv_sems.at[j],
                                                device_id=to, device_id_type=MESH)

        send = [copy(j, slot(chip), me, (*chip, c)) for j, chip in enumerate(chips)]
        for cp in send:
            cp.start()
        for j, chip in enumerate(chips):
            copy(j, me, slot(chip), (x, y, c)).wait_recv()
        for cp in send:
            cp.wait_send()
        own.wait()

    return pl.pallas_call(
        body, name="chip_exchange", in_specs=[HBM], out_specs=HBM,
        out_shape=jax.ShapeDtypeStruct((4, HALF_ROWS, 1024), BF16),
        scratch_shapes=[pltpu.SemaphoreType.DMA((3,)), pltpu.SemaphoreType.DMA((3,)), pltpu.SemaphoreType.DMA],)(cs)


def _pair_concat(rh):
    def body(h_ref, r_ref, send_sem, recv_sem, local_sem):
        x, y, c, _ = _mesh_pos()
        own = pltpu.make_async_copy(h_ref, r_ref.at[_half(c)], local_sem)
        own.start()
        send = pltpu.make_async_remote_copy(src_ref=h_ref, dst_ref=r_ref.at[_half(c)], send_sem=send_sem,
                                            recv_sem=recv_sem, device_id=(x, y, 1 - c), device_id_type=MESH)
        send.start()
        pltpu.make_async_remote_copy(src_ref=h_ref, dst_ref=r_ref.at[_half(1 - c)], send_sem=send_sem,
                                     recv_sem=recv_sem, device_id=(x, y, c), device_id_type=MESH).wait_recv()
        send.wait_send()
        own.wait()

    return pl.pallas_call(
        body, name="pair_concat", in_specs=[HBM], out_specs=HBM,
        out_shape=jax.ShapeDtypeStruct((PACK_ROWS, 1024), F32),
        scratch_shapes=[pltpu.SemaphoreType.DMA, pltpu.SemaphoreType.DMA, pltpu.SemaphoreType.DMA],)(rh)


def _sum_blocks(name, parts, out_dtype):
    n = parts[0].shape[0]
    tr = 400
    assert HALF_ROWS % tr == 0
    per_block = len(parts) > 1

    def body(*refs):
        ins, o_ref = refs[:-1], refs[-1]
        if per_block:
            acc = ins[0][0].astype(F32)
            for r in ins[1:]:
                acc = acc + r[0].astype(F32)
            o_ref[0] = acc.astype(out_dtype)
        else:
            acc = ins[0][0].astype(F32)
            for s in range(1, n):
                acc = acc + ins[0][s].astype(F32)
            o_ref[...] = acc.astype(out_dtype)

    if per_block:
        grid = (n, HALF_ROWS // tr)
        spec = pl.BlockSpec((1, tr, 1024), lambda s, i: (s, i, 0))
        in_specs, out_specs = [spec] * len(parts), spec
        out_shape = jax.ShapeDtypeStruct((n, HALF_ROWS, 1024), out_dtype)
    else:
        grid = (HALF_ROWS // tr,)
        in_specs = [pl.BlockSpec((n, tr, 1024), lambda i: (0, i, 0))]
        out_specs = pl.BlockSpec((tr, 1024), lambda i: (i, 0))
        out_shape = jax.ShapeDtypeStruct((HALF_ROWS, 1024), out_dtype)
    return pl.pallas_call(body, grid=grid, in_specs=in_specs, out_specs=out_specs, out_shape=out_shape,
                          name=name, compiler_params=_params(len(grid)))(*parts)


def _reduce_scatter(p):
    own, sib = _pair_exchange(p)
    chip_sum = _sum_blocks("pair_sum", [own, sib], BF16)
    from_chips = _chip_exchange(chip_sum)
    return _pair_concat(_sum_blocks("chip_sum", [from_chips], F32))


SMALL_ROWS = 24


def _all_reduce_small(v):
    def body(v_ref, o_ref, all_ref, send_sems, recv_sems):
        x, y, c, _ = _mesh_pos()
        me = 4 * x + 2 * y + c
        all_ref[me] = v_ref[...]
        peers = [(x ^ ((k >> 2) & 1), y ^ ((k >> 1) & 1), c ^ (k & 1)) for k in range(1, 8)]
        idx = lambda p: 4 * p[0] + 2 * p[1] + p[2]
        send = [pltpu.make_async_remote_copy(src_ref=v_ref, dst_ref=all_ref.at[me], send_sem=send_sems.at[k],
                                             recv_sem=recv_sems.at[k], device_id=p, device_id_type=MESH)
                for k, p in enumerate(peers)]
        for cp in send:
            cp.start()
        for k, p in enumerate(peers):
            pltpu.make_async_remote_copy(src_ref=v_ref, dst_ref=all_ref.at[idx(p)], send_sem=send_sems.at[k],
                                         recv_sem=recv_sems.at[k], device_id=p, device_id_type=MESH).wait_recv()
        for cp in send:
            cp.wait_send()
        acc = all_ref[0]
        for d in range(1, 8):
            acc = acc + all_ref[d]
        o_ref[...] = acc

    vm = pl.BlockSpec(memory_space=pltpu.VMEM)
    return pl.pallas_call(
        body, name="all_reduce_small", in_specs=[vm], out_specs=vm,
        out_shape=jax.ShapeDtypeStruct((SMALL_ROWS, 1024), F32),
        scratch_shapes=[pltpu.VMEM((8, SMALL_ROWS, 1024), F32), pltpu.SemaphoreType.DMA((7,)),
                        pltpu.SemaphoreType.DMA((7,))],)(v)


def _adamw(name, w, g, m, v):
    rows, cols = w.shape
    tr = rows
    if rows * cols > 400_000:
        tr = max(t for t in range(8, rows, 8) if rows % t == 0 and t * cols <= 400_000)

    def body(w_ref, g_ref, m_ref, v_ref, d_ref, nm_ref, nv_ref):
        gv = g_ref[...]
        m_new = ADAM_B1 * m_ref[...] + (1.0 - ADAM_B1) * gv
        v_new = ADAM_B2 * v_ref[...] + (1.0 - ADAM_B2) * (gv * gv)
        m_hat = m_new / (1.0 - ADAM_B1 ** ADAM_STEP)
        v_hat = v_new / (1.0 - ADAM_B2 ** ADAM_STEP)
        d_ref[...] = -ADAM_LR * (m_hat / (jnp.sqrt(v_hat) + ADAM_EPS) + ADAM_WD * w_ref[...])
        nm_ref[...] = m_new
        nv_ref[...] = v_new

    spec = pl.BlockSpec((tr, cols), lambda i: (i, 0))
    sds = jax.ShapeDtypeStruct((rows, cols), F32)
    return pl.pallas_call(body, grid=(rows // tr,), in_specs=[spec] * 4, out_specs=[spec] * 3, out_shape=[sds] * 3,
                          name=name, compiler_params=_params(1))(w, g, m, v)


WEIGHTS = ("ffn1_norm", "ffn1_w_gu", "ffn1_w_down", "mix_norm", "ffn2_norm", "ffn2_w_gu", "ffn2_w_down",
           "a_w_in", "a_w_conv", "a_A_log", "a_dt_bias", "a_out_norm", "a_w_out",
           "b_w_in", "b_b_in", "b_sinks", "b_w_out", "b_b_out", "final_norm")
SMALL_SLOTS = {"ffn1_norm": (0, 2048), "mix_norm": (2048, 2048), "ffn2_norm": (4096, 2048), "final_norm": (6144, 1024),
               "a_A_log": (7168, 8), "a_dt_bias": (7296, 8), "a_out_norm": (7424, 128), "b_sinks": (7552, 16),
               "loss": (7680, 1)}
SMALL_SHARDED = {"a_w_conv": (8192, 8192, (4,), 3072), "b_b_in": (20480, 11264, (), 1536), "b_b_out": (22016, 11648, (), 1024)}
DEV_SMALL_ROWS = 12


def _pack_rows(parts):
    flat = jnp.concatenate([p.reshape(-1, 1024) for p in parts], axis=0)
    return jnp.pad(flat, ((0, PACK_ROWS - PACK_USED), (0, 0)))


def _place(vec, off, a):
    return lax.dynamic_update_slice(vec, a.reshape(-1).astype(F32), (off,))


def kernel(x, ffn1_norm, ffn1_w_gu, ffn1_w_down, mix_norm, ffn2_norm, ffn2_w_gu, ffn2_w_down, a_w_in, a_w_conv, a_A_log, a_dt_bias, a_out_norm, a_w_out, b_w_in, b_b_in, b_sinks, b_w_out, b_b_out, final_norm, loss_target, m_ffn1_norm, m_ffn1_w_gu, m_ffn1_w_down, m_mix_norm, m_ffn2_norm, m_ffn2_w_gu, m_ffn2_w_down, m_a_w_in, m_a_w_conv, m_a_A_log, m_a_dt_bias, m_a_out_norm, m_a_w_out, m_b_w_in, m_b_b_in, m_b_sinks, m_b_w_out, m_b_b_out, m_final_norm, v_ffn1_norm, v_ffn1_w_gu, v_ffn1_w_down, v_mix_norm, v_ffn2_norm, v_ffn2_w_gu, v_ffn2_w_down, v_a_w_in, v_a_w_conv, v_a_A_log, v_a_dt_bias, v_a_out_norm, v_a_w_out, v_b_w_in, v_b_b_in, v_b_sinks, v_b_w_out, v_b_b_out, v_final_norm):
    w = dict(zip(WEIGHTS, (ffn1_norm, ffn1_w_gu, ffn1_w_down, mix_norm, ffn2_norm, ffn2_w_gu, ffn2_w_down, a_w_in, a_w_conv,
                           a_A_log, a_dt_bias, a_out_norm, a_w_out, b_w_in, b_b_in, b_sinks, b_w_out, b_b_out, final_norm)))
    m = dict(zip(WEIGHTS, (m_ffn1_norm, m_ffn1_w_gu, m_ffn1_w_down, m_mix_norm, m_ffn2_norm, m_ffn2_w_gu, m_ffn2_w_down,
                           m_a_w_in, m_a_w_conv, m_a_A_log, m_a_dt_bias, m_a_out_norm, m_a_w_out, m_b_w_in, m_b_b_in,
                           m_b_sinks, m_b_w_out, m_b_b_out, m_final_norm)))
    v = dict(zip(WEIGHTS, (v_ffn1_norm, v_ffn1_w_gu, v_ffn1_w_down, v_mix_norm, v_ffn2_norm, v_ffn2_w_gu, v_ffn2_w_down,
                           v_a_w_in, v_a_w_conv, v_a_A_log, v_a_dt_bias, v_a_out_norm, v_a_w_out, v_b_w_in, v_b_b_in,
                           v_b_sinks, v_b_w_out, v_b_b_out, v_final_norm)))
    chip = 2 * lax.axis_index("x") + lax.axis_index("y")

    big = _pack_rows([w[n].astype(BF16) for n, _ in PACK])
    small = jnp.zeros((4096,), F32)
    small = _place(small, 0, w["a_w_conv"])
    small = _place(small, 3072, w["b_b_in"])
    small = _place(small, 3456, w["b_b_out"]).reshape(SMALL_SHARD)
    gb, gs = _gather_shards(big, small)
    offs, o = {}, 0
    for n, r in PACK:
        offs[n] = (o, r)
        o += r
    blk = lambda n: gb[:, offs[n][0]:offs[n][0] + offs[n][1]]
    gsf = gs.reshape(4, 4096)
    W = {n: w[n] for n in ("ffn1_norm", "ffn2_norm", "mix_norm", "a_out_norm")}
    for n in ("ffn1_w_gu", "ffn2_w_gu"):
        W[n] = blk(n).reshape(4, 2, 1024, FF_BLK).transpose(1, 0, 2, 3).reshape(2, 2, 2, 1024, FF_BLK)
    for n in ("ffn1_w_down", "ffn2_w_down"):
        W[n] = blk(n).reshape(4, 2, 704, 1024).transpose(1, 0, 2, 3).reshape(2, 2, FF_BLK, 1024)
    W["a_w_in"] = jnp.pad(blk("a_w_in").reshape(4, 1024, 1028).transpose(1, 0, 2).reshape(1024, A_IN_COLS),
                          ((0, 0), (0, A_IN_PAD - A_IN_COLS)))
    W["a_w_out"] = blk("a_w_out").reshape(1024, 1024)
    W["b_w_in"] = blk("b_w_in").reshape(4, 1024, 384).transpose(1, 0, 2).reshape(1024, 1536)
    W["b_w_out"] = blk("b_w_out").reshape(1024, 1024)
    W["a_w_conv"] = gsf[:, 0:3072].reshape(4, 4, 768).transpose(1, 0, 2).reshape(4, 3072)
    W["b_b_in"] = gsf[:, 3072:3456].reshape(1, 1536)
    W["b_b_out"] = gsf[:, 3456:3712].reshape(1, 1024)
    W["a_gate_p"] = (jnp.zeros((2, 128), F32).at[0, 8:16].set(-jnp.exp(w["a_A_log"][0])).at[1, 8:16].set(w["a_dt_bias"][0]))
    W["b_sinks"] = jnp.broadcast_to(w["b_sinks"][0][:, None, None], (B_HEADS, 1, 128))
    W["final_norm"] = w["final_norm"][None]

    loss_p, dx, g = _local_step(x[0], loss_target[0], W)

    parts = [
        g["ffn1_w_gu"].reshape(2, 4, 1024, FF_BLK).transpose(1, 0, 2, 3), g["ffn2_w_gu"].reshape(2, 4, 1024, FF_BLK).transpose(1, 0, 2, 3),
        g["ffn1_w_down"].reshape(2, 4, 704, 1024).transpose(1, 0, 2, 3), g["ffn2_w_down"].reshape(2, 4, 704, 1024).transpose(1, 0, 2, 3),
        g["a_w_in"][:, :A_IN_COLS].reshape(1024, 4, 1028).transpose(1, 0, 2), g["a_w_out"].reshape(4, 256, 1024),
        g["b_w_in"].reshape(1024, 4, 384).transpose(1, 0, 2), g["b_w_out"].reshape(4, 256, 1024)]
    p = jnp.concatenate([a.astype(BF16).reshape(4, -1, 1024) for a in parts], axis=1)
    p = jnp.pad(p, ((0, 0), (0, PACK_ROWS - PACK_USED), (0, 0)))
    red = _reduce_scatter(p)
    grads = {n: red[offs[n][0]:offs[n][0] + offs[n][1]].reshape(w[n].shape) for n, _ in PACK}

    sv = jnp.zeros((SMALL_ROWS * 1024,), F32)
    small_g = {"ffn1_norm": g["ffn1_norm"], "mix_norm": g["mix_norm"], "ffn2_norm": g["ffn2_norm"], "final_norm": g["final_norm"],
               "a_A_log": g["a_gate_p"][0, 8:16], "a_dt_bias": g["a_gate_p"][1, 8:16], "a_out_norm": g["a_out_norm"],
               "b_sinks": g["b_sinks"], "loss": loss_p[0, 0:1]}
    for n, (off, _) in SMALL_SLOTS.items():
        sv = _place(sv, off, small_g[n])
    for n, (off, _, _, _) in SMALL_SHARDED.items():
        sv = _place(sv, off, g[n])
    tot = _all_reduce_small(sv.reshape(SMALL_ROWS, 1024)).reshape(-1)
    for n, (off, size) in SMALL_SLOTS.items():
        if n != "loss":
            grads[n] = tot[off:off + size].reshape(w[n].shape)
    for n, (off, _, lead, last) in SMALL_SHARDED.items():
        full = tot[off:off + (lead[0] if lead else 1) * last].reshape(lead + (last,))
        width = last // 4
        grads[n] = lax.dynamic_slice_in_dim(full, chip * width, width, axis=-1).reshape(w[n].shape)
    loss = tot[SMALL_SLOTS["loss"][0]]

    delta, new_m, new_v = {}, {}, {}
    for n, _ in PACK:
        two_d = lambda a: a.reshape(-1, a.shape[-1])
        d, nm, nv = _adamw("adamw_" + n, two_d(w[n]), two_d(grads[n]), two_d(m[n]), two_d(v[n]))
        delta[n], new_m[n], new_v[n] = d.reshape(w[n].shape), nm.reshape(w[n].shape), nv.reshape(w[n].shape)

    def dev_small(src):
        vec = jnp.zeros((DEV_SMALL_ROWS * 1024,), F32)
        for n, (off, _) in SMALL_SLOTS.items():
            if n != "loss":
                vec = _place(vec, off, src[n])
        for n, (_, off, _, _) in SMALL_SHARDED.items():
            vec = _place(vec, off, src[n])
        return vec.reshape(DEV_SMALL_ROWS, 1024)

    sd, sm, svv = _adamw("adamw_small", dev_small(w), dev_small(grads), dev_small(m), dev_small(v))
    for n in WEIGHTS:
        if n in SMALL_SLOTS:
            off, size = SMALL_SLOTS[n]
        elif n in SMALL_SHARDED:
            off, size = SMALL_SHARDED[n][1], w[n].size
        else:
            continue
        for dst, src in ((delta, sd), (new_m, sm), (new_v, svv)):
            dst[n] = src.reshape(-1)[off:off + size].reshape(w[n].shape)

    return (loss, dx[None], *[grads[n] for n in WEIGHTS], *[delta[n] for n in WEIGHTS],
            *[new_m[n] for n in WEIGHTS], *[new_v[n] for n in WEIGHTS])
```

```python
import jax
import jax.numpy as jnp
from jax import lax
from jax.experimental import pallas as pl
from jax.experimental.pallas import tpu as pltpu

F32 = jnp.float32
BF16 = jnp.bfloat16

D_MODEL = 1024
EPS = 1e-6
FF_BLK = 1408
A_HEADS = 8
A_DK = 128
A_CHUNK = 64
A_HG = 4
A_IN_COLS = 4112
A_IN_PAD = 4224
B_HEADS = 16
B_KV = 4
B_HD = 64
B_BLK = 128
ADAM_LR, ADAM_B1, ADAM_B2, ADAM_EPS, ADAM_WD, ADAM_STEP = 0.001, 0.9, 0.999, 1e-08, 0.01, 10
MESH = pl.DeviceIdType.MESH
VMEM_LIMIT = 56 * 1024 * 1024


def _params(n_axes):
    return pltpu.CompilerParams(dimension_semantics=("arbitrary",) * n_axes, vmem_limit_bytes=VMEM_LIMIT)


def _sigmoid(x):
    return 1.0 / (1.0 + jnp.exp(-x))


def _dot(a, b, ca, cb):
    return lax.dot_general(a, b, (((ca,), (cb,)), ((), ())), preferred_element_type=F32)


def _dotb(a, b, ca=1, cb=0):
    return _dot(a.astype(BF16), b.astype(BF16), ca, cb)


def _dotx(a, b, ca=1, cb=0):
    return lax.dot_general(a, b, (((ca,), (cb,)), ((), ())), preferred_element_type=F32,
                           precision=lax.Precision.HIGHEST)


def _doth(a, b, ca=1, cb=0):
    return lax.dot_general(a, b, (((ca,), (cb,)), ((), ())), preferred_element_type=F32,
                           precision=lax.Precision.HIGH)


def _rms_fwd(name, x, w):
    T, D = x.shape
    tt = min(T, 512)

    def body(x_ref, w_ref, h_ref):
        xv = x_ref[...]
        r = lax.rsqrt(jnp.mean(xv * xv, axis=-1, keepdims=True) + EPS)
        h_ref[...] = (xv * r * w_ref[...]).astype(BF16)

    return pl.pallas_call(
        body, grid=(T // tt,),
        in_specs=[pl.BlockSpec((tt, D), lambda i: (i, 0)), pl.BlockSpec((1, D), lambda i: (0, 0))],
        out_specs=pl.BlockSpec((tt, D), lambda i: (i, 0)),
        out_shape=jax.ShapeDtypeStruct((T, D), BF16), name=name, compiler_params=_params(1))(x, w)


def _rms_bwd_tile(dh, xv, dy, w):
    r = lax.rsqrt(jnp.mean(xv * xv, axis=-1, keepdims=True) + EPS)
    xhat = xv * r
    dxhat = dh * w
    dx = dy + r * (dxhat - xhat * jnp.mean(dxhat * xhat, axis=-1, keepdims=True))
    return dx, jnp.sum(dh * xhat, axis=0, keepdims=True)


def _colsum(name, a):
    T, N = a.shape
    tt = min(T, 512)

    def body(a_ref, o_ref):
        @pl.when(pl.program_id(0) == 0)
        def _():
            o_ref[...] = jnp.zeros_like(o_ref)
        o_ref[...] += jnp.sum(a_ref[...].astype(F32), axis=0, keepdims=True)

    return pl.pallas_call(
        body, grid=(T // tt,), in_specs=[pl.BlockSpec((tt, N), lambda i: (i, 0))],
        out_specs=pl.BlockSpec((1, N), lambda i: (0, 0)),
        out_shape=jax.ShapeDtypeStruct((1, N), F32), name=name, compiler_params=_params(1))(a)


def _matmul(name, a, b, ca, cb, tm, tn, tk, extra_in, outs, epi, order="ji"):
    M, K, N = a.shape[1 - ca], a.shape[ca], b.shape[1 - cb]
    tm, tn, tk = min(tm, M), min(tn, N), min(tk, K)
    assert M % tm == 0 and N % tn == 0 and K % tk == 0, (name, M, N, K, tm, tn, tk)
    ni, nj, nk = M // tm, N // tn, K // tk
    if order == "ji":
        grid = (nj, ni, nk)
        perm = lambda g0, g1, g2: (g1, g0, g2)
    else:
        grid = (ni, nj, nk)
        perm = lambda g0, g1, g2: (g0, g1, g2)

    def wrap(f):
        return lambda g0, g1, g2: f(*perm(g0, g1, g2))

    a_spec = (pl.BlockSpec((tm, tk), wrap(lambda i, j, k: (i, k))) if ca == 1
              else pl.BlockSpec((tk, tm), wrap(lambda i, j, k: (k, i))))
    b_spec = (pl.BlockSpec((tk, tn), wrap(lambda i, j, k: (k, j))) if cb == 0
              else pl.BlockSpec((tn, tk), wrap(lambda i, j, k: (j, k))))
    ne, no = len(extra_in), len(outs)

    def body(*refs):
        a_ref, b_ref = refs[0], refs[1]
        ex, out = refs[2:2 + ne], refs[2 + ne:2 + ne + no]
        i, j, k = perm(pl.program_id(0), pl.program_id(1), pl.program_id(2))
        p = _dotb(a_ref[...], b_ref[...], ca, cb)
        if nk == 1:
            epi(p, ex, out, i, j)
        else:
            acc_ref = refs[-1]

            @pl.when(k == 0)
            def _():
                acc_ref[...] = p

            @pl.when(k > 0)
            def _():
                acc_ref[...] += p

            @pl.when(k == nk - 1)
            def _():
                epi(acc_ref[...], ex, out, i, j)

    return pl.pallas_call(
        body, grid=grid,
        in_specs=[a_spec, b_spec] + [pl.BlockSpec(bs, wrap(f)) for _, bs, f in extra_in],
        out_specs=[pl.BlockSpec(bs, wrap(f)) for _, bs, f in outs],
        out_shape=[s for s, _, _ in outs],
        scratch_shapes=[pltpu.VMEM((tm, tn), F32)] if nk > 1 else [],
        name=name, compiler_params=_params(3))(a, b, *[x for x, _, _ in extra_in])


def _mm_plain(name, a, b, ca, cb, out_dtype, tm=1024, tn=1024, tk=1024, scale=1.0, bias=None):
    M, N = a.shape[1 - ca], b.shape[1 - cb]
    tm, tn = min(tm, M), min(tn, N)
    extra = [] if bias is None else [(bias, (1, tn), lambda i, j, k: (0, j))]

    def epi(acc, ex, out, i, j):
        r = acc * scale if scale != 1.0 else acc
        if bias is not None:
            r = r + ex[0][...]
        out[0][...] = r.astype(out_dtype)

    return _matmul(name, a, b, ca, cb, tm, tn, tk, extra,
                   [(jax.ShapeDtypeStruct((M, N), out_dtype), (tm, tn), lambda i, j, k: (i, j))], epi)[0]


def _mm_residual(name, a, b, x, scale, bias=None, tk=1024):
    M, N = x.shape
    tm, tn = min(512, M), N
    extra = [(x, (tm, tn), lambda i, j, k: (i, j))]
    if bias is not None:
        extra.append((bias, (1, tn), lambda i, j, k: (0, j)))

    def epi(acc, ex, out, i, j):
        r = acc if bias is None else acc + ex[1][...]
        out[0][...] = ex[0][...] + scale * r

    return _matmul(name, a, b, 1, 0, tm, tn, tk, extra,
                   [(jax.ShapeDtypeStruct((M, N), F32), (tm, tn), lambda i, j, k: (i, j))], epi, order="ij")[0]


def _mm_rms_bwd(name, dproj, w_in, x, dy, nw, tk):
    M, N = x.shape
    tm = min(512, M)
    extra = [(x, (tm, N), lambda i, j, k: (i, 0)), (dy, (tm, N), lambda i, j, k: (i, 0)),
             (nw, (1, N), lambda i, j, k: (0, 0))]

    def epi(acc, ex, out, i, j):
        dx, dw = _rms_bwd_tile(acc, ex[0][...], ex[1][...], ex[2][...])
        out[0][...] = dx
        out[1][...] = dx.astype(BF16)

        @pl.when(i == 0)
        def _():
            out[2][...] = dw

        @pl.when(i > 0)
        def _():
            out[2][...] += dw

    return _matmul(name, dproj, w_in, 1, 1, tm, N, tk, extra,
                   [(jax.ShapeDtypeStruct((M, N), F32), (tm, N), lambda i, j, k: (i, 0)),
                    (jax.ShapeDtypeStruct((M, N), BF16), (tm, N), lambda i, j, k: (i, 0)),
                    (jax.ShapeDtypeStruct((1, N), F32), (1, N), lambda i, j, k: (0, 0))], epi, order="ij")


def _ffn_gu(name, h, wgu):
    T, D = h.shape
    tm = min(T, 1024)
    rs = min(tm, 256)

    def body(h_ref, w_ref, gu_ref, act_ref):
        for r in range(tm // rs):
            rows = pl.ds(r * rs, rs)
            hv = h_ref[rows, :]
            g = _dot(hv, w_ref[0, 0], 1, 0)
            u = _dot(hv, w_ref[1, 0], 1, 0)
            gu_ref[0, 0, rows, :] = g.astype(BF16)
            gu_ref[1, 0, rows, :] = u.astype(BF16)
            act_ref[0, rows, :] = (g * _sigmoid(g) * u).astype(BF16)

    return pl.pallas_call(
        body, grid=(2, T // tm),
        in_specs=[pl.BlockSpec((tm, D), lambda j, i: (i, 0)),
                  pl.BlockSpec((2, 1, D, FF_BLK), lambda j, i: (0, j, 0, 0))],
        out_specs=[pl.BlockSpec((2, 1, tm, FF_BLK), lambda j, i: (0, j, i, 0)),
                   pl.BlockSpec((1, tm, FF_BLK), lambda j, i: (j, i, 0))],
        out_shape=[jax.ShapeDtypeStruct((2, 2, T, FF_BLK), BF16), jax.ShapeDtypeStruct((2, T, FF_BLK), BF16)],
        name=name, compiler_params=_params(2))(h, wgu)


def _ffn_down(name, act, wd, x):
    T, D = x.shape
    tm = min(T, 512)

    def body(act_ref, wd_ref, x_ref, o_ref):
        acc = _dot(act_ref[0], wd_ref[0], 1, 0) + _dot(act_ref[1], wd_ref[1], 1, 0)
        o_ref[...] = x_ref[...] + 0.5 * acc

    return pl.pallas_call(
        body, grid=(T // tm,),
        in_specs=[pl.BlockSpec((2, tm, FF_BLK), lambda i: (0, i, 0)),
                  pl.BlockSpec((2, FF_BLK, D), lambda i: (0, 0, 0)),
                  pl.BlockSpec((tm, D), lambda i: (i, 0))],
        out_specs=pl.BlockSpec((tm, D), lambda i: (i, 0)),
        out_shape=jax.ShapeDtypeStruct((T, D), F32), name=name, compiler_params=_params(1))(act, wd, x)


def _ffn_dact(name, dyb, wd, gu):
    T, D = dyb.shape
    tm = min(T, 1024)
    rs = min(tm, 256)

    def body(dy_ref, wd_ref, gu_ref, o_ref):
        for r in range(tm // rs):
            rows = pl.ds(r * rs, rs)
            dact = 0.5 * _dot(dy_ref[rows, :], wd_ref[0], 1, 1)
            g = gu_ref[0, 0, rows, :].astype(F32)
            u = gu_ref[1, 0, rows, :].astype(F32)
            s = _sigmoid(g)
            o_ref[0, 0, rows, :] = (dact * u * (s + g * s * (1.0 - s))).astype(BF16)
            o_ref[1, 0, rows, :] = (dact * g * s).astype(BF16)

    return pl.pallas_call(
        body, grid=(2, T // tm),
        in_specs=[pl.BlockSpec((tm, D), lambda j, i: (i, 0)),
                  pl.BlockSpec((1, FF_BLK, D), lambda j, i: (j, 0, 0)),
                  pl.BlockSpec((2, 1, tm, FF_BLK), lambda j, i: (0, j, i, 0))],
        out_specs=pl.BlockSpec((2, 1, tm, FF_BLK), lambda j, i: (0, j, i, 0)),
        out_shape=jax.ShapeDtypeStruct((2, 2, T, FF_BLK), BF16), name=name, compiler_params=_params(2))(dyb, wd, gu)


def _ffn_dwd(name, act, dyb):
    _, T, _ = act.shape
    D = dyb.shape[1]
    tk = min(T, 1024)

    def body(a_ref, d_ref, o_ref):
        p = 0.5 * _dot(a_ref[0], d_ref[...], 0, 0)

        @pl.when(pl.program_id(1) == 0)
        def _():
            o_ref[0] = p

        @pl.when(pl.program_id(1) > 0)
        def _():
            o_ref[0] += p

    return pl.pallas_call(
        body, grid=(2, T // tk),
        in_specs=[pl.BlockSpec((1, tk, FF_BLK), lambda j, k: (j, k, 0)), pl.BlockSpec((tk, D), lambda j, k: (k, 0))],
        out_specs=pl.BlockSpec((1, FF_BLK, D), lambda j, k: (j, 0, 0)),
        out_shape=jax.ShapeDtypeStruct((2, FF_BLK, D), F32), name=name, compiler_params=_params(2))(act, dyb)


def _ffn_dwgu(name, h, dgu):
    T, D = h.shape
    tk = min(T, 1024)

    def body(h_ref, d_ref, o_ref):
        p = _dot(h_ref[...], d_ref[0, 0], 0, 0)

        @pl.when(pl.program_id(1) == 0)
        def _():
            o_ref[0, 0] = p

        @pl.when(pl.program_id(1) > 0)
        def _():
            o_ref[0, 0] += p

    return pl.pallas_call(
        body, grid=(4, T // tk),
        in_specs=[pl.BlockSpec((tk, D), lambda q, k: (k, 0)),
                  pl.BlockSpec((1, 1, tk, FF_BLK), lambda q, k: (q // 2, q % 2, k, 0))],
        out_specs=pl.BlockSpec((1, 1, D, FF_BLK), lambda q, k: (q // 2, q % 2, 0, 0)),
        out_shape=jax.ShapeDtypeStruct((2, 2, D, FF_BLK), F32), name=name, compiler_params=_params(2))(h, dgu)


def _ffn_dx(name, dgu, wgu, x, dy, nw):
    T, D = x.shape
    tm = min(T, 512)

    def body(d_ref, w_ref, x_ref, dy_ref, nw_ref, dx_ref, dxb_ref, dnw_ref, acc_ref):
        i, k = pl.program_id(0), pl.program_id(1)
        p = _dot(d_ref[0, 0], w_ref[0, 0], 1, 1)

        @pl.when(k == 0)
        def _():
            acc_ref[...] = p

        @pl.when(k > 0)
        def _():
            acc_ref[...] += p

        @pl.when(k == 3)
        def _():
            dx, dw = _rms_bwd_tile(acc_ref[...], x_ref[...], dy_ref[...], nw_ref[...])
            dx_ref[...] = dx
            dxb_ref[...] = dx.astype(BF16)

            @pl.when(i == 0)
            def _():
                dnw_ref[...] = dw

            @pl.when(i > 0)
            def _():
                dnw_ref[...] += dw

    return pl.pallas_call(
        body, grid=(T // tm, 4),
        in_specs=[pl.BlockSpec((1, 1, tm, FF_BLK), lambda i, k: (k // 2, k % 2, i, 0)),
                  pl.BlockSpec((1, 1, D, FF_BLK), lambda i, k: (k // 2, k % 2, 0, 0)),
                  pl.BlockSpec((tm, D), lambda i, k: (i, 0)), pl.BlockSpec((tm, D), lambda i, k: (i, 0)),
                  pl.BlockSpec((1, D), lambda i, k: (0, 0))],
        out_specs=[pl.BlockSpec((tm, D), lambda i, k: (i, 0)), pl.BlockSpec((tm, D), lambda i, k: (i, 0)),
                   pl.BlockSpec((1, D), lambda i, k: (0, 0))],
        out_shape=[jax.ShapeDtypeStruct((T, D), F32), jax.ShapeDtypeStruct((T, D), BF16),
                   jax.ShapeDtypeStruct((1, D), F32)],
        scratch_shapes=[pltpu.VMEM((tm, D), F32)],
        name=name, compiler_params=_params(2))(dgu, wgu, x, dy, nw)


def _ffn_fwd(tag, x, nw, wgu, wd):
    h = _rms_fwd(tag + "_rms", x, nw)
    gu, act = _ffn_gu(tag + "_gu", h, wgu)
    return _ffn_down(tag + "_down", act, wd, x), (h, gu, act)


def _ffn_bwd(tag, dy, dyb, x, nw, wgu, wd, saved):
    h, gu, act = saved
    dgu = _ffn_dact(tag + "_dact", dyb, wd, gu)
    d_wd = _ffn_dwd(tag + "_dwd", act, dyb)
    d_wgu = _ffn_dwgu(tag + "_dwgu", h, dgu)
    dx, dxb, d_nw = _ffn_dx(tag + "_dx", dgu, wgu, x, dy, nw)
    return dx, dxb, d_nw, d_wgu, d_wd


def _conv_taps(cur, halo, w, first, sign):
    tt = cur.shape[0]
    halo = jnp.where(first, 0.0, halo)
    rid = lax.broadcasted_iota(jnp.int32, (8, cur.shape[1]), 0)
    acc = w[3:4, :] * cur
    for s in (1, 2, 3):
        if sign < 0:
            sh = pltpu.roll(cur, s, 0)
            edge = jnp.where(rid < s, pltpu.roll(halo, s, 0), sh[0:8])
            sh = jnp.concatenate([edge, sh[8:]], axis=0) if tt > 8 else edge
        else:
            sh = pltpu.roll(cur, tt - s, 0)
            edge = jnp.where(rid >= 8 - s, pltpu.roll(halo, 8 - s, 0), sh[tt - 8:])
            sh = jnp.concatenate([sh[:tt - 8], edge], axis=0) if tt > 8 else edge
        acc = acc + w[3 - s:4 - s, :] * sh
    return acc


def _gdn_prep(name, proj, wconv, gate_p):
    T = proj.shape[0]
    tt = min(T, 256)
    hb = tt // 8
    nch = tt // A_CHUNK

    def body(cur_ref, halo_ref, ba_ref, w_ref, gp_ref, qkv_ref, bg_ref, gc_ref):
        first = pl.program_id(0) == 0
        for c in range(24):
            cols = pl.ds(c * 128, 128)
            conv = _conv_taps(cur_ref[:, cols], halo_ref[:, cols], w_ref[:, cols], first, -1)
            y = conv * _sigmoid(conv)
            if c < 16:
                y = y * lax.rsqrt(jnp.sum(y * y, axis=-1, keepdims=True) + EPS)
                if c < 8:
                    y = y * (A_DK ** -0.5)
            qkv_ref[:, cols] = y
        ba = ba_ref[...]
        lane = lax.broadcasted_iota(jnp.int32, ba.shape, 1)
        zarg = ba + gp_ref[1:2, :]
        softplus = jnp.maximum(zarg, 0.0) + jnp.log(1.0 + jnp.exp(-jnp.abs(zarg)))
        bg = jnp.where(lane < 8, _sigmoid(ba), jnp.where(lane < 16, gp_ref[0:1, :] * softplus, 0.0))
        bg_ref[...] = bg
        tri = (lax.broadcasted_iota(jnp.int32, (A_CHUNK, A_CHUNK), 0)
               >= lax.broadcasted_iota(jnp.int32, (A_CHUNK, A_CHUNK), 1)).astype(F32)
        for c in range(nch):
            rows = pl.ds(c * A_CHUNK, A_CHUNK)
            gc_ref[rows, :] = _dotx(tri, bg[c * A_CHUNK:(c + 1) * A_CHUNK, :])

    return pl.pallas_call(
        body, grid=(T // tt,),
        in_specs=[pl.BlockSpec((tt, 3072), lambda i: (i, 0)),
                  pl.BlockSpec((8, 3072), lambda i: (jnp.maximum(i * hb - 1, 0), 0)),
                  pl.BlockSpec((tt, 128), lambda i: (i, 32)),
                  pl.BlockSpec((4, 3072), lambda i: (0, 0)),
                  pl.BlockSpec((2, 128), lambda i: (0, 0))],
        out_specs=[pl.BlockSpec((tt, 3072), lambda i: (i, 0)), pl.BlockSpec((tt, 128), lambda i: (i, 0)),
                   pl.BlockSpec((tt, 128), lambda i: (i, 0))],
        out_shape=[jax.ShapeDtypeStruct((T, 3072), F32), jax.ShapeDtypeStruct((T, 128), F32),
                   jax.ShapeDtypeStruct((T, 128), F32)],
        name=name, compiler_params=_params(1))(proj, proj, proj, wconv, gate_p)


def _chunk_masks():
    ri = lax.broadcasted_iota(jnp.int32, (A_CHUNK, A_CHUNK), 0)
    ci = lax.broadcasted_iota(jnp.int32, (A_CHUNK, A_CHUNK), 1)
    return ri >= ci, ri > ci, ri == ci


def _chunk_local(q, k, gcol, grow, bcol):
    incl, strict, _ = _chunk_masks()
    dec = jnp.where(incl, jnp.exp(jnp.where(incl, gcol - grow, 0.0)), 0.0)
    e = jnp.exp(gcol)
    glast = grow[:, A_CHUNK - 1:A_CHUNK]
    f = jnp.exp(glast - gcol)
    gl = jnp.exp(glast)
    kb = k * bcol
    lmat = jnp.where(strict, _dotb(kb, k, 1, 1) * dec, 0.0)
    amat = jnp.where(incl, _dotb(q, k, 1, 1) * dec, 0.0)
    return dec, e, f, gl, kb, lmat, amat


def _unit_lower_inverse(lmat):
    _, _, eye = _chunk_masks()
    t = jnp.where(eye, 1.0, 0.0) - lmat
    lp = _doth(lmat, lmat)
    for it in range(5):
        t = t + _doth(t, lp)
        if it < 4:
            lp = _doth(lp, lp)
    return t


def _gdn_delta_fwd(name, qkv, gcol, grow, bcol):
    T = qkv.shape[0]
    tt = min(T, 512)
    nch = tt // A_CHUNK
    NC = T // A_CHUNK
    wd, ng = 128 * A_HG, A_HEADS // A_HG

    def body(q_ref, k_ref, v_ref, gc_ref, gr_ref, bc_ref, o_ref, s_ref, t_ref, u_ref, w_ref, state):
        @pl.when(pl.program_id(1) == 0)
        def _():
            state[...] = jnp.zeros_like(state)

        def chunk(c, carry):
            rows = pl.ds(pl.multiple_of(c * A_CHUNK, A_CHUNK), A_CHUNK)
            for hh in range(A_HG):
                cols = pl.ds(hh * 128, 128)
                q, k, v = q_ref[rows, cols], k_ref[rows, cols], v_ref[rows, cols]
                gcl, bcl, grw = gc_ref[hh, rows, :], bc_ref[hh, rows, :], gr_ref[hh, c]
                dec, e, f, gl, kb, lmat, amat = _chunk_local(q, k, gcl, grw, bcl)
                tinv = _unit_lower_inverse(lmat)
                u = _doth(tinv, v * bcl)
                w = _doth(tinv, kb * e)
                s = state[hh]
                s_ref[hh, c] = s.astype(BF16)
                t_ref[hh, c] = tinv
                u_ref[rows, cols] = u
                w_ref[rows, cols] = w
                vn = u - _dotb(w, s)
                o_ref[rows, cols] = _dotb(q * e, s) + _dotb(amat, vn)
                state[hh] = s * gl + _dotb(k * f, vn, 0, 0)
            return carry

        lax.fori_loop(0, nch, chunk, 0)

    hd = lambda col: pl.BlockSpec((tt, wd), lambda h, i: (i, col * ng + h))
    col_spec = pl.BlockSpec((A_HG, tt, 1), lambda h, i: (h, i, 0))
    return pl.pallas_call(
        body, grid=(ng, T // tt),
        in_specs=[hd(0), hd(1), hd(2), col_spec, pl.BlockSpec((A_HG, nch, 1, A_CHUNK), lambda h, i: (h, i, 0, 0)), col_spec],
        out_specs=[pl.BlockSpec((tt, wd), lambda h, i: (i, h)),
                   pl.BlockSpec((A_HG, nch, 128, 128), lambda h, i: (h, i, 0, 0)),
                   pl.BlockSpec((A_HG, nch, A_CHUNK, A_CHUNK), lambda h, i: (h, i, 0, 0)),
                   pl.BlockSpec((tt, wd), lambda h, i: (i, h)), pl.BlockSpec((tt, wd), lambda h, i: (i, h))],
        out_shape=[jax.ShapeDtypeStruct((T, 1024), F32), jax.ShapeDtypeStruct((A_HEADS, NC, 128, 128), BF16),
                   jax.ShapeDtypeStruct((A_HEADS, NC, A_CHUNK, A_CHUNK), F32),
                   jax.ShapeDtypeStruct((T, 1024), F32), jax.ShapeDtypeStruct((T, 1024), F32)],
        scratch_shapes=[pltpu.VMEM((A_HG, 128, 128), F32)],
        name=name, compiler_params=_params(2))(qkv, qkv, qkv, gcol, grow, bcol)


def _gdn_delta_bwd(name, qkv, gcol, grow, bcol, d_o, s_sv, t_sv, u_sv, w_sv):
    T = qkv.shape[0]
    tt = min(T, 512)
    nch = tt // A_CHUNK
    ni = T // tt

    def body(q_ref, k_ref, v_ref, gc_ref, gr_ref, bc_ref, do_ref, s_ref, t_ref, u_ref, w_ref,
             dq_ref, dk_ref, dv_ref, dg_ref, db_ref, dstate):
        @pl.when(pl.program_id(1) == 0)
        def _():
            dstate[...] = jnp.zeros_like(dstate)

        incl, strict, _ = _chunk_masks()
        upper = (lax.broadcasted_iota(jnp.int32, (A_CHUNK, A_CHUNK), 0)
                 <= lax.broadcasted_iota(jnp.int32, (A_CHUNK, A_CHUNK), 1)).astype(F32)
        last_row = lax.broadcasted_iota(jnp.int32, (A_CHUNK, 1), 0) == A_CHUNK - 1
        ones = jnp.ones((A_CHUNK, 128), F32)

        rsum = lambda x: jnp.sum(x, axis=1, keepdims=True)

        def chunk(cc, carry):
            c = nch - 1 - cc
            rows = pl.ds(pl.multiple_of(c * A_CHUNK, A_CHUNK), A_CHUNK)
            for hh in range(A_HG):
                cols = pl.ds(hh * 128, 128)
                q, k, v = q_ref[rows, cols], k_ref[rows, cols], v_ref[rows, cols]
                gcl, bcl, grw = gc_ref[hh, rows, :], bc_ref[hh, rows, :], gr_ref[hh, c]
                do, u, w = do_ref[rows, cols], u_ref[rows, cols], w_ref[rows, cols]
                s, tinv = s_ref[hh, c], t_ref[hh, c]
                dec, e, f, gl, kb, lmat, amat = _chunk_local(q, k, gcl, grw, bcl)
                qd, kd, ke = q * e, k * f, kb * e
                ds = dstate[hh]
                vn = u - _dotb(w, s)
                d_qd = _dotb(do, s, 1, 1)
                d_a = jnp.where(incl, _dotb(do, vn, 1, 1), 0.0)
                d_vn = _dotb(amat, do, 0, 0) + _dotb(kd, ds)
                d_kd = _dotb(vn, ds, 1, 1)
                d_gl = jnp.sum(jnp.sum(ds * s.astype(F32), axis=1, keepdims=True), axis=0, keepdims=True)
                d_w = -_dotb(d_vn, s, 1, 1)
                dstate[hh] = gl * ds + _dotb(qd, do, 0, 0) - _dotb(w, d_vn, 0, 0)
                d_bv = _doth(tinv, d_vn, 0, 0)
                d_ke = _doth(tinv, d_w, 0, 0)
                d_l = -jnp.where(strict, _dotb(d_bv, u, 1, 1) + _dotb(d_ke, w, 1, 1), 0.0)
                d_kk = d_l * dec
                d_qk = d_a * dec
                d_kb = _dotb(d_kk, k)
                dk_ref[rows, cols] = (_dotb(d_kk, kb, 0, 0) + _dotb(d_qk, q, 0, 0) + d_kd * f + d_ke * (bcl * e)
                                      + d_kb * bcl)
                dq_ref[rows, cols] = _dotb(d_qk, k) + d_qd * e
                dv_ref[rows, cols] = d_bv * bcl
                db_ref[hh, rows, :] = rsum(d_ke * k) * e + rsum(d_kb * k) + rsum(d_bv * v)
                m = d_l * lmat + d_a * amat
                r_kd = rsum(d_kd * kd)
                col_m = _dotx(m, ones, 0, 0)[:, 0:1]
                d_gc = rsum(m) - col_m + rsum(d_qd * qd) - r_kd + rsum(d_ke * ke)
                tail = jnp.sum(r_kd, axis=0, keepdims=True) + d_gl * gl
                d_gc = d_gc + jnp.where(last_row, tail, 0.0)
                dg_ref[hh, rows, :] = _dotx(upper, d_gc * ones)[:, 0:1]
            return carry

        lax.fori_loop(0, nch, chunk, 0)

    wd, ng = 128 * A_HG, A_HEADS // A_HG
    rev = lambda i: ni - 1 - i
    hd = lambda col: pl.BlockSpec((tt, wd), lambda h, i: (rev(i), col * ng + h))
    hd1 = pl.BlockSpec((tt, wd), lambda h, i: (rev(i), h))
    col_spec = pl.BlockSpec((A_HG, tt, 1), lambda h, i: (h, rev(i), 0))
    return pl.pallas_call(
        body, grid=(ng, ni),
        in_specs=[hd(0), hd(1), hd(2), col_spec,
                  pl.BlockSpec((A_HG, nch, 1, A_CHUNK), lambda h, i: (h, rev(i), 0, 0)), col_spec, hd1,
                  pl.BlockSpec((A_HG, nch, 128, 128), lambda h, i: (h, rev(i), 0, 0)),
                  pl.BlockSpec((A_HG, nch, A_CHUNK, A_CHUNK), lambda h, i: (h, rev(i), 0, 0)), hd1, hd1],
        out_specs=[hd1, hd1, hd1, col_spec, col_spec],
        out_shape=[jax.ShapeDtypeStruct((T, 1024), F32)] * 3 + [jax.ShapeDtypeStruct((A_HEADS, T, 1), F32)] * 2,
        scratch_shapes=[pltpu.VMEM((A_HG, 128, 128), F32)],
        name=name, compiler_params=_params(2))(qkv, qkv, qkv, gcol, grow, bcol, d_o, s_sv, t_sv, u_sv, w_sv)


def _gdn_gate_fwd(name, o, proj, nw):
    T = o.shape[0]
    tt = min(T, 512)

    def body(o_ref, z_ref, nw_ref, y_ref):
        for h in range(A_HEADS):
            cols = pl.ds(h * 128, 128)
            ov, z = o_ref[:, cols], z_ref[:, cols]
            r = lax.rsqrt(jnp.mean(ov * ov, axis=-1, keepdims=True) + EPS)
            y_ref[:, cols] = (ov * r * nw_ref[...] * (z * _sigmoid(z))).astype(BF16)

    return pl.pallas_call(
        body, grid=(T // tt,),
        in_specs=[pl.BlockSpec((tt, 1024), lambda i: (i, 0)), pl.BlockSpec((tt, 1024), lambda i: (i, 3)),
                  pl.BlockSpec((1, 128), lambda i: (0, 0))],
        out_specs=pl.BlockSpec((tt, 1024), lambda i: (i, 0)),
        out_shape=jax.ShapeDtypeStruct((T, 1024), BF16), name=name, compiler_params=_params(1))(o, proj, nw)


def _gdn_gate_bwd(name, dy2, o, proj, nw):
    T = o.shape[0]
    tt = min(T, 512)

    def body(dy_ref, o_ref, z_ref, nw_ref, do_ref, dz_ref, dnw_ref):
        dnw = jnp.zeros((1, 128), F32)
        for h in range(A_HEADS):
            cols = pl.ds(h * 128, 128)
            dy, ov, z = dy_ref[:, cols], o_ref[:, cols], z_ref[:, cols]
            s = _sigmoid(z)
            sz = z * s
            r = lax.rsqrt(jnp.mean(ov * ov, axis=-1, keepdims=True) + EPS)
            xhat = ov * r
            dn = dy * sz
            dz_ref[:, cols] = dy * (xhat * nw_ref[...]) * (s + z * s * (1.0 - s))
            dxhat = dn * nw_ref[...]
            do_ref[:, cols] = r * (dxhat - xhat * jnp.mean(dxhat * xhat, axis=-1, keepdims=True))
            dnw = dnw + jnp.sum(dn * xhat, axis=0, keepdims=True)

        @pl.when(pl.program_id(0) == 0)
        def _():
            dnw_ref[...] = dnw

        @pl.when(pl.program_id(0) > 0)
        def _():
            dnw_ref[...] += dnw

    blk = lambda c: pl.BlockSpec((tt, 1024), lambda i: (i, c))
    return pl.pallas_call(
        body, grid=(T // tt,),
        in_specs=[blk(0), blk(0), blk(3), pl.BlockSpec((1, 128), lambda i: (0, 0))],
        out_specs=[blk(0), blk(0), pl.BlockSpec((1, 128), lambda i: (0, 0))],
        out_shape=[jax.ShapeDtypeStruct((T, 1024), F32), jax.ShapeDtypeStruct((T, 1024), F32),
                   jax.ShapeDtypeStruct((1, 128), F32)],
        name=name, compiler_params=_params(1))(dy2, o, proj, nw)


def _gdn_prep_bwd1(name, proj, wconv, gate_p, dq, dk, dv, dbg):
    T = proj.shape[0]
    tt = min(T, 256)
    hb = tt // 8

    def body(cur_ref, halo_ref, ba_ref, w_ref, gp_ref, dq_ref, dk_ref, dv_ref, dbg_ref,
             dc_ref, dw_ref, dba_ref, dgp_ref):
        first = pl.program_id(0) == 0
        rid = lax.broadcasted_iota(jnp.int32, (8, 128), 0)
        for c in range(24):
            cols = pl.ds(c * 128, 128)
            cur = cur_ref[:, cols]
            halo = jnp.where(first, 0.0, halo_ref[:, cols])
            conv = _conv_taps(cur, halo_ref[:, cols], w_ref[:, cols], first, -1)
            s = _sigmoid(conv)
            y = conv * s
            if c < 16:
                dref = dq_ref if c < 8 else dk_ref
                dn = dref[:, pl.ds((c % 8) * 128, 128)]
                rinv = lax.rsqrt(jnp.sum(y * y, axis=-1, keepdims=True) + EPS)
                yhat = y * rinv
                dyv = rinv * (dn - yhat * jnp.sum(dn * yhat, axis=-1, keepdims=True))
                if c < 8:
                    dyv = dyv * (A_DK ** -0.5)
            else:
                dyv = dv_ref[:, pl.ds((c - 16) * 128, 128)]
            dc = dyv * (s + conv * s * (1.0 - s))
            dc_ref[:, cols] = dc
            parts = [jnp.sum(dc * cur, axis=0, keepdims=True)]
            for sft in (1, 2, 3):
                sh = pltpu.roll(cur, sft, 0)
                edge = jnp.where(rid < sft, pltpu.roll(halo, sft, 0), sh[0:8])
                sh = jnp.concatenate([edge, sh[8:]], axis=0) if tt > 8 else edge
                parts.append(jnp.sum(dc * sh, axis=0, keepdims=True))
            dwc = jnp.concatenate(parts[::-1], axis=0)

            @pl.when(first)
            def _():
                dw_ref[:, cols] = dwc

            @pl.when(jnp.logical_not(first))
            def _():
                dw_ref[:, cols] += dwc

        ba = ba_ref[...]
        dbg = dbg_ref[...]
        lane = lax.broadcasted_iota(jnp.int32, ba.shape, 1)
        sb = _sigmoid(ba)
        zarg = ba + gp_ref[1:2, :]
        softplus = jnp.maximum(zarg, 0.0) + jnp.log(1.0 + jnp.exp(-jnp.abs(zarg)))
        d_b = dbg * sb * (1.0 - sb)
        d_a = dbg * gp_ref[0:1, :] * _sigmoid(zarg)
        dba_ref[...] = jnp.where(lane < 8, d_b, jnp.where(lane < 16, d_a, 0.0))
        g = gp_ref[0:1, :] * softplus
        in_a = (lane >= 8) & (lane < 16)
        sums = jnp.concatenate([jnp.sum(jnp.where(in_a, dbg * g, 0.0), axis=0, keepdims=True),
                                jnp.sum(jnp.where(in_a, d_a, 0.0), axis=0, keepdims=True)], axis=0)

        @pl.when(first)
        def _():
            dgp_ref[...] = sums

        @pl.when(jnp.logical_not(first))
        def _():
            dgp_ref[...] += sums

    row = lambda w, c=0: pl.BlockSpec((tt, w), lambda i: (i, c))
    return pl.pallas_call(
        body, grid=(T // tt,),
        in_specs=[row(3072), pl.BlockSpec((8, 3072), lambda i: (jnp.maximum(i * hb - 1, 0), 0)), row(128, 32),
                  pl.BlockSpec((4, 3072), lambda i: (0, 0)), pl.BlockSpec((2, 128), lambda i: (0, 0)),
                  row(1024), row(1024), row(1024), row(128)],
        out_specs=[row(3072), pl.BlockSpec((4, 3072), lambda i: (0, 0)), row(128),
                   pl.BlockSpec((2, 128), lambda i: (0, 0))],
        out_shape=[jax.ShapeDtypeStruct((T, 3072), F32), jax.ShapeDtypeStruct((4, 3072), F32),
                   jax.ShapeDtypeStruct((T, 128), F32), jax.ShapeDtypeStruct((2, 128), F32)],
        name=name, compiler_params=_params(1))(proj, proj, proj, wconv, gate_p, dq, dk, dv, dbg)


def _gdn_prep_bwd2(name, dc, wconv, dz, dba):
    T = dc.shape[0]
    tt = min(T, 256)
    hb = tt // 8
    ni = T // tt

    def body(cur_ref, halo_ref, w_ref, dz_ref, dba_ref, o_ref):
        last = pl.program_id(0) == ni - 1
        for c in range(24):
            cols = pl.ds(c * 128, 128)
            o_ref[:, cols] = _conv_taps(cur_ref[:, cols], halo_ref[:, cols], w_ref[:, cols], last, +1)
        o_ref[:, pl.ds(3072, 1024)] = dz_ref[...]
        o_ref[:, pl.ds(4096, 128)] = dba_ref[...]

    return pl.pallas_call(
        body, grid=(ni,),
        in_specs=[pl.BlockSpec((tt, 3072), lambda i: (i, 0)),
                  pl.BlockSpec((8, 3072), lambda i: (jnp.minimum((i + 1) * hb, T // 8 - 1), 0)),
                  pl.BlockSpec((4, 3072), lambda i: (0, 0)),
                  pl.BlockSpec((tt, 1024), lambda i: (i, 0)), pl.BlockSpec((tt, 128), lambda i: (i, 0))],
        out_specs=pl.BlockSpec((tt, A_IN_PAD), lambda i: (i, 0)),
        out_shape=jax.ShapeDtypeStruct((T, A_IN_PAD), F32), name=name, compiler_params=_params(1))(
            dc, dc, wconv, dz, dba)


def _gdn_fwd(x, nw, w_in, wconv, gate_p, out_nw, w_out):
    T = x.shape[0]
    h = _rms_fwd("a_rms", x, nw)
    proj = _mm_plain("a_proj", h, w_in, 1, 0, F32, tn=FF_BLK)
    qkv, bg, gcum = _gdn_prep("a_prep", proj, wconv, gate_p)
    bcol = bg[:, 0:8].T.reshape(A_HEADS, T, 1)
    gcol = gcum[:, 8:16].T.reshape(A_HEADS, T, 1)
    grow = gcol.reshape(A_HEADS, T // A_CHUNK, 1, A_CHUNK)
    o, s_sv, t_sv, u_sv, w_sv = _gdn_delta_fwd("a_delta", qkv, gcol, grow, bcol)
    o2 = _gdn_gate_fwd("a_gate", o, proj, out_nw)
    y = _mm_residual("a_out", o2, w_out, x, 1.0)
    return y, (h, proj, qkv, gcol, grow, bcol, o, s_sv, t_sv, u_sv, w_sv, o2)


def _gdn_bwd(dy, dyb, x, nw, w_in, wconv, gate_p, out_nw, w_out, saved):
    h, proj, qkv, gcol, grow, bcol, o, s_sv, t_sv, u_sv, w_sv, o2 = saved
    T = x.shape[0]
    d_o2 = _mm_plain("a_dout", dyb, w_out, 1, 1, F32)
    d_wout = _mm_plain("a_dwout", o2, dyb, 0, 0, F32)
    d_o, d_z, d_outnw = _gdn_gate_bwd("a_dgate", d_o2, o, proj, out_nw)
    dq, dk, dv, dg, dbeta = _gdn_delta_bwd("a_ddelta", qkv, gcol, grow, bcol, d_o, s_sv, t_sv, u_sv, w_sv)
    dbg = jnp.concatenate([dbeta.reshape(A_HEADS, T).T, dg.reshape(A_HEADS, T).T, jnp.zeros((T, 112), F32)], axis=1)
    dc, d_wconv, dba, dgp = _gdn_prep_bwd1("a_dprep1", proj, wconv, gate_p, dq, dk, dv, dbg)
    dproj = _gdn_prep_bwd2("a_dprep2", dc, wconv, d_z, dba)
    d_win = _mm_plain("a_dwin", h, dproj, 0, 0, F32, tn=FF_BLK)
    dx, dxb, d_nw = _mm_rms_bwd("a_dx", dproj, w_in, x, dy, nw, tk=FF_BLK)
    return dx, dxb, d_nw, d_win, d_wconv, dgp, d_outnw, d_wout


def _swa_masks(n):
    qi = lax.broadcasted_iota(jnp.int32, (B_BLK, B_BLK), 0)
    kj = lax.broadcasted_iota(jnp.int32, (B_BLK, B_BLK), 1)
    return kj > qi + jnp.where(n > 0, 0, B_BLK), kj <= qi


def _swa_fwd(name, q, k, v, sinks):
    T = q.shape[1]
    tq = min(T, 1024)
    nbt = tq // B_BLK
    scale = B_HD ** -0.5
    G = B_HEADS // B_KV

    def body(q_ref, k_ref, v_ref, kh_ref, vh_ref, s_ref, o_ref, l_ref):
        first_blk = pl.program_id(1) * nbt

        def block(n, kp, vp):
            m_prev, m_cur = _swa_masks(first_blk + n)
            cur = pl.ds(pl.multiple_of(n * B_BLK, B_BLK), B_BLK)
            kc, vc = k_ref[0, cur, :], v_ref[0, cur, :]
            for g in range(G):
                sink = s_ref[g][:, 0:1]
                qb = q_ref[g, cur, :]
                s_p = jnp.where(m_prev, _dot(qb, kp, 1, 1) * scale, -jnp.inf)
                s_c = jnp.where(m_cur, _dot(qb, kc, 1, 1) * scale, -jnp.inf)
                m = jnp.maximum(jnp.maximum(jnp.max(s_p, axis=1, keepdims=True), jnp.max(s_c, axis=1, keepdims=True)), sink)
                p_p, p_c = jnp.exp(s_p - m), jnp.exp(s_c - m)
                den = jnp.sum(p_p, axis=1, keepdims=True) + jnp.sum(p_c, axis=1, keepdims=True) + jnp.exp(sink - m)
                acc = _dotb(p_p, vp) + _dotb(p_c, vc)
                o_ref[g, cur, :] = (acc / den).astype(BF16)
                l_ref[g, cur, :] = m + jnp.log(den)

        block(0, kh_ref[0], vh_ref[0])

        def rest(n, carry):
            prv = pl.ds(pl.multiple_of((n - 1) * B_BLK, B_BLK), B_BLK)
            block(n, k_ref[0, prv, :], v_ref[0, prv, :])
            return carry

        lax.fori_loop(1, nbt, rest, 0)

    qs = pl.BlockSpec((G, tq, B_HD), lambda kv, i: (kv, i, 0))
    ks = pl.BlockSpec((1, tq, B_HD), lambda kv, i: (kv, i, 0))
    halo = pl.BlockSpec((1, B_BLK, B_HD), lambda kv, i: (kv, jnp.maximum(i * nbt - 1, 0), 0))
    return pl.pallas_call(
        body, grid=(B_KV, T // tq),
        in_specs=[qs, ks, ks, halo, halo, pl.BlockSpec((G, 1, 128), lambda kv, i: (kv, 0, 0))],
        out_specs=[qs, pl.BlockSpec((G, tq, 1), lambda kv, i: (kv, i, 0))],
        out_shape=[jax.ShapeDtypeStruct((B_HEADS, T, B_HD), BF16), jax.ShapeDtypeStruct((B_HEADS, T, 1), F32)],
        name=name, compiler_params=_params(2))(q, k, v, k, v, sinks)


def _swa_bwd(name, q, k, v, sinks, o, lse, do):
    T = q.shape[1]
    tq = min(T, 1024)
    nbt, ni = tq // B_BLK, T // tq
    scale = B_HD ** -0.5
    G = B_HEADS // B_KV

    def body(q_ref, k_ref, v_ref, kh_ref, vh_ref, s_ref, o_ref, l_ref, do_ref, dq_ref, dk_ref, dv_ref, ds_ref,
             dk_halo, dv_halo):
        step = pl.program_id(1)
        first_blk = (ni - 1 - step) * nbt
        last = pl.ds(tq - B_BLK, B_BLK)
        dk_ref[...] = jnp.zeros_like(dk_ref)
        dv_ref[...] = jnp.zeros_like(dv_ref)

        @pl.when(step > 0)
        def _():
            dk_ref[0, last, :] = dk_halo[...]
            dv_ref[0, last, :] = dv_halo[...]

        def block(n, kp, vp, dsinks):
            m_prev, m_cur = _swa_masks(first_blk + n)
            cur = pl.ds(pl.multiple_of(n * B_BLK, B_BLK), B_BLK)
            kc, vc = k_ref[0, cur, :], v_ref[0, cur, :]
            dk_p = dk_c = dv_p = dv_c = jnp.zeros((B_BLK, B_HD), F32)
            out = []
            for g in range(G):
                sink = s_ref[g][:, 0:1]
                qb, dob = q_ref[g, cur, :], do_ref[g, cur, :]
                lse_b = l_ref[g, cur, :]
                p_p = jnp.where(m_prev, jnp.exp(_dot(qb, kp, 1, 1) * scale - lse_b), 0.0)
                p_c = jnp.where(m_cur, jnp.exp(_dot(qb, kc, 1, 1) * scale - lse_b), 0.0)
                delta = jnp.sum(dob.astype(F32) * o_ref[g, cur, :].astype(F32), axis=1, keepdims=True)
                ds_p = p_p * (_dot(dob, vp, 1, 1) - delta)
                ds_c = p_c * (_dot(dob, vc, 1, 1) - delta)
                dq_ref[g, cur, :] = (_dotb(ds_p, kp) + _dotb(ds_c, kc)) * scale
                dk_p = dk_p + _dotb(ds_p, qb, 0, 0)
                dk_c = dk_c + _dotb(ds_c, qb, 0, 0)
                dv_p = dv_p + _dotb(p_p, dob, 0, 0)
                dv_c = dv_c + _dotb(p_c, dob, 0, 0)
                out.append(dsinks[g] - jnp.sum(jnp.exp(sink - lse_b) * delta, axis=0, keepdims=True))
            dk_ref[0, cur, :] += dk_c * scale
            dv_ref[0, cur, :] += dv_c
            return dk_p * scale, dv_p, tuple(out)

        zeros = tuple(jnp.zeros((1, 1), F32) for _ in range(G))
        dk_p, dv_p, dsinks = block(0, kh_ref[0], vh_ref[0], zeros)
        dk_halo[...] = dk_p
        dv_halo[...] = dv_p

        def rest(n, dsinks):
            prv = pl.ds(pl.multiple_of((n - 1) * B_BLK, B_BLK), B_BLK)
            dk_p, dv_p, dsinks = block(n, k_ref[0, prv, :], v_ref[0, prv, :], dsinks)
            dk_ref[0, prv, :] += dk_p
            dv_ref[0, prv, :] += dv_p
            return dsinks

        dsinks = lax.fori_loop(1, nbt, rest, dsinks)
        for g in range(G):
            row = jnp.broadcast_to(dsinks[g], (1, 128))

            @pl.when(step == 0)
            def _():
                ds_ref[g] = row

            @pl.when(step > 0)
            def _():
                ds_ref[g] += row

    rev = lambda i: ni - 1 - i
    qs = pl.BlockSpec((G, tq, B_HD), lambda kv, i: (kv, rev(i), 0))
    ks = pl.BlockSpec((1, tq, B_HD), lambda kv, i: (kv, rev(i), 0))
    halo = pl.BlockSpec((1, B_BLK, B_HD), lambda kv, i: (kv, jnp.maximum(rev(i) * nbt - 1, 0), 0))
    ss = pl.BlockSpec((G, 1, 128), lambda kv, i: (kv, 0, 0))
    return pl.pallas_call(
        body, grid=(B_KV, ni),
        in_specs=[qs, ks, ks, halo, halo, ss, qs, pl.BlockSpec((G, tq, 1), lambda kv, i: (kv, rev(i), 0)), qs],
        out_specs=[qs, ks, ks, ss],
        out_shape=[jax.ShapeDtypeStruct((B_HEADS, T, B_HD), F32), jax.ShapeDtypeStruct((B_KV, T, B_HD), F32),
                   jax.ShapeDtypeStruct((B_KV, T, B_HD), F32), jax.ShapeDtypeStruct((B_HEADS, 1, 128), F32)],
        scratch_shapes=[pltpu.VMEM((B_BLK, B_HD), F32), pltpu.VMEM((B_BLK, B_HD), F32)],
        name=name, compiler_params=_params(2))(q, k, v, k, v, sinks, o, lse, do)


def _split_heads(a, n):
    T = a.shape[0]
    return a.reshape(T, n, B_HD).transpose(1, 0, 2)


def _merge_heads(a):
    n, T, _ = a.shape
    return a.transpose(1, 0, 2).reshape(T, n * B_HD)


def _swa_mixer_fwd(x, nw, w_in, b_in, sinks, w_out, b_out):
    h = _rms_fwd("b_rms", x, nw)
    proj = _mm_plain("b_proj", h, w_in, 1, 0, BF16, tn=768, bias=b_in)
    q, k, v = _split_heads(proj[:, :1024], B_HEADS), _split_heads(proj[:, 1024:1280], B_KV), _split_heads(proj[:, 1280:], B_KV)
    o, lse = _swa_fwd("b_attn", q, k, v, sinks)
    om = _merge_heads(o)
    y = _mm_residual("b_out", om, w_out, x, 1.0, bias=b_out)
    return y, (h, q, k, v, o, lse, om)


def _swa_mixer_bwd(dy, dyb, x, nw, w_in, sinks, w_out, saved):
    h, q, k, v, o, lse, om = saved
    d_om = _mm_plain("b_dout", dyb, w_out, 1, 1, BF16)
    d_wout = _mm_plain("b_dwout", om, dyb, 0, 0, F32)
    d_bout = _colsum("b_dbout", dy)
    dq, dk, dv, dsinks = _swa_bwd("b_dattn", q, k, v, sinks, o, lse, _split_heads(d_om, B_HEADS))
    dproj = jnp.concatenate([_merge_heads(dq), _merge_heads(dk), _merge_heads(dv)], axis=1)
    d_bin = _colsum("b_dbin", dproj)
    d_win = _mm_plain("b_dwin", h, dproj, 0, 0, F32, tn=768)
    dx, dxb, d_nw = _mm_rms_bwd("b_dx", dproj, w_in, x, dy, nw, tk=768)
    return dx, dxb, d_nw, d_win, d_bin, dsinks[:, 0, 0], d_wout, d_bout


def _loss_head(name, x, tgt, fw):
    T, D = x.shape
    tt = min(T, 512)

    def body(x_ref, t_ref, w_ref, dx_ref, dxb_ref, loss_ref, dw_ref):
        xv = x_ref[...]
        r = lax.rsqrt(jnp.mean(xv * xv, axis=-1, keepdims=True) + EPS)
        xhat = xv * r
        diff = xhat * w_ref[...] - t_ref[...]
        part = 0.5 * jnp.sum(jnp.mean(diff * diff, axis=-1, keepdims=True), axis=0, keepdims=True)
        dyv = diff * (1.0 / D)
        dxhat = dyv * w_ref[...]
        dx = r * (dxhat - xhat * jnp.mean(dxhat * xhat, axis=-1, keepdims=True))
        dx_ref[...] = dx
        dxb_ref[...] = dx.astype(BF16)
        dw = jnp.sum(dyv * xhat, axis=0, keepdims=True)
        lp = jnp.broadcast_to(part, (1, 128))

        @pl.when(pl.program_id(0) == 0)
        def _():
            loss_ref[...] = lp
            dw_ref[...] = dw

        @pl.when(pl.program_id(0) > 0)
        def _():
            loss_ref[...] += lp
            dw_ref[...] += dw

    row = pl.BlockSpec((tt, D), lambda i: (i, 0))
    return pl.pallas_call(
        body, grid=(T // tt,), in_specs=[row, row, pl.BlockSpec((1, D), lambda i: (0, 0))],
        out_specs=[row, row, pl.BlockSpec((1, 128), lambda i: (0, 0)), pl.BlockSpec((1, D), lambda i: (0, 0))],
        out_shape=[jax.ShapeDtypeStruct((T, D), F32), jax.ShapeDtypeStruct((T, D), BF16),
                   jax.ShapeDtypeStruct((1, 128), F32), jax.ShapeDtypeStruct((1, D), F32)],
        name=name, compiler_params=_params(1))(x, tgt, fw)


def _local_step(x, tgt, wts):
    W = wts
    g = {}
    x1, sv1 = _ffn_fwd("f10", x, W["ffn1_norm"][0:1], W["ffn1_w_gu"][0], W["ffn1_w_down"][0])
    x2, sva = _gdn_fwd(x1, W["mix_norm"][0:1], W["a_w_in"], W["a_w_conv"], W["a_gate_p"], W["a_out_norm"], W["a_w_out"])
    x3, sv3 = _ffn_fwd("f20", x2, W["ffn2_norm"][0:1], W["ffn2_w_gu"][0], W["ffn2_w_down"][0])
    x4, sv4 = _ffn_fwd("f11", x3, W["ffn1_norm"][1:2], W["ffn1_w_gu"][1], W["ffn1_w_down"][1])
    x5, svb = _swa_mixer_fwd(x4, W["mix_norm"][1:2], W["b_w_in"], W["b_b_in"], W["b_sinks"], W["b_w_out"], W["b_b_out"])
    x6, sv6 = _ffn_fwd("f21", x5, W["ffn2_norm"][1:2], W["ffn2_w_gu"][1], W["ffn2_w_down"][1])
    dx, dxb, loss_p, g["final_norm"] = _loss_head("loss_head", x6, tgt, W["final_norm"])

    dx, dxb, n21, gu21, wd21 = _ffn_bwd("f21", dx, dxb, x5, W["ffn2_norm"][1:2], W["ffn2_w_gu"][1], W["ffn2_w_down"][1], sv6)
    dx, dxb, nb, g["b_w_in"], g["b_b_in"], g["b_sinks"], g["b_w_out"], g["b_b_out"] = _swa_mixer_bwd(
        dx, dxb, x4, W["mix_norm"][1:2], W["b_w_in"], W["b_sinks"], W["b_w_out"], svb)
    dx, dxb, n11, gu11, wd11 = _ffn_bwd("f11", dx, dxb, x3, W["ffn1_norm"][1:2], W["ffn1_w_gu"][1], W["ffn1_w_down"][1], sv4)
    dx, dxb, n20, gu20, wd20 = _ffn_bwd("f20", dx, dxb, x2, W["ffn2_norm"][0:1], W["ffn2_w_gu"][0], W["ffn2_w_down"][0], sv3)
    dx, dxb, na, g["a_w_in"], g["a_w_conv"], g["a_gate_p"], g["a_out_norm"], g["a_w_out"] = _gdn_bwd(
        dx, dxb, x1, W["mix_norm"][0:1], W["a_w_in"], W["a_w_conv"], W["a_gate_p"], W["a_out_norm"], W["a_w_out"], sva)
    dx, dxb, n10, gu10, wd10 = _ffn_bwd("f10", dx, dxb, x, W["ffn1_norm"][0:1], W["ffn1_w_gu"][0], W["ffn1_w_down"][0], sv1)

    g["ffn1_norm"] = jnp.concatenate([n10, n11], axis=0)
    g["ffn2_norm"] = jnp.concatenate([n20, n21], axis=0)
    g["mix_norm"] = jnp.concatenate([na, nb], axis=0)
    g["ffn1_w_gu"] = jnp.stack([gu10, gu11])
    g["ffn2_w_gu"] = jnp.stack([gu20, gu21])
    g["ffn1_w_down"] = jnp.stack([wd10, wd11])
    g["ffn2_w_down"] = jnp.stack([wd20, wd21])
    return loss_p, dx, g


PACK = (("ffn1_w_gu", 2816), ("ffn2_w_gu", 2816), ("ffn1_w_down", 1408), ("ffn2_w_down", 1408),
        ("a_w_in", 1028), ("a_w_out", 256), ("b_w_in", 384), ("b_w_out", 256))
PACK_TILE = 16
PACK_USED = sum(-(-n // PACK_TILE) * PACK_TILE for _, n in PACK)
PACK_ROWS = 10400
assert PACK_USED <= PACK_ROWS
HALF_ROWS = PACK_ROWS // 2
SMALL_SHARD = (8, 512)
HBM = pl.BlockSpec(memory_space=pl.ANY)


def _mesh_pos():
    x, y, c = lax.axis_index("x"), lax.axis_index("y"), lax.axis_index("c")
    return x, y, c, [(1 - x, y), (x, 1 - y), (1 - x, 1 - y)]


def _half(c):
    return pl.ds(pl.multiple_of(c * HALF_ROWS, 16), HALF_ROWS)


MOVE_ROWS = 1040
SUM_ROWS = 400
assert HALF_ROWS % MOVE_ROWS == 0 and HALF_ROWS % SUM_ROWS == 0


def _gather_chips(big, small):
    def body(big_ref, small_ref, rb_ref, rs_ref, send_sems, recv_sems):
        x, y, c, chips = _mesh_pos()
        send = []
        for j, chip in enumerate(chips):
            send.append(pltpu.make_async_remote_copy(src_ref=big_ref.at[_half(c)], dst_ref=rb_ref.at[j],
                                                     send_sem=send_sems.at[j], recv_sem=recv_sems.at[j],
                                                     device_id=(*chip, c), device_id_type=MESH))
            send.append(pltpu.make_async_remote_copy(src_ref=small_ref, dst_ref=rs_ref.at[j],
                                                     send_sem=send_sems.at[3 + j], recv_sem=recv_sems.at[3 + j],
                                                     device_id=(*chip, c), device_id_type=MESH))
        for cp in send:
            cp.start()
        for cp in send:
            cp.wait_recv()
        for cp in send:
            cp.wait_send()

    return pl.pallas_call(
        body, name="gather_chips", in_specs=[HBM, HBM], out_specs=[HBM, HBM],
        out_shape=[jax.ShapeDtypeStruct((3, HALF_ROWS, 1024), BF16), jax.ShapeDtypeStruct((3,) + SMALL_SHARD, F32)],
        scratch_shapes=[pltpu.SemaphoreType.DMA((6,)), pltpu.SemaphoreType.DMA((6,))])(big, small)


def _gather_fill(big, recv):
    nt = HALF_ROWS // MOVE_ROWS
    own_tiles = PACK_ROWS // MOVE_ROWS
    assert own_tiles <= 3 * nt

    def body(recv_ref, big_ref, g_ref, send_sem, recv_sem, local_sems):
        x, y, c, chips = _mesh_pos()
        j, t = pl.program_id(0), pl.program_id(1)
        step = j * nt + t
        src_chip = jnp.where(j == 0, 2 * (1 - x) + y, jnp.where(j == 1, 2 * x + 1 - y, 2 * (1 - x) + 1 - y))
        rows = pl.ds(pl.multiple_of(c * HALF_ROWS + t * MOVE_ROWS, 16), MOVE_ROWS)
        keep = pltpu.make_async_copy(recv_ref.at[0], g_ref.at[src_chip, rows], local_sems.at[0])
        give = pltpu.make_async_remote_copy(src_ref=recv_ref.at[0], dst_ref=g_ref.at[src_chip, rows],
                                            send_sem=send_sem, recv_sem=recv_sem,
                                            device_id=(x, y, 1 - c), device_id_type=MESH)
        keep.start()
        give.start()

        @pl.when(step < own_tiles)
        def _():
            own_rows = pl.ds(pl.multiple_of(step * MOVE_ROWS, 16), MOVE_ROWS)
            own = pltpu.make_async_copy(big_ref, g_ref.at[2 * x + y, own_rows], local_sems.at[1])
            own.start()
            own.wait()

        give.wait_send()
        keep.wait()

        @pl.when(step == 3 * nt - 1)
        def _():
            landed = g_ref.at[pl.ds(0, 3), pl.ds(0, HALF_ROWS)]
            pltpu.make_async_remote_copy(src_ref=landed, dst_ref=landed, send_sem=send_sem, recv_sem=recv_sem,
                                         device_id=(x, y, c), device_id_type=MESH).wait_recv()

    return pl.pallas_call(
        body, grid=(3, nt),
        in_specs=[pl.BlockSpec((1, MOVE_ROWS, 1024), lambda j, t: (j, t, 0)),
                  pl.BlockSpec((MOVE_ROWS, 1024), lambda j, t: (jnp.minimum(j * nt + t, own_tiles - 1), 0))],
        out_specs=HBM, out_shape=jax.ShapeDtypeStruct((4, PACK_ROWS, 1024), BF16),
        scratch_shapes=[pltpu.SemaphoreType.DMA, pltpu.SemaphoreType.DMA, pltpu.SemaphoreType.DMA((2,))],
        name="gather_fill", compiler_params=_params(2))(recv, big)


def _pair_send(p):
    nt = HALF_ROWS // MOVE_ROWS

    def body(p_ref, a_ref, send_sem, recv_sem):
        x, y, c, _ = _mesh_pos()
        s, t = pl.program_id(0), pl.program_id(1)
        rows = pl.ds(pl.multiple_of(t * MOVE_ROWS, 16), MOVE_ROWS)
        give = pltpu.make_async_remote_copy(src_ref=p_ref.at[0], dst_ref=a_ref.at[s, rows], send_sem=send_sem,
                                            recv_sem=recv_sem, device_id=(x, y, 1 - c), device_id_type=MESH)
        give.start()
        give.wait_send()

        @pl.when((s == 3) & (t == nt - 1))
        def _():
            pltpu.make_async_remote_copy(src_ref=a_ref, dst_ref=a_ref, send_sem=send_sem, recv_sem=recv_sem,
                                         device_id=(x, y, c), device_id_type=MESH).wait_recv()

    return pl.pallas_call(
        body, grid=(4, nt),
        in_specs=[pl.BlockSpec((1, MOVE_ROWS, 1024), lambda s, t: (s, (1 - lax.axis_index("c")) * nt + t, 0))],
        out_specs=HBM, out_shape=jax.ShapeDtypeStruct((4, HALF_ROWS, 1024), BF16),
        scratch_shapes=[pltpu.SemaphoreType.DMA, pltpu.SemaphoreType.DMA],
        name="pair_send", compiler_params=_params(2))(p)


def _pair_sum(p, a):
    nt = HALF_ROWS // SUM_ROWS

    def body(p_ref, a_ref, o_ref):
        o_ref[...] = (p_ref[...].astype(F32) + a_ref[...].astype(F32)).astype(BF16)

    spec = pl.BlockSpec((1, SUM_ROWS, 1024), lambda s, t: (s, t, 0))
    return pl.pallas_call(
        body, grid=(4, nt),
        in_specs=[pl.BlockSpec((1, SUM_ROWS, 1024), lambda s, t: (s, lax.axis_index("c") * nt + t, 0)), spec],
        out_specs=spec, out_shape=jax.ShapeDtypeStruct((4, HALF_ROWS, 1024), BF16),
        name="pair_sum", compiler_params=_params(2))(p, a)


def _chip_exchange(cs):
    def body(c_ref, b_ref, send_sems, recv_sems):
        x, y, c, chips = _mesh_pos()
        send = [pltpu.make_async_remote_copy(src_ref=c_ref.at[2 * chip[0] + chip[1]], dst_ref=b_ref.at[j],
                                             send_sem=send_sems.at[j], recv_sem=recv_sems.at[j],
                                             device_id=(*chip, c), device_id_type=MESH)
                for j, chip in enumerate(chips)]
        for cp in send:
            cp.start()
        for cp in send:
            cp.wait_recv()
        for cp in send:
            cp.wait_send()

    return pl.pallas_call(
        body, name="chip_exchange", in_specs=[HBM], out_specs=HBM,
        out_shape=jax.ShapeDtypeStruct((3, HALF_ROWS, 1024), BF16),
        scratch_shapes=[pltpu.SemaphoreType.DMA((3,)), pltpu.SemaphoreType.DMA((3,))])(cs)


def _chip_sum(cs, b):
    nt = HALF_ROWS // SUM_ROWS

    def body(c_ref, b_ref, r_ref, buf, send_sems, recv_sem, local_sems):
        x, y, c, _ = _mesh_pos()
        t = pl.program_id(0)
        slot = lax.rem(t, 2)

        def copies(k, tile):
            rows = pl.ds(pl.multiple_of(c * HALF_ROWS + tile * SUM_ROWS, 8), SUM_ROWS)
            keep = pltpu.make_async_copy(buf.at[k], r_ref.at[rows], local_sems.at[k])
            give = pltpu.make_async_remote_copy(src_ref=buf.at[k], dst_ref=r_ref.at[rows], send_sem=send_sems.at[k],
                                                recv_sem=recv_sem, device_id=(x, y, 1 - c), device_id_type=MESH)
            return keep, give

        @pl.when(t >= 2)
        def _():
            keep, give = copies(slot, t - 2)
            keep.wait()
            give.wait_send()

        buf[slot] = (c_ref[0].astype(F32) + b_ref[0].astype(F32)) + (b_ref[1].astype(F32) + b_ref[2].astype(F32))
        keep, give = copies(slot, t)
        keep.start()
        give.start()

        @pl.when(t == nt - 1)
        def _():
            for back in (1, 0):
                keep, give = copies(lax.rem(t - back, 2), t - back)
                keep.wait()
                give.wait_send()
            landed = r_ref.at[_half(1 - c)]
            pltpu.make_async_remote_copy(src_ref=landed, dst_ref=landed, send_sem=send_sems.at[0], recv_sem=recv_sem,
                                         device_id=(x, y, c), device_id_type=MESH).wait_recv()

    return pl.pallas_call(
        body, grid=(nt,),
        in_specs=[pl.BlockSpec((1, SUM_ROWS, 1024), lambda t: (2 * lax.axis_index("x") + lax.axis_index("y"), t, 0)),
                  pl.BlockSpec((3, SUM_ROWS, 1024), lambda t: (0, t, 0))],
        out_specs=HBM, out_shape=jax.ShapeDtypeStruct((PACK_ROWS, 1024), F32),
        scratch_shapes=[pltpu.VMEM((2, SUM_ROWS, 1024), F32), pltpu.SemaphoreType.DMA((2,)), pltpu.SemaphoreType.DMA,
                        pltpu.SemaphoreType.DMA((2,))],
        name="chip_sum", compiler_params=_params(1))(cs, b)


def _reduce_scatter(p):
    cs = _pair_sum(p, _pair_send(p))
    return _chip_sum(cs, _chip_exchange(cs))


SMALL_ROWS = 24


def _all_reduce_small(v):
    def body(v_ref, o_ref, all_ref, send_sems, recv_sems):
        x, y, c, _ = _mesh_pos()
        me = 4 * x + 2 * y + c
        all_ref[me] = v_ref[...]
        peers = [(x ^ ((k >> 2) & 1), y ^ ((k >> 1) & 1), c ^ (k & 1)) for k in range(1, 8)]
        idx = lambda p: 4 * p[0] + 2 * p[1] + p[2]
        send = [pltpu.make_async_remote_copy(src_ref=v_ref, dst_ref=all_ref.at[me], send_sem=send_sems.at[k],
                                             recv_sem=recv_sems.at[k], device_id=p, device_id_type=MESH)
                for k, p in enumerate(peers)]
        for cp in send:
            cp.start()
        for k, p in enumerate(peers):
            pltpu.make_async_remote_copy(src_ref=v_ref, dst_ref=all_ref.at[idx(p)], send_sem=send_sems.at[k],
                                         recv_sem=recv_sems.at[k], device_id=p, device_id_type=MESH).wait_recv()
        for cp in send:
            cp.wait_send()
        acc = all_ref[0]
        for d in range(1, 8):
            acc = acc + all_ref[d]
        o_ref[...] = acc

    vm = pl.BlockSpec(memory_space=pltpu.VMEM)
    return pl.pallas_call(
        body, name="all_reduce_small", in_specs=[vm], out_specs=vm,
        out_shape=jax.ShapeDtypeStruct((SMALL_ROWS, 1024), F32),
        scratch_shapes=[pltpu.VMEM((8, SMALL_ROWS, 1024), F32), pltpu.SemaphoreType.DMA((7,)),
                        pltpu.SemaphoreType.DMA((7,))],)(v)


def _adamw(name, w, g, m, v):
    rows, cols = w.shape
    tr = rows
    if rows * cols > 400_000:
        tr = max(t for t in range(8, rows, 8) if rows % t == 0 and t * cols <= 400_000)

    def body(w_ref, g_ref, m_ref, v_ref, d_ref, nm_ref, nv_ref):
        gv = g_ref[...]
        m_new = ADAM_B1 * m_ref[...] + (1.0 - ADAM_B1) * gv
        v_new = ADAM_B2 * v_ref[...] + (1.0 - ADAM_B2) * (gv * gv)
        m_hat = m_new / (1.0 - ADAM_B1 ** ADAM_STEP)
        v_hat = v_new / (1.0 - ADAM_B2 ** ADAM_STEP)
        d_ref[...] = -ADAM_LR * (m_hat / (jnp.sqrt(v_hat) + ADAM_EPS) + ADAM_WD * w_ref[...])
        nm_ref[...] = m_new
        nv_ref[...] = v_new

    spec = pl.BlockSpec((tr, cols), lambda i: (i, 0))
    sds = jax.ShapeDtypeStruct((rows, cols), F32)
    return pl.pallas_call(body, grid=(rows // tr,), in_specs=[spec] * 4, out_specs=[spec] * 3, out_shape=[sds] * 3,
                          name=name, compiler_params=_params(1))(w, g, m, v)


WEIGHTS = ("ffn1_norm", "ffn1_w_gu", "ffn1_w_down", "mix_norm", "ffn2_norm", "ffn2_w_gu", "ffn2_w_down",
           "a_w_in", "a_w_conv", "a_A_log", "a_dt_bias", "a_out_norm", "a_w_out",
           "b_w_in", "b_b_in", "b_sinks", "b_w_out", "b_b_out", "final_norm")
SMALL_SLOTS = {"ffn1_norm": (0, 2048), "mix_norm": (2048, 2048), "ffn2_norm": (4096, 2048), "final_norm": (6144, 1024),
               "a_A_log": (7168, 8), "a_dt_bias": (7296, 8), "a_out_norm": (7424, 128), "b_sinks": (7552, 16),
               "loss": (7680, 1)}
SMALL_SHARDED = {"a_w_conv": (8192, 8192, (4,), 3072), "b_b_in": (20480, 11264, (), 1536), "b_b_out": (22016, 11648, (), 1024)}
DEV_SMALL_ROWS = 12


def _pack_rows(parts):
    rows = []
    for p in parts:
        r = p.reshape(p.shape[0], -1, 1024)
        rows.append(jnp.pad(r, ((0, 0), (0, -r.shape[1] % PACK_TILE), (0, 0))))
    rows.append(jnp.zeros((parts[0].shape[0], PACK_ROWS - PACK_USED, 1024), parts[0].dtype))
    return jnp.concatenate(rows, axis=1)


def _place(vec, off, a):
    return lax.dynamic_update_slice(vec, a.reshape(-1).astype(F32), (off,))


def kernel(x, ffn1_norm, ffn1_w_gu, ffn1_w_down, mix_norm, ffn2_norm, ffn2_w_gu, ffn2_w_down, a_w_in, a_w_conv, a_A_log, a_dt_bias, a_out_norm, a_w_out, b_w_in, b_b_in, b_sinks, b_w_out, b_b_out, final_norm, loss_target, m_ffn1_norm, m_ffn1_w_gu, m_ffn1_w_down, m_mix_norm, m_ffn2_norm, m_ffn2_w_gu, m_ffn2_w_down, m_a_w_in, m_a_w_conv, m_a_A_log, m_a_dt_bias, m_a_out_norm, m_a_w_out, m_b_w_in, m_b_b_in, m_b_sinks, m_b_w_out, m_b_b_out, m_final_norm, v_ffn1_norm, v_ffn1_w_gu, v_ffn1_w_down, v_mix_norm, v_ffn2_norm, v_ffn2_w_gu, v_ffn2_w_down, v_a_w_in, v_a_w_conv, v_a_A_log, v_a_dt_bias, v_a_out_norm, v_a_w_out, v_b_w_in, v_b_b_in, v_b_sinks, v_b_w_out, v_b_b_out, v_final_norm):
    w = dict(zip(WEIGHTS, (ffn1_norm, ffn1_w_gu, ffn1_w_down, mix_norm, ffn2_norm, ffn2_w_gu, ffn2_w_down, a_w_in, a_w_conv,
                           a_A_log, a_dt_bias, a_out_norm, a_w_out, b_w_in, b_b_in, b_sinks, b_w_out, b_b_out, final_norm)))
    m = dict(zip(WEIGHTS, (m_ffn1_norm, m_ffn1_w_gu, m_ffn1_w_down, m_mix_norm, m_ffn2_norm, m_ffn2_w_gu, m_ffn2_w_down,
                           m_a_w_in, m_a_w_conv, m_a_A_log, m_a_dt_bias, m_a_out_norm, m_a_w_out, m_b_w_in, m_b_b_in,
                           m_b_sinks, m_b_w_out, m_b_b_out, m_final_norm)))
    v = dict(zip(WEIGHTS, (v_ffn1_norm, v_ffn1_w_gu, v_ffn1_w_down, v_mix_norm, v_ffn2_norm, v_ffn2_w_gu, v_ffn2_w_down,
                           v_a_w_in, v_a_w_conv, v_a_A_log, v_a_dt_bias, v_a_out_norm, v_a_w_out, v_b_w_in, v_b_b_in,
                           v_b_sinks, v_b_w_out, v_b_b_out, v_final_norm)))
    chip = 2 * lax.axis_index("x") + lax.axis_index("y")

    big = _pack_rows([w[n].astype(BF16).reshape(1, -1) for n, _ in PACK])[0]
    small = jnp.zeros((4096,), F32)
    small = _place(small, 0, w["a_w_conv"])
    small = _place(small, 3072, w["b_b_in"])
    small = _place(small, 3456, w["b_b_out"]).reshape(SMALL_SHARD)
    rb, rs = _gather_chips(big, small)
    gb = _gather_fill(big, rb)
    offs, o = {}, 0
    for n, r in PACK:
        offs[n] = (o, r)
        o += -(-r // PACK_TILE) * PACK_TILE
    blk = lambda n: gb[:, offs[n][0]:offs[n][0] + offs[n][1]]
    gsf = lax.dynamic_update_slice(jnp.zeros((4, 4096), F32), small.reshape(1, 4096), (chip, 0))
    for j, other in enumerate((chip ^ 2, chip ^ 1, chip ^ 3)):
        gsf = lax.dynamic_update_slice(gsf, rs[j].reshape(1, 4096), (other, 0))
    W = {n: w[n] for n in ("ffn1_norm", "ffn2_norm", "mix_norm", "a_out_norm")}
    for n in ("ffn1_w_gu", "ffn2_w_gu"):
        W[n] = blk(n).reshape(4, 2, 1024, FF_BLK).transpose(1, 0, 2, 3).reshape(2, 2, 2, 1024, FF_BLK)
    for n in ("ffn1_w_down", "ffn2_w_down"):
        W[n] = blk(n).reshape(4, 2, 704, 1024).transpose(1, 0, 2, 3).reshape(2, 2, FF_BLK, 1024)
    W["a_w_in"] = jnp.pad(blk("a_w_in").reshape(4, 1024, 1028).transpose(1, 0, 2).reshape(1024, A_IN_COLS),
                          ((0, 0), (0, A_IN_PAD - A_IN_COLS)))
    W["a_w_out"] = blk("a_w_out").reshape(1024, 1024)
    W["b_w_in"] = blk("b_w_in").reshape(4, 1024, 384).transpose(1, 0, 2).reshape(1024, 1536)
    W["b_w_out"] = blk("b_w_out").reshape(1024, 1024)
    W["a_w_conv"] = gsf[:, 0:3072].reshape(4, 4, 768).transpose(1, 0, 2).reshape(4, 3072)
    W["b_b_in"] = gsf[:, 3072:3456].reshape(1, 1536)
    W["b_b_out"] = gsf[:, 3456:3712].reshape(1, 1024)
    W["a_gate_p"] = jnp.pad(jnp.concatenate([-jnp.exp(w["a_A_log"]), w["a_dt_bias"]], axis=0), ((0, 0), (8, 112)))
    W["b_sinks"] = jnp.broadcast_to(w["b_sinks"][0][:, None, None], (B_HEADS, 1, 128))
    W["final_norm"] = w["final_norm"][None]

    loss_p, dx, g = _local_step(x[0], loss_target[0], W)

    parts = [
        g["ffn1_w_gu"].reshape(2, 4, 1024, FF_BLK).transpose(1, 0, 2, 3), g["ffn2_w_gu"].reshape(2, 4, 1024, FF_BLK).transpose(1, 0, 2, 3),
        g["ffn1_w_down"].reshape(2, 4, 704, 1024).transpose(1, 0, 2, 3), g["ffn2_w_down"].reshape(2, 4, 704, 1024).transpose(1, 0, 2, 3),
        g["a_w_in"][:, :A_IN_COLS].reshape(1024, 4, 1028).transpose(1, 0, 2), g["a_w_out"].reshape(4, 256, 1024),
        g["b_w_in"].reshape(1024, 4, 384).transpose(1, 0, 2), g["b_w_out"].reshape(4, 256, 1024)]
    red = _reduce_scatter(_pack_rows([a.astype(BF16).reshape(4, -1) for a in parts]))
    grads = {n: red[offs[n][0]:offs[n][0] + offs[n][1]].reshape(w[n].shape) for n, _ in PACK}

    sv = jnp.zeros((SMALL_ROWS * 1024,), F32)
    small_g = {"ffn1_norm": g["ffn1_norm"], "mix_norm": g["mix_norm"], "ffn2_norm": g["ffn2_norm"], "final_norm": g["final_norm"],
               "a_A_log": g["a_gate_p"][0, 8:16], "a_dt_bias": g["a_gate_p"][1, 8:16], "a_out_norm": g["a_out_norm"],
               "b_sinks": g["b_sinks"], "loss": loss_p[0, 0:1]}
    for n, (off, _) in SMALL_SLOTS.items():
        sv = _place(sv, off, small_g[n])
    for n, (off, _, _, _) in SMALL_SHARDED.items():
        sv = _place(sv, off, g[n])
    tot = _all_reduce_small(sv.reshape(SMALL_ROWS, 1024)).reshape(-1)
    for n, (off, size) in SMALL_SLOTS.items():
        if n != "loss":
            grads[n] = tot[off:off + size].reshape(w[n].shape)
    for n, (off, _, lead, last) in SMALL_SHARDED.items():
        full = tot[off:off + (lead[0] if lead else 1) * last].reshape(lead + (last,))
        width = last // 4
        grads[n] = lax.dynamic_slice_in_dim(full, chip * width, width, axis=-1).reshape(w[n].shape)
    loss = tot[SMALL_SLOTS["loss"][0]]

    delta, new_m, new_v = {}, {}, {}
    for n, _ in PACK:
        two_d = lambda a: a.reshape(-1, a.shape[-1])
        d, nm, nv = _adamw("adamw_" + n, two_d(w[n]), two_d(grads[n]), two_d(m[n]), two_d(v[n]))
        delta[n], new_m[n], new_v[n] = d.reshape(w[n].shape), nm.reshape(w[n].shape), nv.reshape(w[n].shape)

    def dev_small(src):
        vec = jnp.zeros((DEV_SMALL_ROWS * 1024,), F32)
        for n, (off, _) in SMALL_SLOTS.items():
            if n != "loss":
                vec = _place(vec, off, src[n])
        for n, (_, off, _, _) in SMALL_SHARDED.items():
            vec = _place(vec, off, src[n])
        return vec.reshape(DEV_SMALL_ROWS, 1024)

    sd, sm, svv = _adamw("adamw_small", dev_small(w), dev_small(grads), dev_small(m), dev_small(v))
    for n in WEIGHTS:
        if n in SMALL_SLOTS:
            off, size = SMALL_SLOTS[n]
        elif n in SMALL_SHARDED:
            off, size = SMALL_SHARDED[n][1], w[n].size
        else:
            continue
        for dst, src in ((delta, sd), (new_m, sm), (new_v, svv)):
            dst[n] = src.reshape(-1)[off:off + size].reshape(w[n].shape)

    return (loss, dx[None], *[grads[n] for n in WEIGHTS], *[delta[n] for n in WEIGHTS],
            *[new_m[n] for n in WEIGHTS], *[new_v[n] for n in WEIGHTS])
```

```python
import jax
import jax.numpy as jnp
from jax import lax
from jax.experimental import pallas as pl
from jax.experimental.pallas import tpu as pltpu

F32 = jnp.float32
BF16 = jnp.bfloat16

D_MODEL = 1024
EPS = 1e-6
FF_BLK = 1408
A_HEADS = 8
A_DK = 128
A_CHUNK = 64
A_HG = 4
A_IN_COLS = 4112
A_IN_PAD = 4224
B_HEADS = 16
B_KV = 4
B_HD = 64
B_BLK = 128
ADAM_LR, ADAM_B1, ADAM_B2, ADAM_EPS, ADAM_WD, ADAM_STEP = 0.001, 0.9, 0.999, 1e-08, 0.01, 10
MESH = pl.DeviceIdType.MESH
VMEM_LIMIT = 56 * 1024 * 1024


def _params(n_axes):
    return pltpu.CompilerParams(dimension_semantics=("arbitrary",) * n_axes, vmem_limit_bytes=VMEM_LIMIT)


def _sigmoid(x):
    return 1.0 / (1.0 + jnp.exp(-x))


def _dot(a, b, ca, cb):
    return lax.dot_general(a, b, (((ca,), (cb,)), ((), ())), preferred_element_type=F32)


def _dotb(a, b, ca=1, cb=0):
    return _dot(a.astype(BF16), b.astype(BF16), ca, cb)


def _dotx(a, b, ca=1, cb=0):
    return lax.dot_general(a, b, (((ca,), (cb,)), ((), ())), preferred_element_type=F32,
                           precision=lax.Precision.HIGHEST)


def _doth(a, b, ca=1, cb=0):
    return lax.dot_general(a, b, (((ca,), (cb,)), ((), ())), preferred_element_type=F32,
                           precision=lax.Precision.HIGH)


def _rms_fwd(name, x, w):
    T, D = x.shape
    tt = min(T, 512)

    def body(x_ref, w_ref, h_ref):
        xv = x_ref[...]
        r = lax.rsqrt(jnp.mean(xv * xv, axis=-1, keepdims=True) + EPS)
        h_ref[...] = (xv * r * w_ref[...]).astype(BF16)

    return pl.pallas_call(
        body, grid=(T // tt,),
        in_specs=[pl.BlockSpec((tt, D), lambda i: (i, 0)), pl.BlockSpec((1, D), lambda i: (0, 0))],
        out_specs=pl.BlockSpec((tt, D), lambda i: (i, 0)),
        out_shape=jax.ShapeDtypeStruct((T, D), BF16), name=name, compiler_params=_params(1))(x, w)


def _rms_bwd_tile(dh, xv, dy, w):
    r = lax.rsqrt(jnp.mean(xv * xv, axis=-1, keepdims=True) + EPS)
    xhat = xv * r
    dxhat = dh * w
    dx = dy + r * (dxhat - xhat * jnp.mean(dxhat * xhat, axis=-1, keepdims=True))
    return dx, jnp.sum(dh * xhat, axis=0, keepdims=True)


def _colsum(name, a):
    T, N = a.shape
    tt = min(T, 512)

    def body(a_ref, o_ref):
        @pl.when(pl.program_id(0) == 0)
        def _():
            o_ref[...] = jnp.zeros_like(o_ref)
        o_ref[...] += jnp.sum(a_ref[...].astype(F32), axis=0, keepdims=True)

    return pl.pallas_call(
        body, grid=(T // tt,), in_specs=[pl.BlockSpec((tt, N), lambda i: (i, 0))],
        out_specs=pl.BlockSpec((1, N), lambda i: (0, 0)),
        out_shape=jax.ShapeDtypeStruct((1, N), F32), name=name, compiler_params=_params(1))(a)


def _matmul(name, a, b, ca, cb, tm, tn, tk, extra_in, outs, epi, order="ji"):
    M, K, N = a.shape[1 - ca], a.shape[ca], b.shape[1 - cb]
    tm, tn, tk = min(tm, M), min(tn, N), min(tk, K)
    assert M % tm == 0 and N % tn == 0 and K % tk == 0, (name, M, N, K, tm, tn, tk)
    ni, nj, nk = M // tm, N // tn, K // tk
    if order == "ji":
        grid = (nj, ni, nk)
        perm = lambda g0, g1, g2: (g1, g0, g2)
    else:
        grid = (ni, nj, nk)
        perm = lambda g0, g1, g2: (g0, g1, g2)

    def wrap(f):
        return lambda g0, g1, g2: f(*perm(g0, g1, g2))

    a_spec = (pl.BlockSpec((tm, tk), wrap(lambda i, j, k: (i, k))) if ca == 1
              else pl.BlockSpec((tk, tm), wrap(lambda i, j, k: (k, i))))
    b_spec = (pl.BlockSpec((tk, tn), wrap(lambda i, j, k: (k, j))) if cb == 0
              else pl.BlockSpec((tn, tk), wrap(lambda i, j, k: (j, k))))
    ne, no = len(extra_in), len(outs)

    def body(*refs):
        a_ref, b_ref = refs[0], refs[1]
        ex, out = refs[2:2 + ne], refs[2 + ne:2 + ne + no]
        i, j, k = perm(pl.program_id(0), pl.program_id(1), pl.program_id(2))
        p = _dotb(a_ref[...], b_ref[...], ca, cb)
        if nk == 1:
            epi(p, ex, out, i, j)
        else:
            acc_ref = refs[-1]

            @pl.when(k == 0)
            def _():
                acc_ref[...] = p

            @pl.when(k > 0)
            def _():
                acc_ref[...] += p

            @pl.when(k == nk - 1)
            def _():
                epi(acc_ref[...], ex, out, i, j)

    return pl.pallas_call(
        body, grid=grid,
        in_specs=[a_spec, b_spec] + [pl.BlockSpec(bs, wrap(f)) for _, bs, f in extra_in],
        out_specs=[pl.BlockSpec(bs, wrap(f)) for _, bs, f in outs],
        out_shape=[s for s, _, _ in outs],
        scratch_shapes=[pltpu.VMEM((tm, tn), F32)] if nk > 1 else [],
        name=name, compiler_params=_params(3))(a, b, *[x for x, _, _ in extra_in])


def _mm_plain(name, a, b, ca, cb, out_dtype, tm=1024, tn=1024, tk=1024, scale=1.0, bias=None):
    M, N = a.shape[1 - ca], b.shape[1 - cb]
    tm, tn = min(tm, M), min(tn, N)
    extra = [] if bias is None else [(bias, (1, tn), lambda i, j, k: (0, j))]

    def epi(acc, ex, out, i, j):
        r = acc * scale if scale != 1.0 else acc
        if bias is not None:
            r = r + ex[0][...]
        out[0][...] = r.astype(out_dtype)

    return _matmul(name, a, b, ca, cb, tm, tn, tk, extra,
                   [(jax.ShapeDtypeStruct((M, N), out_dtype), (tm, tn), lambda i, j, k: (i, j))], epi)[0]


def _mm_residual(name, a, b, x, scale, bias=None, tk=1024):
    M, N = x.shape
    tm, tn = min(512, M), N
    extra = [(x, (tm, tn), lambda i, j, k: (i, j))]
    if bias is not None:
        extra.append((bias, (1, tn), lambda i, j, k: (0, j)))

    def epi(acc, ex, out, i, j):
        r = acc if bias is None else acc + ex[1][...]
        out[0][...] = ex[0][...] + scale * r

    return _matmul(name, a, b, 1, 0, tm, tn, tk, extra,
                   [(jax.ShapeDtypeStruct((M, N), F32), (tm, tn), lambda i, j, k: (i, j))], epi, order="ij")[0]


def _mm_rms_bwd(name, dproj, w_in, x, dy, nw, tk):
    M, N = x.shape
    tm = min(512, M)
    extra = [(x, (tm, N), lambda i, j, k: (i, 0)), (dy, (tm, N), lambda i, j, k: (i, 0)),
             (nw, (1, N), lambda i, j, k: (0, 0))]

    def epi(acc, ex, out, i, j):
        dx, dw = _rms_bwd_tile(acc, ex[0][...], ex[1][...], ex[2][...])
        out[0][...] = dx
        out[1][...] = dx.astype(BF16)

        @pl.when(i == 0)
        def _():
            out[2][...] = dw

        @pl.when(i > 0)
        def _():
            out[2][...] += dw

    return _matmul(name, dproj, w_in, 1, 1, tm, N, tk, extra,
                   [(jax.ShapeDtypeStruct((M, N), F32), (tm, N), lambda i, j, k: (i, 0)),
                    (jax.ShapeDtypeStruct((M, N), BF16), (tm, N), lambda i, j, k: (i, 0)),
                    (jax.ShapeDtypeStruct((1, N), F32), (1, N), lambda i, j, k: (0, 0))], epi, order="ij")


def _ffn_gu(name, h, wgu):
    T, D = h.shape
    tm = min(T, 1024)
    rs = min(tm, 256)

    def body(h_ref, w_ref, gu_ref, act_ref):
        for r in range(tm // rs):
            rows = pl.ds(r * rs, rs)
            hv = h_ref[rows, :]
            g = _dot(hv, w_ref[0, 0], 1, 0)
            u = _dot(hv, w_ref[1, 0], 1, 0)
            gu_ref[0, 0, rows, :] = g.astype(BF16)
            gu_ref[1, 0, rows, :] = u.astype(BF16)
            act_ref[0, rows, :] = (g * _sigmoid(g) * u).astype(BF16)

    return pl.pallas_call(
        body, grid=(2, T // tm),
        in_specs=[pl.BlockSpec((tm, D), lambda j, i: (i, 0)),
                  pl.BlockSpec((2, 1, D, FF_BLK), lambda j, i: (0, j, 0, 0))],
        out_specs=[pl.BlockSpec((2, 1, tm, FF_BLK), lambda j, i: (0, j, i, 0)),
                   pl.BlockSpec((1, tm, FF_BLK), lambda j, i: (j, i, 0))],
        out_shape=[jax.ShapeDtypeStruct((2, 2, T, FF_BLK), BF16), jax.ShapeDtypeStruct((2, T, FF_BLK), BF16)],
        name=name, compiler_params=_params(2))(h, wgu)


def _ffn_down(name, act, wd, x):
    T, D = x.shape
    tm = min(T, 512)

    def body(act_ref, wd_ref, x_ref, o_ref):
        acc = _dot(act_ref[0], wd_ref[0], 1, 0) + _dot(act_ref[1], wd_ref[1], 1, 0)
        o_ref[...] = x_ref[...] + 0.5 * acc

    return pl.pallas_call(
        body, grid=(T // tm,),
        in_specs=[pl.BlockSpec((2, tm, FF_BLK), lambda i: (0, i, 0)),
                  pl.BlockSpec((2, FF_BLK, D), lambda i: (0, 0, 0)),
                  pl.BlockSpec((tm, D), lambda i: (i, 0))],
        out_specs=pl.BlockSpec((tm, D), lambda i: (i, 0)),
        out_shape=jax.ShapeDtypeStruct((T, D), F32), name=name, compiler_params=_params(1))(act, wd, x)


def _ffn_dact(name, dyb, wd, gu):
    T, D = dyb.shape
    tm = min(T, 1024)
    rs = min(tm, 256)

    def body(dy_ref, wd_ref, gu_ref, o_ref):
        for r in range(tm // rs):
            rows = pl.ds(r * rs, rs)
            dact = 0.5 * _dot(dy_ref[rows, :], wd_ref[0], 1, 1)
            g = gu_ref[0, 0, rows, :].astype(F32)
            u = gu_ref[1, 0, rows, :].astype(F32)
            s = _sigmoid(g)
            o_ref[0, 0, rows, :] = (dact * u * (s + g * s * (1.0 - s))).astype(BF16)
            o_ref[1, 0, rows, :] = (dact * g * s).astype(BF16)

    return pl.pallas_call(
        body, grid=(2, T // tm),
        in_specs=[pl.BlockSpec((tm, D), lambda j, i: (i, 0)),
                  pl.BlockSpec((1, FF_BLK, D), lambda j, i: (j, 0, 0)),
                  pl.BlockSpec((2, 1, tm, FF_BLK), lambda j, i: (0, j, i, 0))],
        out_specs=pl.BlockSpec((2, 1, tm, FF_BLK), lambda j, i: (0, j, i, 0)),
        out_shape=jax.ShapeDtypeStruct((2, 2, T, FF_BLK), BF16), name=name, compiler_params=_params(2))(dyb, wd, gu)


def _ffn_dwd(name, act, dyb):
    _, T, _ = act.shape
    D = dyb.shape[1]
    tk = min(T, 1024)

    def body(a_ref, d_ref, o_ref):
        p = 0.5 * _dot(a_ref[0], d_ref[...], 0, 0)

        @pl.when(pl.program_id(1) == 0)
        def _():
            o_ref[0] = p

        @pl.when(pl.program_id(1) > 0)
        def _():
            o_ref[0] += p

    return pl.pallas_call(
        body, grid=(2, T // tk),
        in_specs=[pl.BlockSpec((1, tk, FF_BLK), lambda j, k: (j, k, 0)), pl.BlockSpec((tk, D), lambda j, k: (k, 0))],
        out_specs=pl.BlockSpec((1, FF_BLK, D), lambda j, k: (j, 0, 0)),
        out_shape=jax.ShapeDtypeStruct((2, FF_BLK, D), F32), name=name, compiler_params=_params(2))(act, dyb)


def _ffn_dwgu(name, h, dgu):
    T, D = h.shape
    tk = min(T, 1024)

    def body(h_ref, d_ref, o_ref):
        p = _dot(h_ref[...], d_ref[0, 0], 0, 0)

        @pl.when(pl.program_id(1) == 0)
        def _():
            o_ref[0, 0] = p

        @pl.when(pl.program_id(1) > 0)
        def _():
            o_ref[0, 0] += p

    return pl.pallas_call(
        body, grid=(4, T // tk),
        in_specs=[pl.BlockSpec((tk, D), lambda q, k: (k, 0)),
                  pl.BlockSpec((1, 1, tk, FF_BLK), lambda q, k: (q // 2, q % 2, k, 0))],
        out_specs=pl.BlockSpec((1, 1, D, FF_BLK), lambda q, k: (q // 2, q % 2, 0, 0)),
        out_shape=jax.ShapeDtypeStruct((2, 2, D, FF_BLK), F32), name=name, compiler_params=_params(2))(h, dgu)


def _ffn_dx(name, dgu, wgu, x, dy, nw):
    T, D = x.shape
    tm = min(T, 512)

    def body(d_ref, w_ref, x_ref, dy_ref, nw_ref, dx_ref, dxb_ref, dnw_ref, acc_ref):
        i, k = pl.program_id(0), pl.program_id(1)
        p = _dot(d_ref[0, 0], w_ref[0, 0], 1, 1)

        @pl.when(k == 0)
        def _():
            acc_ref[...] = p

        @pl.when(k > 0)
        def _():
            acc_ref[...] += p

        @pl.when(k == 3)
        def _():
            dx, dw = _rms_bwd_tile(acc_ref[...], x_ref[...], dy_ref[...], nw_ref[...])
            dx_ref[...] = dx
            dxb_ref[...] = dx.astype(BF16)

            @pl.when(i == 0)
            def _():
                dnw_ref[...] = dw

            @pl.when(i > 0)
            def _():
                dnw_ref[...] += dw

    return pl.pallas_call(
        body, grid=(T // tm, 4),
        in_specs=[pl.BlockSpec((1, 1, tm, FF_BLK), lambda i, k: (k // 2, k % 2, i, 0)),
                  pl.BlockSpec((1, 1, D, FF_BLK), lambda i, k: (k // 2, k % 2, 0, 0)),
                  pl.BlockSpec((tm, D), lambda i, k: (i, 0)), pl.BlockSpec((tm, D), lambda i, k: (i, 0)),
                  pl.BlockSpec((1, D), lambda i, k: (0, 0))],
        out_specs=[pl.BlockSpec((tm, D), lambda i, k: (i, 0)), pl.BlockSpec((tm, D), lambda i, k: (i, 0)),
                   pl.BlockSpec((1, D), lambda i, k: (0, 0))],
        out_shape=[jax.ShapeDtypeStruct((T, D), F32), jax.ShapeDtypeStruct((T, D), BF16),
                   jax.ShapeDtypeStruct((1, D), F32)],
        scratch_shapes=[pltpu.VMEM((tm, D), F32)],
        name=name, compiler_params=_params(2))(dgu, wgu, x, dy, nw)


def _ffn_fwd(tag, x, nw, wgu, wd):
    h = _rms_fwd(tag + "_rms", x, nw)
    gu, act = _ffn_gu(tag + "_gu", h, wgu)
    return _ffn_down(tag + "_down", act, wd, x), (h, gu, act)


def _ffn_bwd(tag, dy, dyb, x, nw, wgu, wd, saved):
    h, gu, act = saved
    dgu = _ffn_dact(tag + "_dact", dyb, wd, gu)
    d_wd = _ffn_dwd(tag + "_dwd", act, dyb)
    d_wgu = _ffn_dwgu(tag + "_dwgu", h, dgu)
    dx, dxb, d_nw = _ffn_dx(tag + "_dx", dgu, wgu, x, dy, nw)
    return dx, dxb, d_nw, d_wgu, d_wd


def _conv_taps(cur, halo, w, first, sign):
    tt = cur.shape[0]
    halo = jnp.where(first, 0.0, halo)
    rid = lax.broadcasted_iota(jnp.int32, (8, cur.shape[1]), 0)
    acc = w[3:4, :] * cur
    for s in (1, 2, 3):
        if sign < 0:
            sh = pltpu.roll(cur, s, 0)
            edge = jnp.where(rid < s, pltpu.roll(halo, s, 0), sh[0:8])
            sh = jnp.concatenate([edge, sh[8:]], axis=0) if tt > 8 else edge
        else:
            sh = pltpu.roll(cur, tt - s, 0)
            edge = jnp.where(rid >= 8 - s, pltpu.roll(halo, 8 - s, 0), sh[tt - 8:])
            sh = jnp.concatenate([sh[:tt - 8], edge], axis=0) if tt > 8 else edge
        acc = acc + w[3 - s:4 - s, :] * sh
    return acc


def _gdn_prep(name, proj, wconv, gate_p):
    T = proj.shape[0]
    tt = min(T, 256)
    hb = tt // 8
    nch = tt // A_CHUNK

    def body(cur_ref, halo_ref, ba_ref, w_ref, gp_ref, qkv_ref, bg_ref, gc_ref):
        first = pl.program_id(0) == 0
        for c in range(24):
            cols = pl.ds(c * 128, 128)
            conv = _conv_taps(cur_ref[:, cols], halo_ref[:, cols], w_ref[:, cols], first, -1)
            y = conv * _sigmoid(conv)
            if c < 16:
                y = y * lax.rsqrt(jnp.sum(y * y, axis=-1, keepdims=True) + EPS)
                if c < 8:
                    y = y * (A_DK ** -0.5)
            qkv_ref[:, cols] = y
        ba = ba_ref[...]
        lane = lax.broadcasted_iota(jnp.int32, ba.shape, 1)
        zarg = ba + gp_ref[1:2, :]
        softplus = jnp.maximum(zarg, 0.0) + jnp.log(1.0 + jnp.exp(-jnp.abs(zarg)))
        bg = jnp.where(lane < 8, _sigmoid(ba), jnp.where(lane < 16, gp_ref[0:1, :] * softplus, 0.0))
        bg_ref[...] = bg
        tri = (lax.broadcasted_iota(jnp.int32, (A_CHUNK, A_CHUNK), 0)
               >= lax.broadcasted_iota(jnp.int32, (A_CHUNK, A_CHUNK), 1)).astype(F32)
        for c in range(nch):
            rows = pl.ds(c * A_CHUNK, A_CHUNK)
            gc_ref[rows, :] = _dotx(tri, bg[c * A_CHUNK:(c + 1) * A_CHUNK, :])

    return pl.pallas_call(
        body, grid=(T // tt,),
        in_specs=[pl.BlockSpec((tt, 3072), lambda i: (i, 0)),
                  pl.BlockSpec((8, 3072), lambda i: (jnp.maximum(i * hb - 1, 0), 0)),
                  pl.BlockSpec((tt, 128), lambda i: (i, 32)),
                  pl.BlockSpec((4, 3072), lambda i: (0, 0)),
                  pl.BlockSpec((2, 128), lambda i: (0, 0))],
        out_specs=[pl.BlockSpec((tt, 3072), lambda i: (i, 0)), pl.BlockSpec((tt, 128), lambda i: (i, 0)),
                   pl.BlockSpec((tt, 128), lambda i: (i, 0))],
        out_shape=[jax.ShapeDtypeStruct((T, 3072), F32), jax.ShapeDtypeStruct((T, 128), F32),
                   jax.ShapeDtypeStruct((T, 128), F32)],
        name=name, compiler_params=_params(1))(proj, proj, proj, wconv, gate_p)


def _chunk_masks():
    ri = lax.broadcasted_iota(jnp.int32, (A_CHUNK, A_CHUNK), 0)
    ci = lax.broadcasted_iota(jnp.int32, (A_CHUNK, A_CHUNK), 1)
    return ri >= ci, ri > ci, ri == ci


def _chunk_local(q, k, gcol, grow, bcol):
    incl, strict, _ = _chunk_masks()
    dec = jnp.where(incl, jnp.exp(jnp.where(incl, gcol - grow, 0.0)), 0.0)
    e = jnp.exp(gcol)
    glast = grow[:, A_CHUNK - 1:A_CHUNK]
    f = jnp.exp(glast - gcol)
    gl = jnp.exp(glast)
    kb = k * bcol
    lmat = jnp.where(strict, _dotb(kb, k, 1, 1) * dec, 0.0)
    amat = jnp.where(incl, _dotb(q, k, 1, 1) * dec, 0.0)
    return dec, e, f, gl, kb, lmat, amat


def _unit_lower_inverse(lmats):
    _, _, eye = _chunk_masks()
    ts = [jnp.where(eye, 1.0, 0.0) - lm for lm in lmats]
    lps = [_doth(lm, lm) for lm in lmats]
    for it in range(5):
        ts = [t + _doth(t, lp) for t, lp in zip(ts, lps)]
        if it < 4:
            lps = [_doth(lp, lp) for lp in lps]
    return ts


def _gdn_delta_fwd(name, qkv, gcol, grow, bcol):
    T = qkv.shape[0]
    tt = min(T, 512)
    nch = tt // A_CHUNK
    NC = T // A_CHUNK
    wd, ng = 128 * A_HG, A_HEADS // A_HG

    def body(q_ref, k_ref, v_ref, gc_ref, gr_ref, bc_ref, o_ref, s_ref, t_ref, u_ref, w_ref, state):
        @pl.when(pl.program_id(1) == 0)
        def _():
            state[...] = jnp.zeros_like(state)

        def chunk(c, carry):
            rows = pl.ds(pl.multiple_of(c * A_CHUNK, A_CHUNK), A_CHUNK)
            hs = range(A_HG)
            cols = [pl.ds(h * 128, 128) for h in hs]
            q = [q_ref[rows, cols[h]] for h in hs]
            k = [k_ref[rows, cols[h]] for h in hs]
            v = [v_ref[rows, cols[h]] for h in hs]
            bcl = [bc_ref[h, rows, :] for h in hs]
            loc = [_chunk_local(q[h], k[h], gc_ref[h, rows, :], gr_ref[h, c], bcl[h]) for h in hs]
            e, f, gl, kb, amat = ([l[i] for l in loc] for i in (1, 2, 3, 4, 6))
            tinv = _unit_lower_inverse([l[5] for l in loc])
            u = [_doth(tinv[h], v[h] * bcl[h]) for h in hs]
            w = [_doth(tinv[h], kb[h] * e[h]) for h in hs]
            s = [state[h] for h in hs]
            vn = [u[h] - _dotb(w[h], s[h]) for h in hs]
            o_s = [_dotb(q[h] * e[h], s[h]) for h in hs]
            o_a = [_dotb(amat[h], vn[h]) for h in hs]
            s_new = [s[h] * gl[h] + _dotb(k[h] * f[h], vn[h], 0, 0) for h in hs]
            for h in hs:
                s_ref[h, c] = s[h].astype(BF16)
                t_ref[h, c] = tinv[h]
                u_ref[rows, cols[h]] = u[h]
                w_ref[rows, cols[h]] = w[h]
                o_ref[rows, cols[h]] = o_s[h] + o_a[h]
                state[h] = s_new[h]
            return carry

        lax.fori_loop(0, nch, chunk, 0)

    hd = lambda col: pl.BlockSpec((tt, wd), lambda h, i: (i, col * ng + h))
    col_spec = pl.BlockSpec((A_HG, tt, 1), lambda h, i: (h, i, 0))
    return pl.pallas_call(
        body, grid=(ng, T // tt),
        in_specs=[hd(0), hd(1), hd(2), col_spec, pl.BlockSpec((A_HG, nch, 1, A_CHUNK), lambda h, i: (h, i, 0, 0)), col_spec],
        out_specs=[pl.BlockSpec((tt, wd), lambda h, i: (i, h)),
                   pl.BlockSpec((A_HG, nch, 128, 128), lambda h, i: (h, i, 0, 0)),
                   pl.BlockSpec((A_HG, nch, A_CHUNK, A_CHUNK), lambda h, i: (h, i, 0, 0)),
                   pl.BlockSpec((tt, wd), lambda h, i: (i, h)), pl.BlockSpec((tt, wd), lambda h, i: (i, h))],
        out_shape=[jax.ShapeDtypeStruct((T, 1024), F32), jax.ShapeDtypeStruct((A_HEADS, NC, 128, 128), BF16),
                   jax.ShapeDtypeStruct((A_HEADS, NC, A_CHUNK, A_CHUNK), F32),
                   jax.ShapeDtypeStruct((T, 1024), F32), jax.ShapeDtypeStruct((T, 1024), F32)],
        scratch_shapes=[pltpu.VMEM((A_HG, 128, 128), F32)],
        name=name, compiler_params=_params(2))(qkv, qkv, qkv, gcol, grow, bcol)


def _gdn_delta_bwd(name, qkv, gcol, grow, bcol, d_o, s_sv, t_sv, u_sv, w_sv):
    T = qkv.shape[0]
    tt = min(T, 512)
    nch = tt // A_CHUNK
    ni = T // tt

    def body(q_ref, k_ref, v_ref, gc_ref, gr_ref, bc_ref, do_ref, s_ref, t_ref, u_ref, w_ref,
             dq_ref, dk_ref, dv_ref, dg_ref, db_ref, dstate):
        @pl.when(pl.program_id(1) == 0)
        def _():
            dstate[...] = jnp.zeros_like(dstate)

        incl, strict, _ = _chunk_masks()
        upper = (lax.broadcasted_iota(jnp.int32, (A_CHUNK, A_CHUNK), 0)
                 <= lax.broadcasted_iota(jnp.int32, (A_CHUNK, A_CHUNK), 1)).astype(F32)
        last_row = lax.broadcasted_iota(jnp.int32, (A_CHUNK, 1), 0) == A_CHUNK - 1
        ones = jnp.ones((A_CHUNK, 128), F32)

        rsum = lambda x: jnp.sum(x, axis=1, keepdims=True)

        def chunk(cc, carry):
            c = nch - 1 - cc
            rows = pl.ds(pl.multiple_of(c * A_CHUNK, A_CHUNK), A_CHUNK)
            hs = range(A_HG)
            cols = [pl.ds(h * 128, 128) for h in hs]
            q = [q_ref[rows, cols[h]] for h in hs]
            k = [k_ref[rows, cols[h]] for h in hs]
            v = [v_ref[rows, cols[h]] for h in hs]
            do = [do_ref[rows, cols[h]] for h in hs]
            u = [u_ref[rows, cols[h]] for h in hs]
            w = [w_ref[rows, cols[h]] for h in hs]
            bcl = [bc_ref[h, rows, :] for h in hs]
            s = [s_ref[h, c] for h in hs]
            tinv = [t_ref[h, c] for h in hs]
            ds = [dstate[h] for h in hs]
            loc = [_chunk_local(q[h], k[h], gc_ref[h, rows, :], gr_ref[h, c], bcl[h]) for h in hs]
            dec, e, f, gl, kb, lmat, amat = ([l[i] for l in loc] for i in range(7))
            qd = [q[h] * e[h] for h in hs]
            kd = [k[h] * f[h] for h in hs]
            ke = [kb[h] * e[h] for h in hs]
            vn = [u[h] - _dotb(w[h], s[h]) for h in hs]
            d_qd = [_dotb(do[h], s[h], 1, 1) for h in hs]
            d_a = [jnp.where(incl, _dotb(do[h], vn[h], 1, 1), 0.0) for h in hs]
            d_vn1 = [_dotb(amat[h], do[h], 0, 0) for h in hs]
            d_vn = [d_vn1[h] + _dotb(kd[h], ds[h]) for h in hs]
            d_kd = [_dotb(vn[h], ds[h], 1, 1) for h in hs]
            d_w = [-_dotb(d_vn[h], s[h], 1, 1) for h in hs]
            ds_q = [_dotb(qd[h], do[h], 0, 0) for h in hs]
            ds_w = [_dotb(w[h], d_vn[h], 0, 0) for h in hs]
            d_bv = [_doth(tinv[h], d_vn[h], 0, 0) for h in hs]
            d_ke = [_doth(tinv[h], d_w[h], 0, 0) for h in hs]
            d_l1 = [_dotb(d_bv[h], u[h], 1, 1) for h in hs]
            d_l = [-jnp.where(strict, d_l1[h] + _dotb(d_ke[h], w[h], 1, 1), 0.0) for h in hs]
            d_kk = [d_l[h] * dec[h] for h in hs]
            d_qk = [d_a[h] * dec[h] for h in hs]
            d_kb = [_dotb(d_kk[h], k[h]) for h in hs]
            dk1 = [_dotb(d_kk[h], kb[h], 0, 0) for h in hs]
            dk2 = [_dotb(d_qk[h], q[h], 0, 0) for h in hs]
            dq1 = [_dotb(d_qk[h], k[h]) for h in hs]
            m = [d_l[h] * lmat[h] + d_a[h] * amat[h] for h in hs]
            col_m = [_dotx(m[h], ones, 0, 0)[:, 0:1] for h in hs]
            d_gc = []
            for h in hs:
                d_gl = jnp.sum(jnp.sum(ds[h] * s[h].astype(F32), axis=1, keepdims=True), axis=0, keepdims=True)
                r_kd = rsum(d_kd[h] * kd[h])
                tail = jnp.sum(r_kd, axis=0, keepdims=True) + d_gl * gl[h]
                d_gc.append(rsum(m[h]) - col_m[h] + rsum(d_qd[h] * qd[h]) - r_kd + rsum(d_ke[h] * ke[h])
                            + jnp.where(last_row, tail, 0.0))
            dg = [_dotx(upper, d_gc[h] * ones)[:, 0:1] for h in hs]
            for h in hs:
                dstate[h] = gl[h] * ds[h] + ds_q[h] - ds_w[h]
                dk_ref[rows, cols[h]] = (dk1[h] + dk2[h] + d_kd[h] * f[h] + d_ke[h] * (bcl[h] * e[h])
                                         + d_kb[h] * bcl[h])
                dq_ref[rows, cols[h]] = dq1[h] + d_qd[h] * e[h]
                dv_ref[rows, cols[h]] = d_bv[h] * bcl[h]
                db_ref[h, rows, :] = rsum(d_ke[h] * k[h]) * e[h] + rsum(d_kb[h] * k[h]) + rsum(d_bv[h] * v[h])
                dg_ref[h, rows, :] = dg[h]
            return carry

        lax.fori_loop(0, nch, chunk, 0)

    wd, ng = 128 * A_HG, A_HEADS // A_HG
    rev = lambda i: ni - 1 - i
    hd = lambda col: pl.BlockSpec((tt, wd), lambda h, i: (rev(i), col * ng + h))
    hd1 = pl.BlockSpec((tt, wd), lambda h, i: (rev(i), h))
    col_spec = pl.BlockSpec((A_HG, tt, 1), lambda h, i: (h, rev(i), 0))
    return pl.pallas_call(
        body, grid=(ng, ni),
        in_specs=[hd(0), hd(1), hd(2), col_spec,
                  pl.BlockSpec((A_HG, nch, 1, A_CHUNK), lambda h, i: (h, rev(i), 0, 0)), col_spec, hd1,
                  pl.BlockSpec((A_HG, nch, 128, 128), lambda h, i: (h, rev(i), 0, 0)),
                  pl.BlockSpec((A_HG, nch, A_CHUNK, A_CHUNK), lambda h, i: (h, rev(i), 0, 0)), hd1, hd1],
        out_specs=[hd1, hd1, hd1, col_spec, col_spec],
        out_shape=[jax.ShapeDtypeStruct((T, 1024), F32)] * 3 + [jax.ShapeDtypeStruct((A_HEADS, T, 1), F32)] * 2,
        scratch_shapes=[pltpu.VMEM((A_HG, 128, 128), F32)],
        name=name, compiler_params=_params(2))(qkv, qkv, qkv, gcol, grow, bcol, d_o, s_sv, t_sv, u_sv, w_sv)


def _gdn_gate_fwd(name, o, proj, nw):
    T = o.shape[0]
    tt = min(T, 512)

    def body(o_ref, z_ref, nw_ref, y_ref):
        for h in range(A_HEADS):
            cols = pl.ds(h * 128, 128)
            ov, z = o_ref[:, cols], z_ref[:, cols]
            r = lax.rsqrt(jnp.mean(ov * ov, axis=-1, keepdims=True) + EPS)
            y_ref[:, cols] = (ov * r * nw_ref[...] * (z * _sigmoid(z))).astype(BF16)

    return pl.pallas_call(
        body, grid=(T // tt,),
        in_specs=[pl.BlockSpec((tt, 1024), lambda i: (i, 0)), pl.BlockSpec((tt, 1024), lambda i: (i, 3)),
                  pl.BlockSpec((1, 128), lambda i: (0, 0))],
        out_specs=pl.BlockSpec((tt, 1024), lambda i: (i, 0)),
        out_shape=jax.ShapeDtypeStruct((T, 1024), BF16), name=name, compiler_params=_params(1))(o, proj, nw)


def _gdn_gate_bwd(name, dy2, o, proj, nw):
    T = o.shape[0]
    tt = min(T, 512)

    def body(dy_ref, o_ref, z_ref, nw_ref, do_ref, dz_ref, dnw_ref):
        dnw = jnp.zeros((1, 128), F32)
        for h in range(A_HEADS):
            cols = pl.ds(h * 128, 128)
            dy, ov, z = dy_ref[:, cols], o_ref[:, cols], z_ref[:, cols]
            s = _sigmoid(z)
            sz = z * s
            r = lax.rsqrt(jnp.mean(ov * ov, axis=-1, keepdims=True) + EPS)
            xhat = ov * r
            dn = dy * sz
            dz_ref[:, cols] = dy * (xhat * nw_ref[...]) * (s + z * s * (1.0 - s))
            dxhat = dn * nw_ref[...]
            do_ref[:, cols] = r * (dxhat - xhat * jnp.mean(dxhat * xhat, axis=-1, keepdims=True))
            dnw = dnw + jnp.sum(dn * xhat, axis=0, keepdims=True)

        @pl.when(pl.program_id(0) == 0)
        def _():
            dnw_ref[...] = dnw

        @pl.when(pl.program_id(0) > 0)
        def _():
            dnw_ref[...] += dnw

    blk = lambda c: pl.BlockSpec((tt, 1024), lambda i: (i, c))
    return pl.pallas_call(
        body, grid=(T // tt,),
        in_specs=[blk(0), blk(0), blk(3), pl.BlockSpec((1, 128), lambda i: (0, 0))],
        out_specs=[blk(0), blk(0), pl.BlockSpec((1, 128), lambda i: (0, 0))],
        out_shape=[jax.ShapeDtypeStruct((T, 1024), F32), jax.ShapeDtypeStruct((T, 1024), F32),
                   jax.ShapeDtypeStruct((1, 128), F32)],
        name=name, compiler_params=_params(1))(dy2, o, proj, nw)


def _gdn_prep_bwd1(name, proj, wconv, gate_p, dq, dk, dv, dbg):
    T = proj.shape[0]
    tt = min(T, 256)
    hb = tt // 8

    def body(cur_ref, halo_ref, ba_ref, w_ref, gp_ref, dq_ref, dk_ref, dv_ref, dbg_ref,
             dc_ref, dw_ref, dba_ref, dgp_ref):
        first = pl.program_id(0) == 0
        rid = lax.broadcasted_iota(jnp.int32, (8, 128), 0)
        for c in range(24):
            cols = pl.ds(c * 128, 128)
            cur = cur_ref[:, cols]
            halo = jnp.where(first, 0.0, halo_ref[:, cols])
            conv = _conv_taps(cur, halo_ref[:, cols], w_ref[:, cols], first, -1)
            s = _sigmoid(conv)
            y = conv * s
            if c < 16:
                dref = dq_ref if c < 8 else dk_ref
                dn = dref[:, pl.ds((c % 8) * 128, 128)]
                rinv = lax.rsqrt(jnp.sum(y * y, axis=-1, keepdims=True) + EPS)
                yhat = y * rinv
                dyv = rinv * (dn - yhat * jnp.sum(dn * yhat, axis=-1, keepdims=True))
                if c < 8:
                    dyv = dyv * (A_DK ** -0.5)
            else:
                dyv = dv_ref[:, pl.ds((c - 16) * 128, 128)]
            dc = dyv * (s + conv * s * (1.0 - s))
            dc_ref[:, cols] = dc
            parts = [jnp.sum(dc * cur, axis=0, keepdims=True)]
            for sft in (1, 2, 3):
                sh = pltpu.roll(cur, sft, 0)
                edge = jnp.where(rid < sft, pltpu.roll(halo, sft, 0), sh[0:8])
                sh = jnp.concatenate([edge, sh[8:]], axis=0) if tt > 8 else edge
                parts.append(jnp.sum(dc * sh, axis=0, keepdims=True))
            dwc = jnp.concatenate(parts[::-1], axis=0)

            @pl.when(first)
            def _():
                dw_ref[:, cols] = dwc

            @pl.when(jnp.logical_not(first))
            def _():
                dw_ref[:, cols] += dwc

        ba = ba_ref[...]
        dbg = dbg_ref[...]
        lane = lax.broadcasted_iota(jnp.int32, ba.shape, 1)
        sb = _sigmoid(ba)
        zarg = ba + gp_ref[1:2, :]
        softplus = jnp.maximum(zarg, 0.0) + jnp.log(1.0 + jnp.exp(-jnp.abs(zarg)))
        d_b = dbg * sb * (1.0 - sb)
        d_a = dbg * gp_ref[0:1, :] * _sigmoid(zarg)
        dba_ref[...] = jnp.where(lane < 8, d_b, jnp.where(lane < 16, d_a, 0.0))
        g = gp_ref[0:1, :] * softplus
        in_a = (lane >= 8) & (lane < 16)
        sums = jnp.concatenate([jnp.sum(jnp.where(in_a, dbg * g, 0.0), axis=0, keepdims=True),
                                jnp.sum(jnp.where(in_a, d_a, 0.0), axis=0, keepdims=True)], axis=0)

        @pl.when(first)
        def _():
            dgp_ref[...] = sums

        @pl.when(jnp.logical_not(first))
        def _():
            dgp_ref[...] += sums

    row = lambda w, c=0: pl.BlockSpec((tt, w), lambda i: (i, c))
    return pl.pallas_call(
        body, grid=(T // tt,),
        in_specs=[row(3072), pl.BlockSpec((8, 3072), lambda i: (jnp.maximum(i * hb - 1, 0), 0)), row(128, 32),
                  pl.BlockSpec((4, 3072), lambda i: (0, 0)), pl.BlockSpec((2, 128), lambda i: (0, 0)),
                  row(1024), row(1024), row(1024), row(128)],
        out_specs=[row(3072), pl.BlockSpec((4, 3072), lambda i: (0, 0)), row(128),
                   pl.BlockSpec((2, 128), lambda i: (0, 0))],
        out_shape=[jax.ShapeDtypeStruct((T, 3072), F32), jax.ShapeDtypeStruct((4, 3072), F32),
                   jax.ShapeDtypeStruct((T, 128), F32), jax.ShapeDtypeStruct((2, 128), F32)],
        name=name, compiler_params=_params(1))(proj, proj, proj, wconv, gate_p, dq, dk, dv, dbg)


def _gdn_prep_bwd2(name, dc, wconv, dz, dba):
    T = dc.shape[0]
    tt = min(T, 256)
    hb = tt // 8
    ni = T // tt

    def body(cur_ref, halo_ref, w_ref, dz_ref, dba_ref, o_ref):
        last = pl.program_id(0) == ni - 1
        for c in range(24):
            cols = pl.ds(c * 128, 128)
            o_ref[:, cols] = _conv_taps(cur_ref[:, cols], halo_ref[:, cols], w_ref[:, cols], last, +1)
        o_ref[:, pl.ds(3072, 1024)] = dz_ref[...]
        o_ref[:, pl.ds(4096, 128)] = dba_ref[...]

    return pl.pallas_call(
        body, grid=(ni,),
        in_specs=[pl.BlockSpec((tt, 3072), lambda i: (i, 0)),
                  pl.BlockSpec((8, 3072), lambda i: (jnp.minimum((i + 1) * hb, T // 8 - 1), 0)),
                  pl.BlockSpec((4, 3072), lambda i: (0, 0)),
                  pl.BlockSpec((tt, 1024), lambda i: (i, 0)), pl.BlockSpec((tt, 128), lambda i: (i, 0))],
        out_specs=pl.BlockSpec((tt, A_IN_PAD), lambda i: (i, 0)),
        out_shape=jax.ShapeDtypeStruct((T, A_IN_PAD), F32), name=name, compiler_params=_params(1))(
            dc, dc, wconv, dz, dba)


def _gdn_fwd(x, nw, w_in, wconv, gate_p, out_nw, w_out):
    T = x.shape[0]
    h = _rms_fwd("a_rms", x, nw)
    proj = _mm_plain("a_proj", h, w_in, 1, 0, F32, tn=FF_BLK)
    qkv, bg, gcum = _gdn_prep("a_prep", proj, wconv, gate_p)
    bcol = bg[:, 0:8].T.reshape(A_HEADS, T, 1)
    gcol = gcum[:, 8:16].T.reshape(A_HEADS, T, 1)
    grow = gcol.reshape(A_HEADS, T // A_CHUNK, 1, A_CHUNK)
    o, s_sv, t_sv, u_sv, w_sv = _gdn_delta_fwd("a_delta", qkv, gcol, grow, bcol)
    o2 = _gdn_gate_fwd("a_gate", o, proj, out_nw)
    y = _mm_residual("a_out", o2, w_out, x, 1.0)
    return y, (h, proj, qkv, gcol, grow, bcol, o, s_sv, t_sv, u_sv, w_sv, o2)


def _gdn_bwd(dy, dyb, x, nw, w_in, wconv, gate_p, out_nw, w_out, saved):
    h, proj, qkv, gcol, grow, bcol, o, s_sv, t_sv, u_sv, w_sv, o2 = saved
    T = x.shape[0]
    d_o2 = _mm_plain("a_dout", dyb, w_out, 1, 1, F32)
    d_wout = _mm_plain("a_dwout", o2, dyb, 0, 0, F32)
    d_o, d_z, d_outnw = _gdn_gate_bwd("a_dgate", d_o2, o, proj, out_nw)
    dq, dk, dv, dg, dbeta = _gdn_delta_bwd("a_ddelta", qkv, gcol, grow, bcol, d_o, s_sv, t_sv, u_sv, w_sv)
    dbg = jnp.concatenate([dbeta.reshape(A_HEADS, T).T, dg.reshape(A_HEADS, T).T, jnp.zeros((T, 112), F32)], axis=1)
    dc, d_wconv, dba, dgp = _gdn_prep_bwd1("a_dprep1", proj, wconv, gate_p, dq, dk, dv, dbg)
    dproj = _gdn_prep_bwd2("a_dprep2", dc, wconv, d_z, dba)
    d_win = _mm_plain("a_dwin", h, dproj, 0, 0, F32, tn=FF_BLK)
    dx, dxb, d_nw = _mm_rms_bwd("a_dx", dproj, w_in, x, dy, nw, tk=FF_BLK)
    return dx, dxb, d_nw, d_win, d_wconv, dgp, d_outnw, d_wout


def _swa_masks(n):
    qi = lax.broadcasted_iota(jnp.int32, (B_BLK, B_BLK), 0)
    kj = lax.broadcasted_iota(jnp.int32, (B_BLK, B_BLK), 1)
    return kj > qi + jnp.where(n > 0, 0, B_BLK), kj <= qi


def _swa_fwd(name, q, k, v, sinks):
    T = q.shape[1]
    tq = min(T, 1024)
    nbt = tq // B_BLK
    scale = B_HD ** -0.5
    G = B_HEADS // B_KV

    def body(q_ref, k_ref, v_ref, kh_ref, vh_ref, s_ref, o_ref, l_ref):
        first_blk = pl.program_id(1) * nbt

        def block(n, kp, vp):
            m_prev, m_cur = _swa_masks(first_blk + n)
            cur = pl.ds(pl.multiple_of(n * B_BLK, B_BLK), B_BLK)
            kc, vc = k_ref[0, cur, :], v_ref[0, cur, :]
            gs = range(G)
            rmax = lambda a: jnp.max(a, axis=1, keepdims=True)
            rsum = lambda a: jnp.sum(a, axis=1, keepdims=True)
            sink = [s_ref[g][:, 0:1] for g in gs]
            qb = [q_ref[g, cur, :] for g in gs]
            s_p = [jnp.where(m_prev, _dot(qb[g], kp, 1, 1) * scale, -jnp.inf) for g in gs]
            s_c = [jnp.where(m_cur, _dot(qb[g], kc, 1, 1) * scale, -jnp.inf) for g in gs]
            m = [jnp.maximum(jnp.maximum(rmax(s_p[g]), rmax(s_c[g])), sink[g]) for g in gs]
            p_p = [jnp.exp(s_p[g] - m[g]) for g in gs]
            p_c = [jnp.exp(s_c[g] - m[g]) for g in gs]
            den = [rsum(p_p[g]) + rsum(p_c[g]) + jnp.exp(sink[g] - m[g]) for g in gs]
            a_p = [_dotb(p_p[g], vp) for g in gs]
            a_c = [_dotb(p_c[g], vc) for g in gs]
            for g in gs:
                o_ref[g, cur, :] = ((a_p[g] + a_c[g]) / den[g]).astype(BF16)
                l_ref[g, cur, :] = m[g] + jnp.log(den[g])

        block(0, kh_ref[0], vh_ref[0])

        def rest(n, carry):
            prv = pl.ds(pl.multiple_of((n - 1) * B_BLK, B_BLK), B_BLK)
            block(n, k_ref[0, prv, :], v_ref[0, prv, :])
            return carry

        lax.fori_loop(1, nbt, rest, 0)

    qs = pl.BlockSpec((G, tq, B_HD), lambda kv, i: (kv, i, 0))
    ks = pl.BlockSpec((1, tq, B_HD), lambda kv, i: (kv, i, 0))
    halo = pl.BlockSpec((1, B_BLK, B_HD), lambda kv, i: (kv, jnp.maximum(i * nbt - 1, 0), 0))
    return pl.pallas_call(
        body, grid=(B_KV, T // tq),
        in_specs=[qs, ks, ks, halo, halo, pl.BlockSpec((G, 1, 128), lambda kv, i: (kv, 0, 0))],
        out_specs=[qs, pl.BlockSpec((G, tq, 1), lambda kv, i: (kv, i, 0))],
        out_shape=[jax.ShapeDtypeStruct((B_HEADS, T, B_HD), BF16), jax.ShapeDtypeStruct((B_HEADS, T, 1), F32)],
        name=name, compiler_params=_params(2))(q, k, v, k, v, sinks)


def _swa_bwd(name, q, k, v, sinks, o, lse, do):
    T = q.shape[1]
    tq = min(T, 1024)
    nbt, ni = tq // B_BLK, T // tq
    scale = B_HD ** -0.5
    G = B_HEADS // B_KV

    def body(q_ref, k_ref, v_ref, kh_ref, vh_ref, s_ref, o_ref, l_ref, do_ref, dq_ref, dk_ref, dv_ref, ds_ref,
             dk_halo, dv_halo):
        step = pl.program_id(1)
        first_blk = (ni - 1 - step) * nbt
        last = pl.ds(tq - B_BLK, B_BLK)
        dk_ref[...] = jnp.zeros_like(dk_ref)
        dv_ref[...] = jnp.zeros_like(dv_ref)

        @pl.when(step > 0)
        def _():
            dk_ref[0, last, :] = dk_halo[...]
            dv_ref[0, last, :] = dv_halo[...]

        def block(n, kp, vp, dsinks):
            m_prev, m_cur = _swa_masks(first_blk + n)
            cur = pl.ds(pl.multiple_of(n * B_BLK, B_BLK), B_BLK)
            kc, vc = k_ref[0, cur, :], v_ref[0, cur, :]
            gs = range(G)
            sink = [s_ref[g][:, 0:1] for g in gs]
            qb = [q_ref[g, cur, :] for g in gs]
            dob = [do_ref[g, cur, :] for g in gs]
            lse_b = [l_ref[g, cur, :] for g in gs]
            p_p = [jnp.where(m_prev, jnp.exp(_dot(qb[g], kp, 1, 1) * scale - lse_b[g]), 0.0) for g in gs]
            p_c = [jnp.where(m_cur, jnp.exp(_dot(qb[g], kc, 1, 1) * scale - lse_b[g]), 0.0) for g in gs]
            delta = [jnp.sum(dob[g].astype(F32) * o_ref[g, cur, :].astype(F32), axis=1, keepdims=True) for g in gs]
            ds_p = [p_p[g] * (_dot(dob[g], vp, 1, 1) - delta[g]) for g in gs]
            ds_c = [p_c[g] * (_dot(dob[g], vc, 1, 1) - delta[g]) for g in gs]
            dq_p = [_dotb(ds_p[g], kp) for g in gs]
            dq_c = [_dotb(ds_c[g], kc) for g in gs]
            dk_ps = [_dotb(ds_p[g], qb[g], 0, 0) for g in gs]
            dk_cs = [_dotb(ds_c[g], qb[g], 0, 0) for g in gs]
            dv_ps = [_dotb(p_p[g], dob[g], 0, 0) for g in gs]
            dv_cs = [_dotb(p_c[g], dob[g], 0, 0) for g in gs]
            for g in gs:
                dq_ref[g, cur, :] = (dq_p[g] + dq_c[g]) * scale
            out = tuple(dsinks[g] - jnp.sum(jnp.exp(sink[g] - lse_b[g]) * delta[g], axis=0, keepdims=True) for g in gs)
            total = lambda parts: (parts[0] + parts[1]) + (parts[2] + parts[3])
            dk_ref[0, cur, :] += total(dk_cs) * scale
            dv_ref[0, cur, :] += total(dv_cs)
            return total(dk_ps) * scale, total(dv_ps), out

        zeros = tuple(jnp.zeros((1, 1), F32) for _ in range(G))
        dk_p, dv_p, dsinks = block(0, kh_ref[0], vh_ref[0], zeros)
        dk_halo[...] = dk_p
        dv_halo[...] = dv_p

        def rest(n, dsinks):
            prv = pl.ds(pl.multiple_of((n - 1) * B_BLK, B_BLK), B_BLK)
            dk_p, dv_p, dsinks = block(n, k_ref[0, prv, :], v_ref[0, prv, :], dsinks)
            dk_ref[0, prv, :] += dk_p
            dv_ref[0, prv, :] += dv_p
            return dsinks

        dsinks = lax.fori_loop(1, nbt, rest, dsinks)
        for g in range(G):
            row = jnp.broadcast_to(dsinks[g], (1, 128))

            @pl.when(step == 0)
            def _():
                ds_ref[g] = row

            @pl.when(step > 0)
            def _():
                ds_ref[g] += row

    rev = lambda i: ni - 1 - i
    qs = pl.BlockSpec((G, tq, B_HD), lambda kv, i: (kv, rev(i), 0))
    ks = pl.BlockSpec((1, tq, B_HD), lambda kv, i: (kv, rev(i), 0))
    halo = pl.BlockSpec((1, B_BLK, B_HD), lambda kv, i: (kv, jnp.maximum(rev(i) * nbt - 1, 0), 0))
    ss = pl.BlockSpec((G, 1, 128), lambda kv, i: (kv, 0, 0))
    return pl.pallas_call(
        body, grid=(B_KV, ni),
        in_specs=[qs, ks, ks, halo, halo, ss, qs, pl.BlockSpec((G, tq, 1), lambda kv, i: (kv, rev(i), 0)), qs],
        out_specs=[qs, ks, ks, ss],
        out_shape=[jax.ShapeDtypeStruct((B_HEADS, T, B_HD), F32), jax.ShapeDtypeStruct((B_KV, T, B_HD), F32),
                   jax.ShapeDtypeStruct((B_KV, T, B_HD), F32), jax.ShapeDtypeStruct((B_HEADS, 1, 128), F32)],
        scratch_shapes=[pltpu.VMEM((B_BLK, B_HD), F32), pltpu.VMEM((B_BLK, B_HD), F32)],
        name=name, compiler_params=_params(2))(q, k, v, k, v, sinks, o, lse, do)


def _split_heads(a, n):
    T = a.shape[0]
    return a.reshape(T, n, B_HD).transpose(1, 0, 2)


def _merge_heads(a):
    n, T, _ = a.shape
    return a.transpose(1, 0, 2).reshape(T, n * B_HD)


def _swa_mixer_fwd(x, nw, w_in, b_in, sinks, w_out, b_out):
    h = _rms_fwd("b_rms", x, nw)
    proj = _mm_plain("b_proj", h, w_in, 1, 0, BF16, tn=768, bias=b_in)
    q, k, v = _split_heads(proj[:, :1024], B_HEADS), _split_heads(proj[:, 1024:1280], B_KV), _split_heads(proj[:, 1280:], B_KV)
    o, lse = _swa_fwd("b_attn", q, k, v, sinks)
    om = _merge_heads(o)
    y = _mm_residual("b_out", om, w_out, x, 1.0, bias=b_out)
    return y, (h, q, k, v, o, lse, om)


def _swa_mixer_bwd(dy, dyb, x, nw, w_in, sinks, w_out, saved):
    h, q, k, v, o, lse, om = saved
    d_om = _mm_plain("b_dout", dyb, w_out, 1, 1, BF16)
    d_wout = _mm_plain("b_dwout", om, dyb, 0, 0, F32)
    d_bout = _colsum("b_dbout", dy)
    dq, dk, dv, dsinks = _swa_bwd("b_dattn", q, k, v, sinks, o, lse, _split_heads(d_om, B_HEADS))
    dproj = jnp.concatenate([_merge_heads(dq), _merge_heads(dk), _merge_heads(dv)], axis=1)
    d_bin = _colsum("b_dbin", dproj)
    d_win = _mm_plain("b_dwin", h, dproj, 0, 0, F32, tn=768)
    dx, dxb, d_nw = _mm_rms_bwd("b_dx", dproj, w_in, x, dy, nw, tk=768)
    return dx, dxb, d_nw, d_win, d_bin, dsinks[:, 0, 0], d_wout, d_bout


def _loss_head(name, x, tgt, fw):
    T, D = x.shape
    tt = min(T, 512)

    def body(x_ref, t_ref, w_ref, dx_ref, dxb_ref, loss_ref, dw_ref):
        xv = x_ref[...]
        r = lax.rsqrt(jnp.mean(xv * xv, axis=-1, keepdims=True) + EPS)
        xhat = xv * r
        diff = xhat * w_ref[...] - t_ref[...]
        part = 0.5 * jnp.sum(jnp.mean(diff * diff, axis=-1, keepdims=True), axis=0, keepdims=True)
        dyv = diff * (1.0 / D)
        dxhat = dyv * w_ref[...]
        dx = r * (dxhat - xhat * jnp.mean(dxhat * xhat, axis=-1, keepdims=True))
        dx_ref[...] = dx
        dxb_ref[...] = dx.astype(BF16)
        dw = jnp.sum(dyv * xhat, axis=0, keepdims=True)
        lp = jnp.broadcast_to(part, (1, 128))

        @pl.when(pl.program_id(0) == 0)
        def _():
            loss_ref[...] = lp
            dw_ref[...] = dw

        @pl.when(pl.program_id(0) > 0)
        def _():
            loss_ref[...] += lp
            dw_ref[...] += dw

    row = pl.BlockSpec((tt, D), lambda i: (i, 0))
    return pl.pallas_call(
        body, grid=(T // tt,), in_specs=[row, row, pl.BlockSpec((1, D), lambda i: (0, 0))],
        out_specs=[row, row, pl.BlockSpec((1, 128), lambda i: (0, 0)), pl.BlockSpec((1, D), lambda i: (0, 0))],
        out_shape=[jax.ShapeDtypeStruct((T, D), F32), jax.ShapeDtypeStruct((T, D), BF16),
                   jax.ShapeDtypeStruct((1, 128), F32), jax.ShapeDtypeStruct((1, D), F32)],
        name=name, compiler_params=_params(1))(x, tgt, fw)


def _local_step(x, tgt, wts):
    W = wts
    g = {}
    x1, sv1 = _ffn_fwd("f10", x, W["ffn1_norm"][0:1], W["ffn1_w_gu"][0], W["ffn1_w_down"][0])
    x2, sva = _gdn_fwd(x1, W["mix_norm"][0:1], W["a_w_in"], W["a_w_conv"], W["a_gate_p"], W["a_out_norm"], W["a_w_out"])
    x3, sv3 = _ffn_fwd("f20", x2, W["ffn2_norm"][0:1], W["ffn2_w_gu"][0], W["ffn2_w_down"][0])
    x4, sv4 = _ffn_fwd("f11", x3, W["ffn1_norm"][1:2], W["ffn1_w_gu"][1], W["ffn1_w_down"][1])
    x5, svb = _swa_mixer_fwd(x4, W["mix_norm"][1:2], W["b_w_in"], W["b_b_in"], W["b_sinks"], W["b_w_out"], W["b_b_out"])
    x6, sv6 = _ffn_fwd("f21", x5, W["ffn2_norm"][1:2], W["ffn2_w_gu"][1], W["ffn2_w_down"][1])
    dx, dxb, loss_p, g["final_norm"] = _loss_head("loss_head", x6, tgt, W["final_norm"])

    dx, dxb, n21, gu21, wd21 = _ffn_bwd("f21", dx, dxb, x5, W["ffn2_norm"][1:2], W["ffn2_w_gu"][1], W["ffn2_w_down"][1], sv6)
    dx, dxb, nb, g["b_w_in"], g["b_b_in"], g["b_sinks"], g["b_w_out"], g["b_b_out"] = _swa_mixer_bwd(
        dx, dxb, x4, W["mix_norm"][1:2], W["b_w_in"], W["b_sinks"], W["b_w_out"], svb)
    dx, dxb, n11, gu11, wd11 = _ffn_bwd("f11", dx, dxb, x3, W["ffn1_norm"][1:2], W["ffn1_w_gu"][1], W["ffn1_w_down"][1], sv4)
    dx, dxb, n20, gu20, wd20 = _ffn_bwd("f20", dx, dxb, x2, W["ffn2_norm"][0:1], W["ffn2_w_gu"][0], W["ffn2_w_down"][0], sv3)
    dx, dxb, na, g["a_w_in"], g["a_w_conv"], g["a_gate_p"], g["a_out_norm"], g["a_w_out"] = _gdn_bwd(
        dx, dxb, x1, W["mix_norm"][0:1], W["a_w_in"], W["a_w_conv"], W["a_gate_p"], W["a_out_norm"], W["a_w_out"], sva)
    dx, dxb, n10, gu10, wd10 = _ffn_bwd("f10", dx, dxb, x, W["ffn1_norm"][0:1], W["ffn1_w_gu"][0], W["ffn1_w_down"][0], sv1)

    g["ffn1_norm"] = jnp.concatenate([n10, n11], axis=0)
    g["ffn2_norm"] = jnp.concatenate([n20, n21], axis=0)
    g["mix_norm"] = jnp.concatenate([na, nb], axis=0)
    g["ffn1_w_gu"] = jnp.stack([gu10, gu11])
    g["ffn2_w_gu"] = jnp.stack([gu20, gu21])
    g["ffn1_w_down"] = jnp.stack([wd10, wd11])
    g["ffn2_w_down"] = jnp.stack([wd20, wd21])
    return loss_p, dx, g


PACK = (("ffn1_w_gu", 2816), ("ffn2_w_gu", 2816), ("ffn1_w_down", 1408), ("ffn2_w_down", 1408),
        ("a_w_in", 1028), ("a_w_out", 256), ("b_w_in", 384), ("b_w_out", 256))
PACK_TILE = 16
PACK_USED = sum(-(-n // PACK_TILE) * PACK_TILE for _, n in PACK)
PACK_ROWS = 10400
assert PACK_USED <= PACK_ROWS
HALF_ROWS = PACK_ROWS // 2
SMALL_SHARD = (8, 512)
HBM = pl.BlockSpec(memory_space=pl.ANY)


def _mesh_pos():
    x, y, c = lax.axis_index("x"), lax.axis_index("y"), lax.axis_index("c")
    return x, y, c, [(1 - x, y), (x, 1 - y), (1 - x, 1 - y)]


def _half(c):
    return pl.ds(pl.multiple_of(c * HALF_ROWS, 16), HALF_ROWS)


MOVE_ROWS = 1040
SUM_ROWS = 400
assert HALF_ROWS % MOVE_ROWS == 0 and HALF_ROWS % SUM_ROWS == 0


def _gather_chips(big, small):
    def body(big_ref, small_ref, rb_ref, rs_ref, send_sems, recv_sems):
        x, y, c, chips = _mesh_pos()
        send = []
        for j, chip in enumerate(chips):
            send.append(pltpu.make_async_remote_copy(src_ref=big_ref.at[_half(c)], dst_ref=rb_ref.at[j],
                                                     send_sem=send_sems.at[j], recv_sem=recv_sems.at[j],
                                                     device_id=(*chip, c), device_id_type=MESH))
            send.append(pltpu.make_async_remote_copy(src_ref=small_ref, dst_ref=rs_ref.at[j],
                                                     send_sem=send_sems.at[3 + j], recv_sem=recv_sems.at[3 + j],
                                                     device_id=(*chip, c), device_id_type=MESH))
        for cp in send:
            cp.start()
        for cp in send:
            cp.wait_recv()
        for cp in send:
            cp.wait_send()

    return pl.pallas_call(
        body, name="gather_chips", in_specs=[HBM, HBM], out_specs=[HBM, HBM],
        out_shape=[jax.ShapeDtypeStruct((3, HALF_ROWS, 1024), BF16), jax.ShapeDtypeStruct((3,) + SMALL_SHARD, F32)],
        scratch_shapes=[pltpu.SemaphoreType.DMA((6,)), pltpu.SemaphoreType.DMA((6,))])(big, small)


def _gather_fill(big, recv):
    nt = HALF_ROWS // MOVE_ROWS
    own_tiles = PACK_ROWS // MOVE_ROWS
    assert own_tiles <= 3 * nt

    def body(recv_ref, big_ref, g_ref, send_sem, recv_sem, local_sems):
        x, y, c, chips = _mesh_pos()
        j, t = pl.program_id(0), pl.program_id(1)
        step = j * nt + t
        src_chip = jnp.where(j == 0, 2 * (1 - x) + y, jnp.where(j == 1, 2 * x + 1 - y, 2 * (1 - x) + 1 - y))
        rows = pl.ds(pl.multiple_of(c * HALF_ROWS + t * MOVE_ROWS, 16), MOVE_ROWS)
        keep = pltpu.make_async_copy(recv_ref.at[0], g_ref.at[src_chip, rows], local_sems.at[0])
        give = pltpu.make_async_remote_copy(src_ref=recv_ref.at[0], dst_ref=g_ref.at[src_chip, rows],
                                            send_sem=send_sem, recv_sem=recv_sem,
                                            device_id=(x, y, 1 - c), device_id_type=MESH)
        keep.start()
        give.start()

        @pl.when(step < own_tiles)
        def _():
            own_rows = pl.ds(pl.multiple_of(step * MOVE_ROWS, 16), MOVE_ROWS)
            own = pltpu.make_async_copy(big_ref, g_ref.at[2 * x + y, own_rows], local_sems.at[1])
            own.start()
            own.wait()

        give.wait_send()
        keep.wait()

        @pl.when(step == 3 * nt - 1)
        def _():
            landed = g_ref.at[pl.ds(0, 3), pl.ds(0, HALF_ROWS)]
            pltpu.make_async_remote_copy(src_ref=landed, dst_ref=landed, send_sem=send_sem, recv_sem=recv_sem,
                                         device_id=(x, y, c), device_id_type=MESH).wait_recv()

    return pl.pallas_call(
        body, grid=(3, nt),
        in_specs=[pl.BlockSpec((1, MOVE_ROWS, 1024), lambda j, t: (j, t, 0)),
                  pl.BlockSpec((MOVE_ROWS, 1024), lambda j, t: (jnp.minimum(j * nt + t, own_tiles - 1), 0))],
        out_specs=HBM, out_shape=jax.ShapeDtypeStruct((4, PACK_ROWS, 1024), BF16),
        scratch_shapes=[pltpu.SemaphoreType.DMA, pltpu.SemaphoreType.DMA, pltpu.SemaphoreType.DMA((2,))],
        name="gather_fill", compiler_params=_params(2))(recv, big)


def _pair_send(p):
    nt = HALF_ROWS // MOVE_ROWS

    def body(p_ref, a_ref, send_sem, recv_sem):
        x, y, c, _ = _mesh_pos()
        s, t = pl.program_id(0), pl.program_id(1)
        rows = pl.ds(pl.multiple_of(t * MOVE_ROWS, 16), MOVE_ROWS)
        give = pltpu.make_async_remote_copy(src_ref=p_ref.at[0], dst_ref=a_ref.at[s, rows], send_sem=send_sem,
                                            recv_sem=recv_sem, device_id=(x, y, 1 - c), device_id_type=MESH)
        give.start()
        give.wait_send()

        @pl.when((s == 3) & (t == nt - 1))
        def _():
            pltpu.make_async_remote_copy(src_ref=a_ref, dst_ref=a_ref, send_sem=send_sem, recv_sem=recv_sem,
                                         device_id=(x, y, c), device_id_type=MESH).wait_recv()

    return pl.pallas_call(
        body, grid=(4, nt),
        in_specs=[pl.BlockSpec((1, MOVE_ROWS, 1024), lambda s, t: (s, (1 - lax.axis_index("c")) * nt + t, 0))],
        out_specs=HBM, out_shape=jax.ShapeDtypeStruct((4, HALF_ROWS, 1024), BF16),
        scratch_shapes=[pltpu.SemaphoreType.DMA, pltpu.SemaphoreType.DMA],
        name="pair_send", compiler_params=_params(2))(p)


def _pair_sum(p, a):
    nt = HALF_ROWS // SUM_ROWS

    def body(p_ref, a_ref, o_ref):
        o_ref[...] = (p_ref[...].astype(F32) + a_ref[...].astype(F32)).astype(BF16)

    spec = pl.BlockSpec((1, SUM_ROWS, 1024), lambda s, t: (s, t, 0))
    return pl.pallas_call(
        body, grid=(4, nt),
        in_specs=[pl.BlockSpec((1, SUM_ROWS, 1024), lambda s, t: (s, lax.axis_index("c") * nt + t, 0)), spec],
        out_specs=spec, out_shape=jax.ShapeDtypeStruct((4, HALF_ROWS, 1024), BF16),
        name="pair_sum", compiler_params=_params(2))(p, a)


def _chip_exchange(cs):
    def body(c_ref, b_ref, send_sems, recv_sems):
        x, y, c, chips = _mesh_pos()
        send = [pltpu.make_async_remote_copy(src_ref=c_ref.at[2 * chip[0] + chip[1]], dst_ref=b_ref.at[j],
                                             send_sem=send_sems.at[j], recv_sem=recv_sems.at[j],
                                             device_id=(*chip, c), device_id_type=MESH)
                for j, chip in enumerate(chips)]
        for cp in send:
            cp.start()
        for cp in send:
            cp.wait_recv()
        for cp in send:
            cp.wait_send()

    return pl.pallas_call(
        body, name="chip_exchange", in_specs=[HBM], out_specs=HBM,
        out_shape=jax.ShapeDtypeStruct((3, HALF_ROWS, 1024), BF16),
        scratch_shapes=[pltpu.SemaphoreType.DMA((3,)), pltpu.SemaphoreType.DMA((3,))])(cs)


def _chip_sum(cs, b):
    nt = HALF_ROWS // SUM_ROWS

    def body(c_ref, b_ref, r_ref, buf, send_sems, recv_sem, local_sems):
        x, y, c, _ = _mesh_pos()
        t = pl.program_id(0)
        slot = lax.rem(t, 2)

        def copies(k, tile):
            rows = pl.ds(pl.multiple_of(c * HALF_ROWS + tile * SUM_ROWS, 8), SUM_ROWS)
            keep = pltpu.make_async_copy(buf.at[k], r_ref.at[rows], local_sems.at[k])
            give = pltpu.make_async_remote_copy(src_ref=buf.at[k], dst_ref=r_ref.at[rows], send_sem=send_sems.at[k],
                                                recv_sem=recv_sem, device_id=(x, y, 1 - c), device_id_type=MESH)
            return keep, give

        @pl.when(t >= 2)
        def _():
            keep, give = copies(slot, t - 2)
            keep.wait()
            give.wait_send()

        buf[slot] = (c_ref[0].astype(F32) + b_ref[0].astype(F32)) + (b_ref[1].astype(F32) + b_ref[2].astype(F32))
        keep, give = copies(slot, t)
        keep.start()
        give.start()

        @pl.when(t == nt - 1)
        def _():
            for back in (1, 0):
                keep, give = copies(lax.rem(t - back, 2), t - back)
                keep.wait()
                give.wait_send()
            landed = r_ref.at[_half(1 - c)]
            pltpu.make_async_remote_copy(src_ref=landed, dst_ref=landed, send_sem=send_sems.at[0], recv_sem=recv_sem,
                                         device_id=(x, y, c), device_id_type=MESH).wait_recv()

    return pl.pallas_call(
        body, grid=(nt,),
        in_specs=[pl.BlockSpec((1, SUM_ROWS, 1024), lambda t: (2 * lax.axis_index("x") + lax.axis_index("y"), t, 0)),
                  pl.BlockSpec((3, SUM_ROWS, 1024), lambda t: (0, t, 0))],
        out_specs=HBM, out_shape=jax.ShapeDtypeStruct((PACK_ROWS, 1024), F32),
        scratch_shapes=[pltpu.VMEM((2, SUM_ROWS, 1024), F32), pltpu.SemaphoreType.DMA((2,)), pltpu.SemaphoreType.DMA,
                        pltpu.SemaphoreType.DMA((2,))],
        name="chip_sum", compiler_params=_params(1))(cs, b)


def _reduce_scatter(p):
    cs = _pair_sum(p, _pair_send(p))
    return _chip_sum(cs, _chip_exchange(cs))


SMALL_ROWS = 24


def _all_reduce_small(v):
    def body(v_ref, o_ref, all_ref, send_sems, recv_sems):
        x, y, c, _ = _mesh_pos()
        me = 4 * x + 2 * y + c
        all_ref[me] = v_ref[...]
        peers = [(x ^ ((k >> 2) & 1), y ^ ((k >> 1) & 1), c ^ (k & 1)) for k in range(1, 8)]
        idx = lambda p: 4 * p[0] + 2 * p[1] + p[2]
        send = [pltpu.make_async_remote_copy(src_ref=v_ref, dst_ref=all_ref.at[me], send_sem=send_sems.at[k],
                                             recv_sem=recv_sems.at[k], device_id=p, device_id_type=MESH)
                for k, p in enumerate(peers)]
        for cp in send:
            cp.start()
        for k, p in enumerate(peers):
            pltpu.make_async_remote_copy(src_ref=v_ref, dst_ref=all_ref.at[idx(p)], send_sem=send_sems.at[k],
                                         recv_sem=recv_sems.at[k], device_id=p, device_id_type=MESH).wait_recv()
        for cp in send:
            cp.wait_send()
        acc = all_ref[0]
        for d in range(1, 8):
            acc = acc + all_ref[d]
        o_ref[...] = acc

    vm = pl.BlockSpec(memory_space=pltpu.VMEM)
    return pl.pallas_call(
        body, name="all_reduce_small", in_specs=[vm], out_specs=vm,
        out_shape=jax.ShapeDtypeStruct((SMALL_ROWS, 1024), F32),
        scratch_shapes=[pltpu.VMEM((8, SMALL_ROWS, 1024), F32), pltpu.SemaphoreType.DMA((7,)),
                        pltpu.SemaphoreType.DMA((7,))],)(v)


def _adamw(name, w, g, m, v):
    rows, cols = w.shape
    tr = rows
    if rows * cols > 400_000:
        tr = max(t for t in range(8, rows, 8) if rows % t == 0 and t * cols <= 400_000)

    def body(w_ref, g_ref, m_ref, v_ref, d_ref, nm_ref, nv_ref):
        gv = g_ref[...]
        m_new = ADAM_B1 * m_ref[...] + (1.0 - ADAM_B1) * gv
        v_new = ADAM_B2 * v_ref[...] + (1.0 - ADAM_B2) * (gv * gv)
        m_hat = m_new / (1.0 - ADAM_B1 ** ADAM_STEP)
        v_hat = v_new / (1.0 - ADAM_B2 ** ADAM_STEP)
        d_ref[...] = -ADAM_LR * (m_hat / (jnp.sqrt(v_hat) + ADAM_EPS) + ADAM_WD * w_ref[...])
        nm_ref[...] = m_new
        nv_ref[...] = v_new

    spec = pl.BlockSpec((tr, cols), lambda i: (i, 0))
    sds = jax.ShapeDtypeStruct((rows, cols), F32)
    return pl.pallas_call(body, grid=(rows // tr,), in_specs=[spec] * 4, out_specs=[spec] * 3, out_shape=[sds] * 3,
                          name=name, compiler_params=_params(1))(w, g, m, v)


WEIGHTS = ("ffn1_norm", "ffn1_w_gu", "ffn1_w_down", "mix_norm", "ffn2_norm", "ffn2_w_gu", "ffn2_w_down",
           "a_w_in", "a_w_conv", "a_A_log", "a_dt_bias", "a_out_norm", "a_w_out",
           "b_w_in", "b_b_in", "b_sinks", "b_w_out", "b_b_out", "final_norm")
SMALL_SLOTS = {"ffn1_norm": (0, 2048), "mix_norm": (2048, 2048), "ffn2_norm": (4096, 2048), "final_norm": (6144, 1024),
               "a_A_log": (7168, 8), "a_dt_bias": (7296, 8), "a_out_norm": (7424, 128), "b_sinks": (7552, 16),
               "loss": (7680, 1)}
SMALL_SHARDED = {"a_w_conv": (8192, 8192, (4,), 3072), "b_b_in": (20480, 11264, (), 1536), "b_b_out": (22016, 11648, (), 1024)}
DEV_SMALL_ROWS = 12


def _pack_rows(parts):
    rows = []
    for p in parts:
        r = p.reshape(p.shape[0], -1, 1024)
        rows.append(jnp.pad(r, ((0, 0), (0, -r.shape[1] % PACK_TILE), (0, 0))))
    rows.append(jnp.zeros((parts[0].shape[0], PACK_ROWS - PACK_USED, 1024), parts[0].dtype))
    return jnp.concatenate(rows, axis=1)


def _place(vec, off, a):
    return lax.dynamic_update_slice(vec, a.reshape(-1).astype(F32), (off,))


def kernel(x, ffn1_norm, ffn1_w_gu, ffn1_w_down, mix_norm, ffn2_norm, ffn2_w_gu, ffn2_w_down, a_w_in, a_w_conv, a_A_log, a_dt_bias, a_out_norm, a_w_out, b_w_in, b_b_in, b_sinks, b_w_out, b_b_out, final_norm, loss_target, m_ffn1_norm, m_ffn1_w_gu, m_ffn1_w_down, m_mix_norm, m_ffn2_norm, m_ffn2_w_gu, m_ffn2_w_down, m_a_w_in, m_a_w_conv, m_a_A_log, m_a_dt_bias, m_a_out_norm, m_a_w_out, m_b_w_in, m_b_b_in, m_b_sinks, m_b_w_out, m_b_b_out, m_final_norm, v_ffn1_norm, v_ffn1_w_gu, v_ffn1_w_down, v_mix_norm, v_ffn2_norm, v_ffn2_w_gu, v_ffn2_w_down, v_a_w_in, v_a_w_conv, v_a_A_log, v_a_dt_bias, v_a_out_norm, v_a_w_out, v_b_w_in, v_b_b_in, v_b_sinks, v_b_w_out, v_b_b_out, v_final_norm):
    w = dict(zip(WEIGHTS, (ffn1_norm, ffn1_w_gu, ffn1_w_down, mix_norm, ffn2_norm, ffn2_w_gu, ffn2_w_down, a_w_in, a_w_conv,
                           a_A_log, a_dt_bias, a_out_norm, a_w_out, b_w_in, b_b_in, b_sinks, b_w_out, b_b_out, final_norm)))
    m = dict(zip(WEIGHTS, (m_ffn1_norm, m_ffn1_w_gu, m_ffn1_w_down, m_mix_norm, m_ffn2_norm, m_ffn2_w_gu, m_ffn2_w_down,
                           m_a_w_in, m_a_w_conv, m_a_A_log, m_a_dt_bias, m_a_out_norm, m_a_w_out, m_b_w_in, m_b_b_in,
                           m_b_sinks, m_b_w_out, m_b_b_out, m_final_norm)))
    v = dict(zip(WEIGHTS, (v_ffn1_norm, v_ffn1_w_gu, v_ffn1_w_down, v_mix_norm, v_ffn2_norm, v_ffn2_w_gu, v_ffn2_w_down,
                           v_a_w_in, v_a_w_conv, v_a_A_log, v_a_dt_bias, v_a_out_norm, v_a_w_out, v_b_w_in, v_b_b_in,
                           v_b_sinks, v_b_w_out, v_b_b_out, v_final_norm)))
    chip = 2 * lax.axis_index("x") + lax.axis_index("y")

    big = _pack_rows([w[n].astype(BF16).reshape(1, -1) for n, _ in PACK])[0]
    small = jnp.zeros((4096,), F32)
    small = _place(small, 0, w["a_w_conv"])
    small = _place(small, 3072, w["b_b_in"])
    small = _place(small, 3456, w["b_b_out"]).reshape(SMALL_SHARD)
    rb, rs = _gather_chips(big, small)
    gb = _gather_fill(big, rb)
    offs, o = {}, 0
    for n, r in PACK:
        offs[n] = (o, r)
        o += -(-r // PACK_TILE) * PACK_TILE
    blk = lambda n: gb[:, offs[n][0]:offs[n][0] + offs[n][1]]
    gsf = lax.dynamic_update_slice(jnp.zeros((4, 4096), F32), small.reshape(1, 4096), (chip, 0))
    for j, other in enumerate((chip ^ 2, chip ^ 1, chip ^ 3)):
        gsf = lax.dynamic_update_slice(gsf, rs[j].reshape(1, 4096), (other, 0))
    W = {n: w[n] for n in ("ffn1_norm", "ffn2_norm", "mix_norm", "a_out_norm")}
    for n in ("ffn1_w_gu", "ffn2_w_gu"):
        W[n] = blk(n).reshape(4, 2, 1024, FF_BLK).transpose(1, 0, 2, 3).reshape(2, 2, 2, 1024, FF_BLK)
    for n in ("ffn1_w_down", "ffn2_w_down"):
        W[n] = blk(n).reshape(4, 2, 704, 1024).transpose(1, 0, 2, 3).reshape(2, 2, FF_BLK, 1024)
    W["a_w_in"] = jnp.pad(blk("a_w_in").reshape(4, 1024, 1028).transpose(1, 0, 2).reshape(1024, A_IN_COLS),
                          ((0, 0), (0, A_IN_PAD - A_IN_COLS)))
    W["a_w_out"] = blk("a_w_out").reshape(1024, 1024)
    W["b_w_in"] = blk("b_w_in").reshape(4, 1024, 384).transpose(1, 0, 2).reshape(1024, 1536)
    W["b_w_out"] = blk("b_w_out").reshape(1024, 1024)
    W["a_w_conv"] = gsf[:, 0:3072].reshape(4, 4, 768).transpose(1, 0, 2).reshape(4, 3072)
    W["b_b_in"] = gsf[:, 3072:3456].reshape(1, 1536)
    W["b_b_out"] = gsf[:, 3456:3712].reshape(1, 1024)
    W["a_gate_p"] = jnp.pad(jnp.concatenate([-jnp.exp(w["a_A_log"]), w["a_dt_bias"]], axis=0), ((0, 0), (8, 112)))
    W["b_sinks"] = jnp.broadcast_to(w["b_sinks"][0][:, None, None], (B_HEADS, 1, 128))
    W["final_norm"] = w["final_norm"][None]

    loss_p, dx, g = _local_step(x[0], loss_target[0], W)

    parts = [
        g["ffn1_w_gu"].reshape(2, 4, 1024, FF_BLK).transpose(1, 0, 2, 3), g["ffn2_w_gu"].reshape(2, 4, 1024, FF_BLK).transpose(1, 0, 2, 3),
        g["ffn1_w_down"].reshape(2, 4, 704, 1024).transpose(1, 0, 2, 3), g["ffn2_w_down"].reshape(2, 4, 704, 1024).transpose(1, 0, 2, 3),
        g["a_w_in"][:, :A_IN_COLS].reshape(1024, 4, 1028).transpose(1, 0, 2), g["a_w_out"].reshape(4, 256, 1024),
        g["b_w_in"].reshape(1024, 4, 384).transpose(1, 0, 2), g["b_w_out"].reshape(4, 256, 1024)]
    red = _reduce_scatter(_pack_rows([a.astype(BF16).reshape(4, -1) for a in parts]))
    grads = {n: red[offs[n][0]:offs[n][0] + offs[n][1]].reshape(w[n].shape) for n, _ in PACK}

    sv = jnp.zeros((SMALL_ROWS * 1024,), F32)
    small_g = {"ffn1_norm": g["ffn1_norm"], "mix_norm": g["mix_norm"], "ffn2_norm": g["ffn2_norm"], "final_norm": g["final_norm"],
               "a_A_log": g["a_gate_p"][0, 8:16], "a_dt_bias": g["a_gate_p"][1, 8:16], "a_out_norm": g["a_out_norm"],
               "b_sinks": g["b_sinks"], "loss": loss_p[0, 0:1]}
    for n, (off, _) in SMALL_SLOTS.items():
        sv = _place(sv, off, small_g[n])
    for n, (off, _, _, _) in SMALL_SHARDED.items():
        sv = _place(sv, off, g[n])
    tot = _all_reduce_small(sv.reshape(SMALL_ROWS, 1024)).reshape(-1)
    for n, (off, size) in SMALL_SLOTS.items():
        if n != "loss":
            grads[n] = tot[off:off + size].reshape(w[n].shape)
    for n, (off, _, lead, last) in SMALL_SHARDED.items():
        full = tot[off:off + (lead[0] if lead else 1) * last].reshape(lead + (last,))
        width = last // 4
        grads[n] = lax.dynamic_slice_in_dim(full, chip * width, width, axis=-1).reshape(w[n].shape)
    loss = tot[SMALL_SLOTS["loss"][0]]

    delta, new_m, new_v = {}, {}, {}
    for n, _ in PACK:
        two_d = lambda a: a.reshape(-1, a.shape[-1])
        d, nm, nv = _adamw("adamw_" + n, two_d(w[n]), two_d(grads[n]), two_d(m[n]), two_d(v[n]))
        delta[n], new_m[n], new_v[n] = d.reshape(w[n].shape), nm.reshape(w[n].shape), nv.reshape(w[n].shape)

    def dev_small(src):
        vec = jnp.zeros((DEV_SMALL_ROWS * 1024,), F32)
        for n, (off, _) in SMALL_SLOTS.items():
            if n != "loss":
                vec = _place(vec, off, src[n])
        for n, (_, off, _, _) in SMALL_SHARDED.items():
            vec = _place(vec, off, src[n])
        return vec.reshape(DEV_SMALL_ROWS, 1024)

    sd, sm, svv = _adamw("adamw_small", dev_small(w), dev_small(grads), dev_small(m), dev_small(v))
    for n in WEIGHTS:
        if n in SMALL_SLOTS:
            off, size = SMALL_SLOTS[n]
        elif n in SMALL_SHARDED:
            off, size = SMALL_SHARDED[n][1], w[n].size
        else:
            continue
        for dst, src in ((delta, sd), (new_m, sm), (new_v, svv)):
            dst[n] = src.reshape(-1)[off:off + size].reshape(w[n].shape)

    return (loss, dx[None], *[grads[n] for n in WEIGHTS], *[delta[n] for n in WEIGHTS],
            *[new_m[n] for n in WEIGHTS], *[new_v[n] for n in WEIGHTS])
```

```python
import jax
import jax.numpy as jnp
from jax import lax
from jax.experimental import pallas as pl
from jax.experimental.pallas import tpu as pltpu

F32 = jnp.float32
BF16 = jnp.bfloat16

D_MODEL = 1024
EPS = 1e-6
FF_BLK = 1408
A_HEADS = 8
A_DK = 128
A_CHUNK = 64
A_HG = 4
A_IN_COLS = 4112
A_IN_PAD = 4224
B_HEADS = 16
B_KV = 4
B_HD = 64
B_BLK = 128
ADAM_LR, ADAM_B1, ADAM_B2, ADAM_EPS, ADAM_WD, ADAM_STEP = 0.001, 0.9, 0.999, 1e-08, 0.01, 10
MESH = pl.DeviceIdType.MESH
VMEM_LIMIT = 56 * 1024 * 1024
HBM = pl.BlockSpec(memory_space=pl.ANY)


def _params(n_axes):
    return pltpu.CompilerParams(dimension_semantics=("arbitrary",) * n_axes, vmem_limit_bytes=VMEM_LIMIT)


def _sigmoid(x):
    return 1.0 / (1.0 + jnp.exp(-x))


def _dot(a, b, ca, cb):
    return lax.dot_general(a, b, (((ca,), (cb,)), ((), ())), preferred_element_type=F32)


def _dotb(a, b, ca=1, cb=0):
    return _dot(a.astype(BF16), b.astype(BF16), ca, cb)


def _dotx(a, b, ca=1, cb=0):
    return lax.dot_general(a, b, (((ca,), (cb,)), ((), ())), preferred_element_type=F32,
                           precision=lax.Precision.HIGHEST)


def _doth(a, b, ca=1, cb=0):
    return lax.dot_general(a, b, (((ca,), (cb,)), ((), ())), preferred_element_type=F32,
                           precision=lax.Precision.HIGH)


def _rms_fwd(name, x, w):
    T, D = x.shape
    tt = min(T, 512)

    def body(x_ref, w_ref, h_ref):
        xv = x_ref[...]
        r = lax.rsqrt(jnp.mean(xv * xv, axis=-1, keepdims=True) + EPS)
        h_ref[...] = (xv * r * w_ref[...]).astype(BF16)

    return pl.pallas_call(
        body, grid=(T // tt,),
        in_specs=[pl.BlockSpec((tt, D), lambda i: (i, 0)), pl.BlockSpec((1, D), lambda i: (0, 0))],
        out_specs=pl.BlockSpec((tt, D), lambda i: (i, 0)),
        out_shape=jax.ShapeDtypeStruct((T, D), BF16), name=name, compiler_params=_params(1))(x, w)


def _rms_bwd_tile(dh, xv, dy, w):
    r = lax.rsqrt(jnp.mean(xv * xv, axis=-1, keepdims=True) + EPS)
    xhat = xv * r
    dxhat = dh * w
    dx = dy + r * (dxhat - xhat * jnp.mean(dxhat * xhat, axis=-1, keepdims=True))
    return dx, jnp.sum(dh * xhat, axis=0, keepdims=True)


def _colsum(name, a):
    T, N = a.shape
    tt = min(T, 512)

    def body(a_ref, o_ref):
        @pl.when(pl.program_id(0) == 0)
        def _():
            o_ref[...] = jnp.zeros_like(o_ref)
        o_ref[...] += jnp.sum(a_ref[...].astype(F32), axis=0, keepdims=True)

    return pl.pallas_call(
        body, grid=(T // tt,), in_specs=[pl.BlockSpec((tt, N), lambda i: (i, 0))],
        out_specs=pl.BlockSpec((1, N), lambda i: (0, 0)),
        out_shape=jax.ShapeDtypeStruct((1, N), F32), name=name, compiler_params=_params(1))(a)


def _matmul(name, a, b, ca, cb, tm, tn, tk, extra_in, outs, epi, order="ji"):
    M, K, N = a.shape[1 - ca], a.shape[ca], b.shape[1 - cb]
    tm, tn, tk = min(tm, M), min(tn, N), min(tk, K)
    assert M % tm == 0 and N % tn == 0 and K % tk == 0, (name, M, N, K, tm, tn, tk)
    ni, nj, nk = M // tm, N // tn, K // tk
    if order == "ji":
        grid = (nj, ni, nk)
        perm = lambda g0, g1, g2: (g1, g0, g2)
    else:
        grid = (ni, nj, nk)
        perm = lambda g0, g1, g2: (g0, g1, g2)

    def wrap(f):
        return lambda g0, g1, g2: f(*perm(g0, g1, g2))

    a_spec = (pl.BlockSpec((tm, tk), wrap(lambda i, j, k: (i, k))) if ca == 1
              else pl.BlockSpec((tk, tm), wrap(lambda i, j, k: (k, i))))
    b_spec = (pl.BlockSpec((tk, tn), wrap(lambda i, j, k: (k, j))) if cb == 0
              else pl.BlockSpec((tn, tk), wrap(lambda i, j, k: (j, k))))
    ne, no = len(extra_in), len(outs)

    def body(*refs):
        a_ref, b_ref = refs[0], refs[1]
        ex, out = refs[2:2 + ne], refs[2 + ne:2 + ne + no]
        i, j, k = perm(pl.program_id(0), pl.program_id(1), pl.program_id(2))
        p = _dotb(a_ref[...], b_ref[...], ca, cb)
        if nk == 1:
            epi(p, ex, out, i, j)
        else:
            acc_ref = refs[-1]

            @pl.when(k == 0)
            def _():
                acc_ref[...] = p

            @pl.when(k > 0)
            def _():
                acc_ref[...] += p

            @pl.when(k == nk - 1)
            def _():
                epi(acc_ref[...], ex, out, i, j)

    return pl.pallas_call(
        body, grid=grid,
        in_specs=[a_spec, b_spec] + [pl.BlockSpec(bs, wrap(f)) for _, bs, f in extra_in],
        out_specs=[pl.BlockSpec(bs, wrap(f)) for _, bs, f in outs],
        out_shape=[s for s, _, _ in outs],
        scratch_shapes=[pltpu.VMEM((tm, tn), F32)] if nk > 1 else [],
        name=name, compiler_params=_params(3))(a, b, *[x for x, _, _ in extra_in])


def _mm_plain(name, a, b, ca, cb, out_dtype, tm=1024, tn=1024, tk=1024, scale=1.0, bias=None):
    M, N = a.shape[1 - ca], b.shape[1 - cb]
    tm, tn = min(tm, M), min(tn, N)
    extra = [] if bias is None else [(bias, (1, tn), lambda i, j, k: (0, j))]

    def epi(acc, ex, out, i, j):
        r = acc * scale if scale != 1.0 else acc
        if bias is not None:
            r = r + ex[0][...]
        out[0][...] = r.astype(out_dtype)

    return _matmul(name, a, b, ca, cb, tm, tn, tk, extra,
                   [(jax.ShapeDtypeStruct((M, N), out_dtype), (tm, tn), lambda i, j, k: (i, j))], epi)[0]


def _mm_residual(name, a, b, x, scale, bias=None, tk=1024):
    M, N = x.shape
    tm, tn = min(512, M), N
    extra = [(x, (tm, tn), lambda i, j, k: (i, j))]
    if bias is not None:
        extra.append((bias, (1, tn), lambda i, j, k: (0, j)))

    def epi(acc, ex, out, i, j):
        r = acc if bias is None else acc + ex[1][...]
        out[0][...] = ex[0][...] + scale * r

    return _matmul(name, a, b, 1, 0, tm, tn, tk, extra,
                   [(jax.ShapeDtypeStruct((M, N), F32), (tm, tn), lambda i, j, k: (i, j))], epi, order="ij")[0]


def _mm_rms_bwd(name, dproj, w_in, x, dy, nw, tk, out_scale):
    M, N = x.shape
    tm = min(512, M)
    extra = [(x, (tm, N), lambda i, j, k: (i, 0)), (dy, (tm, N), lambda i, j, k: (i, 0)),
             (nw, (1, N), lambda i, j, k: (0, 0))]

    def epi(acc, ex, out, i, j):
        dx, dw = _rms_bwd_tile(acc, ex[0][...], ex[1][...], ex[2][...])
        out[0][...] = dx
        out[1][...] = (dx * out_scale).astype(BF16)

        @pl.when(i == 0)
        def _():
            out[2][...] = dw

        @pl.when(i > 0)
        def _():
            out[2][...] += dw

    return _matmul(name, dproj, w_in, 1, 1, tm, N, tk, extra,
                   [(jax.ShapeDtypeStruct((M, N), F32), (tm, N), lambda i, j, k: (i, 0)),
                    (jax.ShapeDtypeStruct((M, N), BF16), (tm, N), lambda i, j, k: (i, 0)),
                    (jax.ShapeDtypeStruct((1, N), F32), (1, N), lambda i, j, k: (0, 0))], epi, order="ij")


def _ffn_gu(name, h, ga, blk):
    T, D = h.shape
    tm = min(T, 1024)
    rs = min(tm, 256)

    def body(h_ref, wg_ref, wu_ref, gu_ref, act_ref):
        for r in range(tm // rs):
            rows = pl.ds(r * rs, rs)
            hv = h_ref[rows, :]
            g = _dot(hv, wg_ref[0, 0], 1, 0)
            u = _dot(hv, wu_ref[0, 0], 1, 0)
            gu_ref[0, 0, rows, :] = g.astype(BF16)
            gu_ref[1, 0, rows, :] = u.astype(BF16)
            act_ref[0, rows, :] = (g * _sigmoid(g) * u).astype(BF16)

    return pl.pallas_call(
        body, grid=(2, T // tm),
        in_specs=[pl.BlockSpec((tm, D), lambda j, i: (i, 0)),
                  pl.BlockSpec((1, 1, D, FF_BLK), lambda j, i: (j, blk, 0, 0)),
                  pl.BlockSpec((1, 1, D, FF_BLK), lambda j, i: (2 + j, blk, 0, 0))],
        out_specs=[pl.BlockSpec((2, 1, tm, FF_BLK), lambda j, i: (0, j, i, 0)),
                   pl.BlockSpec((1, tm, FF_BLK), lambda j, i: (j, i, 0))],
        out_shape=[jax.ShapeDtypeStruct((2, 2, T, FF_BLK), BF16), jax.ShapeDtypeStruct((2, T, FF_BLK), BF16)],
        name=name, compiler_params=_params(2))(h, ga, ga)


def _ffn_down(name, act, wd, x):
    T, D = x.shape
    tm = min(T, 512)

    def body(act_ref, wd_ref, x_ref, o_ref):
        acc = _dot(act_ref[0], wd_ref[0], 1, 0) + _dot(act_ref[1], wd_ref[1], 1, 0)
        o_ref[...] = x_ref[...] + 0.5 * acc

    return pl.pallas_call(
        body, grid=(T // tm,),
        in_specs=[pl.BlockSpec((2, tm, FF_BLK), lambda i: (0, i, 0)),
                  pl.BlockSpec((2, FF_BLK, D), lambda i: (0, 0, 0)),
                  pl.BlockSpec((tm, D), lambda i: (i, 0))],
        out_specs=pl.BlockSpec((tm, D), lambda i: (i, 0)),
        out_shape=jax.ShapeDtypeStruct((T, D), F32), name=name, compiler_params=_params(1))(act, wd, x)


def _ffn_dact(name, dyh, wd, gu):
    T, D = dyh.shape
    tm = min(T, 1024)
    rs = min(tm, 256)

    def body(dy_ref, wd_ref, gu_ref, o_ref):
        for r in range(tm // rs):
            rows = pl.ds(r * rs, rs)
            dact = _dot(dy_ref[rows, :], wd_ref[0], 1, 1)
            g = gu_ref[0, 0, rows, :].astype(F32)
            u = gu_ref[1, 0, rows, :].astype(F32)
            s = _sigmoid(g)
            gs = g * s
            o_ref[0, 0, rows, :] = ((dact * u) * (s + gs * (1.0 - s))).astype(BF16)
            o_ref[1, 0, rows, :] = (dact * gs).astype(BF16)

    return pl.pallas_call(
        body, grid=(2, T // tm),
        in_specs=[pl.BlockSpec((tm, D), lambda j, i: (i, 0)),
                  pl.BlockSpec((1, FF_BLK, D), lambda j, i: (j, 0, 0)),
                  pl.BlockSpec((2, 1, tm, FF_BLK), lambda j, i: (0, j, i, 0))],
        out_specs=pl.BlockSpec((2, 1, tm, FF_BLK), lambda j, i: (0, j, i, 0)),
        out_shape=jax.ShapeDtypeStruct((2, 2, T, FF_BLK), BF16), name=name, compiler_params=_params(2))(dyh, wd, gu)


def _ffn_dwd(name, act, dyh):
    _, T, _ = act.shape
    D = dyh.shape[1]
    tk = min(T, 2048)
    nk = T // tk

    def body(a_ref, d_ref, o_ref, acc_ref):
        k = pl.program_id(1)
        p = _dot(a_ref[0], d_ref[...], 0, 0)

        @pl.when(k == 0)
        def _():
            acc_ref[...] = p

        @pl.when(k > 0)
        def _():
            acc_ref[...] += p

        @pl.when(k == nk - 1)
        def _():
            o_ref[0] = acc_ref[...].astype(BF16)

    return pl.pallas_call(
        body, grid=(2, nk),
        in_specs=[pl.BlockSpec((1, tk, FF_BLK), lambda j, k: (j, k, 0)), pl.BlockSpec((tk, D), lambda j, k: (k, 0))],
        out_specs=pl.BlockSpec((1, FF_BLK, D), lambda j, k: (j, 0, 0)),
        out_shape=jax.ShapeDtypeStruct((2, FF_BLK, D), BF16), scratch_shapes=[pltpu.VMEM((FF_BLK, D), F32)],
        name=name, compiler_params=_params(2))(act, dyh)


def _ffn_dwgu(name, h, dgu, pa, blk):
    T, D = h.shape
    tk = min(T, 2048)
    nk = T // tk

    def body(h_ref, d_ref, pa_in, o_ref, acc_ref):
        k = pl.program_id(1)
        p = _dot(h_ref[...], d_ref[0, 0], 0, 0)

        @pl.when(k == 0)
        def _():
            acc_ref[...] = p

        @pl.when(k > 0)
        def _():
            acc_ref[...] += p

        @pl.when(k == nk - 1)
        def _():
            o_ref[0, 0] = acc_ref[...].astype(BF16)

    return pl.pallas_call(
        body, grid=(4, nk),
        in_specs=[pl.BlockSpec((tk, D), lambda q, k: (k, 0)),
                  pl.BlockSpec((1, 1, tk, FF_BLK), lambda q, k: (q // 2, q % 2, k, 0)), HBM],
        out_specs=pl.BlockSpec((1, 1, D, FF_BLK), lambda q, k: (q, blk, 0, 0)),
        out_shape=jax.ShapeDtypeStruct(pa.shape, BF16), scratch_shapes=[pltpu.VMEM((D, FF_BLK), F32)],
        input_output_aliases={2: 0}, name=name, compiler_params=_params(2))(h, dgu, pa)


def _ffn_dx(name, dgu, ga, blk, x, dy, nw, out_scale):
    T, D = x.shape
    tm = min(T, 512)

    def body(d_ref, w0, w1, w2, w3, x_ref, dy_ref, nw_ref, dx_ref, dxb_ref, dnw_ref):
        i = pl.program_id(0)
        acc = (_dot(d_ref[0, 0], w0[0, 0], 1, 1) + _dot(d_ref[0, 1], w1[0, 0], 1, 1)
               + _dot(d_ref[1, 0], w2[0, 0], 1, 1) + _dot(d_ref[1, 1], w3[0, 0], 1, 1))
        dx, dw = _rms_bwd_tile(acc, x_ref[...], dy_ref[...], nw_ref[...])
        dx_ref[...] = dx
        dxb_ref[...] = (dx * out_scale).astype(BF16)

        @pl.when(i == 0)
        def _():
            dnw_ref[...] = dw

        @pl.when(i > 0)
        def _():
            dnw_ref[...] += dw

    wspec = lambda q: pl.BlockSpec((1, 1, D, FF_BLK), lambda i: (q, blk, 0, 0), pipeline_mode=pl.Buffered(1))
    row = pl.BlockSpec((tm, D), lambda i: (i, 0))
    return pl.pallas_call(
        body, grid=(T // tm,),
        in_specs=[pl.BlockSpec((2, 2, tm, FF_BLK), lambda i: (0, 0, i, 0)), wspec(0), wspec(1), wspec(2), wspec(3),
                  row, row, pl.BlockSpec((1, D), lambda i: (0, 0))],
        out_specs=[row, row, pl.BlockSpec((1, D), lambda i: (0, 0))],
        out_shape=[jax.ShapeDtypeStruct((T, D), F32), jax.ShapeDtypeStruct((T, D), BF16),
                   jax.ShapeDtypeStruct((1, D), F32)],
        name=name, compiler_params=_params(1))(dgu, ga, ga, ga, ga, x, dy, nw)


def _ffn_fwd(tag, x, nw, ga, blk, wd):
    h = _rms_fwd(tag + "_rms", x, nw)
    gu, act = _ffn_gu(tag + "_gu", h, ga, blk)
    return _ffn_down(tag + "_down", act, wd, x), (h, gu, act)


def _ffn_bwd(tag, dy, dyh, x, nw, ga, blk, wd, saved, pa, out_scale):
    h, gu, act = saved
    dgu = _ffn_dact(tag + "_dact", dyh, wd, gu)
    d_wd = _ffn_dwd(tag + "_dwd", act, dyh)
    pa = _ffn_dwgu(tag + "_dwgu", h, dgu, pa, blk)
    dx, dxb, d_nw = _ffn_dx(tag + "_dx", dgu, ga, blk, x, dy, nw, out_scale)
    return dx, dxb, d_nw, pa, d_wd


def _conv_taps(cur, halo, w, first, sign):
    tt = cur.shape[0]
    halo = jnp.where(first, 0.0, halo)
    rid = lax.broadcasted_iota(jnp.int32, (8, cur.shape[1]), 0)
    acc = w[3:4, :] * cur
    for s in (1, 2, 3):
        if sign < 0:
            sh = pltpu.roll(cur, s, 0)
            edge = jnp.where(rid < s, pltpu.roll(halo, s, 0), sh[0:8])
            sh = jnp.concatenate([edge, sh[8:]], axis=0) if tt > 8 else edge
        else:
            sh = pltpu.roll(cur, tt - s, 0)
            edge = jnp.where(rid >= 8 - s, pltpu.roll(halo, 8 - s, 0), sh[tt - 8:])
            sh = jnp.concatenate([sh[:tt - 8], edge], axis=0) if tt > 8 else edge
        acc = acc + w[3 - s:4 - s, :] * sh
    return acc


def _gdn_prep(name, proj, wconv, gate_p):
    T = proj.shape[0]
    tt = min(T, 256)
    hb = tt // 8
    nch = tt // A_CHUNK

    def body(cur_ref, halo_ref, ba_ref, w_ref, gp_ref, qkv_ref, bg_ref, gc_ref):
        first = pl.program_id(0) == 0
        for c in range(24):
            cols = pl.ds(c * 128, 128)
            conv = _conv_taps(cur_ref[:, cols], halo_ref[:, cols], w_ref[:, cols], first, -1)
            y = conv * _sigmoid(conv)
            if c < 16:
                y = y * lax.rsqrt(jnp.sum(y * y, axis=-1, keepdims=True) + EPS)
                if c < 8:
                    y = y * (A_DK ** -0.5)
            qkv_ref[:, cols] = y
        ba = ba_ref[...]
        lane = lax.broadcasted_iota(jnp.int32, ba.shape, 1)
        zarg = ba + gp_ref[1:2, :]
        softplus = jnp.maximum(zarg, 0.0) + jnp.log(1.0 + jnp.exp(-jnp.abs(zarg)))
        bg = jnp.where(lane < 8, _sigmoid(ba), jnp.where(lane < 16, gp_ref[0:1, :] * softplus, 0.0))
        bg_ref[...] = bg
        tri = (lax.broadcasted_iota(jnp.int32, (A_CHUNK, A_CHUNK), 0)
               >= lax.broadcasted_iota(jnp.int32, (A_CHUNK, A_CHUNK), 1)).astype(F32)
        for c in range(nch):
            rows = pl.ds(c * A_CHUNK, A_CHUNK)
            gc_ref[rows, :] = _dotx(tri, bg[c * A_CHUNK:(c + 1) * A_CHUNK, :])

    return pl.pallas_call(
        body, grid=(T // tt,),
        in_specs=[pl.BlockSpec((tt, 3072), lambda i: (i, 0)),
                  pl.BlockSpec((8, 3072), lambda i: (jnp.maximum(i * hb - 1, 0), 0)),
                  pl.BlockSpec((tt, 128), lambda i: (i, 32)),
                  pl.BlockSpec((4, 3072), lambda i: (0, 0)),
                  pl.BlockSpec((2, 128), lambda i: (0, 0))],
        out_specs=[pl.BlockSpec((tt, 3072), lambda i: (i, 0)), pl.BlockSpec((tt, 128), lambda i: (i, 0)),
                   pl.BlockSpec((tt, 128), lambda i: (i, 0))],
        out_shape=[jax.ShapeDtypeStruct((T, 3072), F32), jax.ShapeDtypeStruct((T, 128), F32),
                   jax.ShapeDtypeStruct((T, 128), F32)],
        name=name, compiler_params=_params(1))(proj, proj, proj, wconv, gate_p)


def _chunk_masks():
    ri = lax.broadcasted_iota(jnp.int32, (A_CHUNK, A_CHUNK), 0)
    ci = lax.broadcasted_iota(jnp.int32, (A_CHUNK, A_CHUNK), 1)
    return ri >= ci, ri > ci, ri == ci


def _chunk_local(q, k, gcol, grow, bcol):
    incl, strict, _ = _chunk_masks()
    dec = jnp.where(incl, jnp.exp(jnp.where(incl, gcol - grow, 0.0)), 0.0)
    e = jnp.exp(gcol)
    glast = grow[:, A_CHUNK - 1:A_CHUNK]
    f = jnp.exp(glast - gcol)
    gl = jnp.exp(glast)
    kb = k * bcol
    lmat = jnp.where(strict, _dotb(kb, k, 1, 1) * dec, 0.0)
    amat = jnp.where(incl, _dotb(q, k, 1, 1) * dec, 0.0)
    return dec, e, f, gl, kb, lmat, amat


def _unit_lower_inverse(lmats):
    _, _, eye = _chunk_masks()
    ts = [jnp.where(eye, 1.0, 0.0) - lm for lm in lmats]
    lps = [_doth(lm, lm) for lm in lmats]
    for it in range(5):
        ts = [t + _doth(t, lp) for t, lp in zip(ts, lps)]
        if it < 4:
            lps = [_doth(lp, lp) for lp in lps]
    return ts


def _gdn_delta_fwd(name, qkv, gcol, grow, bcol):
    T = qkv.shape[0]
    tt = min(T, 512)
    nch = tt // A_CHUNK
    NC = T // A_CHUNK
    wd, ng = 128 * A_HG, A_HEADS // A_HG

    def body(q_ref, k_ref, v_ref, gc_ref, gr_ref, bc_ref, o_ref, s_ref, t_ref, u_ref, w_ref, state):
        @pl.when(pl.program_id(1) == 0)
        def _():
            state[...] = jnp.zeros_like(state)

        def chunk(c, carry):
            rows = pl.ds(pl.multiple_of(c * A_CHUNK, A_CHUNK), A_CHUNK)
            hs = range(A_HG)
            cols = [pl.ds(h * 128, 128) for h in hs]
            q = [q_ref[rows, cols[h]] for h in hs]
            k = [k_ref[rows, cols[h]] for h in hs]
            v = [v_ref[rows, cols[h]] for h in hs]
            bcl = [bc_ref[h, rows, :] for h in hs]
            loc = [_chunk_local(q[h], k[h], gc_ref[h, rows, :], gr_ref[h, c], bcl[h]) for h in hs]
            e, f, gl, kb, amat = ([l[i] for l in loc] for i in (1, 2, 3, 4, 6))
            tinv = _unit_lower_inverse([l[5] for l in loc])
            u = [_doth(tinv[h], v[h] * bcl[h]) for h in hs]
            w = [_doth(tinv[h], kb[h] * e[h]) for h in hs]
            s = [state[h] for h in hs]
            vn = [u[h] - _dotb(w[h], s[h]) for h in hs]
            o_s = [_dotb(q[h] * e[h], s[h]) for h in hs]
            o_a = [_dotb(amat[h], vn[h]) for h in hs]
            s_new = [s[h] * gl[h] + _dotb(k[h] * f[h], vn[h], 0, 0) for h in hs]
            for h in hs:
                s_ref[h, c] = s[h].astype(BF16)
                t_ref[h, c] = tinv[h]
                u_ref[rows, cols[h]] = u[h]
                w_ref[rows, cols[h]] = w[h]
                o_ref[rows, cols[h]] = o_s[h] + o_a[h]
                state[h] = s_new[h]
            return carry

        lax.fori_loop(0, nch, chunk, 0)

    hd = lambda col: pl.BlockSpec((tt, wd), lambda h, i: (i, col * ng + h))
    col_spec = pl.BlockSpec((A_HG, tt, 1), lambda h, i: (h, i, 0))
    return pl.pallas_call(
        body, grid=(ng, T // tt),
        in_specs=[hd(0), hd(1), hd(2), col_spec, pl.BlockSpec((A_HG, nch, 1, A_CHUNK), lambda h, i: (h, i, 0, 0)), col_spec],
        out_specs=[pl.BlockSpec((tt, wd), lambda h, i: (i, h)),
                   pl.BlockSpec((A_HG, nch, 128, 128), lambda h, i: (h, i, 0, 0)),
                   pl.BlockSpec((A_HG, nch, A_CHUNK, A_CHUNK), lambda h, i: (h, i, 0, 0)),
                   pl.BlockSpec((tt, wd), lambda h, i: (i, h)), pl.BlockSpec((tt, wd), lambda h, i: (i, h))],
        out_shape=[jax.ShapeDtypeStruct((T, 1024), F32), jax.ShapeDtypeStruct((A_HEADS, NC, 128, 128), BF16),
                   jax.ShapeDtypeStruct((A_HEADS, NC, A_CHUNK, A_CHUNK), F32),
                   jax.ShapeDtypeStruct((T, 1024), F32), jax.ShapeDtypeStruct((T, 1024), F32)],
        scratch_shapes=[pltpu.VMEM((A_HG, 128, 128), F32)],
        name=name, compiler_params=_params(2))(qkv, qkv, qkv, gcol, grow, bcol)


def _gdn_delta_bwd(name, qkv, gcol, grow, bcol, d_o, s_sv, t_sv, u_sv, w_sv):
    T = qkv.shape[0]
    tt = min(T, 512)
    nch = tt // A_CHUNK
    ni = T // tt

    def body(q_ref, k_ref, v_ref, gc_ref, gr_ref, bc_ref, do_ref, s_ref, t_ref, u_ref, w_ref,
             dq_ref, dk_ref, dv_ref, dg_ref, db_ref, dstate):
        @pl.when(pl.program_id(1) == 0)
        def _():
            dstate[...] = jnp.zeros_like(dstate)

        incl, strict, _ = _chunk_masks()
        upper = (lax.broadcasted_iota(jnp.int32, (A_CHUNK, A_CHUNK), 0)
                 <= lax.broadcasted_iota(jnp.int32, (A_CHUNK, A_CHUNK), 1)).astype(F32)
        last_row = lax.broadcasted_iota(jnp.int32, (A_CHUNK, 1), 0) == A_CHUNK - 1
        ones = jnp.ones((A_CHUNK, 128), F32)

        rsum = lambda x: jnp.sum(x, axis=1, keepdims=True)

        def chunk(cc, carry):
            c = nch - 1 - cc
            rows = pl.ds(pl.multiple_of(c * A_CHUNK, A_CHUNK), A_CHUNK)
            hs = range(A_HG)
            cols = [pl.ds(h * 128, 128) for h in hs]
            q = [q_ref[rows, cols[h]] for h in hs]
            k = [k_ref[rows, cols[h]] for h in hs]
            v = [v_ref[rows, cols[h]] for h in hs]
            do = [do_ref[rows, cols[h]] for h in hs]
            u = [u_ref[rows, cols[h]] for h in hs]
            w = [w_ref[rows, cols[h]] for h in hs]
            bcl = [bc_ref[h, rows, :] for h in hs]
            s = [s_ref[h, c] for h in hs]
            tinv = [t_ref[h, c] for h in hs]
            ds = [dstate[h] for h in hs]
            loc = [_chunk_local(q[h], k[h], gc_ref[h, rows, :], gr_ref[h, c], bcl[h]) for h in hs]
            dec, e, f, gl, kb, lmat, amat = ([l[i] for l in loc] for i in range(7))
            qd = [q[h] * e[h] for h in hs]
            kd = [k[h] * f[h] for h in hs]
            ke = [kb[h] * e[h] for h in hs]
            vn = [u[h] - _dotb(w[h], s[h]) for h in hs]
            d_qd = [_dotb(do[h], s[h], 1, 1) for h in hs]
            d_a = [jnp.where(incl, _dotb(do[h], vn[h], 1, 1), 0.0) for h in hs]
            d_vn1 = [_dotb(amat[h], do[h], 0, 0) for h in hs]
            d_vn = [d_vn1[h] + _dotb(kd[h], ds[h]) for h in hs]
            d_kd = [_dotb(vn[h], ds[h], 1, 1) for h in hs]
            d_w = [-_dotb(d_vn[h], s[h], 1, 1) for h in hs]
            ds_q = [_dotb(qd[h], do[h], 0, 0) for h in hs]
            ds_w = [_dotb(w[h], d_vn[h], 0, 0) for h in hs]
            d_bv = [_doth(tinv[h], d_vn[h], 0, 0) for h in hs]
            d_ke = [_doth(tinv[h], d_w[h], 0, 0) for h in hs]
            d_l1 = [_dotb(d_bv[h], u[h], 1, 1) for h in hs]
            d_l = [-jnp.where(strict, d_l1[h] + _dotb(d_ke[h], w[h], 1, 1), 0.0) for h in hs]
            d_kk = [d_l[h] * dec[h] for h in hs]
            d_qk = [d_a[h] * dec[h] for h in hs]
            d_kb = [_dotb(d_kk[h], k[h]) for h in hs]
            dk1 = [_dotb(d_kk[h], kb[h], 0, 0) for h in hs]
            dk2 = [_dotb(d_qk[h], q[h], 0, 0) for h in hs]
            dq1 = [_dotb(d_qk[h], k[h]) for h in hs]
            m = [d_l[h] * lmat[h] + d_a[h] * amat[h] for h in hs]
            col_m = [_dotx(m[h], ones, 0, 0)[:, 0:1] for h in hs]
            d_gc = []
            for h in hs:
                d_gl = jnp.sum(jnp.sum(ds[h] * s[h].astype(F32), axis=1, keepdims=True), axis=0, keepdims=True)
                r_kd = rsum(d_kd[h] * kd[h])
                tail = jnp.sum(r_kd, axis=0, keepdims=True) + d_gl * gl[h]
                d_gc.append(rsum(m[h]) - col_m[h] + rsum(d_qd[h] * qd[h]) - r_kd + rsum(d_ke[h] * ke[h])
                            + jnp.where(last_row, tail, 0.0))
            dg = [_dotx(upper, d_gc[h] * ones)[:, 0:1] for h in hs]
            for h in hs:
                dstate[h] = gl[h] * ds[h] + ds_q[h] - ds_w[h]
                dk_ref[rows, cols[h]] = (dk1[h] + dk2[h] + d_kd[h] * f[h] + d_ke[h] * (bcl[h] * e[h])
                                         + d_kb[h] * bcl[h])
                dq_ref[rows, cols[h]] = dq1[h] + d_qd[h] * e[h]
                dv_ref[rows, cols[h]] = d_bv[h] * bcl[h]
                db_ref[h, rows, :] = rsum(d_ke[h] * k[h]) * e[h] + rsum(d_kb[h] * k[h]) + rsum(d_bv[h] * v[h])
                dg_ref[h, rows, :] = dg[h]
            return carry

        lax.fori_loop(0, nch, chunk, 0)

    wd, ng = 128 * A_HG, A_HEADS // A_HG
    rev = lambda i: ni - 1 - i
    hd = lambda col: pl.BlockSpec((tt, wd), lambda h, i: (rev(i), col * ng + h))
    hd1 = pl.BlockSpec((tt, wd), lambda h, i: (rev(i), h))
    col_spec = pl.BlockSpec((A_HG, tt, 1), lambda h, i: (h, rev(i), 0))
    return pl.pallas_call(
        body, grid=(ng, ni),
        in_specs=[hd(0), hd(1), hd(2), col_spec,
                  pl.BlockSpec((A_HG, nch, 1, A_CHUNK), lambda h, i: (h, rev(i), 0, 0)), col_spec, hd1,
                  pl.BlockSpec((A_HG, nch, 128, 128), lambda h, i: (h, rev(i), 0, 0)),
                  pl.BlockSpec((A_HG, nch, A_CHUNK, A_CHUNK), lambda h, i: (h, rev(i), 0, 0)), hd1, hd1],
        out_specs=[hd1, hd1, hd1, col_spec, col_spec],
        out_shape=[jax.ShapeDtypeStruct((T, 1024), F32)] * 3 + [jax.ShapeDtypeStruct((A_HEADS, T, 1), F32)] * 2,
        scratch_shapes=[pltpu.VMEM((A_HG, 128, 128), F32)],
        name=name, compiler_params=_params(2))(qkv, qkv, qkv, gcol, grow, bcol, d_o, s_sv, t_sv, u_sv, w_sv)


def _gdn_gate_fwd(name, o, proj, nw):
    T = o.shape[0]
    tt = min(T, 512)

    def body(o_ref, z_ref, nw_ref, y_ref):
        for h in range(A_HEADS):
            cols = pl.ds(h * 128, 128)
            ov, z = o_ref[:, cols], z_ref[:, cols]
            r = lax.rsqrt(jnp.mean(ov * ov, axis=-1, keepdims=True) + EPS)
            y_ref[:, cols] = (ov * r * nw_ref[...] * (z * _sigmoid(z))).astype(BF16)

    return pl.pallas_call(
        body, grid=(T // tt,),
        in_specs=[pl.BlockSpec((tt, 1024), lambda i: (i, 0)), pl.BlockSpec((tt, 1024), lambda i: (i, 3)),
                  pl.BlockSpec((1, 128), lambda i: (0, 0))],
        out_specs=pl.BlockSpec((tt, 1024), lambda i: (i, 0)),
        out_shape=jax.ShapeDtypeStruct((T, 1024), BF16), name=name, compiler_params=_params(1))(o, proj, nw)


def _gdn_gate_bwd(name, dy2, o, proj, nw):
    T = o.shape[0]
    tt = min(T, 512)

    def body(dy_ref, o_ref, z_ref, nw_ref, do_ref, dz_ref, dnw_ref):
        dnw = jnp.zeros((1, 128), F32)
        for h in range(A_HEADS):
            cols = pl.ds(h * 128, 128)
            dy, ov, z = dy_ref[:, cols], o_ref[:, cols], z_ref[:, cols]
            s = _sigmoid(z)
            sz = z * s
            r = lax.rsqrt(jnp.mean(ov * ov, axis=-1, keepdims=True) + EPS)
            xhat = ov * r
            dn = dy * sz
            dz_ref[:, cols] = dy * (xhat * nw_ref[...]) * (s + z * s * (1.0 - s))
            dxhat = dn * nw_ref[...]
            do_ref[:, cols] = r * (dxhat - xhat * jnp.mean(dxhat * xhat, axis=-1, keepdims=True))
            dnw = dnw + jnp.sum(dn * xhat, axis=0, keepdims=True)

        @pl.when(pl.program_id(0) == 0)
        def _():
            dnw_ref[...] = dnw

        @pl.when(pl.program_id(0) > 0)
        def _():
            dnw_ref[...] += dnw

    blk = lambda c: pl.BlockSpec((tt, 1024), lambda i: (i, c))
    return pl.pallas_call(
        body, grid=(T // tt,),
        in_specs=[blk(0), blk(0), blk(3), pl.BlockSpec((1, 128), lambda i: (0, 0))],
        out_specs=[blk(0), blk(0), pl.BlockSpec((1, 128), lambda i: (0, 0))],
        out_shape=[jax.ShapeDtypeStruct((T, 1024), F32), jax.ShapeDtypeStruct((T, 1024), F32),
                   jax.ShapeDtypeStruct((1, 128), F32)],
        name=name, compiler_params=_params(1))(dy2, o, proj, nw)


def _gdn_prep_bwd1(name, proj, wconv, gate_p, dq, dk, dv, dbg):
    T = proj.shape[0]
    tt = min(T, 256)
    hb = tt // 8

    def body(cur_ref, halo_ref, ba_ref, w_ref, gp_ref, dq_ref, dk_ref, dv_ref, dbg_ref,
             dc_ref, dw_ref, dba_ref, dgp_ref):
        first = pl.program_id(0) == 0
        rid = lax.broadcasted_iota(jnp.int32, (8, 128), 0)
        for c in range(24):
            cols = pl.ds(c * 128, 128)
            cur = cur_ref[:, cols]
            halo = jnp.where(first, 0.0, halo_ref[:, cols])
            conv = _conv_taps(cur, halo_ref[:, cols], w_ref[:, cols], first, -1)
            s = _sigmoid(conv)
            y = conv * s
            if c < 16:
                dref = dq_ref if c < 8 else dk_ref
                dn = dref[:, pl.ds((c % 8) * 128, 128)]
                rinv = lax.rsqrt(jnp.sum(y * y, axis=-1, keepdims=True) + EPS)
                yhat = y * rinv
                dyv = rinv * (dn - yhat * jnp.sum(dn * yhat, axis=-1, keepdims=True))
                if c < 8:
                    dyv = dyv * (A_DK ** -0.5)
            else:
                dyv = dv_ref[:, pl.ds((c - 16) * 128, 128)]
            dc = dyv * (s + conv * s * (1.0 - s))
            dc_ref[:, cols] = dc
            parts = [jnp.sum(dc * cur, axis=0, keepdims=True)]
            for sft in (1, 2, 3):
                sh = pltpu.roll(cur, sft, 0)
                edge = jnp.where(rid < sft, pltpu.roll(halo, sft, 0), sh[0:8])
                sh = jnp.concatenate([edge, sh[8:]], axis=0) if tt > 8 else edge
                parts.append(jnp.sum(dc * sh, axis=0, keepdims=True))
            dwc = jnp.concatenate(parts[::-1], axis=0)

            @pl.when(first)
            def _():
                dw_ref[:, cols] = dwc

            @pl.when(jnp.logical_not(first))
            def _():
                dw_ref[:, cols] += dwc

        ba = ba_ref[...]
        dbg = dbg_ref[...]
        lane = lax.broadcasted_iota(jnp.int32, ba.shape, 1)
        sb = _sigmoid(ba)
        zarg = ba + gp_ref[1:2, :]
        softplus = jnp.maximum(zarg, 0.0) + jnp.log(1.0 + jnp.exp(-jnp.abs(zarg)))
        d_b = dbg * sb * (1.0 - sb)
        d_a = dbg * gp_ref[0:1, :] * _sigmoid(zarg)
        dba_ref[...] = jnp.where(lane < 8, d_b, jnp.where(lane < 16, d_a, 0.0))
        g = gp_ref[0:1, :] * softplus
        in_a = (lane >= 8) & (lane < 16)
        sums = jnp.concatenate([jnp.sum(jnp.where(in_a, dbg * g, 0.0), axis=0, keepdims=True),
                                jnp.sum(jnp.where(in_a, d_a, 0.0), axis=0, keepdims=True)], axis=0)

        @pl.when(first)
        def _():
            dgp_ref[...] = sums

        @pl.when(jnp.logical_not(first))
        def _():
            dgp_ref[...] += sums

    row = lambda w, c=0: pl.BlockSpec((tt, w), lambda i: (i, c))
    return pl.pallas_call(
        body, grid=(T // tt,),
        in_specs=[row(3072), pl.BlockSpec((8, 3072), lambda i: (jnp.maximum(i * hb - 1, 0), 0)), row(128, 32),
                  pl.BlockSpec((4, 3072), lambda i: (0, 0)), pl.BlockSpec((2, 128), lambda i: (0, 0)),
                  row(1024), row(1024), row(1024), row(128)],
        out_specs=[row(3072), pl.BlockSpec((4, 3072), lambda i: (0, 0)), row(128),
                   pl.BlockSpec((2, 128), lambda i: (0, 0))],
        out_shape=[jax.ShapeDtypeStruct((T, 3072), F32), jax.ShapeDtypeStruct((4, 3072), F32),
                   jax.ShapeDtypeStruct((T, 128), F32), jax.ShapeDtypeStruct((2, 128), F32)],
        name=name, compiler_params=_params(1))(proj, proj, proj, wconv, gate_p, dq, dk, dv, dbg)


def _gdn_prep_bwd2(name, dc, wconv, dz, dba):
    T = dc.shape[0]
    tt = min(T, 256)
    hb = tt // 8
    ni = T // tt

    def body(cur_ref, halo_ref, w_ref, dz_ref, dba_ref, o_ref):
        last = pl.program_id(0) == ni - 1
        for c in range(24):
            cols = pl.ds(c * 128, 128)
            o_ref[:, cols] = _conv_taps(cur_ref[:, cols], halo_ref[:, cols], w_ref[:, cols], last, +1)
        o_ref[:, pl.ds(3072, 1024)] = dz_ref[...]
        o_ref[:, pl.ds(4096, 128)] = dba_ref[...]

    return pl.pallas_call(
        body, grid=(ni,),
        in_specs=[pl.BlockSpec((tt, 3072), lambda i: (i, 0)),
                  pl.BlockSpec((8, 3072), lambda i: (jnp.minimum((i + 1) * hb, T // 8 - 1), 0)),
                  pl.BlockSpec((4, 3072), lambda i: (0, 0)),
                  pl.BlockSpec((tt, 1024), lambda i: (i, 0)), pl.BlockSpec((tt, 128), lambda i: (i, 0))],
        out_specs=pl.BlockSpec((tt, A_IN_PAD), lambda i: (i, 0)),
        out_shape=jax.ShapeDtypeStruct((T, A_IN_PAD), F32), name=name, compiler_params=_params(1))(
            dc, dc, wconv, dz, dba)


def _gdn_fwd(x, nw, w_in, wconv, gate_p, out_nw, w_out):
    T = x.shape[0]
    h = _rms_fwd("a_rms", x, nw)
    proj = _mm_plain("a_proj", h, w_in, 1, 0, F32, tn=FF_BLK)
    qkv, bg, gcum = _gdn_prep("a_prep", proj, wconv, gate_p)
    bcol = bg[:, 0:8].T.reshape(A_HEADS, T, 1)
    gcol = gcum[:, 8:16].T.reshape(A_HEADS, T, 1)
    grow = gcol.reshape(A_HEADS, T // A_CHUNK, 1, A_CHUNK)
    o, s_sv, t_sv, u_sv, w_sv = _gdn_delta_fwd("a_delta", qkv, gcol, grow, bcol)
    o2 = _gdn_gate_fwd("a_gate", o, proj, out_nw)
    y = _mm_residual("a_out", o2, w_out, x, 1.0)
    return y, (h, proj, qkv, gcol, grow, bcol, o, s_sv, t_sv, u_sv, w_sv, o2)


def _gdn_bwd(dy, dyb, x, nw, w_in, wconv, gate_p, out_nw, w_out, saved, out_scale):
    h, proj, qkv, gcol, grow, bcol, o, s_sv, t_sv, u_sv, w_sv, o2 = saved
    T = x.shape[0]
    d_o2 = _mm_plain("a_dout", dyb, w_out, 1, 1, F32)
    d_wout = _mm_plain("a_dwout", o2, dyb, 0, 0, F32)
    d_o, d_z, d_outnw = _gdn_gate_bwd("a_dgate", d_o2, o, proj, out_nw)
    dq, dk, dv, dg, dbeta = _gdn_delta_bwd("a_ddelta", qkv, gcol, grow, bcol, d_o, s_sv, t_sv, u_sv, w_sv)
    dbg = jnp.concatenate([dbeta.reshape(A_HEADS, T).T, dg.reshape(A_HEADS, T).T, jnp.zeros((T, 112), F32)], axis=1)
    dc, d_wconv, dba, dgp = _gdn_prep_bwd1("a_dprep1", proj, wconv, gate_p, dq, dk, dv, dbg)
    dproj = _gdn_prep_bwd2("a_dprep2", dc, wconv, d_z, dba)
    d_win = _mm_plain("a_dwin", h, dproj, 0, 0, F32, tn=FF_BLK)
    dx, dxb, d_nw = _mm_rms_bwd("a_dx", dproj, w_in, x, dy, nw, FF_BLK, out_scale)
    return dx, dxb, d_nw, d_win, d_wconv, dgp, d_outnw, d_wout


def _swa_masks(n):
    qi = lax.broadcasted_iota(jnp.int32, (B_BLK, B_BLK), 0)
    kj = lax.broadcasted_iota(jnp.int32, (B_BLK, B_BLK), 1)
    return kj > qi + jnp.where(n > 0, 0, B_BLK), kj <= qi


def _swa_fwd(name, q, k, v, sinks):
    T = q.shape[1]
    tq = min(T, 1024)
    nbt = tq // B_BLK
    scale = B_HD ** -0.5
    G = B_HEADS // B_KV

    def body(q_ref, k_ref, v_ref, kh_ref, vh_ref, s_ref, o_ref, l_ref):
        first_blk = pl.program_id(1) * nbt

        def block(n, kp, vp):
            m_prev, m_cur = _swa_masks(first_blk + n)
            cur = pl.ds(pl.multiple_of(n * B_BLK, B_BLK), B_BLK)
            kc, vc = k_ref[0, cur, :], v_ref[0, cur, :]
            gs = range(G)
            rmax = lambda a: jnp.max(a, axis=1, keepdims=True)
            rsum = lambda a: jnp.sum(a, axis=1, keepdims=True)
            sink = [s_ref[g][:, 0:1] for g in gs]
            qb = [q_ref[g, cur, :] for g in gs]
            s_p = [jnp.where(m_prev, _dot(qb[g], kp, 1, 1) * scale, -jnp.inf) for g in gs]
            s_c = [jnp.where(m_cur, _dot(qb[g], kc, 1, 1) * scale, -jnp.inf) for g in gs]
            m = [jnp.maximum(jnp.maximum(rmax(s_p[g]), rmax(s_c[g])), sink[g]) for g in gs]
            p_p = [jnp.exp(s_p[g] - m[g]) for g in gs]
            p_c = [jnp.exp(s_c[g] - m[g]) for g in gs]
            den = [rsum(p_p[g]) + rsum(p_c[g]) + jnp.exp(sink[g] - m[g]) for g in gs]
            a_p = [_dotb(p_p[g], vp) for g in gs]
            a_c = [_dotb(p_c[g], vc) for g in gs]
            for g in gs:
                o_ref[g, cur, :] = ((a_p[g] + a_c[g]) / den[g]).astype(BF16)
                l_ref[g, cur, :] = m[g] + jnp.log(den[g])

        block(0, kh_ref[0], vh_ref[0])

        def rest(n, carry):
            prv = pl.ds(pl.multiple_of((n - 1) * B_BLK, B_BLK), B_BLK)
            block(n, k_ref[0, prv, :], v_ref[0, prv, :])
            return carry

        lax.fori_loop(1, nbt, rest, 0)

    qs = pl.BlockSpec((G, tq, B_HD), lambda kv, i: (kv, i, 0))
    ks = pl.BlockSpec((1, tq, B_HD), lambda kv, i: (kv, i, 0))
    halo = pl.BlockSpec((1, B_BLK, B_HD), lambda kv, i: (kv, jnp.maximum(i * nbt - 1, 0), 0))
    return pl.pallas_call(
        body, grid=(B_KV, T // tq),
        in_specs=[qs, ks, ks, halo, halo, pl.BlockSpec((G, 1, 128), lambda kv, i: (kv, 0, 0))],
        out_specs=[qs, pl.BlockSpec((G, tq, 1), lambda kv, i: (kv, i, 0))],
        out_shape=[jax.ShapeDtypeStruct((B_HEADS, T, B_HD), BF16), jax.ShapeDtypeStruct((B_HEADS, T, 1), F32)],
        name=name, compiler_params=_params(2))(q, k, v, k, v, sinks)


def _swa_bwd(name, q, k, v, sinks, o, lse, do):
    T = q.shape[1]
    tq = min(T, 1024)
    nbt, ni = tq // B_BLK, T // tq
    scale = B_HD ** -0.5
    G = B_HEADS // B_KV

    def body(q_ref, k_ref, v_ref, kh_ref, vh_ref, s_ref, o_ref, l_ref, do_ref, dq_ref, dk_ref, dv_ref, ds_ref,
             dk_halo, dv_halo):
        step = pl.program_id(1)
        first_blk = (ni - 1 - step) * nbt
        last = pl.ds(tq - B_BLK, B_BLK)
        dk_ref[...] = jnp.zeros_like(dk_ref)
        dv_ref[...] = jnp.zeros_like(dv_ref)

        @pl.when(step > 0)
        def _():
            dk_ref[0, last, :] = dk_halo[...]
            dv_ref[0, last, :] = dv_halo[...]

        def block(n, kp, vp, dsinks):
            m_prev, m_cur = _swa_masks(first_blk + n)
            cur = pl.ds(pl.multiple_of(n * B_BLK, B_BLK), B_BLK)
            kc, vc = k_ref[0, cur, :], v_ref[0, cur, :]
            gs = range(G)
            sink = [s_ref[g][:, 0:1] for g in gs]
            qb = [q_ref[g, cur, :] for g in gs]
            dob = [do_ref[g, cur, :] for g in gs]
            lse_b = [l_ref[g, cur, :] for g in gs]
            p_p = [jnp.where(m_prev, jnp.exp(_dot(qb[g], kp, 1, 1) * scale - lse_b[g]), 0.0) for g in gs]
            p_c = [jnp.where(m_cur, jnp.exp(_dot(qb[g], kc, 1, 1) * scale - lse_b[g]), 0.0) for g in gs]
            delta = [jnp.sum(dob[g].astype(F32) * o_ref[g, cur, :].astype(F32), axis=1, keepdims=True) for g in gs]
            ds_p = [p_p[g] * (_dot(dob[g], vp, 1, 1) - delta[g]) for g in gs]
            ds_c = [p_c[g] * (_dot(dob[g], vc, 1, 1) - delta[g]) for g in gs]
            dq_p = [_dotb(ds_p[g], kp) for g in gs]
            dq_c = [_dotb(ds_c[g], kc) for g in gs]
            dk_ps = [_dotb(ds_p[g], qb[g], 0, 0) for g in gs]
            dk_cs = [_dotb(ds_c[g], qb[g], 0, 0) for g in gs]
            dv_ps = [_dotb(p_p[g], dob[g], 0, 0) for g in gs]
            dv_cs = [_dotb(p_c[g], dob[g], 0, 0) for g in gs]
            for g in gs:
                dq_ref[g, cur, :] = (dq_p[g] + dq_c[g]) * scale
            out = tuple(dsinks[g] - jnp.sum(jnp.exp(sink[g] - lse_b[g]) * delta[g], axis=0, keepdims=True) for g in gs)
            total = lambda parts: (parts[0] + parts[1]) + (parts[2] + parts[3])
            dk_ref[0, cur, :] += total(dk_cs) * scale
            dv_ref[0, cur, :] += total(dv_cs)
            return total(dk_ps) * scale, total(dv_ps), out

        zeros = tuple(jnp.zeros((1, 1), F32) for _ in range(G))
        dk_p, dv_p, dsinks = block(0, kh_ref[0], vh_ref[0], zeros)
        dk_halo[...] = dk_p
        dv_halo[...] = dv_p

        def rest(n, dsinks):
            prv = pl.ds(pl.multiple_of((n - 1) * B_BLK, B_BLK), B_BLK)
            dk_p, dv_p, dsinks = block(n, k_ref[0, prv, :], v_ref[0, prv, :], dsinks)
            dk_ref[0, prv, :] += dk_p
            dv_ref[0, prv, :] += dv_p
            return dsinks

        dsinks = lax.fori_loop(1, nbt, rest, dsinks)
        for g in range(G):
            row = jnp.broadcast_to(dsinks[g], (1, 128))

            @pl.when(step == 0)
            def _():
                ds_ref[g] = row

            @pl.when(step > 0)
            def _():
                ds_ref[g] += row

    rev = lambda i: ni - 1 - i
    qs = pl.BlockSpec((G, tq, B_HD), lambda kv, i: (kv, rev(i), 0))
    ks = pl.BlockSpec((1, tq, B_HD), lambda kv, i: (kv, rev(i), 0))
    halo = pl.BlockSpec((1, B_BLK, B_HD), lambda kv, i: (kv, jnp.maximum(rev(i) * nbt - 1, 0), 0))
    ss = pl.BlockSpec((G, 1, 128), lambda kv, i: (kv, 0, 0))
    return pl.pallas_call(
        body, grid=(B_KV, ni),
        in_specs=[qs, ks, ks, halo, halo, ss, qs, pl.BlockSpec((G, tq, 1), lambda kv, i: (kv, rev(i), 0)), qs],
        out_specs=[qs, ks, ks, ss],
        out_shape=[jax.ShapeDtypeStruct((B_HEADS, T, B_HD), F32), jax.ShapeDtypeStruct((B_KV, T, B_HD), F32),
                   jax.ShapeDtypeStruct((B_KV, T, B_HD), F32), jax.ShapeDtypeStruct((B_HEADS, 1, 128), F32)],
        scratch_shapes=[pltpu.VMEM((B_BLK, B_HD), F32), pltpu.VMEM((B_BLK, B_HD), F32)],
        name=name, compiler_params=_params(2))(q, k, v, k, v, sinks, o, lse, do)


def _split_heads(a, n):
    T = a.shape[0]
    return a.reshape(T, n, B_HD).transpose(1, 0, 2)


def _merge_heads(a):
    n, T, _ = a.shape
    return a.transpose(1, 0, 2).reshape(T, n * B_HD)


def _swa_mixer_fwd(x, nw, w_in, b_in, sinks, w_out, b_out):
    h = _rms_fwd("b_rms", x, nw)
    proj = _mm_plain("b_proj", h, w_in, 1, 0, BF16, tn=768, bias=b_in)
    q, k, v = _split_heads(proj[:, :1024], B_HEADS), _split_heads(proj[:, 1024:1280], B_KV), _split_heads(proj[:, 1280:], B_KV)
    o, lse = _swa_fwd("b_attn", q, k, v, sinks)
    om = _merge_heads(o)
    y = _mm_residual("b_out", om, w_out, x, 1.0, bias=b_out)
    return y, (h, q, k, v, o, lse, om)


def _swa_mixer_bwd(dy, dyb, x, nw, w_in, sinks, w_out, saved, out_scale):
    h, q, k, v, o, lse, om = saved
    d_om = _mm_plain("b_dout", dyb, w_out, 1, 1, BF16)
    d_wout = _mm_plain("b_dwout", om, dyb, 0, 0, F32)
    d_bout = _colsum("b_dbout", dy)
    dq, dk, dv, dsinks = _swa_bwd("b_dattn", q, k, v, sinks, o, lse, _split_heads(d_om, B_HEADS))
    dproj = jnp.concatenate([_merge_heads(dq), _merge_heads(dk), _merge_heads(dv)], axis=1)
    d_bin = _colsum("b_dbin", dproj)
    d_win = _mm_plain("b_dwin", h, dproj, 0, 0, F32, tn=768)
    dx, dxb, d_nw = _mm_rms_bwd("b_dx", dproj, w_in, x, dy, nw, 768, out_scale)
    return dx, dxb, d_nw, d_win, d_bin, dsinks[:, 0, 0], d_wout, d_bout


def _loss_head(name, x, tgt, fw, out_scale):
    T, D = x.shape
    tt = min(T, 512)

    def body(x_ref, t_ref, w_ref, dx_ref, dxb_ref, loss_ref, dw_ref):
        xv = x_ref[...]
        r = lax.rsqrt(jnp.mean(xv * xv, axis=-1, keepdims=True) + EPS)
        xhat = xv * r
        diff = xhat * w_ref[...] - t_ref[...]
        part = 0.5 * jnp.sum(jnp.mean(diff * diff, axis=-1, keepdims=True), axis=0, keepdims=True)
        dyv = diff * (1.0 / D)
        dxhat = dyv * w_ref[...]
        dx = r * (dxhat - xhat * jnp.mean(dxhat * xhat, axis=-1, keepdims=True))
        dx_ref[...] = dx
        dxb_ref[...] = (dx * out_scale).astype(BF16)
        dw = jnp.sum(dyv * xhat, axis=0, keepdims=True)
        lp = jnp.broadcast_to(part, (1, 128))

        @pl.when(pl.program_id(0) == 0)
        def _():
            loss_ref[...] = lp
            dw_ref[...] = dw

        @pl.when(pl.program_id(0) > 0)
        def _():
            loss_ref[...] += lp
            dw_ref[...] += dw

    row = pl.BlockSpec((tt, D), lambda i: (i, 0))
    return pl.pallas_call(
        body, grid=(T // tt,), in_specs=[row, row, pl.BlockSpec((1, D), lambda i: (0, 0))],
        out_specs=[row, row, pl.BlockSpec((1, 128), lambda i: (0, 0)), pl.BlockSpec((1, D), lambda i: (0, 0))],
        out_shape=[jax.ShapeDtypeStruct((T, D), F32), jax.ShapeDtypeStruct((T, D), BF16),
                   jax.ShapeDtypeStruct((1, 128), F32), jax.ShapeDtypeStruct((1, D), F32)],
        name=name, compiler_params=_params(1))(x, tgt, fw)


def _local_step(x, tgt, wts):
    W = wts
    g = {}
    ga, wdn = W["ga"], W["w_down"]
    n1, n2, nm = W["ffn1_norm"], W["ffn2_norm"], W["mix_norm"]
    x1, sv1 = _ffn_fwd("f10", x, n1[0:1], ga, 0, wdn[0])
    x2, sva = _gdn_fwd(x1, nm[0:1], W["a_w_in"], W["a_w_conv"], W["a_gate_p"], W["a_out_norm"], W["a_w_out"])
    x3, sv3 = _ffn_fwd("f20", x2, n2[0:1], ga, 2, wdn[2])
    x4, sv4 = _ffn_fwd("f11", x3, n1[1:2], ga, 1, wdn[1])
    x5, svb = _swa_mixer_fwd(x4, nm[1:2], W["b_w_in"], W["b_b_in"], W["b_sinks"], W["b_w_out"], W["b_b_out"])
    x6, sv6 = _ffn_fwd("f21", x5, n2[1:2], ga, 3, wdn[3])
    dx, dxb, loss_p, g["final_norm"] = _loss_head("loss_head", x6, tgt, W["final_norm"], 0.5)

    pa = jnp.zeros(ga.shape, BF16)
    dx, dxb, n21, pa, wd21 = _ffn_bwd("f21", dx, dxb, x5, n2[1:2], ga, 3, wdn[3], sv6, pa, 1.0)
    dx, dxb, nb, g["b_w_in"], g["b_b_in"], g["b_sinks"], g["b_w_out"], g["b_b_out"] = _swa_mixer_bwd(
        dx, dxb, x4, nm[1:2], W["b_w_in"], W["b_sinks"], W["b_w_out"], svb, 0.5)
    dx, dxb, n11, pa, wd11 = _ffn_bwd("f11", dx, dxb, x3, n1[1:2], ga, 1, wdn[1], sv4, pa, 0.5)
    dx, dxb, n20, pa, wd20 = _ffn_bwd("f20", dx, dxb, x2, n2[0:1], ga, 2, wdn[2], sv3, pa, 1.0)
    dx, dxb, na, g["a_w_in"], g["a_w_conv"], g["a_gate_p"], g["a_out_norm"], g["a_w_out"] = _gdn_bwd(
        dx, dxb, x1, nm[0:1], W["a_w_in"], W["a_w_conv"], W["a_gate_p"], W["a_out_norm"], W["a_w_out"], sva, 0.5)
    dx, dxb, n10, pa, wd10 = _ffn_bwd("f10", dx, dxb, x, n1[0:1], ga, 0, wdn[0], sv1, pa, 1.0)

    g["ffn1_norm"] = jnp.concatenate([n10, n11], axis=0)
    g["ffn2_norm"] = jnp.concatenate([n20, n21], axis=0)
    g["mix_norm"] = jnp.concatenate([na, nb], axis=0)
    g["ga"] = pa
    g["w_down"] = jnp.stack([wd10, wd11, wd20, wd21])
    return loss_p, dx, g


A_ROWS = 4 * D_MODEL
PACK = (("ffn1_w_down", 1408), ("ffn2_w_down", 1408), ("a_w_in", 1028), ("a_w_out", 256), ("b_w_in", 384),
        ("b_w_out", 256))
PACK_TILE = 16
PACK_USED = sum(-(-n // PACK_TILE) * PACK_TILE for _, n in PACK)
PACK_ROWS = 4864
assert PACK_USED <= PACK_ROWS
SMALL_SHARD = (8, 512)
MOVE_ROWS = {"a": 512, "b": 608}
SUM_ROWS = {"a": 256, "b": 304}


def _mesh_pos():
    x, y, c = lax.axis_index("x"), lax.axis_index("y"), lax.axis_index("c")
    return x, y, c, [(1 - x, y), (x, 1 - y), (1 - x, 1 - y)]


def _half(rows, c):
    return pl.ds(pl.multiple_of(c * (rows // 2), 16), rows // 2)


def _gather_chips(big_a, big_b, small):
    bufs = (big_a, big_b, small)

    def body(a_ref, b_ref, small_ref, ra_ref, rb_ref, rs_ref, send_sems, recv_sems):
        x, y, c, chips = _mesh_pos()
        srcs = (a_ref.at[_half(A_ROWS, c)], b_ref.at[_half(PACK_ROWS, c)], small_ref)
        send = []
        for j, chip in enumerate(chips):
            for n, (src, dst) in enumerate(zip(srcs, (ra_ref, rb_ref, rs_ref))):
                send.append(pltpu.make_async_remote_copy(src_ref=src, dst_ref=dst.at[j],
                                                         send_sem=send_sems.at[3 * j + n], recv_sem=recv_sems.at[3 * j + n],
                                                         device_id=(*chip, c), device_id_type=MESH))
        for cp in send:
            cp.start()
        for cp in send:
            cp.wait_recv()
        for cp in send:
            cp.wait_send()

    halves = [jax.ShapeDtypeStruct((3, b.shape[0] // 2, b.shape[1]), BF16) for b in bufs[:2]]
    return pl.pallas_call(
        body, name="gather_chips", in_specs=[HBM, HBM, HBM], out_specs=[HBM, HBM, HBM],
        out_shape=halves + [jax.ShapeDtypeStruct((3,) + SMALL_SHARD, F32)],
        scratch_shapes=[pltpu.SemaphoreType.DMA((9,)), pltpu.SemaphoreType.DMA((9,))])(*bufs)


def _gather_fill(tag, big, recv):
    rows_all, width = big.shape
    half, mv = rows_all // 2, MOVE_ROWS[tag]
    nt = half // mv
    own_tiles = rows_all // mv
    assert half % mv == 0 and own_tiles <= 3 * nt

    def body(recv_ref, big_ref, g_ref, send_sem, recv_sem, local_sems):
        x, y, c, chips = _mesh_pos()
        j, t = pl.program_id(0), pl.program_id(1)
        step = j * nt + t
        src_chip = jnp.where(j == 0, 2 * (1 - x) + y, jnp.where(j == 1, 2 * x + 1 - y, 2 * (1 - x) + 1 - y))
        rows = pl.ds(pl.multiple_of(c * half + t * mv, 16), mv)
        keep = pltpu.make_async_copy(recv_ref.at[0], g_ref.at[src_chip, rows], local_sems.at[0])
        give = pltpu.make_async_remote_copy(src_ref=recv_ref.at[0], dst_ref=g_ref.at[src_chip, rows],
                                            send_sem=send_sem, recv_sem=recv_sem,
                                            device_id=(x, y, 1 - c), device_id_type=MESH)
        keep.start()
        give.start()

        @pl.when(step < own_tiles)
        def _():
            own_rows = pl.ds(pl.multiple_of(step * mv, 16), mv)
            own = pltpu.make_async_copy(big_ref, g_ref.at[2 * x + y, own_rows], local_sems.at[1])
            own.start()
            own.wait()

        give.wait_send()
        keep.wait()

        @pl.when(step == 3 * nt - 1)
        def _():
            landed = g_ref.at[pl.ds(0, 3), pl.ds(0, half)]
            pltpu.make_async_remote_copy(src_ref=landed, dst_ref=landed, send_sem=send_sem, recv_sem=recv_sem,
                                         device_id=(x, y, c), device_id_type=MESH).wait_recv()

    return pl.pallas_call(
        body, grid=(3, nt),
        in_specs=[pl.BlockSpec((1, mv, width), lambda j, t: (j, t, 0)),
                  pl.BlockSpec((mv, width), lambda j, t: (jnp.minimum(j * nt + t, own_tiles - 1), 0))],
        out_specs=HBM, out_shape=jax.ShapeDtypeStruct((4, rows_all, width), BF16),
        scratch_shapes=[pltpu.SemaphoreType.DMA, pltpu.SemaphoreType.DMA, pltpu.SemaphoreType.DMA((2,))],
        name="gather_fill_" + tag, compiler_params=_params(2))(recv, big)


def _pair_send(tag, p):
    _, rows_all, width = p.shape
    half, mv = rows_all // 2, MOVE_ROWS[tag]
    nt = half // mv

    def body(p_ref, a_ref, send_sem, recv_sem):
        x, y, c, _ = _mesh_pos()
        s, t = pl.program_id(0), pl.program_id(1)
        rows = pl.ds(pl.multiple_of(t * mv, 16), mv)
        give = pltpu.make_async_remote_copy(src_ref=p_ref.at[0], dst_ref=a_ref.at[s, rows], send_sem=send_sem,
                                            recv_sem=recv_sem, device_id=(x, y, 1 - c), device_id_type=MESH)
        give.start()
        give.wait_send()

        @pl.when((s == 3) & (t == nt - 1))
        def _():
            pltpu.make_async_remote_copy(src_ref=a_ref, dst_ref=a_ref, send_sem=send_sem, recv_sem=recv_sem,
                                         device_id=(x, y, c), device_id_type=MESH).wait_recv()

    return pl.pallas_call(
        body, grid=(4, nt),
        in_specs=[pl.BlockSpec((1, mv, width), lambda s, t: (s, (1 - lax.axis_index("c")) * nt + t, 0))],
        out_specs=HBM, out_shape=jax.ShapeDtypeStruct((4, half, width), BF16),
        scratch_shapes=[pltpu.SemaphoreType.DMA, pltpu.SemaphoreType.DMA],
        name="pair_send_" + tag, compiler_params=_params(2))(p)


def _pair_sum(tag, p, a):
    _, half, width = a.shape
    sr = SUM_ROWS[tag]
    nt = half // sr
    assert half % sr == 0

    def body(p_ref, a_ref, o_ref):
        o_ref[...] = (p_ref[...].astype(F32) + a_ref[...].astype(F32)).astype(BF16)

    spec = pl.BlockSpec((1, sr, width), lambda s, t: (s, t, 0))
    return pl.pallas_call(
        body, grid=(4, nt),
        in_specs=[pl.BlockSpec((1, sr, width), lambda s, t: (s, lax.axis_index("c") * nt + t, 0)), spec],
        out_specs=spec, out_shape=jax.ShapeDtypeStruct((4, half, width), BF16),
        name="pair_sum_" + tag, compiler_params=_params(2))(p, a)


def _chip_exchange(cs_a, cs_b):
    def body(ca_ref, cb_ref, ba_ref, bb_ref, send_sems, recv_sems):
        x, y, c, chips = _mesh_pos()
        send = []
        for j, chip in enumerate(chips):
            for n, (src, dst) in enumerate(((ca_ref, ba_ref), (cb_ref, bb_ref))):
                send.append(pltpu.make_async_remote_copy(src_ref=src.at[2 * chip[0] + chip[1]], dst_ref=dst.at[j],
                                                         send_sem=send_sems.at[2 * j + n], recv_sem=recv_sems.at[2 * j + n],
                                                         device_id=(*chip, c), device_id_type=MESH))
        for cp in send:
            cp.start()
        for cp in send:
            cp.wait_recv()
        for cp in send:
            cp.wait_send()

    return pl.pallas_call(
        body, name="chip_exchange", in_specs=[HBM, HBM], out_specs=[HBM, HBM],
        out_shape=[jax.ShapeDtypeStruct((3,) + cs.shape[1:], BF16) for cs in (cs_a, cs_b)],
        scratch_shapes=[pltpu.SemaphoreType.DMA((6,)), pltpu.SemaphoreType.DMA((6,))])(cs_a, cs_b)


def _chip_sum(tag, cs, b):
    _, half, width = cs.shape
    sr = SUM_ROWS[tag]
    nt = half // sr

    def body(c_ref, b_ref, r_ref, buf, send_sems, recv_sem, local_sems):
        x, y, c, _ = _mesh_pos()
        t = pl.program_id(0)
        slot = lax.rem(t, 2)

        def copies(k, tile):
            rows = pl.ds(pl.multiple_of(c * half + tile * sr, 8), sr)
            keep = pltpu.make_async_copy(buf.at[k], r_ref.at[rows], local_sems.at[k])
            give = pltpu.make_async_remote_copy(src_ref=buf.at[k], dst_ref=r_ref.at[rows], send_sem=send_sems.at[k],
                                                recv_sem=recv_sem, device_id=(x, y, 1 - c), device_id_type=MESH)
            return keep, give

        @pl.when(t >= 2)
        def _():
            keep, give = copies(slot, t - 2)
            keep.wait()
            give.wait_send()

        buf[slot] = (c_ref[0].astype(F32) + b_ref[0].astype(F32)) + (b_ref[1].astype(F32) + b_ref[2].astype(F32))
        keep, give = copies(slot, t)
        keep.start()
        give.start()

        @pl.when(t == nt - 1)
        def _():
            for back in (1, 0):
                keep, give = copies(lax.rem(t - back, 2), t - back)
                keep.wait()
                give.wait_send()
            landed = r_ref.at[_half(2 * half, 1 - c)]
            pltpu.make_async_remote_copy(src_ref=landed, dst_ref=landed, send_sem=send_sems.at[0], recv_sem=recv_sem,
                                         device_id=(x, y, c), device_id_type=MESH).wait_recv()

    return pl.pallas_call(
        body, grid=(nt,),
        in_specs=[pl.BlockSpec((1, sr, width), lambda t: (2 * lax.axis_index("x") + lax.axis_index("y"), t, 0)),
                  pl.BlockSpec((3, sr, width), lambda t: (0, t, 0))],
        out_specs=HBM, out_shape=jax.ShapeDtypeStruct((2 * half, width), F32),
        scratch_shapes=[pltpu.VMEM((2, sr, width), F32), pltpu.SemaphoreType.DMA((2,)), pltpu.SemaphoreType.DMA,
                        pltpu.SemaphoreType.DMA((2,))],
        name="chip_sum_" + tag, compiler_params=_params(1))(cs, b)


def _reduce_scatter(p_a, p_b):
    cs_a = _pair_sum("a", p_a, _pair_send("a", p_a))
    cs_b = _pair_sum("b", p_b, _pair_send("b", p_b))
    from_a, from_b = _chip_exchange(cs_a, cs_b)
    return _chip_sum("a", cs_a, from_a), _chip_sum("b", cs_b, from_b)


SMALL_ROWS = 24


def _all_reduce_small(v):
    def body(v_ref, o_ref, all_ref, send_sems, recv_sems):
        x, y, c, _ = _mesh_pos()
        me = 4 * x + 2 * y + c
        all_ref[me] = v_ref[...]
        peers = [(x ^ ((k >> 2) & 1), y ^ ((k >> 1) & 1), c ^ (k & 1)) for k in range(1, 8)]
        idx = lambda p: 4 * p[0] + 2 * p[1] + p[2]
        send = [pltpu.make_async_remote_copy(src_ref=v_ref, dst_ref=all_ref.at[me], send_sem=send_sems.at[k],
                                             recv_sem=recv_sems.at[k], device_id=p, device_id_type=MESH)
                for k, p in enumerate(peers)]
        for cp in send:
            cp.start()
        for k, p in enumerate(peers):
            pltpu.make_async_remote_copy(src_ref=v_ref, dst_ref=all_ref.at[idx(p)], send_sem=send_sems.at[k],
                                         recv_sem=recv_sems.at[k], device_id=p, device_id_type=MESH).wait_recv()
        for cp in send:
            cp.wait_send()
        acc = all_ref[0]
        for d in range(1, 8):
            acc = acc + all_ref[d]
        o_ref[...] = acc

    vm = pl.BlockSpec(memory_space=pltpu.VMEM)
    return pl.pallas_call(
        body, name="all_reduce_small", in_specs=[vm], out_specs=vm,
        out_shape=jax.ShapeDtypeStruct((SMALL_ROWS, 1024), F32),
        scratch_shapes=[pltpu.VMEM((8, SMALL_ROWS, 1024), F32), pltpu.SemaphoreType.DMA((7,)),
                        pltpu.SemaphoreType.DMA((7,))],)(v)


def _adamw(name, w, g, m, v):
    rows, cols = w.shape
    tr = rows
    if rows * cols > 400_000:
        tr = max(t for t in range(8, rows, 8) if rows % t == 0 and t * cols <= 400_000)

    def body(w_ref, g_ref, m_ref, v_ref, d_ref, nm_ref, nv_ref):
        gv = g_ref[...]
        m_new = ADAM_B1 * m_ref[...] + (1.0 - ADAM_B1) * gv
        v_new = ADAM_B2 * v_ref[...] + (1.0 - ADAM_B2) * (gv * gv)
        m_hat = m_new / (1.0 - ADAM_B1 ** ADAM_STEP)
        v_hat = v_new / (1.0 - ADAM_B2 ** ADAM_STEP)
        d_ref[...] = -ADAM_LR * (m_hat / (jnp.sqrt(v_hat) + ADAM_EPS) + ADAM_WD * w_ref[...])
        nm_ref[...] = m_new
        nv_ref[...] = v_new

    spec = pl.BlockSpec((tr, cols), lambda i: (i, 0))
    sds = jax.ShapeDtypeStruct((rows, cols), F32)
    return pl.pallas_call(body, grid=(rows // tr,), in_specs=[spec] * 4, out_specs=[spec] * 3, out_shape=[sds] * 3,
                          name=name, compiler_params=_params(1))(w, g, m, v)


WEIGHTS = ("ffn1_norm", "ffn1_w_gu", "ffn1_w_down", "mix_norm", "ffn2_norm", "ffn2_w_gu", "ffn2_w_down",
           "a_w_in", "a_w_conv", "a_A_log", "a_dt_bias", "a_out_norm", "a_w_out",
           "b_w_in", "b_b_in", "b_sinks", "b_w_out", "b_b_out", "final_norm")
SMALL_SLOTS = {"ffn1_norm": (0, 2048), "mix_norm": (2048, 2048), "ffn2_norm": (4096, 2048), "final_norm": (6144, 1024),
               "a_A_log": (7168, 8), "a_dt_bias": (7296, 8), "a_out_norm": (7424, 128), "b_sinks": (7552, 16),
               "loss": (7680, 1)}
SMALL_SHARDED = {"a_w_conv": (8192, 8192, (4,), 3072), "b_b_in": (20480, 11264, (), 1536), "b_b_out": (22016, 11648, (), 1024)}
DEV_SMALL_ROWS = 12


def _pack_rows(parts):
    rows = []
    for p in parts:
        r = p.reshape(p.shape[0], -1, 1024)
        rows.append(jnp.pad(r, ((0, 0), (0, -r.shape[1] % PACK_TILE), (0, 0))))
    rows.append(jnp.zeros((parts[0].shape[0], PACK_ROWS - PACK_USED, 1024), parts[0].dtype))
    return jnp.concatenate(rows, axis=1)


def _place(vec, off, a):
    return lax.dynamic_update_slice(vec, a.reshape(-1).astype(F32), (off,))


def kernel(x, ffn1_norm, ffn1_w_gu, ffn1_w_down, mix_norm, ffn2_norm, ffn2_w_gu, ffn2_w_down, a_w_in, a_w_conv, a_A_log, a_dt_bias, a_out_norm, a_w_out, b_w_in, b_b_in, b_sinks, b_w_out, b_b_out, final_norm, loss_target, m_ffn1_norm, m_ffn1_w_gu, m_ffn1_w_down, m_mix_norm, m_ffn2_norm, m_ffn2_w_gu, m_ffn2_w_down, m_a_w_in, m_a_w_conv, m_a_A_log, m_a_dt_bias, m_a_out_norm, m_a_w_out, m_b_w_in, m_b_b_in, m_b_sinks, m_b_w_out, m_b_b_out, m_final_norm, v_ffn1_norm, v_ffn1_w_gu, v_ffn1_w_down, v_mix_norm, v_ffn2_norm, v_ffn2_w_gu, v_ffn2_w_down, v_a_w_in, v_a_w_conv, v_a_A_log, v_a_dt_bias, v_a_out_norm, v_a_w_out, v_b_w_in, v_b_b_in, v_b_sinks, v_b_w_out, v_b_b_out, v_final_norm):
    w = dict(zip(WEIGHTS, (ffn1_norm, ffn1_w_gu, ffn1_w_down, mix_norm, ffn2_norm, ffn2_w_gu, ffn2_w_down, a_w_in, a_w_conv,
                           a_A_log, a_dt_bias, a_out_norm, a_w_out, b_w_in, b_b_in, b_sinks, b_w_out, b_b_out, final_norm)))
    m = dict(zip(WEIGHTS, (m_ffn1_norm, m_ffn1_w_gu, m_ffn1_w_down, m_mix_norm, m_ffn2_norm, m_ffn2_w_gu, m_ffn2_w_down,
                           m_a_w_in, m_a_w_conv, m_a_A_log, m_a_dt_bias, m_a_out_norm, m_a_w_out, m_b_w_in, m_b_b_in,
                           m_b_sinks, m_b_w_out, m_b_b_out, m_final_norm)))
    v = dict(zip(WEIGHTS, (v_ffn1_norm, v_ffn1_w_gu, v_ffn1_w_down, v_mix_norm, v_ffn2_norm, v_ffn2_w_gu, v_ffn2_w_down,
                           v_a_w_in, v_a_w_conv, v_a_A_log, v_a_dt_bias, v_a_out_norm, v_a_w_out, v_b_w_in, v_b_b_in,
                           v_b_sinks, v_b_w_out, v_b_b_out, v_final_norm)))
    chip = 2 * lax.axis_index("x") + lax.axis_index("y")

    big_a = jnp.concatenate([w["ffn1_w_gu"], w["ffn2_w_gu"]], axis=0).astype(BF16).reshape(A_ROWS, FF_BLK)
    big_b = _pack_rows([w[n].astype(BF16).reshape(1, -1) for n, _ in PACK])[0]
    small = jnp.zeros((4096,), F32)
    small = _place(small, 0, w["a_w_conv"])
    small = _place(small, 3072, w["b_b_in"])
    small = _place(small, 3456, w["b_b_out"]).reshape(SMALL_SHARD)
    ra, rb, rs = _gather_chips(big_a, big_b, small)
    ga = _gather_fill("a", big_a, ra).reshape(4, 4, D_MODEL, FF_BLK)
    gb = _gather_fill("b", big_b, rb)
    offs, o = {}, 0
    for n, r in PACK:
        offs[n] = (o, r)
        o += -(-r // PACK_TILE) * PACK_TILE
    blk = lambda n: gb[:, offs[n][0]:offs[n][0] + offs[n][1]]
    gsf = lax.dynamic_update_slice(jnp.zeros((4, 4096), F32), small.reshape(1, 4096), (chip, 0))
    for j, other in enumerate((chip ^ 2, chip ^ 1, chip ^ 3)):
        gsf = lax.dynamic_update_slice(gsf, rs[j].reshape(1, 4096), (other, 0))
    W = {n: w[n] for n in ("ffn1_norm", "ffn2_norm", "mix_norm", "a_out_norm")}
    W["ga"] = ga
    W["w_down"] = gb[:, 0:2816].reshape(4, 4, 704, 1024).transpose(1, 0, 2, 3).reshape(4, 2, FF_BLK, 1024)
    W["a_w_in"] = jnp.pad(blk("a_w_in").reshape(4, 1024, 1028).transpose(1, 0, 2).reshape(1024, A_IN_COLS),
                          ((0, 0), (0, A_IN_PAD - A_IN_COLS)))
    W["a_w_out"] = blk("a_w_out").reshape(1024, 1024)
    W["b_w_in"] = blk("b_w_in").reshape(4, 1024, 384).transpose(1, 0, 2).reshape(1024, 1536)
    W["b_w_out"] = blk("b_w_out").reshape(1024, 1024)
    W["a_w_conv"] = gsf[:, 0:3072].reshape(4, 4, 768).transpose(1, 0, 2).reshape(4, 3072)
    W["b_b_in"] = gsf[:, 3072:3456].reshape(1, 1536)
    W["b_b_out"] = gsf[:, 3456:3712].reshape(1, 1024)
    W["a_gate_p"] = jnp.pad(jnp.concatenate([-jnp.exp(w["a_A_log"]), w["a_dt_bias"]], axis=0), ((0, 0), (8, 112)))
    W["b_sinks"] = jnp.broadcast_to(w["b_sinks"][0][:, None, None], (B_HEADS, 1, 128))
    W["final_norm"] = w["final_norm"][None]

    loss_p, dx, g = _local_step(x[0], loss_target[0], W)

    down = g["w_down"].reshape(4, 4, 704, 1024).transpose(1, 0, 2, 3)
    parts = [down[:, 0:2], down[:, 2:4],
             g["a_w_in"][:, :A_IN_COLS].reshape(1024, 4, 1028).transpose(1, 0, 2), g["a_w_out"].reshape(4, 256, 1024),
             g["b_w_in"].reshape(1024, 4, 384).transpose(1, 0, 2), g["b_w_out"].reshape(4, 256, 1024)]
    red_a, red_b = _reduce_scatter(g["ga"].reshape(4, A_ROWS, FF_BLK),
                                   _pack_rows([a.astype(BF16).reshape(4, -1) for a in parts]))
    grads = {n: red_b[offs[n][0]:offs[n][0] + offs[n][1]].reshape(w[n].shape) for n, _ in PACK}
    grads["ffn1_w_gu"] = red_a[:A_ROWS // 2].reshape(w["ffn1_w_gu"].shape)
    grads["ffn2_w_gu"] = red_a[A_ROWS // 2:].reshape(w["ffn2_w_gu"].shape)

    sv = jnp.zeros((SMALL_ROWS * 1024,), F32)
    small_g = {"ffn1_norm": g["ffn1_norm"], "mix_norm": g["mix_norm"], "ffn2_norm": g["ffn2_norm"], "final_norm": g["final_norm"],
               "a_A_log": g["a_gate_p"][0, 8:16], "a_dt_bias": g["a_gate_p"][1, 8:16], "a_out_norm": g["a_out_norm"],
               "b_sinks": g["b_sinks"], "loss": loss_p[0, 0:1]}
    for n, (off, _) in SMALL_SLOTS.items():
        sv = _place(sv, off, small_g[n])
    for n, (off, _, _, _) in SMALL_SHARDED.items():
        sv = _place(sv, off, g[n])
    tot = _all_reduce_small(sv.reshape(SMALL_ROWS, 1024)).reshape(-1)
    for n, (off, size) in SMALL_SLOTS.items():
        if n != "loss":
            grads[n] = tot[off:off + size].reshape(w[n].shape)
    for n, (off, _, lead, last) in SMALL_SHARDED.items():
        full = tot[off:off + (lead[0] if lead else 1) * last].reshape(lead + (last,))
        width = last // 4
        grads[n] = lax.dynamic_slice_in_dim(full, chip * width, width, axis=-1).reshape(w[n].shape)
    loss = tot[SMALL_SLOTS["loss"][0]]

    delta, new_m, new_v = {}, {}, {}
    for n in ("ffn1_w_gu", "ffn2_w_gu") + tuple(n for n, _ in PACK):
        two_d = lambda a: a.reshape(-1, a.shape[-1])
        d, nm, nv = _adamw("adamw_" + n, two_d(w[n]), two_d(grads[n]), two_d(m[n]), two_d(v[n]))
        delta[n], new_m[n], new_v[n] = d.reshape(w[n].shape), nm.reshape(w[n].shape), nv.reshape(w[n].shape)

    def dev_small(src):
        vec = jnp.zeros((DEV_SMALL_ROWS * 1024,), F32)
        for n, (off, _) in SMALL_SLOTS.items():
            if n != "loss":
                vec = _place(vec, off, src[n])
        for n, (_, off, _, _) in SMALL_SHARDED.items():
            vec = _place(vec, off, src[n])
        return vec.reshape(DEV_SMALL_ROWS, 1024)

    sd, sm, svv = _adamw("adamw_small", dev_small(w), dev_small(grads), dev_small(m), dev_small(v))
    for n in WEIGHTS:
        if n in SMALL_SLOTS:
            off, size = SMALL_SLOTS[n]
        elif n in SMALL_SHARDED:
            off, size = SMALL_SHARDED[n][1], w[n].size
        else:
            continue
        for dst, src in ((delta, sd), (new_m, sm), (new_v, svv)):
            dst[n] = src.reshape(-1)[off:off + size].reshape(w[n].shape)

    return (loss, dx[None], *[grads[n] for n in WEIGHTS], *[delta[n] for n in WEIGHTS],
            *[new_m[n] for n in WEIGHTS], *[new_v[n] for n in WEIGHTS])
```

```python
import jax
import jax.numpy as jnp
from jax import lax
from jax.experimental import pallas as pl
from jax.experimental.pallas import tpu as pltpu

F32 = jnp.float32
BF16 = jnp.bfloat16

D_MODEL = 1024
EPS = 1e-6
FF_BLK = 1408
A_HEADS = 8
A_DK = 128
A_CHUNK = 64
A_HG = 8
A_IN_COLS = 4112
A_IN_PAD = 4224
B_HEADS = 16
B_KV = 4
B_HD = 64
B_BLK = 128
ADAM_LR, ADAM_B1, ADAM_B2, ADAM_EPS, ADAM_WD, ADAM_STEP = 0.001, 0.9, 0.999, 1e-08, 0.01, 10
MESH = pl.DeviceIdType.MESH
VMEM_LIMIT = 56 * 1024 * 1024
HBM = pl.BlockSpec(memory_space=pl.ANY)


def _params(n_axes):
    return pltpu.CompilerParams(dimension_semantics=("arbitrary",) * n_axes, vmem_limit_bytes=VMEM_LIMIT)


def _sigmoid(x):
    return 1.0 / (1.0 + jnp.exp(-x))


def _dot(a, b, ca, cb):
    return lax.dot_general(a, b, (((ca,), (cb,)), ((), ())), preferred_element_type=F32)


def _dotb(a, b, ca=1, cb=0):
    return _dot(a.astype(BF16), b.astype(BF16), ca, cb)


def _dotx(a, b, ca=1, cb=0):
    return lax.dot_general(a, b, (((ca,), (cb,)), ((), ())), preferred_element_type=F32,
                           precision=lax.Precision.HIGHEST)


def _doth(a, b, ca=1, cb=0):
    return lax.dot_general(a, b, (((ca,), (cb,)), ((), ())), preferred_element_type=F32,
                           precision=lax.Precision.HIGH)


def _rms_fwd(name, x, w):
    T, D = x.shape
    tt = min(T, 512)

    def body(x_ref, w_ref, h_ref):
        xv = x_ref[...]
        r = lax.rsqrt(jnp.mean(xv * xv, axis=-1, keepdims=True) + EPS)
        h_ref[...] = (xv * r * w_ref[...]).astype(BF16)

    return pl.pallas_call(
        body, grid=(T // tt,),
        in_specs=[pl.BlockSpec((tt, D), lambda i: (i, 0)), pl.BlockSpec((1, D), lambda i: (0, 0))],
        out_specs=pl.BlockSpec((tt, D), lambda i: (i, 0)),
        out_shape=jax.ShapeDtypeStruct((T, D), BF16), name=name, compiler_params=_params(1))(x, w)


def _rms_bwd_tile(dh, xv, dy, w):
    r = lax.rsqrt(jnp.mean(xv * xv, axis=-1, keepdims=True) + EPS)
    xhat = xv * r
    dxhat = dh * w
    dx = dy + r * (dxhat - xhat * jnp.mean(dxhat * xhat, axis=-1, keepdims=True))
    return dx, jnp.sum(dh * xhat, axis=0, keepdims=True)


def _colsum(name, a):
    T, N = a.shape
    tt = min(T, 512)

    def body(a_ref, o_ref):
        @pl.when(pl.program_id(0) == 0)
        def _():
            o_ref[...] = jnp.zeros_like(o_ref)
        o_ref[...] += jnp.sum(a_ref[...].astype(F32), axis=0, keepdims=True)

    return pl.pallas_call(
        body, grid=(T // tt,), in_specs=[pl.BlockSpec((tt, N), lambda i: (i, 0))],
        out_specs=pl.BlockSpec((1, N), lambda i: (0, 0)),
        out_shape=jax.ShapeDtypeStruct((1, N), F32), name=name, compiler_params=_params(1))(a)


def _matmul(name, a, b, ca, cb, tm, tn, tk, extra_in, outs, epi, order="ji"):
    M, K, N = a.shape[1 - ca], a.shape[ca], b.shape[1 - cb]
    tm, tn, tk = min(tm, M), min(tn, N), min(tk, K)
    assert M % tm == 0 and N % tn == 0 and K % tk == 0, (name, M, N, K, tm, tn, tk)
    ni, nj, nk = M // tm, N // tn, K // tk
    if order == "ji":
        grid = (nj, ni, nk)
        perm = lambda g0, g1, g2: (g1, g0, g2)
    else:
        grid = (ni, nj, nk)
        perm = lambda g0, g1, g2: (g0, g1, g2)

    def wrap(f):
        return lambda g0, g1, g2: f(*perm(g0, g1, g2))

    a_spec = (pl.BlockSpec((tm, tk), wrap(lambda i, j, k: (i, k))) if ca == 1
              else pl.BlockSpec((tk, tm), wrap(lambda i, j, k: (k, i))))
    b_spec = (pl.BlockSpec((tk, tn), wrap(lambda i, j, k: (k, j))) if cb == 0
              else pl.BlockSpec((tn, tk), wrap(lambda i, j, k: (j, k))))
    ne, no = len(extra_in), len(outs)

    def body(*refs):
        a_ref, b_ref = refs[0], refs[1]
        ex, out = refs[2:2 + ne], refs[2 + ne:2 + ne + no]
        i, j, k = perm(pl.program_id(0), pl.program_id(1), pl.program_id(2))
        p = _dotb(a_ref[...], b_ref[...], ca, cb)
        if nk == 1:
            epi(p, ex, out, i, j)
        else:
            acc_ref = refs[-1]

            @pl.when(k == 0)
            def _():
                acc_ref[...] = p

            @pl.when(k > 0)
            def _():
                acc_ref[...] += p

            @pl.when(k == nk - 1)
            def _():
                epi(acc_ref[...], ex, out, i, j)

    return pl.pallas_call(
        body, grid=grid,
        in_specs=[a_spec, b_spec] + [pl.BlockSpec(bs, wrap(f)) for _, bs, f in extra_in],
        out_specs=[pl.BlockSpec(bs, wrap(f)) for _, bs, f in outs],
        out_shape=[s for s, _, _ in outs],
        scratch_shapes=[pltpu.VMEM((tm, tn), F32)] if nk > 1 else [],
        name=name, compiler_params=_params(3))(a, b, *[x for x, _, _ in extra_in])


def _mm_plain(name, a, b, ca, cb, out_dtype, tm=1024, tn=1024, tk=1024, scale=1.0, bias=None):
    M, N = a.shape[1 - ca], b.shape[1 - cb]
    tm, tn = min(tm, M), min(tn, N)
    extra = [] if bias is None else [(bias, (1, tn), lambda i, j, k: (0, j))]

    def epi(acc, ex, out, i, j):
        r = acc * scale if scale != 1.0 else acc
        if bias is not None:
            r = r + ex[0][...]
        out[0][...] = r.astype(out_dtype)

    return _matmul(name, a, b, ca, cb, tm, tn, tk, extra,
                   [(jax.ShapeDtypeStruct((M, N), out_dtype), (tm, tn), lambda i, j, k: (i, j))], epi)[0]


def _mm_residual(name, a, b, x, scale, bias=None, tk=1024):
    M, N = x.shape
    tm, tn = min(512, M), N
    extra = [(x, (tm, tn), lambda i, j, k: (i, j))]
    if bias is not None:
        extra.append((bias, (1, tn), lambda i, j, k: (0, j)))

    def epi(acc, ex, out, i, j):
        r = acc if bias is None else acc + ex[1][...]
        out[0][...] = ex[0][...] + scale * r

    return _matmul(name, a, b, 1, 0, tm, tn, tk, extra,
                   [(jax.ShapeDtypeStruct((M, N), F32), (tm, tn), lambda i, j, k: (i, j))], epi, order="ij")[0]


def _mm_rms_bwd(name, dproj, w_in, x, dy, nw, tk, out_scale):
    M, N = x.shape
    tm = min(512, M)
    extra = [(x, (tm, N), lambda i, j, k: (i, 0)), (dy, (tm, N), lambda i, j, k: (i, 0)),
             (nw, (1, N), lambda i, j, k: (0, 0))]

    def epi(acc, ex, out, i, j):
        dx, dw = _rms_bwd_tile(acc, ex[0][...], ex[1][...], ex[2][...])
        out[0][...] = dx
        out[1][...] = (dx * out_scale).astype(BF16)

        @pl.when(i == 0)
        def _():
            out[2][...] = dw

        @pl.when(i > 0)
        def _():
            out[2][...] += dw

    return _matmul(name, dproj, w_in, 1, 1, tm, N, tk, extra,
                   [(jax.ShapeDtypeStruct((M, N), F32), (tm, N), lambda i, j, k: (i, 0)),
                    (jax.ShapeDtypeStruct((M, N), BF16), (tm, N), lambda i, j, k: (i, 0)),
                    (jax.ShapeDtypeStruct((1, N), F32), (1, N), lambda i, j, k: (0, 0))], epi, order="ij")


def _ffn_gu(name, h, ga, blk):
    T, D = h.shape
    tm = min(T, 1024)
    rs = min(tm, 256)

    def body(h_ref, wg_ref, wu_ref, gu_ref, act_ref):
        for r in range(tm // rs):
            rows = pl.ds(r * rs, rs)
            hv = h_ref[rows, :]
            g = _dot(hv, wg_ref[0, 0], 1, 0)
            u = _dot(hv, wu_ref[0, 0], 1, 0)
            s = _sigmoid(g)
            gs = g * s
            gu_ref[0, 0, rows, :] = (u * (s + gs * (1.0 - s))).astype(BF16)
            gu_ref[1, 0, rows, :] = gs.astype(BF16)
            act_ref[0, rows, :] = (gs * u).astype(BF16)

    return pl.pallas_call(
        body, grid=(2, T // tm),
        in_specs=[pl.BlockSpec((tm, D), lambda j, i: (i, 0)),
                  pl.BlockSpec((1, 1, D, FF_BLK), lambda j, i: (j, blk, 0, 0)),
                  pl.BlockSpec((1, 1, D, FF_BLK), lambda j, i: (2 + j, blk, 0, 0))],
        out_specs=[pl.BlockSpec((2, 1, tm, FF_BLK), lambda j, i: (0, j, i, 0)),
                   pl.BlockSpec((1, tm, FF_BLK), lambda j, i: (j, i, 0))],
        out_shape=[jax.ShapeDtypeStruct((2, 2, T, FF_BLK), BF16), jax.ShapeDtypeStruct((2, T, FF_BLK), BF16)],
        name=name, compiler_params=_params(2))(h, ga, ga)


def _ffn_down(name, act, wd, x):
    T, D = x.shape
    tm = min(T, 512)

    def body(act_ref, wd_ref, x_ref, o_ref):
        acc = _dot(act_ref[0], wd_ref[0], 1, 0) + _dot(act_ref[1], wd_ref[1], 1, 0)
        o_ref[...] = x_ref[...] + 0.5 * acc

    return pl.pallas_call(
        body, grid=(T // tm,),
        in_specs=[pl.BlockSpec((2, tm, FF_BLK), lambda i: (0, i, 0)),
                  pl.BlockSpec((2, FF_BLK, D), lambda i: (0, 0, 0)),
                  pl.BlockSpec((tm, D), lambda i: (i, 0))],
        out_specs=pl.BlockSpec((tm, D), lambda i: (i, 0)),
        out_shape=jax.ShapeDtypeStruct((T, D), F32), name=name, compiler_params=_params(1))(act, wd, x)


def _ffn_dact(name, dyh, wd, gu):
    T, D = dyh.shape
    tm = min(T, 1024)
    rs = min(tm, 256)

    def body(dy_ref, wd_ref, gu_ref, o_ref):
        for r in range(tm // rs):
            rows = pl.ds(r * rs, rs)
            dact = _dot(dy_ref[rows, :], wd_ref[0], 1, 1)
            o_ref[0, 0, rows, :] = (dact * gu_ref[0, 0, rows, :].astype(F32)).astype(BF16)
            o_ref[1, 0, rows, :] = (dact * gu_ref[1, 0, rows, :].astype(F32)).astype(BF16)

    return pl.pallas_call(
        body, grid=(2, T // tm),
        in_specs=[pl.BlockSpec((tm, D), lambda j, i: (i, 0)),
                  pl.BlockSpec((1, FF_BLK, D), lambda j, i: (j, 0, 0)),
                  pl.BlockSpec((2, 1, tm, FF_BLK), lambda j, i: (0, j, i, 0))],
        out_specs=pl.BlockSpec((2, 1, tm, FF_BLK), lambda j, i: (0, j, i, 0)),
        out_shape=jax.ShapeDtypeStruct((2, 2, T, FF_BLK), BF16), name=name, compiler_params=_params(2))(dyh, wd, gu)


def _ffn_dwd(name, act, dyh):
    _, T, _ = act.shape
    D = dyh.shape[1]
    tk = min(T, 2048)
    nk = T // tk

    def body(a_ref, d_ref, o_ref, acc_ref):
        k = pl.program_id(1)
        p = _dot(a_ref[0], d_ref[...], 0, 0)

        @pl.when(k == 0)
        def _():
            acc_ref[...] = p

        @pl.when(k > 0)
        def _():
            acc_ref[...] += p

        @pl.when(k == nk - 1)
        def _():
            o_ref[0] = acc_ref[...].astype(BF16)

    return pl.pallas_call(
        body, grid=(2, nk),
        in_specs=[pl.BlockSpec((1, tk, FF_BLK), lambda j, k: (j, k, 0)), pl.BlockSpec((tk, D), lambda j, k: (k, 0))],
        out_specs=pl.BlockSpec((1, FF_BLK, D), lambda j, k: (j, 0, 0)),
        out_shape=jax.ShapeDtypeStruct((2, FF_BLK, D), BF16), scratch_shapes=[pltpu.VMEM((FF_BLK, D), F32)],
        name=name, compiler_params=_params(2))(act, dyh)


def _ffn_dwgu(name, h, dgu, pa, blk):
    T, D = h.shape
    tk = min(T, 2048)
    nk = T // tk

    def body(h_ref, d_ref, pa_in, o_ref, acc_ref):
        k = pl.program_id(1)
        p = _dot(h_ref[...], d_ref[0, 0], 0, 0)

        @pl.when(k == 0)
        def _():
            acc_ref[...] = p

        @pl.when(k > 0)
        def _():
            acc_ref[...] += p

        @pl.when(k == nk - 1)
        def _():
            o_ref[0, 0] = acc_ref[...].astype(BF16)

    return pl.pallas_call(
        body, grid=(4, nk),
        in_specs=[pl.BlockSpec((tk, D), lambda q, k: (k, 0)),
                  pl.BlockSpec((1, 1, tk, FF_BLK), lambda q, k: (q // 2, q % 2, k, 0)), HBM],
        out_specs=pl.BlockSpec((1, 1, D, FF_BLK), lambda q, k: (q, blk, 0, 0)),
        out_shape=jax.ShapeDtypeStruct(pa.shape, BF16), scratch_shapes=[pltpu.VMEM((D, FF_BLK), F32)],
        input_output_aliases={2: 0}, name=name, compiler_params=_params(2))(h, dgu, pa)


def _ffn_dx(name, dgu, ga, blk, x, dy, nw, out_scale):
    T, D = x.shape
    tm = min(T, 512)

    def body(d_ref, w0, w1, w2, w3, x_ref, dy_ref, nw_ref, dx_ref, dxb_ref, dnw_ref):
        i = pl.program_id(0)
        acc = (_dot(d_ref[0, 0], w0[0, 0], 1, 1) + _dot(d_ref[0, 1], w1[0, 0], 1, 1)
               + _dot(d_ref[1, 0], w2[0, 0], 1, 1) + _dot(d_ref[1, 1], w3[0, 0], 1, 1))
        dx, dw = _rms_bwd_tile(acc, x_ref[...], dy_ref[...], nw_ref[...])
        dx_ref[...] = dx
        dxb_ref[...] = (dx * out_scale).astype(BF16)

        @pl.when(i == 0)
        def _():
            dnw_ref[...] = dw

        @pl.when(i > 0)
        def _():
            dnw_ref[...] += dw

    wspec = lambda q: pl.BlockSpec((1, 1, D, FF_BLK), lambda i: (q, blk, 0, 0), pipeline_mode=pl.Buffered(1))
    row = pl.BlockSpec((tm, D), lambda i: (i, 0))
    return pl.pallas_call(
        body, grid=(T // tm,),
        in_specs=[pl.BlockSpec((2, 2, tm, FF_BLK), lambda i: (0, 0, i, 0)), wspec(0), wspec(1), wspec(2), wspec(3),
                  row, row, pl.BlockSpec((1, D), lambda i: (0, 0))],
        out_specs=[row, row, pl.BlockSpec((1, D), lambda i: (0, 0))],
        out_shape=[jax.ShapeDtypeStruct((T, D), F32), jax.ShapeDtypeStruct((T, D), BF16),
                   jax.ShapeDtypeStruct((1, D), F32)],
        name=name, compiler_params=_params(1))(dgu, ga, ga, ga, ga, x, dy, nw)


def _ffn_fwd(tag, x, nw, ga, blk, wd):
    h = _rms_fwd(tag + "_rms", x, nw)
    gu, act = _ffn_gu(tag + "_gu", h, ga, blk)
    return _ffn_down(tag + "_down", act, wd, x), (h, gu, act)


def _ffn_bwd(tag, dy, dyh, x, nw, ga, blk, wd, saved, pa, out_scale):
    h, gu, act = saved
    dgu = _ffn_dact(tag + "_dact", dyh, wd, gu)
    d_wd = _ffn_dwd(tag + "_dwd", act, dyh)
    pa = _ffn_dwgu(tag + "_dwgu", h, dgu, pa, blk)
    dx, dxb, d_nw = _ffn_dx(tag + "_dx", dgu, ga, blk, x, dy, nw, out_scale)
    return dx, dxb, d_nw, pa, d_wd


def _conv_taps(cur, halo, w, first, sign):
    tt = cur.shape[0]
    halo = jnp.where(first, 0.0, halo)
    rid = lax.broadcasted_iota(jnp.int32, (8, cur.shape[1]), 0)
    acc = w[3:4, :] * cur
    for s in (1, 2, 3):
        if sign < 0:
            sh = pltpu.roll(cur, s, 0)
            edge = jnp.where(rid < s, pltpu.roll(halo, s, 0), sh[0:8])
            sh = jnp.concatenate([edge, sh[8:]], axis=0) if tt > 8 else edge
        else:
            sh = pltpu.roll(cur, tt - s, 0)
            edge = jnp.where(rid >= 8 - s, pltpu.roll(halo, 8 - s, 0), sh[tt - 8:])
            sh = jnp.concatenate([sh[:tt - 8], edge], axis=0) if tt > 8 else edge
        acc = acc + w[3 - s:4 - s, :] * sh
    return acc


def _gdn_prep(name, proj, wconv, gate_p):
    T = proj.shape[0]
    tt = min(T, 256)
    hb = tt // 8
    nch = tt // A_CHUNK

    def body(cur_ref, halo_ref, ba_ref, w_ref, gp_ref, qkv_ref, bg_ref, gc_ref):
        first = pl.program_id(0) == 0
        for c in range(24):
            cols = pl.ds(c * 128, 128)
            conv = _conv_taps(cur_ref[:, cols], halo_ref[:, cols], w_ref[:, cols], first, -1)
            y = conv * _sigmoid(conv)
            if c < 16:
                y = y * lax.rsqrt(jnp.sum(y * y, axis=-1, keepdims=True) + EPS)
                if c < 8:
                    y = y * (A_DK ** -0.5)
            qkv_ref[:, cols] = y
        ba = ba_ref[...]
        lane = lax.broadcasted_iota(jnp.int32, ba.shape, 1)
        zarg = ba + gp_ref[1:2, :]
        softplus = jnp.maximum(zarg, 0.0) + jnp.log(1.0 + jnp.exp(-jnp.abs(zarg)))
        bg = jnp.where(lane < 8, _sigmoid(ba), jnp.where(lane < 16, gp_ref[0:1, :] * softplus, 0.0))
        bg_ref[...] = bg
        tri = (lax.broadcasted_iota(jnp.int32, (A_CHUNK, A_CHUNK), 0)
               >= lax.broadcasted_iota(jnp.int32, (A_CHUNK, A_CHUNK), 1)).astype(F32)
        for c in range(nch):
            rows = pl.ds(c * A_CHUNK, A_CHUNK)
            gc_ref[rows, :] = _dotx(tri, bg[c * A_CHUNK:(c + 1) * A_CHUNK, :])

    return pl.pallas_call(
        body, grid=(T // tt,),
        in_specs=[pl.BlockSpec((tt, 3072), lambda i: (i, 0)),
                  pl.BlockSpec((8, 3072), lambda i: (jnp.maximum(i * hb - 1, 0), 0)),
                  pl.BlockSpec((tt, 128), lambda i: (i, 32)),
                  pl.BlockSpec((4, 3072), lambda i: (0, 0)),
                  pl.BlockSpec((2, 128), lambda i: (0, 0))],
        out_specs=[pl.BlockSpec((tt, 3072), lambda i: (i, 0)), pl.BlockSpec((tt, 128), lambda i: (i, 0)),
                   pl.BlockSpec((tt, 128), lambda i: (i, 0))],
        out_shape=[jax.ShapeDtypeStruct((T, 3072), F32), jax.ShapeDtypeStruct((T, 128), F32),
                   jax.ShapeDtypeStruct((T, 128), F32)],
        name=name, compiler_params=_params(1))(proj, proj, proj, wconv, gate_p)


def _chunk_masks():
    ri = lax.broadcasted_iota(jnp.int32, (A_CHUNK, A_CHUNK), 0)
    ci = lax.broadcasted_iota(jnp.int32, (A_CHUNK, A_CHUNK), 1)
    return ri >= ci, ri > ci, ri == ci


def _chunk_local(q, k, gcol, grow, bcol):
    incl, strict, _ = _chunk_masks()
    dec = jnp.where(incl, jnp.exp(jnp.where(incl, gcol - grow, 0.0)), 0.0)
    e = jnp.exp(gcol)
    glast = grow[:, A_CHUNK - 1:A_CHUNK]
    f = jnp.exp(glast - gcol)
    gl = jnp.exp(glast)
    kb = k * bcol
    lmat = jnp.where(strict, _dotb(kb, k, 1, 1) * dec, 0.0)
    amat = jnp.where(incl, _dotb(q, k, 1, 1) * dec, 0.0)
    return dec, e, f, gl, kb, lmat, amat


def _unit_lower_inverse(lmats):
    _, _, eye = _chunk_masks()
    ts = [jnp.where(eye, 1.0, 0.0) - lm for lm in lmats]
    lps = [_doth(lm, lm) for lm in lmats]
    for it in range(5):
        ts = [t + _doth(t, lp) for t, lp in zip(ts, lps)]
        if it < 4:
            lps = [_doth(lp, lp) for lp in lps]
    return ts


def _gate_columns(bgt, gct):
    sel = (lax.broadcasted_iota(jnp.int32, (16, 128), 0) == lax.broadcasted_iota(jnp.int32, (16, 128), 1)).astype(F32)
    g_rows = _dotx(sel, gct, 1, 1)
    hs = range(A_HEADS)
    return ([bgt[:, h:h + 1] for h in hs], [gct[:, 8 + h:9 + h] for h in hs], [g_rows[8 + h:9 + h, :] for h in hs])


def _gdn_delta_fwd(name, qkv, bg, gcum):
    T = qkv.shape[0]
    tt = min(T, 512)
    nch = tt // A_CHUNK
    NC = T // A_CHUNK
    wd, ng = 128 * A_HG, A_HEADS // A_HG
    assert ng == 1

    def body(q_ref, k_ref, v_ref, bg_ref, gc_ref, o_ref, s_ref, t_ref, u_ref, w_ref, state):
        @pl.when(pl.program_id(1) == 0)
        def _():
            state[...] = jnp.zeros_like(state)

        def chunk(c, carry):
            rows = pl.ds(pl.multiple_of(c * A_CHUNK, A_CHUNK), A_CHUNK)
            hs = range(A_HG)
            cols = [pl.ds(h * 128, 128) for h in hs]
            q = [q_ref[rows, cols[h]] for h in hs]
            k = [k_ref[rows, cols[h]] for h in hs]
            v = [v_ref[rows, cols[h]] for h in hs]
            bcl, gcl, grw = _gate_columns(bg_ref[rows, :], gc_ref[rows, :])
            loc = [_chunk_local(q[h], k[h], gcl[h], grw[h], bcl[h]) for h in hs]
            e, f, gl, kb, amat = ([l[i] for l in loc] for i in (1, 2, 3, 4, 6))
            tinv = _unit_lower_inverse([l[5] for l in loc])
            u = [_doth(tinv[h], v[h] * bcl[h]) for h in hs]
            w = [_doth(tinv[h], kb[h] * e[h]) for h in hs]
            s = [state[h] for h in hs]
            vn = [u[h] - _dotb(w[h], s[h]) for h in hs]
            o_s = [_dotb(q[h] * e[h], s[h]) for h in hs]
            o_a = [_dotb(amat[h], vn[h]) for h in hs]
            s_new = [s[h] * gl[h] + _dotb(k[h] * f[h], vn[h], 0, 0) for h in hs]
            for h in hs:
                s_ref[h, c] = s[h].astype(BF16)
                t_ref[h, c] = tinv[h]
                u_ref[rows, cols[h]] = u[h]
                w_ref[rows, cols[h]] = w[h]
                o_ref[rows, cols[h]] = o_s[h] + o_a[h]
                state[h] = s_new[h]
            return carry

        lax.fori_loop(0, nch, chunk, 0)

    hd = lambda col: pl.BlockSpec((tt, wd), lambda h, i: (i, col * ng + h))
    gate_spec = pl.BlockSpec((tt, 128), lambda h, i: (i, 0))
    return pl.pallas_call(
        body, grid=(ng, T // tt),
        in_specs=[hd(0), hd(1), hd(2), gate_spec, gate_spec],
        out_specs=[pl.BlockSpec((tt, wd), lambda h, i: (i, h)),
                   pl.BlockSpec((A_HG, nch, 128, 128), lambda h, i: (h, i, 0, 0)),
                   pl.BlockSpec((A_HG, nch, A_CHUNK, A_CHUNK), lambda h, i: (h, i, 0, 0)),
                   pl.BlockSpec((tt, wd), lambda h, i: (i, h)), pl.BlockSpec((tt, wd), lambda h, i: (i, h))],
        out_shape=[jax.ShapeDtypeStruct((T, 1024), F32), jax.ShapeDtypeStruct((A_HEADS, NC, 128, 128), BF16),
                   jax.ShapeDtypeStruct((A_HEADS, NC, A_CHUNK, A_CHUNK), F32),
                   jax.ShapeDtypeStruct((T, 1024), F32), jax.ShapeDtypeStruct((T, 1024), F32)],
        scratch_shapes=[pltpu.VMEM((A_HG, 128, 128), F32)],
        name=name, compiler_params=_params(2))(qkv, qkv, qkv, bg, gcum)


def _gdn_delta_bwd(name, qkv, bg, gcum, d_o, s_sv, t_sv, u_sv, w_sv):
    T = qkv.shape[0]
    tt = min(T, 512)
    nch = tt // A_CHUNK
    ni = T // tt

    def body(q_ref, k_ref, v_ref, bg_ref, gc_ref, do_ref, s_ref, t_ref, u_ref, w_ref,
             dq_ref, dk_ref, dv_ref, dbg_ref, dstate):
        @pl.when(pl.program_id(1) == 0)
        def _():
            dstate[...] = jnp.zeros_like(dstate)

        incl, strict, _ = _chunk_masks()
        upper = (lax.broadcasted_iota(jnp.int32, (A_CHUNK, A_CHUNK), 0)
                 <= lax.broadcasted_iota(jnp.int32, (A_CHUNK, A_CHUNK), 1)).astype(F32)
        last_row = lax.broadcasted_iota(jnp.int32, (A_CHUNK, 1), 0) == A_CHUNK - 1
        ones = jnp.ones((A_CHUNK, 128), F32)

        rsum = lambda x: jnp.sum(x, axis=1, keepdims=True)

        def chunk(cc, carry):
            c = nch - 1 - cc
            rows = pl.ds(pl.multiple_of(c * A_CHUNK, A_CHUNK), A_CHUNK)
            hs = range(A_HG)
            cols = [pl.ds(h * 128, 128) for h in hs]
            q = [q_ref[rows, cols[h]] for h in hs]
            k = [k_ref[rows, cols[h]] for h in hs]
            v = [v_ref[rows, cols[h]] for h in hs]
            do = [do_ref[rows, cols[h]] for h in hs]
            u = [u_ref[rows, cols[h]] for h in hs]
            w = [w_ref[rows, cols[h]] for h in hs]
            bcl, gcl, grw = _gate_columns(bg_ref[rows, :], gc_ref[rows, :])
            s = [s_ref[h, c] for h in hs]
            tinv = [t_ref[h, c] for h in hs]
            ds = [dstate[h] for h in hs]
            loc = [_chunk_local(q[h], k[h], gcl[h], grw[h], bcl[h]) for h in hs]
            dec, e, f, gl, kb, lmat, amat = ([l[i] for l in loc] for i in range(7))
            qd = [q[h] * e[h] for h in hs]
            kd = [k[h] * f[h] for h in hs]
            ke = [kb[h] * e[h] for h in hs]
            vn = [u[h] - _dotb(w[h], s[h]) for h in hs]
            d_qd = [_dotb(do[h], s[h], 1, 1) for h in hs]
            d_a = [jnp.where(incl, _dotb(do[h], vn[h], 1, 1), 0.0) for h in hs]
            d_vn1 = [_dotb(amat[h], do[h], 0, 0) for h in hs]
            d_vn = [d_vn1[h] + _dotb(kd[h], ds[h]) for h in hs]
            d_kd = [_dotb(vn[h], ds[h], 1, 1) for h in hs]
            d_w = [-_dotb(d_vn[h], s[h], 1, 1) for h in hs]
            ds_q = [_dotb(qd[h], do[h], 0, 0) for h in hs]
            ds_w = [_dotb(w[h], d_vn[h], 0, 0) for h in hs]
            d_bv = [_doth(tinv[h], d_vn[h], 0, 0) for h in hs]
            d_ke = [_doth(tinv[h], d_w[h], 0, 0) for h in hs]
            d_l1 = [_dotb(d_bv[h], u[h], 1, 1) for h in hs]
            d_l = [-jnp.where(strict, d_l1[h] + _dotb(d_ke[h], w[h], 1, 1), 0.0) for h in hs]
            d_kk = [d_l[h] * dec[h] for h in hs]
            d_qk = [d_a[h] * dec[h] for h in hs]
            d_kb = [_dotb(d_kk[h], k[h]) for h in hs]
            dk1 = [_dotb(d_kk[h], kb[h], 0, 0) for h in hs]
            dk2 = [_dotb(d_qk[h], q[h], 0, 0) for h in hs]
            dq1 = [_dotb(d_qk[h], k[h]) for h in hs]
            m = [d_l[h] * lmat[h] + d_a[h] * amat[h] for h in hs]
            col_m = [_dotx(m[h], ones, 0, 0)[:, 0:1] for h in hs]
            d_gc = []
            for h in hs:
                d_gl = jnp.sum(jnp.sum(ds[h] * s[h].astype(F32), axis=1, keepdims=True), axis=0, keepdims=True)
                r_kd = rsum(d_kd[h] * kd[h])
                tail = jnp.sum(r_kd, axis=0, keepdims=True) + d_gl * gl[h]
                d_gc.append(rsum(m[h]) - col_m[h] + rsum(d_qd[h] * qd[h]) - r_kd + rsum(d_ke[h] * ke[h])
                            + jnp.where(last_row, tail, 0.0))
            dg = [_dotx(upper, d_gc[h] * ones)[:, 0:1] for h in hs]
            lane = lax.broadcasted_iota(jnp.int32, (A_CHUNK, 128), 1)
            dbg = jnp.zeros((A_CHUNK, 128), F32)
            for h in hs:
                dstate[h] = gl[h] * ds[h] + ds_q[h] - ds_w[h]
                dk_ref[rows, cols[h]] = (dk1[h] + dk2[h] + d_kd[h] * f[h] + d_ke[h] * (bcl[h] * e[h])
                                         + d_kb[h] * bcl[h])
                dq_ref[rows, cols[h]] = dq1[h] + d_qd[h] * e[h]
                dv_ref[rows, cols[h]] = d_bv[h] * bcl[h]
                d_beta = rsum(d_ke[h] * k[h]) * e[h] + rsum(d_kb[h] * k[h]) + rsum(d_bv[h] * v[h])
                dbg = jnp.where(lane == h, d_beta, jnp.where(lane == 8 + h, dg[h], dbg))
            dbg_ref[rows, :] = dbg
            return carry

        lax.fori_loop(0, nch, chunk, 0)

    wd, ng = 128 * A_HG, A_HEADS // A_HG
    assert ng == 1
    rev = lambda i: ni - 1 - i
    hd = lambda col: pl.BlockSpec((tt, wd), lambda h, i: (rev(i), col * ng + h))
    hd1 = pl.BlockSpec((tt, wd), lambda h, i: (rev(i), h))
    gate_spec = pl.BlockSpec((tt, 128), lambda h, i: (rev(i), 0))
    return pl.pallas_call(
        body, grid=(ng, ni),
        in_specs=[hd(0), hd(1), hd(2), gate_spec, gate_spec, hd1,
                  pl.BlockSpec((A_HG, nch, 128, 128), lambda h, i: (h, rev(i), 0, 0)),
                  pl.BlockSpec((A_HG, nch, A_CHUNK, A_CHUNK), lambda h, i: (h, rev(i), 0, 0)), hd1, hd1],
        out_specs=[hd1, hd1, hd1, gate_spec],
        out_shape=[jax.ShapeDtypeStruct((T, 1024), F32)] * 3 + [jax.ShapeDtypeStruct((T, 128), F32)],
        scratch_shapes=[pltpu.VMEM((A_HG, 128, 128), F32)],
        name=name, compiler_params=_params(2))(qkv, qkv, qkv, bg, gcum, d_o, s_sv, t_sv, u_sv, w_sv)


def _gdn_gate_fwd(name, o, proj, nw):
    T = o.shape[0]
    tt = min(T, 512)

    def body(o_ref, z_ref, nw_ref, y_ref):
        for h in range(A_HEADS):
            cols = pl.ds(h * 128, 128)
            ov, z = o_ref[:, cols], z_ref[:, cols]
            r = lax.rsqrt(jnp.mean(ov * ov, axis=-1, keepdims=True) + EPS)
            y_ref[:, cols] = (ov * r * nw_ref[...] * (z * _sigmoid(z))).astype(BF16)

    return pl.pallas_call(
        body, grid=(T // tt,),
        in_specs=[pl.BlockSpec((tt, 1024), lambda i: (i, 0)), pl.BlockSpec((tt, 1024), lambda i: (i, 3)),
                  pl.BlockSpec((1, 128), lambda i: (0, 0))],
        out_specs=pl.BlockSpec((tt, 1024), lambda i: (i, 0)),
        out_shape=jax.ShapeDtypeStruct((T, 1024), BF16), name=name, compiler_params=_params(1))(o, proj, nw)


def _gdn_gate_bwd(name, dy2, o, proj, nw):
    T = o.shape[0]
    tt = min(T, 512)

    def body(dy_ref, o_ref, z_ref, nw_ref, do_ref, dz_ref, dnw_ref):
        dnw = jnp.zeros((1, 128), F32)
        for h in range(A_HEADS):
            cols = pl.ds(h * 128, 128)
            dy, ov, z = dy_ref[:, cols], o_ref[:, cols], z_ref[:, cols]
            s = _sigmoid(z)
            sz = z * s
            r = lax.rsqrt(jnp.mean(ov * ov, axis=-1, keepdims=True) + EPS)
            xhat = ov * r
            dn = dy * sz
            dz_ref[:, cols] = dy * (xhat * nw_ref[...]) * (s + z * s * (1.0 - s))
            dxhat = dn * nw_ref[...]
            do_ref[:, cols] = r * (dxhat - xhat * jnp.mean(dxhat * xhat, axis=-1, keepdims=True))
            dnw = dnw + jnp.sum(dn * xhat, axis=0, keepdims=True)

        @pl.when(pl.program_id(0) == 0)
        def _():
            dnw_ref[...] = dnw

        @pl.when(pl.program_id(0) > 0)
        def _():
            dnw_ref[...] += dnw

    blk = lambda c: pl.BlockSpec((tt, 1024), lambda i: (i, c))
    return pl.pallas_call(
        body, grid=(T // tt,),
        in_specs=[blk(0), blk(0), blk(3), pl.BlockSpec((1, 128), lambda i: (0, 0))],
        out_specs=[blk(0), blk(0), pl.BlockSpec((1, 128), lambda i: (0, 0))],
        out_shape=[jax.ShapeDtypeStruct((T, 1024), F32), jax.ShapeDtypeStruct((T, 1024), F32),
                   jax.ShapeDtypeStruct((1, 128), F32)],
        name=name, compiler_params=_params(1))(dy2, o, proj, nw)


def _gdn_prep_bwd1(name, proj, wconv, gate_p, dq, dk, dv, dbg):
    T = proj.shape[0]
    tt = min(T, 256)
    hb = tt // 8

    def body(cur_ref, halo_ref, ba_ref, w_ref, gp_ref, dq_ref, dk_ref, dv_ref, dbg_ref,
             dc_ref, dw_ref, dba_ref, dgp_ref):
        first = pl.program_id(0) == 0
        rid = lax.broadcasted_iota(jnp.int32, (8, 128), 0)
        for c in range(24):
            cols = pl.ds(c * 128, 128)
            cur = cur_ref[:, cols]
            halo = jnp.where(first, 0.0, halo_ref[:, cols])
            conv = _conv_taps(cur, halo_ref[:, cols], w_ref[:, cols], first, -1)
            s = _sigmoid(conv)
            y = conv * s
            if c < 16:
                dref = dq_ref if c < 8 else dk_ref
                dn = dref[:, pl.ds((c % 8) * 128, 128)]
                rinv = lax.rsqrt(jnp.sum(y * y, axis=-1, keepdims=True) + EPS)
                yhat = y * rinv
                dyv = rinv * (dn - yhat * jnp.sum(dn * yhat, axis=-1, keepdims=True))
                if c < 8:
                    dyv = dyv * (A_DK ** -0.5)
            else:
                dyv = dv_ref[:, pl.ds((c - 16) * 128, 128)]
            dc = dyv * (s + conv * s * (1.0 - s))
            dc_ref[:, cols] = dc
            parts = [jnp.sum(dc * cur, axis=0, keepdims=True)]
            for sft in (1, 2, 3):
                sh = pltpu.roll(cur, sft, 0)
                edge = jnp.where(rid < sft, pltpu.roll(halo, sft, 0), sh[0:8])
                sh = jnp.concatenate([edge, sh[8:]], axis=0) if tt > 8 else edge
                parts.append(jnp.sum(dc * sh, axis=0, keepdims=True))
            dwc = jnp.concatenate(parts[::-1], axis=0)

            @pl.when(first)
            def _():
                dw_ref[:, cols] = dwc

            @pl.when(jnp.logical_not(first))
            def _():
                dw_ref[:, cols] += dwc

        ba = ba_ref[...]
        dbg = dbg_ref[...]
        lane = lax.broadcasted_iota(jnp.int32, ba.shape, 1)
        sb = _sigmoid(ba)
        zarg = ba + gp_ref[1:2, :]
        softplus = jnp.maximum(zarg, 0.0) + jnp.log(1.0 + jnp.exp(-jnp.abs(zarg)))
        d_b = dbg * sb * (1.0 - sb)
        d_a = dbg * gp_ref[0:1, :] * _sigmoid(zarg)
        dba_ref[...] = jnp.where(lane < 8, d_b, jnp.where(lane < 16, d_a, 0.0))
        g = gp_ref[0:1, :] * softplus
        in_a = (lane >= 8) & (lane < 16)
        sums = jnp.concatenate([jnp.sum(jnp.where(in_a, dbg * g, 0.0), axis=0, keepdims=True),
                                jnp.sum(jnp.where(in_a, d_a, 0.0), axis=0, keepdims=True)], axis=0)

        @pl.when(first)
        def _():
            dgp_ref[...] = sums

        @pl.when(jnp.logical_not(first))
        def _():
            dgp_ref[...] += sums

    row = lambda w, c=0: pl.BlockSpec((tt, w), lambda i: (i, c))
    return pl.pallas_call(
        body, grid=(T // tt,),
        in_specs=[row(3072), pl.BlockSpec((8, 3072), lambda i: (jnp.maximum(i * hb - 1, 0), 0)), row(128, 32),
                  pl.BlockSpec((4, 3072), lambda i: (0, 0)), pl.BlockSpec((2, 128), lambda i: (0, 0)),
                  row(1024), row(1024), row(1024), row(128)],
        out_specs=[row(3072), pl.BlockSpec((4, 3072), lambda i: (0, 0)), row(128),
                   pl.BlockSpec((2, 128), lambda i: (0, 0))],
        out_shape=[jax.ShapeDtypeStruct((T, 3072), F32), jax.ShapeDtypeStruct((4, 3072), F32),
                   jax.ShapeDtypeStruct((T, 128), F32), jax.ShapeDtypeStruct((2, 128), F32)],
        name=name, compiler_params=_params(1))(proj, proj, proj, wconv, gate_p, dq, dk, dv, dbg)


def _gdn_prep_bwd2(name, dc, wconv, dz, dba):
    T = dc.shape[0]
    tt = min(T, 256)
    hb = tt // 8
    ni = T // tt

    def body(cur_ref, halo_ref, w_ref, dz_ref, dba_ref, o_ref):
        last = pl.program_id(0) == ni - 1
        for c in range(24):
            cols = pl.ds(c * 128, 128)
            o_ref[:, cols] = _conv_taps(cur_ref[:, cols], halo_ref[:, cols], w_ref[:, cols], last, +1).astype(BF16)
        o_ref[:, pl.ds(3072, 1024)] = dz_ref[...].astype(BF16)
        o_ref[:, pl.ds(4096, 128)] = dba_ref[...].astype(BF16)

    return pl.pallas_call(
        body, grid=(ni,),
        in_specs=[pl.BlockSpec((tt, 3072), lambda i: (i, 0)),
                  pl.BlockSpec((8, 3072), lambda i: (jnp.minimum((i + 1) * hb, T // 8 - 1), 0)),
                  pl.BlockSpec((4, 3072), lambda i: (0, 0)),
                  pl.BlockSpec((tt, 1024), lambda i: (i, 0)), pl.BlockSpec((tt, 128), lambda i: (i, 0))],
        out_specs=pl.BlockSpec((tt, A_IN_PAD), lambda i: (i, 0)),
        out_shape=jax.ShapeDtypeStruct((T, A_IN_PAD), BF16), name=name, compiler_params=_params(1))(
            dc, dc, wconv, dz, dba)


def _gdn_fwd(x, nw, w_in, wconv, gate_p, out_nw, w_out):
    h = _rms_fwd("a_rms", x, nw)
    proj = _mm_plain("a_proj", h, w_in, 1, 0, F32, tn=FF_BLK)
    qkv, bg, gcum = _gdn_prep("a_prep", proj, wconv, gate_p)
    o, s_sv, t_sv, u_sv, w_sv = _gdn_delta_fwd("a_delta", qkv, bg, gcum)
    o2 = _gdn_gate_fwd("a_gate", o, proj, out_nw)
    y = _mm_residual("a_out", o2, w_out, x, 1.0)
    return y, (h, proj, qkv, bg, gcum, o, s_sv, t_sv, u_sv, w_sv, o2)


def _gdn_bwd(dy, dyb, x, nw, w_in, wconv, gate_p, out_nw, w_out, saved, out_scale):
    h, proj, qkv, bg, gcum, o, s_sv, t_sv, u_sv, w_sv, o2 = saved
    d_o2 = _mm_plain("a_dout", dyb, w_out, 1, 1, F32)
    d_wout = _mm_plain("a_dwout", o2, dyb, 0, 0, F32)
    d_o, d_z, d_outnw = _gdn_gate_bwd("a_dgate", d_o2, o, proj, out_nw)
    dq, dk, dv, dbg = _gdn_delta_bwd("a_ddelta", qkv, bg, gcum, d_o, s_sv, t_sv, u_sv, w_sv)
    dc, d_wconv, dba, dgp = _gdn_prep_bwd1("a_dprep1", proj, wconv, gate_p, dq, dk, dv, dbg)
    dproj = _gdn_prep_bwd2("a_dprep2", dc, wconv, d_z, dba)
    d_win = _mm_plain("a_dwin", h, dproj, 0, 0, F32, tn=FF_BLK, tk=2048)
    dx, dxb, d_nw = _mm_rms_bwd("a_dx", dproj, w_in, x, dy, nw, A_IN_PAD, out_scale)
    return dx, dxb, d_nw, d_win, d_wconv, dgp, d_outnw, d_wout


def _swa_masks(n):
    qi = lax.broadcasted_iota(jnp.int32, (B_BLK, B_BLK), 0)
    kj = lax.broadcasted_iota(jnp.int32, (B_BLK, B_BLK), 1)
    return kj > qi + jnp.where(n > 0, 0, B_BLK), kj <= qi


def _swa_fwd(name, q, k, v, sinks):
    T = q.shape[1]
    tq = min(T, 1024)
    nbt = tq // B_BLK
    scale = B_HD ** -0.5
    G = B_HEADS // B_KV

    def body(q_ref, k_ref, v_ref, kh_ref, vh_ref, s_ref, o_ref, l_ref):
        first_blk = pl.program_id(1) * nbt

        def block(n, kp, vp):
            m_prev, m_cur = _swa_masks(first_blk + n)
            cur = pl.ds(pl.multiple_of(n * B_BLK, B_BLK), B_BLK)
            kc, vc = k_ref[0, cur, :], v_ref[0, cur, :]
            gs = range(G)
            rmax = lambda a: jnp.max(a, axis=1, keepdims=True)
            rsum = lambda a: jnp.sum(a, axis=1, keepdims=True)
            sink = [s_ref[g][:, 0:1] for g in gs]
            qb = [q_ref[g, cur, :] for g in gs]
            s_p = [jnp.where(m_prev, _dot(qb[g], kp, 1, 1) * scale, -jnp.inf) for g in gs]
            s_c = [jnp.where(m_cur, _dot(qb[g], kc, 1, 1) * scale, -jnp.inf) for g in gs]
            m = [jnp.maximum(jnp.maximum(rmax(s_p[g]), rmax(s_c[g])), sink[g]) for g in gs]
            p_p = [jnp.exp(s_p[g] - m[g]) for g in gs]
            p_c = [jnp.exp(s_c[g] - m[g]) for g in gs]
            den = [rsum(p_p[g]) + rsum(p_c[g]) + jnp.exp(sink[g] - m[g]) for g in gs]
            a_p = [_dotb(p_p[g], vp) for g in gs]
            a_c = [_dotb(p_c[g], vc) for g in gs]
            for g in gs:
                o_ref[g, cur, :] = ((a_p[g] + a_c[g]) / den[g]).astype(BF16)
                l_ref[g, cur, :] = m[g] + jnp.log(den[g])

        block(0, kh_ref[0], vh_ref[0])

        def rest(n, carry):
            prv = pl.ds(pl.multiple_of((n - 1) * B_BLK, B_BLK), B_BLK)
            block(n, k_ref[0, prv, :], v_ref[0, prv, :])
            return carry

        lax.fori_loop(1, nbt, rest, 0)

    qs = pl.BlockSpec((G, tq, B_HD), lambda kv, i: (kv, i, 0))
    ks = pl.BlockSpec((1, tq, B_HD), lambda kv, i: (kv, i, 0))
    halo = pl.BlockSpec((1, B_BLK, B_HD), lambda kv, i: (kv, jnp.maximum(i * nbt - 1, 0), 0))
    return pl.pallas_call(
        body, grid=(B_KV, T // tq),
        in_specs=[qs, ks, ks, halo, halo, pl.BlockSpec((G, 1, 128), lambda kv, i: (kv, 0, 0))],
        out_specs=[qs, pl.BlockSpec((G, tq, 1), lambda kv, i: (kv, i, 0))],
        out_shape=[jax.ShapeDtypeStruct((B_HEADS, T, B_HD), BF16), jax.ShapeDtypeStruct((B_HEADS, T, 1), F32)],
        name=name, compiler_params=_params(2))(q, k, v, k, v, sinks)


def _swa_bwd(name, q, k, v, sinks, o, lse, do):
    T = q.shape[1]
    tq = min(T, 1024)
    nbt, ni = tq // B_BLK, T // tq
    scale = B_HD ** -0.5
    G = B_HEADS // B_KV

    def body(q_ref, k_ref, v_ref, kh_ref, vh_ref, s_ref, o_ref, l_ref, do_ref, dq_ref, dk_ref, dv_ref, ds_ref,
             dk_halo, dv_halo):
        step = pl.program_id(1)
        first_blk = (ni - 1 - step) * nbt
        last = pl.ds(tq - B_BLK, B_BLK)
        dk_ref[...] = jnp.zeros_like(dk_ref)
        dv_ref[...] = jnp.zeros_like(dv_ref)

        @pl.when(step > 0)
        def _():
            dk_ref[0, last, :] = dk_halo[...]
            dv_ref[0, last, :] = dv_halo[...]

        def block(n, kp, vp, dsinks):
            m_prev, m_cur = _swa_masks(first_blk + n)
            cur = pl.ds(pl.multiple_of(n * B_BLK, B_BLK), B_BLK)
            kc, vc = k_ref[0, cur, :], v_ref[0, cur, :]
            gs = range(G)
            sink = [s_ref[g][:, 0:1] for g in gs]
            qb = [q_ref[g, cur, :] for g in gs]
            dob = [do_ref[g, cur, :] for g in gs]
            lse_b = [l_ref[g, cur, :] for g in gs]
            p_p = [jnp.where(m_prev, jnp.exp(_dot(qb[g], kp, 1, 1) * scale - lse_b[g]), 0.0) for g in gs]
            p_c = [jnp.where(m_cur, jnp.exp(_dot(qb[g], kc, 1, 1) * scale - lse_b[g]), 0.0) for g in gs]
            delta = [jnp.sum(dob[g].astype(F32) * o_ref[g, cur, :].astype(F32), axis=1, keepdims=True) for g in gs]
            ds_p = [p_p[g] * (_dot(dob[g], vp, 1, 1) - delta[g]) for g in gs]
            ds_c = [p_c[g] * (_dot(dob[g], vc, 1, 1) - delta[g]) for g in gs]
            dq_p = [_dotb(ds_p[g], kp) for g in gs]
            dq_c = [_dotb(ds_c[g], kc) for g in gs]
            dk_ps = [_dotb(ds_p[g], qb[g], 0, 0) for g in gs]
            dk_cs = [_dotb(ds_c[g], qb[g], 0, 0) for g in gs]
            dv_ps = [_dotb(p_p[g], dob[g], 0, 0) for g in gs]
            dv_cs = [_dotb(p_c[g], dob[g], 0, 0) for g in gs]
            for g in gs:
                dq_ref[g, cur, :] = (dq_p[g] + dq_c[g]) * scale
            out = tuple(dsinks[g] - jnp.sum(jnp.exp(sink[g] - lse_b[g]) * delta[g], axis=0, keepdims=True) for g in gs)
            total = lambda parts: (parts[0] + parts[1]) + (parts[2] + parts[3])
            dk_ref[0, cur, :] += total(dk_cs) * scale
            dv_ref[0, cur, :] += total(dv_cs)
            return total(dk_ps) * scale, total(dv_ps), out

        zeros = tuple(jnp.zeros((1, 1), F32) for _ in range(G))
        dk_p, dv_p, dsinks = block(0, kh_ref[0], vh_ref[0], zeros)
        dk_halo[...] = dk_p
        dv_halo[...] = dv_p

        def rest(n, dsinks):
            prv = pl.ds(pl.multiple_of((n - 1) * B_BLK, B_BLK), B_BLK)
            dk_p, dv_p, dsinks = block(n, k_ref[0, prv, :], v_ref[0, prv, :], dsinks)
            dk_ref[0, prv, :] += dk_p
            dv_ref[0, prv, :] += dv_p
            return dsinks

        dsinks = lax.fori_loop(1, nbt, rest, dsinks)
        for g in range(G):
            row = jnp.broadcast_to(dsinks[g], (1, 128))

            @pl.when(step == 0)
            def _():
                ds_ref[g] = row

            @pl.when(step > 0)
            def _():
                ds_ref[g] += row

    rev = lambda i: ni - 1 - i
    qs = pl.BlockSpec((G, tq, B_HD), lambda kv, i: (kv, rev(i), 0))
    ks = pl.BlockSpec((1, tq, B_HD), lambda kv, i: (kv, rev(i), 0))
    halo = pl.BlockSpec((1, B_BLK, B_HD), lambda kv, i: (kv, jnp.maximum(rev(i) * nbt - 1, 0), 0))
    ss = pl.BlockSpec((G, 1, 128), lambda kv, i: (kv, 0, 0))
    return pl.pallas_call(
        body, grid=(B_KV, ni),
        in_specs=[qs, ks, ks, halo, halo, ss, qs, pl.BlockSpec((G, tq, 1), lambda kv, i: (kv, rev(i), 0)), qs],
        out_specs=[qs, ks, ks, ss],
        out_shape=[jax.ShapeDtypeStruct((B_HEADS, T, B_HD), F32), jax.ShapeDtypeStruct((B_KV, T, B_HD), F32),
                   jax.ShapeDtypeStruct((B_KV, T, B_HD), F32), jax.ShapeDtypeStruct((B_HEADS, 1, 128), F32)],
        scratch_shapes=[pltpu.VMEM((B_BLK, B_HD), F32), pltpu.VMEM((B_BLK, B_HD), F32)],
        name=name, compiler_params=_params(2))(q, k, v, k, v, sinks, o, lse, do)


def _split_heads(a, n):
    T = a.shape[0]
    return a.reshape(T, n, B_HD).transpose(1, 0, 2)


def _merge_heads(a):
    n, T, _ = a.shape
    return a.transpose(1, 0, 2).reshape(T, n * B_HD)


def _swa_mixer_fwd(x, nw, w_in, b_in, sinks, w_out, b_out):
    h = _rms_fwd("b_rms", x, nw)
    proj = _mm_plain("b_proj", h, w_in, 1, 0, BF16, tn=768, bias=b_in)
    q, k, v = _split_heads(proj[:, :1024], B_HEADS), _split_heads(proj[:, 1024:1280], B_KV), _split_heads(proj[:, 1280:], B_KV)
    o, lse = _swa_fwd("b_attn", q, k, v, sinks)
    om = _merge_heads(o)
    y = _mm_residual("b_out", om, w_out, x, 1.0, bias=b_out)
    return y, (h, q, k, v, o, lse, om)


def _swa_mixer_bwd(dy, dyb, x, nw, w_in, sinks, w_out, saved, out_scale):
    h, q, k, v, o, lse, om = saved
    d_om = _mm_plain("b_dout", dyb, w_out, 1, 1, BF16)
    d_wout = _mm_plain("b_dwout", om, dyb, 0, 0, F32)
    d_bout = _colsum("b_dbout", dy)
    dq, dk, dv, dsinks = _swa_bwd("b_dattn", q, k, v, sinks, o, lse, _split_heads(d_om, B_HEADS))
    dproj = jnp.concatenate([_merge_heads(dq), _merge_heads(dk), _merge_heads(dv)], axis=1)
    d_bin = _colsum("b_dbin", dproj)
    d_win = _mm_plain("b_dwin", h, dproj, 0, 0, F32, tn=768)
    dx, dxb, d_nw = _mm_rms_bwd("b_dx", dproj, w_in, x, dy, nw, 768, out_scale)
    return dx, dxb, d_nw, d_win, d_bin, dsinks[:, 0, 0], d_wout, d_bout


def _loss_head(name, x, tgt, fw, out_scale):
    T, D = x.shape
    tt = min(T, 512)

    def body(x_ref, t_ref, w_ref, dx_ref, dxb_ref, loss_ref, dw_ref):
        xv = x_ref[...]
        r = lax.rsqrt(jnp.mean(xv * xv, axis=-1, keepdims=True) + EPS)
        xhat = xv * r
        diff = xhat * w_ref[...] - t_ref[...]
        part = 0.5 * jnp.sum(jnp.mean(diff * diff, axis=-1, keepdims=True), axis=0, keepdims=True)
        dyv = diff * (1.0 / D)
        dxhat = dyv * w_ref[...]
        dx = r * (dxhat - xhat * jnp.mean(dxhat * xhat, axis=-1, keepdims=True))
        dx_ref[...] = dx
        dxb_ref[...] = (dx * out_scale).astype(BF16)
        dw = jnp.sum(dyv * xhat, axis=0, keepdims=True)
        lp = jnp.broadcast_to(part, (1, 128))

        @pl.when(pl.program_id(0) == 0)
        def _():
            loss_ref[...] = lp
            dw_ref[...] = dw

        @pl.when(pl.program_id(0) > 0)
        def _():
            loss_ref[...] += lp
            dw_ref[...] += dw

    row = pl.BlockSpec((tt, D), lambda i: (i, 0))
    return pl.pallas_call(
        body, grid=(T // tt,), in_specs=[row, row, pl.BlockSpec((1, D), lambda i: (0, 0))],
        out_specs=[row, row, pl.BlockSpec((1, 128), lambda i: (0, 0)), pl.BlockSpec((1, D), lambda i: (0, 0))],
        out_shape=[jax.ShapeDtypeStruct((T, D), F32), jax.ShapeDtypeStruct((T, D), BF16),
                   jax.ShapeDtypeStruct((1, 128), F32), jax.ShapeDtypeStruct((1, D), F32)],
        name=name, compiler_params=_params(1))(x, tgt, fw)


def _local_step(x, tgt, wts):
    W = wts
    g = {}
    ga, wdn = W["ga"], W["w_down"]
    n1, n2, nm = W["ffn1_norm"], W["ffn2_norm"], W["mix_norm"]
    x1, sv1 = _ffn_fwd("f10", x, n1[0:1], ga, 0, wdn[0])
    x2, sva = _gdn_fwd(x1, nm[0:1], W["a_w_in"], W["a_w_conv"], W["a_gate_p"], W["a_out_norm"], W["a_w_out"])
    x3, sv3 = _ffn_fwd("f20", x2, n2[0:1], ga, 2, wdn[2])
    x4, sv4 = _ffn_fwd("f11", x3, n1[1:2], ga, 1, wdn[1])
    x5, svb = _swa_mixer_fwd(x4, nm[1:2], W["b_w_in"], W["b_b_in"], W["b_sinks"], W["b_w_out"], W["b_b_out"])
    x6, sv6 = _ffn_fwd("f21", x5, n2[1:2], ga, 3, wdn[3])
    dx, dxb, loss_p, g["final_norm"] = _loss_head("loss_head", x6, tgt, W["final_norm"], 0.5)

    pa = jnp.zeros(ga.shape, BF16)
    dx, dxb, n21, pa, wd21 = _ffn_bwd("f21", dx, dxb, x5, n2[1:2], ga, 3, wdn[3], sv6, pa, 1.0)
    dx, dxb, nb, g["b_w_in"], g["b_b_in"], g["b_sinks"], g["b_w_out"], g["b_b_out"] = _swa_mixer_bwd(
        dx, dxb, x4, nm[1:2], W["b_w_in"], W["b_sinks"], W["b_w_out"], svb, 0.5)
    dx, dxb, n11, pa, wd11 = _ffn_bwd("f11", dx, dxb, x3, n1[1:2], ga, 1, wdn[1], sv4, pa, 0.5)
    dx, dxb, n20, pa, wd20 = _ffn_bwd("f20", dx, dxb, x2, n2[0:1], ga, 2, wdn[2], sv3, pa, 1.0)
    dx, dxb, na, g["a_w_in"], g["a_w_conv"], g["a_gate_p"], g["a_out_norm"], g["a_w_out"] = _gdn_bwd(
        dx, dxb, x1, nm[0:1], W["a_w_in"], W["a_w_conv"], W["a_gate_p"], W["a_out_norm"], W["a_w_out"], sva, 0.5)
    dx, dxb, n10, pa, wd10 = _ffn_bwd("f10", dx, dxb, x, n1[0:1], ga, 0, wdn[0], sv1, pa, 1.0)

    g["ffn1_norm"] = jnp.concatenate([n10, n11], axis=0)
    g["ffn2_norm"] = jnp.concatenate([n20, n21], axis=0)
    g["mix_norm"] = jnp.concatenate([na, nb], axis=0)
    g["ga"] = pa
    g["w_down"] = jnp.stack([wd10, wd11, wd20, wd21])
    return loss_p, dx, g


A_ROWS = 4 * D_MODEL
PACK = (("ffn1_w_down", 1408), ("ffn2_w_down", 1408), ("a_w_in", 1028), ("a_w_out", 256), ("b_w_in", 384),
        ("b_w_out", 256))
PACK_TILE = 16
PACK_USED = sum(-(-n // PACK_TILE) * PACK_TILE for _, n in PACK)
PACK_ROWS = 4864
assert PACK_USED <= PACK_ROWS
SMALL_SHARD = (8, 512)
MOVE_ROWS = {"a": 512, "b": 608}
SUM_ROWS = {"a": 256, "b": 304}


def _mesh_pos():
    x, y, c = lax.axis_index("x"), lax.axis_index("y"), lax.axis_index("c")
    return x, y, c, [(1 - x, y), (x, 1 - y), (1 - x, 1 - y)]


def _half(rows, c):
    return pl.ds(pl.multiple_of(c * (rows // 2), 16), rows // 2)


def _gather_chips(big_a, big_b, small):
    bufs = (big_a, big_b, small)

    def body(a_ref, b_ref, small_ref, ra_ref, rb_ref, rs_ref, send_sems, recv_sems):
        x, y, c, chips = _mesh_pos()
        srcs = (a_ref.at[_half(A_ROWS, c)], b_ref.at[_half(PACK_ROWS, c)], small_ref)
        send = []
        for j, chip in enumerate(chips):
            for n, (src, dst) in enumerate(zip(srcs, (ra_ref, rb_ref, rs_ref))):
                send.append(pltpu.make_async_remote_copy(src_ref=src, dst_ref=dst.at[j],
                                                         send_sem=send_sems.at[3 * j + n], recv_sem=recv_sems.at[3 * j + n],
                                                         device_id=(*chip, c), device_id_type=MESH))
        for cp in send:
            cp.start()
        for cp in send:
            cp.wait_recv()
        for cp in send:
            cp.wait_send()

    halves = [jax.ShapeDtypeStruct((3, b.shape[0] // 2, b.shape[1]), BF16) for b in bufs[:2]]
    return pl.pallas_call(
        body, name="gather_chips", in_specs=[HBM, HBM, HBM], out_specs=[HBM, HBM, HBM],
        out_shape=halves + [jax.ShapeDtypeStruct((3,) + SMALL_SHARD, F32)],
        scratch_shapes=[pltpu.SemaphoreType.DMA((9,)), pltpu.SemaphoreType.DMA((9,))])(*bufs)


def _gather_fill(tag, big, recv):
    rows_all, width = big.shape
    half, mv = rows_all // 2, MOVE_ROWS[tag]
    nt = half // mv
    own_tiles = rows_all // mv
    assert half % mv == 0 and own_tiles <= 3 * nt

    def body(recv_ref, big_ref, g_ref, send_sem, recv_sem, local_sems):
        x, y, c, chips = _mesh_pos()
        j, t = pl.program_id(0), pl.program_id(1)
        step = j * nt + t
        src_chip = jnp.where(j == 0, 2 * (1 - x) + y, jnp.where(j == 1, 2 * x + 1 - y, 2 * (1 - x) + 1 - y))
        rows = pl.ds(pl.multiple_of(c * half + t * mv, 16), mv)
        keep = pltpu.make_async_copy(recv_ref.at[0], g_ref.at[src_chip, rows], local_sems.at[0])
        give = pltpu.make_async_remote_copy(src_ref=recv_ref.at[0], dst_ref=g_ref.at[src_chip, rows],
                                            send_sem=send_sem, recv_sem=recv_sem,
                                            device_id=(x, y, 1 - c), device_id_type=MESH)
        keep.start()
        give.start()

        @pl.when(step < own_tiles)
        def _():
            own_rows = pl.ds(pl.multiple_of(step * mv, 16), mv)
            own = pltpu.make_async_copy(big_ref, g_ref.at[2 * x + y, own_rows], local_sems.at[1])
            own.start()
            own.wait()

        give.wait_send()
        keep.wait()

        @pl.when(step == 3 * nt - 1)
        def _():
            landed = g_ref.at[pl.ds(0, 3), pl.ds(0, half)]
            pltpu.make_async_remote_copy(src_ref=landed, dst_ref=landed, send_sem=send_sem, recv_sem=recv_sem,
                                         device_id=(x, y, c), device_id_type=MESH).wait_recv()

    return pl.pallas_call(
        body, grid=(3, nt),
        in_specs=[pl.BlockSpec((1, mv, width), lambda j, t: (j, t, 0)),
                  pl.BlockSpec((mv, width), lambda j, t: (jnp.minimum(j * nt + t, own_tiles - 1), 0))],
        out_specs=HBM, out_shape=jax.ShapeDtypeStruct((4, rows_all, width), BF16),
        scratch_shapes=[pltpu.SemaphoreType.DMA, pltpu.SemaphoreType.DMA, pltpu.SemaphoreType.DMA((2,))],
        name="gather_fill_" + tag, compiler_params=_params(2))(recv, big)


def _pair_send(tag, p):
    _, rows_all, width = p.shape
    half, mv = rows_all // 2, MOVE_ROWS[tag]
    nt = half // mv

    def body(p_ref, a_ref, send_sem, recv_sem):
        x, y, c, _ = _mesh_pos()
        s, t = pl.program_id(0), pl.program_id(1)
        rows = pl.ds(pl.multiple_of(t * mv, 16), mv)
        give = pltpu.make_async_remote_copy(src_ref=p_ref.at[0], dst_ref=a_ref.at[s, rows], send_sem=send_sem,
                                            recv_sem=recv_sem, device_id=(x, y, 1 - c), device_id_type=MESH)
        give.start()
        give.wait_send()

        @pl.when((s == 3) & (t == nt - 1))
        def _():
            pltpu.make_async_remote_copy(src_ref=a_ref, dst_ref=a_ref, send_sem=send_sem, recv_sem=recv_sem,
                                         device_id=(x, y, c), device_id_type=MESH).wait_recv()

    return pl.pallas_call(
        body, grid=(4, nt),
        in_specs=[pl.BlockSpec((1, mv, width), lambda s, t: (s, (1 - lax.axis_index("c")) * nt + t, 0))],
        out_specs=HBM, out_shape=jax.ShapeDtypeStruct((4, half, width), BF16),
        scratch_shapes=[pltpu.SemaphoreType.DMA, pltpu.SemaphoreType.DMA],
        name="pair_send_" + tag, compiler_params=_params(2))(p)


def _pair_sum(tag, p, a):
    _, half, width = a.shape
    sr = SUM_ROWS[tag]
    nt = half // sr
    assert half % sr == 0

    def body(p_ref, a_ref, o_ref):
        o_ref[...] = (p_ref[...].astype(F32) + a_ref[...].astype(F32)).astype(BF16)

    spec = pl.BlockSpec((1, sr, width), lambda s, t: (s, t, 0))
    return pl.pallas_call(
        body, grid=(4, nt),
        in_specs=[pl.BlockSpec((1, sr, width), lambda s, t: (s, lax.axis_index("c") * nt + t, 0)), spec],
        out_specs=spec, out_shape=jax.ShapeDtypeStruct((4, half, width), BF16),
        name="pair_sum_" + tag, compiler_params=_params(2))(p, a)


def _chip_exchange(cs_a, cs_b):
    def body(ca_ref, cb_ref, ba_ref, bb_ref, send_sems, recv_sems):
        x, y, c, chips = _mesh_pos()
        send = []
        for j, chip in enumerate(chips):
            for n, (src, dst) in enumerate(((ca_ref, ba_ref), (cb_ref, bb_ref))):
                send.append(pltpu.make_async_remote_copy(src_ref=src.at[2 * chip[0] + chip[1]], dst_ref=dst.at[j],
                                                         send_sem=send_sems.at[2 * j + n], recv_sem=recv_sems.at[2 * j + n],
                                                         device_id=(*chip, c), device_id_type=MESH))
        for cp in send:
            cp.start()
        for cp in send:
            cp.wait_recv()
        for cp in send:
            cp.wait_send()

    return pl.pallas_call(
        body, name="chip_exchange", in_specs=[HBM, HBM], out_specs=[HBM, HBM],
        out_shape=[jax.ShapeDtypeStruct((3,) + cs.shape[1:], BF16) for cs in (cs_a, cs_b)],
        scratch_shapes=[pltpu.SemaphoreType.DMA((6,)), pltpu.SemaphoreType.DMA((6,))])(cs_a, cs_b)


def _chip_sum(tag, cs, b):
    _, half, width = cs.shape
    sr = SUM_ROWS[tag]
    nt = half // sr

    def body(c_ref, b_ref, r_ref, buf, send_sems, recv_sem, local_sems):
        x, y, c, _ = _mesh_pos()
        t = pl.program_id(0)
        slot = lax.rem(t, 2)

        def copies(k, tile):
            rows = pl.ds(pl.multiple_of(c * half + tile * sr, 8), sr)
            keep = pltpu.make_async_copy(buf.at[k], r_ref.at[rows], local_sems.at[k])
            give = pltpu.make_async_remote_copy(src_ref=buf.at[k], dst_ref=r_ref.at[rows], send_sem=send_sems.at[k],
                                                recv_sem=recv_sem, device_id=(x, y, 1 - c), device_id_type=MESH)
            return keep, give

        @pl.when(t >= 2)
        def _():
            keep, give = copies(slot, t - 2)
            keep.wait()
            give.wait_send()

        buf[slot] = (c_ref[0].astype(F32) + b_ref[0].astype(F32)) + (b_ref[1].astype(F32) + b_ref[2].astype(F32))
        keep, give = copies(slot, t)
        keep.start()
        give.start()

        @pl.when(t == nt - 1)
        def _():
            for back in (1, 0):
                keep, give = copies(lax.rem(t - back, 2), t - back)
                keep.wait()
                give.wait_send()
            landed = r_ref.at[_half(2 * half, 1 - c)]
            pltpu.make_async_remote_copy(src_ref=landed, dst_ref=landed, send_sem=send_sems.at[0], recv_sem=recv_sem,
                                         device_id=(x, y, c), device_id_type=MESH).wait_recv()

    return pl.pallas_call(
        body, grid=(nt,),
        in_specs=[pl.BlockSpec((1, sr, width), lambda t: (2 * lax.axis_index("x") + lax.axis_index("y"), t, 0)),
                  pl.BlockSpec((3, sr, width), lambda t: (0, t, 0))],
        out_specs=HBM, out_shape=jax.ShapeDtypeStruct((2 * half, width), F32),
        scratch_shapes=[pltpu.VMEM((2, sr, width), F32), pltpu.SemaphoreType.DMA((2,)), pltpu.SemaphoreType.DMA,
                        pltpu.SemaphoreType.DMA((2,))],
        name="chip_sum_" + tag, compiler_params=_params(1))(cs, b)


def _reduce_scatter(p_a, p_b):
    cs_a = _pair_sum("a", p_a, _pair_send("a", p_a))
    cs_b = _pair_sum("b", p_b, _pair_send("b", p_b))
    from_a, from_b = _chip_exchange(cs_a, cs_b)
    return _chip_sum("a", cs_a, from_a), _chip_sum("b", cs_b, from_b)


SMALL_ROWS = 24


def _all_reduce_small(v):
    def body(v_ref, o_ref, all_ref, send_sems, recv_sems):
        x, y, c, _ = _mesh_pos()
        me = 4 * x + 2 * y + c
        all_ref[me] = v_ref[...]
        peers = [(x ^ ((k >> 2) & 1), y ^ ((k >> 1) & 1), c ^ (k & 1)) for k in range(1, 8)]
        idx = lambda p: 4 * p[0] + 2 * p[1] + p[2]
        send = [pltpu.make_async_remote_copy(src_ref=v_ref, dst_ref=all_ref.at[me], send_sem=send_sems.at[k],
                                             recv_sem=recv_sems.at[k], device_id=p, device_id_type=MESH)
                for k, p in enumerate(peers)]
        for cp in send:
            cp.start()
        for k, p in enumerate(peers):
            pltpu.make_async_remote_copy(src_ref=v_ref, dst_ref=all_ref.at[idx(p)], send_sem=send_sems.at[k],
                                         recv_sem=recv_sems.at[k], device_id=p, device_id_type=MESH).wait_recv()
        for cp in send:
            cp.wait_send()
        acc = all_ref[0]
        for d in range(1, 8):
            acc = acc + all_ref[d]
        o_ref[...] = acc

    vm = pl.BlockSpec(memory_space=pltpu.VMEM)
    return pl.pallas_call(
        body, name="all_reduce_small", in_specs=[vm], out_specs=vm,
        out_shape=jax.ShapeDtypeStruct((SMALL_ROWS, 1024), F32),
        scratch_shapes=[pltpu.VMEM((8, SMALL_ROWS, 1024), F32), pltpu.SemaphoreType.DMA((7,)),
                        pltpu.SemaphoreType.DMA((7,))],)(v)


def _adamw(name, w, g, m, v):
    rows, cols = w.shape
    tr = rows
    if rows * cols > 400_000:
        tr = max(t for t in range(8, rows, 8) if rows % t == 0 and t * cols <= 400_000)

    def body(w_ref, g_ref, m_ref, v_ref, d_ref, nm_ref, nv_ref):
        gv = g_ref[...]
        m_new = ADAM_B1 * m_ref[...] + (1.0 - ADAM_B1) * gv
        v_new = ADAM_B2 * v_ref[...] + (1.0 - ADAM_B2) * (gv * gv)
        m_hat = m_new / (1.0 - ADAM_B1 ** ADAM_STEP)
        v_hat = v_new / (1.0 - ADAM_B2 ** ADAM_STEP)
        d_ref[...] = -ADAM_LR * (m_hat / (jnp.sqrt(v_hat) + ADAM_EPS) + ADAM_WD * w_ref[...])
        nm_ref[...] = m_new
        nv_ref[...] = v_new

    spec = pl.BlockSpec((tr, cols), lambda i: (i, 0))
    sds = jax.ShapeDtypeStruct((rows, cols), F32)
    return pl.pallas_call(body, grid=(rows // tr,), in_specs=[spec] * 4, out_specs=[spec] * 3, out_shape=[sds] * 3,
                          name=name, compiler_params=_params(1))(w, g, m, v)


WEIGHTS = ("ffn1_norm", "ffn1_w_gu", "ffn1_w_down", "mix_norm", "ffn2_norm", "ffn2_w_gu", "ffn2_w_down",
           "a_w_in", "a_w_conv", "a_A_log", "a_dt_bias", "a_out_norm", "a_w_out",
           "b_w_in", "b_b_in", "b_sinks", "b_w_out", "b_b_out", "final_norm")
SMALL_SLOTS = {"ffn1_norm": (0, 2048), "mix_norm": (2048, 2048), "ffn2_norm": (4096, 2048), "final_norm": (6144, 1024),
               "a_A_log": (7168, 8), "a_dt_bias": (7296, 8), "a_out_norm": (7424, 128), "b_sinks": (7552, 16),
               "loss": (7680, 1)}
SMALL_SHARDED = {"a_w_conv": (8192, 8192, (4,), 3072), "b_b_in": (20480, 11264, (), 1536), "b_b_out": (22016, 11648, (), 1024)}
DEV_SMALL_ROWS = 12


def _pack_rows(parts):
    rows = []
    for p in parts:
        r = p.reshape(p.shape[0], -1, 1024)
        rows.append(jnp.pad(r, ((0, 0), (0, -r.shape[1] % PACK_TILE), (0, 0))))
    rows.append(jnp.zeros((parts[0].shape[0], PACK_ROWS - PACK_USED, 1024), parts[0].dtype))
    return jnp.concatenate(rows, axis=1)


def _place(vec, off, a):
    return lax.dynamic_update_slice(vec, a.reshape(-1).astype(F32), (off,))


def kernel(x, ffn1_norm, ffn1_w_gu, ffn1_w_down, mix_norm, ffn2_norm, ffn2_w_gu, ffn2_w_down, a_w_in, a_w_conv, a_A_log, a_dt_bias, a_out_norm, a_w_out, b_w_in, b_b_in, b_sinks, b_w_out, b_b_out, final_norm, loss_target, m_ffn1_norm, m_ffn1_w_gu, m_ffn1_w_down, m_mix_norm, m_ffn2_norm, m_ffn2_w_gu, m_ffn2_w_down, m_a_w_in, m_a_w_conv, m_a_A_log, m_a_dt_bias, m_a_out_norm, m_a_w_out, m_b_w_in, m_b_b_in, m_b_sinks, m_b_w_out, m_b_b_out, m_final_norm, v_ffn1_norm, v_ffn1_w_gu, v_ffn1_w_down, v_mix_norm, v_ffn2_norm, v_ffn2_w_gu, v_ffn2_w_down, v_a_w_in, v_a_w_conv, v_a_A_log, v_a_dt_bias, v_a_out_norm, v_a_w_out, v_b_w_in, v_b_b_in, v_b_sinks, v_b_w_out, v_b_b_out, v_final_norm):
    w = dict(zip(WEIGHTS, (ffn1_norm, ffn1_w_gu, ffn1_w_down, mix_norm, ffn2_norm, ffn2_w_gu, ffn2_w_down, a_w_in, a_w_conv,
                           a_A_log, a_dt_bias, a_out_norm, a_w_out, b_w_in, b_b_in, b_sinks, b_w_out, b_b_out, final_norm)))
    m = dict(zip(WEIGHTS, (m_ffn1_norm, m_ffn1_w_gu, m_ffn1_w_down, m_mix_norm, m_ffn2_norm, m_ffn2_w_gu, m_ffn2_w_down,
                           m_a_w_in, m_a_w_conv, m_a_A_log, m_a_dt_bias, m_a_out_norm, m_a_w_out, m_b_w_in, m_b_b_in,
                           m_b_sinks, m_b_w_out, m_b_b_out, m_final_norm)))
    v = dict(zip(WEIGHTS, (v_ffn1_norm, v_ffn1_w_gu, v_ffn1_w_down, v_mix_norm, v_ffn2_norm, v_ffn2_w_gu, v_ffn2_w_down,
                           v_a_w_in, v_a_w_conv, v_a_A_log, v_a_dt_bias, v_a_out_norm, v_a_w_out, v_b_w_in, v_b_b_in,
                           v_b_sinks, v_b_w_out, v_b_b_out, v_final_norm)))
    chip = 2 * lax.axis_index("x") + lax.axis_index("y")

    big_a = jnp.concatenate([w["ffn1_w_gu"], w["ffn2_w_gu"]], axis=0).astype(BF16).reshape(A_ROWS, FF_BLK)
    big_b = _pack_rows([w[n].astype(BF16).reshape(1, -1) for n, _ in PACK])[0]
    small = jnp.zeros((4096,), F32)
    small = _place(small, 0, w["a_w_conv"])
    small = _place(small, 3072, w["b_b_in"])
    small = _place(small, 3456, w["b_b_out"]).reshape(SMALL_SHARD)
    ra, rb, rs = _gather_chips(big_a, big_b, small)
    ga = _gather_fill("a", big_a, ra).reshape(4, 4, D_MODEL, FF_BLK)
    gb = _gather_fill("b", big_b, rb)
    offs, o = {}, 0
    for n, r in PACK:
        offs[n] = (o, r)
        o += -(-r // PACK_TILE) * PACK_TILE
    blk = lambda n: gb[:, offs[n][0]:offs[n][0] + offs[n][1]]
    gsf = lax.dynamic_update_slice(jnp.zeros((4, 4096), F32), small.reshape(1, 4096), (chip, 0))
    for j, other in enumerate((chip ^ 2, chip ^ 1, chip ^ 3)):
        gsf = lax.dynamic_update_slice(gsf, rs[j].reshape(1, 4096), (other, 0))
    W = {n: w[n] for n in ("ffn1_norm", "ffn2_norm", "mix_norm", "a_out_norm")}
    W["ga"] = ga
    W["w_down"] = gb[:, 0:2816].reshape(4, 4, 704, 1024).transpose(1, 0, 2, 3).reshape(4, 2, FF_BLK, 1024)
    W["a_w_in"] = jnp.pad(blk("a_w_in").reshape(4, 1024, 1028).transpose(1, 0, 2).reshape(1024, A_IN_COLS),
                          ((0, 0), (0, A_IN_PAD - A_IN_COLS)))
    W["a_w_out"] = blk("a_w_out").reshape(1024, 1024)
    W["b_w_in"] = blk("b_w_in").reshape(4, 1024, 384).transpose(1, 0, 2).reshape(1024, 1536)
    W["b_w_out"] = blk("b_w_out").reshape(1024, 1024)
    W["a_w_conv"] = gsf[:, 0:3072].reshape(4, 4, 768).transpose(1, 0, 2).reshape(4, 3072)
    W["b_b_in"] = gsf[:, 3072:3456].reshape(1, 1536)
    W["b_b_out"] = gsf[:, 3456:3712].reshape(1, 1024)
    W["a_gate_p"] = jnp.pad(jnp.concatenate([-jnp.exp(w["a_A_log"]), w["a_dt_bias"]], axis=0), ((0, 0), (8, 112)))
    W["b_sinks"] = jnp.broadcast_to(w["b_sinks"][0][:, None, None], (B_HEADS, 1, 128))
    W["final_norm"] = w["final_norm"][None]

    loss_p, dx, g = _local_step(x[0], loss_target[0], W)

    down = g["w_down"].reshape(4, 4, 704, 1024).transpose(1, 0, 2, 3)
    parts = [down[:, 0:2], down[:, 2:4],
             g["a_w_in"][:, :A_IN_COLS].reshape(1024, 4, 1028).transpose(1, 0, 2), g["a_w_out"].reshape(4, 256, 1024),
             g["b_w_in"].reshape(1024, 4, 384).transpose(1, 0, 2), g["b_w_out"].reshape(4, 256, 1024)]
    red_a, red_b = _reduce_scatter(g["ga"].reshape(4, A_ROWS, FF_BLK),
                                   _pack_rows([a.astype(BF16).reshape(4, -1) for a in parts]))
    grads = {n: red_b[offs[n][0]:offs[n][0] + offs[n][1]].reshape(w[n].shape) for n, _ in PACK}
    grads["ffn1_w_gu"] = red_a[:A_ROWS // 2].reshape(w["ffn1_w_gu"].shape)
    grads["ffn2_w_gu"] = red_a[A_ROWS // 2:].reshape(w["ffn2_w_gu"].shape)

    sv = jnp.zeros((SMALL_ROWS * 1024,), F32)
    small_g = {"ffn1_norm": g["ffn1_norm"], "mix_norm": g["mix_norm"], "ffn2_norm": g["ffn2_norm"], "final_norm": g["final_norm"],
               "a_A_log": g["a_gate_p"][0, 8:16], "a_dt_bias": g["a_gate_p"][1, 8:16], "a_out_norm": g["a_out_norm"],
               "b_sinks": g["b_sinks"], "loss": loss_p[0, 0:1]}
    for n, (off, _) in SMALL_SLOTS.items():
        sv = _place(sv, off, small_g[n])
    for n, (off, _, _, _) in SMALL_SHARDED.items():
        sv = _place(sv, off, g[n])
    tot = _all_reduce_small(sv.reshape(SMALL_ROWS, 1024)).reshape(-1)
    for n, (off, size) in SMALL_SLOTS.items():
        if n != "loss":
            grads[n] = tot[off:off + size].reshape(w[n].shape)
    for n, (off, _, lead, last) in SMALL_SHARDED.items():
        full = tot[off:off + (lead[0] if lead else 1) * last].reshape(lead + (last,))
        width = last // 4
        grads[n] = lax.dynamic_slice_in_dim(full, chip * width, width, axis=-1).reshape(w[n].shape)
    loss = tot[SMALL_SLOTS["loss"][0]]

    delta, new_m, new_v = {}, {}, {}
    for n in ("ffn1_w_gu", "ffn2_w_gu") + tuple(n for n, _ in PACK):
        two_d = lambda a: a.reshape(-1, a.shape[-1])
        d, nm, nv = _adamw("adamw_" + n, two_d(w[n]), two_d(grads[n]), two_d(m[n]), two_d(v[n]))
        delta[n], new_m[n], new_v[n] = d.reshape(w[n].shape), nm.reshape(w[n].shape), nv.reshape(w[n].shape)

    def dev_small(src):
        vec = jnp.zeros((DEV_SMALL_ROWS * 1024,), F32)
        for n, (off, _) in SMALL_SLOTS.items():
            if n != "loss":
                vec = _place(vec, off, src[n])
        for n, (_, off, _, _) in SMALL_SHARDED.items():
            vec = _place(vec, off, src[n])
        return vec.reshape(DEV_SMALL_ROWS, 1024)

    sd, sm, svv = _adamw("adamw_small", dev_small(w), dev_small(grads), dev_small(m), dev_small(v))
    for n in WEIGHTS:
        if n in SMALL_SLOTS:
            off, size = SMALL_SLOTS[n]
        elif n in SMALL_SHARDED:
            off, size = SMALL_SHARDED[n][1], w[n].size
        else:
            continue
        for dst, src in ((delta, sd), (new_m, sm), (new_v, svv)):
            dst[n] = src.reshape(-1)[off:off + size].reshape(w[n].shape)

    return (loss, dx[None], *[grads[n] for n in WEIGHTS], *[delta[n] for n in WEIGHTS],
            *[new_m[n] for n in WEIGHTS], *[new_v[n] for n in WEIGHTS])
```

```python
import jax
import jax.numpy as jnp
from jax import lax
from jax.experimental import pallas as pl
from jax.experimental.pallas import tpu as pltpu

F32 = jnp.float32
BF16 = jnp.bfloat16

D_MODEL = 1024
EPS = 1e-6
FF_BLK = 1408
A_HEADS = 8
A_DK = 128
A_CHUNK = 64
A_HG = 8
A_IN_COLS = 4112
A_IN_PAD = 4224
B_HEADS = 16
B_KV = 4
B_HD = 64
B_BLK = 128
ADAM_LR, ADAM_B1, ADAM_B2, ADAM_EPS, ADAM_WD, ADAM_STEP = 0.001, 0.9, 0.999, 1e-08, 0.01, 10
MESH = pl.DeviceIdType.MESH
VMEM_LIMIT = 56 * 1024 * 1024
HBM = pl.BlockSpec(memory_space=pl.ANY)


def _params(n_axes):
    return pltpu.CompilerParams(dimension_semantics=("arbitrary",) * n_axes, vmem_limit_bytes=VMEM_LIMIT)


def _sigmoid(x):
    return 1.0 / (1.0 + jnp.exp(-x))


def _dot(a, b, ca, cb):
    return lax.dot_general(a, b, (((ca,), (cb,)), ((), ())), preferred_element_type=F32)


def _dotb(a, b, ca=1, cb=0):
    return _dot(a.astype(BF16), b.astype(BF16), ca, cb)


def _dotx(a, b, ca=1, cb=0):
    return lax.dot_general(a, b, (((ca,), (cb,)), ((), ())), preferred_element_type=F32,
                           precision=lax.Precision.HIGHEST)


def _doth(a, b, ca=1, cb=0):
    return lax.dot_general(a, b, (((ca,), (cb,)), ((), ())), preferred_element_type=F32,
                           precision=lax.Precision.HIGH)


def _rms_fwd(name, x, w):
    T, D = x.shape
    tt = min(T, 512)

    def body(x_ref, w_ref, h_ref):
        xv = x_ref[...]
        r = lax.rsqrt(jnp.mean(xv * xv, axis=-1, keepdims=True) + EPS)
        h_ref[...] = (xv * r * w_ref[...]).astype(BF16)

    return pl.pallas_call(
        body, grid=(T // tt,),
        in_specs=[pl.BlockSpec((tt, D), lambda i: (i, 0)), pl.BlockSpec((1, D), lambda i: (0, 0))],
        out_specs=pl.BlockSpec((tt, D), lambda i: (i, 0)),
        out_shape=jax.ShapeDtypeStruct((T, D), BF16), name=name, compiler_params=_params(1))(x, w)


def _rms_bwd_tile(dh, xv, dy, w):
    r = lax.rsqrt(jnp.mean(xv * xv, axis=-1, keepdims=True) + EPS)
    xhat = xv * r
    dxhat = dh * w
    dx = dy + r * (dxhat - xhat * jnp.mean(dxhat * xhat, axis=-1, keepdims=True))
    return dx, jnp.sum(dh * xhat, axis=0, keepdims=True)


def _colsum(name, a):
    T, N = a.shape
    tt = min(T, 512)

    def body(a_ref, o_ref):
        @pl.when(pl.program_id(0) == 0)
        def _():
            o_ref[...] = jnp.zeros_like(o_ref)
        o_ref[...] += jnp.sum(a_ref[...].astype(F32), axis=0, keepdims=True)

    return pl.pallas_call(
        body, grid=(T // tt,), in_specs=[pl.BlockSpec((tt, N), lambda i: (i, 0))],
        out_specs=pl.BlockSpec((1, N), lambda i: (0, 0)),
        out_shape=jax.ShapeDtypeStruct((1, N), F32), name=name, compiler_params=_params(1))(a)


def _matmul(name, a, b, ca, cb, tm, tn, tk, extra_in, outs, epi, order="ji"):
    M, K, N = a.shape[1 - ca], a.shape[ca], b.shape[1 - cb]
    tm, tn, tk = min(tm, M), min(tn, N), min(tk, K)
    assert M % tm == 0 and N % tn == 0 and K % tk == 0, (name, M, N, K, tm, tn, tk)
    ni, nj, nk = M // tm, N // tn, K // tk
    if order == "ji":
        grid = (nj, ni, nk)
        perm = lambda g0, g1, g2: (g1, g0, g2)
    else:
        grid = (ni, nj, nk)
        perm = lambda g0, g1, g2: (g0, g1, g2)

    def wrap(f):
        return lambda g0, g1, g2: f(*perm(g0, g1, g2))

    a_spec = (pl.BlockSpec((tm, tk), wrap(lambda i, j, k: (i, k))) if ca == 1
              else pl.BlockSpec((tk, tm), wrap(lambda i, j, k: (k, i))))
    b_spec = (pl.BlockSpec((tk, tn), wrap(lambda i, j, k: (k, j))) if cb == 0
              else pl.BlockSpec((tn, tk), wrap(lambda i, j, k: (j, k))))
    ne, no = len(extra_in), len(outs)

    def body(*refs):
        a_ref, b_ref = refs[0], refs[1]
        ex, out = refs[2:2 + ne], refs[2 + ne:2 + ne + no]
        i, j, k = perm(pl.program_id(0), pl.program_id(1), pl.program_id(2))
        p = _dotb(a_ref[...], b_ref[...], ca, cb)
        if nk == 1:
            epi(p, ex, out, i, j)
        else:
            acc_ref = refs[-1]

            @pl.when(k == 0)
            def _():
                acc_ref[...] = p

            @pl.when(k > 0)
            def _():
                acc_ref[...] += p

            @pl.when(k == nk - 1)
            def _():
                epi(acc_ref[...], ex, out, i, j)

    return pl.pallas_call(
        body, grid=grid,
        in_specs=[a_spec, b_spec] + [pl.BlockSpec(bs, wrap(f)) for _, bs, f in extra_in],
        out_specs=[pl.BlockSpec(bs, wrap(f)) for _, bs, f in outs],
        out_shape=[s for s, _, _ in outs],
        scratch_shapes=[pltpu.VMEM((tm, tn), F32)] if nk > 1 else [],
        name=name, compiler_params=_params(3))(a, b, *[x for x, _, _ in extra_in])


def _mm_plain(name, a, b, ca, cb, out_dtype, tm=1024, tn=1024, tk=1024, scale=1.0, bias=None):
    M, N = a.shape[1 - ca], b.shape[1 - cb]
    tm, tn = min(tm, M), min(tn, N)
    extra = [] if bias is None else [(bias, (1, tn), lambda i, j, k: (0, j))]

    def epi(acc, ex, out, i, j):
        r = acc * scale if scale != 1.0 else acc
        if bias is not None:
            r = r + ex[0][...]
        out[0][...] = r.astype(out_dtype)

    return _matmul(name, a, b, ca, cb, tm, tn, tk, extra,
                   [(jax.ShapeDtypeStruct((M, N), out_dtype), (tm, tn), lambda i, j, k: (i, j))], epi)[0]


def _mm_residual(name, a, b, x, scale, bias=None, tk=1024):
    M, N = x.shape
    tm, tn = min(512, M), N
    extra = [(x, (tm, tn), lambda i, j, k: (i, j))]
    if bias is not None:
        extra.append((bias, (1, tn), lambda i, j, k: (0, j)))

    def epi(acc, ex, out, i, j):
        r = acc if bias is None else acc + ex[1][...]
        out[0][...] = ex[0][...] + scale * r

    return _matmul(name, a, b, 1, 0, tm, tn, tk, extra,
                   [(jax.ShapeDtypeStruct((M, N), F32), (tm, tn), lambda i, j, k: (i, j))], epi, order="ij")[0]


def _mm_rms_bwd(name, dproj, w_in, x, dy, nw, tk, out_scale):
    M, N = x.shape
    tm = min(512, M)
    extra = [(x, (tm, N), lambda i, j, k: (i, 0)), (dy, (tm, N), lambda i, j, k: (i, 0)),
             (nw, (1, N), lambda i, j, k: (0, 0))]

    def epi(acc, ex, out, i, j):
        dx, dw = _rms_bwd_tile(acc, ex[0][...], ex[1][...], ex[2][...])
        out[0][...] = dx
        out[1][...] = (dx * out_scale).astype(BF16)

        @pl.when(i == 0)
        def _():
            out[2][...] = dw

        @pl.when(i > 0)
        def _():
            out[2][...] += dw

    return _matmul(name, dproj, w_in, 1, 1, tm, N, tk, extra,
                   [(jax.ShapeDtypeStruct((M, N), F32), (tm, N), lambda i, j, k: (i, 0)),
                    (jax.ShapeDtypeStruct((M, N), BF16), (tm, N), lambda i, j, k: (i, 0)),
                    (jax.ShapeDtypeStruct((1, N), F32), (1, N), lambda i, j, k: (0, 0))], epi, order="ij")


def _ffn_gu(name, x, nw, ga, blk):
    T, D = x.shape
    tm = min(T, 512)
    rs = min(tm, 256)

    def body(x_ref, nw_ref, wg0, wg1, wu0, wu1, h_ref, gu_ref, act_ref):
        for r in range(tm // rs):
            rows = pl.ds(r * rs, rs)
            xv = x_ref[rows, :]
            hv = (xv * lax.rsqrt(jnp.mean(xv * xv, axis=-1, keepdims=True) + EPS) * nw_ref[...]).astype(BF16)
            h_ref[rows, :] = hv
            for j, (wg_ref, wu_ref) in enumerate(((wg0, wu0), (wg1, wu1))):
                g = _dot(hv, wg_ref[0, 0], 1, 0)
                u = _dot(hv, wu_ref[0, 0], 1, 0)
                s = _sigmoid(g)
                gs = g * s
                gu_ref[0, j, rows, :] = (u * (s + gs * (1.0 - s))).astype(BF16)
                gu_ref[1, j, rows, :] = gs.astype(BF16)
                act_ref[j, rows, :] = (gs * u).astype(BF16)

    wspec = lambda q: pl.BlockSpec((1, 1, D, FF_BLK), lambda i: (q, blk, 0, 0), pipeline_mode=pl.Buffered(1))
    return pl.pallas_call(
        body, grid=(T // tm,),
        in_specs=[pl.BlockSpec((tm, D), lambda i: (i, 0)), pl.BlockSpec((1, D), lambda i: (0, 0)),
                  wspec(0), wspec(1), wspec(2), wspec(3)],
        out_specs=[pl.BlockSpec((tm, D), lambda i: (i, 0)),
                   pl.BlockSpec((2, 2, tm, FF_BLK), lambda i: (0, 0, i, 0)),
                   pl.BlockSpec((2, tm, FF_BLK), lambda i: (0, i, 0))],
        out_shape=[jax.ShapeDtypeStruct((T, D), BF16), jax.ShapeDtypeStruct((2, 2, T, FF_BLK), BF16),
                   jax.ShapeDtypeStruct((2, T, FF_BLK), BF16)],
        name=name, compiler_params=_params(1))(x, nw, ga, ga, ga, ga)


def _ffn_down(name, act, wd, x):
    T, D = x.shape
    tm = min(T, 512)

    def body(act_ref, wd_ref, x_ref, o_ref):
        acc = _dot(act_ref[0], wd_ref[0], 1, 0) + _dot(act_ref[1], wd_ref[1], 1, 0)
        o_ref[...] = x_ref[...] + 0.5 * acc

    return pl.pallas_call(
        body, grid=(T // tm,),
        in_specs=[pl.BlockSpec((2, tm, FF_BLK), lambda i: (0, i, 0)),
                  pl.BlockSpec((2, FF_BLK, D), lambda i: (0, 0, 0)),
                  pl.BlockSpec((tm, D), lambda i: (i, 0))],
        out_specs=pl.BlockSpec((tm, D), lambda i: (i, 0)),
        out_shape=jax.ShapeDtypeStruct((T, D), F32), name=name, compiler_params=_params(1))(act, wd, x)


def _ffn_dact(name, dyh, wd, gu):
    T, D = dyh.shape
    tm = min(T, 1024)
    rs = min(tm, 256)

    def body(dy_ref, wd_ref, gu_ref, o_ref):
        for r in range(tm // rs):
            rows = pl.ds(r * rs, rs)
            dact = _dot(dy_ref[rows, :], wd_ref[0], 1, 1)
            o_ref[0, 0, rows, :] = (dact * gu_ref[0, 0, rows, :].astype(F32)).astype(BF16)
            o_ref[1, 0, rows, :] = (dact * gu_ref[1, 0, rows, :].astype(F32)).astype(BF16)

    return pl.pallas_call(
        body, grid=(2, T // tm),
        in_specs=[pl.BlockSpec((tm, D), lambda j, i: (i, 0)),
                  pl.BlockSpec((1, FF_BLK, D), lambda j, i: (j, 0, 0)),
                  pl.BlockSpec((2, 1, tm, FF_BLK), lambda j, i: (0, j, i, 0))],
        out_specs=pl.BlockSpec((2, 1, tm, FF_BLK), lambda j, i: (0, j, i, 0)),
        out_shape=jax.ShapeDtypeStruct((2, 2, T, FF_BLK), BF16), name=name, compiler_params=_params(2))(dyh, wd, gu)


def _ffn_dwd(name, act, dyh):
    _, T, _ = act.shape
    D = dyh.shape[1]
    tk = min(T, 2048)
    nk = T // tk

    def body(a_ref, d_ref, o_ref, acc_ref):
        k = pl.program_id(1)
        p = _dot(a_ref[0], d_ref[...], 0, 0)

        @pl.when(k == 0)
        def _():
            acc_ref[...] = p

        @pl.when(k > 0)
        def _():
            acc_ref[...] += p

        @pl.when(k == nk - 1)
        def _():
            o_ref[0] = acc_ref[...].astype(BF16)

    return pl.pallas_call(
        body, grid=(2, nk),
        in_specs=[pl.BlockSpec((1, tk, FF_BLK), lambda j, k: (j, k, 0)), pl.BlockSpec((tk, D), lambda j, k: (k, 0))],
        out_specs=pl.BlockSpec((1, FF_BLK, D), lambda j, k: (j, 0, 0)),
        out_shape=jax.ShapeDtypeStruct((2, FF_BLK, D), BF16), scratch_shapes=[pltpu.VMEM((FF_BLK, D), F32)],
        name=name, compiler_params=_params(2))(act, dyh)


def _ffn_dwgu(name, h, dgu, pa, blk):
    T, D = h.shape
    tk = min(T, 2048)
    nk = T // tk

    def body(h_ref, d_ref, pa_in, o_ref, acc_ref):
        k = pl.program_id(1)
        p = _dot(h_ref[...], d_ref[0, 0], 0, 0)

        @pl.when(k == 0)
        def _():
            acc_ref[...] = p

        @pl.when(k > 0)
        def _():
            acc_ref[...] += p

        @pl.when(k == nk - 1)
        def _():
            o_ref[0, 0] = acc_ref[...].astype(BF16)

    return pl.pallas_call(
        body, grid=(4, nk),
        in_specs=[pl.BlockSpec((tk, D), lambda q, k: (k, 0)),
                  pl.BlockSpec((1, 1, tk, FF_BLK), lambda q, k: (q // 2, q % 2, k, 0)), HBM],
        out_specs=pl.BlockSpec((1, 1, D, FF_BLK), lambda q, k: (q, blk, 0, 0)),
        out_shape=jax.ShapeDtypeStruct(pa.shape, BF16), scratch_shapes=[pltpu.VMEM((D, FF_BLK), F32)],
        input_output_aliases={2: 0}, name=name, compiler_params=_params(2))(h, dgu, pa)


def _ffn_dx(name, dgu, ga, blk, x, dy, nw, out_scale):
    T, D = x.shape
    tm = min(T, 512)

    def body(d_ref, w0, w1, w2, w3, x_ref, dy_ref, nw_ref, dx_ref, dxb_ref, dnw_ref):
        i = pl.program_id(0)
        acc = (_dot(d_ref[0, 0], w0[0, 0], 1, 1) + _dot(d_ref[0, 1], w1[0, 0], 1, 1)
               + _dot(d_ref[1, 0], w2[0, 0], 1, 1) + _dot(d_ref[1, 1], w3[0, 0], 1, 1))
        dx, dw = _rms_bwd_tile(acc, x_ref[...], dy_ref[...], nw_ref[...])
        dx_ref[...] = dx
        dxb_ref[...] = (dx * out_scale).astype(BF16)

        @pl.when(i == 0)
        def _():
            dnw_ref[...] = dw

        @pl.when(i > 0)
        def _():
            dnw_ref[...] += dw

    wspec = lambda q: pl.BlockSpec((1, 1, D, FF_BLK), lambda i: (q, blk, 0, 0), pipeline_mode=pl.Buffered(1))
    row = pl.BlockSpec((tm, D), lambda i: (i, 0))
    return pl.pallas_call(
        body, grid=(T // tm,),
        in_specs=[pl.BlockSpec((2, 2, tm, FF_BLK), lambda i: (0, 0, i, 0)), wspec(0), wspec(1), wspec(2), wspec(3),
                  row, row, pl.BlockSpec((1, D), lambda i: (0, 0))],
        out_specs=[row, row, pl.BlockSpec((1, D), lambda i: (0, 0))],
        out_shape=[jax.ShapeDtypeStruct((T, D), F32), jax.ShapeDtypeStruct((T, D), BF16),
                   jax.ShapeDtypeStruct((1, D), F32)],
        name=name, compiler_params=_params(1))(dgu, ga, ga, ga, ga, x, dy, nw)


def _ffn_fwd(tag, x, nw, ga, blk, wd):
    h, gu, act = _ffn_gu(tag + "_gu", x, nw, ga, blk)
    return _ffn_down(tag + "_down", act, wd, x), (h, gu, act)


def _ffn_bwd(tag, dy, dyh, x, nw, ga, blk, wd, saved, pa, out_scale):
    h, gu, act = saved
    dgu = _ffn_dact(tag + "_dact", dyh, wd, gu)
    d_wd = _ffn_dwd(tag + "_dwd", act, dyh)
    pa = _ffn_dwgu(tag + "_dwgu", h, dgu, pa, blk)
    dx, dxb, d_nw = _ffn_dx(tag + "_dx", dgu, ga, blk, x, dy, nw, out_scale)
    return dx, dxb, d_nw, pa, d_wd


def _conv_taps(cur, halo, w, first, sign):
    tt = cur.shape[0]
    halo = jnp.where(first, 0.0, halo)
    rid = lax.broadcasted_iota(jnp.int32, (8, cur.shape[1]), 0)
    acc = w[3:4, :] * cur
    for s in (1, 2, 3):
        if sign < 0:
            sh = pltpu.roll(cur, s, 0)
            edge = jnp.where(rid < s, pltpu.roll(halo, s, 0), sh[0:8])
            sh = jnp.concatenate([edge, sh[8:]], axis=0) if tt > 8 else edge
        else:
            sh = pltpu.roll(cur, tt - s, 0)
            edge = jnp.where(rid >= 8 - s, pltpu.roll(halo, 8 - s, 0), sh[tt - 8:])
            sh = jnp.concatenate([sh[:tt - 8], edge], axis=0) if tt > 8 else edge
        acc = acc + w[3 - s:4 - s, :] * sh
    return acc


def _gdn_prep(name, proj, wconv, gate_p):
    T = proj.shape[0]
    tt = min(T, 256)
    hb = tt // 8
    nch = tt // A_CHUNK

    def body(cur_ref, halo_ref, ba_ref, w_ref, gp_ref, qkv_ref, bg_ref, gc_ref):
        first = pl.program_id(0) == 0
        for c in range(24):
            cols = pl.ds(c * 128, 128)
            conv = _conv_taps(cur_ref[:, cols], halo_ref[:, cols], w_ref[:, cols], first, -1)
            y = conv * _sigmoid(conv)
            if c < 16:
                y = y * lax.rsqrt(jnp.sum(y * y, axis=-1, keepdims=True) + EPS)
                if c < 8:
                    y = y * (A_DK ** -0.5)
            qkv_ref[:, cols] = y
        ba = ba_ref[...]
        lane = lax.broadcasted_iota(jnp.int32, ba.shape, 1)
        zarg = ba + gp_ref[1:2, :]
        softplus = jnp.maximum(zarg, 0.0) + jnp.log(1.0 + jnp.exp(-jnp.abs(zarg)))
        bg = jnp.where(lane < 8, _sigmoid(ba), jnp.where(lane < 16, gp_ref[0:1, :] * softplus, 0.0))
        bg_ref[...] = bg
        tri = (lax.broadcasted_iota(jnp.int32, (A_CHUNK, A_CHUNK), 0)
               >= lax.broadcasted_iota(jnp.int32, (A_CHUNK, A_CHUNK), 1)).astype(F32)
        for c in range(nch):
            rows = pl.ds(c * A_CHUNK, A_CHUNK)
            gc_ref[rows, :] = _dotx(tri, bg[c * A_CHUNK:(c + 1) * A_CHUNK, :])

    return pl.pallas_call(
        body, grid=(T // tt,),
        in_specs=[pl.BlockSpec((tt, 3072), lambda i: (i, 0)),
                  pl.BlockSpec((8, 3072), lambda i: (jnp.maximum(i * hb - 1, 0), 0)),
                  pl.BlockSpec((tt, 128), lambda i: (i, 32)),
                  pl.BlockSpec((4, 3072), lambda i: (0, 0)),
                  pl.BlockSpec((2, 128), lambda i: (0, 0))],
        out_specs=[pl.BlockSpec((tt, 3072), lambda i: (i, 0)), pl.BlockSpec((tt, 128), lambda i: (i, 0)),
                   pl.BlockSpec((tt, 128), lambda i: (i, 0))],
        out_shape=[jax.ShapeDtypeStruct((T, 3072), F32), jax.ShapeDtypeStruct((T, 128), F32),
                   jax.ShapeDtypeStruct((T, 128), F32)],
        name=name, compiler_params=_params(1))(proj, proj, proj, wconv, gate_p)


def _chunk_masks():
    ri = lax.broadcasted_iota(jnp.int32, (A_CHUNK, A_CHUNK), 0)
    ci = lax.broadcasted_iota(jnp.int32, (A_CHUNK, A_CHUNK), 1)
    return ri >= ci, ri > ci, ri == ci


def _chunk_local(q, k, gcol, grow, bcol):
    incl, strict, _ = _chunk_masks()
    dec = jnp.where(incl, jnp.exp(jnp.where(incl, gcol - grow, 0.0)), 0.0)
    e = jnp.exp(gcol)
    glast = grow[:, A_CHUNK - 1:A_CHUNK]
    f = jnp.exp(glast - gcol)
    gl = jnp.exp(glast)
    kb = k * bcol
    lmat = jnp.where(strict, _dotb(kb, k, 1, 1) * dec, 0.0)
    amat = jnp.where(incl, _dotb(q, k, 1, 1) * dec, 0.0)
    return dec, e, f, gl, kb, lmat, amat


def _unit_lower_inverse(lmats):
    _, _, eye = _chunk_masks()
    ts = [jnp.where(eye, 1.0, 0.0) - lm for lm in lmats]
    lps = [_doth(lm, lm) for lm in lmats]
    for it in range(5):
        ts = [t + _doth(t, lp) for t, lp in zip(ts, lps)]
        if it < 4:
            lps = [_doth(lp, lp) for lp in lps]
    return ts


def _gate_columns(bgt, gct):
    sel = (lax.broadcasted_iota(jnp.int32, (16, 128), 0) == lax.broadcasted_iota(jnp.int32, (16, 128), 1)).astype(F32)
    g_rows = _dotx(sel, gct, 1, 1)
    hs = range(A_HEADS)
    return ([bgt[:, h:h + 1] for h in hs], [gct[:, 8 + h:9 + h] for h in hs], [g_rows[8 + h:9 + h, :] for h in hs])


def _gdn_delta_fwd(name, qkv, bg, gcum):
    T = qkv.shape[0]
    tt = min(T, 512)
    nch = tt // A_CHUNK
    NC = T // A_CHUNK
    wd, ng = 128 * A_HG, A_HEADS // A_HG
    assert ng == 1

    def body(q_ref, k_ref, v_ref, bg_ref, gc_ref, o_ref, s_ref, t_ref, u_ref, w_ref, state):
        @pl.when(pl.program_id(1) == 0)
        def _():
            state[...] = jnp.zeros_like(state)

        def chunk(c, carry):
            rows = pl.ds(pl.multiple_of(c * A_CHUNK, A_CHUNK), A_CHUNK)
            hs = range(A_HG)
            cols = [pl.ds(h * 128, 128) for h in hs]
            q = [q_ref[rows, cols[h]] for h in hs]
            k = [k_ref[rows, cols[h]] for h in hs]
            v = [v_ref[rows, cols[h]] for h in hs]
            bcl, gcl, grw = _gate_columns(bg_ref[rows, :], gc_ref[rows, :])
            loc = [_chunk_local(q[h], k[h], gcl[h], grw[h], bcl[h]) for h in hs]
            e, f, gl, kb, amat = ([l[i] for l in loc] for i in (1, 2, 3, 4, 6))
            tinv = _unit_lower_inverse([l[5] for l in loc])
            u = [_doth(tinv[h], v[h] * bcl[h]) for h in hs]
            w = [_doth(tinv[h], kb[h] * e[h]) for h in hs]
            s = [state[h] for h in hs]
            vn = [u[h] - _dotb(w[h], s[h]) for h in hs]
            o_s = [_dotb(q[h] * e[h], s[h]) for h in hs]
            o_a = [_dotb(amat[h], vn[h]) for h in hs]
            s_new = [s[h] * gl[h] + _dotb(k[h] * f[h], vn[h], 0, 0) for h in hs]
            for h in hs:
                s_ref[h, c] = s[h].astype(BF16)
                t_ref[h, c] = tinv[h]
                u_ref[rows, cols[h]] = u[h]
                w_ref[rows, cols[h]] = w[h]
                o_ref[rows, cols[h]] = o_s[h] + o_a[h]
                state[h] = s_new[h]
            return carry

        lax.fori_loop(0, nch, chunk, 0)

    hd = lambda col: pl.BlockSpec((tt, wd), lambda h, i: (i, col * ng + h))
    gate_spec = pl.BlockSpec((tt, 128), lambda h, i: (i, 0))
    return pl.pallas_call(
        body, grid=(ng, T // tt),
        in_specs=[hd(0), hd(1), hd(2), gate_spec, gate_spec],
        out_specs=[pl.BlockSpec((tt, wd), lambda h, i: (i, h)),
                   pl.BlockSpec((A_HG, nch, 128, 128), lambda h, i: (h, i, 0, 0)),
                   pl.BlockSpec((A_HG, nch, A_CHUNK, A_CHUNK), lambda h, i: (h, i, 0, 0)),
                   pl.BlockSpec((tt, wd), lambda h, i: (i, h)), pl.BlockSpec((tt, wd), lambda h, i: (i, h))],
        out_shape=[jax.ShapeDtypeStruct((T, 1024), F32), jax.ShapeDtypeStruct((A_HEADS, NC, 128, 128), BF16),
                   jax.ShapeDtypeStruct((A_HEADS, NC, A_CHUNK, A_CHUNK), F32),
                   jax.ShapeDtypeStruct((T, 1024), F32), jax.ShapeDtypeStruct((T, 1024), F32)],
        scratch_shapes=[pltpu.VMEM((A_HG, 128, 128), F32)],
        name=name, compiler_params=_params(2))(qkv, qkv, qkv, bg, gcum)


def _gdn_delta_bwd(name, qkv, bg, gcum, d_o, s_sv, t_sv, u_sv, w_sv):
    T = qkv.shape[0]
    tt = min(T, 512)
    nch = tt // A_CHUNK
    ni = T // tt

    def body(q_ref, k_ref, v_ref, bg_ref, gc_ref, do_ref, s_ref, t_ref, u_ref, w_ref,
             dq_ref, dk_ref, dv_ref, dbg_ref, dstate):
        @pl.when(pl.program_id(1) == 0)
        def _():
            dstate[...] = jnp.zeros_like(dstate)

        incl, strict, _ = _chunk_masks()
        upper = (lax.broadcasted_iota(jnp.int32, (A_CHUNK, A_CHUNK), 0)
                 <= lax.broadcasted_iota(jnp.int32, (A_CHUNK, A_CHUNK), 1)).astype(F32)
        last_row = lax.broadcasted_iota(jnp.int32, (A_CHUNK, 1), 0) == A_CHUNK - 1
        ones = jnp.ones((A_CHUNK, 128), F32)

        rsum = lambda x: jnp.sum(x, axis=1, keepdims=True)

        def chunk(cc, carry):
            c = nch - 1 - cc
            rows = pl.ds(pl.multiple_of(c * A_CHUNK, A_CHUNK), A_CHUNK)
            bcl, gcl, grw = _gate_columns(bg_ref[rows, :], gc_ref[rows, :])
            lane = lax.broadcasted_iota(jnp.int32, (A_CHUNK, 128), 1)
            dbg = jnp.zeros((A_CHUNK, 128), F32)
            for first in range(0, A_HG, 4):
                hs = range(first, first + 4)
                cols = {h: pl.ds(h * 128, 128) for h in hs}
                q = {h: q_ref[rows, cols[h]] for h in hs}
                k = {h: k_ref[rows, cols[h]] for h in hs}
                v = {h: v_ref[rows, cols[h]] for h in hs}
                do = {h: do_ref[rows, cols[h]] for h in hs}
                u = {h: u_ref[rows, cols[h]] for h in hs}
                w = {h: w_ref[rows, cols[h]] for h in hs}
                s = {h: s_ref[h, c] for h in hs}
                tinv = {h: t_ref[h, c] for h in hs}
                ds = {h: dstate[h] for h in hs}
                loc = {h: _chunk_local(q[h], k[h], gcl[h], grw[h], bcl[h]) for h in hs}
                dec, e, f, gl, kb, lmat, amat = ({h: loc[h][i] for h in hs} for i in range(7))
                qd = {h: q[h] * e[h] for h in hs}
                kd = {h: k[h] * f[h] for h in hs}
                ke = {h: kb[h] * e[h] for h in hs}
                vn = {h: u[h] - _dotb(w[h], s[h]) for h in hs}
                d_qd = {h: _dotb(do[h], s[h], 1, 1) for h in hs}
                d_a = {h: jnp.where(incl, _dotb(do[h], vn[h], 1, 1), 0.0) for h in hs}
                d_vn1 = {h: _dotb(amat[h], do[h], 0, 0) for h in hs}
                d_vn = {h: d_vn1[h] + _dotb(kd[h], ds[h]) for h in hs}
                d_kd = {h: _dotb(vn[h], ds[h], 1, 1) for h in hs}
                d_w = {h: -_dotb(d_vn[h], s[h], 1, 1) for h in hs}
                ds_q = {h: _dotb(qd[h], do[h], 0, 0) for h in hs}
                ds_w = {h: _dotb(w[h], d_vn[h], 0, 0) for h in hs}
                d_bv = {h: _doth(tinv[h], d_vn[h], 0, 0) for h in hs}
                d_ke = {h: _doth(tinv[h], d_w[h], 0, 0) for h in hs}
                d_l1 = {h: _dotb(d_bv[h], u[h], 1, 1) for h in hs}
                d_l = {h: -jnp.where(strict, d_l1[h] + _dotb(d_ke[h], w[h], 1, 1), 0.0) for h in hs}
                d_kk = {h: d_l[h] * dec[h] for h in hs}
                d_qk = {h: d_a[h] * dec[h] for h in hs}
                d_kb = {h: _dotb(d_kk[h], k[h]) for h in hs}
                dk1 = {h: _dotb(d_kk[h], kb[h], 0, 0) for h in hs}
                dk2 = {h: _dotb(d_qk[h], q[h], 0, 0) for h in hs}
                dq1 = {h: _dotb(d_qk[h], k[h]) for h in hs}
                m = {h: d_l[h] * lmat[h] + d_a[h] * amat[h] for h in hs}
                col_m = {h: _dotx(m[h], ones, 0, 0)[:, 0:1] for h in hs}
                d_gc = {}
                for h in hs:
                    d_gl = jnp.sum(jnp.sum(ds[h] * s[h].astype(F32), axis=1, keepdims=True), axis=0, keepdims=True)
                    r_kd = rsum(d_kd[h] * kd[h])
                    tail = jnp.sum(r_kd, axis=0, keepdims=True) + d_gl * gl[h]
                    d_gc[h] = (rsum(m[h]) - col_m[h] + rsum(d_qd[h] * qd[h]) - r_kd + rsum(d_ke[h] * ke[h])
                               + jnp.where(last_row, tail, 0.0))
                dg = {h: _dotx(upper, d_gc[h] * ones)[:, 0:1] for h in hs}
                for h in hs:
                    dstate[h] = gl[h] * ds[h] + ds_q[h] - ds_w[h]
                    dk_ref[rows, cols[h]] = (dk1[h] + dk2[h] + d_kd[h] * f[h] + d_ke[h] * (bcl[h] * e[h])
                                             + d_kb[h] * bcl[h])
                    dq_ref[rows, cols[h]] = dq1[h] + d_qd[h] * e[h]
                    dv_ref[rows, cols[h]] = d_bv[h] * bcl[h]
                    d_beta = rsum(d_ke[h] * k[h]) * e[h] + rsum(d_kb[h] * k[h]) + rsum(d_bv[h] * v[h])
                    dbg = jnp.where(lane == h, d_beta, jnp.where(lane == 8 + h, dg[h], dbg))
            dbg_ref[rows, :] = dbg
            return carry

        lax.fori_loop(0, nch, chunk, 0)

    wd, ng = 128 * A_HG, A_HEADS // A_HG
    assert ng == 1
    rev = lambda i: ni - 1 - i
    hd = lambda col: pl.BlockSpec((tt, wd), lambda h, i: (rev(i), col * ng + h))
    hd1 = pl.BlockSpec((tt, wd), lambda h, i: (rev(i), h))
    gate_spec = pl.BlockSpec((tt, 128), lambda h, i: (rev(i), 0))
    return pl.pallas_call(
        body, grid=(ng, ni),
        in_specs=[hd(0), hd(1), hd(2), gate_spec, gate_spec, hd1,
                  pl.BlockSpec((A_HG, nch, 128, 128), lambda h, i: (h, rev(i), 0, 0)),
                  pl.BlockSpec((A_HG, nch, A_CHUNK, A_CHUNK), lambda h, i: (h, rev(i), 0, 0)), hd1, hd1],
        out_specs=[hd1, hd1, hd1, gate_spec],
        out_shape=[jax.ShapeDtypeStruct((T, 1024), F32)] * 3 + [jax.ShapeDtypeStruct((T, 128), F32)],
        scratch_shapes=[pltpu.VMEM((A_HG, 128, 128), F32)],
        name=name, compiler_params=_params(2))(qkv, qkv, qkv, bg, gcum, d_o, s_sv, t_sv, u_sv, w_sv)


def _gdn_gate_fwd(name, o, proj, nw):
    T = o.shape[0]
    tt = min(T, 512)

    def body(o_ref, z_ref, nw_ref, y_ref):
        for h in range(A_HEADS):
            cols = pl.ds(h * 128, 128)
            ov, z = o_ref[:, cols], z_ref[:, cols]
            r = lax.rsqrt(jnp.mean(ov * ov, axis=-1, keepdims=True) + EPS)
            y_ref[:, cols] = (ov * r * nw_ref[...] * (z * _sigmoid(z))).astype(BF16)

    return pl.pallas_call(
        body, grid=(T // tt,),
        in_specs=[pl.BlockSpec((tt, 1024), lambda i: (i, 0)), pl.BlockSpec((tt, 1024), lambda i: (i, 3)),
                  pl.BlockSpec((1, 128), lambda i: (0, 0))],
        out_specs=pl.BlockSpec((tt, 1024), lambda i: (i, 0)),
        out_shape=jax.ShapeDtypeStruct((T, 1024), BF16), name=name, compiler_params=_params(1))(o, proj, nw)


def _gdn_gate_bwd(name, dy2, o, proj, nw):
    T = o.shape[0]
    tt = min(T, 512)

    def body(dy_ref, o_ref, z_ref, nw_ref, do_ref, dz_ref, dnw_ref):
        dnw = jnp.zeros((1, 128), F32)
        for h in range(A_HEADS):
            cols = pl.ds(h * 128, 128)
            dy, ov, z = dy_ref[:, cols], o_ref[:, cols], z_ref[:, cols]
            s = _sigmoid(z)
            sz = z * s
            r = lax.rsqrt(jnp.mean(ov * ov, axis=-1, keepdims=True) + EPS)
            xhat = ov * r
            dn = dy * sz
            dz_ref[:, cols] = dy * (xhat * nw_ref[...]) * (s + z * s * (1.0 - s))
            dxhat = dn * nw_ref[...]
            do_ref[:, cols] = r * (dxhat - xhat * jnp.mean(dxhat * xhat, axis=-1, keepdims=True))
            dnw = dnw + jnp.sum(dn * xhat, axis=0, keepdims=True)

        @pl.when(pl.program_id(0) == 0)
        def _():
            dnw_ref[...] = dnw

        @pl.when(pl.program_id(0) > 0)
        def _():
            dnw_ref[...] += dnw

    blk = lambda c: pl.BlockSpec((tt, 1024), lambda i: (i, c))
    return pl.pallas_call(
        body, grid=(T // tt,),
        in_specs=[blk(0), blk(0), blk(3), pl.BlockSpec((1, 128), lambda i: (0, 0))],
        out_specs=[blk(0), blk(0), pl.BlockSpec((1, 128), lambda i: (0, 0))],
        out_shape=[jax.ShapeDtypeStruct((T, 1024), F32), jax.ShapeDtypeStruct((T, 1024), F32),
                   jax.ShapeDtypeStruct((1, 128), F32)],
        name=name, compiler_params=_params(1))(dy2, o, proj, nw)


def _gdn_prep_bwd1(name, proj, wconv, gate_p, dq, dk, dv, dbg):
    T = proj.shape[0]
    tt = min(T, 256)
    hb = tt // 8

    def body(cur_ref, halo_ref, ba_ref, w_ref, gp_ref, dq_ref, dk_ref, dv_ref, dbg_ref,
             dc_ref, dw_ref, dba_ref, dgp_ref):
        first = pl.program_id(0) == 0
        rid = lax.broadcasted_iota(jnp.int32, (8, 128), 0)
        for c in range(24):
            cols = pl.ds(c * 128, 128)
            cur = cur_ref[:, cols]
            halo = jnp.where(first, 0.0, halo_ref[:, cols])
            conv = _conv_taps(cur, halo_ref[:, cols], w_ref[:, cols], first, -1)
            s = _sigmoid(conv)
            y = conv * s
            if c < 16:
                dref = dq_ref if c < 8 else dk_ref
                dn = dref[:, pl.ds((c % 8) * 128, 128)]
                rinv = lax.rsqrt(jnp.sum(y * y, axis=-1, keepdims=True) + EPS)
                yhat = y * rinv
                dyv = rinv * (dn - yhat * jnp.sum(dn * yhat, axis=-1, keepdims=True))
                if c < 8:
                    dyv = dyv * (A_DK ** -0.5)
            else:
                dyv = dv_ref[:, pl.ds((c - 16) * 128, 128)]
            dc = dyv * (s + conv * s * (1.0 - s))
            dc_ref[:, cols] = dc
            parts = [jnp.sum(dc * cur, axis=0, keepdims=True)]
            for sft in (1, 2, 3):
                sh = pltpu.roll(cur, sft, 0)
                edge = jnp.where(rid < sft, pltpu.roll(halo, sft, 0), sh[0:8])
                sh = jnp.concatenate([edge, sh[8:]], axis=0) if tt > 8 else edge
                parts.append(jnp.sum(dc * sh, axis=0, keepdims=True))
            dwc = jnp.concatenate(parts[::-1], axis=0)

            @pl.when(first)
            def _():
                dw_ref[:, cols] = dwc

            @pl.when(jnp.logical_not(first))
            def _():
                dw_ref[:, cols] += dwc

        ba = ba_ref[...]
        dbg = dbg_ref[...]
        lane = lax.broadcasted_iota(jnp.int32, ba.shape, 1)
        sb = _sigmoid(ba)
        zarg = ba + gp_ref[1:2, :]
        softplus = jnp.maximum(zarg, 0.0) + jnp.log(1.0 + jnp.exp(-jnp.abs(zarg)))
        d_b = dbg * sb * (1.0 - sb)
        d_a = dbg * gp_ref[0:1, :] * _sigmoid(zarg)
        dba_ref[...] = jnp.where(lane < 8, d_b, jnp.where(lane < 16, d_a, 0.0))
        g = gp_ref[0:1, :] * softplus
        in_a = (lane >= 8) & (lane < 16)
        sums = jnp.concatenate([jnp.sum(jnp.where(in_a, dbg * g, 0.0), axis=0, keepdims=True),
                                jnp.sum(jnp.where(in_a, d_a, 0.0), axis=0, keepdims=True)], axis=0)

        @pl.when(first)
        def _():
            dgp_ref[...] = sums

        @pl.when(jnp.logical_not(first))
        def _():
            dgp_ref[...] += sums

    row = lambda w, c=0: pl.BlockSpec((tt, w), lambda i: (i, c))
    return pl.pallas_call(
        body, grid=(T // tt,),
        in_specs=[row(3072), pl.BlockSpec((8, 3072), lambda i: (jnp.maximum(i * hb - 1, 0), 0)), row(128, 32),
                  pl.BlockSpec((4, 3072), lambda i: (0, 0)), pl.BlockSpec((2, 128), lambda i: (0, 0)),
                  row(1024), row(1024), row(1024), row(128)],
        out_specs=[row(3072), pl.BlockSpec((4, 3072), lambda i: (0, 0)), row(128),
                   pl.BlockSpec((2, 128), lambda i: (0, 0))],
        out_shape=[jax.ShapeDtypeStruct((T, 3072), F32), jax.ShapeDtypeStruct((4, 3072), F32),
                   jax.ShapeDtypeStruct((T, 128), F32), jax.ShapeDtypeStruct((2, 128), F32)],
        name=name, compiler_params=_params(1))(proj, proj, proj, wconv, gate_p, dq, dk, dv, dbg)


def _gdn_prep_bwd2(name, dc, wconv, dz, dba):
    T = dc.shape[0]
    tt = min(T, 256)
    hb = tt // 8
    ni = T // tt

    def body(cur_ref, halo_ref, w_ref, dz_ref, dba_ref, o_ref):
        last = pl.program_id(0) == ni - 1
        for c in range(24):
            cols = pl.ds(c * 128, 128)
            o_ref[:, cols] = _conv_taps(cur_ref[:, cols], halo_ref[:, cols], w_ref[:, cols], last, +1).astype(BF16)
        o_ref[:, pl.ds(3072, 1024)] = dz_ref[...].astype(BF16)
        o_ref[:, pl.ds(4096, 128)] = dba_ref[...].astype(BF16)

    return pl.pallas_call(
        body, grid=(ni,),
        in_specs=[pl.BlockSpec((tt, 3072), lambda i: (i, 0)),
                  pl.BlockSpec((8, 3072), lambda i: (jnp.minimum((i + 1) * hb, T // 8 - 1), 0)),
                  pl.BlockSpec((4, 3072), lambda i: (0, 0)),
                  pl.BlockSpec((tt, 1024), lambda i: (i, 0)), pl.BlockSpec((tt, 128), lambda i: (i, 0))],
        out_specs=pl.BlockSpec((tt, A_IN_PAD), lambda i: (i, 0)),
        out_shape=jax.ShapeDtypeStruct((T, A_IN_PAD), BF16), name=name, compiler_params=_params(1))(
            dc, dc, wconv, dz, dba)


def _gdn_fwd(x, nw, w_in, wconv, gate_p, out_nw, w_out):
    h = _rms_fwd("a_rms", x, nw)
    proj = _mm_plain("a_proj", h, w_in, 1, 0, F32, tn=FF_BLK)
    qkv, bg, gcum = _gdn_prep("a_prep", proj, wconv, gate_p)
    o, s_sv, t_sv, u_sv, w_sv = _gdn_delta_fwd("a_delta", qkv, bg, gcum)
    o2 = _gdn_gate_fwd("a_gate", o, proj, out_nw)
    y = _mm_residual("a_out", o2, w_out, x, 1.0)
    return y, (h, proj, qkv, bg, gcum, o, s_sv, t_sv, u_sv, w_sv, o2)


def _gdn_bwd(dy, dyb, x, nw, w_in, wconv, gate_p, out_nw, w_out, saved, out_scale):
    h, proj, qkv, bg, gcum, o, s_sv, t_sv, u_sv, w_sv, o2 = saved
    d_o2 = _mm_plain("a_dout", dyb, w_out, 1, 1, F32)
    d_wout = _mm_plain("a_dwout", o2, dyb, 0, 0, F32)
    d_o, d_z, d_outnw = _gdn_gate_bwd("a_dgate", d_o2, o, proj, out_nw)
    dq, dk, dv, dbg = _gdn_delta_bwd("a_ddelta", qkv, bg, gcum, d_o, s_sv, t_sv, u_sv, w_sv)
    dc, d_wconv, dba, dgp = _gdn_prep_bwd1("a_dprep1", proj, wconv, gate_p, dq, dk, dv, dbg)
    dproj = _gdn_prep_bwd2("a_dprep2", dc, wconv, d_z, dba)
    d_win = _mm_plain("a_dwin", h, dproj, 0, 0, F32, tn=FF_BLK, tk=2048)
    dx, dxb, d_nw = _mm_rms_bwd("a_dx", dproj, w_in, x, dy, nw, A_IN_PAD, out_scale)
    return dx, dxb, d_nw, d_win, d_wconv, dgp, d_outnw, d_wout


def _swa_masks(n):
    qi = lax.broadcasted_iota(jnp.int32, (B_BLK, B_BLK), 0)
    kj = lax.broadcasted_iota(jnp.int32, (B_BLK, B_BLK), 1)
    return kj > qi + jnp.where(n > 0, 0, B_BLK), kj <= qi


def _swa_fwd(name, q, k, v, sinks):
    T = q.shape[1]
    tq = min(T, 1024)
    nbt = tq // B_BLK
    scale = B_HD ** -0.5
    G = B_HEADS // B_KV

    def body(q_ref, k_ref, v_ref, kh_ref, vh_ref, s_ref, o_ref, l_ref):
        first_blk = pl.program_id(1) * nbt

        def block(n, kp, vp):
            m_prev, m_cur = _swa_masks(first_blk + n)
            cur = pl.ds(pl.multiple_of(n * B_BLK, B_BLK), B_BLK)
            kc, vc = k_ref[0, cur, :], v_ref[0, cur, :]
            gs = range(G)
            rmax = lambda a: jnp.max(a, axis=1, keepdims=True)
            rsum = lambda a: jnp.sum(a, axis=1, keepdims=True)
            sink = [s_ref[g][:, 0:1] for g in gs]
            qb = [q_ref[g, cur, :] for g in gs]
            s_p = [jnp.where(m_prev, _dot(qb[g], kp, 1, 1) * scale, -jnp.inf) for g in gs]
            s_c = [jnp.where(m_cur, _dot(qb[g], kc, 1, 1) * scale, -jnp.inf) for g in gs]
            m = [jnp.maximum(jnp.maximum(rmax(s_p[g]), rmax(s_c[g])), sink[g]) for g in gs]
            p_p = [jnp.exp(s_p[g] - m[g]) for g in gs]
            p_c = [jnp.exp(s_c[g] - m[g]) for g in gs]
            den = [rsum(p_p[g]) + rsum(p_c[g]) + jnp.exp(sink[g] - m[g]) for g in gs]
            a_p = [_dotb(p_p[g], vp) for g in gs]
            a_c = [_dotb(p_c[g], vc) for g in gs]
            for g in gs:
                o_ref[g, cur, :] = ((a_p[g] + a_c[g]) / den[g]).astype(BF16)
                l_ref[g, cur, :] = m[g] + jnp.log(den[g])

        block(0, kh_ref[0], vh_ref[0])

        def rest(n, carry):
            prv = pl.ds(pl.multiple_of((n - 1) * B_BLK, B_BLK), B_BLK)
            block(n, k_ref[0, prv, :], v_ref[0, prv, :])
            return carry

        lax.fori_loop(1, nbt, rest, 0)

    qs = pl.BlockSpec((G, tq, B_HD), lambda kv, i: (kv, i, 0))
    ks = pl.BlockSpec((1, tq, B_HD), lambda kv, i: (kv, i, 0))
    halo = pl.BlockSpec((1, B_BLK, B_HD), lambda kv, i: (kv, jnp.maximum(i * nbt - 1, 0), 0))
    return pl.pallas_call(
        body, grid=(B_KV, T // tq),
        in_specs=[qs, ks, ks, halo, halo, pl.BlockSpec((G, 1, 128), lambda kv, i: (kv, 0, 0))],
        out_specs=[qs, pl.BlockSpec((G, tq, 1), lambda kv, i: (kv, i, 0))],
        out_shape=[jax.ShapeDtypeStruct((B_HEADS, T, B_HD), BF16), jax.ShapeDtypeStruct((B_HEADS, T, 1), F32)],
        name=name, compiler_params=_params(2))(q, k, v, k, v, sinks)


def _swa_bwd(name, q, k, v, sinks, o, lse, do):
    T = q.shape[1]
    tq = min(T, 1024)
    nbt, ni = tq // B_BLK, T // tq
    scale = B_HD ** -0.5
    G = B_HEADS // B_KV

    def body(q_ref, k_ref, v_ref, kh_ref, vh_ref, s_ref, o_ref, l_ref, do_ref, dq_ref, dk_ref, dv_ref, ds_ref,
             dk_halo, dv_halo):
        step = pl.program_id(1)
        first_blk = (ni - 1 - step) * nbt
        last = pl.ds(tq - B_BLK, B_BLK)
        dk_ref[...] = jnp.zeros_like(dk_ref)
        dv_ref[...] = jnp.zeros_like(dv_ref)

        @pl.when(step > 0)
        def _():
            dk_ref[0, last, :] = dk_halo[...]
            dv_ref[0, last, :] = dv_halo[...]

        def block(n, kp, vp, dsinks):
            m_prev, m_cur = _swa_masks(first_blk + n)
            cur = pl.ds(pl.multiple_of(n * B_BLK, B_BLK), B_BLK)
            kc, vc = k_ref[0, cur, :], v_ref[0, cur, :]
            gs = range(G)
            sink = [s_ref[g][:, 0:1] for g in gs]
            qb = [q_ref[g, cur, :] for g in gs]
            dob = [do_ref[g, cur, :] for g in gs]
            lse_b = [l_ref[g, cur, :] for g in gs]
            p_p = [jnp.where(m_prev, jnp.exp(_dot(qb[g], kp, 1, 1) * scale - lse_b[g]), 0.0) for g in gs]
            p_c = [jnp.where(m_cur, jnp.exp(_dot(qb[g], kc, 1, 1) * scale - lse_b[g]), 0.0) for g in gs]
            delta = [jnp.sum(dob[g].astype(F32) * o_ref[g, cur, :].astype(F32), axis=1, keepdims=True) for g in gs]
            ds_p = [p_p[g] * (_dot(dob[g], vp, 1, 1) - delta[g]) for g in gs]
            ds_c = [p_c[g] * (_dot(dob[g], vc, 1, 1) - delta[g]) for g in gs]
            dq_p = [_dotb(ds_p[g], kp) for g in gs]
            dq_c = [_dotb(ds_c[g], kc) for g in gs]
            dk_ps = [_dotb(ds_p[g], qb[g], 0, 0) for g in gs]
            dk_cs = [_dotb(ds_c[g], qb[g], 0, 0) for g in gs]
            dv_ps = [_dotb(p_p[g], dob[g], 0, 0) for g in gs]
            dv_cs = [_dotb(p_c[g], dob[g], 0, 0) for g in gs]
            for g in gs:
                dq_ref[g, cur, :] = (dq_p[g] + dq_c[g]) * scale
            out = tuple(dsinks[g] - jnp.sum(jnp.exp(sink[g] - lse_b[g]) * delta[g], axis=0, keepdims=True) for g in gs)
            total = lambda parts: (parts[0] + parts[1]) + (parts[2] + parts[3])
            dk_ref[0, cur, :] += total(dk_cs) * scale
            dv_ref[0, cur, :] += total(dv_cs)
            return total(dk_ps) * scale, total(dv_ps), out

        zeros = tuple(jnp.zeros((1, 1), F32) for _ in range(G))
        dk_p, dv_p, dsinks = block(0, kh_ref[0], vh_ref[0], zeros)
        dk_halo[...] = dk_p
        dv_halo[...] = dv_p

        def rest(n, dsinks):
            prv = pl.ds(pl.multiple_of((n - 1) * B_BLK, B_BLK), B_BLK)
            dk_p, dv_p, dsinks = block(n, k_ref[0, prv, :], v_ref[0, prv, :], dsinks)
            dk_ref[0, prv, :] += dk_p
            dv_ref[0, prv, :] += dv_p
            return dsinks

        dsinks = lax.fori_loop(1, nbt, rest, dsinks)
        for g in range(G):
            row = jnp.broadcast_to(dsinks[g], (1, 128))

            @pl.when(step == 0)
            def _():
                ds_ref[g] = row

            @pl.when(step > 0)
            def _():
                ds_ref[g] += row

    rev = lambda i: ni - 1 - i
    qs = pl.BlockSpec((G, tq, B_HD), lambda kv, i: (kv, rev(i), 0))
    ks = pl.BlockSpec((1, tq, B_HD), lambda kv, i: (kv, rev(i), 0))
    halo = pl.BlockSpec((1, B_BLK, B_HD), lambda kv, i: (kv, jnp.maximum(rev(i) * nbt - 1, 0), 0))
    ss = pl.BlockSpec((G, 1, 128), lambda kv, i: (kv, 0, 0))
    return pl.pallas_call(
        body, grid=(B_KV, ni),
        in_specs=[qs, ks, ks, halo, halo, ss, qs, pl.BlockSpec((G, tq, 1), lambda kv, i: (kv, rev(i), 0)), qs],
        out_specs=[qs, ks, ks, ss],
        out_shape=[jax.ShapeDtypeStruct((B_HEADS, T, B_HD), F32), jax.ShapeDtypeStruct((B_KV, T, B_HD), F32),
                   jax.ShapeDtypeStruct((B_KV, T, B_HD), F32), jax.ShapeDtypeStruct((B_HEADS, 1, 128), F32)],
        scratch_shapes=[pltpu.VMEM((B_BLK, B_HD), F32), pltpu.VMEM((B_BLK, B_HD), F32)],
        name=name, compiler_params=_params(2))(q, k, v, k, v, sinks, o, lse, do)


def _split_heads(a, n):
    T = a.shape[0]
    return a.reshape(T, n, B_HD).transpose(1, 0, 2)


def _merge_heads(a):
    n, T, _ = a.shape
    return a.transpose(1, 0, 2).reshape(T, n * B_HD)


def _swa_mixer_fwd(x, nw, w_in, b_in, sinks, w_out, b_out):
    h = _rms_fwd("b_rms", x, nw)
    proj = _mm_plain("b_proj", h, w_in, 1, 0, BF16, tn=768, bias=b_in)
    q, k, v = _split_heads(proj[:, :1024], B_HEADS), _split_heads(proj[:, 1024:1280], B_KV), _split_heads(proj[:, 1280:], B_KV)
    o, lse = _swa_fwd("b_attn", q, k, v, sinks)
    om = _merge_heads(o)
    y = _mm_residual("b_out", om, w_out, x, 1.0, bias=b_out)
    return y, (h, q, k, v, o, lse, om)


def _swa_mixer_bwd(dy, dyb, x, nw, w_in, sinks, w_out, saved, out_scale):
    h, q, k, v, o, lse, om = saved
    d_om = _mm_plain("b_dout", dyb, w_out, 1, 1, BF16)
    d_wout = _mm_plain("b_dwout", om, dyb, 0, 0, F32)
    d_bout = _colsum("b_dbout", dy)
    dq, dk, dv, dsinks = _swa_bwd("b_dattn", q, k, v, sinks, o, lse, _split_heads(d_om, B_HEADS))
    dproj = jnp.concatenate([_merge_heads(dq), _merge_heads(dk), _merge_heads(dv)], axis=1)
    d_bin = _colsum("b_dbin", dproj)
    d_win = _mm_plain("b_dwin", h, dproj, 0, 0, F32, tn=768)
    dx, dxb, d_nw = _mm_rms_bwd("b_dx", dproj, w_in, x, dy, nw, 1536, out_scale)
    return dx, dxb, d_nw, d_win, d_bin, dsinks[:, 0, 0], d_wout, d_bout


def _loss_head(name, x, tgt, fw, out_scale):
    T, D = x.shape
    tt = min(T, 512)

    def body(x_ref, t_ref, w_ref, dx_ref, dxb_ref, loss_ref, dw_ref):
        xv = x_ref[...]
        r = lax.rsqrt(jnp.mean(xv * xv, axis=-1, keepdims=True) + EPS)
        xhat = xv * r
        diff = xhat * w_ref[...] - t_ref[...]
        part = 0.5 * jnp.sum(jnp.mean(diff * diff, axis=-1, keepdims=True), axis=0, keepdims=True)
        dyv = diff * (1.0 / D)
        dxhat = dyv * w_ref[...]
        dx = r * (dxhat - xhat * jnp.mean(dxhat * xhat, axis=-1, keepdims=True))
        dx_ref[...] = dx
        dxb_ref[...] = (dx * out_scale).astype(BF16)
        dw = jnp.sum(dyv * xhat, axis=0, keepdims=True)
        lp = jnp.broadcast_to(part, (1, 128))

        @pl.when(pl.program_id(0) == 0)
        def _():
            loss_ref[...] = lp
            dw_ref[...] = dw

        @pl.when(pl.program_id(0) > 0)
        def _():
            loss_ref[...] += lp
            dw_ref[...] += dw

    row = pl.BlockSpec((tt, D), lambda i: (i, 0))
    return pl.pallas_call(
        body, grid=(T // tt,), in_specs=[row, row, pl.BlockSpec((1, D), lambda i: (0, 0))],
        out_specs=[row, row, pl.BlockSpec((1, 128), lambda i: (0, 0)), pl.BlockSpec((1, D), lambda i: (0, 0))],
        out_shape=[jax.ShapeDtypeStruct((T, D), F32), jax.ShapeDtypeStruct((T, D), BF16),
                   jax.ShapeDtypeStruct((1, 128), F32), jax.ShapeDtypeStruct((1, D), F32)],
        name=name, compiler_params=_params(1))(x, tgt, fw)


def _local_step(x, tgt, wts):
    W = wts
    g = {}
    ga, wdn = W["ga"], W["w_down"]
    n1, n2, nm = W["ffn1_norm"], W["ffn2_norm"], W["mix_norm"]
    x1, sv1 = _ffn_fwd("f10", x, n1[0:1], ga, 0, wdn[0])
    x2, sva = _gdn_fwd(x1, nm[0:1], W["a_w_in"], W["a_w_conv"], W["a_gate_p"], W["a_out_norm"], W["a_w_out"])
    x3, sv3 = _ffn_fwd("f20", x2, n2[0:1], ga, 2, wdn[2])
    x4, sv4 = _ffn_fwd("f11", x3, n1[1:2], ga, 1, wdn[1])
    x5, svb = _swa_mixer_fwd(x4, nm[1:2], W["b_w_in"], W["b_b_in"], W["b_sinks"], W["b_w_out"], W["b_b_out"])
    x6, sv6 = _ffn_fwd("f21", x5, n2[1:2], ga, 3, wdn[3])
    dx, dxb, loss_p, g["final_norm"] = _loss_head("loss_head", x6, tgt, W["final_norm"], 0.5)

    pa = jnp.zeros(ga.shape, BF16)
    dx, dxb, n21, pa, wd21 = _ffn_bwd("f21", dx, dxb, x5, n2[1:2], ga, 3, wdn[3], sv6, pa, 1.0)
    dx, dxb, nb, g["b_w_in"], g["b_b_in"], g["b_sinks"], g["b_w_out"], g["b_b_out"] = _swa_mixer_bwd(
        dx, dxb, x4, nm[1:2], W["b_w_in"], W["b_sinks"], W["b_w_out"], svb, 0.5)
    dx, dxb, n11, pa, wd11 = _ffn_bwd("f11", dx, dxb, x3, n1[1:2], ga, 1, wdn[1], sv4, pa, 0.5)
    dx, dxb, n20, pa, wd20 = _ffn_bwd("f20", dx, dxb, x2, n2[0:1], ga, 2, wdn[2], sv3, pa, 1.0)
    dx, dxb, na, g["a_w_in"], g["a_w_conv"], g["a_gate_p"], g["a_out_norm"], g["a_w_out"] = _gdn_bwd(
        dx, dxb, x1, nm[0:1], W["a_w_in"], W["a_w_conv"], W["a_gate_p"], W["a_out_norm"], W["a_w_out"], sva, 0.5)
    dx, dxb, n10, pa, wd10 = _ffn_bwd("f10", dx, dxb, x, n1[0:1], ga, 0, wdn[0], sv1, pa, 1.0)

    g["ffn1_norm"] = jnp.concatenate([n10, n11], axis=0)
    g["ffn2_norm"] = jnp.concatenate([n20, n21], axis=0)
    g["mix_norm"] = jnp.concatenate([na, nb], axis=0)
    g["ga"] = pa
    g["w_down"] = jnp.stack([wd10, wd11, wd20, wd21])
    return loss_p, dx, g


A_ROWS = 4 * D_MODEL
PACK = (("ffn1_w_down", 1408), ("ffn2_w_down", 1408), ("a_w_in", 1028), ("a_w_out", 256), ("b_w_in", 384),
        ("b_w_out", 256))
PACK_TILE = 16
PACK_USED = sum(-(-n // PACK_TILE) * PACK_TILE for _, n in PACK)
PACK_ROWS = 4864
assert PACK_USED <= PACK_ROWS
SMALL_SHARD = (8, 512)
MOVE_ROWS = {"a": 512, "b": 608}
SUM_ROWS = {"a": 256, "b": 304}


def _mesh_pos():
    x, y, c = lax.axis_index("x"), lax.axis_index("y"), lax.axis_index("c")
    return x, y, c, [(1 - x, y), (x, 1 - y), (1 - x, 1 - y)]


def _half(rows, c):
    return pl.ds(pl.multiple_of(c * (rows // 2), 16), rows // 2)


def _gather_chips(big_a, big_b, small):
    bufs = (big_a, big_b, small)

    def body(a_ref, b_ref, small_ref, ra_ref, rb_ref, rs_ref, send_sems, recv_sems):
        x, y, c, chips = _mesh_pos()
        srcs = (a_ref.at[_half(A_ROWS, c)], b_ref.at[_half(PACK_ROWS, c)], small_ref)
        send = []
        for j, chip in enumerate(chips):
            for n, (src, dst) in enumerate(zip(srcs, (ra_ref, rb_ref, rs_ref))):
                send.append(pltpu.make_async_remote_copy(src_ref=src, dst_ref=dst.at[j],
                                                         send_sem=send_sems.at[3 * j + n], recv_sem=recv_sems.at[3 * j + n],
                                                         device_id=(*chip, c), device_id_type=MESH))
        for cp in send:
            cp.start()
        for cp in send:
            cp.wait_recv()
        for cp in send:
            cp.wait_send()

    halves = [jax.ShapeDtypeStruct((3, b.shape[0] // 2, b.shape[1]), BF16) for b in bufs[:2]]
    return pl.pallas_call(
        body, name="gather_chips", in_specs=[HBM, HBM, HBM], out_specs=[HBM, HBM, HBM],
        out_shape=halves + [jax.ShapeDtypeStruct((3,) + SMALL_SHARD, F32)],
        scratch_shapes=[pltpu.SemaphoreType.DMA((9,)), pltpu.SemaphoreType.DMA((9,))])(*bufs)


def _gather_fill(tag, big, recv):
    rows_all, width = big.shape
    half, mv = rows_all // 2, MOVE_ROWS[tag]
    nt = half // mv
    own_tiles = rows_all // mv
    assert half % mv == 0 and own_tiles <= 3 * nt

    def body(recv_ref, big_ref, g_ref, send_sem, recv_sem, local_sems):
        x, y, c, chips = _mesh_pos()
        j, t = pl.program_id(0), pl.program_id(1)
        step = j * nt + t
        src_chip = jnp.where(j == 0, 2 * (1 - x) + y, jnp.where(j == 1, 2 * x + 1 - y, 2 * (1 - x) + 1 - y))
        rows = pl.ds(pl.multiple_of(c * half + t * mv, 16), mv)
        keep = pltpu.make_async_copy(recv_ref.at[0], g_ref.at[src_chip, rows], local_sems.at[0])
        give = pltpu.make_async_remote_copy(src_ref=recv_ref.at[0], dst_ref=g_ref.at[src_chip, rows],
                                            send_sem=send_sem, recv_sem=recv_sem,
                                            device_id=(x, y, 1 - c), device_id_type=MESH)
        keep.start()
        give.start()

        @pl.when(step < own_tiles)
        def _():
            own_rows = pl.ds(pl.multiple_of(step * mv, 16), mv)
            own = pltpu.make_async_copy(big_ref, g_ref.at[2 * x + y, own_rows], local_sems.at[1])
            own.start()
            own.wait()

        give.wait_send()
        keep.wait()

        @pl.when(step == 3 * nt - 1)
        def _():
            landed = g_ref.at[pl.ds(0, 3), pl.ds(0, half)]
            pltpu.make_async_remote_copy(src_ref=landed, dst_ref=landed, send_sem=send_sem, recv_sem=recv_sem,
                                         device_id=(x, y, c), device_id_type=MESH).wait_recv()

    return pl.pallas_call(
        body, grid=(3, nt),
        in_specs=[pl.BlockSpec((1, mv, width), lambda j, t: (j, t, 0)),
                  pl.BlockSpec((mv, width), lambda j, t: (jnp.minimum(j * nt + t, own_tiles - 1), 0))],
        out_specs=HBM, out_shape=jax.ShapeDtypeStruct((4, rows_all, width), BF16),
        scratch_shapes=[pltpu.SemaphoreType.DMA, pltpu.SemaphoreType.DMA, pltpu.SemaphoreType.DMA((2,))],
        name="gather_fill_" + tag, compiler_params=_params(2))(recv, big)


def _pair_send(tag, p):
    _, rows_all, width = p.shape
    half, mv = rows_all // 2, MOVE_ROWS[tag]
    nt = half // mv

    def body(p_ref, a_ref, send_sem, recv_sem):
        x, y, c, _ = _mesh_pos()
        s, t = pl.program_id(0), pl.program_id(1)
        rows = pl.ds(pl.multiple_of(t * mv, 16), mv)
        give = pltpu.make_async_remote_copy(src_ref=p_ref.at[0], dst_ref=a_ref.at[s, rows], send_sem=send_sem,
                                            recv_sem=recv_sem, device_id=(x, y, 1 - c), device_id_type=MESH)
        give.start()
        give.wait_send()

        @pl.when((s == 3) & (t == nt - 1))
        def _():
            pltpu.make_async_remote_copy(src_ref=a_ref, dst_ref=a_ref, send_sem=send_sem, recv_sem=recv_sem,
                                         device_id=(x, y, c), device_id_type=MESH).wait_recv()

    return pl.pallas_call(
        body, grid=(4, nt),
        in_specs=[pl.BlockSpec((1, mv, width), lambda s, t: (s, (1 - lax.axis_index("c")) * nt + t, 0))],
        out_specs=HBM, out_shape=jax.ShapeDtypeStruct((4, half, width), BF16),
        scratch_shapes=[pltpu.SemaphoreType.DMA, pltpu.SemaphoreType.DMA],
        name="pair_send_" + tag, compiler_params=_params(2))(p)


def _pair_sum(tag, p, a):
    _, half, width = a.shape
    sr = SUM_ROWS[tag]
    nt = half // sr
    assert half % sr == 0

    def body(p_ref, a_ref, o_ref):
        o_ref[...] = (p_ref[...].astype(F32) + a_ref[...].astype(F32)).astype(BF16)

    spec = pl.BlockSpec((1, sr, width), lambda s, t: (s, t, 0))
    return pl.pallas_call(
        body, grid=(4, nt),
        in_specs=[pl.BlockSpec((1, sr, width), lambda s, t: (s, lax.axis_index("c") * nt + t, 0)), spec],
        out_specs=spec, out_shape=jax.ShapeDtypeStruct((4, half, width), BF16),
        name="pair_sum_" + tag, compiler_params=_params(2))(p, a)


def _chip_exchange(cs_a, cs_b):
    def body(ca_ref, cb_ref, ba_ref, bb_ref, send_sems, recv_sems):
        x, y, c, chips = _mesh_pos()
        send = []
        for j, chip in enumerate(chips):
            for n, (src, dst) in enumerate(((ca_ref, ba_ref), (cb_ref, bb_ref))):
                send.append(pltpu.make_async_remote_copy(src_ref=src.at[2 * chip[0] + chip[1]], dst_ref=dst.at[j],
                                                         send_sem=send_sems.at[2 * j + n], recv_sem=recv_sems.at[2 * j + n],
                                                         device_id=(*chip, c), device_id_type=MESH))
        for cp in send:
            cp.start()
        for cp in send:
            cp.wait_recv()
        for cp in send:
            cp.wait_send()

    return pl.pallas_call(
        body, name="chip_exchange", in_specs=[HBM, HBM], out_specs=[HBM, HBM],
        out_shape=[jax.ShapeDtypeStruct((3,) + cs.shape[1:], BF16) for cs in (cs_a, cs_b)],
        scratch_shapes=[pltpu.SemaphoreType.DMA((6,)), pltpu.SemaphoreType.DMA((6,))])(cs_a, cs_b)


def _chip_sum(tag, cs, b):
    _, half, width = cs.shape
    sr = SUM_ROWS[tag]
    nt = half // sr

    def body(c_ref, b_ref, r_ref, buf, send_sems, recv_sem, local_sems):
        x, y, c, _ = _mesh_pos()
        t = pl.program_id(0)
        slot = lax.rem(t, 2)

        def copies(k, tile):
            rows = pl.ds(pl.multiple_of(c * half + tile * sr, 8), sr)
            keep = pltpu.make_async_copy(buf.at[k], r_ref.at[rows], local_sems.at[k])
            give = pltpu.make_async_remote_copy(src_ref=buf.at[k], dst_ref=r_ref.at[rows], send_sem=send_sems.at[k],
                                                recv_sem=recv_sem, device_id=(x, y, 1 - c), device_id_type=MESH)
            return keep, give

        @pl.when(t >= 2)
        def _():
            keep, give = copies(slot, t - 2)
            keep.wait()
            give.wait_send()

        buf[slot] = (c_ref[0].astype(F32) + b_ref[0].astype(F32)) + (b_ref[1].astype(F32) + b_ref[2].astype(F32))
        keep, give = copies(slot, t)
        keep.start()
        give.start()

        @pl.when(t == nt - 1)
        def _():
            for back in (1, 0):
                keep, give = copies(lax.rem(t - back, 2), t - back)
                keep.wait()
                give.wait_send()
            landed = r_ref.at[_half(2 * half, 1 - c)]
            pltpu.make_async_remote_copy(src_ref=landed, dst_ref=landed, send_sem=send_sems.at[0], recv_sem=recv_sem,
                                         device_id=(x, y, c), device_id_type=MESH).wait_recv()

    return pl.pallas_call(
        body, grid=(nt,),
        in_specs=[pl.BlockSpec((1, sr, width), lambda t: (2 * lax.axis_index("x") + lax.axis_index("y"), t, 0)),
                  pl.BlockSpec((3, sr, width), lambda t: (0, t, 0))],
        out_specs=HBM, out_shape=jax.ShapeDtypeStruct((2 * half, width), F32),
        scratch_shapes=[pltpu.VMEM((2, sr, width), F32), pltpu.SemaphoreType.DMA((2,)), pltpu.SemaphoreType.DMA,
                        pltpu.SemaphoreType.DMA((2,))],
        name="chip_sum_" + tag, compiler_params=_params(1))(cs, b)


def _reduce_scatter(p_a, p_b):
    cs_a = _pair_sum("a", p_a, _pair_send("a", p_a))
    cs_b = _pair_sum("b", p_b, _pair_send("b", p_b))
    from_a, from_b = _chip_exchange(cs_a, cs_b)
    return _chip_sum("a", cs_a, from_a), _chip_sum("b", cs_b, from_b)


SMALL_ROWS = 24


def _all_reduce_small(v):
    def body(v_ref, o_ref, all_ref, send_sems, recv_sems):
        x, y, c, _ = _mesh_pos()
        me = 4 * x + 2 * y + c
        all_ref[me] = v_ref[...]
        peers = [(x ^ ((k >> 2) & 1), y ^ ((k >> 1) & 1), c ^ (k & 1)) for k in range(1, 8)]
        idx = lambda p: 4 * p[0] + 2 * p[1] + p[2]
        send = [pltpu.make_async_remote_copy(src_ref=v_ref, dst_ref=all_ref.at[me], send_sem=send_sems.at[k],
                                             recv_sem=recv_sems.at[k], device_id=p, device_id_type=MESH)
                for k, p in enumerate(peers)]
        for cp in send:
            cp.start()
        for k, p in enumerate(peers):
            pltpu.make_async_remote_copy(src_ref=v_ref, dst_ref=all_ref.at[idx(p)], send_sem=send_sems.at[k],
                                         recv_sem=recv_sems.at[k], device_id=p, device_id_type=MESH).wait_recv()
        for cp in send:
            cp.wait_send()
        acc = all_ref[0]
        for d in range(1, 8):
            acc = acc + all_ref[d]
        o_ref[...] = acc

    vm = pl.BlockSpec(memory_space=pltpu.VMEM)
    return pl.pallas_call(
        body, name="all_reduce_small", in_specs=[vm], out_specs=vm,
        out_shape=jax.ShapeDtypeStruct((SMALL_ROWS, 1024), F32),
        scratch_shapes=[pltpu.VMEM((8, SMALL_ROWS, 1024), F32), pltpu.SemaphoreType.DMA((7,)),
                        pltpu.SemaphoreType.DMA((7,))],)(v)


def _adamw(name, w, g, m, v):
    rows, cols = w.shape
    tr = rows
    if rows * cols > 400_000:
        tr = max(t for t in range(8, rows, 8) if rows % t == 0 and t * cols <= 400_000)

    def body(w_ref, g_ref, m_ref, v_ref, d_ref, nm_ref, nv_ref):
        gv = g_ref[...]
        m_new = ADAM_B1 * m_ref[...] + (1.0 - ADAM_B1) * gv
        v_new = ADAM_B2 * v_ref[...] + (1.0 - ADAM_B2) * (gv * gv)
        m_hat = m_new / (1.0 - ADAM_B1 ** ADAM_STEP)
        v_hat = v_new / (1.0 - ADAM_B2 ** ADAM_STEP)
        d_ref[...] = -ADAM_LR * (m_hat / (jnp.sqrt(v_hat) + ADAM_EPS) + ADAM_WD * w_ref[...])
        nm_ref[...] = m_new
        nv_ref[...] = v_new

    spec = pl.BlockSpec((tr, cols), lambda i: (i, 0))
    sds = jax.ShapeDtypeStruct((rows, cols), F32)
    return pl.pallas_call(body, grid=(rows // tr,), in_specs=[spec] * 4, out_specs=[spec] * 3, out_shape=[sds] * 3,
                          name=name, compiler_params=_params(1))(w, g, m, v)


WEIGHTS = ("ffn1_norm", "ffn1_w_gu", "ffn1_w_down", "mix_norm", "ffn2_norm", "ffn2_w_gu", "ffn2_w_down",
           "a_w_in", "a_w_conv", "a_A_log", "a_dt_bias", "a_out_norm", "a_w_out",
           "b_w_in", "b_b_in", "b_sinks", "b_w_out", "b_b_out", "final_norm")
SMALL_SLOTS = {"ffn1_norm": (0, 2048), "mix_norm": (2048, 2048), "ffn2_norm": (4096, 2048), "final_norm": (6144, 1024),
               "a_A_log": (7168, 8), "a_dt_bias": (7296, 8), "a_out_norm": (7424, 128), "b_sinks": (7552, 16),
               "loss": (7680, 1)}
SMALL_SHARDED = {"a_w_conv": (8192, 8192, (4,), 3072), "b_b_in": (20480, 11264, (), 1536), "b_b_out": (22016, 11648, (), 1024)}
DEV_SMALL_ROWS = 12


def _pack_rows(parts):
    rows = []
    for p in parts:
        r = p.reshape(p.shape[0], -1, 1024)
        rows.append(jnp.pad(r, ((0, 0), (0, -r.shape[1] % PACK_TILE), (0, 0))))
    rows.append(jnp.zeros((parts[0].shape[0], PACK_ROWS - PACK_USED, 1024), parts[0].dtype))
    return jnp.concatenate(rows, axis=1)


def _place(vec, off, a):
    return lax.dynamic_update_slice(vec, a.reshape(-1).astype(F32), (off,))


def kernel(x, ffn1_norm, ffn1_w_gu, ffn1_w_down, mix_norm, ffn2_norm, ffn2_w_gu, ffn2_w_down, a_w_in, a_w_conv, a_A_log, a_dt_bias, a_out_norm, a_w_out, b_w_in, b_b_in, b_sinks, b_w_out, b_b_out, final_norm, loss_target, m_ffn1_norm, m_ffn1_w_gu, m_ffn1_w_down, m_mix_norm, m_ffn2_norm, m_ffn2_w_gu, m_ffn2_w_down, m_a_w_in, m_a_w_conv, m_a_A_log, m_a_dt_bias, m_a_out_norm, m_a_w_out, m_b_w_in, m_b_b_in, m_b_sinks, m_b_w_out, m_b_b_out, m_final_norm, v_ffn1_norm, v_ffn1_w_gu, v_ffn1_w_down, v_mix_norm, v_ffn2_norm, v_ffn2_w_gu, v_ffn2_w_down, v_a_w_in, v_a_w_conv, v_a_A_log, v_a_dt_bias, v_a_out_norm, v_a_w_out, v_b_w_in, v_b_b_in, v_b_sinks, v_b_w_out, v_b_b_out, v_final_norm):
    w = dict(zip(WEIGHTS, (ffn1_norm, ffn1_w_gu, ffn1_w_down, mix_norm, ffn2_norm, ffn2_w_gu, ffn2_w_down, a_w_in, a_w_conv,
                           a_A_log, a_dt_bias, a_out_norm, a_w_out, b_w_in, b_b_in, b_sinks, b_w_out, b_b_out, final_norm)))
    m = dict(zip(WEIGHTS, (m_ffn1_norm, m_ffn1_w_gu, m_ffn1_w_down, m_mix_norm, m_ffn2_norm, m_ffn2_w_gu, m_ffn2_w_down,
                           m_a_w_in, m_a_w_conv, m_a_A_log, m_a_dt_bias, m_a_out_norm, m_a_w_out, m_b_w_in, m_b_b_in,
                           m_b_sinks, m_b_w_out, m_b_b_out, m_final_norm)))
    v = dict(zip(WEIGHTS, (v_ffn1_norm, v_ffn1_w_gu, v_ffn1_w_down, v_mix_norm, v_ffn2_norm, v_ffn2_w_gu, v_ffn2_w_down,
                           v_a_w_in, v_a_w_conv, v_a_A_log, v_a_dt_bias, v_a_out_norm, v_a_w_out, v_b_w_in, v_b_b_in,
                           v_b_sinks, v_b_w_out, v_b_b_out, v_final_norm)))
    chip = 2 * lax.axis_index("x") + lax.axis_index("y")

    big_a = jnp.concatenate([w["ffn1_w_gu"], w["ffn2_w_gu"]], axis=0).astype(BF16).reshape(A_ROWS, FF_BLK)
    big_b = _pack_rows([w[n].astype(BF16).reshape(1, -1) for n, _ in PACK])[0]
    small = jnp.zeros((4096,), F32)
    small = _place(small, 0, w["a_w_conv"])
    small = _place(small, 3072, w["b_b_in"])
    small = _place(small, 3456, w["b_b_out"]).reshape(SMALL_SHARD)
    ra, rb, rs = _gather_chips(big_a, big_b, small)
    ga = _gather_fill("a", big_a, ra).reshape(4, 4, D_MODEL, FF_BLK)
    gb = _gather_fill("b", big_b, rb)
    offs, o = {}, 0
    for n, r in PACK:
        offs[n] = (o, r)
        o += -(-r // PACK_TILE) * PACK_TILE
    blk = lambda n: gb[:, offs[n][0]:offs[n][0] + offs[n][1]]
    gsf = lax.dynamic_update_slice(jnp.zeros((4, 4096), F32), small.reshape(1, 4096), (chip, 0))
    for j, other in enumerate((chip ^ 2, chip ^ 1, chip ^ 3)):
        gsf = lax.dynamic_update_slice(gsf, rs[j].reshape(1, 4096), (other, 0))
    W = {n: w[n] for n in ("ffn1_norm", "ffn2_norm", "mix_norm", "a_out_norm")}
    W["ga"] = ga
    W["w_down"] = gb[:, 0:2816].reshape(4, 4, 704, 1024).transpose(1, 0, 2, 3).reshape(4, 2, FF_BLK, 1024)
    W["a_w_in"] = jnp.pad(blk("a_w_in").reshape(4, 1024, 1028).transpose(1, 0, 2).reshape(1024, A_IN_COLS),
                          ((0, 0), (0, A_IN_PAD - A_IN_COLS)))
    W["a_w_out"] = blk("a_w_out").reshape(1024, 1024)
    W["b_w_in"] = blk("b_w_in").reshape(4, 1024, 384).transpose(1, 0, 2).reshape(1024, 1536)
    W["b_w_out"] = blk("b_w_out").reshape(1024, 1024)
    W["a_w_conv"] = gsf[:, 0:3072].reshape(4, 4, 768).transpose(1, 0, 2).reshape(4, 3072)
    W["b_b_in"] = gsf[:, 3072:3456].reshape(1, 1536)
    W["b_b_out"] = gsf[:, 3456:3712].reshape(1, 1024)
    W["a_gate_p"] = jnp.pad(jnp.concatenate([-jnp.exp(w["a_A_log"]), w["a_dt_bias"]], axis=0), ((0, 0), (8, 112)))
    W["b_sinks"] = jnp.broadcast_to(w["b_sinks"][0][:, None, None], (B_HEADS, 1, 128))
    W["final_norm"] = w["final_norm"][None]

    loss_p, dx, g = _local_step(x[0], loss_target[0], W)

    down = g["w_down"].reshape(4, 4, 704, 1024).transpose(1, 0, 2, 3)
    parts = [down[:, 0:2], down[:, 2:4],
             g["a_w_in"][:, :A_IN_COLS].reshape(1024, 4, 1028).transpose(1, 0, 2), g["a_w_out"].reshape(4, 256, 1024),
             g["b_w_in"].reshape(1024, 4, 384).transpose(1, 0, 2), g["b_w_out"].reshape(4, 256, 1024)]
    red_a, red_b = _reduce_scatter(g["ga"].reshape(4, A_ROWS, FF_BLK),
                                   _pack_rows([a.astype(BF16).reshape(4, -1) for a in parts]))
    grads = {n: red_b[offs[n][0]:offs[n][0] + offs[n][1]].reshape(w[n].shape) for n, _ in PACK}
    grads["ffn1_w_gu"] = red_a[:A_ROWS // 2].reshape(w["ffn1_w_gu"].shape)
    grads["ffn2_w_gu"] = red_a[A_ROWS // 2:].reshape(w["ffn2_w_gu"].shape)

    sv = jnp.zeros((SMALL_ROWS * 1024,), F32)
    small_g = {"ffn1_norm": g["ffn1_norm"], "mix_norm": g["mix_norm"], "ffn2_norm": g["ffn2_norm"], "final_norm": g["final_norm"],
               "a_A_log": g["a_gate_p"][0, 8:16], "a_dt_bias": g["a_gate_p"][1, 8:16], "a_out_norm": g["a_out_norm"],
               "b_sinks": g["b_sinks"], "loss": loss_p[0, 0:1]}
    for n, (off, _) in SMALL_SLOTS.items():
        sv = _place(sv, off, small_g[n])
    for n, (off, _, _, _) in SMALL_SHARDED.items():
        sv = _place(sv, off, g[n])
    tot = _all_reduce_small(sv.reshape(SMALL_ROWS, 1024)).reshape(-1)
    for n, (off, size) in SMALL_SLOTS.items():
        if n != "loss":
            grads[n] = tot[off:off + size].reshape(w[n].shape)
    for n, (off, _, lead, last) in SMALL_SHARDED.items():
        full = tot[off:off + (lead[0] if lead else 1) * last].reshape(lead + (last,))
        width = last // 4
        grads[n] = lax.dynamic_slice_in_dim(full, chip * width, width, axis=-1).reshape(w[n].shape)
    loss = tot[SMALL_SLOTS["loss"][0]]

    delta, new_m, new_v = {}, {}, {}
    for n in ("ffn1_w_gu", "ffn2_w_gu") + tuple(n for n, _ in PACK):
        two_d = lambda a: a.reshape(-1, a.shape[-1])
        d, nm, nv = _adamw("adamw_" + n, two_d(w[n]), two_d(grads[n]), two_d(m[n]), two_d(v[n]))
        delta[n], new_m[n], new_v[n] = d.reshape(w[n].shape), nm.reshape(w[n].shape), nv.reshape(w[n].shape)

    def dev_small(src):
        vec = jnp.zeros((DEV_SMALL_ROWS * 1024,), F32)
        for n, (off, _) in SMALL_SLOTS.items():
            if n != "loss":
                vec = _place(vec, off, src[n])
        for n, (_, off, _, _) in SMALL_SHARDED.items():
            vec = _place(vec, off, src[n])
        return vec.reshape(DEV_SMALL_ROWS, 1024)

    sd, sm, svv = _adamw("adamw_small", dev_small(w), dev_small(grads), dev_small(m), dev_small(v))
    for n in WEIGHTS:
        if n in SMALL_SLOTS:
            off, size = SMALL_SLOTS[n]
        elif n in SMALL_SHARDED:
            off, size = SMALL_SHARDED[n][1], w[n].size
        else:
            continue
        for dst, src in ((delta, sd), (new_m, sm), (new_v, svv)):
            dst[n] = src.reshape(-1)[off:off + size].reshape(w[n].shape)

    return (loss, dx[None], *[grads[n] for n in WEIGHTS], *[delta[n] for n in WEIGHTS],
            *[new_m[n] for n in WEIGHTS], *[new_v[n] for n in WEIGHTS])
```

```python
import jax
import jax.numpy as jnp
from jax import lax
from jax.experimental import pallas as pl
from jax.experimental.pallas import tpu as pltpu

F32 = jnp.float32
BF16 = jnp.bfloat16

D_MODEL = 1024
EPS = 1e-6
FF_BLK = 1408
A_HEADS = 8
A_DK = 128
A_CHUNK = 64
A_HG = 8
A_IN_COLS = 4112
A_IN_PAD = 4224
B_HEADS = 16
B_KV = 4
B_HD = 64
B_BLK = 128
ADAM_LR, ADAM_B1, ADAM_B2, ADAM_EPS, ADAM_WD, ADAM_STEP = 0.001, 0.9, 0.999, 1e-08, 0.01, 10
MESH = pl.DeviceIdType.MESH
VMEM_LIMIT = 56 * 1024 * 1024
HBM = pl.BlockSpec(memory_space=pl.ANY)


def _params(n_axes):
    return pltpu.CompilerParams(dimension_semantics=("arbitrary",) * n_axes, vmem_limit_bytes=VMEM_LIMIT)


def _sigmoid(x):
    return 1.0 / (1.0 + jnp.exp(-x))


def _dot(a, b, ca, cb):
    return lax.dot_general(a, b, (((ca,), (cb,)), ((), ())), preferred_element_type=F32)


def _dotb(a, b, ca=1, cb=0):
    return _dot(a.astype(BF16), b.astype(BF16), ca, cb)


def _dotx(a, b, ca=1, cb=0):
    return lax.dot_general(a, b, (((ca,), (cb,)), ((), ())), preferred_element_type=F32,
                           precision=lax.Precision.HIGHEST)


def _doth(a, b, ca=1, cb=0):
    return lax.dot_general(a, b, (((ca,), (cb,)), ((), ())), preferred_element_type=F32,
                           precision=lax.Precision.HIGH)


def _rms_fwd(name, x, w):
    T, D = x.shape
    tt = min(T, 512)

    def body(x_ref, w_ref, h_ref):
        xv = x_ref[...]
        r = lax.rsqrt(jnp.mean(xv * xv, axis=-1, keepdims=True) + EPS)
        h_ref[...] = (xv * r * w_ref[...]).astype(BF16)

    return pl.pallas_call(
        body, grid=(T // tt,),
        in_specs=[pl.BlockSpec((tt, D), lambda i: (i, 0)), pl.BlockSpec((1, D), lambda i: (0, 0))],
        out_specs=pl.BlockSpec((tt, D), lambda i: (i, 0)),
        out_shape=jax.ShapeDtypeStruct((T, D), BF16), name=name, compiler_params=_params(1))(x, w)


def _rms_bwd_tile(dh, xv, dy, w):
    r = lax.rsqrt(jnp.mean(xv * xv, axis=-1, keepdims=True) + EPS)
    xhat = xv * r
    dxhat = dh * w
    dx = dy + r * (dxhat - xhat * jnp.mean(dxhat * xhat, axis=-1, keepdims=True))
    return dx, jnp.sum(dh * xhat, axis=0, keepdims=True)


def _colsum(name, a):
    T, N = a.shape
    tt = min(T, 512)

    def body(a_ref, o_ref):
        @pl.when(pl.program_id(0) == 0)
        def _():
            o_ref[...] = jnp.zeros_like(o_ref)
        o_ref[...] += jnp.sum(a_ref[...].astype(F32), axis=0, keepdims=True)

    return pl.pallas_call(
        body, grid=(T // tt,), in_specs=[pl.BlockSpec((tt, N), lambda i: (i, 0))],
        out_specs=pl.BlockSpec((1, N), lambda i: (0, 0)),
        out_shape=jax.ShapeDtypeStruct((1, N), F32), name=name, compiler_params=_params(1))(a)


def _matmul(name, a, b, ca, cb, tm, tn, tk, extra_in, outs, epi, order="ji"):
    M, K, N = a.shape[1 - ca], a.shape[ca], b.shape[1 - cb]
    tm, tn, tk = min(tm, M), min(tn, N), min(tk, K)
    assert M % tm == 0 and N % tn == 0 and K % tk == 0, (name, M, N, K, tm, tn, tk)
    ni, nj, nk = M // tm, N // tn, K // tk
    if order == "ji":
        grid = (nj, ni, nk)
        perm = lambda g0, g1, g2: (g1, g0, g2)
    else:
        grid = (ni, nj, nk)
        perm = lambda g0, g1, g2: (g0, g1, g2)

    def wrap(f):
        return lambda g0, g1, g2: f(*perm(g0, g1, g2))

    a_spec = (pl.BlockSpec((tm, tk), wrap(lambda i, j, k: (i, k))) if ca == 1
              else pl.BlockSpec((tk, tm), wrap(lambda i, j, k: (k, i))))
    b_spec = (pl.BlockSpec((tk, tn), wrap(lambda i, j, k: (k, j))) if cb == 0
              else pl.BlockSpec((tn, tk), wrap(lambda i, j, k: (j, k))))
    ne, no = len(extra_in), len(outs)

    def body(*refs):
        a_ref, b_ref = refs[0], refs[1]
        ex, out = refs[2:2 + ne], refs[2 + ne:2 + ne + no]
        i, j, k = perm(pl.program_id(0), pl.program_id(1), pl.program_id(2))
        p = _dotb(a_ref[...], b_ref[...], ca, cb)
        if nk == 1:
            epi(p, ex, out, i, j)
        else:
            acc_ref = refs[-1]

            @pl.when(k == 0)
            def _():
                acc_ref[...] = p

            @pl.when(k > 0)
            def _():
                acc_ref[...] += p

            @pl.when(k == nk - 1)
            def _():
                epi(acc_ref[...], ex, out, i, j)

    return pl.pallas_call(
        body, grid=grid,
        in_specs=[a_spec, b_spec] + [pl.BlockSpec(bs, wrap(f)) for _, bs, f in extra_in],
        out_specs=[pl.BlockSpec(bs, wrap(f)) for _, bs, f in outs],
        out_shape=[s for s, _, _ in outs],
        scratch_shapes=[pltpu.VMEM((tm, tn), F32)] if nk > 1 else [],
        name=name, compiler_params=_params(3))(a, b, *[x for x, _, _ in extra_in])


def _mm_plain(name, a, b, ca, cb, out_dtype, tm=1024, tn=1024, tk=1024, scale=1.0, bias=None):
    M, N = a.shape[1 - ca], b.shape[1 - cb]
    tm, tn = min(tm, M), min(tn, N)
    extra = [] if bias is None else [(bias, (1, tn), lambda i, j, k: (0, j))]

    def epi(acc, ex, out, i, j):
        r = acc * scale if scale != 1.0 else acc
        if bias is not None:
            r = r + ex[0][...]
        out[0][...] = r.astype(out_dtype)

    return _matmul(name, a, b, ca, cb, tm, tn, tk, extra,
                   [(jax.ShapeDtypeStruct((M, N), out_dtype), (tm, tn), lambda i, j, k: (i, j))], epi)[0]


def _mm_residual(name, a, b, x, scale, bias=None, tk=1024):
    M, N = x.shape
    tm, tn = min(512, M), N
    extra = [(x, (tm, tn), lambda i, j, k: (i, j))]
    if bias is not None:
        extra.append((bias, (1, tn), lambda i, j, k: (0, j)))

    def epi(acc, ex, out, i, j):
        r = acc if bias is None else acc + ex[1][...]
        out[0][...] = ex[0][...] + scale * r

    return _matmul(name, a, b, 1, 0, tm, tn, tk, extra,
                   [(jax.ShapeDtypeStruct((M, N), F32), (tm, tn), lambda i, j, k: (i, j))], epi, order="ij")[0]


def _mm_rms_bwd(name, dproj, w_in, x, dy, nw, tk, out_scale):
    M, N = x.shape
    tm = min(512, M)
    extra = [(x, (tm, N), lambda i, j, k: (i, 0)), (dy, (tm, N), lambda i, j, k: (i, 0)),
             (nw, (1, N), lambda i, j, k: (0, 0))]

    def epi(acc, ex, out, i, j):
        dx, dw = _rms_bwd_tile(acc, ex[0][...], ex[1][...], ex[2][...])
        out[0][...] = dx
        out[1][...] = (dx * out_scale).astype(BF16)

        @pl.when(i == 0)
        def _():
            out[2][...] = dw

        @pl.when(i > 0)
        def _():
            out[2][...] += dw

    return _matmul(name, dproj, w_in, 1, 1, tm, N, tk, extra,
                   [(jax.ShapeDtypeStruct((M, N), F32), (tm, N), lambda i, j, k: (i, 0)),
                    (jax.ShapeDtypeStruct((M, N), BF16), (tm, N), lambda i, j, k: (i, 0)),
                    (jax.ShapeDtypeStruct((1, N), F32), (1, N), lambda i, j, k: (0, 0))], epi, order="ij")


def _ffn_gu(name, x, nw, ga, blk):
    T, D = x.shape
    tm = min(T, 512)
    rs = min(tm, 256)

    def body(x_ref, nw_ref, wg0, wg1, wu0, wu1, h_ref, gu_ref, act_ref):
        for r in range(tm // rs):
            rows = pl.ds(r * rs, rs)
            xv = x_ref[rows, :]
            hv = (xv * lax.rsqrt(jnp.mean(xv * xv, axis=-1, keepdims=True) + EPS) * nw_ref[...]).astype(BF16)
            h_ref[rows, :] = hv
            for j, (wg_ref, wu_ref) in enumerate(((wg0, wu0), (wg1, wu1))):
                g = _dot(hv, wg_ref[0, 0], 1, 0)
                u = _dot(hv, wu_ref[0, 0], 1, 0)
                s = _sigmoid(g)
                gs = g * s
                gu_ref[0, j, rows, :] = (u * (s + gs * (1.0 - s))).astype(BF16)
                gu_ref[1, j, rows, :] = gs.astype(BF16)
                act_ref[j, rows, :] = (gs * u).astype(BF16)

    wspec = lambda q: pl.BlockSpec((1, 1, D, FF_BLK), lambda i: (q, blk, 0, 0), pipeline_mode=pl.Buffered(1))
    return pl.pallas_call(
        body, grid=(T // tm,),
        in_specs=[pl.BlockSpec((tm, D), lambda i: (i, 0)), pl.BlockSpec((1, D), lambda i: (0, 0)),
                  wspec(0), wspec(1), wspec(2), wspec(3)],
        out_specs=[pl.BlockSpec((tm, D), lambda i: (i, 0)),
                   pl.BlockSpec((2, 2, tm, FF_BLK), lambda i: (0, 0, i, 0)),
                   pl.BlockSpec((2, tm, FF_BLK), lambda i: (0, i, 0))],
        out_shape=[jax.ShapeDtypeStruct((T, D), BF16), jax.ShapeDtypeStruct((2, 2, T, FF_BLK), BF16),
                   jax.ShapeDtypeStruct((2, T, FF_BLK), BF16)],
        name=name, compiler_params=_params(1))(x, nw, ga, ga, ga, ga)


def _ffn_down(name, act, wd, x):
    T, D = x.shape
    tm = min(T, 512)

    def body(act_ref, wd_ref, x_ref, o_ref):
        acc = _dot(act_ref[0], wd_ref[0], 1, 0) + _dot(act_ref[1], wd_ref[1], 1, 0)
        o_ref[...] = x_ref[...] + 0.5 * acc

    return pl.pallas_call(
        body, grid=(T // tm,),
        in_specs=[pl.BlockSpec((2, tm, FF_BLK), lambda i: (0, i, 0)),
                  pl.BlockSpec((2, FF_BLK, D), lambda i: (0, 0, 0)),
                  pl.BlockSpec((tm, D), lambda i: (i, 0))],
        out_specs=pl.BlockSpec((tm, D), lambda i: (i, 0)),
        out_shape=jax.ShapeDtypeStruct((T, D), F32), name=name, compiler_params=_params(1))(act, wd, x)


def _ffn_dact(name, dyh, wd, gu):
    T, D = dyh.shape
    tm = min(T, 1024)
    rs = min(tm, 256)

    def body(dy_ref, wd_ref, gu_ref, o_ref):
        for r in range(tm // rs):
            rows = pl.ds(r * rs, rs)
            dact = _dot(dy_ref[rows, :], wd_ref[0], 1, 1)
            o_ref[0, 0, rows, :] = (dact * gu_ref[0, 0, rows, :].astype(F32)).astype(BF16)
            o_ref[1, 0, rows, :] = (dact * gu_ref[1, 0, rows, :].astype(F32)).astype(BF16)

    return pl.pallas_call(
        body, grid=(2, T // tm),
        in_specs=[pl.BlockSpec((tm, D), lambda j, i: (i, 0)),
                  pl.BlockSpec((1, FF_BLK, D), lambda j, i: (j, 0, 0)),
                  pl.BlockSpec((2, 1, tm, FF_BLK), lambda j, i: (0, j, i, 0))],
        out_specs=pl.BlockSpec((2, 1, tm, FF_BLK), lambda j, i: (0, j, i, 0)),
        out_shape=jax.ShapeDtypeStruct((2, 2, T, FF_BLK), BF16), name=name, compiler_params=_params(2))(dyh, wd, gu)


def _ffn_dwd(name, act, dyh):
    _, T, _ = act.shape
    D = dyh.shape[1]
    tk = min(T, 2048)
    nk = T // tk

    def body(a_ref, d_ref, o_ref, acc_ref):
        k = pl.program_id(1)
        p = _dot(a_ref[0], d_ref[...], 0, 0)

        @pl.when(k == 0)
        def _():
            acc_ref[...] = p

        @pl.when(k > 0)
        def _():
            acc_ref[...] += p

        @pl.when(k == nk - 1)
        def _():
            o_ref[0] = acc_ref[...].astype(BF16)

    return pl.pallas_call(
        body, grid=(2, nk),
        in_specs=[pl.BlockSpec((1, tk, FF_BLK), lambda j, k: (j, k, 0)), pl.BlockSpec((tk, D), lambda j, k: (k, 0))],
        out_specs=pl.BlockSpec((1, FF_BLK, D), lambda j, k: (j, 0, 0)),
        out_shape=jax.ShapeDtypeStruct((2, FF_BLK, D), BF16), scratch_shapes=[pltpu.VMEM((FF_BLK, D), F32)],
        name=name, compiler_params=_params(2))(act, dyh)


def _ffn_dwgu(name, h, dgu, pa, blk):
    T, D = h.shape
    tk = min(T, 2048)
    nk = T // tk

    def body(h_ref, d_ref, pa_in, o_ref, acc_ref):
        k = pl.program_id(1)
        p = _dot(h_ref[...], d_ref[0, 0], 0, 0)

        @pl.when(k == 0)
        def _():
            acc_ref[...] = p

        @pl.when(k > 0)
        def _():
            acc_ref[...] += p

        @pl.when(k == nk - 1)
        def _():
            o_ref[0, 0] = acc_ref[...].astype(BF16)

    return pl.pallas_call(
        body, grid=(4, nk),
        in_specs=[pl.BlockSpec((tk, D), lambda q, k: (k, 0)),
                  pl.BlockSpec((1, 1, tk, FF_BLK), lambda q, k: (q // 2, q % 2, k, 0)), HBM],
        out_specs=pl.BlockSpec((1, 1, D, FF_BLK), lambda q, k: (q, blk, 0, 0)),
        out_shape=jax.ShapeDtypeStruct(pa.shape, BF16), scratch_shapes=[pltpu.VMEM((D, FF_BLK), F32)],
        input_output_aliases={2: 0}, name=name, compiler_params=_params(2))(h, dgu, pa)


def _ffn_dx(name, dgu, ga, blk, x, dy, nw, out_scale):
    T, D = x.shape
    tm = min(T, 512)

    def body(d_ref, w0, w1, w2, w3, x_ref, dy_ref, nw_ref, dx_ref, dxb_ref, dnw_ref):
        i = pl.program_id(0)
        acc = (_dot(d_ref[0, 0], w0[0, 0], 1, 1) + _dot(d_ref[0, 1], w1[0, 0], 1, 1)
               + _dot(d_ref[1, 0], w2[0, 0], 1, 1) + _dot(d_ref[1, 1], w3[0, 0], 1, 1))
        dx, dw = _rms_bwd_tile(acc, x_ref[...], dy_ref[...], nw_ref[...])
        dx_ref[...] = dx
        dxb_ref[...] = (dx * out_scale).astype(BF16)

        @pl.when(i == 0)
        def _():
            dnw_ref[...] = dw

        @pl.when(i > 0)
        def _():
            dnw_ref[...] += dw

    wspec = lambda q: pl.BlockSpec((1, 1, D, FF_BLK), lambda i: (q, blk, 0, 0), pipeline_mode=pl.Buffered(1))
    row = pl.BlockSpec((tm, D), lambda i: (i, 0))
    return pl.pallas_call(
        body, grid=(T // tm,),
        in_specs=[pl.BlockSpec((2, 2, tm, FF_BLK), lambda i: (0, 0, i, 0)), wspec(0), wspec(1), wspec(2), wspec(3),
                  row, row, pl.BlockSpec((1, D), lambda i: (0, 0))],
        out_specs=[row, row, pl.BlockSpec((1, D), lambda i: (0, 0))],
        out_shape=[jax.ShapeDtypeStruct((T, D), F32), jax.ShapeDtypeStruct((T, D), BF16),
                   jax.ShapeDtypeStruct((1, D), F32)],
        name=name, compiler_params=_params(1))(dgu, ga, ga, ga, ga, x, dy, nw)


def _ffn_fwd(tag, x, nw, ga, blk, wd):
    h, gu, act = _ffn_gu(tag + "_gu", x, nw, ga, blk)
    return _ffn_down(tag + "_down", act, wd, x), (h, gu, act)


def _ffn_bwd(tag, dy, dyh, x, nw, ga, blk, wd, saved, pa, out_scale):
    h, gu, act = saved
    dgu = _ffn_dact(tag + "_dact", dyh, wd, gu)
    d_wd = _ffn_dwd(tag + "_dwd", act, dyh)
    pa = _ffn_dwgu(tag + "_dwgu", h, dgu, pa, blk)
    dx, dxb, d_nw = _ffn_dx(tag + "_dx", dgu, ga, blk, x, dy, nw, out_scale)
    return dx, dxb, d_nw, pa, d_wd


def _conv_taps(cur, halo, w, first, sign):
    tt = cur.shape[0]
    halo = jnp.where(first, 0.0, halo)
    rid = lax.broadcasted_iota(jnp.int32, (8, cur.shape[1]), 0)
    acc = w[3:4, :] * cur
    for s in (1, 2, 3):
        if sign < 0:
            sh = pltpu.roll(cur, s, 0)
            edge = jnp.where(rid < s, pltpu.roll(halo, s, 0), sh[0:8])
            sh = jnp.concatenate([edge, sh[8:]], axis=0) if tt > 8 else edge
        else:
            sh = pltpu.roll(cur, tt - s, 0)
            edge = jnp.where(rid >= 8 - s, pltpu.roll(halo, 8 - s, 0), sh[tt - 8:])
            sh = jnp.concatenate([sh[:tt - 8], edge], axis=0) if tt > 8 else edge
        acc = acc + w[3 - s:4 - s, :] * sh
    return acc


def _gdn_prep(name, proj, wconv, gate_p):
    T = proj.shape[0]
    tt = min(T, 256)
    hb = tt // 8
    nch = tt // A_CHUNK

    def body(cur_ref, halo_ref, ba_ref, w_ref, gp_ref, qkv_ref, bg_ref, gc_ref):
        first = pl.program_id(0) == 0
        for c in range(24):
            cols = pl.ds(c * 128, 128)
            conv = _conv_taps(cur_ref[:, cols], halo_ref[:, cols], w_ref[:, cols], first, -1)
            y = conv * _sigmoid(conv)
            if c < 16:
                y = y * lax.rsqrt(jnp.sum(y * y, axis=-1, keepdims=True) + EPS)
                if c < 8:
                    y = y * (A_DK ** -0.5)
            qkv_ref[:, cols] = y
        ba = ba_ref[...]
        lane = lax.broadcasted_iota(jnp.int32, ba.shape, 1)
        zarg = ba + gp_ref[1:2, :]
        softplus = jnp.maximum(zarg, 0.0) + jnp.log(1.0 + jnp.exp(-jnp.abs(zarg)))
        bg = jnp.where(lane < 8, _sigmoid(ba), jnp.where(lane < 16, gp_ref[0:1, :] * softplus, 0.0))
        bg_ref[...] = bg
        tri = (lax.broadcasted_iota(jnp.int32, (A_CHUNK, A_CHUNK), 0)
               >= lax.broadcasted_iota(jnp.int32, (A_CHUNK, A_CHUNK), 1)).astype(F32)
        for c in range(nch):
            rows = pl.ds(c * A_CHUNK, A_CHUNK)
            gc_ref[rows, :] = _dotx(tri, bg[c * A_CHUNK:(c + 1) * A_CHUNK, :])

    return pl.pallas_call(
        body, grid=(T // tt,),
        in_specs=[pl.BlockSpec((tt, 3072), lambda i: (i, 0)),
                  pl.BlockSpec((8, 3072), lambda i: (jnp.maximum(i * hb - 1, 0), 0)),
                  pl.BlockSpec((tt, 128), lambda i: (i, 32)),
                  pl.BlockSpec((4, 3072), lambda i: (0, 0)),
                  pl.BlockSpec((2, 128), lambda i: (0, 0))],
        out_specs=[pl.BlockSpec((tt, 3072), lambda i: (i, 0)), pl.BlockSpec((tt, 128), lambda i: (i, 0)),
                   pl.BlockSpec((tt, 128), lambda i: (i, 0))],
        out_shape=[jax.ShapeDtypeStruct((T, 3072), F32), jax.ShapeDtypeStruct((T, 128), F32),
                   jax.ShapeDtypeStruct((T, 128), F32)],
        name=name, compiler_params=_params(1))(proj, proj, proj, wconv, gate_p)


def _chunk_masks():
    ri = lax.broadcasted_iota(jnp.int32, (A_CHUNK, A_CHUNK), 0)
    ci = lax.broadcasted_iota(jnp.int32, (A_CHUNK, A_CHUNK), 1)
    return ri >= ci, ri > ci, ri == ci


def _chunk_local(q, k, gcol, grow, bcol):
    incl, strict, _ = _chunk_masks()
    dec = jnp.where(incl, jnp.exp(jnp.where(incl, gcol - grow, 0.0)), 0.0)
    e = jnp.exp(gcol)
    glast = grow[:, A_CHUNK - 1:A_CHUNK]
    f = jnp.exp(glast - gcol)
    gl = jnp.exp(glast)
    kb = k * bcol
    lmat = jnp.where(strict, _dotb(kb, k, 1, 1) * dec, 0.0)
    amat = jnp.where(incl, _dotb(q, k, 1, 1) * dec, 0.0)
    return dec, e, f, gl, kb, lmat, amat


def _unit_lower_inverse(lmats):
    _, _, eye = _chunk_masks()
    ts = [jnp.where(eye, 1.0, 0.0) - lm for lm in lmats]
    lps = [_doth(lm, lm) for lm in lmats]
    for it in range(5):
        ts = [t + _doth(t, lp) for t, lp in zip(ts, lps)]
        if it < 4:
            lps = [_doth(lp, lp) for lp in lps]
    return ts


def _gate_columns(bgt, gct):
    sel = (lax.broadcasted_iota(jnp.int32, (16, 128), 0) == lax.broadcasted_iota(jnp.int32, (16, 128), 1)).astype(F32)
    g_rows = _dotx(sel, gct, 1, 1)
    hs = range(A_HEADS)
    return ([bgt[:, h:h + 1] for h in hs], [gct[:, 8 + h:9 + h] for h in hs], [g_rows[8 + h:9 + h, :] for h in hs])


def _gdn_delta_fwd(name, qkv, bg, gcum, a4=None):
    T = qkv.shape[0]
    tt = min(T, 512)
    nch = tt // A_CHUNK
    NC = T // A_CHUNK
    ni = T // tt
    wd, ng = 128 * A_HG, A_HEADS // A_HG
    assert ng == 1

    def body(*refs):
        q_ref, k_ref, v_ref, bg_ref, gc_ref = refs[:5]
        if a4 is None:
            o_ref, s_ref, t_ref, u_ref, w_ref, state = refs[5:]
        else:
            a4_ref, o_ref, s_ref, t_ref, u_ref, w_ref, ra_ref, state, send_sems, recv_sems = refs[5:]

            @pl.when(pl.program_id(1) == 0)
            def _():
                _later_blocks_start(a4_ref, ra_ref, send_sems, recv_sems)

        @pl.when(pl.program_id(1) == 0)
        def _():
            state[...] = jnp.zeros_like(state)

        def chunk(c, carry):
            rows = pl.ds(pl.multiple_of(c * A_CHUNK, A_CHUNK), A_CHUNK)
            hs = range(A_HG)
            cols = [pl.ds(h * 128, 128) for h in hs]
            q = [q_ref[rows, cols[h]] for h in hs]
            k = [k_ref[rows, cols[h]] for h in hs]
            v = [v_ref[rows, cols[h]] for h in hs]
            bcl, gcl, grw = _gate_columns(bg_ref[rows, :], gc_ref[rows, :])
            loc = [_chunk_local(q[h], k[h], gcl[h], grw[h], bcl[h]) for h in hs]
            e, f, gl, kb, amat = ([l[i] for l in loc] for i in (1, 2, 3, 4, 6))
            tinv = _unit_lower_inverse([l[5] for l in loc])
            u = [_doth(tinv[h], v[h] * bcl[h]) for h in hs]
            w = [_doth(tinv[h], kb[h] * e[h]) for h in hs]
            s = [state[h] for h in hs]
            vn = [u[h] - _dotb(w[h], s[h]) for h in hs]
            o_s = [_dotb(q[h] * e[h], s[h]) for h in hs]
            o_a = [_dotb(amat[h], vn[h]) for h in hs]
            s_new = [s[h] * gl[h] + _dotb(k[h] * f[h], vn[h], 0, 0) for h in hs]
            for h in hs:
                s_ref[h, c] = s[h].astype(BF16)
                t_ref[h, c] = tinv[h]
                u_ref[rows, cols[h]] = u[h]
                w_ref[rows, cols[h]] = w[h]
                o_ref[rows, cols[h]] = o_s[h] + o_a[h]
                state[h] = s_new[h]
            return carry

        lax.fori_loop(0, nch, chunk, 0)

        if a4 is not None:
            @pl.when(pl.program_id(1) == ni - 1)
            def _():
                x, y, c, chips = _mesh_pos()
                for j, chip in enumerate(chips):
                    cp = pltpu.make_async_remote_copy(src_ref=a4_ref.at[pl.ds(1, 3), c], dst_ref=ra_ref.at[j],
                                                      send_sem=send_sems.at[j], recv_sem=recv_sems.at[j],
                                                      device_id=(*chip, c), device_id_type=MESH)
                    cp.wait_recv()
                    cp.wait_send()

    hd = lambda col: pl.BlockSpec((tt, wd), lambda h, i: (i, col * ng + h))
    gate_spec = pl.BlockSpec((tt, 128), lambda h, i: (i, 0))
    in_specs = [hd(0), hd(1), hd(2), gate_spec, gate_spec]
    out_specs = [pl.BlockSpec((tt, wd), lambda h, i: (i, h)),
                 pl.BlockSpec((A_HG, nch, 128, 128), lambda h, i: (h, i, 0, 0)),
                 pl.BlockSpec((A_HG, nch, A_CHUNK, A_CHUNK), lambda h, i: (h, i, 0, 0)),
                 pl.BlockSpec((tt, wd), lambda h, i: (i, h)), pl.BlockSpec((tt, wd), lambda h, i: (i, h))]
    out_shape = [jax.ShapeDtypeStruct((T, 1024), F32), jax.ShapeDtypeStruct((A_HEADS, NC, 128, 128), BF16),
                 jax.ShapeDtypeStruct((A_HEADS, NC, A_CHUNK, A_CHUNK), F32),
                 jax.ShapeDtypeStruct((T, 1024), F32), jax.ShapeDtypeStruct((T, 1024), F32)]
    scratch = [pltpu.VMEM((A_HG, 128, 128), F32)]
    args = [qkv, qkv, qkv, bg, gcum]
    if a4 is not None:
        in_specs.append(HBM)
        out_specs.append(HBM)
        out_shape.append(jax.ShapeDtypeStruct((3, 3, A_HALF, FF_BLK), BF16))
        scratch += [pltpu.SemaphoreType.DMA((3,)), pltpu.SemaphoreType.DMA((3,))]
        args.append(a4)
    return pl.pallas_call(body, grid=(ng, ni), in_specs=in_specs, out_specs=out_specs, out_shape=out_shape,
                          scratch_shapes=scratch, name=name, compiler_params=_params(2))(*args)


def _gdn_delta_bwd(name, qkv, bg, gcum, d_o, s_sv, t_sv, u_sv, w_sv):
    T = qkv.shape[0]
    tt = min(T, 512)
    nch = tt // A_CHUNK
    ni = T // tt

    def body(q_ref, k_ref, v_ref, bg_ref, gc_ref, do_ref, s_ref, t_ref, u_ref, w_ref,
             dq_ref, dk_ref, dv_ref, dbg_ref, dstate):
        @pl.when(pl.program_id(1) == 0)
        def _():
            dstate[...] = jnp.zeros_like(dstate)

        incl, strict, _ = _chunk_masks()
        upper = (lax.broadcasted_iota(jnp.int32, (A_CHUNK, A_CHUNK), 0)
                 <= lax.broadcasted_iota(jnp.int32, (A_CHUNK, A_CHUNK), 1)).astype(F32)
        last_row = lax.broadcasted_iota(jnp.int32, (A_CHUNK, 1), 0) == A_CHUNK - 1
        ones = jnp.ones((A_CHUNK, 128), F32)

        rsum = lambda x: jnp.sum(x, axis=1, keepdims=True)

        def chunk(cc, carry):
            c = nch - 1 - cc
            rows = pl.ds(pl.multiple_of(c * A_CHUNK, A_CHUNK), A_CHUNK)
            bcl, gcl, grw = _gate_columns(bg_ref[rows, :], gc_ref[rows, :])
            lane = lax.broadcasted_iota(jnp.int32, (A_CHUNK, 128), 1)
            dbg = jnp.zeros((A_CHUNK, 128), F32)
            for first in range(0, A_HG, 4):
                hs = range(first, first + 4)
                cols = {h: pl.ds(h * 128, 128) for h in hs}
                q = {h: q_ref[rows, cols[h]] for h in hs}
                k = {h: k_ref[rows, cols[h]] for h in hs}
                v = {h: v_ref[rows, cols[h]] for h in hs}
                do = {h: do_ref[rows, cols[h]] for h in hs}
                u = {h: u_ref[rows, cols[h]] for h in hs}
                w = {h: w_ref[rows, cols[h]] for h in hs}
                s = {h: s_ref[h, c] for h in hs}
                tinv = {h: t_ref[h, c] for h in hs}
                ds = {h: dstate[h] for h in hs}
                loc = {h: _chunk_local(q[h], k[h], gcl[h], grw[h], bcl[h]) for h in hs}
                dec, e, f, gl, kb, lmat, amat = ({h: loc[h][i] for h in hs} for i in range(7))
                qd = {h: q[h] * e[h] for h in hs}
                kd = {h: k[h] * f[h] for h in hs}
                ke = {h: kb[h] * e[h] for h in hs}
                vn = {h: u[h] - _dotb(w[h], s[h]) for h in hs}
                d_qd = {h: _dotb(do[h], s[h], 1, 1) for h in hs}
                d_a = {h: jnp.where(incl, _dotb(do[h], vn[h], 1, 1), 0.0) for h in hs}
                d_vn1 = {h: _dotb(amat[h], do[h], 0, 0) for h in hs}
                d_vn = {h: d_vn1[h] + _dotb(kd[h], ds[h]) for h in hs}
                d_kd = {h: _dotb(vn[h], ds[h], 1, 1) for h in hs}
                d_w = {h: -_dotb(d_vn[h], s[h], 1, 1) for h in hs}
                ds_q = {h: _dotb(qd[h], do[h], 0, 0) for h in hs}
                ds_w = {h: _dotb(w[h], d_vn[h], 0, 0) for h in hs}
                d_bv = {h: _doth(tinv[h], d_vn[h], 0, 0) for h in hs}
                d_ke = {h: _doth(tinv[h], d_w[h], 0, 0) for h in hs}
                d_l1 = {h: _dotb(d_bv[h], u[h], 1, 1) for h in hs}
                d_l = {h: -jnp.where(strict, d_l1[h] + _dotb(d_ke[h], w[h], 1, 1), 0.0) for h in hs}
                d_kk = {h: d_l[h] * dec[h] for h in hs}
                d_qk = {h: d_a[h] * dec[h] for h in hs}
                d_kb = {h: _dotb(d_kk[h], k[h]) for h in hs}
                dk1 = {h: _dotb(d_kk[h], kb[h], 0, 0) for h in hs}
                dk2 = {h: _dotb(d_qk[h], q[h], 0, 0) for h in hs}
                dq1 = {h: _dotb(d_qk[h], k[h]) for h in hs}
                m = {h: d_l[h] * lmat[h] + d_a[h] * amat[h] for h in hs}
                col_m = {h: _dotx(m[h], ones, 0, 0)[:, 0:1] for h in hs}
                d_gc = {}
                for h in hs:
                    d_gl = jnp.sum(jnp.sum(ds[h] * s[h].astype(F32), axis=1, keepdims=True), axis=0, keepdims=True)
                    r_kd = rsum(d_kd[h] * kd[h])
                    tail = jnp.sum(r_kd, axis=0, keepdims=True) + d_gl * gl[h]
                    d_gc[h] = (rsum(m[h]) - col_m[h] + rsum(d_qd[h] * qd[h]) - r_kd + rsum(d_ke[h] * ke[h])
                               + jnp.where(last_row, tail, 0.0))
                dg = {h: _dotx(upper, d_gc[h] * ones)[:, 0:1] for h in hs}
                for h in hs:
                    dstate[h] = gl[h] * ds[h] + ds_q[h] - ds_w[h]
                    dk_ref[rows, cols[h]] = (dk1[h] + dk2[h] + d_kd[h] * f[h] + d_ke[h] * (bcl[h] * e[h])
                                             + d_kb[h] * bcl[h])
                    dq_ref[rows, cols[h]] = dq1[h] + d_qd[h] * e[h]
                    dv_ref[rows, cols[h]] = d_bv[h] * bcl[h]
                    d_beta = rsum(d_ke[h] * k[h]) * e[h] + rsum(d_kb[h] * k[h]) + rsum(d_bv[h] * v[h])
                    dbg = jnp.where(lane == h, d_beta, jnp.where(lane == 8 + h, dg[h], dbg))
            dbg_ref[rows, :] = dbg
            return carry

        lax.fori_loop(0, nch, chunk, 0)

    wd, ng = 128 * A_HG, A_HEADS // A_HG
    assert ng == 1
    rev = lambda i: ni - 1 - i
    hd = lambda col: pl.BlockSpec((tt, wd), lambda h, i: (rev(i), col * ng + h))
    hd1 = pl.BlockSpec((tt, wd), lambda h, i: (rev(i), h))
    gate_spec = pl.BlockSpec((tt, 128), lambda h, i: (rev(i), 0))
    return pl.pallas_call(
        body, grid=(ng, ni),
        in_specs=[hd(0), hd(1), hd(2), gate_spec, gate_spec, hd1,
                  pl.BlockSpec((A_HG, nch, 128, 128), lambda h, i: (h, rev(i), 0, 0)),
                  pl.BlockSpec((A_HG, nch, A_CHUNK, A_CHUNK), lambda h, i: (h, rev(i), 0, 0)), hd1, hd1],
        out_specs=[hd1, hd1, hd1, gate_spec],
        out_shape=[jax.ShapeDtypeStruct((T, 1024), F32)] * 3 + [jax.ShapeDtypeStruct((T, 128), F32)],
        scratch_shapes=[pltpu.VMEM((A_HG, 128, 128), F32)],
        name=name, compiler_params=_params(2))(qkv, qkv, qkv, bg, gcum, d_o, s_sv, t_sv, u_sv, w_sv)


def _gdn_gate_fwd(name, o, proj, nw):
    T = o.shape[0]
    tt = min(T, 512)

    def body(o_ref, z_ref, nw_ref, y_ref):
        for h in range(A_HEADS):
            cols = pl.ds(h * 128, 128)
            ov, z = o_ref[:, cols], z_ref[:, cols]
            r = lax.rsqrt(jnp.mean(ov * ov, axis=-1, keepdims=True) + EPS)
            y_ref[:, cols] = (ov * r * nw_ref[...] * (z * _sigmoid(z))).astype(BF16)

    return pl.pallas_call(
        body, grid=(T // tt,),
        in_specs=[pl.BlockSpec((tt, 1024), lambda i: (i, 0)), pl.BlockSpec((tt, 1024), lambda i: (i, 3)),
                  pl.BlockSpec((1, 128), lambda i: (0, 0))],
        out_specs=pl.BlockSpec((tt, 1024), lambda i: (i, 0)),
        out_shape=jax.ShapeDtypeStruct((T, 1024), BF16), name=name, compiler_params=_params(1))(o, proj, nw)


def _gdn_gate_bwd(name, dy2, o, proj, nw):
    T = o.shape[0]
    tt = min(T, 512)

    def body(dy_ref, o_ref, z_ref, nw_ref, do_ref, dz_ref, dnw_ref):
        dnw = jnp.zeros((1, 128), F32)
        for h in range(A_HEADS):
            cols = pl.ds(h * 128, 128)
            dy, ov, z = dy_ref[:, cols], o_ref[:, cols], z_ref[:, cols]
            s = _sigmoid(z)
            sz = z * s
            r = lax.rsqrt(jnp.mean(ov * ov, axis=-1, keepdims=True) + EPS)
            xhat = ov * r
            dn = dy * sz
            dz_ref[:, cols] = dy * (xhat * nw_ref[...]) * (s + z * s * (1.0 - s))
            dxhat = dn * nw_ref[...]
            do_ref[:, cols] = r * (dxhat - xhat * jnp.mean(dxhat * xhat, axis=-1, keepdims=True))
            dnw = dnw + jnp.sum(dn * xhat, axis=0, keepdims=True)

        @pl.when(pl.program_id(0) == 0)
        def _():
            dnw_ref[...] = dnw

        @pl.when(pl.program_id(0) > 0)
        def _():
            dnw_ref[...] += dnw

    blk = lambda c: pl.BlockSpec((tt, 1024), lambda i: (i, c))
    return pl.pallas_call(
        body, grid=(T // tt,),
        in_specs=[blk(0), blk(0), blk(3), pl.BlockSpec((1, 128), lambda i: (0, 0))],
        out_specs=[blk(0), blk(0), pl.BlockSpec((1, 128), lambda i: (0, 0))],
        out_shape=[jax.ShapeDtypeStruct((T, 1024), F32), jax.ShapeDtypeStruct((T, 1024), F32),
                   jax.ShapeDtypeStruct((1, 128), F32)],
        name=name, compiler_params=_params(1))(dy2, o, proj, nw)


def _gdn_prep_bwd1(name, proj, wconv, gate_p, dq, dk, dv, dbg):
    T = proj.shape[0]
    tt = min(T, 256)
    hb = tt // 8

    def body(cur_ref, halo_ref, ba_ref, w_ref, gp_ref, dq_ref, dk_ref, dv_ref, dbg_ref,
             dc_ref, dw_ref, dba_ref, dgp_ref):
        first = pl.program_id(0) == 0
        rid = lax.broadcasted_iota(jnp.int32, (8, 128), 0)
        for c in range(24):
            cols = pl.ds(c * 128, 128)
            cur = cur_ref[:, cols]
            halo = jnp.where(first, 0.0, halo_ref[:, cols])
            conv = _conv_taps(cur, halo_ref[:, cols], w_ref[:, cols], first, -1)
            s = _sigmoid(conv)
            y = conv * s
            if c < 16:
                dref = dq_ref if c < 8 else dk_ref
                dn = dref[:, pl.ds((c % 8) * 128, 128)]
                rinv = lax.rsqrt(jnp.sum(y * y, axis=-1, keepdims=True) + EPS)
                yhat = y * rinv
                dyv = rinv * (dn - yhat * jnp.sum(dn * yhat, axis=-1, keepdims=True))
                if c < 8:
                    dyv = dyv * (A_DK ** -0.5)
            else:
                dyv = dv_ref[:, pl.ds((c - 16) * 128, 128)]
            dc = dyv * (s + conv * s * (1.0 - s))
            dc_ref[:, cols] = dc
            parts = [jnp.sum(dc * cur, axis=0, keepdims=True)]
            for sft in (1, 2, 3):
                sh = pltpu.roll(cur, sft, 0)
                edge = jnp.where(rid < sft, pltpu.roll(halo, sft, 0), sh[0:8])
                sh = jnp.concatenate([edge, sh[8:]], axis=0) if tt > 8 else edge
                parts.append(jnp.sum(dc * sh, axis=0, keepdims=True))
            dwc = jnp.concatenate(parts[::-1], axis=0)

            @pl.when(first)
            def _():
                dw_ref[:, cols] = dwc

            @pl.when(jnp.logical_not(first))
            def _():
                dw_ref[:, cols] += dwc

        ba = ba_ref[...]
        dbg = dbg_ref[...]
        lane = lax.broadcasted_iota(jnp.int32, ba.shape, 1)
        sb = _sigmoid(ba)
        zarg = ba + gp_ref[1:2, :]
        softplus = jnp.maximum(zarg, 0.0) + jnp.log(1.0 + jnp.exp(-jnp.abs(zarg)))
        d_b = dbg * sb * (1.0 - sb)
        d_a = dbg * gp_ref[0:1, :] * _sigmoid(zarg)
        dba_ref[...] = jnp.where(lane < 8, d_b, jnp.where(lane < 16, d_a, 0.0))
        g = gp_ref[0:1, :] * softplus
        in_a = (lane >= 8) & (lane < 16)
        sums = jnp.concatenate([jnp.sum(jnp.where(in_a, dbg * g, 0.0), axis=0, keepdims=True),
                                jnp.sum(jnp.where(in_a, d_a, 0.0), axis=0, keepdims=True)], axis=0)

        @pl.when(first)
        def _():
            dgp_ref[...] = sums

        @pl.when(jnp.logical_not(first))
        def _():
            dgp_ref[...] += sums

    row = lambda w, c=0: pl.BlockSpec((tt, w), lambda i: (i, c))
    return pl.pallas_call(
        body, grid=(T // tt,),
        in_specs=[row(3072), pl.BlockSpec((8, 3072), lambda i: (jnp.maximum(i * hb - 1, 0), 0)), row(128, 32),
                  pl.BlockSpec((4, 3072), lambda i: (0, 0)), pl.BlockSpec((2, 128), lambda i: (0, 0)),
                  row(1024), row(1024), row(1024), row(128)],
        out_specs=[row(3072), pl.BlockSpec((4, 3072), lambda i: (0, 0)), row(128),
                   pl.BlockSpec((2, 128), lambda i: (0, 0))],
        out_shape=[jax.ShapeDtypeStruct((T, 3072), F32), jax.ShapeDtypeStruct((4, 3072), F32),
                   jax.ShapeDtypeStruct((T, 128), F32), jax.ShapeDtypeStruct((2, 128), F32)],
        name=name, compiler_params=_params(1))(proj, proj, proj, wconv, gate_p, dq, dk, dv, dbg)


def _gdn_prep_bwd2(name, dc, wconv, dz, dba):
    T = dc.shape[0]
    tt = min(T, 256)
    hb = tt // 8
    ni = T // tt

    def body(cur_ref, halo_ref, w_ref, dz_ref, dba_ref, o_ref):
        last = pl.program_id(0) == ni - 1
        for c in range(24):
            cols = pl.ds(c * 128, 128)
            o_ref[:, cols] = _conv_taps(cur_ref[:, cols], halo_ref[:, cols], w_ref[:, cols], last, +1).astype(BF16)
        o_ref[:, pl.ds(3072, 1024)] = dz_ref[...].astype(BF16)
        o_ref[:, pl.ds(4096, 128)] = dba_ref[...].astype(BF16)

    return pl.pallas_call(
        body, grid=(ni,),
        in_specs=[pl.BlockSpec((tt, 3072), lambda i: (i, 0)),
                  pl.BlockSpec((8, 3072), lambda i: (jnp.minimum((i + 1) * hb, T // 8 - 1), 0)),
                  pl.BlockSpec((4, 3072), lambda i: (0, 0)),
                  pl.BlockSpec((tt, 1024), lambda i: (i, 0)), pl.BlockSpec((tt, 128), lambda i: (i, 0))],
        out_specs=pl.BlockSpec((tt, A_IN_PAD), lambda i: (i, 0)),
        out_shape=jax.ShapeDtypeStruct((T, A_IN_PAD), BF16), name=name, compiler_params=_params(1))(
            dc, dc, wconv, dz, dba)


def _gdn_fwd(x, nw, w_in, wconv, gate_p, out_nw, w_out, a4=None):
    h = _rms_fwd("a_rms", x, nw)
    proj = _mm_plain("a_proj", h, w_in, 1, 0, F32, tn=FF_BLK)
    qkv, bg, gcum = _gdn_prep("a_prep", proj, wconv, gate_p)
    o, s_sv, t_sv, u_sv, w_sv, *arrived = _gdn_delta_fwd("a_delta", qkv, bg, gcum, a4)
    o2 = _gdn_gate_fwd("a_gate", o, proj, out_nw)
    y = _mm_residual("a_out", o2, w_out, x, 1.0)
    return y, (h, proj, qkv, bg, gcum, o, s_sv, t_sv, u_sv, w_sv, o2), (arrived[0] if arrived else None)


def _gdn_bwd(dy, dyb, x, nw, w_in, wconv, gate_p, out_nw, w_out, saved, out_scale):
    h, proj, qkv, bg, gcum, o, s_sv, t_sv, u_sv, w_sv, o2 = saved
    d_o2 = _mm_plain("a_dout", dyb, w_out, 1, 1, F32)
    d_wout = _mm_plain("a_dwout", o2, dyb, 0, 0, F32)
    d_o, d_z, d_outnw = _gdn_gate_bwd("a_dgate", d_o2, o, proj, out_nw)
    dq, dk, dv, dbg = _gdn_delta_bwd("a_ddelta", qkv, bg, gcum, d_o, s_sv, t_sv, u_sv, w_sv)
    dc, d_wconv, dba, dgp = _gdn_prep_bwd1("a_dprep1", proj, wconv, gate_p, dq, dk, dv, dbg)
    dproj = _gdn_prep_bwd2("a_dprep2", dc, wconv, d_z, dba)
    d_win = _mm_plain("a_dwin", h, dproj, 0, 0, F32, tn=FF_BLK, tk=2048)
    dx, dxb, d_nw = _mm_rms_bwd("a_dx", dproj, w_in, x, dy, nw, A_IN_PAD, out_scale)
    return dx, dxb, d_nw, d_win, d_wconv, dgp, d_outnw, d_wout


def _swa_masks(n):
    qi = lax.broadcasted_iota(jnp.int32, (B_BLK, B_BLK), 0)
    kj = lax.broadcasted_iota(jnp.int32, (B_BLK, B_BLK), 1)
    return kj > qi + jnp.where(n > 0, 0, B_BLK), kj <= qi


def _swa_fwd(name, q, k, v, sinks):
    T = q.shape[1]
    tq = min(T, 1024)
    nbt = tq // B_BLK
    scale = B_HD ** -0.5
    G = B_HEADS // B_KV

    def body(q_ref, k_ref, v_ref, kh_ref, vh_ref, s_ref, o_ref, l_ref):
        first_blk = pl.program_id(1) * nbt

        def block(n, kp, vp):
            m_prev, m_cur = _swa_masks(first_blk + n)
            cur = pl.ds(pl.multiple_of(n * B_BLK, B_BLK), B_BLK)
            kc, vc = k_ref[0, cur, :], v_ref[0, cur, :]
            gs = range(G)
            rmax = lambda a: jnp.max(a, axis=1, keepdims=True)
            rsum = lambda a: jnp.sum(a, axis=1, keepdims=True)
            sink = [s_ref[g][:, 0:1] for g in gs]
            qb = [q_ref[g, cur, :] for g in gs]
            s_p = [jnp.where(m_prev, _dot(qb[g], kp, 1, 1) * scale, -jnp.inf) for g in gs]
            s_c = [jnp.where(m_cur, _dot(qb[g], kc, 1, 1) * scale, -jnp.inf) for g in gs]
            m = [jnp.maximum(jnp.maximum(rmax(s_p[g]), rmax(s_c[g])), sink[g]) for g in gs]
            p_p = [jnp.exp(s_p[g] - m[g]) for g in gs]
            p_c = [jnp.exp(s_c[g] - m[g]) for g in gs]
            den = [rsum(p_p[g]) + rsum(p_c[g]) + jnp.exp(sink[g] - m[g]) for g in gs]
            a_p = [_dotb(p_p[g], vp) for g in gs]
            a_c = [_dotb(p_c[g], vc) for g in gs]
            for g in gs:
                o_ref[g, cur, :] = ((a_p[g] + a_c[g]) / den[g]).astype(BF16)
                l_ref[g, cur, :] = m[g] + jnp.log(den[g])

        block(0, kh_ref[0], vh_ref[0])

        def rest(n, carry):
            prv = pl.ds(pl.multiple_of((n - 1) * B_BLK, B_BLK), B_BLK)
            block(n, k_ref[0, prv, :], v_ref[0, prv, :])
            return carry

        lax.fori_loop(1, nbt, rest, 0)

    qs = pl.BlockSpec((G, tq, B_HD), lambda kv, i: (kv, i, 0))
    ks = pl.BlockSpec((1, tq, B_HD), lambda kv, i: (kv, i, 0))
    halo = pl.BlockSpec((1, B_BLK, B_HD), lambda kv, i: (kv, jnp.maximum(i * nbt - 1, 0), 0))
    return pl.pallas_call(
        body, grid=(B_KV, T // tq),
        in_specs=[qs, ks, ks, halo, halo, pl.BlockSpec((G, 1, 128), lambda kv, i: (kv, 0, 0))],
        out_specs=[qs, pl.BlockSpec((G, tq, 1), lambda kv, i: (kv, i, 0))],
        out_shape=[jax.ShapeDtypeStruct((B_HEADS, T, B_HD), BF16), jax.ShapeDtypeStruct((B_HEADS, T, 1), F32)],
        name=name, compiler_params=_params(2))(q, k, v, k, v, sinks)


def _swa_bwd(name, q, k, v, sinks, o, lse, do):
    T = q.shape[1]
    tq = min(T, 1024)
    nbt, ni = tq // B_BLK, T // tq
    scale = B_HD ** -0.5
    G = B_HEADS // B_KV

    def body(q_ref, k_ref, v_ref, kh_ref, vh_ref, s_ref, o_ref, l_ref, do_ref, dq_ref, dk_ref, dv_ref, ds_ref,
             dk_halo, dv_halo):
        step = pl.program_id(1)
        first_blk = (ni - 1 - step) * nbt
        last = pl.ds(tq - B_BLK, B_BLK)
        dk_ref[...] = jnp.zeros_like(dk_ref)
        dv_ref[...] = jnp.zeros_like(dv_ref)

        @pl.when(step > 0)
        def _():
            dk_ref[0, last, :] = dk_halo[...]
            dv_ref[0, last, :] = dv_halo[...]

        def block(n, kp, vp, dsinks):
            m_prev, m_cur = _swa_masks(first_blk + n)
            cur = pl.ds(pl.multiple_of(n * B_BLK, B_BLK), B_BLK)
            kc, vc = k_ref[0, cur, :], v_ref[0, cur, :]
            gs = range(G)
            sink = [s_ref[g][:, 0:1] for g in gs]
            qb = [q_ref[g, cur, :] for g in gs]
            dob = [do_ref[g, cur, :] for g in gs]
            lse_b = [l_ref[g, cur, :] for g in gs]
            p_p = [jnp.where(m_prev, jnp.exp(_dot(qb[g], kp, 1, 1) * scale - lse_b[g]), 0.0) for g in gs]
            p_c = [jnp.where(m_cur, jnp.exp(_dot(qb[g], kc, 1, 1) * scale - lse_b[g]), 0.0) for g in gs]
            delta = [jnp.sum(dob[g].astype(F32) * o_ref[g, cur, :].astype(F32), axis=1, keepdims=True) for g in gs]
            ds_p = [p_p[g] * (_dot(dob[g], vp, 1, 1) - delta[g]) for g in gs]
            ds_c = [p_c[g] * (_dot(dob[g], vc, 1, 1) - delta[g]) for g in gs]
            dq_p = [_dotb(ds_p[g], kp) for g in gs]
            dq_c = [_dotb(ds_c[g], kc) for g in gs]
            dk_ps = [_dotb(ds_p[g], qb[g], 0, 0) for g in gs]
            dk_cs = [_dotb(ds_c[g], qb[g], 0, 0) for g in gs]
            dv_ps = [_dotb(p_p[g], dob[g], 0, 0) for g in gs]
            dv_cs = [_dotb(p_c[g], dob[g], 0, 0) for g in gs]
            for g in gs:
                dq_ref[g, cur, :] = (dq_p[g] + dq_c[g]) * scale
            out = tuple(dsinks[g] - jnp.sum(jnp.exp(sink[g] - lse_b[g]) * delta[g], axis=0, keepdims=True) for g in gs)
            total = lambda parts: (parts[0] + parts[1]) + (parts[2] + parts[3])
            dk_ref[0, cur, :] += total(dk_cs) * scale
            dv_ref[0, cur, :] += total(dv_cs)
            return total(dk_ps) * scale, total(dv_ps), out

        zeros = tuple(jnp.zeros((1, 1), F32) for _ in range(G))
        dk_p, dv_p, dsinks = block(0, kh_ref[0], vh_ref[0], zeros)
        dk_halo[...] = dk_p
        dv_halo[...] = dv_p

        def rest(n, dsinks):
            prv = pl.ds(pl.multiple_of((n - 1) * B_BLK, B_BLK), B_BLK)
            dk_p, dv_p, dsinks = block(n, k_ref[0, prv, :], v_ref[0, prv, :], dsinks)
            dk_ref[0, prv, :] += dk_p
            dv_ref[0, prv, :] += dv_p
            return dsinks

        dsinks = lax.fori_loop(1, nbt, rest, dsinks)
        for g in range(G):
            row = jnp.broadcast_to(dsinks[g], (1, 128))

            @pl.when(step == 0)
            def _():
                ds_ref[g] = row

            @pl.when(step > 0)
            def _():
                ds_ref[g] += row

    rev = lambda i: ni - 1 - i
    qs = pl.BlockSpec((G, tq, B_HD), lambda kv, i: (kv, rev(i), 0))
    ks = pl.BlockSpec((1, tq, B_HD), lambda kv, i: (kv, rev(i), 0))
    halo = pl.BlockSpec((1, B_BLK, B_HD), lambda kv, i: (kv, jnp.maximum(rev(i) * nbt - 1, 0), 0))
    ss = pl.BlockSpec((G, 1, 128), lambda kv, i: (kv, 0, 0))
    return pl.pallas_call(
        body, grid=(B_KV, ni),
        in_specs=[qs, ks, ks, halo, halo, ss, qs, pl.BlockSpec((G, tq, 1), lambda kv, i: (kv, rev(i), 0)), qs],
        out_specs=[qs, ks, ks, ss],
        out_shape=[jax.ShapeDtypeStruct((B_HEADS, T, B_HD), F32), jax.ShapeDtypeStruct((B_KV, T, B_HD), F32),
                   jax.ShapeDtypeStruct((B_KV, T, B_HD), F32), jax.ShapeDtypeStruct((B_HEADS, 1, 128), F32)],
        scratch_shapes=[pltpu.VMEM((B_BLK, B_HD), F32), pltpu.VMEM((B_BLK, B_HD), F32)],
        name=name, compiler_params=_params(2))(q, k, v, k, v, sinks, o, lse, do)


def _split_heads(a, n):
    T = a.shape[0]
    return a.reshape(T, n, B_HD).transpose(1, 0, 2)


def _merge_heads(a):
    n, T, _ = a.shape
    return a.transpose(1, 0, 2).reshape(T, n * B_HD)


def _swa_mixer_fwd(x, nw, w_in, b_in, sinks, w_out, b_out):
    h = _rms_fwd("b_rms", x, nw)
    proj = _mm_plain("b_proj", h, w_in, 1, 0, BF16, tn=768, bias=b_in)
    q, k, v = _split_heads(proj[:, :1024], B_HEADS), _split_heads(proj[:, 1024:1280], B_KV), _split_heads(proj[:, 1280:], B_KV)
    o, lse = _swa_fwd("b_attn", q, k, v, sinks)
    om = _merge_heads(o)
    y = _mm_residual("b_out", om, w_out, x, 1.0, bias=b_out)
    return y, (h, q, k, v, o, lse, om)


def _swa_mixer_bwd(dy, dyb, x, nw, w_in, sinks, w_out, saved, out_scale):
    h, q, k, v, o, lse, om = saved
    d_om = _mm_plain("b_dout", dyb, w_out, 1, 1, BF16)
    d_wout = _mm_plain("b_dwout", om, dyb, 0, 0, F32)
    d_bout = _colsum("b_dbout", dy)
    dq, dk, dv, dsinks = _swa_bwd("b_dattn", q, k, v, sinks, o, lse, _split_heads(d_om, B_HEADS))
    dproj = jnp.concatenate([_merge_heads(dq), _merge_heads(dk), _merge_heads(dv)], axis=1)
    d_bin = _colsum("b_dbin", dproj)
    d_win = _mm_plain("b_dwin", h, dproj, 0, 0, F32, tn=768)
    dx, dxb, d_nw = _mm_rms_bwd("b_dx", dproj, w_in, x, dy, nw, 1536, out_scale)
    return dx, dxb, d_nw, d_win, d_bin, dsinks[:, 0, 0], d_wout, d_bout


def _loss_head(name, x, tgt, fw, out_scale):
    T, D = x.shape
    tt = min(T, 512)

    def body(x_ref, t_ref, w_ref, dx_ref, dxb_ref, loss_ref, dw_ref):
        xv = x_ref[...]
        r = lax.rsqrt(jnp.mean(xv * xv, axis=-1, keepdims=True) + EPS)
        xhat = xv * r
        diff = xhat * w_ref[...] - t_ref[...]
        part = 0.5 * jnp.sum(jnp.mean(diff * diff, axis=-1, keepdims=True), axis=0, keepdims=True)
        dyv = diff * (1.0 / D)
        dxhat = dyv * w_ref[...]
        dx = r * (dxhat - xhat * jnp.mean(dxhat * xhat, axis=-1, keepdims=True))
        dx_ref[...] = dx
        dxb_ref[...] = (dx * out_scale).astype(BF16)
        dw = jnp.sum(dyv * xhat, axis=0, keepdims=True)
        lp = jnp.broadcast_to(part, (1, 128))

        @pl.when(pl.program_id(0) == 0)
        def _():
            loss_ref[...] = lp
            dw_ref[...] = dw

        @pl.when(pl.program_id(0) > 0)
        def _():
            loss_ref[...] += lp
            dw_ref[...] += dw

    row = pl.BlockSpec((tt, D), lambda i: (i, 0))
    return pl.pallas_call(
        body, grid=(T // tt,), in_specs=[row, row, pl.BlockSpec((1, D), lambda i: (0, 0))],
        out_specs=[row, row, pl.BlockSpec((1, 128), lambda i: (0, 0)), pl.BlockSpec((1, D), lambda i: (0, 0))],
        out_shape=[jax.ShapeDtypeStruct((T, D), F32), jax.ShapeDtypeStruct((T, D), BF16),
                   jax.ShapeDtypeStruct((1, 128), F32), jax.ShapeDtypeStruct((1, D), F32)],
        name=name, compiler_params=_params(1))(x, tgt, fw)


def _local_step(x, tgt, wts, a4=None):
    W = wts
    g = {}
    ga, wdn = W["ga"], W["w_down"]
    n1, n2, nm = W["ffn1_norm"], W["ffn2_norm"], W["mix_norm"]
    x1, sv1 = _ffn_fwd("f10", x, n1[0:1], ga, 0, wdn[0])
    x2, sva, arrived = _gdn_fwd(x1, nm[0:1], W["a_w_in"], W["a_w_conv"], W["a_gate_p"], W["a_out_norm"], W["a_w_out"], a4)
    if a4 is not None:
        ga = _fill_a(1, a4, arrived, ga.reshape(4, 4, 2, A_HALF, FF_BLK)).reshape(ga.shape)
    x3, sv3 = _ffn_fwd("f20", x2, n2[0:1], ga, 2, wdn[2])
    x4, sv4 = _ffn_fwd("f11", x3, n1[1:2], ga, 1, wdn[1])
    x5, svb = _swa_mixer_fwd(x4, nm[1:2], W["b_w_in"], W["b_b_in"], W["b_sinks"], W["b_w_out"], W["b_b_out"])
    x6, sv6 = _ffn_fwd("f21", x5, n2[1:2], ga, 3, wdn[3])
    dx, dxb, loss_p, g["final_norm"] = _loss_head("loss_head", x6, tgt, W["final_norm"], 0.5)

    pa = jnp.zeros(ga.shape, BF16)
    dx, dxb, n21, pa, wd21 = _ffn_bwd("f21", dx, dxb, x5, n2[1:2], ga, 3, wdn[3], sv6, pa, 1.0)
    dx, dxb, nb, g["b_w_in"], g["b_b_in"], g["b_sinks"], g["b_w_out"], g["b_b_out"] = _swa_mixer_bwd(
        dx, dxb, x4, nm[1:2], W["b_w_in"], W["b_sinks"], W["b_w_out"], svb, 0.5)
    dx, dxb, n11, pa, wd11 = _ffn_bwd("f11", dx, dxb, x3, n1[1:2], ga, 1, wdn[1], sv4, pa, 0.5)
    dx, dxb, n20, pa, wd20 = _ffn_bwd("f20", dx, dxb, x2, n2[0:1], ga, 2, wdn[2], sv3, pa, 1.0)
    dx, dxb, na, g["a_w_in"], g["a_w_conv"], g["a_gate_p"], g["a_out_norm"], g["a_w_out"] = _gdn_bwd(
        dx, dxb, x1, nm[0:1], W["a_w_in"], W["a_w_conv"], W["a_gate_p"], W["a_out_norm"], W["a_w_out"], sva, 0.5)
    dx, dxb, n10, pa, wd10 = _ffn_bwd("f10", dx, dxb, x, n1[0:1], ga, 0, wdn[0], sv1, pa, 1.0)

    g["ffn1_norm"] = jnp.concatenate([n10, n11], axis=0)
    g["ffn2_norm"] = jnp.concatenate([n20, n21], axis=0)
    g["mix_norm"] = jnp.concatenate([na, nb], axis=0)
    g["ga"] = pa
    g["w_down"] = jnp.stack([wd10, wd11, wd20, wd21])
    return loss_p, dx, g


A_ROWS = 4 * D_MODEL
PACK = (("ffn1_w_down", 1408), ("ffn2_w_down", 1408), ("a_w_in", 1028), ("a_w_out", 256), ("b_w_in", 384),
        ("b_w_out", 256))
PACK_TILE = 16
PACK_USED = sum(-(-n // PACK_TILE) * PACK_TILE for _, n in PACK)
PACK_ROWS = 4864
assert PACK_USED <= PACK_ROWS
SMALL_SHARD = (8, 512)
MOVE_ROWS = {"a": 512, "b": 608}
SUM_ROWS = {"a": 256, "b": 304}


def _mesh_pos():
    x, y, c = lax.axis_index("x"), lax.axis_index("y"), lax.axis_index("c")
    return x, y, c, [(1 - x, y), (x, 1 - y), (1 - x, 1 - y)]


def _half(rows, c):
    return pl.ds(pl.multiple_of(c * (rows // 2), 16), rows // 2)


A_HALF = D_MODEL // 2


def _src_chip(j):
    x, y = lax.axis_index("x"), lax.axis_index("y")
    return jnp.where(j == 0, 2 * (1 - x) + y, jnp.where(j == 1, 2 * x + 1 - y, 2 * (1 - x) + 1 - y))


def _later_blocks_start(a4_ref, ra_ref, send_sems, recv_sems):
    x, y, c, chips = _mesh_pos()
    copies = [pltpu.make_async_remote_copy(src_ref=a4_ref.at[pl.ds(1, 3), c], dst_ref=ra_ref.at[j],
                                           send_sem=send_sems.at[j], recv_sem=recv_sems.at[j],
                                           device_id=(*chip, c), device_id_type=MESH)
              for j, chip in enumerate(chips)]
    for cp in copies:
        cp.start()
    return copies


def _fill_a(phase, a4, ra, ga=None):
    nb = 1 if phase == 0 else 3
    first = 0 if phase == 0 else 1
    steps = 3 * nb
    own_tiles = 2 * nb
    ra = ra.reshape(3, nb, A_HALF, FF_BLK)

    def body(*refs):
        if phase == 0:
            r_ref, own_ref, g_ref, send_sem, recv_sem, local_sems = refs
        else:
            r_ref, own_ref, _, g_ref, send_sem, recv_sem, local_sems = refs
        x, y, c, _ = _mesh_pos()
        s = pl.program_id(0)
        j, b = s // nb, s % nb
        dst = g_ref.at[_src_chip(j), first + b, c]
        keep = pltpu.make_async_copy(r_ref.at[0, 0], dst, local_sems.at[0])
        give = pltpu.make_async_remote_copy(src_ref=r_ref.at[0, 0], dst_ref=dst, send_sem=send_sem, recv_sem=recv_sem,
                                            device_id=(x, y, 1 - c), device_id_type=MESH)
        keep.start()
        give.start()

        @pl.when(s < own_tiles)
        def _():
            own = pltpu.make_async_copy(own_ref.at[0, 0], g_ref.at[2 * x + y, first + s // 2, s % 2], local_sems.at[1])
            own.start()
            own.wait()

        give.wait_send()
        keep.wait()

        @pl.when(s == steps - 1)
        def _():
            landed = g_ref.at[pl.ds(0, 3), pl.ds(0, nb), 0]
            pltpu.make_async_remote_copy(src_ref=landed, dst_ref=landed, send_sem=send_sem, recv_sem=recv_sem,
                                         device_id=(x, y, c), device_id_type=MESH).wait_recv()

    tile = (1, 1, A_HALF, FF_BLK)
    in_specs = [pl.BlockSpec(tile, lambda s: (s // nb, s % nb, 0, 0)),
                pl.BlockSpec(tile, lambda s: (first + jnp.minimum(s, own_tiles - 1) // 2, jnp.minimum(s, own_tiles - 1) % 2, 0, 0))]
    args = [ra, a4]
    if phase == 1:
        in_specs.append(HBM)
        args.append(ga)
    return pl.pallas_call(
        body, grid=(steps,), in_specs=in_specs, out_specs=HBM,
        out_shape=jax.ShapeDtypeStruct((4, 4, 2, A_HALF, FF_BLK), BF16),
        scratch_shapes=[pltpu.SemaphoreType.DMA, pltpu.SemaphoreType.DMA, pltpu.SemaphoreType.DMA((2,))],
        input_output_aliases={2: 0} if phase == 1 else {},
        name="fill_a%d" % phase, compiler_params=_params(1))(*args)


def _gather_chips(big_a4, big_b, small):
    bufs = (big_a4, big_b, small)

    def body(a_ref, b_ref, small_ref, ra_ref, rb_ref, rs_ref, send_sems, recv_sems):
        x, y, c, chips = _mesh_pos()
        srcs = (a_ref.at[0, c], b_ref.at[_half(PACK_ROWS, c)], small_ref)
        send = []
        for j, chip in enumerate(chips):
            for n, (src, dst) in enumerate(zip(srcs, (ra_ref, rb_ref, rs_ref))):
                send.append(pltpu.make_async_remote_copy(src_ref=src, dst_ref=dst.at[j],
                                                         send_sem=send_sems.at[3 * j + n], recv_sem=recv_sems.at[3 * j + n],
                                                         device_id=(*chip, c), device_id_type=MESH))
        for cp in send:
            cp.start()
        for cp in send:
            cp.wait_recv()
        for cp in send:
            cp.wait_send()

    return pl.pallas_call(
        body, name="gather_chips", in_specs=[HBM, HBM, HBM], out_specs=[HBM, HBM, HBM],
        out_shape=[jax.ShapeDtypeStruct((3, A_HALF, FF_BLK), BF16),
                   jax.ShapeDtypeStruct((3, PACK_ROWS // 2, 1024), BF16), jax.ShapeDtypeStruct((3,) + SMALL_SHARD, F32)],
        scratch_shapes=[pltpu.SemaphoreType.DMA((9,)), pltpu.SemaphoreType.DMA((9,))])(*bufs)


def _gather_fill(tag, big, recv):
    rows_all, width = big.shape
    half, mv = rows_all // 2, MOVE_ROWS[tag]
    nt = half // mv
    own_tiles = rows_all // mv
    assert half % mv == 0 and own_tiles <= 3 * nt

    def body(recv_ref, big_ref, g_ref, send_sem, recv_sem, local_sems):
        x, y, c, chips = _mesh_pos()
        j, t = pl.program_id(0), pl.program_id(1)
        step = j * nt + t
        src_chip = jnp.where(j == 0, 2 * (1 - x) + y, jnp.where(j == 1, 2 * x + 1 - y, 2 * (1 - x) + 1 - y))
        rows = pl.ds(pl.multiple_of(c * half + t * mv, 16), mv)
        keep = pltpu.make_async_copy(recv_ref.at[0], g_ref.at[src_chip, rows], local_sems.at[0])
        give = pltpu.make_async_remote_copy(src_ref=recv_ref.at[0], dst_ref=g_ref.at[src_chip, rows],
                                            send_sem=send_sem, recv_sem=recv_sem,
                                            device_id=(x, y, 1 - c), device_id_type=MESH)
        keep.start()
        give.start()

        @pl.when(step < own_tiles)
        def _():
            own_rows = pl.ds(pl.multiple_of(step * mv, 16), mv)
            own = pltpu.make_async_copy(big_ref, g_ref.at[2 * x + y, own_rows], local_sems.at[1])
            own.start()
            own.wait()

        give.wait_send()
        keep.wait()

        @pl.when(step == 3 * nt - 1)
        def _():
            landed = g_ref.at[pl.ds(0, 3), pl.ds(0, half)]
            pltpu.make_async_remote_copy(src_ref=landed, dst_ref=landed, send_sem=send_sem, recv_sem=recv_sem,
                                         device_id=(x, y, c), device_id_type=MESH).wait_recv()

    return pl.pallas_call(
        body, grid=(3, nt),
        in_specs=[pl.BlockSpec((1, mv, width), lambda j, t: (j, t, 0)),
                  pl.BlockSpec((mv, width), lambda j, t: (jnp.minimum(j * nt + t, own_tiles - 1), 0))],
        out_specs=HBM, out_shape=jax.ShapeDtypeStruct((4, rows_all, width), BF16),
        scratch_shapes=[pltpu.SemaphoreType.DMA, pltpu.SemaphoreType.DMA, pltpu.SemaphoreType.DMA((2,))],
        name="gather_fill_" + tag, compiler_params=_params(2))(recv, big)


def _pair_send(tag, p):
    _, rows_all, width = p.shape
    half, mv = rows_all // 2, MOVE_ROWS[tag]
    nt = half // mv

    def body(p_ref, a_ref, send_sem, recv_sem):
        x, y, c, _ = _mesh_pos()
        s, t = pl.program_id(0), pl.program_id(1)
        rows = pl.ds(pl.multiple_of(t * mv, 16), mv)
        give = pltpu.make_async_remote_copy(src_ref=p_ref.at[0], dst_ref=a_ref.at[s, rows], send_sem=send_sem,
                                            recv_sem=recv_sem, device_id=(x, y, 1 - c), device_id_type=MESH)
        give.start()
        give.wait_send()

        @pl.when((s == 3) & (t == nt - 1))
        def _():
            pltpu.make_async_remote_copy(src_ref=a_ref, dst_ref=a_ref, send_sem=send_sem, recv_sem=recv_sem,
                                         device_id=(x, y, c), device_id_type=MESH).wait_recv()

    return pl.pallas_call(
        body, grid=(4, nt),
        in_specs=[pl.BlockSpec((1, mv, width), lambda s, t: (s, (1 - lax.axis_index("c")) * nt + t, 0))],
        out_specs=HBM, out_shape=jax.ShapeDtypeStruct((4, half, width), BF16),
        scratch_shapes=[pltpu.SemaphoreType.DMA, pltpu.SemaphoreType.DMA],
        name="pair_send_" + tag, compiler_params=_params(2))(p)


def _pair_sum(tag, p, a):
    _, half, width = a.shape
    sr = SUM_ROWS[tag]
    nt = half // sr
    assert half % sr == 0

    def body(p_ref, a_ref, o_ref):
        o_ref[...] = (p_ref[...].astype(F32) + a_ref[...].astype(F32)).astype(BF16)

    spec = pl.BlockSpec((1, sr, width), lambda s, t: (s, t, 0))
    return pl.pallas_call(
        body, grid=(4, nt),
        in_specs=[pl.BlockSpec((1, sr, width), lambda s, t: (s, lax.axis_index("c") * nt + t, 0)), spec],
        out_specs=spec, out_shape=jax.ShapeDtypeStruct((4, half, width), BF16),
        name="pair_sum_" + tag, compiler_params=_params(2))(p, a)


def _chip_exchange(cs_a, cs_b):
    def body(ca_ref, cb_ref, ba_ref, bb_ref, send_sems, recv_sems):
        x, y, c, chips = _mesh_pos()
        send = []
        for j, chip in enumerate(chips):
            for n, (src, dst) in enumerate(((ca_ref, ba_ref), (cb_ref, bb_ref))):
                send.append(pltpu.make_async_remote_copy(src_ref=src.at[2 * chip[0] + chip[1]], dst_ref=dst.at[j],
                                                         send_sem=send_sems.at[2 * j + n], recv_sem=recv_sems.at[2 * j + n],
                                                         device_id=(*chip, c), device_id_type=MESH))
        for cp in send:
            cp.start()
        for cp in send:
            cp.wait_recv()
        for cp in send:
            cp.wait_send()

    return pl.pallas_call(
        body, name="chip_exchange", in_specs=[HBM, HBM], out_specs=[HBM, HBM],
        out_shape=[jax.ShapeDtypeStruct((3,) + cs.shape[1:], BF16) for cs in (cs_a, cs_b)],
        scratch_shapes=[pltpu.SemaphoreType.DMA((6,)), pltpu.SemaphoreType.DMA((6,))])(cs_a, cs_b)


def _chip_sum(tag, cs, b):
    _, half, width = cs.shape
    sr = SUM_ROWS[tag]
    nt = half // sr

    def body(c_ref, b_ref, r_ref, buf, send_sems, recv_sem, local_sems):
        x, y, c, _ = _mesh_pos()
        t = pl.program_id(0)
        slot = lax.rem(t, 2)

        def copies(k, tile):
            rows = pl.ds(pl.multiple_of(c * half + tile * sr, 8), sr)
            keep = pltpu.make_async_copy(buf.at[k], r_ref.at[rows], local_sems.at[k])
            give = pltpu.make_async_remote_copy(src_ref=buf.at[k], dst_ref=r_ref.at[rows], send_sem=send_sems.at[k],
                                                recv_sem=recv_sem, device_id=(x, y, 1 - c), device_id_type=MESH)
            return keep, give

        @pl.when(t >= 2)
        def _():
            keep, give = copies(slot, t - 2)
            keep.wait()
            give.wait_send()

        buf[slot] = (c_ref[0].astype(F32) + b_ref[0].astype(F32)) + (b_ref[1].astype(F32) + b_ref[2].astype(F32))
        keep, give = copies(slot, t)
        keep.start()
        give.start()

        @pl.when(t == nt - 1)
        def _():
            for back in (1, 0):
                keep, give = copies(lax.rem(t - back, 2), t - back)
                keep.wait()
                give.wait_send()
            landed = r_ref.at[_half(2 * half, 1 - c)]
            pltpu.make_async_remote_copy(src_ref=landed, dst_ref=landed, send_sem=send_sems.at[0], recv_sem=recv_sem,
                                         device_id=(x, y, c), device_id_type=MESH).wait_recv()

    return pl.pallas_call(
        body, grid=(nt,),
        in_specs=[pl.BlockSpec((1, sr, width), lambda t: (2 * lax.axis_index("x") + lax.axis_index("y"), t, 0)),
                  pl.BlockSpec((3, sr, width), lambda t: (0, t, 0))],
        out_specs=HBM, out_shape=jax.ShapeDtypeStruct((2 * half, width), F32),
        scratch_shapes=[pltpu.VMEM((2, sr, width), F32), pltpu.SemaphoreType.DMA((2,)), pltpu.SemaphoreType.DMA,
                        pltpu.SemaphoreType.DMA((2,))],
        name="chip_sum_" + tag, compiler_params=_params(1))(cs, b)


def _reduce_scatter(p_a, p_b):
    cs_a = _pair_sum("a", p_a, _pair_send("a", p_a))
    cs_b = _pair_sum("b", p_b, _pair_send("b", p_b))
    from_a, from_b = _chip_exchange(cs_a, cs_b)
    return _chip_sum("a", cs_a, from_a), _chip_sum("b", cs_b, from_b)


SMALL_ROWS = 24


def _all_reduce_small(v):
    def body(v_ref, o_ref, all_ref, send_sems, recv_sems):
        x, y, c, _ = _mesh_pos()
        me = 4 * x + 2 * y + c
        all_ref[me] = v_ref[...]
        peers = [(x ^ ((k >> 2) & 1), y ^ ((k >> 1) & 1), c ^ (k & 1)) for k in range(1, 8)]
        idx = lambda p: 4 * p[0] + 2 * p[1] + p[2]
        send = [pltpu.make_async_remote_copy(src_ref=v_ref, dst_ref=all_ref.at[me], send_sem=send_sems.at[k],
                                             recv_sem=recv_sems.at[k], device_id=p, device_id_type=MESH)
                for k, p in enumerate(peers)]
        for cp in send:
            cp.start()
        for k, p in enumerate(peers):
            pltpu.make_async_remote_copy(src_ref=v_ref, dst_ref=all_ref.at[idx(p)], send_sem=send_sems.at[k],
                                         recv_sem=recv_sems.at[k], device_id=p, device_id_type=MESH).wait_recv()
        for cp in send:
            cp.wait_send()
        acc = all_ref[0]
        for d in range(1, 8):
            acc = acc + all_ref[d]
        o_ref[...] = acc

    vm = pl.BlockSpec(memory_space=pltpu.VMEM)
    return pl.pallas_call(
        body, name="all_reduce_small", in_specs=[vm], out_specs=vm,
        out_shape=jax.ShapeDtypeStruct((SMALL_ROWS, 1024), F32),
        scratch_shapes=[pltpu.VMEM((8, SMALL_ROWS, 1024), F32), pltpu.SemaphoreType.DMA((7,)),
                        pltpu.SemaphoreType.DMA((7,))],)(v)


def _adamw(name, w, g, m, v):
    rows, cols = w.shape
    tr = rows
    if rows * cols > 400_000:
        tr = max(t for t in range(8, rows, 8) if rows % t == 0 and t * cols <= 400_000)

    def body(w_ref, g_ref, m_ref, v_ref, d_ref, nm_ref, nv_ref):
        gv = g_ref[...]
        m_new = ADAM_B1 * m_ref[...] + (1.0 - ADAM_B1) * gv
        v_new = ADAM_B2 * v_ref[...] + (1.0 - ADAM_B2) * (gv * gv)
        m_hat = m_new / (1.0 - ADAM_B1 ** ADAM_STEP)
        v_hat = v_new / (1.0 - ADAM_B2 ** ADAM_STEP)
        d_ref[...] = -ADAM_LR * (m_hat / (jnp.sqrt(v_hat) + ADAM_EPS) + ADAM_WD * w_ref[...])
        nm_ref[...] = m_new
        nv_ref[...] = v_new

    spec = pl.BlockSpec((tr, cols), lambda i: (i, 0))
    sds = jax.ShapeDtypeStruct((rows, cols), F32)
    return pl.pallas_call(body, grid=(rows // tr,), in_specs=[spec] * 4, out_specs=[spec] * 3, out_shape=[sds] * 3,
                          name=name, compiler_params=_params(1))(w, g, m, v)


WEIGHTS = ("ffn1_norm", "ffn1_w_gu", "ffn1_w_down", "mix_norm", "ffn2_norm", "ffn2_w_gu", "ffn2_w_down",
           "a_w_in", "a_w_conv", "a_A_log", "a_dt_bias", "a_out_norm", "a_w_out",
           "b_w_in", "b_b_in", "b_sinks", "b_w_out", "b_b_out", "final_norm")
SMALL_SLOTS = {"ffn1_norm": (0, 2048), "mix_norm": (2048, 2048), "ffn2_norm": (4096, 2048), "final_norm": (6144, 1024),
               "a_A_log": (7168, 8), "a_dt_bias": (7296, 8), "a_out_norm": (7424, 128), "b_sinks": (7552, 16),
               "loss": (7680, 1)}
SMALL_SHARDED = {"a_w_conv": (8192, 8192, (4,), 3072), "b_b_in": (20480, 11264, (), 1536), "b_b_out": (22016, 11648, (), 1024)}
DEV_SMALL_ROWS = 12


def _pack_rows(parts):
    rows = []
    for p in parts:
        r = p.reshape(p.shape[0], -1, 1024)
        rows.append(jnp.pad(r, ((0, 0), (0, -r.shape[1] % PACK_TILE), (0, 0))))
    rows.append(jnp.zeros((parts[0].shape[0], PACK_ROWS - PACK_USED, 1024), parts[0].dtype))
    return jnp.concatenate(rows, axis=1)


def _place(vec, off, a):
    return lax.dynamic_update_slice(vec, a.reshape(-1).astype(F32), (off,))


def kernel(x, ffn1_norm, ffn1_w_gu, ffn1_w_down, mix_norm, ffn2_norm, ffn2_w_gu, ffn2_w_down, a_w_in, a_w_conv, a_A_log, a_dt_bias, a_out_norm, a_w_out, b_w_in, b_b_in, b_sinks, b_w_out, b_b_out, final_norm, loss_target, m_ffn1_norm, m_ffn1_w_gu, m_ffn1_w_down, m_mix_norm, m_ffn2_norm, m_ffn2_w_gu, m_ffn2_w_down, m_a_w_in, m_a_w_conv, m_a_A_log, m_a_dt_bias, m_a_out_norm, m_a_w_out, m_b_w_in, m_b_b_in, m_b_sinks, m_b_w_out, m_b_b_out, m_final_norm, v_ffn1_norm, v_ffn1_w_gu, v_ffn1_w_down, v_mix_norm, v_ffn2_norm, v_ffn2_w_gu, v_ffn2_w_down, v_a_w_in, v_a_w_conv, v_a_A_log, v_a_dt_bias, v_a_out_norm, v_a_w_out, v_b_w_in, v_b_b_in, v_b_sinks, v_b_w_out, v_b_b_out, v_final_norm):
    w = dict(zip(WEIGHTS, (ffn1_norm, ffn1_w_gu, ffn1_w_down, mix_norm, ffn2_norm, ffn2_w_gu, ffn2_w_down, a_w_in, a_w_conv,
                           a_A_log, a_dt_bias, a_out_norm, a_w_out, b_w_in, b_b_in, b_sinks, b_w_out, b_b_out, final_norm)))
    m = dict(zip(WEIGHTS, (m_ffn1_norm, m_ffn1_w_gu, m_ffn1_w_down, m_mix_norm, m_ffn2_norm, m_ffn2_w_gu, m_ffn2_w_down,
                           m_a_w_in, m_a_w_conv, m_a_A_log, m_a_dt_bias, m_a_out_norm, m_a_w_out, m_b_w_in, m_b_b_in,
                           m_b_sinks, m_b_w_out, m_b_b_out, m_final_norm)))
    v = dict(zip(WEIGHTS, (v_ffn1_norm, v_ffn1_w_gu, v_ffn1_w_down, v_mix_norm, v_ffn2_norm, v_ffn2_w_gu, v_ffn2_w_down,
                           v_a_w_in, v_a_w_conv, v_a_A_log, v_a_dt_bias, v_a_out_norm, v_a_w_out, v_b_w_in, v_b_b_in,
                           v_b_sinks, v_b_w_out, v_b_b_out, v_final_norm)))
    chip = 2 * lax.axis_index("x") + lax.axis_index("y")

    big_a4 = jnp.concatenate([w["ffn1_w_gu"], w["ffn2_w_gu"]], axis=0).astype(BF16).reshape(4, 2, A_HALF, FF_BLK)
    big_b = _pack_rows([w[n].astype(BF16).reshape(1, -1) for n, _ in PACK])[0]
    small = jnp.zeros((4096,), F32)
    small = _place(small, 0, w["a_w_conv"])
    small = _place(small, 3072, w["b_b_in"])
    small = _place(small, 3456, w["b_b_out"]).reshape(SMALL_SHARD)
    ra, rb, rs = _gather_chips(big_a4, big_b, small)
    ga = _fill_a(0, big_a4, ra).reshape(4, 4, D_MODEL, FF_BLK)
    gb = _gather_fill("b", big_b, rb)
    offs, o = {}, 0
    for n, r in PACK:
        offs[n] = (o, r)
        o += -(-r // PACK_TILE) * PACK_TILE
    blk = lambda n: gb[:, offs[n][0]:offs[n][0] + offs[n][1]]
    gsf = lax.dynamic_update_slice(jnp.zeros((4, 4096), F32), small.reshape(1, 4096), (chip, 0))
    for j, other in enumerate((chip ^ 2, chip ^ 1, chip ^ 3)):
        gsf = lax.dynamic_update_slice(gsf, rs[j].reshape(1, 4096), (other, 0))
    W = {n: w[n] for n in ("ffn1_norm", "ffn2_norm", "mix_norm", "a_out_norm")}
    W["ga"] = ga
    W["w_down"] = gb[:, 0:2816].reshape(4, 4, 704, 1024).transpose(1, 0, 2, 3).reshape(4, 2, FF_BLK, 1024)
    W["a_w_in"] = jnp.pad(blk("a_w_in").reshape(4, 1024, 1028).transpose(1, 0, 2).reshape(1024, A_IN_COLS),
                          ((0, 0), (0, A_IN_PAD - A_IN_COLS)))
    W["a_w_out"] = blk("a_w_out").reshape(1024, 1024)
    W["b_w_in"] = blk("b_w_in").reshape(4, 1024, 384).transpose(1, 0, 2).reshape(1024, 1536)
    W["b_w_out"] = blk("b_w_out").reshape(1024, 1024)
    W["a_w_conv"] = gsf[:, 0:3072].reshape(4, 4, 768).transpose(1, 0, 2).reshape(4, 3072)
    W["b_b_in"] = gsf[:, 3072:3456].reshape(1, 1536)
    W["b_b_out"] = gsf[:, 3456:3712].reshape(1, 1024)
    W["a_gate_p"] = jnp.pad(jnp.concatenate([-jnp.exp(w["a_A_log"]), w["a_dt_bias"]], axis=0), ((0, 0), (8, 112)))
    W["b_sinks"] = jnp.broadcast_to(w["b_sinks"][0][:, None, None], (B_HEADS, 1, 128))
    W["final_norm"] = w["final_norm"][None]

    loss_p, dx, g = _local_step(x[0], loss_target[0], W, big_a4)

    down = g["w_down"].reshape(4, 4, 704, 1024).transpose(1, 0, 2, 3)
    parts = [down[:, 0:2], down[:, 2:4],
             g["a_w_in"][:, :A_IN_COLS].reshape(1024, 4, 1028).transpose(1, 0, 2), g["a_w_out"].reshape(4, 256, 1024),
             g["b_w_in"].reshape(1024, 4, 384).transpose(1, 0, 2), g["b_w_out"].reshape(4, 256, 1024)]
    red_a, red_b = _reduce_scatter(g["ga"].reshape(4, A_ROWS, FF_BLK),
                                   _pack_rows([a.astype(BF16).reshape(4, -1) for a in parts]))
    grads = {n: red_b[offs[n][0]:offs[n][0] + offs[n][1]].reshape(w[n].shape) for n, _ in PACK}
    grads["ffn1_w_gu"] = red_a[:A_ROWS // 2].reshape(w["ffn1_w_gu"].shape)
    grads["ffn2_w_gu"] = red_a[A_ROWS // 2:].reshape(w["ffn2_w_gu"].shape)

    sv = jnp.zeros((SMALL_ROWS * 1024,), F32)
    small_g = {"ffn1_norm": g["ffn1_norm"], "mix_norm": g["mix_norm"], "ffn2_norm": g["ffn2_norm"], "final_norm": g["final_norm"],
               "a_A_log": g["a_gate_p"][0, 8:16], "a_dt_bias": g["a_gate_p"][1, 8:16], "a_out_norm": g["a_out_norm"],
               "b_sinks": g["b_sinks"], "loss": loss_p[0, 0:1]}
    for n, (off, _) in SMALL_SLOTS.items():
        sv = _place(sv, off, small_g[n])
    for n, (off, _, _, _) in SMALL_SHARDED.items():
        sv = _place(sv, off, g[n])
    tot = _all_reduce_small(sv.reshape(SMALL_ROWS, 1024)).reshape(-1)
    for n, (off, size) in SMALL_SLOTS.items():
        if n != "loss":
            grads[n] = tot[off:off + size].reshape(w[n].shape)
    for n, (off, _, lead, last) in SMALL_SHARDED.items():
        full = tot[off:off + (lead[0] if lead else 1) * last].reshape(lead + (last,))
        width = last // 4
        grads[n] = lax.dynamic_slice_in_dim(full, chip * width, width, axis=-1).reshape(w[n].shape)
    loss = tot[SMALL_SLOTS["loss"][0]]

    delta, new_m, new_v = {}, {}, {}
    for n in ("ffn1_w_gu", "ffn2_w_gu") + tuple(n for n, _ in PACK):
        two_d = lambda a: a.reshape(-1, a.shape[-1])
        d, nm, nv = _adamw("adamw_" + n, two_d(w[n]), two_d(grads[n]), two_d(m[n]), two_d(v[n]))
        delta[n], new_m[n], new_v[n] = d.reshape(w[n].shape), nm.reshape(w[n].shape), nv.reshape(w[n].shape)

    def dev_small(src):
        vec = jnp.zeros((DEV_SMALL_ROWS * 1024,), F32)
        for n, (off, _) in SMALL_SLOTS.items():
            if n != "loss":
                vec = _place(vec, off, src[n])
        for n, (_, off, _, _) in SMALL_SHARDED.items():
            vec = _place(vec, off, src[n])
        return vec.reshape(DEV_SMALL_ROWS, 1024)

    sd, sm, svv = _adamw("adamw_small", dev_small(w), dev_small(grads), dev_small(m), dev_small(v))
    for n in WEIGHTS:
        if n in SMALL_SLOTS:
            off, size = SMALL_SLOTS[n]
        elif n in SMALL_SHARDED:
            off, size = SMALL_SHARDED[n][1], w[n].size
        else:
            continue
        for dst, src in ((delta, sd), (new_m, sm), (new_v, svv)):
            dst[n] = src.reshape(-1)[off:off + size].reshape(w[n].shape)

    return (loss, dx[None], *[grads[n] for n in WEIGHTS], *[delta[n] for n in WEIGHTS],
            *[new_m[n] for n in WEIGHTS], *[new_v[n] for n in WEIGHTS])
```

```python
import jax
import jax.numpy as jnp
from jax import lax
from jax.experimental import pallas as pl
from jax.experimental.pallas import tpu as pltpu

F32 = jnp.float32
BF16 = jnp.bfloat16

D_MODEL = 1024
EPS = 1e-6
FF_BLK = 1408
A_HEADS = 8
A_DK = 128
A_CHUNK = 64
A_HG = 8
A_IN_COLS = 4112
A_IN_PAD = 4224
B_HEADS = 16
B_KV = 4
B_HD = 64
B_BLK = 128
ADAM_LR, ADAM_B1, ADAM_B2, ADAM_EPS, ADAM_WD, ADAM_STEP = 0.001, 0.9, 0.999, 1e-08, 0.01, 10
MESH = pl.DeviceIdType.MESH
VMEM_LIMIT = 56 * 1024 * 1024
HBM = pl.BlockSpec(memory_space=pl.ANY)


def _params(n_axes):
    return pltpu.CompilerParams(dimension_semantics=("arbitrary",) * n_axes, vmem_limit_bytes=VMEM_LIMIT)


def _sigmoid(x):
    return 1.0 / (1.0 + jnp.exp(-x))


def _dot(a, b, ca, cb):
    return lax.dot_general(a, b, (((ca,), (cb,)), ((), ())), preferred_element_type=F32)


def _dotb(a, b, ca=1, cb=0):
    return _dot(a.astype(BF16), b.astype(BF16), ca, cb)


def _dotx(a, b, ca=1, cb=0):
    return lax.dot_general(a, b, (((ca,), (cb,)), ((), ())), preferred_element_type=F32,
                           precision=lax.Precision.HIGHEST)


def _doth(a, b, ca=1, cb=0):
    return lax.dot_general(a, b, (((ca,), (cb,)), ((), ())), preferred_element_type=F32,
                           precision=lax.Precision.HIGH)


def _rms_fwd(name, x, w):
    T, D = x.shape
    tt = min(T, 512)

    def body(x_ref, w_ref, h_ref):
        xv = x_ref[...]
        r = lax.rsqrt(jnp.mean(xv * xv, axis=-1, keepdims=True) + EPS)
        h_ref[...] = (xv * r * w_ref[...]).astype(BF16)

    return pl.pallas_call(
        body, grid=(T // tt,),
        in_specs=[pl.BlockSpec((tt, D), lambda i: (i, 0)), pl.BlockSpec((1, D), lambda i: (0, 0))],
        out_specs=pl.BlockSpec((tt, D), lambda i: (i, 0)),
        out_shape=jax.ShapeDtypeStruct((T, D), BF16), name=name, compiler_params=_params(1))(x, w)


def _rms_bwd_tile(dh, xv, dy, w):
    r = lax.rsqrt(jnp.mean(xv * xv, axis=-1, keepdims=True) + EPS)
    xhat = xv * r
    dxhat = dh * w
    dx = dy + r * (dxhat - xhat * jnp.mean(dxhat * xhat, axis=-1, keepdims=True))
    return dx, jnp.sum(dh * xhat, axis=0, keepdims=True)


def _colsum(name, a):
    T, N = a.shape
    tt = min(T, 512)

    def body(a_ref, o_ref):
        @pl.when(pl.program_id(0) == 0)
        def _():
            o_ref[...] = jnp.zeros_like(o_ref)
        o_ref[...] += jnp.sum(a_ref[...].astype(F32), axis=0, keepdims=True)

    return pl.pallas_call(
        body, grid=(T // tt,), in_specs=[pl.BlockSpec((tt, N), lambda i: (i, 0))],
        out_specs=pl.BlockSpec((1, N), lambda i: (0, 0)),
        out_shape=jax.ShapeDtypeStruct((1, N), F32), name=name, compiler_params=_params(1))(a)


def _matmul(name, a, b, ca, cb, tm, tn, tk, extra_in, outs, epi, order="ji"):
    M, K, N = a.shape[1 - ca], a.shape[ca], b.shape[1 - cb]
    tm, tn, tk = min(tm, M), min(tn, N), min(tk, K)
    assert M % tm == 0 and N % tn == 0 and K % tk == 0, (name, M, N, K, tm, tn, tk)
    ni, nj, nk = M // tm, N // tn, K // tk
    if order == "ji":
        grid = (nj, ni, nk)
        perm = lambda g0, g1, g2: (g1, g0, g2)
    else:
        grid = (ni, nj, nk)
        perm = lambda g0, g1, g2: (g0, g1, g2)

    def wrap(f):
        return lambda g0, g1, g2: f(*perm(g0, g1, g2))

    a_spec = (pl.BlockSpec((tm, tk), wrap(lambda i, j, k: (i, k))) if ca == 1
              else pl.BlockSpec((tk, tm), wrap(lambda i, j, k: (k, i))))
    b_spec = (pl.BlockSpec((tk, tn), wrap(lambda i, j, k: (k, j))) if cb == 0
              else pl.BlockSpec((tn, tk), wrap(lambda i, j, k: (j, k))))
    ne, no = len(extra_in), len(outs)

    def body(*refs):
        a_ref, b_ref = refs[0], refs[1]
        ex, out = refs[2:2 + ne], refs[2 + ne:2 + ne + no]
        i, j, k = perm(pl.program_id(0), pl.program_id(1), pl.program_id(2))
        p = _dotb(a_ref[...], b_ref[...], ca, cb)
        if nk == 1:
            epi(p, ex, out, i, j)
        else:
            acc_ref = refs[-1]

            @pl.when(k == 0)
            def _():
                acc_ref[...] = p

            @pl.when(k > 0)
            def _():
                acc_ref[...] += p

            @pl.when(k == nk - 1)
            def _():
                epi(acc_ref[...], ex, out, i, j)

    return pl.pallas_call(
        body, grid=grid,
        in_specs=[a_spec, b_spec] + [pl.BlockSpec(bs, wrap(f)) for _, bs, f in extra_in],
        out_specs=[pl.BlockSpec(bs, wrap(f)) for _, bs, f in outs],
        out_shape=[s for s, _, _ in outs],
        scratch_shapes=[pltpu.VMEM((tm, tn), F32)] if nk > 1 else [],
        name=name, compiler_params=_params(3))(a, b, *[x for x, _, _ in extra_in])


def _mm_plain(name, a, b, ca, cb, out_dtype, tm=1024, tn=1024, tk=1024, scale=1.0, bias=None):
    M, N = a.shape[1 - ca], b.shape[1 - cb]
    tm, tn = min(tm, M), min(tn, N)
    extra = [] if bias is None else [(bias, (1, tn), lambda i, j, k: (0, j))]

    def epi(acc, ex, out, i, j):
        r = acc * scale if scale != 1.0 else acc
        if bias is not None:
            r = r + ex[0][...]
        out[0][...] = r.astype(out_dtype)

    return _matmul(name, a, b, ca, cb, tm, tn, tk, extra,
                   [(jax.ShapeDtypeStruct((M, N), out_dtype), (tm, tn), lambda i, j, k: (i, j))], epi)[0]


def _mm_residual(name, a, b, x, scale, bias=None, tk=1024):
    M, N = x.shape
    tm, tn = min(512, M), N
    extra = [(x, (tm, tn), lambda i, j, k: (i, j))]
    if bias is not None:
        extra.append((bias, (1, tn), lambda i, j, k: (0, j)))

    def epi(acc, ex, out, i, j):
        r = acc if bias is None else acc + ex[1][...]
        out[0][...] = ex[0][...] + scale * r

    return _matmul(name, a, b, 1, 0, tm, tn, tk, extra,
                   [(jax.ShapeDtypeStruct((M, N), F32), (tm, tn), lambda i, j, k: (i, j))], epi, order="ij")[0]


def _mm_rms_bwd(name, dproj, w_in, x, dy, nw, tk, out_scale):
    M, N = x.shape
    tm = min(512, M)
    extra = [(x, (tm, N), lambda i, j, k: (i, 0)), (dy, (tm, N), lambda i, j, k: (i, 0)),
             (nw, (1, N), lambda i, j, k: (0, 0))]

    def epi(acc, ex, out, i, j):
        dx, dw = _rms_bwd_tile(acc, ex[0][...], ex[1][...], ex[2][...])
        out[0][...] = dx
        out[1][...] = (dx * out_scale).astype(BF16)

        @pl.when(i == 0)
        def _():
            out[2][...] = dw

        @pl.when(i > 0)
        def _():
            out[2][...] += dw

    return _matmul(name, dproj, w_in, 1, 1, tm, N, tk, extra,
                   [(jax.ShapeDtypeStruct((M, N), F32), (tm, N), lambda i, j, k: (i, 0)),
                    (jax.ShapeDtypeStruct((M, N), BF16), (tm, N), lambda i, j, k: (i, 0)),
                    (jax.ShapeDtypeStruct((1, N), F32), (1, N), lambda i, j, k: (0, 0))], epi, order="ij")


def _ffn_gu(name, x, nw, ga, blk):
    T, D = x.shape
    tm = min(T, 512)
    rs = min(tm, 256)

    def body(x_ref, nw_ref, wg0, wg1, wu0, wu1, h_ref, gu_ref, act_ref):
        for r in range(tm // rs):
            rows = pl.ds(r * rs, rs)
            xv = x_ref[rows, :]
            hv = (xv * lax.rsqrt(jnp.mean(xv * xv, axis=-1, keepdims=True) + EPS) * nw_ref[...]).astype(BF16)
            h_ref[rows, :] = hv
            for j, (wg_ref, wu_ref) in enumerate(((wg0, wu0), (wg1, wu1))):
                g = _dot(hv, wg_ref[0, 0], 1, 0)
                u = _dot(hv, wu_ref[0, 0], 1, 0)
                s = _sigmoid(g)
                gs = g * s
                gu_ref[0, j, rows, :] = (u * (s + gs * (1.0 - s))).astype(BF16)
                gu_ref[1, j, rows, :] = gs.astype(BF16)
                act_ref[j, rows, :] = (gs * u).astype(BF16)

    wspec = lambda q: pl.BlockSpec((1, 1, D, FF_BLK), lambda i: (q, blk, 0, 0), pipeline_mode=pl.Buffered(1))
    return pl.pallas_call(
        body, grid=(T // tm,),
        in_specs=[pl.BlockSpec((tm, D), lambda i: (i, 0)), pl.BlockSpec((1, D), lambda i: (0, 0)),
                  wspec(0), wspec(1), wspec(2), wspec(3)],
        out_specs=[pl.BlockSpec((tm, D), lambda i: (i, 0)),
                   pl.BlockSpec((2, 2, tm, FF_BLK), lambda i: (0, 0, i, 0)),
                   pl.BlockSpec((2, tm, FF_BLK), lambda i: (0, i, 0))],
        out_shape=[jax.ShapeDtypeStruct((T, D), BF16), jax.ShapeDtypeStruct((2, 2, T, FF_BLK), BF16),
                   jax.ShapeDtypeStruct((2, T, FF_BLK), BF16)],
        name=name, compiler_params=_params(1))(x, nw, ga, ga, ga, ga)


def _ffn_down(name, act, wd, x):
    T, D = x.shape
    tm = min(T, 512)

    def body(act_ref, wd_ref, x_ref, o_ref):
        acc = _dot(act_ref[0], wd_ref[0], 1, 0) + _dot(act_ref[1], wd_ref[1], 1, 0)
        o_ref[...] = x_ref[...] + 0.5 * acc

    return pl.pallas_call(
        body, grid=(T // tm,),
        in_specs=[pl.BlockSpec((2, tm, FF_BLK), lambda i: (0, i, 0)),
                  pl.BlockSpec((2, FF_BLK, D), lambda i: (0, 0, 0)),
                  pl.BlockSpec((tm, D), lambda i: (i, 0))],
        out_specs=pl.BlockSpec((tm, D), lambda i: (i, 0)),
        out_shape=jax.ShapeDtypeStruct((T, D), F32), name=name, compiler_params=_params(1))(act, wd, x)


def _ffn_dact(name, dyh, wd, gu):
    T, D = dyh.shape
    tm = min(T, 1024)
    rs = min(tm, 256)

    def body(dy_ref, wd_ref, gu_ref, o_ref):
        for r in range(tm // rs):
            rows = pl.ds(r * rs, rs)
            dact = _dot(dy_ref[rows, :], wd_ref[0], 1, 1)
            o_ref[0, 0, rows, :] = (dact * gu_ref[0, 0, rows, :].astype(F32)).astype(BF16)
            o_ref[1, 0, rows, :] = (dact * gu_ref[1, 0, rows, :].astype(F32)).astype(BF16)

    return pl.pallas_call(
        body, grid=(2, T // tm),
        in_specs=[pl.BlockSpec((tm, D), lambda j, i: (i, 0)),
                  pl.BlockSpec((1, FF_BLK, D), lambda j, i: (j, 0, 0)),
                  pl.BlockSpec((2, 1, tm, FF_BLK), lambda j, i: (0, j, i, 0))],
        out_specs=pl.BlockSpec((2, 1, tm, FF_BLK), lambda j, i: (0, j, i, 0)),
        out_shape=jax.ShapeDtypeStruct((2, 2, T, FF_BLK), BF16), name=name, compiler_params=_params(2))(dyh, wd, gu)


def _ffn_dwd(name, act, dyh):
    _, T, _ = act.shape
    D = dyh.shape[1]
    tk = min(T, 2048)
    nk = T // tk

    def body(a_ref, d_ref, o_ref, acc_ref):
        k = pl.program_id(1)
        p = _dot(a_ref[0], d_ref[...], 0, 0)

        @pl.when(k == 0)
        def _():
            acc_ref[...] = p

        @pl.when(k > 0)
        def _():
            acc_ref[...] += p

        @pl.when(k == nk - 1)
        def _():
            o_ref[0] = acc_ref[...].astype(BF16)

    return pl.pallas_call(
        body, grid=(2, nk),
        in_specs=[pl.BlockSpec((1, tk, FF_BLK), lambda j, k: (j, k, 0)), pl.BlockSpec((tk, D), lambda j, k: (k, 0))],
        out_specs=pl.BlockSpec((1, FF_BLK, D), lambda j, k: (j, 0, 0)),
        out_shape=jax.ShapeDtypeStruct((2, FF_BLK, D), BF16), scratch_shapes=[pltpu.VMEM((FF_BLK, D), F32)],
        name=name, compiler_params=_params(2))(act, dyh)


def _ffn_dwgu(name, h, dgu, pa, blk):
    T, D = h.shape
    tk = min(T, 2048)
    nk = T // tk

    def body(h_ref, d_ref, pa_in, o_ref, acc_ref):
        k = pl.program_id(1)
        p = _dot(h_ref[...], d_ref[0, 0], 0, 0)

        @pl.when(k == 0)
        def _():
            acc_ref[...] = p

        @pl.when(k > 0)
        def _():
            acc_ref[...] += p

        @pl.when(k == nk - 1)
        def _():
            o_ref[0, 0] = acc_ref[...].astype(BF16)

    return pl.pallas_call(
        body, grid=(4, nk),
        in_specs=[pl.BlockSpec((tk, D), lambda q, k: (k, 0)),
                  pl.BlockSpec((1, 1, tk, FF_BLK), lambda q, k: (q // 2, q % 2, k, 0)), HBM],
        out_specs=pl.BlockSpec((1, 1, D, FF_BLK), lambda q, k: (q, blk, 0, 0)),
        out_shape=jax.ShapeDtypeStruct(pa.shape, BF16), scratch_shapes=[pltpu.VMEM((D, FF_BLK), F32)],
        input_output_aliases={2: 0}, name=name, compiler_params=_params(2))(h, dgu, pa)


def _ffn_dx(name, dgu, ga, blk, x, dy, nw, out_scale):
    T, D = x.shape
    tm = min(T, 512)

    def body(d_ref, w0, w1, w2, w3, x_ref, dy_ref, nw_ref, dx_ref, dxb_ref, dnw_ref):
        i = pl.program_id(0)
        acc = (_dot(d_ref[0, 0], w0[0, 0], 1, 1) + _dot(d_ref[0, 1], w1[0, 0], 1, 1)
               + _dot(d_ref[1, 0], w2[0, 0], 1, 1) + _dot(d_ref[1, 1], w3[0, 0], 1, 1))
        dx, dw = _rms_bwd_tile(acc, x_ref[...], dy_ref[...], nw_ref[...])
        dx_ref[...] = dx
        dxb_ref[...] = (dx * out_scale).astype(BF16)

        @pl.when(i == 0)
        def _():
            dnw_ref[...] = dw

        @pl.when(i > 0)
        def _():
            dnw_ref[...] += dw

    wspec = lambda q: pl.BlockSpec((1, 1, D, FF_BLK), lambda i: (q, blk, 0, 0), pipeline_mode=pl.Buffered(1))
    row = pl.BlockSpec((tm, D), lambda i: (i, 0))
    return pl.pallas_call(
        body, grid=(T // tm,),
        in_specs=[pl.BlockSpec((2, 2, tm, FF_BLK), lambda i: (0, 0, i, 0)), wspec(0), wspec(1), wspec(2), wspec(3),
                  row, row, pl.BlockSpec((1, D), lambda i: (0, 0))],
        out_specs=[row, row, pl.BlockSpec((1, D), lambda i: (0, 0))],
        out_shape=[jax.ShapeDtypeStruct((T, D), F32), jax.ShapeDtypeStruct((T, D), BF16),
                   jax.ShapeDtypeStruct((1, D), F32)],
        name=name, compiler_params=_params(1))(dgu, ga, ga, ga, ga, x, dy, nw)


def _ffn_fwd(tag, x, nw, ga, blk, wd):
    h, gu, act = _ffn_gu(tag + "_gu", x, nw, ga, blk)
    return _ffn_down(tag + "_down", act, wd, x), (h, gu, act)


def _ffn_bwd(tag, dy, dyh, x, nw, ga, blk, wd, saved, pa, out_scale):
    h, gu, act = saved
    dgu = _ffn_dact(tag + "_dact", dyh, wd, gu)
    d_wd = _ffn_dwd(tag + "_dwd", act, dyh)
    pa = _ffn_dwgu(tag + "_dwgu", h, dgu, pa, blk)
    dx, dxb, d_nw = _ffn_dx(tag + "_dx", dgu, ga, blk, x, dy, nw, out_scale)
    return dx, dxb, d_nw, pa, d_wd


def _conv_taps(cur, halo, w, first, sign):
    tt = cur.shape[0]
    halo = jnp.where(first, 0.0, halo)
    rid = lax.broadcasted_iota(jnp.int32, (8, cur.shape[1]), 0)
    acc = w[3:4, :] * cur
    for s in (1, 2, 3):
        if sign < 0:
            sh = pltpu.roll(cur, s, 0)
            edge = jnp.where(rid < s, pltpu.roll(halo, s, 0), sh[0:8])
            sh = jnp.concatenate([edge, sh[8:]], axis=0) if tt > 8 else edge
        else:
            sh = pltpu.roll(cur, tt - s, 0)
            edge = jnp.where(rid >= 8 - s, pltpu.roll(halo, 8 - s, 0), sh[tt - 8:])
            sh = jnp.concatenate([sh[:tt - 8], edge], axis=0) if tt > 8 else edge
        acc = acc + w[3 - s:4 - s, :] * sh
    return acc


def _gdn_prep(name, proj, wconv, gate_p):
    T = proj.shape[0]
    tt = min(T, 256)
    hb = tt // 8
    nch = tt // A_CHUNK

    def body(cur_ref, halo_ref, ba_ref, w_ref, gp_ref, qkv_ref, bg_ref, gc_ref):
        first = pl.program_id(0) == 0
        for c in range(24):
            cols = pl.ds(c * 128, 128)
            conv = _conv_taps(cur_ref[:, cols], halo_ref[:, cols], w_ref[:, cols], first, -1)
            y = conv * _sigmoid(conv)
            if c < 16:
                y = y * lax.rsqrt(jnp.sum(y * y, axis=-1, keepdims=True) + EPS)
                if c < 8:
                    y = y * (A_DK ** -0.5)
            qkv_ref[:, cols] = y
        ba = ba_ref[...]
        lane = lax.broadcasted_iota(jnp.int32, ba.shape, 1)
        zarg = ba + gp_ref[1:2, :]
        softplus = jnp.maximum(zarg, 0.0) + jnp.log(1.0 + jnp.exp(-jnp.abs(zarg)))
        bg = jnp.where(lane < 8, _sigmoid(ba), jnp.where(lane < 16, gp_ref[0:1, :] * softplus, 0.0))
        bg_ref[...] = bg
        tri = (lax.broadcasted_iota(jnp.int32, (A_CHUNK, A_CHUNK), 0)
               >= lax.broadcasted_iota(jnp.int32, (A_CHUNK, A_CHUNK), 1)).astype(F32)
        for c in range(nch):
            rows = pl.ds(c * A_CHUNK, A_CHUNK)
            gc_ref[rows, :] = _dotx(tri, bg[c * A_CHUNK:(c + 1) * A_CHUNK, :])

    return pl.pallas_call(
        body, grid=(T // tt,),
        in_specs=[pl.BlockSpec((tt, 3072), lambda i: (i, 0)),
                  pl.BlockSpec((8, 3072), lambda i: (jnp.maximum(i * hb - 1, 0), 0)),
                  pl.BlockSpec((tt, 128), lambda i: (i, 32)),
                  pl.BlockSpec((4, 3072), lambda i: (0, 0)),
                  pl.BlockSpec((2, 128), lambda i: (0, 0))],
        out_specs=[pl.BlockSpec((tt, 3072), lambda i: (i, 0)), pl.BlockSpec((tt, 128), lambda i: (i, 0)),
                   pl.BlockSpec((tt, 128), lambda i: (i, 0))],
        out_shape=[jax.ShapeDtypeStruct((T, 3072), F32), jax.ShapeDtypeStruct((T, 128), F32),
                   jax.ShapeDtypeStruct((T, 128), F32)],
        name=name, compiler_params=_params(1))(proj, proj, proj, wconv, gate_p)


def _chunk_masks():
    ri = lax.broadcasted_iota(jnp.int32, (A_CHUNK, A_CHUNK), 0)
    ci = lax.broadcasted_iota(jnp.int32, (A_CHUNK, A_CHUNK), 1)
    return ri >= ci, ri > ci, ri == ci


def _chunk_local(q, k, gcol, grow, bcol):
    incl, strict, _ = _chunk_masks()
    dec = jnp.where(incl, jnp.exp(jnp.where(incl, gcol - grow, 0.0)), 0.0)
    e = jnp.exp(gcol)
    glast = grow[:, A_CHUNK - 1:A_CHUNK]
    f = jnp.exp(glast - gcol)
    gl = jnp.exp(glast)
    kb = k * bcol
    lmat = jnp.where(strict, _dotb(kb, k, 1, 1) * dec, 0.0)
    amat = jnp.where(incl, _dotb(q, k, 1, 1) * dec, 0.0)
    return dec, e, f, gl, kb, lmat, amat


def _unit_lower_inverse(lmats):
    _, _, eye = _chunk_masks()
    ts = [jnp.where(eye, 1.0, 0.0) - lm for lm in lmats]
    lps = [_doth(lm, lm) for lm in lmats]
    for it in range(5):
        ts = [t + _doth(t, lp) for t, lp in zip(ts, lps)]
        if it < 4:
            lps = [_doth(lp, lp) for lp in lps]
    return ts


def _gate_columns(bgt, gct):
    sel = (lax.broadcasted_iota(jnp.int32, (16, 128), 0) == lax.broadcasted_iota(jnp.int32, (16, 128), 1)).astype(F32)
    g_rows = _dotx(sel, gct, 1, 1)
    hs = range(A_HEADS)
    return ([bgt[:, h:h + 1] for h in hs], [gct[:, 8 + h:9 + h] for h in hs], [g_rows[8 + h:9 + h, :] for h in hs])


def _gdn_delta_fwd(name, qkv, bg, gcum, a4=None):
    T = qkv.shape[0]
    tt = min(T, 512)
    nch = tt // A_CHUNK
    NC = T // A_CHUNK
    ni = T // tt
    wd, ng = 128 * A_HG, A_HEADS // A_HG
    assert ng == 1

    def body(*refs):
        q_ref, k_ref, v_ref, bg_ref, gc_ref = refs[:5]
        if a4 is None:
            o_ref, s_ref, t_ref, u_ref, w_ref, state = refs[5:]
        else:
            a4_ref, o_ref, s_ref, t_ref, u_ref, w_ref, ra_ref, state, send_sems, recv_sems = refs[5:]

            @pl.when(pl.program_id(1) == 0)
            def _():
                _later_blocks_start(a4_ref, ra_ref, send_sems, recv_sems)

        @pl.when(pl.program_id(1) == 0)
        def _():
            state[...] = jnp.zeros_like(state)

        def chunk(c, carry):
            rows = pl.ds(pl.multiple_of(c * A_CHUNK, A_CHUNK), A_CHUNK)
            hs = range(A_HG)
            cols = [pl.ds(h * 128, 128) for h in hs]
            q = [q_ref[rows, cols[h]] for h in hs]
            k = [k_ref[rows, cols[h]] for h in hs]
            v = [v_ref[rows, cols[h]] for h in hs]
            bcl, gcl, grw = _gate_columns(bg_ref[rows, :], gc_ref[rows, :])
            loc = [_chunk_local(q[h], k[h], gcl[h], grw[h], bcl[h]) for h in hs]
            e, f, gl, kb, amat = ([l[i] for l in loc] for i in (1, 2, 3, 4, 6))
            tinv = _unit_lower_inverse([l[5] for l in loc])
            u = [_doth(tinv[h], v[h] * bcl[h]) for h in hs]
            w = [_doth(tinv[h], kb[h] * e[h]) for h in hs]
            s = [state[h] for h in hs]
            vn = [u[h] - _dotb(w[h], s[h]) for h in hs]
            o_s = [_dotb(q[h] * e[h], s[h]) for h in hs]
            o_a = [_dotb(amat[h], vn[h]) for h in hs]
            s_new = [s[h] * gl[h] + _dotb(k[h] * f[h], vn[h], 0, 0) for h in hs]
            for h in hs:
                s_ref[h, c] = s[h].astype(BF16)
                t_ref[h, c] = tinv[h]
                u_ref[rows, cols[h]] = u[h]
                w_ref[rows, cols[h]] = w[h]
                o_ref[rows, cols[h]] = o_s[h] + o_a[h]
                state[h] = s_new[h]
            return carry

        lax.fori_loop(0, nch, chunk, 0)

        if a4 is not None:
            @pl.when(pl.program_id(1) == ni - 1)
            def _():
                x, y, c, chips = _mesh_pos()
                for j, chip in enumerate(chips):
                    cp = pltpu.make_async_remote_copy(src_ref=a4_ref.at[pl.ds(1, 3), c], dst_ref=ra_ref.at[j],
                                                      send_sem=send_sems.at[j], recv_sem=recv_sems.at[j],
                                                      device_id=(*chip, c), device_id_type=MESH)
                    cp.wait_recv()
                    cp.wait_send()

    hd = lambda col: pl.BlockSpec((tt, wd), lambda h, i: (i, col * ng + h))
    gate_spec = pl.BlockSpec((tt, 128), lambda h, i: (i, 0))
    in_specs = [hd(0), hd(1), hd(2), gate_spec, gate_spec]
    out_specs = [pl.BlockSpec((tt, wd), lambda h, i: (i, h)),
                 pl.BlockSpec((A_HG, nch, 128, 128), lambda h, i: (h, i, 0, 0)),
                 pl.BlockSpec((A_HG, nch, A_CHUNK, A_CHUNK), lambda h, i: (h, i, 0, 0)),
                 pl.BlockSpec((tt, wd), lambda h, i: (i, h)), pl.BlockSpec((tt, wd), lambda h, i: (i, h))]
    out_shape = [jax.ShapeDtypeStruct((T, 1024), F32), jax.ShapeDtypeStruct((A_HEADS, NC, 128, 128), BF16),
                 jax.ShapeDtypeStruct((A_HEADS, NC, A_CHUNK, A_CHUNK), F32),
                 jax.ShapeDtypeStruct((T, 1024), F32), jax.ShapeDtypeStruct((T, 1024), F32)]
    scratch = [pltpu.VMEM((A_HG, 128, 128), F32)]
    args = [qkv, qkv, qkv, bg, gcum]
    if a4 is not None:
        in_specs.append(HBM)
        out_specs.append(HBM)
        out_shape.append(jax.ShapeDtypeStruct((3, 3, A_HALF, FF_BLK), BF16))
        scratch += [pltpu.SemaphoreType.DMA((3,)), pltpu.SemaphoreType.DMA((3,))]
        args.append(a4)
    return pl.pallas_call(body, grid=(ng, ni), in_specs=in_specs, out_specs=out_specs, out_shape=out_shape,
                          scratch_shapes=scratch, name=name, compiler_params=_params(2))(*args)


def _grads_exchange(cs_ref, from_ref, send_sems, recv_sems):
    x, y, c, chips = _mesh_pos()
    return [pltpu.make_async_remote_copy(src_ref=cs_ref.at[2 * chip[0] + chip[1]], dst_ref=from_ref.at[j],
                                         send_sem=send_sems.at[j], recv_sem=recv_sems.at[j],
                                         device_id=(*chip, c), device_id_type=MESH)
            for j, chip in enumerate(chips)]


def _gdn_delta_bwd(name, qkv, bg, gcum, d_o, s_sv, t_sv, u_sv, w_sv, cs_early=None):
    T = qkv.shape[0]
    tt = min(T, 512)
    nch = tt // A_CHUNK
    ni = T // tt

    def body(*refs):
        q_ref, k_ref, v_ref, bg_ref, gc_ref, do_ref, s_ref, t_ref, u_ref, w_ref = refs[:10]
        if cs_early is None:
            dq_ref, dk_ref, dv_ref, dbg_ref, dstate = refs[10:]
        else:
            cs_ref, dq_ref, dk_ref, dv_ref, dbg_ref, from_ref, dstate, send_sems, recv_sems = refs[10:]

            @pl.when(pl.program_id(1) == 0)
            def _():
                for cp in _grads_exchange(cs_ref, from_ref, send_sems, recv_sems):
                    cp.start()

        @pl.when(pl.program_id(1) == 0)
        def _():
            dstate[...] = jnp.zeros_like(dstate)

        incl, strict, _ = _chunk_masks()
        upper = (lax.broadcasted_iota(jnp.int32, (A_CHUNK, A_CHUNK), 0)
                 <= lax.broadcasted_iota(jnp.int32, (A_CHUNK, A_CHUNK), 1)).astype(F32)
        last_row = lax.broadcasted_iota(jnp.int32, (A_CHUNK, 1), 0) == A_CHUNK - 1
        ones = jnp.ones((A_CHUNK, 128), F32)

        rsum = lambda x: jnp.sum(x, axis=1, keepdims=True)

        def chunk(cc, carry):
            c = nch - 1 - cc
            rows = pl.ds(pl.multiple_of(c * A_CHUNK, A_CHUNK), A_CHUNK)
            bcl, gcl, grw = _gate_columns(bg_ref[rows, :], gc_ref[rows, :])
            lane = lax.broadcasted_iota(jnp.int32, (A_CHUNK, 128), 1)
            dbg = jnp.zeros((A_CHUNK, 128), F32)
            for first in range(0, A_HG, 4):
                hs = range(first, first + 4)
                cols = {h: pl.ds(h * 128, 128) for h in hs}
                q = {h: q_ref[rows, cols[h]] for h in hs}
                k = {h: k_ref[rows, cols[h]] for h in hs}
                v = {h: v_ref[rows, cols[h]] for h in hs}
                do = {h: do_ref[rows, cols[h]] for h in hs}
                u = {h: u_ref[rows, cols[h]] for h in hs}
                w = {h: w_ref[rows, cols[h]] for h in hs}
                s = {h: s_ref[h, c] for h in hs}
                tinv = {h: t_ref[h, c] for h in hs}
                ds = {h: dstate[h] for h in hs}
                loc = {h: _chunk_local(q[h], k[h], gcl[h], grw[h], bcl[h]) for h in hs}
                dec, e, f, gl, kb, lmat, amat = ({h: loc[h][i] for h in hs} for i in range(7))
                qd = {h: q[h] * e[h] for h in hs}
                kd = {h: k[h] * f[h] for h in hs}
                ke = {h: kb[h] * e[h] for h in hs}
                vn = {h: u[h] - _dotb(w[h], s[h]) for h in hs}
                d_qd = {h: _dotb(do[h], s[h], 1, 1) for h in hs}
                d_a = {h: jnp.where(incl, _dotb(do[h], vn[h], 1, 1), 0.0) for h in hs}
                d_vn1 = {h: _dotb(amat[h], do[h], 0, 0) for h in hs}
                d_vn = {h: d_vn1[h] + _dotb(kd[h], ds[h]) for h in hs}
                d_kd = {h: _dotb(vn[h], ds[h], 1, 1) for h in hs}
                d_w = {h: -_dotb(d_vn[h], s[h], 1, 1) for h in hs}
                ds_q = {h: _dotb(qd[h], do[h], 0, 0) for h in hs}
                ds_w = {h: _dotb(w[h], d_vn[h], 0, 0) for h in hs}
                d_bv = {h: _doth(tinv[h], d_vn[h], 0, 0) for h in hs}
                d_ke = {h: _doth(tinv[h], d_w[h], 0, 0) for h in hs}
                d_l1 = {h: _dotb(d_bv[h], u[h], 1, 1) for h in hs}
                d_l = {h: -jnp.where(strict, d_l1[h] + _dotb(d_ke[h], w[h], 1, 1), 0.0) for h in hs}
                d_kk = {h: d_l[h] * dec[h] for h in hs}
                d_qk = {h: d_a[h] * dec[h] for h in hs}
                d_kb = {h: _dotb(d_kk[h], k[h]) for h in hs}
                dk1 = {h: _dotb(d_kk[h], kb[h], 0, 0) for h in hs}
                dk2 = {h: _dotb(d_qk[h], q[h], 0, 0) for h in hs}
                dq1 = {h: _dotb(d_qk[h], k[h]) for h in hs}
                m = {h: d_l[h] * lmat[h] + d_a[h] * amat[h] for h in hs}
                col_m = {h: _dotx(m[h], ones, 0, 0)[:, 0:1] for h in hs}
                d_gc = {}
                for h in hs:
                    d_gl = jnp.sum(jnp.sum(ds[h] * s[h].astype(F32), axis=1, keepdims=True), axis=0, keepdims=True)
                    r_kd = rsum(d_kd[h] * kd[h])
                    tail = jnp.sum(r_kd, axis=0, keepdims=True) + d_gl * gl[h]
                    d_gc[h] = (rsum(m[h]) - col_m[h] + rsum(d_qd[h] * qd[h]) - r_kd + rsum(d_ke[h] * ke[h])
                               + jnp.where(last_row, tail, 0.0))
                dg = {h: _dotx(upper, d_gc[h] * ones)[:, 0:1] for h in hs}
                for h in hs:
                    dstate[h] = gl[h] * ds[h] + ds_q[h] - ds_w[h]
                    dk_ref[rows, cols[h]] = (dk1[h] + dk2[h] + d_kd[h] * f[h] + d_ke[h] * (bcl[h] * e[h])
                                             + d_kb[h] * bcl[h])
                    dq_ref[rows, cols[h]] = dq1[h] + d_qd[h] * e[h]
                    dv_ref[rows, cols[h]] = d_bv[h] * bcl[h]
                    d_beta = rsum(d_ke[h] * k[h]) * e[h] + rsum(d_kb[h] * k[h]) + rsum(d_bv[h] * v[h])
                    dbg = jnp.where(lane == h, d_beta, jnp.where(lane == 8 + h, dg[h], dbg))
            dbg_ref[rows, :] = dbg
            return carry

        lax.fori_loop(0, nch, chunk, 0)

        if cs_early is not None:
            @pl.when(pl.program_id(1) == ni - 1)
            def _():
                for cp in _grads_exchange(cs_ref, from_ref, send_sems, recv_sems):
                    cp.wait_recv()
                    cp.wait_send()

    wd, ng = 128 * A_HG, A_HEADS // A_HG
    assert ng == 1
    rev = lambda i: ni - 1 - i
    hd = lambda col: pl.BlockSpec((tt, wd), lambda h, i: (rev(i), col * ng + h))
    hd1 = pl.BlockSpec((tt, wd), lambda h, i: (rev(i), h))
    gate_spec = pl.BlockSpec((tt, 128), lambda h, i: (rev(i), 0))
    in_specs = [hd(0), hd(1), hd(2), gate_spec, gate_spec, hd1,
                pl.BlockSpec((A_HG, nch, 128, 128), lambda h, i: (h, rev(i), 0, 0)),
                pl.BlockSpec((A_HG, nch, A_CHUNK, A_CHUNK), lambda h, i: (h, rev(i), 0, 0)), hd1, hd1]
    out_specs = [hd1, hd1, hd1, gate_spec]
    out_shape = [jax.ShapeDtypeStruct((T, 1024), F32)] * 3 + [jax.ShapeDtypeStruct((T, 128), F32)]
    scratch = [pltpu.VMEM((A_HG, 128, 128), F32)]
    args = [qkv, qkv, qkv, bg, gcum, d_o, s_sv, t_sv, u_sv, w_sv]
    if cs_early is not None:
        in_specs.append(HBM)
        out_specs.append(HBM)
        out_shape.append(jax.ShapeDtypeStruct((3,) + cs_early.shape[1:], BF16))
        scratch += [pltpu.SemaphoreType.DMA((3,)), pltpu.SemaphoreType.DMA((3,))]
        args.append(cs_early)
    return pl.pallas_call(body, grid=(ng, ni), in_specs=in_specs, out_specs=out_specs, out_shape=out_shape,
                          scratch_shapes=scratch, name=name, compiler_params=_params(2))(*args)


def _gdn_gate_fwd(name, o, proj, nw):
    T = o.shape[0]
    tt = min(T, 512)

    def body(o_ref, z_ref, nw_ref, y_ref):
        for h in range(A_HEADS):
            cols = pl.ds(h * 128, 128)
            ov, z = o_ref[:, cols], z_ref[:, cols]
            r = lax.rsqrt(jnp.mean(ov * ov, axis=-1, keepdims=True) + EPS)
            y_ref[:, cols] = (ov * r * nw_ref[...] * (z * _sigmoid(z))).astype(BF16)

    return pl.pallas_call(
        body, grid=(T // tt,),
        in_specs=[pl.BlockSpec((tt, 1024), lambda i: (i, 0)), pl.BlockSpec((tt, 1024), lambda i: (i, 3)),
                  pl.BlockSpec((1, 128), lambda i: (0, 0))],
        out_specs=pl.BlockSpec((tt, 1024), lambda i: (i, 0)),
        out_shape=jax.ShapeDtypeStruct((T, 1024), BF16), name=name, compiler_params=_params(1))(o, proj, nw)


def _gdn_gate_bwd(name, dy2, o, proj, nw):
    T = o.shape[0]
    tt = min(T, 512)

    def body(dy_ref, o_ref, z_ref, nw_ref, do_ref, dz_ref, dnw_ref):
        dnw = jnp.zeros((1, 128), F32)
        for h in range(A_HEADS):
            cols = pl.ds(h * 128, 128)
            dy, ov, z = dy_ref[:, cols], o_ref[:, cols], z_ref[:, cols]
            s = _sigmoid(z)
            sz = z * s
            r = lax.rsqrt(jnp.mean(ov * ov, axis=-1, keepdims=True) + EPS)
            xhat = ov * r
            dn = dy * sz
            dz_ref[:, cols] = dy * (xhat * nw_ref[...]) * (s + z * s * (1.0 - s))
            dxhat = dn * nw_ref[...]
            do_ref[:, cols] = r * (dxhat - xhat * jnp.mean(dxhat * xhat, axis=-1, keepdims=True))
            dnw = dnw + jnp.sum(dn * xhat, axis=0, keepdims=True)

        @pl.when(pl.program_id(0) == 0)
        def _():
            dnw_ref[...] = dnw

        @pl.when(pl.program_id(0) > 0)
        def _():
            dnw_ref[...] += dnw

    blk = lambda c: pl.BlockSpec((tt, 1024), lambda i: (i, c))
    return pl.pallas_call(
        body, grid=(T // tt,),
        in_specs=[blk(0), blk(0), blk(3), pl.BlockSpec((1, 128), lambda i: (0, 0))],
        out_specs=[blk(0), blk(0), pl.BlockSpec((1, 128), lambda i: (0, 0))],
        out_shape=[jax.ShapeDtypeStruct((T, 1024), F32), jax.ShapeDtypeStruct((T, 1024), F32),
                   jax.ShapeDtypeStruct((1, 128), F32)],
        name=name, compiler_params=_params(1))(dy2, o, proj, nw)


def _gdn_prep_bwd1(name, proj, wconv, gate_p, dq, dk, dv, dbg):
    T = proj.shape[0]
    tt = min(T, 256)
    hb = tt // 8

    def body(cur_ref, halo_ref, ba_ref, w_ref, gp_ref, dq_ref, dk_ref, dv_ref, dbg_ref,
             dc_ref, dw_ref, dba_ref, dgp_ref):
        first = pl.program_id(0) == 0
        rid = lax.broadcasted_iota(jnp.int32, (8, 128), 0)
        for c in range(24):
            cols = pl.ds(c * 128, 128)
            cur = cur_ref[:, cols]
            halo = jnp.where(first, 0.0, halo_ref[:, cols])
            conv = _conv_taps(cur, halo_ref[:, cols], w_ref[:, cols], first, -1)
            s = _sigmoid(conv)
            y = conv * s
            if c < 16:
                dref = dq_ref if c < 8 else dk_ref
                dn = dref[:, pl.ds((c % 8) * 128, 128)]
                rinv = lax.rsqrt(jnp.sum(y * y, axis=-1, keepdims=True) + EPS)
                yhat = y * rinv
                dyv = rinv * (dn - yhat * jnp.sum(dn * yhat, axis=-1, keepdims=True))
                if c < 8:
                    dyv = dyv * (A_DK ** -0.5)
            else:
                dyv = dv_ref[:, pl.ds((c - 16) * 128, 128)]
            dc = dyv * (s + conv * s * (1.0 - s))
            dc_ref[:, cols] = dc
            parts = [jnp.sum(dc * cur, axis=0, keepdims=True)]
            for sft in (1, 2, 3):
                sh = pltpu.roll(cur, sft, 0)
                edge = jnp.where(rid < sft, pltpu.roll(halo, sft, 0), sh[0:8])
                sh = jnp.concatenate([edge, sh[8:]], axis=0) if tt > 8 else edge
                parts.append(jnp.sum(dc * sh, axis=0, keepdims=True))
            dwc = jnp.concatenate(parts[::-1], axis=0)

            @pl.when(first)
            def _():
                dw_ref[:, cols] = dwc

            @pl.when(jnp.logical_not(first))
            def _():
                dw_ref[:, cols] += dwc

        ba = ba_ref[...]
        dbg = dbg_ref[...]
        lane = lax.broadcasted_iota(jnp.int32, ba.shape, 1)
        sb = _sigmoid(ba)
        zarg = ba + gp_ref[1:2, :]
        softplus = jnp.maximum(zarg, 0.0) + jnp.log(1.0 + jnp.exp(-jnp.abs(zarg)))
        d_b = dbg * sb * (1.0 - sb)
        d_a = dbg * gp_ref[0:1, :] * _sigmoid(zarg)
        dba_ref[...] = jnp.where(lane < 8, d_b, jnp.where(lane < 16, d_a, 0.0))
        g = gp_ref[0:1, :] * softplus
        in_a = (lane >= 8) & (lane < 16)
        sums = jnp.concatenate([jnp.sum(jnp.where(in_a, dbg * g, 0.0), axis=0, keepdims=True),
                                jnp.sum(jnp.where(in_a, d_a, 0.0), axis=0, keepdims=True)], axis=0)

        @pl.when(first)
        def _():
            dgp_ref[...] = sums

        @pl.when(jnp.logical_not(first))
        def _():
            dgp_ref[...] += sums

    row = lambda w, c=0: pl.BlockSpec((tt, w), lambda i: (i, c))
    return pl.pallas_call(
        body, grid=(T // tt,),
        in_specs=[row(3072), pl.BlockSpec((8, 3072), lambda i: (jnp.maximum(i * hb - 1, 0), 0)), row(128, 32),
                  pl.BlockSpec((4, 3072), lambda i: (0, 0)), pl.BlockSpec((2, 128), lambda i: (0, 0)),
                  row(1024), row(1024), row(1024), row(128)],
        out_specs=[row(3072), pl.BlockSpec((4, 3072), lambda i: (0, 0)), row(128),
                   pl.BlockSpec((2, 128), lambda i: (0, 0))],
        out_shape=[jax.ShapeDtypeStruct((T, 3072), F32), jax.ShapeDtypeStruct((4, 3072), F32),
                   jax.ShapeDtypeStruct((T, 128), F32), jax.ShapeDtypeStruct((2, 128), F32)],
        name=name, compiler_params=_params(1))(proj, proj, proj, wconv, gate_p, dq, dk, dv, dbg)


def _gdn_prep_bwd2(name, dc, wconv, dz, dba):
    T = dc.shape[0]
    tt = min(T, 256)
    hb = tt // 8
    ni = T // tt

    def body(cur_ref, halo_ref, w_ref, dz_ref, dba_ref, o_ref):
        last = pl.program_id(0) == ni - 1
        for c in range(24):
            cols = pl.ds(c * 128, 128)
            o_ref[:, cols] = _conv_taps(cur_ref[:, cols], halo_ref[:, cols], w_ref[:, cols], last, +1).astype(BF16)
        o_ref[:, pl.ds(3072, 1024)] = dz_ref[...].astype(BF16)
        o_ref[:, pl.ds(4096, 128)] = dba_ref[...].astype(BF16)

    return pl.pallas_call(
        body, grid=(ni,),
        in_specs=[pl.BlockSpec((tt, 3072), lambda i: (i, 0)),
                  pl.BlockSpec((8, 3072), lambda i: (jnp.minimum((i + 1) * hb, T // 8 - 1), 0)),
                  pl.BlockSpec((4, 3072), lambda i: (0, 0)),
                  pl.BlockSpec((tt, 1024), lambda i: (i, 0)), pl.BlockSpec((tt, 128), lambda i: (i, 0))],
        out_specs=pl.BlockSpec((tt, A_IN_PAD), lambda i: (i, 0)),
        out_shape=jax.ShapeDtypeStruct((T, A_IN_PAD), BF16), name=name, compiler_params=_params(1))(
            dc, dc, wconv, dz, dba)


def _gdn_fwd(x, nw, w_in, wconv, gate_p, out_nw, w_out, a4=None):
    h = _rms_fwd("a_rms", x, nw)
    proj = _mm_plain("a_proj", h, w_in, 1, 0, F32, tn=FF_BLK)
    qkv, bg, gcum = _gdn_prep("a_prep", proj, wconv, gate_p)
    o, s_sv, t_sv, u_sv, w_sv, *arrived = _gdn_delta_fwd("a_delta", qkv, bg, gcum, a4)
    o2 = _gdn_gate_fwd("a_gate", o, proj, out_nw)
    y = _mm_residual("a_out", o2, w_out, x, 1.0)
    return y, (h, proj, qkv, bg, gcum, o, s_sv, t_sv, u_sv, w_sv, o2), (arrived[0] if arrived else None)


def _gdn_bwd(dy, dyb, x, nw, w_in, wconv, gate_p, out_nw, w_out, saved, out_scale, cs_early=None):
    h, proj, qkv, bg, gcum, o, s_sv, t_sv, u_sv, w_sv, o2 = saved
    d_o2 = _mm_plain("a_dout", dyb, w_out, 1, 1, F32)
    d_wout = _mm_plain("a_dwout", o2, dyb, 0, 0, F32)
    d_o, d_z, d_outnw = _gdn_gate_bwd("a_dgate", d_o2, o, proj, out_nw)
    dq, dk, dv, dbg, *arrived = _gdn_delta_bwd("a_ddelta", qkv, bg, gcum, d_o, s_sv, t_sv, u_sv, w_sv, cs_early)
    dc, d_wconv, dba, dgp = _gdn_prep_bwd1("a_dprep1", proj, wconv, gate_p, dq, dk, dv, dbg)
    dproj = _gdn_prep_bwd2("a_dprep2", dc, wconv, d_z, dba)
    d_win = _mm_plain("a_dwin", h, dproj, 0, 0, F32, tn=FF_BLK, tk=2048)
    dx, dxb, d_nw = _mm_rms_bwd("a_dx", dproj, w_in, x, dy, nw, A_IN_PAD, out_scale)
    return dx, dxb, d_nw, d_win, d_wconv, dgp, d_outnw, d_wout, (arrived[0] if arrived else None)


def _swa_masks(n):
    qi = lax.broadcasted_iota(jnp.int32, (B_BLK, B_BLK), 0)
    kj = lax.broadcasted_iota(jnp.int32, (B_BLK, B_BLK), 1)
    return kj > qi + jnp.where(n > 0, 0, B_BLK), kj <= qi


def _swa_fwd(name, q, k, v, sinks):
    T = q.shape[1]
    tq = min(T, 1024)
    nbt = tq // B_BLK
    scale = B_HD ** -0.5
    G = B_HEADS // B_KV

    def body(q_ref, k_ref, v_ref, kh_ref, vh_ref, s_ref, o_ref, l_ref):
        first_blk = pl.program_id(1) * nbt

        def block(n, kp, vp):
            m_prev, m_cur = _swa_masks(first_blk + n)
            cur = pl.ds(pl.multiple_of(n * B_BLK, B_BLK), B_BLK)
            kc, vc = k_ref[0, cur, :], v_ref[0, cur, :]
            gs = range(G)
            rmax = lambda a: jnp.max(a, axis=1, keepdims=True)
            rsum = lambda a: jnp.sum(a, axis=1, keepdims=True)
            sink = [s_ref[g][:, 0:1] for g in gs]
            qb = [q_ref[g, cur, :] for g in gs]
            s_p = [jnp.where(m_prev, _dot(qb[g], kp, 1, 1) * scale, -jnp.inf) for g in gs]
            s_c = [jnp.where(m_cur, _dot(qb[g], kc, 1, 1) * scale, -jnp.inf) for g in gs]
            m = [jnp.maximum(jnp.maximum(rmax(s_p[g]), rmax(s_c[g])), sink[g]) for g in gs]
            p_p = [jnp.exp(s_p[g] - m[g]) for g in gs]
            p_c = [jnp.exp(s_c[g] - m[g]) for g in gs]
            den = [rsum(p_p[g]) + rsum(p_c[g]) + jnp.exp(sink[g] - m[g]) for g in gs]
            a_p = [_dotb(p_p[g], vp) for g in gs]
            a_c = [_dotb(p_c[g], vc) for g in gs]
            for g in gs:
                o_ref[g, cur, :] = ((a_p[g] + a_c[g]) / den[g]).astype(BF16)
                l_ref[g, cur, :] = m[g] + jnp.log(den[g])

        block(0, kh_ref[0], vh_ref[0])

        def rest(n, carry):
            prv = pl.ds(pl.multiple_of((n - 1) * B_BLK, B_BLK), B_BLK)
            block(n, k_ref[0, prv, :], v_ref[0, prv, :])
            return carry

        lax.fori_loop(1, nbt, rest, 0)

    qs = pl.BlockSpec((G, tq, B_HD), lambda kv, i: (kv, i, 0))
    ks = pl.BlockSpec((1, tq, B_HD), lambda kv, i: (kv, i, 0))
    halo = pl.BlockSpec((1, B_BLK, B_HD), lambda kv, i: (kv, jnp.maximum(i * nbt - 1, 0), 0))
    return pl.pallas_call(
        body, grid=(B_KV, T // tq),
        in_specs=[qs, ks, ks, halo, halo, pl.BlockSpec((G, 1, 128), lambda kv, i: (kv, 0, 0))],
        out_specs=[qs, pl.BlockSpec((G, tq, 1), lambda kv, i: (kv, i, 0))],
        out_shape=[jax.ShapeDtypeStruct((B_HEADS, T, B_HD), BF16), jax.ShapeDtypeStruct((B_HEADS, T, 1), F32)],
        name=name, compiler_params=_params(2))(q, k, v, k, v, sinks)


def _swa_bwd(name, q, k, v, sinks, o, lse, do):
    T = q.shape[1]
    tq = min(T, 1024)
    nbt, ni = tq // B_BLK, T // tq
    scale = B_HD ** -0.5
    G = B_HEADS // B_KV

    def body(q_ref, k_ref, v_ref, kh_ref, vh_ref, s_ref, o_ref, l_ref, do_ref, dq_ref, dk_ref, dv_ref, ds_ref,
             dk_halo, dv_halo):
        step = pl.program_id(1)
        first_blk = (ni - 1 - step) * nbt
        last = pl.ds(tq - B_BLK, B_BLK)
        dk_ref[...] = jnp.zeros_like(dk_ref)
        dv_ref[...] = jnp.zeros_like(dv_ref)

        @pl.when(step > 0)
        def _():
            dk_ref[0, last, :] = dk_halo[...]
            dv_ref[0, last, :] = dv_halo[...]

        def block(n, kp, vp, dsinks):
            m_prev, m_cur = _swa_masks(first_blk + n)
            cur = pl.ds(pl.multiple_of(n * B_BLK, B_BLK), B_BLK)
            kc, vc = k_ref[0, cur, :], v_ref[0, cur, :]
            gs = range(G)
            sink = [s_ref[g][:, 0:1] for g in gs]
            qb = [q_ref[g, cur, :] for g in gs]
            dob = [do_ref[g, cur, :] for g in gs]
            lse_b = [l_ref[g, cur, :] for g in gs]
            p_p = [jnp.where(m_prev, jnp.exp(_dot(qb[g], kp, 1, 1) * scale - lse_b[g]), 0.0) for g in gs]
            p_c = [jnp.where(m_cur, jnp.exp(_dot(qb[g], kc, 1, 1) * scale - lse_b[g]), 0.0) for g in gs]
            delta = [jnp.sum(dob[g].astype(F32) * o_ref[g, cur, :].astype(F32), axis=1, keepdims=True) for g in gs]
            ds_p = [p_p[g] * (_dot(dob[g], vp, 1, 1) - delta[g]) for g in gs]
            ds_c = [p_c[g] * (_dot(dob[g], vc, 1, 1) - delta[g]) for g in gs]
            dq_p = [_dotb(ds_p[g], kp) for g in gs]
            dq_c = [_dotb(ds_c[g], kc) for g in gs]
            dk_ps = [_dotb(ds_p[g], qb[g], 0, 0) for g in gs]
            dk_cs = [_dotb(ds_c[g], qb[g], 0, 0) for g in gs]
            dv_ps = [_dotb(p_p[g], dob[g], 0, 0) for g in gs]
            dv_cs = [_dotb(p_c[g], dob[g], 0, 0) for g in gs]
            for g in gs:
                dq_ref[g, cur, :] = (dq_p[g] + dq_c[g]) * scale
            out = tuple(dsinks[g] - jnp.sum(jnp.exp(sink[g] - lse_b[g]) * delta[g], axis=0, keepdims=True) for g in gs)
            total = lambda parts: (parts[0] + parts[1]) + (parts[2] + parts[3])
            dk_ref[0, cur, :] += total(dk_cs) * scale
            dv_ref[0, cur, :] += total(dv_cs)
            return total(dk_ps) * scale, total(dv_ps), out

        zeros = tuple(jnp.zeros((1, 1), F32) for _ in range(G))
        dk_p, dv_p, dsinks = block(0, kh_ref[0], vh_ref[0], zeros)
        dk_halo[...] = dk_p
        dv_halo[...] = dv_p

        def rest(n, dsinks):
            prv = pl.ds(pl.multiple_of((n - 1) * B_BLK, B_BLK), B_BLK)
            dk_p, dv_p, dsinks = block(n, k_ref[0, prv, :], v_ref[0, prv, :], dsinks)
            dk_ref[0, prv, :] += dk_p
            dv_ref[0, prv, :] += dv_p
            return dsinks

        dsinks = lax.fori_loop(1, nbt, rest, dsinks)
        for g in range(G):
            row = jnp.broadcast_to(dsinks[g], (1, 128))

            @pl.when(step == 0)
            def _():
                ds_ref[g] = row

            @pl.when(step > 0)
            def _():
                ds_ref[g] += row

    rev = lambda i: ni - 1 - i
    qs = pl.BlockSpec((G, tq, B_HD), lambda kv, i: (kv, rev(i), 0))
    ks = pl.BlockSpec((1, tq, B_HD), lambda kv, i: (kv, rev(i), 0))
    halo = pl.BlockSpec((1, B_BLK, B_HD), lambda kv, i: (kv, jnp.maximum(rev(i) * nbt - 1, 0), 0))
    ss = pl.BlockSpec((G, 1, 128), lambda kv, i: (kv, 0, 0))
    return pl.pallas_call(
        body, grid=(B_KV, ni),
        in_specs=[qs, ks, ks, halo, halo, ss, qs, pl.BlockSpec((G, tq, 1), lambda kv, i: (kv, rev(i), 0)), qs],
        out_specs=[qs, ks, ks, ss],
        out_shape=[jax.ShapeDtypeStruct((B_HEADS, T, B_HD), F32), jax.ShapeDtypeStruct((B_KV, T, B_HD), F32),
                   jax.ShapeDtypeStruct((B_KV, T, B_HD), F32), jax.ShapeDtypeStruct((B_HEADS, 1, 128), F32)],
        scratch_shapes=[pltpu.VMEM((B_BLK, B_HD), F32), pltpu.VMEM((B_BLK, B_HD), F32)],
        name=name, compiler_params=_params(2))(q, k, v, k, v, sinks, o, lse, do)


def _split_heads(a, n):
    T = a.shape[0]
    return a.reshape(T, n, B_HD).transpose(1, 0, 2)


def _merge_heads(a):
    n, T, _ = a.shape
    return a.transpose(1, 0, 2).reshape(T, n * B_HD)


def _swa_mixer_fwd(x, nw, w_in, b_in, sinks, w_out, b_out):
    h = _rms_fwd("b_rms", x, nw)
    proj = _mm_plain("b_proj", h, w_in, 1, 0, BF16, tn=768, bias=b_in)
    q, k, v = _split_heads(proj[:, :1024], B_HEADS), _split_heads(proj[:, 1024:1280], B_KV), _split_heads(proj[:, 1280:], B_KV)
    o, lse = _swa_fwd("b_attn", q, k, v, sinks)
    om = _merge_heads(o)
    y = _mm_residual("b_out", om, w_out, x, 1.0, bias=b_out)
    return y, (h, q, k, v, o, lse, om)


def _swa_mixer_bwd(dy, dyb, x, nw, w_in, sinks, w_out, saved, out_scale):
    h, q, k, v, o, lse, om = saved
    d_om = _mm_plain("b_dout", dyb, w_out, 1, 1, BF16)
    d_wout = _mm_plain("b_dwout", om, dyb, 0, 0, F32)
    d_bout = _colsum("b_dbout", dy)
    dq, dk, dv, dsinks = _swa_bwd("b_dattn", q, k, v, sinks, o, lse, _split_heads(d_om, B_HEADS))
    dproj = jnp.concatenate([_merge_heads(dq), _merge_heads(dk), _merge_heads(dv)], axis=1)
    d_bin = _colsum("b_dbin", dproj)
    d_win = _mm_plain("b_dwin", h, dproj, 0, 0, F32, tn=768)
    dx, dxb, d_nw = _mm_rms_bwd("b_dx", dproj, w_in, x, dy, nw, 1536, out_scale)
    return dx, dxb, d_nw, d_win, d_bin, dsinks[:, 0, 0], d_wout, d_bout


def _loss_head(name, x, tgt, fw, out_scale):
    T, D = x.shape
    tt = min(T, 512)

    def body(x_ref, t_ref, w_ref, dx_ref, dxb_ref, loss_ref, dw_ref):
        xv = x_ref[...]
        r = lax.rsqrt(jnp.mean(xv * xv, axis=-1, keepdims=True) + EPS)
        xhat = xv * r
        diff = xhat * w_ref[...] - t_ref[...]
        part = 0.5 * jnp.sum(jnp.mean(diff * diff, axis=-1, keepdims=True), axis=0, keepdims=True)
        dyv = diff * (1.0 / D)
        dxhat = dyv * w_ref[...]
        dx = r * (dxhat - xhat * jnp.mean(dxhat * xhat, axis=-1, keepdims=True))
        dx_ref[...] = dx
        dxb_ref[...] = (dx * out_scale).astype(BF16)
        dw = jnp.sum(dyv * xhat, axis=0, keepdims=True)
        lp = jnp.broadcast_to(part, (1, 128))

        @pl.when(pl.program_id(0) == 0)
        def _():
            loss_ref[...] = lp
            dw_ref[...] = dw

        @pl.when(pl.program_id(0) > 0)
        def _():
            loss_ref[...] += lp
            dw_ref[...] += dw

    row = pl.BlockSpec((tt, D), lambda i: (i, 0))
    return pl.pallas_call(
        body, grid=(T // tt,), in_specs=[row, row, pl.BlockSpec((1, D), lambda i: (0, 0))],
        out_specs=[row, row, pl.BlockSpec((1, 128), lambda i: (0, 0)), pl.BlockSpec((1, D), lambda i: (0, 0))],
        out_shape=[jax.ShapeDtypeStruct((T, D), F32), jax.ShapeDtypeStruct((T, D), BF16),
                   jax.ShapeDtypeStruct((1, 128), F32), jax.ShapeDtypeStruct((1, D), F32)],
        name=name, compiler_params=_params(1))(x, tgt, fw)


def _local_step(x, tgt, wts, a4=None):
    W = wts
    g = {}
    ga, wdn = W["ga"], W["w_down"]
    n1, n2, nm = W["ffn1_norm"], W["ffn2_norm"], W["mix_norm"]
    x1, sv1 = _ffn_fwd("f10", x, n1[0:1], ga, 0, wdn[0])
    x2, sva, arrived = _gdn_fwd(x1, nm[0:1], W["a_w_in"], W["a_w_conv"], W["a_gate_p"], W["a_out_norm"], W["a_w_out"], a4)
    if a4 is not None:
        ga = _fill_a(1, a4, arrived, ga.reshape(4, 4, 2, A_HALF, FF_BLK)).reshape(ga.shape)
    x3, sv3 = _ffn_fwd("f20", x2, n2[0:1], ga, 2, wdn[2])
    x4, sv4 = _ffn_fwd("f11", x3, n1[1:2], ga, 1, wdn[1])
    x5, svb = _swa_mixer_fwd(x4, nm[1:2], W["b_w_in"], W["b_b_in"], W["b_sinks"], W["b_w_out"], W["b_b_out"])
    x6, sv6 = _ffn_fwd("f21", x5, n2[1:2], ga, 3, wdn[3])
    dx, dxb, loss_p, g["final_norm"] = _loss_head("loss_head", x6, tgt, W["final_norm"], 0.5)

    pa = jnp.zeros(ga.shape, BF16)
    dx, dxb, n21, pa, wd21 = _ffn_bwd("f21", dx, dxb, x5, n2[1:2], ga, 3, wdn[3], sv6, pa, 1.0)
    dx, dxb, nb, g["b_w_in"], g["b_b_in"], g["b_sinks"], g["b_w_out"], g["b_b_out"] = _swa_mixer_bwd(
        dx, dxb, x4, nm[1:2], W["b_w_in"], W["b_sinks"], W["b_w_out"], svb, 0.5)
    dx, dxb, n11, pa, wd11 = _ffn_bwd("f11", dx, dxb, x3, n1[1:2], ga, 1, wdn[1], sv4, pa, 0.5)
    dx, dxb, n20, pa, wd20 = _ffn_bwd("f20", dx, dxb, x2, n2[0:1], ga, 2, wdn[2], sv3, pa, 1.0)
    g["cs_early"] = _pair_sums_a("1", pa.reshape(4, 4, 2, A_HALF, FF_BLK), 1, 3) if a4 is not None else None
    dx, dxb, na, g["a_w_in"], g["a_w_conv"], g["a_gate_p"], g["a_out_norm"], g["a_w_out"], g["from_early"] = _gdn_bwd(
        dx, dxb, x1, nm[0:1], W["a_w_in"], W["a_w_conv"], W["a_gate_p"], W["a_out_norm"], W["a_w_out"], sva, 0.5,
        g["cs_early"])
    dx, dxb, n10, pa, wd10 = _ffn_bwd("f10", dx, dxb, x, n1[0:1], ga, 0, wdn[0], sv1, pa, 1.0)

    g["ffn1_norm"] = jnp.concatenate([n10, n11], axis=0)
    g["ffn2_norm"] = jnp.concatenate([n20, n21], axis=0)
    g["mix_norm"] = jnp.concatenate([na, nb], axis=0)
    g["ga"] = pa
    g["w_down"] = jnp.stack([wd10, wd11, wd20, wd21])
    return loss_p, dx, g


A_ROWS = 4 * D_MODEL
PACK = (("ffn1_w_down", 1408), ("ffn2_w_down", 1408), ("a_w_in", 1028), ("a_w_out", 256), ("b_w_in", 384),
        ("b_w_out", 256))
PACK_TILE = 16
PACK_USED = sum(-(-n // PACK_TILE) * PACK_TILE for _, n in PACK)
PACK_ROWS = 4864
assert PACK_USED <= PACK_ROWS
SMALL_SHARD = (8, 512)
MOVE_ROWS = {"a": 512, "b": 608}
SUM_ROWS = {"a": 256, "b": 304}


def _mesh_pos():
    x, y, c = lax.axis_index("x"), lax.axis_index("y"), lax.axis_index("c")
    return x, y, c, [(1 - x, y), (x, 1 - y), (1 - x, 1 - y)]


def _half(rows, c):
    return pl.ds(pl.multiple_of(c * (rows // 2), 16), rows // 2)


A_HALF = D_MODEL // 2


def _src_chip(j):
    x, y = lax.axis_index("x"), lax.axis_index("y")
    return jnp.where(j == 0, 2 * (1 - x) + y, jnp.where(j == 1, 2 * x + 1 - y, 2 * (1 - x) + 1 - y))


def _later_blocks_start(a4_ref, ra_ref, send_sems, recv_sems):
    x, y, c, chips = _mesh_pos()
    copies = [pltpu.make_async_remote_copy(src_ref=a4_ref.at[pl.ds(1, 3), c], dst_ref=ra_ref.at[j],
                                           send_sem=send_sems.at[j], recv_sem=recv_sems.at[j],
                                           device_id=(*chip, c), device_id_type=MESH)
              for j, chip in enumerate(chips)]
    for cp in copies:
        cp.start()
    return copies


def _fill_a(phase, a4, ra, ga=None):
    nb = 1 if phase == 0 else 3
    first = 0 if phase == 0 else 1
    steps = 3 * nb
    own_tiles = 2 * nb
    ra = ra.reshape(3, nb, A_HALF, FF_BLK)

    def body(*refs):
        if phase == 0:
            r_ref, own_ref, g_ref, send_sem, recv_sem, local_sems = refs
        else:
            r_ref, own_ref, _, g_ref, send_sem, recv_sem, local_sems = refs
        x, y, c, _ = _mesh_pos()
        s = pl.program_id(0)
        j, b = s // nb, s % nb
        dst = g_ref.at[_src_chip(j), first + b, c]
        keep = pltpu.make_async_copy(r_ref.at[0, 0], dst, local_sems.at[0])
        give = pltpu.make_async_remote_copy(src_ref=r_ref.at[0, 0], dst_ref=dst, send_sem=send_sem, recv_sem=recv_sem,
                                            device_id=(x, y, 1 - c), device_id_type=MESH)
        keep.start()
        give.start()

        @pl.when(s < own_tiles)
        def _():
            own = pltpu.make_async_copy(own_ref.at[0, 0], g_ref.at[2 * x + y, first + s // 2, s % 2], local_sems.at[1])
            own.start()
            own.wait()

        give.wait_send()
        keep.wait()

        @pl.when(s == steps - 1)
        def _():
            landed = g_ref.at[pl.ds(0, 3), pl.ds(0, nb), 0]
            pltpu.make_async_remote_copy(src_ref=landed, dst_ref=landed, send_sem=send_sem, recv_sem=recv_sem,
                                         device_id=(x, y, c), device_id_type=MESH).wait_recv()

    tile = (1, 1, A_HALF, FF_BLK)
    in_specs = [pl.BlockSpec(tile, lambda s: (s // nb, s % nb, 0, 0)),
                pl.BlockSpec(tile, lambda s: (first + jnp.minimum(s, own_tiles - 1) // 2, jnp.minimum(s, own_tiles - 1) % 2, 0, 0))]
    args = [ra, a4]
    if phase == 1:
        in_specs.append(HBM)
        args.append(ga)
    return pl.pallas_call(
        body, grid=(steps,), in_specs=in_specs, out_specs=HBM,
        out_shape=jax.ShapeDtypeStruct((4, 4, 2, A_HALF, FF_BLK), BF16),
        scratch_shapes=[pltpu.SemaphoreType.DMA, pltpu.SemaphoreType.DMA, pltpu.SemaphoreType.DMA((2,))],
        input_output_aliases={2: 0} if phase == 1 else {},
        name="fill_a%d" % phase, compiler_params=_params(1))(*args)


def _gather_chips(big_a4, big_b, small):
    bufs = (big_a4, big_b, small)

    def body(a_ref, b_ref, small_ref, ra_ref, rb_ref, rs_ref, send_sems, recv_sems):
        x, y, c, chips = _mesh_pos()
        srcs = (a_ref.at[0, c], b_ref.at[_half(PACK_ROWS, c)], small_ref)
        send = []
        for j, chip in enumerate(chips):
            for n, (src, dst) in enumerate(zip(srcs, (ra_ref, rb_ref, rs_ref))):
                send.append(pltpu.make_async_remote_copy(src_ref=src, dst_ref=dst.at[j],
                                                         send_sem=send_sems.at[3 * j + n], recv_sem=recv_sems.at[3 * j + n],
                                                         device_id=(*chip, c), device_id_type=MESH))
        for cp in send:
            cp.start()
        for cp in send:
            cp.wait_recv()
        for cp in send:
            cp.wait_send()

    return pl.pallas_call(
        body, name="gather_chips", in_specs=[HBM, HBM, HBM], out_specs=[HBM, HBM, HBM],
        out_shape=[jax.ShapeDtypeStruct((3, A_HALF, FF_BLK), BF16),
                   jax.ShapeDtypeStruct((3, PACK_ROWS // 2, 1024), BF16), jax.ShapeDtypeStruct((3,) + SMALL_SHARD, F32)],
        scratch_shapes=[pltpu.SemaphoreType.DMA((9,)), pltpu.SemaphoreType.DMA((9,))])(*bufs)


def _gather_fill(tag, big, recv):
    rows_all, width = big.shape
    half, mv = rows_all // 2, MOVE_ROWS[tag]
    nt = half // mv
    own_tiles = rows_all // mv
    assert half % mv == 0 and own_tiles <= 3 * nt

    def body(recv_ref, big_ref, g_ref, send_sem, recv_sem, local_sems):
        x, y, c, chips = _mesh_pos()
        j, t = pl.program_id(0), pl.program_id(1)
        step = j * nt + t
        src_chip = jnp.where(j == 0, 2 * (1 - x) + y, jnp.where(j == 1, 2 * x + 1 - y, 2 * (1 - x) + 1 - y))
        rows = pl.ds(pl.multiple_of(c * half + t * mv, 16), mv)
        keep = pltpu.make_async_copy(recv_ref.at[0], g_ref.at[src_chip, rows], local_sems.at[0])
        give = pltpu.make_async_remote_copy(src_ref=recv_ref.at[0], dst_ref=g_ref.at[src_chip, rows],
                                            send_sem=send_sem, recv_sem=recv_sem,
                                            device_id=(x, y, 1 - c), device_id_type=MESH)
        keep.start()
        give.start()

        @pl.when(step < own_tiles)
        def _():
            own_rows = pl.ds(pl.multiple_of(step * mv, 16), mv)
            own = pltpu.make_async_copy(big_ref, g_ref.at[2 * x + y, own_rows], local_sems.at[1])
            own.start()
            own.wait()

        give.wait_send()
        keep.wait()

        @pl.when(step == 3 * nt - 1)
        def _():
            landed = g_ref.at[pl.ds(0, 3), pl.ds(0, half)]
            pltpu.make_async_remote_copy(src_ref=landed, dst_ref=landed, send_sem=send_sem, recv_sem=recv_sem,
                                         device_id=(x, y, c), device_id_type=MESH).wait_recv()

    return pl.pallas_call(
        body, grid=(3, nt),
        in_specs=[pl.BlockSpec((1, mv, width), lambda j, t: (j, t, 0)),
                  pl.BlockSpec((mv, width), lambda j, t: (jnp.minimum(j * nt + t, own_tiles - 1), 0))],
        out_specs=HBM, out_shape=jax.ShapeDtypeStruct((4, rows_all, width), BF16),
        scratch_shapes=[pltpu.SemaphoreType.DMA, pltpu.SemaphoreType.DMA, pltpu.SemaphoreType.DMA((2,))],
        name="gather_fill_" + tag, compiler_params=_params(2))(recv, big)


def _pair_send(tag, p):
    _, rows_all, width = p.shape
    half, mv = rows_all // 2, MOVE_ROWS[tag]
    nt = half // mv

    def body(p_ref, a_ref, send_sem, recv_sem):
        x, y, c, _ = _mesh_pos()
        s, t = pl.program_id(0), pl.program_id(1)
        rows = pl.ds(pl.multiple_of(t * mv, 16), mv)
        give = pltpu.make_async_remote_copy(src_ref=p_ref.at[0], dst_ref=a_ref.at[s, rows], send_sem=send_sem,
                                            recv_sem=recv_sem, device_id=(x, y, 1 - c), device_id_type=MESH)
        give.start()
        give.wait_send()

        @pl.when((s == 3) & (t == nt - 1))
        def _():
            pltpu.make_async_remote_copy(src_ref=a_ref, dst_ref=a_ref, send_sem=send_sem, recv_sem=recv_sem,
                                         device_id=(x, y, c), device_id_type=MESH).wait_recv()

    return pl.pallas_call(
        body, grid=(4, nt),
        in_specs=[pl.BlockSpec((1, mv, width), lambda s, t: (s, (1 - lax.axis_index("c")) * nt + t, 0))],
        out_specs=HBM, out_shape=jax.ShapeDtypeStruct((4, half, width), BF16),
        scratch_shapes=[pltpu.SemaphoreType.DMA, pltpu.SemaphoreType.DMA],
        name="pair_send_" + tag, compiler_params=_params(2))(p)


def _pair_sum(tag, p, a):
    _, half, width = a.shape
    sr = SUM_ROWS[tag]
    nt = half // sr
    assert half % sr == 0

    def body(p_ref, a_ref, o_ref):
        o_ref[...] = (p_ref[...].astype(F32) + a_ref[...].astype(F32)).astype(BF16)

    spec = pl.BlockSpec((1, sr, width), lambda s, t: (s, t, 0))
    return pl.pallas_call(
        body, grid=(4, nt),
        in_specs=[pl.BlockSpec((1, sr, width), lambda s, t: (s, lax.axis_index("c") * nt + t, 0)), spec],
        out_specs=spec, out_shape=jax.ShapeDtypeStruct((4, half, width), BF16),
        name="pair_sum_" + tag, compiler_params=_params(2))(p, a)


def _chip_exchange(cs_a, cs_b):
    def body(ca_ref, cb_ref, ba_ref, bb_ref, send_sems, recv_sems):
        x, y, c, chips = _mesh_pos()
        send = []
        for j, chip in enumerate(chips):
            for n, (src, dst) in enumerate(((ca_ref, ba_ref), (cb_ref, bb_ref))):
                send.append(pltpu.make_async_remote_copy(src_ref=src.at[2 * chip[0] + chip[1]], dst_ref=dst.at[j],
                                                         send_sem=send_sems.at[2 * j + n], recv_sem=recv_sems.at[2 * j + n],
                                                         device_id=(*chip, c), device_id_type=MESH))
        for cp in send:
            cp.start()
        for cp in send:
            cp.wait_recv()
        for cp in send:
            cp.wait_send()

    return pl.pallas_call(
        body, name="chip_exchange", in_specs=[HBM, HBM], out_specs=[HBM, HBM],
        out_shape=[jax.ShapeDtypeStruct((3,) + cs.shape[1:], BF16) for cs in (cs_a, cs_b)],
        scratch_shapes=[pltpu.SemaphoreType.DMA((6,)), pltpu.SemaphoreType.DMA((6,))])(cs_a, cs_b)


def _chip_sum(tag, cs, b):
    _, half, width = cs.shape
    sr = SUM_ROWS[tag]
    nt = half // sr

    def body(c_ref, b_ref, r_ref, buf, send_sems, recv_sem, local_sems):
        x, y, c, _ = _mesh_pos()
        t = pl.program_id(0)
        slot = lax.rem(t, 2)

        def copies(k, tile):
            rows = pl.ds(pl.multiple_of(c * half + tile * sr, 8), sr)
            keep = pltpu.make_async_copy(buf.at[k], r_ref.at[rows], local_sems.at[k])
            give = pltpu.make_async_remote_copy(src_ref=buf.at[k], dst_ref=r_ref.at[rows], send_sem=send_sems.at[k],
                                                recv_sem=recv_sem, device_id=(x, y, 1 - c), device_id_type=MESH)
            return keep, give

        @pl.when(t >= 2)
        def _():
            keep, give = copies(slot, t - 2)
            keep.wait()
            give.wait_send()

        buf[slot] = (c_ref[0].astype(F32) + b_ref[0].astype(F32)) + (b_ref[1].astype(F32) + b_ref[2].astype(F32))
        keep, give = copies(slot, t)
        keep.start()
        give.start()

        @pl.when(t == nt - 1)
        def _():
            for back in (1, 0):
                keep, give = copies(lax.rem(t - back, 2), t - back)
                keep.wait()
                give.wait_send()
            landed = r_ref.at[_half(2 * half, 1 - c)]
            pltpu.make_async_remote_copy(src_ref=landed, dst_ref=landed, send_sem=send_sems.at[0], recv_sem=recv_sem,
                                         device_id=(x, y, c), device_id_type=MESH).wait_recv()

    return pl.pallas_call(
        body, grid=(nt,),
        in_specs=[pl.BlockSpec((1, sr, width), lambda t: (2 * lax.axis_index("x") + lax.axis_index("y"), t, 0)),
                  pl.BlockSpec((3, sr, width), lambda t: (0, t, 0))],
        out_specs=HBM, out_shape=jax.ShapeDtypeStruct((2 * half, width), F32),
        scratch_shapes=[pltpu.VMEM((2, sr, width), F32), pltpu.SemaphoreType.DMA((2,)), pltpu.SemaphoreType.DMA,
                        pltpu.SemaphoreType.DMA((2,))],
        name="chip_sum_" + tag, compiler_params=_params(1))(cs, b)


def _pair_sums_a(tag, p5, b0, nb):
    tile5 = (1, 1, 1, A_HALF, FF_BLK)
    tile4 = (1, 1, A_HALF, FF_BLK)

    def send_body(p_ref, a_ref, send_sem, recv_sem):
        x, y, c, _ = _mesh_pos()
        s, t = pl.program_id(0), pl.program_id(1)
        give = pltpu.make_async_remote_copy(src_ref=p_ref.at[0, 0, 0], dst_ref=a_ref.at[s, t], send_sem=send_sem,
                                            recv_sem=recv_sem, device_id=(x, y, 1 - c), device_id_type=MESH)
        give.start()
        give.wait_send()

        @pl.when((s == 3) & (t == nb - 1))
        def _():
            pltpu.make_async_remote_copy(src_ref=a_ref, dst_ref=a_ref, send_sem=send_sem, recv_sem=recv_sem,
                                         device_id=(x, y, c), device_id_type=MESH).wait_recv()

    shape = jax.ShapeDtypeStruct((4, nb, A_HALF, FF_BLK), BF16)
    recv = pl.pallas_call(
        send_body, grid=(4, nb),
        in_specs=[pl.BlockSpec(tile5, lambda s, t: (s, b0 + t, 1 - lax.axis_index("c"), 0, 0))],
        out_specs=HBM, out_shape=shape, scratch_shapes=[pltpu.SemaphoreType.DMA, pltpu.SemaphoreType.DMA],
        name="pair_send_a" + tag, compiler_params=_params(2))(p5)

    def sum_body(p_ref, a_ref, o_ref):
        o_ref[0, 0] = (p_ref[0, 0, 0].astype(F32) + a_ref[0, 0].astype(F32)).astype(BF16)

    spec = pl.BlockSpec(tile4, lambda s, t: (s, t, 0, 0))
    return pl.pallas_call(
        sum_body, grid=(4, nb),
        in_specs=[pl.BlockSpec(tile5, lambda s, t: (s, b0 + t, lax.axis_index("c"), 0, 0)), spec],
        out_specs=spec, out_shape=shape, name="pair_sum_a" + tag, compiler_params=_params(2))(p5, recv)


def _chip_sum_a(tag, cs, frm, b0, r_prev=None):
    nb = cs.shape[1]
    sr = SUM_ROWS["a"]
    per = A_HALF // sr
    nt = nb * per

    def body(*refs):
        c_ref, b_ref = refs[:2]
        r_ref, buf, send_sems, recv_sem, local_sems = refs[-5:]
        x, y, c, _ = _mesh_pos()
        t = pl.program_id(0)
        slot = lax.rem(t, 2)

        def copies(k, tile):
            dst = r_ref.at[b0 + tile // per, c, pl.ds(pl.multiple_of(lax.rem(tile, per) * sr, 8), sr)]
            keep = pltpu.make_async_copy(buf.at[k], dst, local_sems.at[k])
            give = pltpu.make_async_remote_copy(src_ref=buf.at[k], dst_ref=dst, send_sem=send_sems.at[k],
                                                recv_sem=recv_sem, device_id=(x, y, 1 - c), device_id_type=MESH)
            return keep, give

        @pl.when(t >= 2)
        def _():
            keep, give = copies(slot, t - 2)
            keep.wait()
            give.wait_send()

        buf[slot] = ((c_ref[0, 0].astype(F32) + b_ref[0, 0].astype(F32))
                     + (b_ref[1, 0].astype(F32) + b_ref[2, 0].astype(F32)))
        keep, give = copies(slot, t)
        keep.start()
        give.start()

        @pl.when(t == nt - 1)
        def _():
            for back in (1, 0):
                keep, give = copies(lax.rem(t - back, 2), t - back)
                keep.wait()
                give.wait_send()
            landed = r_ref.at[pl.ds(b0, nb), 1 - c]
            pltpu.make_async_remote_copy(src_ref=landed, dst_ref=landed, send_sem=send_sems.at[0], recv_sem=recv_sem,
                                         device_id=(x, y, c), device_id_type=MESH).wait_recv()

    in_specs = [pl.BlockSpec((1, 1, sr, FF_BLK),
                             lambda t: (2 * lax.axis_index("x") + lax.axis_index("y"), t // per, t % per, 0)),
                pl.BlockSpec((3, 1, sr, FF_BLK), lambda t: (0, t // per, t % per, 0))]
    args = [cs, frm]
    if r_prev is not None:
        in_specs.append(HBM)
        args.append(r_prev)
    return pl.pallas_call(
        body, grid=(nt,), in_specs=in_specs, out_specs=HBM,
        out_shape=jax.ShapeDtypeStruct((4, 2, A_HALF, FF_BLK), F32),
        scratch_shapes=[pltpu.VMEM((2, sr, FF_BLK), F32), pltpu.SemaphoreType.DMA((2,)), pltpu.SemaphoreType.DMA,
                        pltpu.SemaphoreType.DMA((2,))],
        input_output_aliases={2: 0} if r_prev is not None else {},
        name="chip_sum_a" + tag, compiler_params=_params(1))(*args)


def _reduce_scatter(p5, cs_early, from_early, p_b):
    cs_late = _pair_sums_a("0", p5, 0, 1)
    cs_b = _pair_sum("b", p_b, _pair_send("b", p_b))
    from_late, from_b = _chip_exchange(cs_late, cs_b)
    red = _chip_sum_a("1", cs_early, from_early, 1)
    red = _chip_sum_a("0", cs_late, from_late, 0, red)
    return red.reshape(A_ROWS, FF_BLK), _chip_sum("b", cs_b, from_b)


SMALL_ROWS = 24


def _all_reduce_small(v):
    def body(v_ref, o_ref, all_ref, send_sems, recv_sems):
        x, y, c, _ = _mesh_pos()
        me = 4 * x + 2 * y + c
        all_ref[me] = v_ref[...]
        peers = [(x ^ ((k >> 2) & 1), y ^ ((k >> 1) & 1), c ^ (k & 1)) for k in range(1, 8)]
        idx = lambda p: 4 * p[0] + 2 * p[1] + p[2]
        send = [pltpu.make_async_remote_copy(src_ref=v_ref, dst_ref=all_ref.at[me], send_sem=send_sems.at[k],
                                             recv_sem=recv_sems.at[k], device_id=p, device_id_type=MESH)
                for k, p in enumerate(peers)]
        for cp in send:
            cp.start()
        for k, p in enumerate(peers):
            pltpu.make_async_remote_copy(src_ref=v_ref, dst_ref=all_ref.at[idx(p)], send_sem=send_sems.at[k],
                                         recv_sem=recv_sems.at[k], device_id=p, device_id_type=MESH).wait_recv()
        for cp in send:
            cp.wait_send()
        acc = all_ref[0]
        for d in range(1, 8):
            acc = acc + all_ref[d]
        o_ref[...] = acc

    vm = pl.BlockSpec(memory_space=pltpu.VMEM)
    return pl.pallas_call(
        body, name="all_reduce_small", in_specs=[vm], out_specs=vm,
        out_shape=jax.ShapeDtypeStruct((SMALL_ROWS, 1024), F32),
        scratch_shapes=[pltpu.VMEM((8, SMALL_ROWS, 1024), F32), pltpu.SemaphoreType.DMA((7,)),
                        pltpu.SemaphoreType.DMA((7,))],)(v)


def _adamw(name, w, g, m, v):
    rows, cols = w.shape
    tr = rows
    if rows * cols > 400_000:
        tr = max(t for t in range(8, rows, 8) if rows % t == 0 and t * cols <= 400_000)

    def body(w_ref, g_ref, m_ref, v_ref, d_ref, nm_ref, nv_ref):
        gv = g_ref[...]
        m_new = ADAM_B1 * m_ref[...] + (1.0 - ADAM_B1) * gv
        v_new = ADAM_B2 * v_ref[...] + (1.0 - ADAM_B2) * (gv * gv)
        m_hat = m_new / (1.0 - ADAM_B1 ** ADAM_STEP)
        v_hat = v_new / (1.0 - ADAM_B2 ** ADAM_STEP)
        d_ref[...] = -ADAM_LR * (m_hat / (jnp.sqrt(v_hat) + ADAM_EPS) + ADAM_WD * w_ref[...])
        nm_ref[...] = m_new
        nv_ref[...] = v_new

    spec = pl.BlockSpec((tr, cols), lambda i: (i, 0))
    sds = jax.ShapeDtypeStruct((rows, cols), F32)
    return pl.pallas_call(body, grid=(rows // tr,), in_specs=[spec] * 4, out_specs=[spec] * 3, out_shape=[sds] * 3,
                          name=name, compiler_params=_params(1))(w, g, m, v)


WEIGHTS = ("ffn1_norm", "ffn1_w_gu", "ffn1_w_down", "mix_norm", "ffn2_norm", "ffn2_w_gu", "ffn2_w_down",
           "a_w_in", "a_w_conv", "a_A_log", "a_dt_bias", "a_out_norm", "a_w_out",
           "b_w_in", "b_b_in", "b_sinks", "b_w_out", "b_b_out", "final_norm")
SMALL_SLOTS = {"ffn1_norm": (0, 2048), "mix_norm": (2048, 2048), "ffn2_norm": (4096, 2048), "final_norm": (6144, 1024),
               "a_A_log": (7168, 8), "a_dt_bias": (7296, 8), "a_out_norm": (7424, 128), "b_sinks": (7552, 16),
               "loss": (7680, 1)}
SMALL_SHARDED = {"a_w_conv": (8192, 8192, (4,), 3072), "b_b_in": (20480, 11264, (), 1536), "b_b_out": (22016, 11648, (), 1024)}
DEV_SMALL_ROWS = 12


def _pack_rows(parts):
    rows = []
    for p in parts:
        r = p.reshape(p.shape[0], -1, 1024)
        rows.append(jnp.pad(r, ((0, 0), (0, -r.shape[1] % PACK_TILE), (0, 0))))
    rows.append(jnp.zeros((parts[0].shape[0], PACK_ROWS - PACK_USED, 1024), parts[0].dtype))
    return jnp.concatenate(rows, axis=1)


def _place(vec, off, a):
    return lax.dynamic_update_slice(vec, a.reshape(-1).astype(F32), (off,))


def kernel(x, ffn1_norm, ffn1_w_gu, ffn1_w_down, mix_norm, ffn2_norm, ffn2_w_gu, ffn2_w_down, a_w_in, a_w_conv, a_A_log, a_dt_bias, a_out_norm, a_w_out, b_w_in, b_b_in, b_sinks, b_w_out, b_b_out, final_norm, loss_target, m_ffn1_norm, m_ffn1_w_gu, m_ffn1_w_down, m_mix_norm, m_ffn2_norm, m_ffn2_w_gu, m_ffn2_w_down, m_a_w_in, m_a_w_conv, m_a_A_log, m_a_dt_bias, m_a_out_norm, m_a_w_out, m_b_w_in, m_b_b_in, m_b_sinks, m_b_w_out, m_b_b_out, m_final_norm, v_ffn1_norm, v_ffn1_w_gu, v_ffn1_w_down, v_mix_norm, v_ffn2_norm, v_ffn2_w_gu, v_ffn2_w_down, v_a_w_in, v_a_w_conv, v_a_A_log, v_a_dt_bias, v_a_out_norm, v_a_w_out, v_b_w_in, v_b_b_in, v_b_sinks, v_b_w_out, v_b_b_out, v_final_norm):
    w = dict(zip(WEIGHTS, (ffn1_norm, ffn1_w_gu, ffn1_w_down, mix_norm, ffn2_norm, ffn2_w_gu, ffn2_w_down, a_w_in, a_w_conv,
                           a_A_log, a_dt_bias, a_out_norm, a_w_out, b_w_in, b_b_in, b_sinks, b_w_out, b_b_out, final_norm)))
    m = dict(zip(WEIGHTS, (m_ffn1_norm, m_ffn1_w_gu, m_ffn1_w_down, m_mix_norm, m_ffn2_norm, m_ffn2_w_gu, m_ffn2_w_down,
                           m_a_w_in, m_a_w_conv, m_a_A_log, m_a_dt_bias, m_a_out_norm, m_a_w_out, m_b_w_in, m_b_b_in,
                           m_b_sinks, m_b_w_out, m_b_b_out, m_final_norm)))
    v = dict(zip(WEIGHTS, (v_ffn1_norm, v_ffn1_w_gu, v_ffn1_w_down, v_mix_norm, v_ffn2_norm, v_ffn2_w_gu, v_ffn2_w_down,
                           v_a_w_in, v_a_w_conv, v_a_A_log, v_a_dt_bias, v_a_out_norm, v_a_w_out, v_b_w_in, v_b_b_in,
                           v_b_sinks, v_b_w_out, v_b_b_out, v_final_norm)))
    chip = 2 * lax.axis_index("x") + lax.axis_index("y")

    big_a4 = jnp.concatenate([w["ffn1_w_gu"], w["ffn2_w_gu"]], axis=0).astype(BF16).reshape(4, 2, A_HALF, FF_BLK)
    big_b = _pack_rows([w[n].astype(BF16).reshape(1, -1) for n, _ in PACK])[0]
    small = jnp.zeros((4096,), F32)
    small = _place(small, 0, w["a_w_conv"])
    small = _place(small, 3072, w["b_b_in"])
    small = _place(small, 3456, w["b_b_out"]).reshape(SMALL_SHARD)
    ra, rb, rs = _gather_chips(big_a4, big_b, small)
    ga = _fill_a(0, big_a4, ra).reshape(4, 4, D_MODEL, FF_BLK)
    gb = _gather_fill("b", big_b, rb)
    offs, o = {}, 0
    for n, r in PACK:
        offs[n] = (o, r)
        o += -(-r // PACK_TILE) * PACK_TILE
    blk = lambda n: gb[:, offs[n][0]:offs[n][0] + offs[n][1]]
    gsf = lax.dynamic_update_slice(jnp.zeros((4, 4096), F32), small.reshape(1, 4096), (chip, 0))
    for j, other in enumerate((chip ^ 2, chip ^ 1, chip ^ 3)):
        gsf = lax.dynamic_update_slice(gsf, rs[j].reshape(1, 4096), (other, 0))
    W = {n: w[n] for n in ("ffn1_norm", "ffn2_norm", "mix_norm", "a_out_norm")}
    W["ga"] = ga
    W["w_down"] = gb[:, 0:2816].reshape(4, 4, 704, 1024).transpose(1, 0, 2, 3).reshape(4, 2, FF_BLK, 1024)
    W["a_w_in"] = jnp.pad(blk("a_w_in").reshape(4, 1024, 1028).transpose(1, 0, 2).reshape(1024, A_IN_COLS),
                          ((0, 0), (0, A_IN_PAD - A_IN_COLS)))
    W["a_w_out"] = blk("a_w_out").reshape(1024, 1024)
    W["b_w_in"] = blk("b_w_in").reshape(4, 1024, 384).transpose(1, 0, 2).reshape(1024, 1536)
    W["b_w_out"] = blk("b_w_out").reshape(1024, 1024)
    W["a_w_conv"] = gsf[:, 0:3072].reshape(4, 4, 768).transpose(1, 0, 2).reshape(4, 3072)
    W["b_b_in"] = gsf[:, 3072:3456].reshape(1, 1536)
    W["b_b_out"] = gsf[:, 3456:3712].reshape(1, 1024)
    W["a_gate_p"] = jnp.pad(jnp.concatenate([-jnp.exp(w["a_A_log"]), w["a_dt_bias"]], axis=0), ((0, 0), (8, 112)))
    W["b_sinks"] = jnp.broadcast_to(w["b_sinks"][0][:, None, None], (B_HEADS, 1, 128))
    W["final_norm"] = w["final_norm"][None]

    loss_p, dx, g = _local_step(x[0], loss_target[0], W, big_a4)

    down = g["w_down"].reshape(4, 4, 704, 1024).transpose(1, 0, 2, 3)
    parts = [down[:, 0:2], down[:, 2:4],
             g["a_w_in"][:, :A_IN_COLS].reshape(1024, 4, 1028).transpose(1, 0, 2), g["a_w_out"].reshape(4, 256, 1024),
             g["b_w_in"].reshape(1024, 4, 384).transpose(1, 0, 2), g["b_w_out"].reshape(4, 256, 1024)]
    red_a, red_b = _reduce_scatter(g["ga"].reshape(4, 4, 2, A_HALF, FF_BLK), g["cs_early"], g["from_early"],
                                   _pack_rows([a.astype(BF16).reshape(4, -1) for a in parts]))
    grads = {n: red_b[offs[n][0]:offs[n][0] + offs[n][1]].reshape(w[n].shape) for n, _ in PACK}
    grads["ffn1_w_gu"] = red_a[:A_ROWS // 2].reshape(w["ffn1_w_gu"].shape)
    grads["ffn2_w_gu"] = red_a[A_ROWS // 2:].reshape(w["ffn2_w_gu"].shape)

    sv = jnp.zeros((SMALL_ROWS * 1024,), F32)
    small_g = {"ffn1_norm": g["ffn1_norm"], "mix_norm": g["mix_norm"], "ffn2_norm": g["ffn2_norm"], "final_norm": g["final_norm"],
               "a_A_log": g["a_gate_p"][0, 8:16], "a_dt_bias": g["a_gate_p"][1, 8:16], "a_out_norm": g["a_out_norm"],
               "b_sinks": g["b_sinks"], "loss": loss_p[0, 0:1]}
    for n, (off, _) in SMALL_SLOTS.items():
        sv = _place(sv, off, small_g[n])
    for n, (off, _, _, _) in SMALL_SHARDED.items():
        sv = _place(sv, off, g[n])
    tot = _all_reduce_small(sv.reshape(SMALL_ROWS, 1024)).reshape(-1)
    for n, (off, size) in SMALL_SLOTS.items():
        if n != "loss":
            grads[n] = tot[off:off + size].reshape(w[n].shape)
    for n, (off, _, lead, last) in SMALL_SHARDED.items():
        full = tot[off:off + (lead[0] if lead else 1) * last].reshape(lead + (last,))
        width = last // 4
        grads[n] = lax.dynamic_slice_in_dim(full, chip * width, width, axis=-1).reshape(w[n].shape)
    loss = tot[SMALL_SLOTS["loss"][0]]

    delta, new_m, new_v = {}, {}, {}
    for n in ("ffn1_w_gu", "ffn2_w_gu") + tuple(n for n, _ in PACK):
        two_d = lambda a: a.reshape(-1, a.shape[-1])
        d, nm, nv = _adamw("adamw_" + n, two_d(w[n]), two_d(grads[n]), two_d(m[n]), two_d(v[n]))
        delta[n], new_m[n], new_v[n] = d.reshape(w[n].shape), nm.reshape(w[n].shape), nv.reshape(w[n].shape)

    def dev_small(src):
        vec = jnp.zeros((DEV_SMALL_ROWS * 1024,), F32)
        for n, (off, _) in SMALL_SLOTS.items():
            if n != "loss":
                vec = _place(vec, off, src[n])
        for n, (_, off, _, _) in SMALL_SHARDED.items():
            vec = _place(vec, off, src[n])
        return vec.reshape(DEV_SMALL_ROWS, 1024)

    sd, sm, svv = _adamw("adamw_small", dev_small(w), dev_small(grads), dev_small(m), dev_small(v))
    for n in WEIGHTS:
        if n in SMALL_SLOTS:
            off, size = SMALL_SLOTS[n]
        elif n in SMALL_SHARDED:
            off, size = SMALL_SHARDED[n][1], w[n].size
        else:
            continue
        for dst, src in ((delta, sd), (new_m, sm), (new_v, svv)):
            dst[n] = src.reshape(-1)[off:off + size].reshape(w[n].shape)

    return (loss, dx[None], *[grads[n] for n in WEIGHTS], *[delta[n] for n in WEIGHTS],
            *[new_m[n] for n in WEIGHTS], *[new_v[n] for n in WEIGHTS])
```

```python
import jax
import jax.numpy as jnp
from jax import lax
from jax.experimental import pallas as pl
from jax.experimental.pallas import tpu as pltpu

F32 = jnp.float32
BF16 = jnp.bfloat16

D_MODEL = 1024
EPS = 1e-6
FF_BLK = 1408
A_HEADS = 8
A_DK = 128
A_CHUNK = 64
A_HG = 8
A_IN_COLS = 4112
A_IN_PAD = 4224
B_HEADS = 16
B_KV = 4
B_HD = 64
B_BLK = 128
ADAM_LR, ADAM_B1, ADAM_B2, ADAM_EPS, ADAM_WD, ADAM_STEP = 0.001, 0.9, 0.999, 1e-08, 0.01, 10
MESH = pl.DeviceIdType.MESH
VMEM_LIMIT = 56 * 1024 * 1024
HBM = pl.BlockSpec(memory_space=pl.ANY)


def _params(n_axes):
    return pltpu.CompilerParams(dimension_semantics=("arbitrary",) * n_axes, vmem_limit_bytes=VMEM_LIMIT)


def _sigmoid(x):
    return 1.0 / (1.0 + jnp.exp(-x))


def _dot(a, b, ca, cb):
    return lax.dot_general(a, b, (((ca,), (cb,)), ((), ())), preferred_element_type=F32)


def _dotb(a, b, ca=1, cb=0):
    return _dot(a.astype(BF16), b.astype(BF16), ca, cb)


def _dotx(a, b, ca=1, cb=0):
    return lax.dot_general(a, b, (((ca,), (cb,)), ((), ())), preferred_element_type=F32,
                           precision=lax.Precision.HIGHEST)


def _doth(a, b, ca=1, cb=0):
    return lax.dot_general(a, b, (((ca,), (cb,)), ((), ())), preferred_element_type=F32,
                           precision=lax.Precision.HIGH)


def _rms_fwd(name, x, w):
    T, D = x.shape
    tt = min(T, 512)

    def body(x_ref, w_ref, h_ref):
        xv = x_ref[...]
        r = lax.rsqrt(jnp.mean(xv * xv, axis=-1, keepdims=True) + EPS)
        h_ref[...] = (xv * r * w_ref[...]).astype(BF16)

    return pl.pallas_call(
        body, grid=(T // tt,),
        in_specs=[pl.BlockSpec((tt, D), lambda i: (i, 0)), pl.BlockSpec((1, D), lambda i: (0, 0))],
        out_specs=pl.BlockSpec((tt, D), lambda i: (i, 0)),
        out_shape=jax.ShapeDtypeStruct((T, D), BF16), name=name, compiler_params=_params(1))(x, w)


def _rms_bwd_tile(dh, xv, dy, w):
    r = lax.rsqrt(jnp.mean(xv * xv, axis=-1, keepdims=True) + EPS)
    xhat = xv * r
    dxhat = dh * w
    dx = dy + r * (dxhat - xhat * jnp.mean(dxhat * xhat, axis=-1, keepdims=True))
    return dx, jnp.sum(dh * xhat, axis=0, keepdims=True)


def _colsum(name, a):
    T, N = a.shape
    tt = min(T, 512)

    def body(a_ref, o_ref):
        @pl.when(pl.program_id(0) == 0)
        def _():
            o_ref[...] = jnp.zeros_like(o_ref)
        o_ref[...] += jnp.sum(a_ref[...].astype(F32), axis=0, keepdims=True)

    return pl.pallas_call(
        body, grid=(T // tt,), in_specs=[pl.BlockSpec((tt, N), lambda i: (i, 0))],
        out_specs=pl.BlockSpec((1, N), lambda i: (0, 0)),
        out_shape=jax.ShapeDtypeStruct((1, N), F32), name=name, compiler_params=_params(1))(a)


def _matmul(name, a, b, ca, cb, tm, tn, tk, extra_in, outs, epi, order="ji"):
    M, K, N = a.shape[1 - ca], a.shape[ca], b.shape[1 - cb]
    tm, tn, tk = min(tm, M), min(tn, N), min(tk, K)
    assert M % tm == 0 and N % tn == 0 and K % tk == 0, (name, M, N, K, tm, tn, tk)
    ni, nj, nk = M // tm, N // tn, K // tk
    if order == "ji":
        grid = (nj, ni, nk)
        perm = lambda g0, g1, g2: (g1, g0, g2)
    else:
        grid = (ni, nj, nk)
        perm = lambda g0, g1, g2: (g0, g1, g2)

    def wrap(f):
        return lambda g0, g1, g2: f(*perm(g0, g1, g2))

    a_spec = (pl.BlockSpec((tm, tk), wrap(lambda i, j, k: (i, k))) if ca == 1
              else pl.BlockSpec((tk, tm), wrap(lambda i, j, k: (k, i))))
    b_spec = (pl.BlockSpec((tk, tn), wrap(lambda i, j, k: (k, j))) if cb == 0
              else pl.BlockSpec((tn, tk), wrap(lambda i, j, k: (j, k))))
    ne, no = len(extra_in), len(outs)

    def body(*refs):
        a_ref, b_ref = refs[0], refs[1]
        ex, out = refs[2:2 + ne], refs[2 + ne:2 + ne + no]
        i, j, k = perm(pl.program_id(0), pl.program_id(1), pl.program_id(2))
        p = _dotb(a_ref[...], b_ref[...], ca, cb)
        if nk == 1:
            epi(p, ex, out, i, j)
        else:
            acc_ref = refs[-1]

            @pl.when(k == 0)
            def _():
                acc_ref[...] = p

            @pl.when(k > 0)
            def _():
                acc_ref[...] += p

            @pl.when(k == nk - 1)
            def _():
                epi(acc_ref[...], ex, out, i, j)

    return pl.pallas_call(
        body, grid=grid,
        in_specs=[a_spec, b_spec] + [pl.BlockSpec(bs, wrap(f)) for _, bs, f in extra_in],
        out_specs=[pl.BlockSpec(bs, wrap(f)) for _, bs, f in outs],
        out_shape=[s for s, _, _ in outs],
        scratch_shapes=[pltpu.VMEM((tm, tn), F32)] if nk > 1 else [],
        name=name, compiler_params=_params(3))(a, b, *[x for x, _, _ in extra_in])


def _mm_plain(name, a, b, ca, cb, out_dtype, tm=1024, tn=1024, tk=1024, scale=1.0, bias=None):
    M, N = a.shape[1 - ca], b.shape[1 - cb]
    tm, tn = min(tm, M), min(tn, N)
    extra = [] if bias is None else [(bias, (1, tn), lambda i, j, k: (0, j))]

    def epi(acc, ex, out, i, j):
        r = acc * scale if scale != 1.0 else acc
        if bias is not None:
            r = r + ex[0][...]
        out[0][...] = r.astype(out_dtype)

    return _matmul(name, a, b, ca, cb, tm, tn, tk, extra,
                   [(jax.ShapeDtypeStruct((M, N), out_dtype), (tm, tn), lambda i, j, k: (i, j))], epi)[0]


def _mm_residual(name, a, b, x, scale, bias=None, tk=1024):
    M, N = x.shape
    tm, tn = min(512, M), N
    extra = [(x, (tm, tn), lambda i, j, k: (i, j))]
    if bias is not None:
        extra.append((bias, (1, tn), lambda i, j, k: (0, j)))

    def epi(acc, ex, out, i, j):
        r = acc if bias is None else acc + ex[1][...]
        out[0][...] = ex[0][...] + scale * r

    return _matmul(name, a, b, 1, 0, tm, tn, tk, extra,
                   [(jax.ShapeDtypeStruct((M, N), F32), (tm, tn), lambda i, j, k: (i, j))], epi, order="ij")[0]


def _mm_rms_bwd(name, dproj, w_in, x, dy, nw, tk, out_scale):
    M, N = x.shape
    tm = min(512, M)
    extra = [(x, (tm, N), lambda i, j, k: (i, 0)), (dy, (tm, N), lambda i, j, k: (i, 0)),
             (nw, (1, N), lambda i, j, k: (0, 0))]

    def epi(acc, ex, out, i, j):
        dx, dw = _rms_bwd_tile(acc, ex[0][...], ex[1][...], ex[2][...])
        out[0][...] = dx
        out[1][...] = (dx * out_scale).astype(BF16)

        @pl.when(i == 0)
        def _():
            out[2][...] = dw

        @pl.when(i > 0)
        def _():
            out[2][...] += dw

    return _matmul(name, dproj, w_in, 1, 1, tm, N, tk, extra,
                   [(jax.ShapeDtypeStruct((M, N), F32), (tm, N), lambda i, j, k: (i, 0)),
                    (jax.ShapeDtypeStruct((M, N), BF16), (tm, N), lambda i, j, k: (i, 0)),
                    (jax.ShapeDtypeStruct((1, N), F32), (1, N), lambda i, j, k: (0, 0))], epi, order="ij")


def _peer_copies(pairs, send_sems, recv_sems):
    x, y, c, chips = _mesh_pos()
    n = len(pairs)
    return [pltpu.make_async_remote_copy(src_ref=src(chip), dst_ref=dst.at[j], send_sem=send_sems.at[n * j + k],
                                         recv_sem=recv_sems.at[n * j + k], device_id=(*chip, c), device_id_type=MESH)
            for j, chip in enumerate(chips) for k, (src, dst) in enumerate(pairs)]


def _ffn_gu(name, x, nw, ga, blk, exchange=None):
    T, D = x.shape
    tm = min(T, 512)
    rs = min(tm, 256)
    ni = T // tm

    def body(*refs):
        x_ref, nw_ref, wg0, wg1, wu0, wu1 = refs[:6]
        if exchange is None:
            h_ref, gu_ref, act_ref = refs[6:]
        else:
            bb_ref, sm_ref, h_ref, gu_ref, act_ref, rb_ref, rs_ref, send_sems, recv_sems = refs[6:]
            c = lax.axis_index("c")
            pairs = [(lambda chip: bb_ref.at[_half(PACK_ROWS, c)], rb_ref), (lambda chip: sm_ref, rs_ref)]

            @pl.when(pl.program_id(0) == 0)
            def _():
                for cp in _peer_copies(pairs, send_sems, recv_sems):
                    cp.start()

            @pl.when(pl.program_id(0) == ni - 1)
            def _():
                for cp in _peer_copies(pairs, send_sems, recv_sems):
                    cp.wait_recv()
                    cp.wait_send()

        for r in range(tm // rs):
            rows = pl.ds(r * rs, rs)
            xv = x_ref[rows, :]
            hv = (xv * lax.rsqrt(jnp.mean(xv * xv, axis=-1, keepdims=True) + EPS) * nw_ref[...]).astype(BF16)
            h_ref[rows, :] = hv
            for j, (wg_ref, wu_ref) in enumerate(((wg0, wu0), (wg1, wu1))):
                g = _dot(hv, wg_ref[0, 0], 1, 0)
                u = _dot(hv, wu_ref[0, 0], 1, 0)
                s = _sigmoid(g)
                gs = g * s
                gu_ref[0, j, rows, :] = (u * (s + gs * (1.0 - s))).astype(BF16)
                gu_ref[1, j, rows, :] = gs.astype(BF16)
                act_ref[j, rows, :] = (gs * u).astype(BF16)

    wspec = lambda q: pl.BlockSpec((1, 1, D, FF_BLK), lambda i: (q, blk, 0, 0), pipeline_mode=pl.Buffered(1))
    in_specs = [pl.BlockSpec((tm, D), lambda i: (i, 0)), pl.BlockSpec((1, D), lambda i: (0, 0)),
                wspec(0), wspec(1), wspec(2), wspec(3)]
    out_specs = [pl.BlockSpec((tm, D), lambda i: (i, 0)), pl.BlockSpec((2, 2, tm, FF_BLK), lambda i: (0, 0, i, 0)),
                 pl.BlockSpec((2, tm, FF_BLK), lambda i: (0, i, 0))]
    out_shape = [jax.ShapeDtypeStruct((T, D), BF16), jax.ShapeDtypeStruct((2, 2, T, FF_BLK), BF16),
                 jax.ShapeDtypeStruct((2, T, FF_BLK), BF16)]
    args, scratch = [x, nw, ga, ga, ga, ga], []
    if exchange is not None:
        big_b, small = exchange
        in_specs += [HBM, HBM]
        out_specs += [HBM, HBM]
        out_shape += [jax.ShapeDtypeStruct((3, big_b.shape[0] // 2, big_b.shape[1]), BF16),
                      jax.ShapeDtypeStruct((3,) + small.shape, F32)]
        scratch = [pltpu.SemaphoreType.DMA((6,)), pltpu.SemaphoreType.DMA((6,))]
        args += [big_b, small]
    return pl.pallas_call(body, grid=(ni,), in_specs=in_specs, out_specs=out_specs, out_shape=out_shape,
                          scratch_shapes=scratch, name=name, compiler_params=_params(1))(*args)


def _ffn_down(name, act, wd, x):
    T, D = x.shape
    tm = min(T, 512)

    def body(act_ref, wd_ref, x_ref, o_ref):
        acc = _dot(act_ref[0], wd_ref[0], 1, 0) + _dot(act_ref[1], wd_ref[1], 1, 0)
        o_ref[...] = x_ref[...] + 0.5 * acc

    return pl.pallas_call(
        body, grid=(T // tm,),
        in_specs=[pl.BlockSpec((2, tm, FF_BLK), lambda i: (0, i, 0)),
                  pl.BlockSpec((2, FF_BLK, D), lambda i: (0, 0, 0)),
                  pl.BlockSpec((tm, D), lambda i: (i, 0))],
        out_specs=pl.BlockSpec((tm, D), lambda i: (i, 0)),
        out_shape=jax.ShapeDtypeStruct((T, D), F32), name=name, compiler_params=_params(1))(act, wd, x)


def _ffn_dact(name, dyh, wd, gu):
    T, D = dyh.shape
    tm = min(T, 1024)
    rs = min(tm, 256)

    def body(dy_ref, wd_ref, gu_ref, o_ref):
        for r in range(tm // rs):
            rows = pl.ds(r * rs, rs)
            dact = _dot(dy_ref[rows, :], wd_ref[0], 1, 1)
            o_ref[0, 0, rows, :] = (dact * gu_ref[0, 0, rows, :].astype(F32)).astype(BF16)
            o_ref[1, 0, rows, :] = (dact * gu_ref[1, 0, rows, :].astype(F32)).astype(BF16)

    return pl.pallas_call(
        body, grid=(2, T // tm),
        in_specs=[pl.BlockSpec((tm, D), lambda j, i: (i, 0)),
                  pl.BlockSpec((1, FF_BLK, D), lambda j, i: (j, 0, 0)),
                  pl.BlockSpec((2, 1, tm, FF_BLK), lambda j, i: (0, j, i, 0))],
        out_specs=pl.BlockSpec((2, 1, tm, FF_BLK), lambda j, i: (0, j, i, 0)),
        out_shape=jax.ShapeDtypeStruct((2, 2, T, FF_BLK), BF16), name=name, compiler_params=_params(2))(dyh, wd, gu)


def _ffn_dwd(name, act, dyh):
    _, T, _ = act.shape
    D = dyh.shape[1]
    tk = min(T, 2048)
    nk = T // tk

    def body(a_ref, d_ref, o_ref, acc_ref):
        k = pl.program_id(1)
        p = _dot(a_ref[0], d_ref[...], 0, 0)

        @pl.when(k == 0)
        def _():
            acc_ref[...] = p

        @pl.when(k > 0)
        def _():
            acc_ref[...] += p

        @pl.when(k == nk - 1)
        def _():
            o_ref[0] = acc_ref[...].astype(BF16)

    return pl.pallas_call(
        body, grid=(2, nk),
        in_specs=[pl.BlockSpec((1, tk, FF_BLK), lambda j, k: (j, k, 0)), pl.BlockSpec((tk, D), lambda j, k: (k, 0))],
        out_specs=pl.BlockSpec((1, FF_BLK, D), lambda j, k: (j, 0, 0)),
        out_shape=jax.ShapeDtypeStruct((2, FF_BLK, D), BF16), scratch_shapes=[pltpu.VMEM((FF_BLK, D), F32)],
        name=name, compiler_params=_params(2))(act, dyh)


def _ffn_dwgu(name, h, dgu, pa, blk):
    T, D = h.shape
    tk = min(T, 2048)
    nk = T // tk

    def body(h_ref, d_ref, pa_in, o_ref, acc_ref):
        k = pl.program_id(1)
        p = _dot(h_ref[...], d_ref[0, 0], 0, 0)

        @pl.when(k == 0)
        def _():
            acc_ref[...] = p

        @pl.when(k > 0)
        def _():
            acc_ref[...] += p

        @pl.when(k == nk - 1)
        def _():
            o_ref[0, 0] = acc_ref[...].astype(BF16)

    return pl.pallas_call(
        body, grid=(4, nk),
        in_specs=[pl.BlockSpec((tk, D), lambda q, k: (k, 0)),
                  pl.BlockSpec((1, 1, tk, FF_BLK), lambda q, k: (q // 2, q % 2, k, 0)), HBM],
        out_specs=pl.BlockSpec((1, 1, D, FF_BLK), lambda q, k: (q, blk, 0, 0)),
        out_shape=jax.ShapeDtypeStruct(pa.shape, BF16), scratch_shapes=[pltpu.VMEM((D, FF_BLK), F32)],
        input_output_aliases={2: 0}, name=name, compiler_params=_params(2))(h, dgu, pa)


def _ffn_dx(name, dgu, ga, blk, x, dy, nw, out_scale, exchange=None):
    T, D = x.shape
    tm = min(T, 512)
    ni = T // tm

    def body(*refs):
        d_ref, w0, w1, w2, w3, x_ref, dy_ref, nw_ref = refs[:8]
        if exchange is None:
            dx_ref, dxb_ref, dnw_ref = refs[8:]
        else:
            ca_ref, cb_ref, dx_ref, dxb_ref, dnw_ref, fa_ref, fb_ref, send_sems, recv_sems = refs[8:]
            pairs = [(lambda chip: ca_ref.at[2 * chip[0] + chip[1]], fa_ref),
                     (lambda chip: cb_ref.at[2 * chip[0] + chip[1]], fb_ref)]

            @pl.when(pl.program_id(0) == 0)
            def _():
                for cp in _peer_copies(pairs, send_sems, recv_sems):
                    cp.start()

            @pl.when(pl.program_id(0) == ni - 1)
            def _():
                for cp in _peer_copies(pairs, send_sems, recv_sems):
                    cp.wait_recv()
                    cp.wait_send()

        i = pl.program_id(0)
        acc = (_dot(d_ref[0, 0], w0[0, 0], 1, 1) + _dot(d_ref[0, 1], w1[0, 0], 1, 1)
               + _dot(d_ref[1, 0], w2[0, 0], 1, 1) + _dot(d_ref[1, 1], w3[0, 0], 1, 1))
        dx, dw = _rms_bwd_tile(acc, x_ref[...], dy_ref[...], nw_ref[...])
        dx_ref[...] = dx
        dxb_ref[...] = (dx * out_scale).astype(BF16)

        @pl.when(i == 0)
        def _():
            dnw_ref[...] = dw

        @pl.when(i > 0)
        def _():
            dnw_ref[...] += dw

    wspec = lambda q: pl.BlockSpec((1, 1, D, FF_BLK), lambda i: (q, blk, 0, 0), pipeline_mode=pl.Buffered(1))
    row = pl.BlockSpec((tm, D), lambda i: (i, 0))
    in_specs = [pl.BlockSpec((2, 2, tm, FF_BLK), lambda i: (0, 0, i, 0)), wspec(0), wspec(1), wspec(2), wspec(3),
                row, row, pl.BlockSpec((1, D), lambda i: (0, 0))]
    out_specs = [row, row, pl.BlockSpec((1, D), lambda i: (0, 0))]
    out_shape = [jax.ShapeDtypeStruct((T, D), F32), jax.ShapeDtypeStruct((T, D), BF16), jax.ShapeDtypeStruct((1, D), F32)]
    args, scratch = [dgu, ga, ga, ga, ga, x, dy, nw], []
    if exchange is not None:
        in_specs += [HBM, HBM]
        out_specs += [HBM, HBM]
        out_shape += [jax.ShapeDtypeStruct((3,) + cs.shape[1:], BF16) for cs in exchange]
        scratch = [pltpu.SemaphoreType.DMA((6,)), pltpu.SemaphoreType.DMA((6,))]
        args += list(exchange)
    return pl.pallas_call(body, grid=(ni,), in_specs=in_specs, out_specs=out_specs, out_shape=out_shape,
                          scratch_shapes=scratch, name=name, compiler_params=_params(1))(*args)


def _ffn_fwd(tag, x, nw, ga, blk, wd):
    h, gu, act = _ffn_gu(tag + "_gu", x, nw, ga, blk)
    return _ffn_down(tag + "_down", act, wd, x), (h, gu, act)


def _ffn_bwd(tag, dy, dyh, x, nw, ga, blk, wd, saved, pa, out_scale):
    h, gu, act = saved
    dgu = _ffn_dact(tag + "_dact", dyh, wd, gu)
    d_wd = _ffn_dwd(tag + "_dwd", act, dyh)
    pa = _ffn_dwgu(tag + "_dwgu", h, dgu, pa, blk)
    dx, dxb, d_nw = _ffn_dx(tag + "_dx", dgu, ga, blk, x, dy, nw, out_scale)
    return dx, dxb, d_nw, pa, d_wd


def _conv_taps(cur, halo, w, first, sign):
    tt = cur.shape[0]
    halo = jnp.where(first, 0.0, halo)
    rid = lax.broadcasted_iota(jnp.int32, (8, cur.shape[1]), 0)
    acc = w[3:4, :] * cur
    for s in (1, 2, 3):
        if sign < 0:
            sh = pltpu.roll(cur, s, 0)
            edge = jnp.where(rid < s, pltpu.roll(halo, s, 0), sh[0:8])
            sh = jnp.concatenate([edge, sh[8:]], axis=0) if tt > 8 else edge
        else:
            sh = pltpu.roll(cur, tt - s, 0)
            edge = jnp.where(rid >= 8 - s, pltpu.roll(halo, 8 - s, 0), sh[tt - 8:])
            sh = jnp.concatenate([sh[:tt - 8], edge], axis=0) if tt > 8 else edge
        acc = acc + w[3 - s:4 - s, :] * sh
    return acc


def _gdn_prep(name, proj, wconv, gate_p):
    T = proj.shape[0]
    tt = min(T, 256)
    hb = tt // 8
    nch = tt // A_CHUNK

    def body(cur_ref, halo_ref, ba_ref, w_ref, gp_ref, qkv_ref, bg_ref, gc_ref):
        first = pl.program_id(0) == 0
        for c in range(24):
            cols = pl.ds(c * 128, 128)
            conv = _conv_taps(cur_ref[:, cols], halo_ref[:, cols], w_ref[:, cols], first, -1)
            y = conv * _sigmoid(conv)
            if c < 16:
                y = y * lax.rsqrt(jnp.sum(y * y, axis=-1, keepdims=True) + EPS)
                if c < 8:
                    y = y * (A_DK ** -0.5)
            qkv_ref[:, cols] = y
        ba = ba_ref[...]
        lane = lax.broadcasted_iota(jnp.int32, ba.shape, 1)
        zarg = ba + gp_ref[1:2, :]
        softplus = jnp.maximum(zarg, 0.0) + jnp.log(1.0 + jnp.exp(-jnp.abs(zarg)))
        bg = jnp.where(lane < 8, _sigmoid(ba), jnp.where(lane < 16, gp_ref[0:1, :] * softplus, 0.0))
        bg_ref[...] = bg
        tri = (lax.broadcasted_iota(jnp.int32, (A_CHUNK, A_CHUNK), 0)
               >= lax.broadcasted_iota(jnp.int32, (A_CHUNK, A_CHUNK), 1)).astype(F32)
        for c in range(nch):
            rows = pl.ds(c * A_CHUNK, A_CHUNK)
            gc_ref[rows, :] = _dotx(tri, bg[c * A_CHUNK:(c + 1) * A_CHUNK, :])

    return pl.pallas_call(
        body, grid=(T // tt,),
        in_specs=[pl.BlockSpec((tt, 3072), lambda i: (i, 0)),
                  pl.BlockSpec((8, 3072), lambda i: (jnp.maximum(i * hb - 1, 0), 0)),
                  pl.BlockSpec((tt, 128), lambda i: (i, 32)),
                  pl.BlockSpec((4, 3072), lambda i: (0, 0)),
                  pl.BlockSpec((2, 128), lambda i: (0, 0))],
        out_specs=[pl.BlockSpec((tt, 3072), lambda i: (i, 0)), pl.BlockSpec((tt, 128), lambda i: (i, 0)),
                   pl.BlockSpec((tt, 128), lambda i: (i, 0))],
        out_shape=[jax.ShapeDtypeStruct((T, 3072), F32), jax.ShapeDtypeStruct((T, 128), F32),
                   jax.ShapeDtypeStruct((T, 128), F32)],
        name=name, compiler_params=_params(1))(proj, proj, proj, wconv, gate_p)


def _chunk_masks():
    ri = lax.broadcasted_iota(jnp.int32, (A_CHUNK, A_CHUNK), 0)
    ci = lax.broadcasted_iota(jnp.int32, (A_CHUNK, A_CHUNK), 1)
    return ri >= ci, ri > ci, ri == ci


def _chunk_local(q, k, gcol, grow, bcol):
    incl, strict, _ = _chunk_masks()
    dec = jnp.where(incl, jnp.exp(jnp.where(incl, gcol - grow, 0.0)), 0.0)
    e = jnp.exp(gcol)
    glast = grow[:, A_CHUNK - 1:A_CHUNK]
    f = jnp.exp(glast - gcol)
    gl = jnp.exp(glast)
    kb = k * bcol
    lmat = jnp.where(strict, _dotb(kb, k, 1, 1) * dec, 0.0)
    amat = jnp.where(incl, _dotb(q, k, 1, 1) * dec, 0.0)
    return dec, e, f, gl, kb, lmat, amat


def _unit_lower_inverse(lmats):
    _, _, eye = _chunk_masks()
    ts = [jnp.where(eye, 1.0, 0.0) - lm for lm in lmats]
    lps = [_doth(lm, lm) for lm in lmats]
    for it in range(5):
        ts = [t + _doth(t, lp) for t, lp in zip(ts, lps)]
        if it < 4:
            lps = [_doth(lp, lp) for lp in lps]
    return ts


def _gate_columns(bgt, gct):
    sel = (lax.broadcasted_iota(jnp.int32, (16, 128), 0) == lax.broadcasted_iota(jnp.int32, (16, 128), 1)).astype(F32)
    g_rows = _dotx(sel, gct, 1, 1)
    hs = range(A_HEADS)
    return ([bgt[:, h:h + 1] for h in hs], [gct[:, 8 + h:9 + h] for h in hs], [g_rows[8 + h:9 + h, :] for h in hs])


def _gdn_delta_fwd(name, qkv, bg, gcum, a4=None):
    T = qkv.shape[0]
    tt = min(T, 512)
    nch = tt // A_CHUNK
    NC = T // A_CHUNK
    ni = T // tt
    wd, ng = 128 * A_HG, A_HEADS // A_HG
    assert ng == 1

    def body(*refs):
        q_ref, k_ref, v_ref, bg_ref, gc_ref = refs[:5]
        if a4 is None:
            o_ref, s_ref, t_ref, u_ref, w_ref, state = refs[5:]
        else:
            a4_ref, o_ref, s_ref, t_ref, u_ref, w_ref, ra_ref, state, send_sems, recv_sems = refs[5:]

            @pl.when(pl.program_id(1) == 0)
            def _():
                _later_blocks_start(a4_ref, ra_ref, send_sems, recv_sems)

        @pl.when(pl.program_id(1) == 0)
        def _():
            state[...] = jnp.zeros_like(state)

        def chunk(c, carry):
            rows = pl.ds(pl.multiple_of(c * A_CHUNK, A_CHUNK), A_CHUNK)
            hs = range(A_HG)
            cols = [pl.ds(h * 128, 128) for h in hs]
            q = [q_ref[rows, cols[h]] for h in hs]
            k = [k_ref[rows, cols[h]] for h in hs]
            v = [v_ref[rows, cols[h]] for h in hs]
            bcl, gcl, grw = _gate_columns(bg_ref[rows, :], gc_ref[rows, :])
            loc = [_chunk_local(q[h], k[h], gcl[h], grw[h], bcl[h]) for h in hs]
            e, f, gl, kb, amat = ([l[i] for l in loc] for i in (1, 2, 3, 4, 6))
            tinv = _unit_lower_inverse([l[5] for l in loc])
            u = [_doth(tinv[h], v[h] * bcl[h]) for h in hs]
            w = [_doth(tinv[h], kb[h] * e[h]) for h in hs]
            s = [state[h] for h in hs]
            vn = [u[h] - _dotb(w[h], s[h]) for h in hs]
            o_s = [_dotb(q[h] * e[h], s[h]) for h in hs]
            o_a = [_dotb(amat[h], vn[h]) for h in hs]
            s_new = [s[h] * gl[h] + _dotb(k[h] * f[h], vn[h], 0, 0) for h in hs]
            for h in hs:
                s_ref[h, c] = s[h].astype(BF16)
                t_ref[h, c] = tinv[h]
                u_ref[rows, cols[h]] = u[h]
                w_ref[rows, cols[h]] = w[h]
                o_ref[rows, cols[h]] = o_s[h] + o_a[h]
                state[h] = s_new[h]
            return carry

        lax.fori_loop(0, nch, chunk, 0)

        if a4 is not None:
            @pl.when(pl.program_id(1) == ni - 1)
            def _():
                x, y, c, chips = _mesh_pos()
                for j, chip in enumerate(chips):
                    cp = pltpu.make_async_remote_copy(src_ref=a4_ref.at[pl.ds(1, 3), c], dst_ref=ra_ref.at[j],
                                                      send_sem=send_sems.at[j], recv_sem=recv_sems.at[j],
                                                      device_id=(*chip, c), device_id_type=MESH)
                    cp.wait_recv()
                    cp.wait_send()

    hd = lambda col: pl.BlockSpec((tt, wd), lambda h, i: (i, col * ng + h))
    gate_spec = pl.BlockSpec((tt, 128), lambda h, i: (i, 0))
    in_specs = [hd(0), hd(1), hd(2), gate_spec, gate_spec]
    out_specs = [pl.BlockSpec((tt, wd), lambda h, i: (i, h)),
                 pl.BlockSpec((A_HG, nch, 128, 128), lambda h, i: (h, i, 0, 0)),
                 pl.BlockSpec((A_HG, nch, A_CHUNK, A_CHUNK), lambda h, i: (h, i, 0, 0)),
                 pl.BlockSpec((tt, wd), lambda h, i: (i, h)), pl.BlockSpec((tt, wd), lambda h, i: (i, h))]
    out_shape = [jax.ShapeDtypeStruct((T, 1024), F32), jax.ShapeDtypeStruct((A_HEADS, NC, 128, 128), BF16),
                 jax.ShapeDtypeStruct((A_HEADS, NC, A_CHUNK, A_CHUNK), F32),
                 jax.ShapeDtypeStruct((T, 1024), F32), jax.ShapeDtypeStruct((T, 1024), F32)]
    scratch = [pltpu.VMEM((A_HG, 128, 128), F32)]
    args = [qkv, qkv, qkv, bg, gcum]
    if a4 is not None:
        in_specs.append(HBM)
        out_specs.append(HBM)
        out_shape.append(jax.ShapeDtypeStruct((3, 3, A_HALF, FF_BLK), BF16))
        scratch += [pltpu.SemaphoreType.DMA((3,)), pltpu.SemaphoreType.DMA((3,))]
        args.append(a4)
    return pl.pallas_call(body, grid=(ng, ni), in_specs=in_specs, out_specs=out_specs, out_shape=out_shape,
                          scratch_shapes=scratch, name=name, compiler_params=_params(2))(*args)


def _grads_exchange(cs_ref, from_ref, send_sems, recv_sems):
    x, y, c, chips = _mesh_pos()
    return [pltpu.make_async_remote_copy(src_ref=cs_ref.at[2 * chip[0] + chip[1]], dst_ref=from_ref.at[j],
                                         send_sem=send_sems.at[j], recv_sem=recv_sems.at[j],
                                         device_id=(*chip, c), device_id_type=MESH)
            for j, chip in enumerate(chips)]


def _gdn_delta_bwd(name, qkv, bg, gcum, d_o, s_sv, t_sv, u_sv, w_sv, cs_early=None):
    T = qkv.shape[0]
    tt = min(T, 512)
    nch = tt // A_CHUNK
    ni = T // tt

    def body(*refs):
        q_ref, k_ref, v_ref, bg_ref, gc_ref, do_ref, s_ref, t_ref, u_ref, w_ref = refs[:10]
        if cs_early is None:
            dq_ref, dk_ref, dv_ref, dbg_ref, dstate = refs[10:]
        else:
            cs_ref, dq_ref, dk_ref, dv_ref, dbg_ref, from_ref, dstate, send_sems, recv_sems = refs[10:]

            @pl.when(pl.program_id(1) == 0)
            def _():
                for cp in _grads_exchange(cs_ref, from_ref, send_sems, recv_sems):
                    cp.start()

        @pl.when(pl.program_id(1) == 0)
        def _():
            dstate[...] = jnp.zeros_like(dstate)

        incl, strict, _ = _chunk_masks()
        upper = (lax.broadcasted_iota(jnp.int32, (A_CHUNK, A_CHUNK), 0)
                 <= lax.broadcasted_iota(jnp.int32, (A_CHUNK, A_CHUNK), 1)).astype(F32)
        last_row = lax.broadcasted_iota(jnp.int32, (A_CHUNK, 1), 0) == A_CHUNK - 1
        ones = jnp.ones((A_CHUNK, 128), F32)

        rsum = lambda x: jnp.sum(x, axis=1, keepdims=True)

        def chunk(cc, carry):
            c = nch - 1 - cc
            rows = pl.ds(pl.multiple_of(c * A_CHUNK, A_CHUNK), A_CHUNK)
            bcl, gcl, grw = _gate_columns(bg_ref[rows, :], gc_ref[rows, :])
            lane = lax.broadcasted_iota(jnp.int32, (A_CHUNK, 128), 1)
            dbg = jnp.zeros((A_CHUNK, 128), F32)
            for first in range(0, A_HG, 4):
                hs = range(first, first + 4)
                cols = {h: pl.ds(h * 128, 128) for h in hs}
                q = {h: q_ref[rows, cols[h]] for h in hs}
                k = {h: k_ref[rows, cols[h]] for h in hs}
                v = {h: v_ref[rows, cols[h]] for h in hs}
                do = {h: do_ref[rows, cols[h]] for h in hs}
                u = {h: u_ref[rows, cols[h]] for h in hs}
                w = {h: w_ref[rows, cols[h]] for h in hs}
                s = {h: s_ref[h, c] for h in hs}
                tinv = {h: t_ref[h, c] for h in hs}
                ds = {h: dstate[h] for h in hs}
                loc = {h: _chunk_local(q[h], k[h], gcl[h], grw[h], bcl[h]) for h in hs}
                dec, e, f, gl, kb, lmat, amat = ({h: loc[h][i] for h in hs} for i in range(7))
                qd = {h: q[h] * e[h] for h in hs}
                kd = {h: k[h] * f[h] for h in hs}
                ke = {h: kb[h] * e[h] for h in hs}
                vn = {h: u[h] - _dotb(w[h], s[h]) for h in hs}
                d_qd = {h: _dotb(do[h], s[h], 1, 1) for h in hs}
                d_a = {h: jnp.where(incl, _dotb(do[h], vn[h], 1, 1), 0.0) for h in hs}
                d_vn1 = {h: _dotb(amat[h], do[h], 0, 0) for h in hs}
                d_vn = {h: d_vn1[h] + _dotb(kd[h], ds[h]) for h in hs}
                d_kd = {h: _dotb(vn[h], ds[h], 1, 1) for h in hs}
                d_w = {h: -_dotb(d_vn[h], s[h], 1, 1) for h in hs}
                ds_q = {h: _dotb(qd[h], do[h], 0, 0) for h in hs}
                ds_w = {h: _dotb(w[h], d_vn[h], 0, 0) for h in hs}
                d_bv = {h: _doth(tinv[h], d_vn[h], 0, 0) for h in hs}
                d_ke = {h: _doth(tinv[h], d_w[h], 0, 0) for h in hs}
                d_l1 = {h: _dotb(d_bv[h], u[h], 1, 1) for h in hs}
                d_l = {h: -jnp.where(strict, d_l1[h] + _dotb(d_ke[h], w[h], 1, 1), 0.0) for h in hs}
                d_kk = {h: d_l[h] * dec[h] for h in hs}
                d_qk = {h: d_a[h] * dec[h] for h in hs}
                d_kb = {h: _dotb(d_kk[h], k[h]) for h in hs}
                dk1 = {h: _dotb(d_kk[h], kb[h], 0, 0) for h in hs}
                dk2 = {h: _dotb(d_qk[h], q[h], 0, 0) for h in hs}
                dq1 = {h: _dotb(d_qk[h], k[h]) for h in hs}
                m = {h: d_l[h] * lmat[h] + d_a[h] * amat[h] for h in hs}
                col_m = {h: _dotx(m[h], ones, 0, 0)[:, 0:1] for h in hs}
                d_gc = {}
                for h in hs:
                    d_gl = jnp.sum(jnp.sum(ds[h] * s[h].astype(F32), axis=1, keepdims=True), axis=0, keepdims=True)
                    r_kd = rsum(d_kd[h] * kd[h])
                    tail = jnp.sum(r_kd, axis=0, keepdims=True) + d_gl * gl[h]
                    d_gc[h] = (rsum(m[h]) - col_m[h] + rsum(d_qd[h] * qd[h]) - r_kd + rsum(d_ke[h] * ke[h])
                               + jnp.where(last_row, tail, 0.0))
                dg = {h: _dotx(upper, d_gc[h] * ones)[:, 0:1] for h in hs}
                for h in hs:
                    dstate[h] = gl[h] * ds[h] + ds_q[h] - ds_w[h]
                    dk_ref[rows, cols[h]] = (dk1[h] + dk2[h] + d_kd[h] * f[h] + d_ke[h] * (bcl[h] * e[h])
                                             + d_kb[h] * bcl[h])
                    dq_ref[rows, cols[h]] = dq1[h] + d_qd[h] * e[h]
                    dv_ref[rows, cols[h]] = d_bv[h] * bcl[h]
                    d_beta = rsum(d_ke[h] * k[h]) * e[h] + rsum(d_kb[h] * k[h]) + rsum(d_bv[h] * v[h])
                    dbg = jnp.where(lane == h, d_beta, jnp.where(lane == 8 + h, dg[h], dbg))
            dbg_ref[rows, :] = dbg
            return carry

        lax.fori_loop(0, nch, chunk, 0)

        if cs_early is not None:
            @pl.when(pl.program_id(1) == ni - 1)
            def _():
                for cp in _grads_exchange(cs_ref, from_ref, send_sems, recv_sems):
                    cp.wait_recv()
                    cp.wait_send()

    wd, ng = 128 * A_HG, A_HEADS // A_HG
    assert ng == 1
    rev = lambda i: ni - 1 - i
    hd = lambda col: pl.BlockSpec((tt, wd), lambda h, i: (rev(i), col * ng + h))
    hd1 = pl.BlockSpec((tt, wd), lambda h, i: (rev(i), h))
    gate_spec = pl.BlockSpec((tt, 128), lambda h, i: (rev(i), 0))
    in_specs = [hd(0), hd(1), hd(2), gate_spec, gate_spec, hd1,
                pl.BlockSpec((A_HG, nch, 128, 128), lambda h, i: (h, rev(i), 0, 0)),
                pl.BlockSpec((A_HG, nch, A_CHUNK, A_CHUNK), lambda h, i: (h, rev(i), 0, 0)), hd1, hd1]
    out_specs = [hd1, hd1, hd1, gate_spec]
    out_shape = [jax.ShapeDtypeStruct((T, 1024), F32)] * 3 + [jax.ShapeDtypeStruct((T, 128), F32)]
    scratch = [pltpu.VMEM((A_HG, 128, 128), F32)]
    args = [qkv, qkv, qkv, bg, gcum, d_o, s_sv, t_sv, u_sv, w_sv]
    if cs_early is not None:
        in_specs.append(HBM)
        out_specs.append(HBM)
        out_shape.append(jax.ShapeDtypeStruct((3,) + cs_early.shape[1:], BF16))
        scratch += [pltpu.SemaphoreType.DMA((3,)), pltpu.SemaphoreType.DMA((3,))]
        args.append(cs_early)
    return pl.pallas_call(body, grid=(ng, ni), in_specs=in_specs, out_specs=out_specs, out_shape=out_shape,
                          scratch_shapes=scratch, name=name, compiler_params=_params(2))(*args)


def _gdn_gate_fwd(name, o, proj, nw):
    T = o.shape[0]
    tt = min(T, 512)

    def body(o_ref, z_ref, nw_ref, y_ref):
        for h in range(A_HEADS):
            cols = pl.ds(h * 128, 128)
            ov, z = o_ref[:, cols], z_ref[:, cols]
            r = lax.rsqrt(jnp.mean(ov * ov, axis=-1, keepdims=True) + EPS)
            y_ref[:, cols] = (ov * r * nw_ref[...] * (z * _sigmoid(z))).astype(BF16)

    return pl.pallas_call(
        body, grid=(T // tt,),
        in_specs=[pl.BlockSpec((tt, 1024), lambda i: (i, 0)), pl.BlockSpec((tt, 1024), lambda i: (i, 3)),
                  pl.BlockSpec((1, 128), lambda i: (0, 0))],
        out_specs=pl.BlockSpec((tt, 1024), lambda i: (i, 0)),
        out_shape=jax.ShapeDtypeStruct((T, 1024), BF16), name=name, compiler_params=_params(1))(o, proj, nw)


def _gdn_gate_bwd(name, dy2, o, proj, nw):
    T = o.shape[0]
    tt = min(T, 512)

    def body(dy_ref, o_ref, z_ref, nw_ref, do_ref, dz_ref, dnw_ref):
        dnw = jnp.zeros((1, 128), F32)
        for h in range(A_HEADS):
            cols = pl.ds(h * 128, 128)
            dy, ov, z = dy_ref[:, cols], o_ref[:, cols], z_ref[:, cols]
            s = _sigmoid(z)
            sz = z * s
            r = lax.rsqrt(jnp.mean(ov * ov, axis=-1, keepdims=True) + EPS)
            xhat = ov * r
            dn = dy * sz
            dz_ref[:, cols] = dy * (xhat * nw_ref[...]) * (s + z * s * (1.0 - s))
            dxhat = dn * nw_ref[...]
            do_ref[:, cols] = r * (dxhat - xhat * jnp.mean(dxhat * xhat, axis=-1, keepdims=True))
            dnw = dnw + jnp.sum(dn * xhat, axis=0, keepdims=True)

        @pl.when(pl.program_id(0) == 0)
        def _():
            dnw_ref[...] = dnw

        @pl.when(pl.program_id(0) > 0)
        def _():
            dnw_ref[...] += dnw

    blk = lambda c: pl.BlockSpec((tt, 1024), lambda i: (i, c))
    return pl.pallas_call(
        body, grid=(T // tt,),
        in_specs=[blk(0), blk(0), blk(3), pl.BlockSpec((1, 128), lambda i: (0, 0))],
        out_specs=[blk(0), blk(0), pl.BlockSpec((1, 128), lambda i: (0, 0))],
        out_shape=[jax.ShapeDtypeStruct((T, 1024), F32), jax.ShapeDtypeStruct((T, 1024), F32),
                   jax.ShapeDtypeStruct((1, 128), F32)],
        name=name, compiler_params=_params(1))(dy2, o, proj, nw)


def _gdn_prep_bwd1(name, proj, wconv, gate_p, dq, dk, dv, dbg):
    T = proj.shape[0]
    tt = min(T, 256)
    hb = tt // 8

    def body(cur_ref, halo_ref, ba_ref, w_ref, gp_ref, dq_ref, dk_ref, dv_ref, dbg_ref,
             dc_ref, dw_ref, dba_ref, dgp_ref):
        first = pl.program_id(0) == 0
        rid = lax.broadcasted_iota(jnp.int32, (8, 128), 0)
        for c in range(24):
            cols = pl.ds(c * 128, 128)
            cur = cur_ref[:, cols]
            halo = jnp.where(first, 0.0, halo_ref[:, cols])
            conv = _conv_taps(cur, halo_ref[:, cols], w_ref[:, cols], first, -1)
            s = _sigmoid(conv)
            y = conv * s
            if c < 16:
                dref = dq_ref if c < 8 else dk_ref
                dn = dref[:, pl.ds((c % 8) * 128, 128)]
                rinv = lax.rsqrt(jnp.sum(y * y, axis=-1, keepdims=True) + EPS)
                yhat = y * rinv
                dyv = rinv * (dn - yhat * jnp.sum(dn * yhat, axis=-1, keepdims=True))
                if c < 8:
                    dyv = dyv * (A_DK ** -0.5)
            else:
                dyv = dv_ref[:, pl.ds((c - 16) * 128, 128)]
            dc = dyv * (s + conv * s * (1.0 - s))
            dc_ref[:, cols] = dc
            parts = [jnp.sum(dc * cur, axis=0, keepdims=True)]
            for sft in (1, 2, 3):
                sh = pltpu.roll(cur, sft, 0)
                edge = jnp.where(rid < sft, pltpu.roll(halo, sft, 0), sh[0:8])
                sh = jnp.concatenate([edge, sh[8:]], axis=0) if tt > 8 else edge
                parts.append(jnp.sum(dc * sh, axis=0, keepdims=True))
            dwc = jnp.concatenate(parts[::-1], axis=0)

            @pl.when(first)
            def _():
                dw_ref[:, cols] = dwc

            @pl.when(jnp.logical_not(first))
            def _():
                dw_ref[:, cols] += dwc

        ba = ba_ref[...]
        dbg = dbg_ref[...]
        lane = lax.broadcasted_iota(jnp.int32, ba.shape, 1)
        sb = _sigmoid(ba)
        zarg = ba + gp_ref[1:2, :]
        softplus = jnp.maximum(zarg, 0.0) + jnp.log(1.0 + jnp.exp(-jnp.abs(zarg)))
        d_b = dbg * sb * (1.0 - sb)
        d_a = dbg * gp_ref[0:1, :] * _sigmoid(zarg)
        dba_ref[...] = jnp.where(lane < 8, d_b, jnp.where(lane < 16, d_a, 0.0))
        g = gp_ref[0:1, :] * softplus
        in_a = (lane >= 8) & (lane < 16)
        sums = jnp.concatenate([jnp.sum(jnp.where(in_a, dbg * g, 0.0), axis=0, keepdims=True),
                                jnp.sum(jnp.where(in_a, d_a, 0.0), axis=0, keepdims=True)], axis=0)

        @pl.when(first)
        def _():
            dgp_ref[...] = sums

        @pl.when(jnp.logical_not(first))
        def _():
            dgp_ref[...] += sums

    row = lambda w, c=0: pl.BlockSpec((tt, w), lambda i: (i, c))
    return pl.pallas_call(
        body, grid=(T // tt,),
        in_specs=[row(3072), pl.BlockSpec((8, 3072), lambda i: (jnp.maximum(i * hb - 1, 0), 0)), row(128, 32),
                  pl.BlockSpec((4, 3072), lambda i: (0, 0)), pl.BlockSpec((2, 128), lambda i: (0, 0)),
                  row(1024), row(1024), row(1024), row(128)],
        out_specs=[row(3072), pl.BlockSpec((4, 3072), lambda i: (0, 0)), row(128),
                   pl.BlockSpec((2, 128), lambda i: (0, 0))],
        out_shape=[jax.ShapeDtypeStruct((T, 3072), F32), jax.ShapeDtypeStruct((4, 3072), F32),
                   jax.ShapeDtypeStruct((T, 128), F32), jax.ShapeDtypeStruct((2, 128), F32)],
        name=name, compiler_params=_params(1))(proj, proj, proj, wconv, gate_p, dq, dk, dv, dbg)


def _gdn_prep_bwd2(name, dc, wconv, dz, dba):
    T = dc.shape[0]
    tt = min(T, 256)
    hb = tt // 8
    ni = T // tt

    def body(cur_ref, halo_ref, w_ref, dz_ref, dba_ref, o_ref):
        last = pl.program_id(0) == ni - 1
        for c in range(24):
            cols = pl.ds(c * 128, 128)
            o_ref[:, cols] = _conv_taps(cur_ref[:, cols], halo_ref[:, cols], w_ref[:, cols], last, +1).astype(BF16)
        o_ref[:, pl.ds(3072, 1024)] = dz_ref[...].astype(BF16)
        o_ref[:, pl.ds(4096, 128)] = dba_ref[...].astype(BF16)

    return pl.pallas_call(
        body, grid=(ni,),
        in_specs=[pl.BlockSpec((tt, 3072), lambda i: (i, 0)),
                  pl.BlockSpec((8, 3072), lambda i: (jnp.minimum((i + 1) * hb, T // 8 - 1), 0)),
                  pl.BlockSpec((4, 3072), lambda i: (0, 0)),
                  pl.BlockSpec((tt, 1024), lambda i: (i, 0)), pl.BlockSpec((tt, 128), lambda i: (i, 0))],
        out_specs=pl.BlockSpec((tt, A_IN_PAD), lambda i: (i, 0)),
        out_shape=jax.ShapeDtypeStruct((T, A_IN_PAD), BF16), name=name, compiler_params=_params(1))(
            dc, dc, wconv, dz, dba)


def _gdn_fwd(x, nw, w_in, wconv, gate_p, out_nw, w_out, a4=None):
    h = _rms_fwd("a_rms", x, nw)
    proj = _mm_plain("a_proj", h, w_in, 1, 0, F32, tn=FF_BLK)
    qkv, bg, gcum = _gdn_prep("a_prep", proj, wconv, gate_p)
    o, s_sv, t_sv, u_sv, w_sv, *arrived = _gdn_delta_fwd("a_delta", qkv, bg, gcum, a4)
    o2 = _gdn_gate_fwd("a_gate", o, proj, out_nw)
    y = _mm_residual("a_out", o2, w_out, x, 1.0)
    return y, (h, proj, qkv, bg, gcum, o, s_sv, t_sv, u_sv, w_sv, o2), (arrived[0] if arrived else None)


def _gdn_bwd(dy, dyb, x, nw, w_in, wconv, gate_p, out_nw, w_out, saved, out_scale, cs_early=None):
    h, proj, qkv, bg, gcum, o, s_sv, t_sv, u_sv, w_sv, o2 = saved
    d_o2 = _mm_plain("a_dout", dyb, w_out, 1, 1, F32)
    d_wout = _mm_plain("a_dwout", o2, dyb, 0, 0, F32)
    d_o, d_z, d_outnw = _gdn_gate_bwd("a_dgate", d_o2, o, proj, out_nw)
    dq, dk, dv, dbg, *arrived = _gdn_delta_bwd("a_ddelta", qkv, bg, gcum, d_o, s_sv, t_sv, u_sv, w_sv, cs_early)
    dc, d_wconv, dba, dgp = _gdn_prep_bwd1("a_dprep1", proj, wconv, gate_p, dq, dk, dv, dbg)
    dproj = _gdn_prep_bwd2("a_dprep2", dc, wconv, d_z, dba)
    d_win = _mm_plain("a_dwin", h, dproj, 0, 0, F32, tn=FF_BLK, tk=2048)
    dx, dxb, d_nw = _mm_rms_bwd("a_dx", dproj, w_in, x, dy, nw, A_IN_PAD, out_scale)
    return dx, dxb, d_nw, d_win, d_wconv, dgp, d_outnw, d_wout, (arrived[0] if arrived else None)


def _swa_masks(n):
    qi = lax.broadcasted_iota(jnp.int32, (B_BLK, B_BLK), 0)
    kj = lax.broadcasted_iota(jnp.int32, (B_BLK, B_BLK), 1)
    return kj > qi + jnp.where(n > 0, 0, B_BLK), kj <= qi


def _swa_fwd(name, q, k, v, sinks):
    T = q.shape[1]
    tq = min(T, 1024)
    nbt = tq // B_BLK
    scale = B_HD ** -0.5
    G = B_HEADS // B_KV

    def body(q_ref, k_ref, v_ref, kh_ref, vh_ref, s_ref, o_ref, l_ref):
        first_blk = pl.program_id(1) * nbt

        def block(n, kp, vp):
            m_prev, m_cur = _swa_masks(first_blk + n)
            cur = pl.ds(pl.multiple_of(n * B_BLK, B_BLK), B_BLK)
            kc, vc = k_ref[0, cur, :], v_ref[0, cur, :]
            gs = range(G)
            rmax = lambda a: jnp.max(a, axis=1, keepdims=True)
            rsum = lambda a: jnp.sum(a, axis=1, keepdims=True)
            sink = [s_ref[g][:, 0:1] for g in gs]
            qb = [q_ref[g, cur, :] for g in gs]
            s_p = [jnp.where(m_prev, _dot(qb[g], kp, 1, 1) * scale, -jnp.inf) for g in gs]
            s_c = [jnp.where(m_cur, _dot(qb[g], kc, 1, 1) * scale, -jnp.inf) for g in gs]
            m = [jnp.maximum(jnp.maximum(rmax(s_p[g]), rmax(s_c[g])), sink[g]) for g in gs]
            p_p = [jnp.exp(s_p[g] - m[g]) for g in gs]
            p_c = [jnp.exp(s_c[g] - m[g]) for g in gs]
            den = [rsum(p_p[g]) + rsum(p_c[g]) + jnp.exp(sink[g] - m[g]) for g in gs]
            a_p = [_dotb(p_p[g], vp) for g in gs]
            a_c = [_dotb(p_c[g], vc) for g in gs]
            for g in gs:
                o_ref[g, cur, :] = ((a_p[g] + a_c[g]) / den[g]).astype(BF16)
                l_ref[g, cur, :] = m[g] + jnp.log(den[g])

        block(0, kh_ref[0], vh_ref[0])

        def rest(n, carry):
            prv = pl.ds(pl.multiple_of((n - 1) * B_BLK, B_BLK), B_BLK)
            block(n, k_ref[0, prv, :], v_ref[0, prv, :])
            return carry

        lax.fori_loop(1, nbt, rest, 0)

    qs = pl.BlockSpec((G, tq, B_HD), lambda kv, i: (kv, i, 0))
    ks = pl.BlockSpec((1, tq, B_HD), lambda kv, i: (kv, i, 0))
    halo = pl.BlockSpec((1, B_BLK, B_HD), lambda kv, i: (kv, jnp.maximum(i * nbt - 1, 0), 0))
    return pl.pallas_call(
        body, grid=(B_KV, T // tq),
        in_specs=[qs, ks, ks, halo, halo, pl.BlockSpec((G, 1, 128), lambda kv, i: (kv, 0, 0))],
        out_specs=[qs, pl.BlockSpec((G, tq, 1), lambda kv, i: (kv, i, 0))],
        out_shape=[jax.ShapeDtypeStruct((B_HEADS, T, B_HD), BF16), jax.ShapeDtypeStruct((B_HEADS, T, 1), F32)],
        name=name, compiler_params=_params(2))(q, k, v, k, v, sinks)


def _swa_bwd(name, q, k, v, sinks, o, lse, do):
    T = q.shape[1]
    tq = min(T, 1024)
    nbt, ni = tq // B_BLK, T // tq
    scale = B_HD ** -0.5
    G = B_HEADS // B_KV

    def body(q_ref, k_ref, v_ref, kh_ref, vh_ref, s_ref, o_ref, l_ref, do_ref, dq_ref, dk_ref, dv_ref, ds_ref,
             dk_halo, dv_halo):
        step = pl.program_id(1)
        first_blk = (ni - 1 - step) * nbt
        last = pl.ds(tq - B_BLK, B_BLK)
        dk_ref[...] = jnp.zeros_like(dk_ref)
        dv_ref[...] = jnp.zeros_like(dv_ref)

        @pl.when(step > 0)
        def _():
            dk_ref[0, last, :] = dk_halo[...]
            dv_ref[0, last, :] = dv_halo[...]

        def block(n, kp, vp, dsinks):
            m_prev, m_cur = _swa_masks(first_blk + n)
            cur = pl.ds(pl.multiple_of(n * B_BLK, B_BLK), B_BLK)
            kc, vc = k_ref[0, cur, :], v_ref[0, cur, :]
            gs = range(G)
            sink = [s_ref[g][:, 0:1] for g in gs]
            qb = [q_ref[g, cur, :] for g in gs]
            dob = [do_ref[g, cur, :] for g in gs]
            lse_b = [l_ref[g, cur, :] for g in gs]
            p_p = [jnp.where(m_prev, jnp.exp(_dot(qb[g], kp, 1, 1) * scale - lse_b[g]), 0.0) for g in gs]
            p_c = [jnp.where(m_cur, jnp.exp(_dot(qb[g], kc, 1, 1) * scale - lse_b[g]), 0.0) for g in gs]
            delta = [jnp.sum(dob[g].astype(F32) * o_ref[g, cur, :].astype(F32), axis=1, keepdims=True) for g in gs]
            ds_p = [p_p[g] * (_dot(dob[g], vp, 1, 1) - delta[g]) for g in gs]
            ds_c = [p_c[g] * (_dot(dob[g], vc, 1, 1) - delta[g]) for g in gs]
            dq_p = [_dotb(ds_p[g], kp) for g in gs]
            dq_c = [_dotb(ds_c[g], kc) for g in gs]
            dk_ps = [_dotb(ds_p[g], qb[g], 0, 0) for g in gs]
            dk_cs = [_dotb(ds_c[g], qb[g], 0, 0) for g in gs]
            dv_ps = [_dotb(p_p[g], dob[g], 0, 0) for g in gs]
            dv_cs = [_dotb(p_c[g], dob[g], 0, 0) for g in gs]
            for g in gs:
                dq_ref[g, cur, :] = (dq_p[g] + dq_c[g]) * scale
            out = tuple(dsinks[g] - jnp.sum(jnp.exp(sink[g] - lse_b[g]) * delta[g], axis=0, keepdims=True) for g in gs)
            total = lambda parts: (parts[0] + parts[1]) + (parts[2] + parts[3])
            dk_ref[0, cur, :] += total(dk_cs) * scale
            dv_ref[0, cur, :] += total(dv_cs)
            return total(dk_ps) * scale, total(dv_ps), out

        zeros = tuple(jnp.zeros((1, 1), F32) for _ in range(G))
        dk_p, dv_p, dsinks = block(0, kh_ref[0], vh_ref[0], zeros)
        dk_halo[...] = dk_p
        dv_halo[...] = dv_p

        def rest(n, dsinks):
            prv = pl.ds(pl.multiple_of((n - 1) * B_BLK, B_BLK), B_BLK)
            dk_p, dv_p, dsinks = block(n, k_ref[0, prv, :], v_ref[0, prv, :], dsinks)
            dk_ref[0, prv, :] += dk_p
            dv_ref[0, prv, :] += dv_p
            return dsinks

        dsinks = lax.fori_loop(1, nbt, rest, dsinks)
        for g in range(G):
            row = jnp.broadcast_to(dsinks[g], (1, 128))

            @pl.when(step == 0)
            def _():
                ds_ref[g] = row

            @pl.when(step > 0)
            def _():
                ds_ref[g] += row

    rev = lambda i: ni - 1 - i
    qs = pl.BlockSpec((G, tq, B_HD), lambda kv, i: (kv, rev(i), 0))
    ks = pl.BlockSpec((1, tq, B_HD), lambda kv, i: (kv, rev(i), 0))
    halo = pl.BlockSpec((1, B_BLK, B_HD), lambda kv, i: (kv, jnp.maximum(rev(i) * nbt - 1, 0), 0))
    ss = pl.BlockSpec((G, 1, 128), lambda kv, i: (kv, 0, 0))
    return pl.pallas_call(
        body, grid=(B_KV, ni),
        in_specs=[qs, ks, ks, halo, halo, ss, qs, pl.BlockSpec((G, tq, 1), lambda kv, i: (kv, rev(i), 0)), qs],
        out_specs=[qs, ks, ks, ss],
        out_shape=[jax.ShapeDtypeStruct((B_HEADS, T, B_HD), F32), jax.ShapeDtypeStruct((B_KV, T, B_HD), F32),
                   jax.ShapeDtypeStruct((B_KV, T, B_HD), F32), jax.ShapeDtypeStruct((B_HEADS, 1, 128), F32)],
        scratch_shapes=[pltpu.VMEM((B_BLK, B_HD), F32), pltpu.VMEM((B_BLK, B_HD), F32)],
        name=name, compiler_params=_params(2))(q, k, v, k, v, sinks, o, lse, do)


def _split_heads(a, n):
    T = a.shape[0]
    return a.reshape(T, n, B_HD).transpose(1, 0, 2)


def _merge_heads(a):
    n, T, _ = a.shape
    return a.transpose(1, 0, 2).reshape(T, n * B_HD)


def _swa_mixer_fwd(x, nw, w_in, b_in, sinks, w_out, b_out):
    h = _rms_fwd("b_rms", x, nw)
    proj = _mm_plain("b_proj", h, w_in, 1, 0, BF16, tn=768, bias=b_in)
    q, k, v = _split_heads(proj[:, :1024], B_HEADS), _split_heads(proj[:, 1024:1280], B_KV), _split_heads(proj[:, 1280:], B_KV)
    o, lse = _swa_fwd("b_attn", q, k, v, sinks)
    om = _merge_heads(o)
    y = _mm_residual("b_out", om, w_out, x, 1.0, bias=b_out)
    return y, (h, q, k, v, o, lse, om)


def _swa_mixer_bwd(dy, dyb, x, nw, w_in, sinks, w_out, saved, out_scale):
    h, q, k, v, o, lse, om = saved
    d_om = _mm_plain("b_dout", dyb, w_out, 1, 1, BF16)
    d_wout = _mm_plain("b_dwout", om, dyb, 0, 0, F32)
    d_bout = _colsum("b_dbout", dy)
    dq, dk, dv, dsinks = _swa_bwd("b_dattn", q, k, v, sinks, o, lse, _split_heads(d_om, B_HEADS))
    dproj = jnp.concatenate([_merge_heads(dq), _merge_heads(dk), _merge_heads(dv)], axis=1)
    d_bin = _colsum("b_dbin", dproj)
    d_win = _mm_plain("b_dwin", h, dproj, 0, 0, F32, tn=768)
    dx, dxb, d_nw = _mm_rms_bwd("b_dx", dproj, w_in, x, dy, nw, 1536, out_scale)
    return dx, dxb, d_nw, d_win, d_bin, dsinks[:, 0, 0], d_wout, d_bout


def _loss_head(name, x, tgt, fw, out_scale):
    T, D = x.shape
    tt = min(T, 512)

    def body(x_ref, t_ref, w_ref, dx_ref, dxb_ref, loss_ref, dw_ref):
        xv = x_ref[...]
        r = lax.rsqrt(jnp.mean(xv * xv, axis=-1, keepdims=True) + EPS)
        xhat = xv * r
        diff = xhat * w_ref[...] - t_ref[...]
        part = 0.5 * jnp.sum(jnp.mean(diff * diff, axis=-1, keepdims=True), axis=0, keepdims=True)
        dyv = diff * (1.0 / D)
        dxhat = dyv * w_ref[...]
        dx = r * (dxhat - xhat * jnp.mean(dxhat * xhat, axis=-1, keepdims=True))
        dx_ref[...] = dx
        dxb_ref[...] = (dx * out_scale).astype(BF16)
        dw = jnp.sum(dyv * xhat, axis=0, keepdims=True)
        lp = jnp.broadcast_to(part, (1, 128))

        @pl.when(pl.program_id(0) == 0)
        def _():
            loss_ref[...] = lp
            dw_ref[...] = dw

        @pl.when(pl.program_id(0) > 0)
        def _():
            loss_ref[...] += lp
            dw_ref[...] += dw

    row = pl.BlockSpec((tt, D), lambda i: (i, 0))
    return pl.pallas_call(
        body, grid=(T // tt,), in_specs=[row, row, pl.BlockSpec((1, D), lambda i: (0, 0))],
        out_specs=[row, row, pl.BlockSpec((1, 128), lambda i: (0, 0)), pl.BlockSpec((1, D), lambda i: (0, 0))],
        out_shape=[jax.ShapeDtypeStruct((T, D), F32), jax.ShapeDtypeStruct((T, D), BF16),
                   jax.ShapeDtypeStruct((1, 128), F32), jax.ShapeDtypeStruct((1, D), F32)],
        name=name, compiler_params=_params(1))(x, tgt, fw)


def _local_step(x, tgt, wts, comm=None):
    W = dict(wts)
    g = {}
    ga = W["ga"]
    a4 = comm["a4"] if comm else None
    n1, n2, nm = W["ffn1_norm"], W["ffn2_norm"], W["mix_norm"]
    h10, gu10, act10, *arrived = _ffn_gu("f10_gu", x, n1[0:1], ga, 0, (comm["big_b"], comm["small"]) if comm else None)
    if comm:
        W.update(comm["finish"](*arrived))
    wdn = W["w_down"]
    x1, sv1 = _ffn_down("f10_down", act10, wdn[0], x), (h10, gu10, act10)
    x2, sva, arrived = _gdn_fwd(x1, nm[0:1], W["a_w_in"], W["a_w_conv"], W["a_gate_p"], W["a_out_norm"], W["a_w_out"], a4)
    if a4 is not None:
        ga = _fill_a(1, a4, arrived, ga.reshape(4, 4, 2, A_HALF, FF_BLK)).reshape(ga.shape)
    x3, sv3 = _ffn_fwd("f20", x2, n2[0:1], ga, 2, wdn[2])
    x4, sv4 = _ffn_fwd("f11", x3, n1[1:2], ga, 1, wdn[1])
    x5, svb = _swa_mixer_fwd(x4, nm[1:2], W["b_w_in"], W["b_b_in"], W["b_sinks"], W["b_w_out"], W["b_b_out"])
    x6, sv6 = _ffn_fwd("f21", x5, n2[1:2], ga, 3, wdn[3])
    dx, dxb, loss_p, g["final_norm"] = _loss_head("loss_head", x6, tgt, W["final_norm"], 0.5)

    pa = jnp.zeros(ga.shape, BF16)
    dx, dxb, n21, pa, wd21 = _ffn_bwd("f21", dx, dxb, x5, n2[1:2], ga, 3, wdn[3], sv6, pa, 1.0)
    dx, dxb, nb, g["b_w_in"], g["b_b_in"], g["b_sinks"], g["b_w_out"], g["b_b_out"] = _swa_mixer_bwd(
        dx, dxb, x4, nm[1:2], W["b_w_in"], W["b_sinks"], W["b_w_out"], svb, 0.5)
    dx, dxb, n11, pa, wd11 = _ffn_bwd("f11", dx, dxb, x3, n1[1:2], ga, 1, wdn[1], sv4, pa, 0.5)
    dx, dxb, n20, pa, wd20 = _ffn_bwd("f20", dx, dxb, x2, n2[0:1], ga, 2, wdn[2], sv3, pa, 1.0)
    g["cs_early"] = _pair_sums_a("1", pa.reshape(4, 4, 2, A_HALF, FF_BLK), 1, 3) if a4 is not None else None
    dx, dxb, na, g["a_w_in"], g["a_w_conv"], g["a_gate_p"], g["a_out_norm"], g["a_w_out"], g["from_early"] = _gdn_bwd(
        dx, dxb, x1, nm[0:1], W["a_w_in"], W["a_w_conv"], W["a_gate_p"], W["a_out_norm"], W["a_w_out"], sva, 0.5,
        g["cs_early"])
    dgu10 = _ffn_dact("f10_dact", dxb, wdn[0], gu10)
    wd10 = _ffn_dwd("f10_dwd", act10, dxb)
    pa = _ffn_dwgu("f10_dwgu", h10, dgu10, pa, 0)
    g["ga"] = pa
    g["w_down"] = jnp.stack([wd10, wd11, wd20, wd21])
    late = None
    if comm:
        p_b = comm["pack_b"](g)
        late = (_pair_sums_a("0", pa.reshape(4, 4, 2, A_HALF, FF_BLK), 0, 1), _pair_sum("b", p_b, _pair_send("b", p_b)))
    dx, dxb, n10, *arrived = _ffn_dx("f10_dx", dgu10, ga, 0, x, dx, n1[0:1], 1.0, late)
    g["late"] = (late, tuple(arrived)) if comm else None

    g["ffn1_norm"] = jnp.concatenate([n10, n11], axis=0)
    g["ffn2_norm"] = jnp.concatenate([n20, n21], axis=0)
    g["mix_norm"] = jnp.concatenate([na, nb], axis=0)
    return loss_p, dx, g


A_ROWS = 4 * D_MODEL
PACK = (("ffn1_w_down", 1408), ("ffn2_w_down", 1408), ("a_w_in", 1028), ("a_w_out", 256), ("b_w_in", 384),
        ("b_w_out", 256))
PACK_TILE = 16
PACK_USED = sum(-(-n // PACK_TILE) * PACK_TILE for _, n in PACK)
PACK_ROWS = 4864
assert PACK_USED <= PACK_ROWS
SMALL_SHARD = (8, 512)
MOVE_ROWS = {"a": 512, "b": 608}
SUM_ROWS = {"a": 256, "b": 304}


def _mesh_pos():
    x, y, c = lax.axis_index("x"), lax.axis_index("y"), lax.axis_index("c")
    return x, y, c, [(1 - x, y), (x, 1 - y), (1 - x, 1 - y)]


def _half(rows, c):
    return pl.ds(pl.multiple_of(c * (rows // 2), 16), rows // 2)


A_HALF = D_MODEL // 2


def _src_chip(j):
    x, y = lax.axis_index("x"), lax.axis_index("y")
    return jnp.where(j == 0, 2 * (1 - x) + y, jnp.where(j == 1, 2 * x + 1 - y, 2 * (1 - x) + 1 - y))


def _later_blocks_start(a4_ref, ra_ref, send_sems, recv_sems):
    x, y, c, chips = _mesh_pos()
    copies = [pltpu.make_async_remote_copy(src_ref=a4_ref.at[pl.ds(1, 3), c], dst_ref=ra_ref.at[j],
                                           send_sem=send_sems.at[j], recv_sem=recv_sems.at[j],
                                           device_id=(*chip, c), device_id_type=MESH)
              for j, chip in enumerate(chips)]
    for cp in copies:
        cp.start()
    return copies


def _fill_a(phase, a4, ra, ga=None):
    nb = 1 if phase == 0 else 3
    first = 0 if phase == 0 else 1
    steps = 3 * nb
    own_tiles = 2 * nb
    ra = ra.reshape(3, nb, A_HALF, FF_BLK)

    def body(*refs):
        if phase == 0:
            r_ref, own_ref, g_ref, send_sem, recv_sem, local_sems = refs
        else:
            r_ref, own_ref, _, g_ref, send_sem, recv_sem, local_sems = refs
        x, y, c, _ = _mesh_pos()
        s = pl.program_id(0)
        j, b = s // nb, s % nb
        dst = g_ref.at[_src_chip(j), first + b, c]
        keep = pltpu.make_async_copy(r_ref.at[0, 0], dst, local_sems.at[0])
        give = pltpu.make_async_remote_copy(src_ref=r_ref.at[0, 0], dst_ref=dst, send_sem=send_sem, recv_sem=recv_sem,
                                            device_id=(x, y, 1 - c), device_id_type=MESH)
        keep.start()
        give.start()

        @pl.when(s < own_tiles)
        def _():
            own = pltpu.make_async_copy(own_ref.at[0, 0], g_ref.at[2 * x + y, first + s // 2, s % 2], local_sems.at[1])
            own.start()
            own.wait()

        give.wait_send()
        keep.wait()

        @pl.when(s == steps - 1)
        def _():
            landed = g_ref.at[pl.ds(0, 3), pl.ds(0, nb), 0]
            pltpu.make_async_remote_copy(src_ref=landed, dst_ref=landed, send_sem=send_sem, recv_sem=recv_sem,
                                         device_id=(x, y, c), device_id_type=MESH).wait_recv()

    tile = (1, 1, A_HALF, FF_BLK)
    in_specs = [pl.BlockSpec(tile, lambda s: (s // nb, s % nb, 0, 0)),
                pl.BlockSpec(tile, lambda s: (first + jnp.minimum(s, own_tiles - 1) // 2, jnp.minimum(s, own_tiles - 1) % 2, 0, 0))]
    args = [ra, a4]
    if phase == 1:
        in_specs.append(HBM)
        args.append(ga)
    return pl.pallas_call(
        body, grid=(steps,), in_specs=in_specs, out_specs=HBM,
        out_shape=jax.ShapeDtypeStruct((4, 4, 2, A_HALF, FF_BLK), BF16),
        scratch_shapes=[pltpu.SemaphoreType.DMA, pltpu.SemaphoreType.DMA, pltpu.SemaphoreType.DMA((2,))],
        input_output_aliases={2: 0} if phase == 1 else {},
        name="fill_a%d" % phase, compiler_params=_params(1))(*args)


def _gather_chips(big_a4):
    def body(a_ref, ra_ref, send_sems, recv_sems):
        c = lax.axis_index("c")
        send = _peer_copies([(lambda chip: a_ref.at[0, c], ra_ref)], send_sems, recv_sems)
        for cp in send:
            cp.start()
        for cp in send:
            cp.wait_recv()
        for cp in send:
            cp.wait_send()

    return pl.pallas_call(
        body, name="gather_chips", in_specs=[HBM], out_specs=HBM,
        out_shape=jax.ShapeDtypeStruct((3, A_HALF, FF_BLK), BF16),
        scratch_shapes=[pltpu.SemaphoreType.DMA((3,)), pltpu.SemaphoreType.DMA((3,))])(big_a4)


def _gather_fill(tag, big, recv):
    rows_all, width = big.shape
    half, mv = rows_all // 2, MOVE_ROWS[tag]
    nt = half // mv
    own_tiles = rows_all // mv
    assert half % mv == 0 and own_tiles <= 3 * nt

    def body(recv_ref, big_ref, g_ref, send_sem, recv_sem, local_sems):
        x, y, c, chips = _mesh_pos()
        j, t = pl.program_id(0), pl.program_id(1)
        step = j * nt + t
        src_chip = jnp.where(j == 0, 2 * (1 - x) + y, jnp.where(j == 1, 2 * x + 1 - y, 2 * (1 - x) + 1 - y))
        rows = pl.ds(pl.multiple_of(c * half + t * mv, 16), mv)
        keep = pltpu.make_async_copy(recv_ref.at[0], g_ref.at[src_chip, rows], local_sems.at[0])
        give = pltpu.make_async_remote_copy(src_ref=recv_ref.at[0], dst_ref=g_ref.at[src_chip, rows],
                                            send_sem=send_sem, recv_sem=recv_sem,
                                            device_id=(x, y, 1 - c), device_id_type=MESH)
        keep.start()
        give.start()

        @pl.when(step < own_tiles)
        def _():
            own_rows = pl.ds(pl.multiple_of(step * mv, 16), mv)
            own = pltpu.make_async_copy(big_ref, g_ref.at[2 * x + y, own_rows], local_sems.at[1])
            own.start()
            own.wait()

        give.wait_send()
        keep.wait()

        @pl.when(step == 3 * nt - 1)
        def _():
            landed = g_ref.at[pl.ds(0, 3), pl.ds(0, half)]
            pltpu.make_async_remote_copy(src_ref=landed, dst_ref=landed, send_sem=send_sem, recv_sem=recv_sem,
                                         device_id=(x, y, c), device_id_type=MESH).wait_recv()

    return pl.pallas_call(
        body, grid=(3, nt),
        in_specs=[pl.BlockSpec((1, mv, width), lambda j, t: (j, t, 0)),
                  pl.BlockSpec((mv, width), lambda j, t: (jnp.minimum(j * nt + t, own_tiles - 1), 0))],
        out_specs=HBM, out_shape=jax.ShapeDtypeStruct((4, rows_all, width), BF16),
        scratch_shapes=[pltpu.SemaphoreType.DMA, pltpu.SemaphoreType.DMA, pltpu.SemaphoreType.DMA((2,))],
        name="gather_fill_" + tag, compiler_params=_params(2))(recv, big)


def _pair_send(tag, p):
    _, rows_all, width = p.shape
    half, mv = rows_all // 2, MOVE_ROWS[tag]
    nt = half // mv

    def body(p_ref, a_ref, send_sem, recv_sem):
        x, y, c, _ = _mesh_pos()
        s, t = pl.program_id(0), pl.program_id(1)
        rows = pl.ds(pl.multiple_of(t * mv, 16), mv)
        give = pltpu.make_async_remote_copy(src_ref=p_ref.at[0], dst_ref=a_ref.at[s, rows], send_sem=send_sem,
                                            recv_sem=recv_sem, device_id=(x, y, 1 - c), device_id_type=MESH)
        give.start()
        give.wait_send()

        @pl.when((s == 3) & (t == nt - 1))
        def _():
            pltpu.make_async_remote_copy(src_ref=a_ref, dst_ref=a_ref, send_sem=send_sem, recv_sem=recv_sem,
                                         device_id=(x, y, c), device_id_type=MESH).wait_recv()

    return pl.pallas_call(
        body, grid=(4, nt),
        in_specs=[pl.BlockSpec((1, mv, width), lambda s, t: (s, (1 - lax.axis_index("c")) * nt + t, 0))],
        out_specs=HBM, out_shape=jax.ShapeDtypeStruct((4, half, width), BF16),
        scratch_shapes=[pltpu.SemaphoreType.DMA, pltpu.SemaphoreType.DMA],
        name="pair_send_" + tag, compiler_params=_params(2))(p)


def _pair_sum(tag, p, a):
    _, half, width = a.shape
    sr = SUM_ROWS[tag]
    nt = half // sr
    assert half % sr == 0

    def body(p_ref, a_ref, o_ref):
        o_ref[...] = (p_ref[...].astype(F32) + a_ref[...].astype(F32)).astype(BF16)

    spec = pl.BlockSpec((1, sr, width), lambda s, t: (s, t, 0))
    return pl.pallas_call(
        body, grid=(4, nt),
        in_specs=[pl.BlockSpec((1, sr, width), lambda s, t: (s, lax.axis_index("c") * nt + t, 0)), spec],
        out_specs=spec, out_shape=jax.ShapeDtypeStruct((4, half, width), BF16),
        name="pair_sum_" + tag, compiler_params=_params(2))(p, a)


def _chip_exchange(cs_a, cs_b):
    def body(ca_ref, cb_ref, ba_ref, bb_ref, send_sems, recv_sems):
        x, y, c, chips = _mesh_pos()
        send = []
        for j, chip in enumerate(chips):
            for n, (src, dst) in enumerate(((ca_ref, ba_ref), (cb_ref, bb_ref))):
                send.append(pltpu.make_async_remote_copy(src_ref=src.at[2 * chip[0] + chip[1]], dst_ref=dst.at[j],
                                                         send_sem=send_sems.at[2 * j + n], recv_sem=recv_sems.at[2 * j + n],
                                                         device_id=(*chip, c), device_id_type=MESH))
        for cp in send:
            cp.start()
        for cp in send:
            cp.wait_recv()
        for cp in send:
            cp.wait_send()

    return pl.pallas_call(
        body, name="chip_exchange", in_specs=[HBM, HBM], out_specs=[HBM, HBM],
        out_shape=[jax.ShapeDtypeStruct((3,) + cs.shape[1:], BF16) for cs in (cs_a, cs_b)],
        scratch_shapes=[pltpu.SemaphoreType.DMA((6,)), pltpu.SemaphoreType.DMA((6,))])(cs_a, cs_b)


def _chip_sum(tag, cs, b):
    _, half, width = cs.shape
    sr = SUM_ROWS[tag]
    nt = half // sr

    def body(c_ref, b_ref, r_ref, buf, send_sems, recv_sem, local_sems):
        x, y, c, _ = _mesh_pos()
        t = pl.program_id(0)
        slot = lax.rem(t, 2)

        def copies(k, tile):
            rows = pl.ds(pl.multiple_of(c * half + tile * sr, 8), sr)
            keep = pltpu.make_async_copy(buf.at[k], r_ref.at[rows], local_sems.at[k])
            give = pltpu.make_async_remote_copy(src_ref=buf.at[k], dst_ref=r_ref.at[rows], send_sem=send_sems.at[k],
                                                recv_sem=recv_sem, device_id=(x, y, 1 - c), device_id_type=MESH)
            return keep, give

        @pl.when(t >= 2)
        def _():
            keep, give = copies(slot, t - 2)
            keep.wait()
            give.wait_send()

        buf[slot] = (c_ref[0].astype(F32) + b_ref[0].astype(F32)) + (b_ref[1].astype(F32) + b_ref[2].astype(F32))
        keep, give = copies(slot, t)
        keep.start()
        give.start()

        @pl.when(t == nt - 1)
        def _():
            for back in (1, 0):
                keep, give = copies(lax.rem(t - back, 2), t - back)
                keep.wait()
                give.wait_send()
            landed = r_ref.at[_half(2 * half, 1 - c)]
            pltpu.make_async_remote_copy(src_ref=landed, dst_ref=landed, send_sem=send_sems.at[0], recv_sem=recv_sem,
                                         device_id=(x, y, c), device_id_type=MESH).wait_recv()

    return pl.pallas_call(
        body, grid=(nt,),
        in_specs=[pl.BlockSpec((1, sr, width), lambda t: (2 * lax.axis_index("x") + lax.axis_index("y"), t, 0)),
                  pl.BlockSpec((3, sr, width), lambda t: (0, t, 0))],
        out_specs=HBM, out_shape=jax.ShapeDtypeStruct((2 * half, width), F32),
        scratch_shapes=[pltpu.VMEM((2, sr, width), F32), pltpu.SemaphoreType.DMA((2,)), pltpu.SemaphoreType.DMA,
                        pltpu.SemaphoreType.DMA((2,))],
        name="chip_sum_" + tag, compiler_params=_params(1))(cs, b)


def _pair_sums_a(tag, p5, b0, nb):
    tile5 = (1, 1, 1, A_HALF, FF_BLK)
    tile4 = (1, 1, A_HALF, FF_BLK)

    def send_body(p_ref, a_ref, send_sem, recv_sem):
        x, y, c, _ = _mesh_pos()
        s, t = pl.program_id(0), pl.program_id(1)
        give = pltpu.make_async_remote_copy(src_ref=p_ref.at[0, 0, 0], dst_ref=a_ref.at[s, t], send_sem=send_sem,
                                            recv_sem=recv_sem, device_id=(x, y, 1 - c), device_id_type=MESH)
        give.start()
        give.wait_send()

        @pl.when((s == 3) & (t == nb - 1))
        def _():
            pltpu.make_async_remote_copy(src_ref=a_ref, dst_ref=a_ref, send_sem=send_sem, recv_sem=recv_sem,
                                         device_id=(x, y, c), device_id_type=MESH).wait_recv()

    shape = jax.ShapeDtypeStruct((4, nb, A_HALF, FF_BLK), BF16)
    recv = pl.pallas_call(
        send_body, grid=(4, nb),
        in_specs=[pl.BlockSpec(tile5, lambda s, t: (s, b0 + t, 1 - lax.axis_index("c"), 0, 0))],
        out_specs=HBM, out_shape=shape, scratch_shapes=[pltpu.SemaphoreType.DMA, pltpu.SemaphoreType.DMA],
        name="pair_send_a" + tag, compiler_params=_params(2))(p5)

    def sum_body(p_ref, a_ref, o_ref):
        o_ref[0, 0] = (p_ref[0, 0, 0].astype(F32) + a_ref[0, 0].astype(F32)).astype(BF16)

    spec = pl.BlockSpec(tile4, lambda s, t: (s, t, 0, 0))
    return pl.pallas_call(
        sum_body, grid=(4, nb),
        in_specs=[pl.BlockSpec(tile5, lambda s, t: (s, b0 + t, lax.axis_index("c"), 0, 0)), spec],
        out_specs=spec, out_shape=shape, name="pair_sum_a" + tag, compiler_params=_params(2))(p5, recv)


def _chip_sum_a(tag, cs, frm, b0, r_prev=None):
    nb = cs.shape[1]
    sr = SUM_ROWS["a"]
    per = A_HALF // sr
    nt = nb * per

    def body(*refs):
        c_ref, b_ref = refs[:2]
        r_ref, buf, send_sems, recv_sem, local_sems = refs[-5:]
        x, y, c, _ = _mesh_pos()
        t = pl.program_id(0)
        slot = lax.rem(t, 2)

        def copies(k, tile):
            dst = r_ref.at[b0 + tile // per, c, pl.ds(pl.multiple_of(lax.rem(tile, per) * sr, 8), sr)]
            keep = pltpu.make_async_copy(buf.at[k], dst, local_sems.at[k])
            give = pltpu.make_async_remote_copy(src_ref=buf.at[k], dst_ref=dst, send_sem=send_sems.at[k],
                                                recv_sem=recv_sem, device_id=(x, y, 1 - c), device_id_type=MESH)
            return keep, give

        @pl.when(t >= 2)
        def _():
            keep, give = copies(slot, t - 2)
            keep.wait()
            give.wait_send()

        buf[slot] = ((c_ref[0, 0].astype(F32) + b_ref[0, 0].astype(F32))
                     + (b_ref[1, 0].astype(F32) + b_ref[2, 0].astype(F32)))
        keep, give = copies(slot, t)
        keep.start()
        give.start()

        @pl.when(t == nt - 1)
        def _():
            for back in (1, 0):
                keep, give = copies(lax.rem(t - back, 2), t - back)
                keep.wait()
                give.wait_send()
            landed = r_ref.at[pl.ds(b0, nb), 1 - c]
            pltpu.make_async_remote_copy(src_ref=landed, dst_ref=landed, send_sem=send_sems.at[0], recv_sem=recv_sem,
                                         device_id=(x, y, c), device_id_type=MESH).wait_recv()

    in_specs = [pl.BlockSpec((1, 1, sr, FF_BLK),
                             lambda t: (2 * lax.axis_index("x") + lax.axis_index("y"), t // per, t % per, 0)),
                pl.BlockSpec((3, 1, sr, FF_BLK), lambda t: (0, t // per, t % per, 0))]
    args = [cs, frm]
    if r_prev is not None:
        in_specs.append(HBM)
        args.append(r_prev)
    return pl.pallas_call(
        body, grid=(nt,), in_specs=in_specs, out_specs=HBM,
        out_shape=jax.ShapeDtypeStruct((4, 2, A_HALF, FF_BLK), F32),
        scratch_shapes=[pltpu.VMEM((2, sr, FF_BLK), F32), pltpu.SemaphoreType.DMA((2,)), pltpu.SemaphoreType.DMA,
                        pltpu.SemaphoreType.DMA((2,))],
        input_output_aliases={2: 0} if r_prev is not None else {},
        name="chip_sum_a" + tag, compiler_params=_params(1))(*args)


def _reduce_scatter(cs_early, from_early, late):
    (cs_late, cs_b), (from_late, from_b) = late
    red = _chip_sum_a("1", cs_early, from_early, 1)
    red = _chip_sum_a("0", cs_late, from_late, 0, red)
    return red.reshape(A_ROWS, FF_BLK), _chip_sum("b", cs_b, from_b)


SMALL_ROWS = 24


def _all_reduce_small(v):
    def body(v_ref, o_ref, all_ref, send_sems, recv_sems):
        x, y, c, _ = _mesh_pos()
        me = 4 * x + 2 * y + c
        all_ref[me] = v_ref[...]
        peers = [(x ^ ((k >> 2) & 1), y ^ ((k >> 1) & 1), c ^ (k & 1)) for k in range(1, 8)]
        idx = lambda p: 4 * p[0] + 2 * p[1] + p[2]
        send = [pltpu.make_async_remote_copy(src_ref=v_ref, dst_ref=all_ref.at[me], send_sem=send_sems.at[k],
                                             recv_sem=recv_sems.at[k], device_id=p, device_id_type=MESH)
                for k, p in enumerate(peers)]
        for cp in send:
            cp.start()
        for k, p in enumerate(peers):
            pltpu.make_async_remote_copy(src_ref=v_ref, dst_ref=all_ref.at[idx(p)], send_sem=send_sems.at[k],
                                         recv_sem=recv_sems.at[k], device_id=p, device_id_type=MESH).wait_recv()
        for cp in send:
            cp.wait_send()
        acc = all_ref[0]
        for d in range(1, 8):
            acc = acc + all_ref[d]
        o_ref[...] = acc

    vm = pl.BlockSpec(memory_space=pltpu.VMEM)
    return pl.pallas_call(
        body, name="all_reduce_small", in_specs=[vm], out_specs=vm,
        out_shape=jax.ShapeDtypeStruct((SMALL_ROWS, 1024), F32),
        scratch_shapes=[pltpu.VMEM((8, SMALL_ROWS, 1024), F32), pltpu.SemaphoreType.DMA((7,)),
                        pltpu.SemaphoreType.DMA((7,))],)(v)


def _adamw(name, w, g, m, v):
    rows, cols = w.shape
    tr = rows
    if rows * cols > 400_000:
        tr = max(t for t in range(8, rows, 8) if rows % t == 0 and t * cols <= 400_000)

    def body(w_ref, g_ref, m_ref, v_ref, d_ref, nm_ref, nv_ref):
        gv = g_ref[...]
        m_new = ADAM_B1 * m_ref[...] + (1.0 - ADAM_B1) * gv
        v_new = ADAM_B2 * v_ref[...] + (1.0 - ADAM_B2) * (gv * gv)
        m_hat = m_new / (1.0 - ADAM_B1 ** ADAM_STEP)
        v_hat = v_new / (1.0 - ADAM_B2 ** ADAM_STEP)
        d_ref[...] = -ADAM_LR * (m_hat / (jnp.sqrt(v_hat) + ADAM_EPS) + ADAM_WD * w_ref[...])
        nm_ref[...] = m_new
        nv_ref[...] = v_new

    spec = pl.BlockSpec((tr, cols), lambda i: (i, 0))
    sds = jax.ShapeDtypeStruct((rows, cols), F32)
    return pl.pallas_call(body, grid=(rows // tr,), in_specs=[spec] * 4, out_specs=[spec] * 3, out_shape=[sds] * 3,
                          name=name, compiler_params=_params(1))(w, g, m, v)


WEIGHTS = ("ffn1_norm", "ffn1_w_gu", "ffn1_w_down", "mix_norm", "ffn2_norm", "ffn2_w_gu", "ffn2_w_down",
           "a_w_in", "a_w_conv", "a_A_log", "a_dt_bias", "a_out_norm", "a_w_out",
           "b_w_in", "b_b_in", "b_sinks", "b_w_out", "b_b_out", "final_norm")
SMALL_SLOTS = {"ffn1_norm": (0, 2048), "mix_norm": (2048, 2048), "ffn2_norm": (4096, 2048), "final_norm": (6144, 1024),
               "a_A_log": (7168, 8), "a_dt_bias": (7296, 8), "a_out_norm": (7424, 128), "b_sinks": (7552, 16),
               "loss": (7680, 1)}
SMALL_SHARDED = {"a_w_conv": (8192, 8192, (4,), 3072), "b_b_in": (20480, 11264, (), 1536), "b_b_out": (22016, 11648, (), 1024)}
DEV_SMALL_ROWS = 12


def _pack_rows(parts):
    rows = []
    for p in parts:
        r = p.reshape(p.shape[0], -1, 1024)
        rows.append(jnp.pad(r, ((0, 0), (0, -r.shape[1] % PACK_TILE), (0, 0))))
    rows.append(jnp.zeros((parts[0].shape[0], PACK_ROWS - PACK_USED, 1024), parts[0].dtype))
    return jnp.concatenate(rows, axis=1)


def _place(vec, off, a):
    return lax.dynamic_update_slice(vec, a.reshape(-1).astype(F32), (off,))


def kernel(x, ffn1_norm, ffn1_w_gu, ffn1_w_down, mix_norm, ffn2_norm, ffn2_w_gu, ffn2_w_down, a_w_in, a_w_conv, a_A_log, a_dt_bias, a_out_norm, a_w_out, b_w_in, b_b_in, b_sinks, b_w_out, b_b_out, final_norm, loss_target, m_ffn1_norm, m_ffn1_w_gu, m_ffn1_w_down, m_mix_norm, m_ffn2_norm, m_ffn2_w_gu, m_ffn2_w_down, m_a_w_in, m_a_w_conv, m_a_A_log, m_a_dt_bias, m_a_out_norm, m_a_w_out, m_b_w_in, m_b_b_in, m_b_sinks, m_b_w_out, m_b_b_out, m_final_norm, v_ffn1_norm, v_ffn1_w_gu, v_ffn1_w_down, v_mix_norm, v_ffn2_norm, v_ffn2_w_gu, v_ffn2_w_down, v_a_w_in, v_a_w_conv, v_a_A_log, v_a_dt_bias, v_a_out_norm, v_a_w_out, v_b_w_in, v_b_b_in, v_b_sinks, v_b_w_out, v_b_b_out, v_final_norm):
    w = dict(zip(WEIGHTS, (ffn1_norm, ffn1_w_gu, ffn1_w_down, mix_norm, ffn2_norm, ffn2_w_gu, ffn2_w_down, a_w_in, a_w_conv,
                           a_A_log, a_dt_bias, a_out_norm, a_w_out, b_w_in, b_b_in, b_sinks, b_w_out, b_b_out, final_norm)))
    m = dict(zip(WEIGHTS, (m_ffn1_norm, m_ffn1_w_gu, m_ffn1_w_down, m_mix_norm, m_ffn2_norm, m_ffn2_w_gu, m_ffn2_w_down,
                           m_a_w_in, m_a_w_conv, m_a_A_log, m_a_dt_bias, m_a_out_norm, m_a_w_out, m_b_w_in, m_b_b_in,
                           m_b_sinks, m_b_w_out, m_b_b_out, m_final_norm)))
    v = dict(zip(WEIGHTS, (v_ffn1_norm, v_ffn1_w_gu, v_ffn1_w_down, v_mix_norm, v_ffn2_norm, v_ffn2_w_gu, v_ffn2_w_down,
                           v_a_w_in, v_a_w_conv, v_a_A_log, v_a_dt_bias, v_a_out_norm, v_a_w_out, v_b_w_in, v_b_b_in,
                           v_b_sinks, v_b_w_out, v_b_b_out, v_final_norm)))
    chip = 2 * lax.axis_index("x") + lax.axis_index("y")

    big_a4 = jnp.concatenate([w["ffn1_w_gu"], w["ffn2_w_gu"]], axis=0).astype(BF16).reshape(4, 2, A_HALF, FF_BLK)
    big_b = _pack_rows([w[n].astype(BF16).reshape(1, -1) for n, _ in PACK])[0]
    small = jnp.zeros((4096,), F32)
    small = _place(small, 0, w["a_w_conv"])
    small = _place(small, 3072, w["b_b_in"])
    small = _place(small, 3456, w["b_b_out"]).reshape(SMALL_SHARD)
    ga = _fill_a(0, big_a4, _gather_chips(big_a4)).reshape(4, 4, D_MODEL, FF_BLK)
    offs, o = {}, 0
    for n, r in PACK:
        offs[n] = (o, r)
        o += -(-r // PACK_TILE) * PACK_TILE

    def finish_weights(rb, rs):
        gb = _gather_fill("b", big_b, rb)
        blk = lambda n: gb[:, offs[n][0]:offs[n][0] + offs[n][1]]
        gsf = lax.dynamic_update_slice(jnp.zeros((4, 4096), F32), small.reshape(1, 4096), (chip, 0))
        for j, other in enumerate((chip ^ 2, chip ^ 1, chip ^ 3)):
            gsf = lax.dynamic_update_slice(gsf, rs[j].reshape(1, 4096), (other, 0))
        return {
            "w_down": gb[:, 0:2816].reshape(4, 4, 704, 1024).transpose(1, 0, 2, 3).reshape(4, 2, FF_BLK, 1024),
            "a_w_in": jnp.pad(blk("a_w_in").reshape(4, 1024, 1028).transpose(1, 0, 2).reshape(1024, A_IN_COLS),
                              ((0, 0), (0, A_IN_PAD - A_IN_COLS))),
            "a_w_out": blk("a_w_out").reshape(1024, 1024),
            "b_w_in": blk("b_w_in").reshape(4, 1024, 384).transpose(1, 0, 2).reshape(1024, 1536),
            "b_w_out": blk("b_w_out").reshape(1024, 1024),
            "a_w_conv": gsf[:, 0:3072].reshape(4, 4, 768).transpose(1, 0, 2).reshape(4, 3072),
            "b_b_in": gsf[:, 3072:3456].reshape(1, 1536),
            "b_b_out": gsf[:, 3456:3712].reshape(1, 1024)}

    def pack_b(g):
        down = g["w_down"].reshape(4, 4, 704, 1024).transpose(1, 0, 2, 3)
        parts = [down[:, 0:2], down[:, 2:4],
                 g["a_w_in"][:, :A_IN_COLS].reshape(1024, 4, 1028).transpose(1, 0, 2), g["a_w_out"].reshape(4, 256, 1024),
                 g["b_w_in"].reshape(1024, 4, 384).transpose(1, 0, 2), g["b_w_out"].reshape(4, 256, 1024)]
        return _pack_rows([a.astype(BF16).reshape(4, -1) for a in parts])

    W = {n: w[n] for n in ("ffn1_norm", "ffn2_norm", "mix_norm", "a_out_norm")}
    W["ga"] = ga
    W["a_gate_p"] = jnp.pad(jnp.concatenate([-jnp.exp(w["a_A_log"]), w["a_dt_bias"]], axis=0), ((0, 0), (8, 112)))
    W["b_sinks"] = jnp.broadcast_to(w["b_sinks"][0][:, None, None], (B_HEADS, 1, 128))
    W["final_norm"] = w["final_norm"][None]

    comm = {"a4": big_a4, "big_b": big_b, "small": small, "finish": finish_weights, "pack_b": pack_b}
    loss_p, dx, g = _local_step(x[0], loss_target[0], W, comm)

    red_a, red_b = _reduce_scatter(g["cs_early"], g["from_early"], g["late"])
    grads = {n: red_b[offs[n][0]:offs[n][0] + offs[n][1]].reshape(w[n].shape) for n, _ in PACK}
    grads["ffn1_w_gu"] = red_a[:A_ROWS // 2].reshape(w["ffn1_w_gu"].shape)
    grads["ffn2_w_gu"] = red_a[A_ROWS // 2:].reshape(w["ffn2_w_gu"].shape)

    sv = jnp.zeros((SMALL_ROWS * 1024,), F32)
    small_g = {"ffn1_norm": g["ffn1_norm"], "mix_norm": g["mix_norm"], "ffn2_norm": g["ffn2_norm"], "final_norm": g["final_norm"],
               "a_A_log": g["a_gate_p"][0, 8:16], "a_dt_bias": g["a_gate_p"][1, 8:16], "a_out_norm": g["a_out_norm"],
               "b_sinks": g["b_sinks"], "loss": loss_p[0, 0:1]}
    for n, (off, _) in SMALL_SLOTS.items():
        sv = _place(sv, off, small_g[n])
    for n, (off, _, _, _) in SMALL_SHARDED.items():
        sv = _place(sv, off, g[n])
    tot = _all_reduce_small(sv.reshape(SMALL_ROWS, 1024)).reshape(-1)
    for n, (off, size) in SMALL_SLOTS.items():
        if n != "loss":
            grads[n] = tot[off:off + size].reshape(w[n].shape)
    for n, (off, _, lead, last) in SMALL_SHARDED.items():
        full = tot[off:off + (lead[0] if lead else 1) * last].reshape(lead + (last,))
        width = last // 4
        grads[n] = lax.dynamic_slice_in_dim(full, chip * width, width, axis=-1).reshape(w[n].shape)
    loss = tot[SMALL_SLOTS["loss"][0]]

    delta, new_m, new_v = {}, {}, {}
    for n in ("ffn1_w_gu", "ffn2_w_gu") + tuple(n for n, _ in PACK):
        two_d = lambda a: a.reshape(-1, a.shape[-1])
        d, nm, nv = _adamw("adamw_" + n, two_d(w[n]), two_d(grads[n]), two_d(m[n]), two_d(v[n]))
        delta[n], new_m[n], new_v[n] = d.reshape(w[n].shape), nm.reshape(w[n].shape), nv.reshape(w[n].shape)

    def dev_small(src):
        vec = jnp.zeros((DEV_SMALL_ROWS * 1024,), F32)
        for n, (off, _) in SMALL_SLOTS.items():
            if n != "loss":
                vec = _place(vec, off, src[n])
        for n, (_, off, _, _) in SMALL_SHARDED.items():
            vec = _place(vec, off, src[n])
        return vec.reshape(DEV_SMALL_ROWS, 1024)

    sd, sm, svv = _adamw("adamw_small", dev_small(w), dev_small(grads), dev_small(m), dev_small(v))
    for n in WEIGHTS:
        if n in SMALL_SLOTS:
            off, size = SMALL_SLOTS[n]
        elif n in SMALL_SHARDED:
            off, size = SMALL_SHARDED[n][1], w[n].size
        else:
            continue
        for dst, src in ((delta, sd), (new_m, sm), (new_v, svv)):
            dst[n] = src.reshape(-1)[off:off + size].reshape(w[n].shape)

    return (loss, dx[None], *[grads[n] for n in WEIGHTS], *[delta[n] for n in WEIGHTS],
            *[new_m[n] for n in WEIGHTS], *[new_v[n] for n in WEIGHTS])
```

```python
import jax
import jax.numpy as jnp
from jax import lax
from jax.experimental import pallas as pl
from jax.experimental.pallas import tpu as pltpu

F32 = jnp.float32
BF16 = jnp.bfloat16

D_MODEL = 1024
EPS = 1e-6
FF_BLK = 1408
A_HEADS = 8
A_DK = 128
A_CHUNK = 64
A_HG = 8
A_IN_COLS = 4112
A_IN_PAD = 4224
B_HEADS = 16
B_KV = 4
B_HD = 64
B_BLK = 128
ADAM_LR, ADAM_B1, ADAM_B2, ADAM_EPS, ADAM_WD, ADAM_STEP = 0.001, 0.9, 0.999, 1e-08, 0.01, 10
MESH = pl.DeviceIdType.MESH
VMEM_LIMIT = 56 * 1024 * 1024
HBM = pl.BlockSpec(memory_space=pl.ANY)


def _params(n_axes):
    return pltpu.CompilerParams(dimension_semantics=("arbitrary",) * n_axes, vmem_limit_bytes=VMEM_LIMIT)


def _sigmoid(x):
    return 1.0 / (1.0 + jnp.exp(-x))


def _dot(a, b, ca, cb):
    return lax.dot_general(a, b, (((ca,), (cb,)), ((), ())), preferred_element_type=F32)


def _dotb(a, b, ca=1, cb=0):
    return _dot(a.astype(BF16), b.astype(BF16), ca, cb)


def _dotx(a, b, ca=1, cb=0):
    return lax.dot_general(a, b, (((ca,), (cb,)), ((), ())), preferred_element_type=F32,
                           precision=lax.Precision.HIGHEST)


def _doth(a, b, ca=1, cb=0):
    return lax.dot_general(a, b, (((ca,), (cb,)), ((), ())), preferred_element_type=F32,
                           precision=lax.Precision.HIGH)


def _rms_fwd(name, x, w):
    T, D = x.shape
    tt = min(T, 512)

    def body(x_ref, w_ref, h_ref):
        xv = x_ref[...]
        r = lax.rsqrt(jnp.mean(xv * xv, axis=-1, keepdims=True) + EPS)
        h_ref[...] = (xv * r * w_ref[...]).astype(BF16)

    return pl.pallas_call(
        body, grid=(T // tt,),
        in_specs=[pl.BlockSpec((tt, D), lambda i: (i, 0)), pl.BlockSpec((1, D), lambda i: (0, 0))],
        out_specs=pl.BlockSpec((tt, D), lambda i: (i, 0)),
        out_shape=jax.ShapeDtypeStruct((T, D), BF16), name=name, compiler_params=_params(1))(x, w)


def _rms_bwd_tile(dh, xv, dy, w):
    r = lax.rsqrt(jnp.mean(xv * xv, axis=-1, keepdims=True) + EPS)
    xhat = xv * r
    dxhat = dh * w
    dx = dy + r * (dxhat - xhat * jnp.mean(dxhat * xhat, axis=-1, keepdims=True))
    return dx, jnp.sum(dh * xhat, axis=0, keepdims=True)


def _colsum(name, a):
    T, N = a.shape
    tt = min(T, 512)

    def body(a_ref, o_ref):
        @pl.when(pl.program_id(0) == 0)
        def _():
            o_ref[...] = jnp.zeros_like(o_ref)
        o_ref[...] += jnp.sum(a_ref[...].astype(F32), axis=0, keepdims=True)

    return pl.pallas_call(
        body, grid=(T // tt,), in_specs=[pl.BlockSpec((tt, N), lambda i: (i, 0))],
        out_specs=pl.BlockSpec((1, N), lambda i: (0, 0)),
        out_shape=jax.ShapeDtypeStruct((1, N), F32), name=name, compiler_params=_params(1))(a)


def _matmul(name, a, b, ca, cb, tm, tn, tk, extra_in, outs, epi, order="ji"):
    M, K, N = a.shape[1 - ca], a.shape[ca], b.shape[1 - cb]
    tm, tn, tk = min(tm, M), min(tn, N), min(tk, K)
    assert M % tm == 0 and N % tn == 0 and K % tk == 0, (name, M, N, K, tm, tn, tk)
    ni, nj, nk = M // tm, N // tn, K // tk
    if order == "ji":
        grid = (nj, ni, nk)
        perm = lambda g0, g1, g2: (g1, g0, g2)
    else:
        grid = (ni, nj, nk)
        perm = lambda g0, g1, g2: (g0, g1, g2)

    def wrap(f):
        return lambda g0, g1, g2: f(*perm(g0, g1, g2))

    a_spec = (pl.BlockSpec((tm, tk), wrap(lambda i, j, k: (i, k))) if ca == 1
              else pl.BlockSpec((tk, tm), wrap(lambda i, j, k: (k, i))))
    b_spec = (pl.BlockSpec((tk, tn), wrap(lambda i, j, k: (k, j))) if cb == 0
              else pl.BlockSpec((tn, tk), wrap(lambda i, j, k: (j, k))))
    ne, no = len(extra_in), len(outs)

    def body(*refs):
        a_ref, b_ref = refs[0], refs[1]
        ex, out = refs[2:2 + ne], refs[2 + ne:2 + ne + no]
        i, j, k = perm(pl.program_id(0), pl.program_id(1), pl.program_id(2))
        p = _dotb(a_ref[...], b_ref[...], ca, cb)
        if nk == 1:
            epi(p, ex, out, i, j)
        else:
            acc_ref = refs[-1]

            @pl.when(k == 0)
            def _():
                acc_ref[...] = p

            @pl.when(k > 0)
            def _():
                acc_ref[...] += p

            @pl.when(k == nk - 1)
            def _():
                epi(acc_ref[...], ex, out, i, j)

    return pl.pallas_call(
        body, grid=grid,
        in_specs=[a_spec, b_spec] + [pl.BlockSpec(bs, wrap(f)) for _, bs, f in extra_in],
        out_specs=[pl.BlockSpec(bs, wrap(f)) for _, bs, f in outs],
        out_shape=[s for s, _, _ in outs],
        scratch_shapes=[pltpu.VMEM((tm, tn), F32)] if nk > 1 else [],
        name=name, compiler_params=_params(3))(a, b, *[x for x, _, _ in extra_in])


def _mm_plain(name, a, b, ca, cb, out_dtype, tm=1024, tn=1024, tk=1024, scale=1.0, bias=None):
    M, N = a.shape[1 - ca], b.shape[1 - cb]
    tm, tn = min(tm, M), min(tn, N)
    extra = [] if bias is None else [(bias, (1, tn), lambda i, j, k: (0, j))]

    def epi(acc, ex, out, i, j):
        r = acc * scale if scale != 1.0 else acc
        if bias is not None:
            r = r + ex[0][...]
        out[0][...] = r.astype(out_dtype)

    return _matmul(name, a, b, ca, cb, tm, tn, tk, extra,
                   [(jax.ShapeDtypeStruct((M, N), out_dtype), (tm, tn), lambda i, j, k: (i, j))], epi)[0]


def _mm_residual(name, a, b, x, scale, bias=None, tk=1024):
    M, N = x.shape
    tm, tn = min(512, M), N
    extra = [(x, (tm, tn), lambda i, j, k: (i, j))]
    if bias is not None:
        extra.append((bias, (1, tn), lambda i, j, k: (0, j)))

    def epi(acc, ex, out, i, j):
        r = acc if bias is None else acc + ex[1][...]
        out[0][...] = ex[0][...] + scale * r

    return _matmul(name, a, b, 1, 0, tm, tn, tk, extra,
                   [(jax.ShapeDtypeStruct((M, N), F32), (tm, tn), lambda i, j, k: (i, j))], epi, order="ij")[0]


def _mm_rms_bwd(name, dproj, w_in, x, dy, nw, tk, out_scale):
    M, N = x.shape
    tm = min(512, M)
    extra = [(x, (tm, N), lambda i, j, k: (i, 0)), (dy, (tm, N), lambda i, j, k: (i, 0)),
             (nw, (1, N), lambda i, j, k: (0, 0))]

    def epi(acc, ex, out, i, j):
        dx, dw = _rms_bwd_tile(acc, ex[0][...], ex[1][...], ex[2][...])
        out[0][...] = dx
        out[1][...] = (dx * out_scale).astype(BF16)

        @pl.when(i == 0)
        def _():
            out[2][...] = dw

        @pl.when(i > 0)
        def _():
            out[2][...] += dw

    return _matmul(name, dproj, w_in, 1, 1, tm, N, tk, extra,
                   [(jax.ShapeDtypeStruct((M, N), F32), (tm, N), lambda i, j, k: (i, 0)),
                    (jax.ShapeDtypeStruct((M, N), BF16), (tm, N), lambda i, j, k: (i, 0)),
                    (jax.ShapeDtypeStruct((1, N), F32), (1, N), lambda i, j, k: (0, 0))], epi, order="ij")


def _peer_copies(pairs, send_sems, recv_sems):
    x, y, c, chips = _mesh_pos()
    n = len(pairs)
    return [pltpu.make_async_remote_copy(src_ref=src(chip), dst_ref=dst.at[j], send_sem=send_sems.at[n * j + k],
                                         recv_sem=recv_sems.at[n * j + k], device_id=(*chip, c), device_id_type=MESH)
            for j, chip in enumerate(chips) for k, (src, dst) in enumerate(pairs)]


def _ffn_gu(name, x, nw, ga, blk, exchange=None):
    T, D = x.shape
    tm = min(T, 512)
    rs = min(tm, 256)
    ni = T // tm

    def body(*refs):
        x_ref, nw_ref, wg0, wg1, wu0, wu1 = refs[:6]
        if exchange is None:
            h_ref, gu_ref, act_ref = refs[6:]
        else:
            bb_ref, sm_ref, h_ref, gu_ref, act_ref, rb_ref, rs_ref, send_sems, recv_sems = refs[6:]
            c = lax.axis_index("c")
            pairs = [(lambda chip: bb_ref.at[_half(PACK_ROWS, c)], rb_ref), (lambda chip: sm_ref, rs_ref)]

            @pl.when(pl.program_id(0) == 0)
            def _():
                for cp in _peer_copies(pairs, send_sems, recv_sems):
                    cp.start()

            @pl.when(pl.program_id(0) == ni - 1)
            def _():
                for cp in _peer_copies(pairs, send_sems, recv_sems):
                    cp.wait_recv()
                    cp.wait_send()

        for r in range(tm // rs):
            rows = pl.ds(r * rs, rs)
            xv = x_ref[rows, :]
            hv = (xv * lax.rsqrt(jnp.mean(xv * xv, axis=-1, keepdims=True) + EPS) * nw_ref[...]).astype(BF16)
            h_ref[rows, :] = hv
            for j, (wg_ref, wu_ref) in enumerate(((wg0, wu0), (wg1, wu1))):
                g = _dot(hv, wg_ref[0, 0], 1, 0)
                u = _dot(hv, wu_ref[0, 0], 1, 0)
                s = _sigmoid(g)
                gs = g * s
                gu_ref[0, j, rows, :] = (u * (s + gs * (1.0 - s))).astype(BF16)
                gu_ref[1, j, rows, :] = gs.astype(BF16)
                act_ref[j, rows, :] = (gs * u).astype(BF16)

    wspec = lambda q: pl.BlockSpec((1, 1, D, FF_BLK), lambda i: (q, blk, 0, 0), pipeline_mode=pl.Buffered(1))
    in_specs = [pl.BlockSpec((tm, D), lambda i: (i, 0)), pl.BlockSpec((1, D), lambda i: (0, 0)),
                wspec(0), wspec(1), wspec(2), wspec(3)]
    out_specs = [pl.BlockSpec((tm, D), lambda i: (i, 0)), pl.BlockSpec((2, 2, tm, FF_BLK), lambda i: (0, 0, i, 0)),
                 pl.BlockSpec((2, tm, FF_BLK), lambda i: (0, i, 0))]
    out_shape = [jax.ShapeDtypeStruct((T, D), BF16), jax.ShapeDtypeStruct((2, 2, T, FF_BLK), BF16),
                 jax.ShapeDtypeStruct((2, T, FF_BLK), BF16)]
    args, scratch = [x, nw, ga, ga, ga, ga], []
    if exchange is not None:
        big_b, small = exchange
        in_specs += [HBM, HBM]
        out_specs += [HBM, HBM]
        out_shape += [jax.ShapeDtypeStruct((3, big_b.shape[0] // 2, big_b.shape[1]), BF16),
                      jax.ShapeDtypeStruct((3,) + small.shape, F32)]
        scratch = [pltpu.SemaphoreType.DMA((6,)), pltpu.SemaphoreType.DMA((6,))]
        args += [big_b, small]
    return pl.pallas_call(body, grid=(ni,), in_specs=in_specs, out_specs=out_specs, out_shape=out_shape,
                          scratch_shapes=scratch, name=name, compiler_params=_params(1))(*args)


def _ffn_down(name, act, wd, x):
    T, D = x.shape
    tm = min(T, 512)

    def body(act_ref, wd_ref, x_ref, o_ref):
        acc = _dot(act_ref[0], wd_ref[0], 1, 0) + _dot(act_ref[1], wd_ref[1], 1, 0)
        o_ref[...] = x_ref[...] + 0.5 * acc

    return pl.pallas_call(
        body, grid=(T // tm,),
        in_specs=[pl.BlockSpec((2, tm, FF_BLK), lambda i: (0, i, 0)),
                  pl.BlockSpec((2, FF_BLK, D), lambda i: (0, 0, 0)),
                  pl.BlockSpec((tm, D), lambda i: (i, 0))],
        out_specs=pl.BlockSpec((tm, D), lambda i: (i, 0)),
        out_shape=jax.ShapeDtypeStruct((T, D), F32), name=name, compiler_params=_params(1))(act, wd, x)


def _ffn_dact(name, dyh, wd, gu):
    T, D = dyh.shape
    tm = min(T, 1024)
    rs = min(tm, 256)

    def body(dy_ref, wd_ref, gu_ref, o_ref):
        for r in range(tm // rs):
            rows = pl.ds(r * rs, rs)
            dact = _dot(dy_ref[rows, :], wd_ref[0], 1, 1)
            o_ref[0, 0, rows, :] = (dact * gu_ref[0, 0, rows, :].astype(F32)).astype(BF16)
            o_ref[1, 0, rows, :] = (dact * gu_ref[1, 0, rows, :].astype(F32)).astype(BF16)

    return pl.pallas_call(
        body, grid=(2, T // tm),
        in_specs=[pl.BlockSpec((tm, D), lambda j, i: (i, 0)),
                  pl.BlockSpec((1, FF_BLK, D), lambda j, i: (j, 0, 0)),
                  pl.BlockSpec((2, 1, tm, FF_BLK), lambda j, i: (0, j, i, 0))],
        out_specs=pl.BlockSpec((2, 1, tm, FF_BLK), lambda j, i: (0, j, i, 0)),
        out_shape=jax.ShapeDtypeStruct((2, 2, T, FF_BLK), BF16), name=name, compiler_params=_params(2))(dyh, wd, gu)


def _ffn_dwd(name, act, dyh):
    _, T, _ = act.shape
    D = dyh.shape[1]
    tk = min(T, 2048)
    nk = T // tk

    def body(a_ref, d_ref, o_ref, acc_ref):
        k = pl.program_id(1)
        p = _dot(a_ref[0], d_ref[...], 0, 0)

        @pl.when(k == 0)
        def _():
            acc_ref[...] = p

        @pl.when(k > 0)
        def _():
            acc_ref[...] += p

        @pl.when(k == nk - 1)
        def _():
            o_ref[0] = acc_ref[...].astype(BF16)

    return pl.pallas_call(
        body, grid=(2, nk),
        in_specs=[pl.BlockSpec((1, tk, FF_BLK), lambda j, k: (j, k, 0)), pl.BlockSpec((tk, D), lambda j, k: (k, 0))],
        out_specs=pl.BlockSpec((1, FF_BLK, D), lambda j, k: (j, 0, 0)),
        out_shape=jax.ShapeDtypeStruct((2, FF_BLK, D), BF16), scratch_shapes=[pltpu.VMEM((FF_BLK, D), F32)],
        name=name, compiler_params=_params(2))(act, dyh)


def _ffn_dwgu(name, h, dgu, pa, blk, exchange=None):
    T, D = h.shape
    tk = min(T, 2048)
    nk = T // tk

    def body(*refs):
        h_ref, d_ref, pa_in = refs[:3]
        if exchange is None:
            o_ref, acc_ref = refs[3:]
        else:
            cs_ref, o_ref, f_ref, acc_ref, send_sems, recv_sems = refs[3:]
            pairs = [(lambda chip: cs_ref.at[2 * chip[0] + chip[1]], f_ref)]

            @pl.when((pl.program_id(0) == 0) & (pl.program_id(1) == 0))
            def _():
                for cp in _peer_copies(pairs, send_sems, recv_sems):
                    cp.start()

            @pl.when((pl.program_id(0) == 3) & (pl.program_id(1) == nk - 1))
            def _():
                for cp in _peer_copies(pairs, send_sems, recv_sems):
                    cp.wait_recv()
                    cp.wait_send()

        k = pl.program_id(1)
        p = _dot(h_ref[...], d_ref[0, 0], 0, 0)

        @pl.when(k == 0)
        def _():
            acc_ref[...] = p

        @pl.when(k > 0)
        def _():
            acc_ref[...] += p

        @pl.when(k == nk - 1)
        def _():
            o_ref[0, 0] = acc_ref[...].astype(BF16)

    in_specs = [pl.BlockSpec((tk, D), lambda q, k: (k, 0)),
                pl.BlockSpec((1, 1, tk, FF_BLK), lambda q, k: (q // 2, q % 2, k, 0)), HBM]
    out_specs = [pl.BlockSpec((1, 1, D, FF_BLK), lambda q, k: (q, blk, 0, 0))]
    out_shape = [jax.ShapeDtypeStruct(pa.shape, BF16)]
    scratch, args = [pltpu.VMEM((D, FF_BLK), F32)], [h, dgu, pa]
    if exchange is not None:
        in_specs.append(HBM)
        out_specs.append(HBM)
        out_shape.append(jax.ShapeDtypeStruct((3,) + exchange.shape[1:], BF16))
        scratch += [pltpu.SemaphoreType.DMA((3,)), pltpu.SemaphoreType.DMA((3,))]
        args.append(exchange)
    out = pl.pallas_call(body, grid=(4, nk), in_specs=in_specs, out_specs=out_specs, out_shape=out_shape,
                         scratch_shapes=scratch, input_output_aliases={2: 0}, name=name,
                         compiler_params=_params(2))(*args)
    return out[0] if exchange is None else out


def _ffn_dx(name, dgu, ga, blk, x, dy, nw, out_scale, exchange=None):
    T, D = x.shape
    tm = min(T, 512)
    ni = T // tm

    def body(*refs):
        d_ref, w0, w1, w2, w3, x_ref, dy_ref, nw_ref = refs[:8]
        if exchange is None:
            dx_ref, dxb_ref, dnw_ref = refs[8:]
        else:
            n = len(exchange)
            dx_ref, dxb_ref, dnw_ref = refs[8 + n:11 + n]
            send_sems, recv_sems = refs[-2:]
            pairs = [(lambda chip, r=r: r.at[2 * chip[0] + chip[1]], f)
                     for r, f in zip(refs[8:8 + n], refs[11 + n:11 + 2 * n])]

            @pl.when(pl.program_id(0) == 0)
            def _():
                for cp in _peer_copies(pairs, send_sems, recv_sems):
                    cp.start()

            @pl.when(pl.program_id(0) == ni - 1)
            def _():
                for cp in _peer_copies(pairs, send_sems, recv_sems):
                    cp.wait_recv()
                    cp.wait_send()

        i = pl.program_id(0)
        acc = (_dot(d_ref[0, 0], w0[0, 0], 1, 1) + _dot(d_ref[0, 1], w1[0, 0], 1, 1)
               + _dot(d_ref[1, 0], w2[0, 0], 1, 1) + _dot(d_ref[1, 1], w3[0, 0], 1, 1))
        dx, dw = _rms_bwd_tile(acc, x_ref[...], dy_ref[...], nw_ref[...])
        dx_ref[...] = dx
        dxb_ref[...] = (dx * out_scale).astype(BF16)

        @pl.when(i == 0)
        def _():
            dnw_ref[...] = dw

        @pl.when(i > 0)
        def _():
            dnw_ref[...] += dw

    wspec = lambda q: pl.BlockSpec((1, 1, D, FF_BLK), lambda i: (q, blk, 0, 0), pipeline_mode=pl.Buffered(1))
    row = pl.BlockSpec((tm, D), lambda i: (i, 0))
    in_specs = [pl.BlockSpec((2, 2, tm, FF_BLK), lambda i: (0, 0, i, 0)), wspec(0), wspec(1), wspec(2), wspec(3),
                row, row, pl.BlockSpec((1, D), lambda i: (0, 0))]
    out_specs = [row, row, pl.BlockSpec((1, D), lambda i: (0, 0))]
    out_shape = [jax.ShapeDtypeStruct((T, D), F32), jax.ShapeDtypeStruct((T, D), BF16), jax.ShapeDtypeStruct((1, D), F32)]
    args, scratch = [dgu, ga, ga, ga, ga, x, dy, nw], []
    if exchange is not None:
        in_specs += [HBM] * len(exchange)
        out_specs += [HBM] * len(exchange)
        out_shape += [jax.ShapeDtypeStruct((3,) + cs.shape[1:], BF16) for cs in exchange]
        scratch = [pltpu.SemaphoreType.DMA((3 * len(exchange),)), pltpu.SemaphoreType.DMA((3 * len(exchange),))]
        args += list(exchange)
    return pl.pallas_call(body, grid=(ni,), in_specs=in_specs, out_specs=out_specs, out_shape=out_shape,
                          scratch_shapes=scratch, name=name, compiler_params=_params(1))(*args)


def _ffn_fwd(tag, x, nw, ga, blk, wd):
    h, gu, act = _ffn_gu(tag + "_gu", x, nw, ga, blk)
    return _ffn_down(tag + "_down", act, wd, x), (h, gu, act)


def _ffn_bwd(tag, dy, dyh, x, nw, ga, blk, wd, saved, pa, out_scale):
    h, gu, act = saved
    dgu = _ffn_dact(tag + "_dact", dyh, wd, gu)
    d_wd = _ffn_dwd(tag + "_dwd", act, dyh)
    pa = _ffn_dwgu(tag + "_dwgu", h, dgu, pa, blk)
    dx, dxb, d_nw = _ffn_dx(tag + "_dx", dgu, ga, blk, x, dy, nw, out_scale)
    return dx, dxb, d_nw, pa, d_wd


def _conv_taps(cur, halo, w, first, sign):
    tt = cur.shape[0]
    halo = jnp.where(first, 0.0, halo)
    rid = lax.broadcasted_iota(jnp.int32, (8, cur.shape[1]), 0)
    acc = w[3:4, :] * cur
    for s in (1, 2, 3):
        if sign < 0:
            sh = pltpu.roll(cur, s, 0)
            edge = jnp.where(rid < s, pltpu.roll(halo, s, 0), sh[0:8])
            sh = jnp.concatenate([edge, sh[8:]], axis=0) if tt > 8 else edge
        else:
            sh = pltpu.roll(cur, tt - s, 0)
            edge = jnp.where(rid >= 8 - s, pltpu.roll(halo, 8 - s, 0), sh[tt - 8:])
            sh = jnp.concatenate([sh[:tt - 8], edge], axis=0) if tt > 8 else edge
        acc = acc + w[3 - s:4 - s, :] * sh
    return acc


def _gdn_prep(name, proj, wconv, gate_p):
    T = proj.shape[0]
    tt = min(T, 256)
    hb = tt // 8
    nch = tt // A_CHUNK

    def body(cur_ref, halo_ref, ba_ref, w_ref, gp_ref, qkv_ref, bg_ref, gc_ref):
        first = pl.program_id(0) == 0
        for c in range(24):
            cols = pl.ds(c * 128, 128)
            conv = _conv_taps(cur_ref[:, cols], halo_ref[:, cols], w_ref[:, cols], first, -1)
            y = conv * _sigmoid(conv)
            if c < 16:
                y = y * lax.rsqrt(jnp.sum(y * y, axis=-1, keepdims=True) + EPS)
                if c < 8:
                    y = y * (A_DK ** -0.5)
            qkv_ref[:, cols] = y
        ba = ba_ref[...]
        lane = lax.broadcasted_iota(jnp.int32, ba.shape, 1)
        zarg = ba + gp_ref[1:2, :]
        softplus = jnp.maximum(zarg, 0.0) + jnp.log(1.0 + jnp.exp(-jnp.abs(zarg)))
        bg = jnp.where(lane < 8, _sigmoid(ba), jnp.where(lane < 16, gp_ref[0:1, :] * softplus, 0.0))
        bg_ref[...] = bg
        tri = (lax.broadcasted_iota(jnp.int32, (A_CHUNK, A_CHUNK), 0)
               >= lax.broadcasted_iota(jnp.int32, (A_CHUNK, A_CHUNK), 1)).astype(F32)
        for c in range(nch):
            rows = pl.ds(c * A_CHUNK, A_CHUNK)
            gc_ref[rows, :] = _dotx(tri, bg[c * A_CHUNK:(c + 1) * A_CHUNK, :])

    return pl.pallas_call(
        body, grid=(T // tt,),
        in_specs=[pl.BlockSpec((tt, 3072), lambda i: (i, 0)),
                  pl.BlockSpec((8, 3072), lambda i: (jnp.maximum(i * hb - 1, 0), 0)),
                  pl.BlockSpec((tt, 128), lambda i: (i, 32)),
                  pl.BlockSpec((4, 3072), lambda i: (0, 0)),
                  pl.BlockSpec((2, 128), lambda i: (0, 0))],
        out_specs=[pl.BlockSpec((tt, 3072), lambda i: (i, 0)), pl.BlockSpec((tt, 128), lambda i: (i, 0)),
                   pl.BlockSpec((tt, 128), lambda i: (i, 0))],
        out_shape=[jax.ShapeDtypeStruct((T, 3072), F32), jax.ShapeDtypeStruct((T, 128), F32),
                   jax.ShapeDtypeStruct((T, 128), F32)],
        name=name, compiler_params=_params(1))(proj, proj, proj, wconv, gate_p)


def _chunk_masks():
    ri = lax.broadcasted_iota(jnp.int32, (A_CHUNK, A_CHUNK), 0)
    ci = lax.broadcasted_iota(jnp.int32, (A_CHUNK, A_CHUNK), 1)
    return ri >= ci, ri > ci, ri == ci


def _chunk_local(q, k, gcol, grow, bcol):
    incl, strict, _ = _chunk_masks()
    dec = jnp.where(incl, jnp.exp(jnp.where(incl, gcol - grow, 0.0)), 0.0)
    e = jnp.exp(gcol)
    glast = grow[:, A_CHUNK - 1:A_CHUNK]
    f = jnp.exp(glast - gcol)
    gl = jnp.exp(glast)
    kb = k * bcol
    lmat = jnp.where(strict, _dotb(kb, k, 1, 1) * dec, 0.0)
    amat = jnp.where(incl, _dotb(q, k, 1, 1) * dec, 0.0)
    return dec, e, f, gl, kb, lmat, amat


def _unit_lower_inverse(lmats):
    _, _, eye = _chunk_masks()
    ts = [jnp.where(eye, 1.0, 0.0) - lm for lm in lmats]
    lps = [_doth(lm, lm) for lm in lmats]
    for it in range(5):
        ts = [t + _doth(t, lp) for t, lp in zip(ts, lps)]
        if it < 4:
            lps = [_doth(lp, lp) for lp in lps]
    return ts


def _gate_columns(bgt, gct):
    sel = (lax.broadcasted_iota(jnp.int32, (16, 128), 0) == lax.broadcasted_iota(jnp.int32, (16, 128), 1)).astype(F32)
    g_rows = _dotx(sel, gct, 1, 1)
    hs = range(A_HEADS)
    return ([bgt[:, h:h + 1] for h in hs], [gct[:, 8 + h:9 + h] for h in hs], [g_rows[8 + h:9 + h, :] for h in hs])


def _gdn_delta_fwd(name, qkv, bg, gcum, a4=None):
    T = qkv.shape[0]
    tt = min(T, 512)
    nch = tt // A_CHUNK
    NC = T // A_CHUNK
    ni = T // tt
    wd, ng = 128 * A_HG, A_HEADS // A_HG
    assert ng == 1

    def body(*refs):
        q_ref, k_ref, v_ref, bg_ref, gc_ref = refs[:5]
        if a4 is None:
            o_ref, s_ref, t_ref, u_ref, w_ref, state = refs[5:]
        else:
            a4_ref, o_ref, s_ref, t_ref, u_ref, w_ref, ra_ref, state, send_sems, recv_sems = refs[5:]

            @pl.when(pl.program_id(1) == 0)
            def _():
                _later_blocks_start(a4_ref, ra_ref, send_sems, recv_sems)

        @pl.when(pl.program_id(1) == 0)
        def _():
            state[...] = jnp.zeros_like(state)

        def chunk(c, carry):
            rows = pl.ds(pl.multiple_of(c * A_CHUNK, A_CHUNK), A_CHUNK)
            hs = range(A_HG)
            cols = [pl.ds(h * 128, 128) for h in hs]
            q = [q_ref[rows, cols[h]] for h in hs]
            k = [k_ref[rows, cols[h]] for h in hs]
            v = [v_ref[rows, cols[h]] for h in hs]
            bcl, gcl, grw = _gate_columns(bg_ref[rows, :], gc_ref[rows, :])
            loc = [_chunk_local(q[h], k[h], gcl[h], grw[h], bcl[h]) for h in hs]
            e, f, gl, kb, amat = ([l[i] for l in loc] for i in (1, 2, 3, 4, 6))
            tinv = _unit_lower_inverse([l[5] for l in loc])
            u = [_doth(tinv[h], v[h] * bcl[h]) for h in hs]
            w = [_doth(tinv[h], kb[h] * e[h]) for h in hs]
            s = [state[h] for h in hs]
            vn = [u[h] - _dotb(w[h], s[h]) for h in hs]
            o_s = [_dotb(q[h] * e[h], s[h]) for h in hs]
            o_a = [_dotb(amat[h], vn[h]) for h in hs]
            s_new = [s[h] * gl[h] + _dotb(k[h] * f[h], vn[h], 0, 0) for h in hs]
            for h in hs:
                s_ref[h, c] = s[h].astype(BF16)
                t_ref[h, c] = tinv[h]
                u_ref[rows, cols[h]] = u[h]
                w_ref[rows, cols[h]] = w[h]
                o_ref[rows, cols[h]] = o_s[h] + o_a[h]
                state[h] = s_new[h]
            return carry

        lax.fori_loop(0, nch, chunk, 0)

        if a4 is not None:
            @pl.when(pl.program_id(1) == ni - 1)
            def _():
                x, y, c, chips = _mesh_pos()
                for j, chip in enumerate(chips):
                    cp = pltpu.make_async_remote_copy(src_ref=a4_ref.at[pl.ds(1, 3), c], dst_ref=ra_ref.at[j],
                                                      send_sem=send_sems.at[j], recv_sem=recv_sems.at[j],
                                                      device_id=(*chip, c), device_id_type=MESH)
                    cp.wait_recv()
                    cp.wait_send()

    hd = lambda col: pl.BlockSpec((tt, wd), lambda h, i: (i, col * ng + h))
    gate_spec = pl.BlockSpec((tt, 128), lambda h, i: (i, 0))
    in_specs = [hd(0), hd(1), hd(2), gate_spec, gate_spec]
    out_specs = [pl.BlockSpec((tt, wd), lambda h, i: (i, h)),
                 pl.BlockSpec((A_HG, nch, 128, 128), lambda h, i: (h, i, 0, 0)),
                 pl.BlockSpec((A_HG, nch, A_CHUNK, A_CHUNK), lambda h, i: (h, i, 0, 0)),
                 pl.BlockSpec((tt, wd), lambda h, i: (i, h)), pl.BlockSpec((tt, wd), lambda h, i: (i, h))]
    out_shape = [jax.ShapeDtypeStruct((T, 1024), F32), jax.ShapeDtypeStruct((A_HEADS, NC, 128, 128), BF16),
                 jax.ShapeDtypeStruct((A_HEADS, NC, A_CHUNK, A_CHUNK), F32),
                 jax.ShapeDtypeStruct((T, 1024), F32), jax.ShapeDtypeStruct((T, 1024), F32)]
    scratch = [pltpu.VMEM((A_HG, 128, 128), F32)]
    args = [qkv, qkv, qkv, bg, gcum]
    if a4 is not None:
        in_specs.append(HBM)
        out_specs.append(HBM)
        out_shape.append(jax.ShapeDtypeStruct((3, 3, A_HALF, FF_BLK), BF16))
        scratch += [pltpu.SemaphoreType.DMA((3,)), pltpu.SemaphoreType.DMA((3,))]
        args.append(a4)
    return pl.pallas_call(body, grid=(ng, ni), in_specs=in_specs, out_specs=out_specs, out_shape=out_shape,
                          scratch_shapes=scratch, name=name, compiler_params=_params(2))(*args)


def _grads_exchange(cs_ref, from_ref, send_sems, recv_sems):
    x, y, c, chips = _mesh_pos()
    return [pltpu.make_async_remote_copy(src_ref=cs_ref.at[2 * chip[0] + chip[1]], dst_ref=from_ref.at[j],
                                         send_sem=send_sems.at[j], recv_sem=recv_sems.at[j],
                                         device_id=(*chip, c), device_id_type=MESH)
            for j, chip in enumerate(chips)]


def _gdn_delta_bwd(name, qkv, bg, gcum, d_o, s_sv, t_sv, u_sv, w_sv, cs_early=None):
    T = qkv.shape[0]
    tt = min(T, 512)
    nch = tt // A_CHUNK
    ni = T // tt

    def body(*refs):
        q_ref, k_ref, v_ref, bg_ref, gc_ref, do_ref, s_ref, t_ref, u_ref, w_ref = refs[:10]
        if cs_early is None:
            dq_ref, dk_ref, dv_ref, dbg_ref, dstate = refs[10:]
        else:
            cs_ref, dq_ref, dk_ref, dv_ref, dbg_ref, from_ref, dstate, send_sems, recv_sems = refs[10:]

            @pl.when(pl.program_id(1) == 0)
            def _():
                for cp in _grads_exchange(cs_ref, from_ref, send_sems, recv_sems):
                    cp.start()

        @pl.when(pl.program_id(1) == 0)
        def _():
            dstate[...] = jnp.zeros_like(dstate)

        incl, strict, _ = _chunk_masks()
        upper = (lax.broadcasted_iota(jnp.int32, (A_CHUNK, A_CHUNK), 0)
                 <= lax.broadcasted_iota(jnp.int32, (A_CHUNK, A_CHUNK), 1)).astype(F32)
        last_row = lax.broadcasted_iota(jnp.int32, (A_CHUNK, 1), 0) == A_CHUNK - 1
        ones = jnp.ones((A_CHUNK, 128), F32)

        rsum = lambda x: jnp.sum(x, axis=1, keepdims=True)

        def chunk(cc, carry):
            c = nch - 1 - cc
            rows = pl.ds(pl.multiple_of(c * A_CHUNK, A_CHUNK), A_CHUNK)
            bcl, gcl, grw = _gate_columns(bg_ref[rows, :], gc_ref[rows, :])
            lane = lax.broadcasted_iota(jnp.int32, (A_CHUNK, 128), 1)
            dbg = jnp.zeros((A_CHUNK, 128), F32)
            for first in range(0, A_HG, 4):
                hs = range(first, first + 4)
                cols = {h: pl.ds(h * 128, 128) for h in hs}
                q = {h: q_ref[rows, cols[h]] for h in hs}
                k = {h: k_ref[rows, cols[h]] for h in hs}
                v = {h: v_ref[rows, cols[h]] for h in hs}
                do = {h: do_ref[rows, cols[h]] for h in hs}
                u = {h: u_ref[rows, cols[h]] for h in hs}
                w = {h: w_ref[rows, cols[h]] for h in hs}
                s = {h: s_ref[h, c] for h in hs}
                tinv = {h: t_ref[h, c] for h in hs}
                ds = {h: dstate[h] for h in hs}
                loc = {h: _chunk_local(q[h], k[h], gcl[h], grw[h], bcl[h]) for h in hs}
                dec, e, f, gl, kb, lmat, amat = ({h: loc[h][i] for h in hs} for i in range(7))
                qd = {h: q[h] * e[h] for h in hs}
                kd = {h: k[h] * f[h] for h in hs}
                ke = {h: kb[h] * e[h] for h in hs}
                vn = {h: u[h] - _dotb(w[h], s[h]) for h in hs}
                d_qd = {h: _dotb(do[h], s[h], 1, 1) for h in hs}
                d_a = {h: jnp.where(incl, _dotb(do[h], vn[h], 1, 1), 0.0) for h in hs}
                d_vn1 = {h: _dotb(amat[h], do[h], 0, 0) for h in hs}
                d_vn = {h: d_vn1[h] + _dotb(kd[h], ds[h]) for h in hs}
                d_kd = {h: _dotb(vn[h], ds[h], 1, 1) for h in hs}
                d_w = {h: -_dotb(d_vn[h], s[h], 1, 1) for h in hs}
                ds_q = {h: _dotb(qd[h], do[h], 0, 0) for h in hs}
                ds_w = {h: _dotb(w[h], d_vn[h], 0, 0) for h in hs}
                d_bv = {h: _doth(tinv[h], d_vn[h], 0, 0) for h in hs}
                d_ke = {h: _doth(tinv[h], d_w[h], 0, 0) for h in hs}
                d_l1 = {h: _dotb(d_bv[h], u[h], 1, 1) for h in hs}
                d_l = {h: -jnp.where(strict, d_l1[h] + _dotb(d_ke[h], w[h], 1, 1), 0.0) for h in hs}
                d_kk = {h: d_l[h] * dec[h] for h in hs}
                d_qk = {h: d_a[h] * dec[h] for h in hs}
                d_kb = {h: _dotb(d_kk[h], k[h]) for h in hs}
                dk1 = {h: _dotb(d_kk[h], kb[h], 0, 0) for h in hs}
                dk2 = {h: _dotb(d_qk[h], q[h], 0, 0) for h in hs}
                dq1 = {h: _dotb(d_qk[h], k[h]) for h in hs}
                m = {h: d_l[h] * lmat[h] + d_a[h] * amat[h] for h in hs}
                col_m = {h: _dotx(m[h], ones, 0, 0)[:, 0:1] for h in hs}
                d_gc = {}
                for h in hs:
                    d_gl = jnp.sum(jnp.sum(ds[h] * s[h].astype(F32), axis=1, keepdims=True), axis=0, keepdims=True)
                    r_kd = rsum(d_kd[h] * kd[h])
                    tail = jnp.sum(r_kd, axis=0, keepdims=True) + d_gl * gl[h]
                    d_gc[h] = (rsum(m[h]) - col_m[h] + rsum(d_qd[h] * qd[h]) - r_kd + rsum(d_ke[h] * ke[h])
                               + jnp.where(last_row, tail, 0.0))
                dg = {h: _dotx(upper, d_gc[h] * ones)[:, 0:1] for h in hs}
                for h in hs:
                    dstate[h] = gl[h] * ds[h] + ds_q[h] - ds_w[h]
                    dk_ref[rows, cols[h]] = (dk1[h] + dk2[h] + d_kd[h] * f[h] + d_ke[h] * (bcl[h] * e[h])
                                             + d_kb[h] * bcl[h])
                    dq_ref[rows, cols[h]] = dq1[h] + d_qd[h] * e[h]
                    dv_ref[rows, cols[h]] = d_bv[h] * bcl[h]
                    d_beta = rsum(d_ke[h] * k[h]) * e[h] + rsum(d_kb[h] * k[h]) + rsum(d_bv[h] * v[h])
                    dbg = jnp.where(lane == h, d_beta, jnp.where(lane == 8 + h, dg[h], dbg))
            dbg_ref[rows, :] = dbg
            return carry

        lax.fori_loop(0, nch, chunk, 0)

        if cs_early is not None:
            @pl.when(pl.program_id(1) == ni - 1)
            def _():
                for cp in _grads_exchange(cs_ref, from_ref, send_sems, recv_sems):
                    cp.wait_recv()
                    cp.wait_send()

    wd, ng = 128 * A_HG, A_HEADS // A_HG
    assert ng == 1
    rev = lambda i: ni - 1 - i
    hd = lambda col: pl.BlockSpec((tt, wd), lambda h, i: (rev(i), col * ng + h))
    hd1 = pl.BlockSpec((tt, wd), lambda h, i: (rev(i), h))
    gate_spec = pl.BlockSpec((tt, 128), lambda h, i: (rev(i), 0))
    in_specs = [hd(0), hd(1), hd(2), gate_spec, gate_spec, hd1,
                pl.BlockSpec((A_HG, nch, 128, 128), lambda h, i: (h, rev(i), 0, 0)),
                pl.BlockSpec((A_HG, nch, A_CHUNK, A_CHUNK), lambda h, i: (h, rev(i), 0, 0)), hd1, hd1]
    out_specs = [hd1, hd1, hd1, gate_spec]
    out_shape = [jax.ShapeDtypeStruct((T, 1024), F32)] * 3 + [jax.ShapeDtypeStruct((T, 128), F32)]
    scratch = [pltpu.VMEM((A_HG, 128, 128), F32)]
    args = [qkv, qkv, qkv, bg, gcum, d_o, s_sv, t_sv, u_sv, w_sv]
    if cs_early is not None:
        in_specs.append(HBM)
        out_specs.append(HBM)
        out_shape.append(jax.ShapeDtypeStruct((3,) + cs_early.shape[1:], BF16))
        scratch += [pltpu.SemaphoreType.DMA((3,)), pltpu.SemaphoreType.DMA((3,))]
        args.append(cs_early)
    return pl.pallas_call(body, grid=(ng, ni), in_specs=in_specs, out_specs=out_specs, out_shape=out_shape,
                          scratch_shapes=scratch, name=name, compiler_params=_params(2))(*args)


def _gdn_gate_fwd(name, o, proj, nw):
    T = o.shape[0]
    tt = min(T, 512)

    def body(o_ref, z_ref, nw_ref, y_ref):
        for h in range(A_HEADS):
            cols = pl.ds(h * 128, 128)
            ov, z = o_ref[:, cols], z_ref[:, cols]
            r = lax.rsqrt(jnp.mean(ov * ov, axis=-1, keepdims=True) + EPS)
            y_ref[:, cols] = (ov * r * nw_ref[...] * (z * _sigmoid(z))).astype(BF16)

    return pl.pallas_call(
        body, grid=(T // tt,),
        in_specs=[pl.BlockSpec((tt, 1024), lambda i: (i, 0)), pl.BlockSpec((tt, 1024), lambda i: (i, 3)),
                  pl.BlockSpec((1, 128), lambda i: (0, 0))],
        out_specs=pl.BlockSpec((tt, 1024), lambda i: (i, 0)),
        out_shape=jax.ShapeDtypeStruct((T, 1024), BF16), name=name, compiler_params=_params(1))(o, proj, nw)


def _gdn_gate_bwd(name, dy2, o, proj, nw):
    T = o.shape[0]
    tt = min(T, 512)

    def body(dy_ref, o_ref, z_ref, nw_ref, do_ref, dz_ref, dnw_ref):
        dnw = jnp.zeros((1, 128), F32)
        for h in range(A_HEADS):
            cols = pl.ds(h * 128, 128)
            dy, ov, z = dy_ref[:, cols], o_ref[:, cols], z_ref[:, cols]
            s = _sigmoid(z)
            sz = z * s
            r = lax.rsqrt(jnp.mean(ov * ov, axis=-1, keepdims=True) + EPS)
            xhat = ov * r
            dn = dy * sz
            dz_ref[:, cols] = dy * (xhat * nw_ref[...]) * (s + z * s * (1.0 - s))
            dxhat = dn * nw_ref[...]
            do_ref[:, cols] = r * (dxhat - xhat * jnp.mean(dxhat * xhat, axis=-1, keepdims=True))
            dnw = dnw + jnp.sum(dn * xhat, axis=0, keepdims=True)

        @pl.when(pl.program_id(0) == 0)
        def _():
            dnw_ref[...] = dnw

        @pl.when(pl.program_id(0) > 0)
        def _():
            dnw_ref[...] += dnw

    blk = lambda c: pl.BlockSpec((tt, 1024), lambda i: (i, c))
    return pl.pallas_call(
        body, grid=(T // tt,),
        in_specs=[blk(0), blk(0), blk(3), pl.BlockSpec((1, 128), lambda i: (0, 0))],
        out_specs=[blk(0), blk(0), pl.BlockSpec((1, 128), lambda i: (0, 0))],
        out_shape=[jax.ShapeDtypeStruct((T, 1024), F32), jax.ShapeDtypeStruct((T, 1024), F32),
                   jax.ShapeDtypeStruct((1, 128), F32)],
        name=name, compiler_params=_params(1))(dy2, o, proj, nw)


def _gdn_prep_bwd1(name, proj, wconv, gate_p, dq, dk, dv, dbg):
    T = proj.shape[0]
    tt = min(T, 256)
    hb = tt // 8

    def body(cur_ref, halo_ref, ba_ref, w_ref, gp_ref, dq_ref, dk_ref, dv_ref, dbg_ref,
             dc_ref, dw_ref, dba_ref, dgp_ref):
        first = pl.program_id(0) == 0
        rid = lax.broadcasted_iota(jnp.int32, (8, 128), 0)
        for c in range(24):
            cols = pl.ds(c * 128, 128)
            cur = cur_ref[:, cols]
            halo = jnp.where(first, 0.0, halo_ref[:, cols])
            conv = _conv_taps(cur, halo_ref[:, cols], w_ref[:, cols], first, -1)
            s = _sigmoid(conv)
            y = conv * s
            if c < 16:
                dref = dq_ref if c < 8 else dk_ref
                dn = dref[:, pl.ds((c % 8) * 128, 128)]
                rinv = lax.rsqrt(jnp.sum(y * y, axis=-1, keepdims=True) + EPS)
                yhat = y * rinv
                dyv = rinv * (dn - yhat * jnp.sum(dn * yhat, axis=-1, keepdims=True))
                if c < 8:
                    dyv = dyv * (A_DK ** -0.5)
            else:
                dyv = dv_ref[:, pl.ds((c - 16) * 128, 128)]
            dc = dyv * (s + conv * s * (1.0 - s))
            dc_ref[:, cols] = dc
            parts = [jnp.sum(dc * cur, axis=0, keepdims=True)]
            for sft in (1, 2, 3):
                sh = pltpu.roll(cur, sft, 0)
                edge = jnp.where(rid < sft, pltpu.roll(halo, sft, 0), sh[0:8])
                sh = jnp.concatenate([edge, sh[8:]], axis=0) if tt > 8 else edge
                parts.append(jnp.sum(dc * sh, axis=0, keepdims=True))
            dwc = jnp.concatenate(parts[::-1], axis=0)

            @pl.when(first)
            def _():
                dw_ref[:, cols] = dwc

            @pl.when(jnp.logical_not(first))
            def _():
                dw_ref[:, cols] += dwc

        ba = ba_ref[...]
        dbg = dbg_ref[...]
        lane = lax.broadcasted_iota(jnp.int32, ba.shape, 1)
        sb = _sigmoid(ba)
        zarg = ba + gp_ref[1:2, :]
        softplus = jnp.maximum(zarg, 0.0) + jnp.log(1.0 + jnp.exp(-jnp.abs(zarg)))
        d_b = dbg * sb * (1.0 - sb)
        d_a = dbg * gp_ref[0:1, :] * _sigmoid(zarg)
        dba_ref[...] = jnp.where(lane < 8, d_b, jnp.where(lane < 16, d_a, 0.0))
        g = gp_ref[0:1, :] * softplus
        in_a = (lane >= 8) & (lane < 16)
        sums = jnp.concatenate([jnp.sum(jnp.where(in_a, dbg * g, 0.0), axis=0, keepdims=True),
                                jnp.sum(jnp.where(in_a, d_a, 0.0), axis=0, keepdims=True)], axis=0)

        @pl.when(first)
        def _():
            dgp_ref[...] = sums

        @pl.when(jnp.logical_not(first))
        def _():
            dgp_ref[...] += sums

    row = lambda w, c=0: pl.BlockSpec((tt, w), lambda i: (i, c))
    return pl.pallas_call(
        body, grid=(T // tt,),
        in_specs=[row(3072), pl.BlockSpec((8, 3072), lambda i: (jnp.maximum(i * hb - 1, 0), 0)), row(128, 32),
                  pl.BlockSpec((4, 3072), lambda i: (0, 0)), pl.BlockSpec((2, 128), lambda i: (0, 0)),
                  row(1024), row(1024), row(1024), row(128)],
        out_specs=[row(3072), pl.BlockSpec((4, 3072), lambda i: (0, 0)), row(128),
                   pl.BlockSpec((2, 128), lambda i: (0, 0))],
        out_shape=[jax.ShapeDtypeStruct((T, 3072), F32), jax.ShapeDtypeStruct((4, 3072), F32),
                   jax.ShapeDtypeStruct((T, 128), F32), jax.ShapeDtypeStruct((2, 128), F32)],
        name=name, compiler_params=_params(1))(proj, proj, proj, wconv, gate_p, dq, dk, dv, dbg)


def _gdn_prep_bwd2(name, dc, wconv, dz, dba):
    T = dc.shape[0]
    tt = min(T, 256)
    hb = tt // 8
    ni = T // tt

    def body(cur_ref, halo_ref, w_ref, dz_ref, dba_ref, o_ref):
        last = pl.program_id(0) == ni - 1
        for c in range(24):
            cols = pl.ds(c * 128, 128)
            o_ref[:, cols] = _conv_taps(cur_ref[:, cols], halo_ref[:, cols], w_ref[:, cols], last, +1).astype(BF16)
        o_ref[:, pl.ds(3072, 1024)] = dz_ref[...].astype(BF16)
        o_ref[:, pl.ds(4096, 128)] = dba_ref[...].astype(BF16)

    return pl.pallas_call(
        body, grid=(ni,),
        in_specs=[pl.BlockSpec((tt, 3072), lambda i: (i, 0)),
                  pl.BlockSpec((8, 3072), lambda i: (jnp.minimum((i + 1) * hb, T // 8 - 1), 0)),
                  pl.BlockSpec((4, 3072), lambda i: (0, 0)),
                  pl.BlockSpec((tt, 1024), lambda i: (i, 0)), pl.BlockSpec((tt, 128), lambda i: (i, 0))],
        out_specs=pl.BlockSpec((tt, A_IN_PAD), lambda i: (i, 0)),
        out_shape=jax.ShapeDtypeStruct((T, A_IN_PAD), BF16), name=name, compiler_params=_params(1))(
            dc, dc, wconv, dz, dba)


def _gdn_fwd(x, nw, w_in, wconv, gate_p, out_nw, w_out, a4=None):
    h = _rms_fwd("a_rms", x, nw)
    proj = _mm_plain("a_proj", h, w_in, 1, 0, F32, tn=FF_BLK)
    qkv, bg, gcum = _gdn_prep("a_prep", proj, wconv, gate_p)
    o, s_sv, t_sv, u_sv, w_sv, *arrived = _gdn_delta_fwd("a_delta", qkv, bg, gcum, a4)
    o2 = _gdn_gate_fwd("a_gate", o, proj, out_nw)
    y = _mm_residual("a_out", o2, w_out, x, 1.0)
    return y, (h, proj, qkv, bg, gcum, o, s_sv, t_sv, u_sv, w_sv, o2), (arrived[0] if arrived else None)


def _gdn_bwd(dy, dyb, x, nw, w_in, wconv, gate_p, out_nw, w_out, saved, out_scale, cs_early=None):
    h, proj, qkv, bg, gcum, o, s_sv, t_sv, u_sv, w_sv, o2 = saved
    d_o2 = _mm_plain("a_dout", dyb, w_out, 1, 1, F32)
    d_wout = _mm_plain("a_dwout", o2, dyb, 0, 0, F32)
    d_o, d_z, d_outnw = _gdn_gate_bwd("a_dgate", d_o2, o, proj, out_nw)
    dq, dk, dv, dbg, *arrived = _gdn_delta_bwd("a_ddelta", qkv, bg, gcum, d_o, s_sv, t_sv, u_sv, w_sv, cs_early)
    dc, d_wconv, dba, dgp = _gdn_prep_bwd1("a_dprep1", proj, wconv, gate_p, dq, dk, dv, dbg)
    dproj = _gdn_prep_bwd2("a_dprep2", dc, wconv, d_z, dba)
    d_win = _mm_plain("a_dwin", h, dproj, 0, 0, F32, tn=FF_BLK, tk=2048)
    dx, dxb, d_nw = _mm_rms_bwd("a_dx", dproj, w_in, x, dy, nw, A_IN_PAD, out_scale)
    return dx, dxb, d_nw, d_win, d_wconv, dgp, d_outnw, d_wout, (arrived[0] if arrived else None)


def _swa_masks(n):
    qi = lax.broadcasted_iota(jnp.int32, (B_BLK, B_BLK), 0)
    kj = lax.broadcasted_iota(jnp.int32, (B_BLK, B_BLK), 1)
    return kj > qi + jnp.where(n > 0, 0, B_BLK), kj <= qi


def _swa_fwd(name, q, k, v, sinks):
    T = q.shape[1]
    tq = min(T, 1024)
    nbt = tq // B_BLK
    scale = B_HD ** -0.5
    G = B_HEADS // B_KV

    def body(q_ref, k_ref, v_ref, kh_ref, vh_ref, s_ref, o_ref, l_ref):
        first_blk = pl.program_id(1) * nbt

        def block(n, kp, vp):
            m_prev, m_cur = _swa_masks(first_blk + n)
            cur = pl.ds(pl.multiple_of(n * B_BLK, B_BLK), B_BLK)
            kc, vc = k_ref[0, cur, :], v_ref[0, cur, :]
            gs = range(G)
            rmax = lambda a: jnp.max(a, axis=1, keepdims=True)
            rsum = lambda a: jnp.sum(a, axis=1, keepdims=True)
            sink = [s_ref[g][:, 0:1] for g in gs]
            qb = [q_ref[g, cur, :] for g in gs]
            s_p = [jnp.where(m_prev, _dot(qb[g], kp, 1, 1) * scale, -jnp.inf) for g in gs]
            s_c = [jnp.where(m_cur, _dot(qb[g], kc, 1, 1) * scale, -jnp.inf) for g in gs]
            m = [jnp.maximum(jnp.maximum(rmax(s_p[g]), rmax(s_c[g])), sink[g]) for g in gs]
            p_p = [jnp.exp(s_p[g] - m[g]) for g in gs]
            p_c = [jnp.exp(s_c[g] - m[g]) for g in gs]
            den = [rsum(p_p[g]) + rsum(p_c[g]) + jnp.exp(sink[g] - m[g]) for g in gs]
            a_p = [_dotb(p_p[g], vp) for g in gs]
            a_c = [_dotb(p_c[g], vc) for g in gs]
            for g in gs:
                o_ref[g, cur, :] = ((a_p[g] + a_c[g]) / den[g]).astype(BF16)
                l_ref[g, cur, :] = m[g] + jnp.log(den[g])

        block(0, kh_ref[0], vh_ref[0])

        def rest(n, carry):
            prv = pl.ds(pl.multiple_of((n - 1) * B_BLK, B_BLK), B_BLK)
            block(n, k_ref[0, prv, :], v_ref[0, prv, :])
            return carry

        lax.fori_loop(1, nbt, rest, 0)

    qs = pl.BlockSpec((G, tq, B_HD), lambda kv, i: (kv, i, 0))
    ks = pl.BlockSpec((1, tq, B_HD), lambda kv, i: (kv, i, 0))
    halo = pl.BlockSpec((1, B_BLK, B_HD), lambda kv, i: (kv, jnp.maximum(i * nbt - 1, 0), 0))
    return pl.pallas_call(
        body, grid=(B_KV, T // tq),
        in_specs=[qs, ks, ks, halo, halo, pl.BlockSpec((G, 1, 128), lambda kv, i: (kv, 0, 0))],
        out_specs=[qs, pl.BlockSpec((G, tq, 1), lambda kv, i: (kv, i, 0))],
        out_shape=[jax.ShapeDtypeStruct((B_HEADS, T, B_HD), BF16), jax.ShapeDtypeStruct((B_HEADS, T, 1), F32)],
        name=name, compiler_params=_params(2))(q, k, v, k, v, sinks)


def _swa_bwd(name, q, k, v, sinks, o, lse, do):
    T = q.shape[1]
    tq = min(T, 1024)
    nbt, ni = tq // B_BLK, T // tq
    scale = B_HD ** -0.5
    G = B_HEADS // B_KV

    def body(q_ref, k_ref, v_ref, kh_ref, vh_ref, s_ref, o_ref, l_ref, do_ref, dq_ref, dk_ref, dv_ref, ds_ref,
             dk_halo, dv_halo):
        step = pl.program_id(1)
        first_blk = (ni - 1 - step) * nbt
        last = pl.ds(tq - B_BLK, B_BLK)
        dk_ref[...] = jnp.zeros_like(dk_ref)
        dv_ref[...] = jnp.zeros_like(dv_ref)

        @pl.when(step > 0)
        def _():
            dk_ref[0, last, :] = dk_halo[...]
            dv_ref[0, last, :] = dv_halo[...]

        def block(n, kp, vp, dsinks):
            m_prev, m_cur = _swa_masks(first_blk + n)
            cur = pl.ds(pl.multiple_of(n * B_BLK, B_BLK), B_BLK)
            kc, vc = k_ref[0, cur, :], v_ref[0, cur, :]
            gs = range(G)
            sink = [s_ref[g][:, 0:1] for g in gs]
            qb = [q_ref[g, cur, :] for g in gs]
            dob = [do_ref[g, cur, :] for g in gs]
            lse_b = [l_ref[g, cur, :] for g in gs]
            p_p = [jnp.where(m_prev, jnp.exp(_dot(qb[g], kp, 1, 1) * scale - lse_b[g]), 0.0) for g in gs]
            p_c = [jnp.where(m_cur, jnp.exp(_dot(qb[g], kc, 1, 1) * scale - lse_b[g]), 0.0) for g in gs]
            delta = [jnp.sum(dob[g].astype(F32) * o_ref[g, cur, :].astype(F32), axis=1, keepdims=True) for g in gs]
            ds_p = [p_p[g] * (_dot(dob[g], vp, 1, 1) - delta[g]) for g in gs]
            ds_c = [p_c[g] * (_dot(dob[g], vc, 1, 1) - delta[g]) for g in gs]
            dq_p = [_dotb(ds_p[g], kp) for g in gs]
            dq_c = [_dotb(ds_c[g], kc) for g in gs]
            dk_ps = [_dotb(ds_p[g], qb[g], 0, 0) for g in gs]
            dk_cs = [_dotb(ds_c[g], qb[g], 0, 0) for g in gs]
            dv_ps = [_dotb(p_p[g], dob[g], 0, 0) for g in gs]
            dv_cs = [_dotb(p_c[g], dob[g], 0, 0) for g in gs]
            for g in gs:
                dq_ref[g, cur, :] = (dq_p[g] + dq_c[g]) * scale
            out = tuple(dsinks[g] - jnp.sum(jnp.exp(sink[g] - lse_b[g]) * delta[g], axis=0, keepdims=True) for g in gs)
            total = lambda parts: (parts[0] + parts[1]) + (parts[2] + parts[3])
            dk_ref[0, cur, :] += total(dk_cs) * scale
            dv_ref[0, cur, :] += total(dv_cs)
            return total(dk_ps) * scale, total(dv_ps), out

        zeros = tuple(jnp.zeros((1, 1), F32) for _ in range(G))
        dk_p, dv_p, dsinks = block(0, kh_ref[0], vh_ref[0], zeros)
        dk_halo[...] = dk_p
        dv_halo[...] = dv_p

        def rest(n, dsinks):
            prv = pl.ds(pl.multiple_of((n - 1) * B_BLK, B_BLK), B_BLK)
            dk_p, dv_p, dsinks = block(n, k_ref[0, prv, :], v_ref[0, prv, :], dsinks)
            dk_ref[0, prv, :] += dk_p
            dv_ref[0, prv, :] += dv_p
            return dsinks

        dsinks = lax.fori_loop(1, nbt, rest, dsinks)
        for g in range(G):
            row = jnp.broadcast_to(dsinks[g], (1, 128))

            @pl.when(step == 0)
            def _():
                ds_ref[g] = row

            @pl.when(step > 0)
            def _():
                ds_ref[g] += row

    rev = lambda i: ni - 1 - i
    qs = pl.BlockSpec((G, tq, B_HD), lambda kv, i: (kv, rev(i), 0))
    ks = pl.BlockSpec((1, tq, B_HD), lambda kv, i: (kv, rev(i), 0))
    halo = pl.BlockSpec((1, B_BLK, B_HD), lambda kv, i: (kv, jnp.maximum(rev(i) * nbt - 1, 0), 0))
    ss = pl.BlockSpec((G, 1, 128), lambda kv, i: (kv, 0, 0))
    return pl.pallas_call(
        body, grid=(B_KV, ni),
        in_specs=[qs, ks, ks, halo, halo, ss, qs, pl.BlockSpec((G, tq, 1), lambda kv, i: (kv, rev(i), 0)), qs],
        out_specs=[qs, ks, ks, ss],
        out_shape=[jax.ShapeDtypeStruct((B_HEADS, T, B_HD), F32), jax.ShapeDtypeStruct((B_KV, T, B_HD), F32),
                   jax.ShapeDtypeStruct((B_KV, T, B_HD), F32), jax.ShapeDtypeStruct((B_HEADS, 1, 128), F32)],
        scratch_shapes=[pltpu.VMEM((B_BLK, B_HD), F32), pltpu.VMEM((B_BLK, B_HD), F32)],
        name=name, compiler_params=_params(2))(q, k, v, k, v, sinks, o, lse, do)


def _split_heads(a, n):
    T = a.shape[0]
    return a.reshape(T, n, B_HD).transpose(1, 0, 2)


def _merge_heads(a):
    n, T, _ = a.shape
    return a.transpose(1, 0, 2).reshape(T, n * B_HD)


def _swa_mixer_fwd(x, nw, w_in, b_in, sinks, w_out, b_out):
    h = _rms_fwd("b_rms", x, nw)
    proj = _mm_plain("b_proj", h, w_in, 1, 0, BF16, tn=768, bias=b_in)
    q, k, v = _split_heads(proj[:, :1024], B_HEADS), _split_heads(proj[:, 1024:1280], B_KV), _split_heads(proj[:, 1280:], B_KV)
    o, lse = _swa_fwd("b_attn", q, k, v, sinks)
    om = _merge_heads(o)
    y = _mm_residual("b_out", om, w_out, x, 1.0, bias=b_out)
    return y, (h, q, k, v, o, lse, om)


def _swa_mixer_bwd(dy, dyb, x, nw, w_in, sinks, w_out, saved, out_scale):
    h, q, k, v, o, lse, om = saved
    d_om = _mm_plain("b_dout", dyb, w_out, 1, 1, BF16)
    d_wout = _mm_plain("b_dwout", om, dyb, 0, 0, F32)
    d_bout = _colsum("b_dbout", dy)
    dq, dk, dv, dsinks = _swa_bwd("b_dattn", q, k, v, sinks, o, lse, _split_heads(d_om, B_HEADS))
    dproj = jnp.concatenate([_merge_heads(dq), _merge_heads(dk), _merge_heads(dv)], axis=1)
    d_bin = _colsum("b_dbin", dproj)
    d_win = _mm_plain("b_dwin", h, dproj, 0, 0, F32, tn=768)
    dx, dxb, d_nw = _mm_rms_bwd("b_dx", dproj, w_in, x, dy, nw, 1536, out_scale)
    return dx, dxb, d_nw, d_win, d_bin, dsinks[:, 0, 0], d_wout, d_bout


def _loss_head(name, x, tgt, fw, out_scale):
    T, D = x.shape
    tt = min(T, 512)

    def body(x_ref, t_ref, w_ref, dx_ref, dxb_ref, loss_ref, dw_ref):
        xv = x_ref[...]
        r = lax.rsqrt(jnp.mean(xv * xv, axis=-1, keepdims=True) + EPS)
        xhat = xv * r
        diff = xhat * w_ref[...] - t_ref[...]
        part = 0.5 * jnp.sum(jnp.mean(diff * diff, axis=-1, keepdims=True), axis=0, keepdims=True)
        dyv = diff * (1.0 / D)
        dxhat = dyv * w_ref[...]
        dx = r * (dxhat - xhat * jnp.mean(dxhat * xhat, axis=-1, keepdims=True))
        dx_ref[...] = dx
        dxb_ref[...] = (dx * out_scale).astype(BF16)
        dw = jnp.sum(dyv * xhat, axis=0, keepdims=True)
        lp = jnp.broadcast_to(part, (1, 128))

        @pl.when(pl.program_id(0) == 0)
        def _():
            loss_ref[...] = lp
            dw_ref[...] = dw

        @pl.when(pl.program_id(0) > 0)
        def _():
            loss_ref[...] += lp
            dw_ref[...] += dw

    row = pl.BlockSpec((tt, D), lambda i: (i, 0))
    return pl.pallas_call(
        body, grid=(T // tt,), in_specs=[row, row, pl.BlockSpec((1, D), lambda i: (0, 0))],
        out_specs=[row, row, pl.BlockSpec((1, 128), lambda i: (0, 0)), pl.BlockSpec((1, D), lambda i: (0, 0))],
        out_shape=[jax.ShapeDtypeStruct((T, D), F32), jax.ShapeDtypeStruct((T, D), BF16),
                   jax.ShapeDtypeStruct((1, 128), F32), jax.ShapeDtypeStruct((1, D), F32)],
        name=name, compiler_params=_params(1))(x, tgt, fw)


def _local_step(x, tgt, wts, comm=None):
    W = dict(wts)
    g = {}
    ga = W["ga"]
    a4 = comm["a4"] if comm else None
    n1, n2, nm = W["ffn1_norm"], W["ffn2_norm"], W["mix_norm"]
    h10, gu10, act10, *arrived = _ffn_gu("f10_gu", x, n1[0:1], ga, 0, (comm["big_b"], comm["small"]) if comm else None)
    if comm:
        W.update(comm["finish"](*arrived))
    wdn = W["w_down"]
    x1, sv1 = _ffn_down("f10_down", act10, wdn[0], x), (h10, gu10, act10)
    x2, sva, arrived = _gdn_fwd(x1, nm[0:1], W["a_w_in"], W["a_w_conv"], W["a_gate_p"], W["a_out_norm"], W["a_w_out"], a4)
    if a4 is not None:
        ga = _fill_a(1, a4, arrived, ga.reshape(4, 4, 2, A_HALF, FF_BLK)).reshape(ga.shape)
    x3, sv3 = _ffn_fwd("f20", x2, n2[0:1], ga, 2, wdn[2])
    x4, sv4 = _ffn_fwd("f11", x3, n1[1:2], ga, 1, wdn[1])
    x5, svb = _swa_mixer_fwd(x4, nm[1:2], W["b_w_in"], W["b_b_in"], W["b_sinks"], W["b_w_out"], W["b_b_out"])
    x6, sv6 = _ffn_fwd("f21", x5, n2[1:2], ga, 3, wdn[3])
    dx, dxb, loss_p, g["final_norm"] = _loss_head("loss_head", x6, tgt, W["final_norm"], 0.5)

    pa = jnp.zeros(ga.shape, BF16)
    dx, dxb, n21, pa, wd21 = _ffn_bwd("f21", dx, dxb, x5, n2[1:2], ga, 3, wdn[3], sv6, pa, 1.0)
    dx, dxb, nb, g["b_w_in"], g["b_b_in"], g["b_sinks"], g["b_w_out"], g["b_b_out"] = _swa_mixer_bwd(
        dx, dxb, x4, nm[1:2], W["b_w_in"], W["b_sinks"], W["b_w_out"], svb, 0.5)
    dx, dxb, n11, pa, wd11 = _ffn_bwd("f11", dx, dxb, x3, n1[1:2], ga, 1, wdn[1], sv4, pa, 0.5)
    dx, dxb, n20, pa, wd20 = _ffn_bwd("f20", dx, dxb, x2, n2[0:1], ga, 2, wdn[2], sv3, pa, 1.0)
    g["cs_early"] = _pair_sums_a("1", pa.reshape(4, 4, 2, A_HALF, FF_BLK), 1, 3) if a4 is not None else None
    dx, dxb, na, g["a_w_in"], g["a_w_conv"], g["a_gate_p"], g["a_out_norm"], g["a_w_out"], g["from_early"] = _gdn_bwd(
        dx, dxb, x1, nm[0:1], W["a_w_in"], W["a_w_conv"], W["a_gate_p"], W["a_out_norm"], W["a_w_out"], sva, 0.5,
        g["cs_early"])
    dgu10 = _ffn_dact("f10_dact", dxb, wdn[0], gu10)
    wd10 = _ffn_dwd("f10_dwd", act10, dxb)
    g["w_down"] = jnp.stack([wd10, wd11, wd20, wd21])
    if comm:
        p_b = comm["pack_b"](g)
        cs_b = _pair_sum("b", p_b, _pair_send("b", p_b))
        pa, from_b = _ffn_dwgu("f10_dwgu", h10, dgu10, pa, 0, cs_b)
        cs_late = _pair_sums_a("0", pa.reshape(4, 4, 2, A_HALF, FF_BLK), 0, 1)
        dx, dxb, n10, from_late = _ffn_dx("f10_dx", dgu10, ga, 0, x, dx, n1[0:1], 1.0, (cs_late,))
        g["late"] = ((cs_late, cs_b), (from_late, from_b))
    else:
        pa = _ffn_dwgu("f10_dwgu", h10, dgu10, pa, 0)
        dx, dxb, n10 = _ffn_dx("f10_dx", dgu10, ga, 0, x, dx, n1[0:1], 1.0)
        g["late"] = None
    g["ga"] = pa

    g["ffn1_norm"] = jnp.concatenate([n10, n11], axis=0)
    g["ffn2_norm"] = jnp.concatenate([n20, n21], axis=0)
    g["mix_norm"] = jnp.concatenate([na, nb], axis=0)
    return loss_p, dx, g


A_ROWS = 4 * D_MODEL
PACK = (("ffn1_w_down", 1408), ("ffn2_w_down", 1408), ("a_w_in", 1028), ("a_w_out", 256), ("b_w_in", 384),
        ("b_w_out", 256))
PACK_TILE = 16
PACK_USED = sum(-(-n // PACK_TILE) * PACK_TILE for _, n in PACK)
PACK_ROWS = 4864
assert PACK_USED <= PACK_ROWS
SMALL_SHARD = (8, 512)
MOVE_ROWS = {"a": 512, "b": 608}
SUM_ROWS = {"a": 256, "b": 304}


def _mesh_pos():
    x, y, c = lax.axis_index("x"), lax.axis_index("y"), lax.axis_index("c")
    return x, y, c, [(1 - x, y), (x, 1 - y), (1 - x, 1 - y)]


def _half(rows, c):
    return pl.ds(pl.multiple_of(c * (rows // 2), 16), rows // 2)


A_HALF = D_MODEL // 2


def _src_chip(j):
    x, y = lax.axis_index("x"), lax.axis_index("y")
    return jnp.where(j == 0, 2 * (1 - x) + y, jnp.where(j == 1, 2 * x + 1 - y, 2 * (1 - x) + 1 - y))


def _later_blocks_start(a4_ref, ra_ref, send_sems, recv_sems):
    x, y, c, chips = _mesh_pos()
    copies = [pltpu.make_async_remote_copy(src_ref=a4_ref.at[pl.ds(1, 3), c], dst_ref=ra_ref.at[j],
                                           send_sem=send_sems.at[j], recv_sem=recv_sems.at[j],
                                           device_id=(*chip, c), device_id_type=MESH)
              for j, chip in enumerate(chips)]
    for cp in copies:
        cp.start()
    return copies


def _fill_a(phase, a4, ra, ga=None):
    nb = 1 if phase == 0 else 3
    first = 0 if phase == 0 else 1
    steps = 3 * nb
    own_tiles = 2 * nb
    ra = ra.reshape(3, nb, A_HALF, FF_BLK)

    def body(*refs):
        if phase == 0:
            r_ref, own_ref, g_ref, send_sem, recv_sem, local_sems = refs
        else:
            r_ref, own_ref, _, g_ref, send_sem, recv_sem, local_sems = refs
        x, y, c, _ = _mesh_pos()
        s = pl.program_id(0)
        j, b = s // nb, s % nb
        dst = g_ref.at[_src_chip(j), first + b, c]
        keep = pltpu.make_async_copy(r_ref.at[0, 0], dst, local_sems.at[0])
        give = pltpu.make_async_remote_copy(src_ref=r_ref.at[0, 0], dst_ref=dst, send_sem=send_sem, recv_sem=recv_sem,
                                            device_id=(x, y, 1 - c), device_id_type=MESH)
        keep.start()
        give.start()

        @pl.when(s < own_tiles)
        def _():
            own = pltpu.make_async_copy(own_ref.at[0, 0], g_ref.at[2 * x + y, first + s // 2, s % 2], local_sems.at[1])
            own.start()
            own.wait()

        give.wait_send()
        keep.wait()

        @pl.when(s == steps - 1)
        def _():
            landed = g_ref.at[pl.ds(0, 3), pl.ds(0, nb), 0]
            pltpu.make_async_remote_copy(src_ref=landed, dst_ref=landed, send_sem=send_sem, recv_sem=recv_sem,
                                         device_id=(x, y, c), device_id_type=MESH).wait_recv()

    tile = (1, 1, A_HALF, FF_BLK)
    in_specs = [pl.BlockSpec(tile, lambda s: (s // nb, s % nb, 0, 0)),
                pl.BlockSpec(tile, lambda s: (first + jnp.minimum(s, own_tiles - 1) // 2, jnp.minimum(s, own_tiles - 1) % 2, 0, 0))]
    args = [ra, a4]
    if phase == 1:
        in_specs.append(HBM)
        args.append(ga)
    return pl.pallas_call(
        body, grid=(steps,), in_specs=in_specs, out_specs=HBM,
        out_shape=jax.ShapeDtypeStruct((4, 4, 2, A_HALF, FF_BLK), BF16),
        scratch_shapes=[pltpu.SemaphoreType.DMA, pltpu.SemaphoreType.DMA, pltpu.SemaphoreType.DMA((2,))],
        input_output_aliases={2: 0} if phase == 1 else {},
        name="fill_a%d" % phase, compiler_params=_params(1))(*args)


def _gather_chips(big_a4):
    def body(a_ref, ra_ref, send_sems, recv_sems):
        c = lax.axis_index("c")
        send = _peer_copies([(lambda chip: a_ref.at[0, c], ra_ref)], send_sems, recv_sems)
        for cp in send:
            cp.start()
        for cp in send:
            cp.wait_recv()
        for cp in send:
            cp.wait_send()

    return pl.pallas_call(
        body, name="gather_chips", in_specs=[HBM], out_specs=HBM,
        out_shape=jax.ShapeDtypeStruct((3, A_HALF, FF_BLK), BF16),
        scratch_shapes=[pltpu.SemaphoreType.DMA((3,)), pltpu.SemaphoreType.DMA((3,))])(big_a4)


def _gather_fill(tag, big, recv):
    rows_all, width = big.shape
    half, mv = rows_all // 2, MOVE_ROWS[tag]
    nt = half // mv
    own_tiles = rows_all // mv
    assert half % mv == 0 and own_tiles <= 3 * nt

    def body(recv_ref, big_ref, g_ref, send_sem, recv_sem, local_sems):
        x, y, c, chips = _mesh_pos()
        j, t = pl.program_id(0), pl.program_id(1)
        step = j * nt + t
        src_chip = jnp.where(j == 0, 2 * (1 - x) + y, jnp.where(j == 1, 2 * x + 1 - y, 2 * (1 - x) + 1 - y))
        rows = pl.ds(pl.multiple_of(c * half + t * mv, 16), mv)
        keep = pltpu.make_async_copy(recv_ref.at[0], g_ref.at[src_chip, rows], local_sems.at[0])
        give = pltpu.make_async_remote_copy(src_ref=recv_ref.at[0], dst_ref=g_ref.at[src_chip, rows],
                                            send_sem=send_sem, recv_sem=recv_sem,
                                            device_id=(x, y, 1 - c), device_id_type=MESH)
        keep.start()
        give.start()

        @pl.when(step < own_tiles)
        def _():
            own_rows = pl.ds(pl.multiple_of(step * mv, 16), mv)
            own = pltpu.make_async_copy(big_ref, g_ref.at[2 * x + y, own_rows], local_sems.at[1])
            own.start()
            own.wait()

        give.wait_send()
        keep.wait()

        @pl.when(step == 3 * nt - 1)
        def _():
            landed = g_ref.at[pl.ds(0, 3), pl.ds(0, half)]
            pltpu.make_async_remote_copy(src_ref=landed, dst_ref=landed, send_sem=send_sem, recv_sem=recv_sem,
                                         device_id=(x, y, c), device_id_type=MESH).wait_recv()

    return pl.pallas_call(
        body, grid=(3, nt),
        in_specs=[pl.BlockSpec((1, mv, width), lambda j, t: (j, t, 0)),
                  pl.BlockSpec((mv, width), lambda j, t: (jnp.minimum(j * nt + t, own_tiles - 1), 0))],
        out_specs=HBM, out_shape=jax.ShapeDtypeStruct((4, rows_all, width), BF16),
        scratch_shapes=[pltpu.SemaphoreType.DMA, pltpu.SemaphoreType.DMA, pltpu.SemaphoreType.DMA((2,))],
        name="gather_fill_" + tag, compiler_params=_params(2))(recv, big)


def _pair_send(tag, p):
    _, rows_all, width = p.shape
    half, mv = rows_all // 2, MOVE_ROWS[tag]
    nt = half // mv

    def body(p_ref, a_ref, send_sem, recv_sem):
        x, y, c, _ = _mesh_pos()
        s, t = pl.program_id(0), pl.program_id(1)
        rows = pl.ds(pl.multiple_of(t * mv, 16), mv)
        give = pltpu.make_async_remote_copy(src_ref=p_ref.at[0], dst_ref=a_ref.at[s, rows], send_sem=send_sem,
                                            recv_sem=recv_sem, device_id=(x, y, 1 - c), device_id_type=MESH)
        give.start()
        give.wait_send()

        @pl.when((s == 3) & (t == nt - 1))
        def _():
            pltpu.make_async_remote_copy(src_ref=a_ref, dst_ref=a_ref, send_sem=send_sem, recv_sem=recv_sem,
                                         device_id=(x, y, c), device_id_type=MESH).wait_recv()

    return pl.pallas_call(
        body, grid=(4, nt),
        in_specs=[pl.BlockSpec((1, mv, width), lambda s, t: (s, (1 - lax.axis_index("c")) * nt + t, 0))],
        out_specs=HBM, out_shape=jax.ShapeDtypeStruct((4, half, width), BF16),
        scratch_shapes=[pltpu.SemaphoreType.DMA, pltpu.SemaphoreType.DMA],
        name="pair_send_" + tag, compiler_params=_params(2))(p)


def _pair_sum(tag, p, a):
    _, half, width = a.shape
    sr = SUM_ROWS[tag]
    nt = half // sr
    assert half % sr == 0

    def body(p_ref, a_ref, o_ref):
        o_ref[...] = (p_ref[...].astype(F32) + a_ref[...].astype(F32)).astype(BF16)

    spec = pl.BlockSpec((1, sr, width), lambda s, t: (s, t, 0))
    return pl.pallas_call(
        body, grid=(4, nt),
        in_specs=[pl.BlockSpec((1, sr, width), lambda s, t: (s, lax.axis_index("c") * nt + t, 0)), spec],
        out_specs=spec, out_shape=jax.ShapeDtypeStruct((4, half, width), BF16),
        name="pair_sum_" + tag, compiler_params=_params(2))(p, a)


def _chip_exchange(cs_a, cs_b):
    def body(ca_ref, cb_ref, ba_ref, bb_ref, send_sems, recv_sems):
        x, y, c, chips = _mesh_pos()
        send = []
        for j, chip in enumerate(chips):
            for n, (src, dst) in enumerate(((ca_ref, ba_ref), (cb_ref, bb_ref))):
                send.append(pltpu.make_async_remote_copy(src_ref=src.at[2 * chip[0] + chip[1]], dst_ref=dst.at[j],
                                                         send_sem=send_sems.at[2 * j + n], recv_sem=recv_sems.at[2 * j + n],
                                                         device_id=(*chip, c), device_id_type=MESH))
        for cp in send:
            cp.start()
        for cp in send:
            cp.wait_recv()
        for cp in send:
            cp.wait_send()

    return pl.pallas_call(
        body, name="chip_exchange", in_specs=[HBM, HBM], out_specs=[HBM, HBM],
        out_shape=[jax.ShapeDtypeStruct((3,) + cs.shape[1:], BF16) for cs in (cs_a, cs_b)],
        scratch_shapes=[pltpu.SemaphoreType.DMA((6,)), pltpu.SemaphoreType.DMA((6,))])(cs_a, cs_b)


def _chip_sum(tag, cs, b):
    _, half, width = cs.shape
    sr = SUM_ROWS[tag]
    nt = half // sr

    def body(c_ref, b_ref, r_ref, buf, send_sems, recv_sem, local_sems):
        x, y, c, _ = _mesh_pos()
        t = pl.program_id(0)
        slot = lax.rem(t, 2)

        def copies(k, tile):
            rows = pl.ds(pl.multiple_of(c * half + tile * sr, 8), sr)
            keep = pltpu.make_async_copy(buf.at[k], r_ref.at[rows], local_sems.at[k])
            give = pltpu.make_async_remote_copy(src_ref=buf.at[k], dst_ref=r_ref.at[rows], send_sem=send_sems.at[k],
                                                recv_sem=recv_sem, device_id=(x, y, 1 - c), device_id_type=MESH)
            return keep, give

        @pl.when(t >= 2)
        def _():
            keep, give = copies(slot, t - 2)
            keep.wait()
            give.wait_send()

        buf[slot] = (c_ref[0].astype(F32) + b_ref[0].astype(F32)) + (b_ref[1].astype(F32) + b_ref[2].astype(F32))
        keep, give = copies(slot, t)
        keep.start()
        give.start()

        @pl.when(t == nt - 1)
        def _():
            for back in (1, 0):
                keep, give = copies(lax.rem(t - back, 2), t - back)
                keep.wait()
                give.wait_send()
            landed = r_ref.at[_half(2 * half, 1 - c)]
            pltpu.make_async_remote_copy(src_ref=landed, dst_ref=landed, send_sem=send_sems.at[0], recv_sem=recv_sem,
                                         device_id=(x, y, c), device_id_type=MESH).wait_recv()

    return pl.pallas_call(
        body, grid=(nt,),
        in_specs=[pl.BlockSpec((1, sr, width), lambda t: (2 * lax.axis_index("x") + lax.axis_index("y"), t, 0)),
                  pl.BlockSpec((3, sr, width), lambda t: (0, t, 0))],
        out_specs=HBM, out_shape=jax.ShapeDtypeStruct((2 * half, width), F32),
        scratch_shapes=[pltpu.VMEM((2, sr, width), F32), pltpu.SemaphoreType.DMA((2,)), pltpu.SemaphoreType.DMA,
                        pltpu.SemaphoreType.DMA((2,))],
        name="chip_sum_" + tag, compiler_params=_params(1))(cs, b)


def _pair_sums_a(tag, p5, b0, nb):
    tile5 = (1, 1, 1, A_HALF, FF_BLK)
    tile4 = (1, 1, A_HALF, FF_BLK)

    def send_body(p_ref, a_ref, send_sem, recv_sem):
        x, y, c, _ = _mesh_pos()
        s, t = pl.program_id(0), pl.program_id(1)
        give = pltpu.make_async_remote_copy(src_ref=p_ref.at[0, 0, 0], dst_ref=a_ref.at[s, t], send_sem=send_sem,
                                            recv_sem=recv_sem, device_id=(x, y, 1 - c), device_id_type=MESH)
        give.start()
        give.wait_send()

        @pl.when((s == 3) & (t == nb - 1))
        def _():
            pltpu.make_async_remote_copy(src_ref=a_ref, dst_ref=a_ref, send_sem=send_sem, recv_sem=recv_sem,
                                         device_id=(x, y, c), device_id_type=MESH).wait_recv()

    shape = jax.ShapeDtypeStruct((4, nb, A_HALF, FF_BLK), BF16)
    recv = pl.pallas_call(
        send_body, grid=(4, nb),
        in_specs=[pl.BlockSpec(tile5, lambda s, t: (s, b0 + t, 1 - lax.axis_index("c"), 0, 0))],
        out_specs=HBM, out_shape=shape, scratch_shapes=[pltpu.SemaphoreType.DMA, pltpu.SemaphoreType.DMA],
        name="pair_send_a" + tag, compiler_params=_params(2))(p5)

    def sum_body(p_ref, a_ref, o_ref):
        o_ref[0, 0] = (p_ref[0, 0, 0].astype(F32) + a_ref[0, 0].astype(F32)).astype(BF16)

    spec = pl.BlockSpec(tile4, lambda s, t: (s, t, 0, 0))
    return pl.pallas_call(
        sum_body, grid=(4, nb),
        in_specs=[pl.BlockSpec(tile5, lambda s, t: (s, b0 + t, lax.axis_index("c"), 0, 0)), spec],
        out_specs=spec, out_shape=shape, name="pair_sum_a" + tag, compiler_params=_params(2))(p5, recv)


def _chip_sum_a(tag, cs, frm, b0, r_prev=None):
    nb = cs.shape[1]
    sr = SUM_ROWS["a"]
    per = A_HALF // sr
    nt = nb * per

    def body(*refs):
        c_ref, b_ref = refs[:2]
        r_ref, buf, send_sems, recv_sem, local_sems = refs[-5:]
        x, y, c, _ = _mesh_pos()
        t = pl.program_id(0)
        slot = lax.rem(t, 2)

        def copies(k, tile):
            dst = r_ref.at[b0 + tile // per, c, pl.ds(pl.multiple_of(lax.rem(tile, per) * sr, 8), sr)]
            keep = pltpu.make_async_copy(buf.at[k], dst, local_sems.at[k])
            give = pltpu.make_async_remote_copy(src_ref=buf.at[k], dst_ref=dst, send_sem=send_sems.at[k],
                                                recv_sem=recv_sem, device_id=(x, y, 1 - c), device_id_type=MESH)
            return keep, give

        @pl.when(t >= 2)
        def _():
            keep, give = copies(slot, t - 2)
            keep.wait()
            give.wait_send()

        buf[slot] = ((c_ref[0, 0].astype(F32) + b_ref[0, 0].astype(F32))
                     + (b_ref[1, 0].astype(F32) + b_ref[2, 0].astype(F32)))
        keep, give = copies(slot, t)
        keep.start()
        give.start()

        @pl.when(t == nt - 1)
        def _():
            for back in (1, 0):
                keep, give = copies(lax.rem(t - back, 2), t - back)
                keep.wait()
                give.wait_send()
            landed = r_ref.at[pl.ds(b0, nb), 1 - c]
            pltpu.make_async_remote_copy(src_ref=landed, dst_ref=landed, send_sem=send_sems.at[0], recv_sem=recv_sem,
                                         device_id=(x, y, c), device_id_type=MESH).wait_recv()

    in_specs = [pl.BlockSpec((1, 1, sr, FF_BLK),
                             lambda t: (2 * lax.axis_index("x") + lax.axis_index("y"), t // per, t % per, 0)),
                pl.BlockSpec((3, 1, sr, FF_BLK), lambda t: (0, t // per, t % per, 0))]
    args = [cs, frm]
    if r_prev is not None:
        in_specs.append(HBM)
        args.append(r_prev)
    return pl.pallas_call(
        body, grid=(nt,), in_specs=in_specs, out_specs=HBM,
        out_shape=jax.ShapeDtypeStruct((4, 2, A_HALF, FF_BLK), F32),
        scratch_shapes=[pltpu.VMEM((2, sr, FF_BLK), F32), pltpu.SemaphoreType.DMA((2,)), pltpu.SemaphoreType.DMA,
                        pltpu.SemaphoreType.DMA((2,))],
        input_output_aliases={2: 0} if r_prev is not None else {},
        name="chip_sum_a" + tag, compiler_params=_params(1))(*args)


def _reduce_scatter(cs_early, from_early, late):
    (cs_late, cs_b), (from_late, from_b) = late
    red = _chip_sum_a("1", cs_early, from_early, 1)
    red = _chip_sum_a("0", cs_late, from_late, 0, red)
    return red.reshape(A_ROWS, FF_BLK), _chip_sum("b", cs_b, from_b)


SMALL_ROWS = 24


def _all_reduce_small(v):
    def body(v_ref, o_ref, all_ref, send_sems, recv_sems):
        x, y, c, _ = _mesh_pos()
        me = 4 * x + 2 * y + c
        all_ref[me] = v_ref[...]
        peers = [(x ^ ((k >> 2) & 1), y ^ ((k >> 1) & 1), c ^ (k & 1)) for k in range(1, 8)]
        idx = lambda p: 4 * p[0] + 2 * p[1] + p[2]
        send = [pltpu.make_async_remote_copy(src_ref=v_ref, dst_ref=all_ref.at[me], send_sem=send_sems.at[k],
                                             recv_sem=recv_sems.at[k], device_id=p, device_id_type=MESH)
                for k, p in enumerate(peers)]
        for cp in send:
            cp.start()
        for k, p in enumerate(peers):
            pltpu.make_async_remote_copy(src_ref=v_ref, dst_ref=all_ref.at[idx(p)], send_sem=send_sems.at[k],
                                         recv_sem=recv_sems.at[k], device_id=p, device_id_type=MESH).wait_recv()
        for cp in send:
            cp.wait_send()
        acc = all_ref[0]
        for d in range(1, 8):
            acc = acc + all_ref[d]
        o_ref[...] = acc

    vm = pl.BlockSpec(memory_space=pltpu.VMEM)
    return pl.pallas_call(
        body, name="all_reduce_small", in_specs=[vm], out_specs=vm,
        out_shape=jax.ShapeDtypeStruct((SMALL_ROWS, 1024), F32),
        scratch_shapes=[pltpu.VMEM((8, SMALL_ROWS, 1024), F32), pltpu.SemaphoreType.DMA((7,)),
                        pltpu.SemaphoreType.DMA((7,))],)(v)


def _adamw(name, w, g, m, v):
    rows, cols = w.shape
    tr = rows
    if rows * cols > 400_000:
        tr = max(t for t in range(8, rows, 8) if rows % t == 0 and t * cols <= 400_000)

    def body(w_ref, g_ref, m_ref, v_ref, d_ref, nm_ref, nv_ref):
        gv = g_ref[...]
        m_new = ADAM_B1 * m_ref[...] + (1.0 - ADAM_B1) * gv
        v_new = ADAM_B2 * v_ref[...] + (1.0 - ADAM_B2) * (gv * gv)
        m_hat = m_new / (1.0 - ADAM_B1 ** ADAM_STEP)
        v_hat = v_new / (1.0 - ADAM_B2 ** ADAM_STEP)
        d_ref[...] = -ADAM_LR * (m_hat / (jnp.sqrt(v_hat) + ADAM_EPS) + ADAM_WD * w_ref[...])
        nm_ref[...] = m_new
        nv_ref[...] = v_new

    spec = pl.BlockSpec((tr, cols), lambda i: (i, 0))
    sds = jax.ShapeDtypeStruct((rows, cols), F32)
    return pl.pallas_call(body, grid=(rows // tr,), in_specs=[spec] * 4, out_specs=[spec] * 3, out_shape=[sds] * 3,
                          name=name, compiler_params=_params(1))(w, g, m, v)


WEIGHTS = ("ffn1_norm", "ffn1_w_gu", "ffn1_w_down", "mix_norm", "ffn2_norm", "ffn2_w_gu", "ffn2_w_down",
           "a_w_in", "a_w_conv", "a_A_log", "a_dt_bias", "a_out_norm", "a_w_out",
           "b_w_in", "b_b_in", "b_sinks", "b_w_out", "b_b_out", "final_norm")
SMALL_SLOTS = {"ffn1_norm": (0, 2048), "mix_norm": (2048, 2048), "ffn2_norm": (4096, 2048), "final_norm": (6144, 1024),
               "a_A_log": (7168, 8), "a_dt_bias": (7296, 8), "a_out_norm": (7424, 128), "b_sinks": (7552, 16),
               "loss": (7680, 1)}
SMALL_SHARDED = {"a_w_conv": (8192, 8192, (4,), 3072), "b_b_in": (20480, 11264, (), 1536), "b_b_out": (22016, 11648, (), 1024)}
DEV_SMALL_ROWS = 12


def _pack_rows(parts):
    rows = []
    for p in parts:
        r = p.reshape(p.shape[0], -1, 1024)
        rows.append(jnp.pad(r, ((0, 0), (0, -r.shape[1] % PACK_TILE), (0, 0))))
    rows.append(jnp.zeros((parts[0].shape[0], PACK_ROWS - PACK_USED, 1024), parts[0].dtype))
    return jnp.concatenate(rows, axis=1)


def _place(vec, off, a):
    return lax.dynamic_update_slice(vec, a.reshape(-1).astype(F32), (off,))


def kernel(x, ffn1_norm, ffn1_w_gu, ffn1_w_down, mix_norm, ffn2_norm, ffn2_w_gu, ffn2_w_down, a_w_in, a_w_conv, a_A_log, a_dt_bias, a_out_norm, a_w_out, b_w_in, b_b_in, b_sinks, b_w_out, b_b_out, final_norm, loss_target, m_ffn1_norm, m_ffn1_w_gu, m_ffn1_w_down, m_mix_norm, m_ffn2_norm, m_ffn2_w_gu, m_ffn2_w_down, m_a_w_in, m_a_w_conv, m_a_A_log, m_a_dt_bias, m_a_out_norm, m_a_w_out, m_b_w_in, m_b_b_in, m_b_sinks, m_b_w_out, m_b_b_out, m_final_norm, v_ffn1_norm, v_ffn1_w_gu, v_ffn1_w_down, v_mix_norm, v_ffn2_norm, v_ffn2_w_gu, v_ffn2_w_down, v_a_w_in, v_a_w_conv, v_a_A_log, v_a_dt_bias, v_a_out_norm, v_a_w_out, v_b_w_in, v_b_b_in, v_b_sinks, v_b_w_out, v_b_b_out, v_final_norm):
    w = dict(zip(WEIGHTS, (ffn1_norm, ffn1_w_gu, ffn1_w_down, mix_norm, ffn2_norm, ffn2_w_gu, ffn2_w_down, a_w_in, a_w_conv,
                           a_A_log, a_dt_bias, a_out_norm, a_w_out, b_w_in, b_b_in, b_sinks, b_w_out, b_b_out, final_norm)))
    m = dict(zip(WEIGHTS, (m_ffn1_norm, m_ffn1_w_gu, m_ffn1_w_down, m_mix_norm, m_ffn2_norm, m_ffn2_w_gu, m_ffn2_w_down,
                           m_a_w_in, m_a_w_conv, m_a_A_log, m_a_dt_bias, m_a_out_norm, m_a_w_out, m_b_w_in, m_b_b_in,
                           m_b_sinks, m_b_w_out, m_b_b_out, m_final_norm)))
    v = dict(zip(WEIGHTS, (v_ffn1_norm, v_ffn1_w_gu, v_ffn1_w_down, v_mix_norm, v_ffn2_norm, v_ffn2_w_gu, v_ffn2_w_down,
                           v_a_w_in, v_a_w_conv, v_a_A_log, v_a_dt_bias, v_a_out_norm, v_a_w_out, v_b_w_in, v_b_b_in,
                           v_b_sinks, v_b_w_out, v_b_b_out, v_final_norm)))
    chip = 2 * lax.axis_index("x") + lax.axis_index("y")

    big_a4 = jnp.concatenate([w["ffn1_w_gu"], w["ffn2_w_gu"]], axis=0).astype(BF16).reshape(4, 2, A_HALF, FF_BLK)
    big_b = _pack_rows([w[n].astype(BF16).reshape(1, -1) for n, _ in PACK])[0]
    small = jnp.zeros((4096,), F32)
    small = _place(small, 0, w["a_w_conv"])
    small = _place(small, 3072, w["b_b_in"])
    small = _place(small, 3456, w["b_b_out"]).reshape(SMALL_SHARD)
    ga = _fill_a(0, big_a4, _gather_chips(big_a4)).reshape(4, 4, D_MODEL, FF_BLK)
    offs, o = {}, 0
    for n, r in PACK:
        offs[n] = (o, r)
        o += -(-r // PACK_TILE) * PACK_TILE

    def finish_weights(rb, rs):
        gb = _gather_fill("b", big_b, rb)
        blk = lambda n: gb[:, offs[n][0]:offs[n][0] + offs[n][1]]
        gsf = lax.dynamic_update_slice(jnp.zeros((4, 4096), F32), small.reshape(1, 4096), (chip, 0))
        for j, other in enumerate((chip ^ 2, chip ^ 1, chip ^ 3)):
            gsf = lax.dynamic_update_slice(gsf, rs[j].reshape(1, 4096), (other, 0))
        return {
            "w_down": gb[:, 0:2816].reshape(4, 4, 704, 1024).transpose(1, 0, 2, 3).reshape(4, 2, FF_BLK, 1024),
            "a_w_in": jnp.pad(blk("a_w_in").reshape(4, 1024, 1028).transpose(1, 0, 2).reshape(1024, A_IN_COLS),
                              ((0, 0), (0, A_IN_PAD - A_IN_COLS))),
            "a_w_out": blk("a_w_out").reshape(1024, 1024),
            "b_w_in": blk("b_w_in").reshape(4, 1024, 384).transpose(1, 0, 2).reshape(1024, 1536),
            "b_w_out": blk("b_w_out").reshape(1024, 1024),
            "a_w_conv": gsf[:, 0:3072].reshape(4, 4, 768).transpose(1, 0, 2).reshape(4, 3072),
            "b_b_in": gsf[:, 3072:3456].reshape(1, 1536),
            "b_b_out": gsf[:, 3456:3712].reshape(1, 1024)}

    def pack_b(g):
        down = g["w_down"].reshape(4, 4, 704, 1024).transpose(1, 0, 2, 3)
        parts = [down[:, 0:2], down[:, 2:4],
                 g["a_w_in"][:, :A_IN_COLS].reshape(1024, 4, 1028).transpose(1, 0, 2), g["a_w_out"].reshape(4, 256, 1024),
                 g["b_w_in"].reshape(1024, 4, 384).transpose(1, 0, 2), g["b_w_out"].reshape(4, 256, 1024)]
        return _pack_rows([a.astype(BF16).reshape(4, -1) for a in parts])

    W = {n: w[n] for n in ("ffn1_norm", "ffn2_norm", "mix_norm", "a_out_norm")}
    W["ga"] = ga
    W["a_gate_p"] = jnp.pad(jnp.concatenate([-jnp.exp(w["a_A_log"]), w["a_dt_bias"]], axis=0), ((0, 0), (8, 112)))
    W["b_sinks"] = jnp.broadcast_to(w["b_sinks"][0][:, None, None], (B_HEADS, 1, 128))
    W["final_norm"] = w["final_norm"][None]

    comm = {"a4": big_a4, "big_b": big_b, "small": small, "finish": finish_weights, "pack_b": pack_b}
    loss_p, dx, g = _local_step(x[0], loss_target[0], W, comm)

    red_a, red_b = _reduce_scatter(g["cs_early"], g["from_early"], g["late"])
    grads = {n: red_b[offs[n][0]:offs[n][0] + offs[n][1]].reshape(w[n].shape) for n, _ in PACK}
    grads["ffn1_w_gu"] = red_a[:A_ROWS // 2].reshape(w["ffn1_w_gu"].shape)
    grads["ffn2_w_gu"] = red_a[A_ROWS // 2:].reshape(w["ffn2_w_gu"].shape)

    sv = jnp.zeros((SMALL_ROWS * 1024,), F32)
    small_g = {"ffn1_norm": g["ffn1_norm"], "mix_norm": g["mix_norm"], "ffn2_norm": g["ffn2_norm"], "final_norm": g["final_norm"],
               "a_A_log": g["a_gate_p"][0, 8:16], "a_dt_bias": g["a_gate_p"][1, 8:16], "a_out_norm": g["a_out_norm"],
               "b_sinks": g["b_sinks"], "loss": loss_p[0, 0:1]}
    for n, (off, _) in SMALL_SLOTS.items():
        sv = _place(sv, off, small_g[n])
    for n, (off, _, _, _) in SMALL_SHARDED.items():
        sv = _place(sv, off, g[n])
    tot = _all_reduce_small(sv.reshape(SMALL_ROWS, 1024)).reshape(-1)
    for n, (off, size) in SMALL_SLOTS.items():
        if n != "loss":
            grads[n] = tot[off:off + size].reshape(w[n].shape)
    for n, (off, _, lead, last) in SMALL_SHARDED.items():
        full = tot[off:off + (lead[0] if lead else 1) * last].reshape(lead + (last,))
        width = last // 4
        grads[n] = lax.dynamic_slice_in_dim(full, chip * width, width, axis=-1).reshape(w[n].shape)
    loss = tot[SMALL_SLOTS["loss"][0]]

    delta, new_m, new_v = {}, {}, {}
    for n in ("ffn1_w_gu", "ffn2_w_gu") + tuple(n for n, _ in PACK):
        two_d = lambda a: a.reshape(-1, a.shape[-1])
        d, nm, nv = _adamw("adamw_" + n, two_d(w[n]), two_d(grads[n]), two_d(m[n]), two_d(v[n]))
        delta[n], new_m[n], new_v[n] = d.reshape(w[n].shape), nm.reshape(w[n].shape), nv.reshape(w[n].shape)

    def dev_small(src):
        vec = jnp.zeros((DEV_SMALL_ROWS * 1024,), F32)
        for n, (off, _) in SMALL_SLOTS.items():
            if n != "loss":
                vec = _place(vec, off, src[n])
        for n, (_, off, _, _) in SMALL_SHARDED.items():
            vec = _place(vec, off, src[n])
        return vec.reshape(DEV_SMALL_ROWS, 1024)

    sd, sm, svv = _adamw("adamw_small", dev_small(w), dev_small(grads), dev_small(m), dev_small(v))
    for n in WEIGHTS:
        if n in SMALL_SLOTS:
            off, size = SMALL_SLOTS[n]
        elif n in SMALL_SHARDED:
            off, size = SMALL_SHARDED[n][1], w[n].size
        else:
            continue
        for dst, src in ((delta, sd), (new_m, sm), (new_v, svv)):
            dst[n] = src.reshape(-1)[off:off + size].reshape(w[n].shape)

    return (loss, dx[None], *[grads[n] for n in WEIGHTS], *[delta[n] for n in WEIGHTS],
            *[new_m[n] for n in WEIGHTS], *[new_v[n] for n in WEIGHTS])
```

```python
import jax
import jax.numpy as jnp
from jax import lax
from jax.experimental import pallas as pl
from jax.experimental.pallas import tpu as pltpu

F32 = jnp.float32
BF16 = jnp.bfloat16

D_MODEL = 1024
EPS = 1e-6
FF_BLK = 1408
A_HEADS = 8
A_DK = 128
A_CHUNK = 64
A_HG = 8
A_IN_COLS = 4112
A_IN_PAD = 4224
B_HEADS = 16
B_KV = 4
B_HD = 64
B_BLK = 128
ADAM_LR, ADAM_B1, ADAM_B2, ADAM_EPS, ADAM_WD, ADAM_STEP = 0.001, 0.9, 0.999, 1e-08, 0.01, 10
MESH = pl.DeviceIdType.MESH
VMEM_LIMIT = 56 * 1024 * 1024
HBM = pl.BlockSpec(memory_space=pl.ANY)


def _params(n_axes):
    return pltpu.CompilerParams(dimension_semantics=("arbitrary",) * n_axes, vmem_limit_bytes=VMEM_LIMIT)


def _sigmoid(x):
    return 1.0 / (1.0 + jnp.exp(-x))


def _dot(a, b, ca, cb):
    return lax.dot_general(a, b, (((ca,), (cb,)), ((), ())), preferred_element_type=F32)


def _dotb(a, b, ca=1, cb=0):
    return _dot(a.astype(BF16), b.astype(BF16), ca, cb)


def _dotx(a, b, ca=1, cb=0):
    return lax.dot_general(a, b, (((ca,), (cb,)), ((), ())), preferred_element_type=F32,
                           precision=lax.Precision.HIGHEST)


def _doth(a, b, ca=1, cb=0):
    return lax.dot_general(a, b, (((ca,), (cb,)), ((), ())), preferred_element_type=F32,
                           precision=lax.Precision.HIGH)


def _rms_fwd(name, x, w):
    T, D = x.shape
    tt = min(T, 512)

    def body(x_ref, w_ref, h_ref):
        xv = x_ref[...]
        r = lax.rsqrt(jnp.mean(xv * xv, axis=-1, keepdims=True) + EPS)
        h_ref[...] = (xv * r * w_ref[...]).astype(BF16)

    return pl.pallas_call(
        body, grid=(T // tt,),
        in_specs=[pl.BlockSpec((tt, D), lambda i: (i, 0)), pl.BlockSpec((1, D), lambda i: (0, 0))],
        out_specs=pl.BlockSpec((tt, D), lambda i: (i, 0)),
        out_shape=jax.ShapeDtypeStruct((T, D), BF16), name=name, compiler_params=_params(1))(x, w)


def _rms_bwd_tile(dh, xv, dy, w):
    r = lax.rsqrt(jnp.mean(xv * xv, axis=-1, keepdims=True) + EPS)
    xhat = xv * r
    dxhat = dh * w
    dx = dy + r * (dxhat - xhat * jnp.mean(dxhat * xhat, axis=-1, keepdims=True))
    return dx, jnp.sum(dh * xhat, axis=0, keepdims=True)


def _colsum(name, a):
    T, N = a.shape
    tt = min(T, 512)

    def body(a_ref, o_ref):
        @pl.when(pl.program_id(0) == 0)
        def _():
            o_ref[...] = jnp.zeros_like(o_ref)
        o_ref[...] += jnp.sum(a_ref[...].astype(F32), axis=0, keepdims=True)

    return pl.pallas_call(
        body, grid=(T // tt,), in_specs=[pl.BlockSpec((tt, N), lambda i: (i, 0))],
        out_specs=pl.BlockSpec((1, N), lambda i: (0, 0)),
        out_shape=jax.ShapeDtypeStruct((1, N), F32), name=name, compiler_params=_params(1))(a)


def _matmul(name, a, b, ca, cb, tm, tn, tk, extra_in, outs, epi, order="ji"):
    M, K, N = a.shape[1 - ca], a.shape[ca], b.shape[1 - cb]
    tm, tn, tk = min(tm, M), min(tn, N), min(tk, K)
    assert M % tm == 0 and N % tn == 0 and K % tk == 0, (name, M, N, K, tm, tn, tk)
    ni, nj, nk = M // tm, N // tn, K // tk
    if order == "ji":
        grid = (nj, ni, nk)
        perm = lambda g0, g1, g2: (g1, g0, g2)
    else:
        grid = (ni, nj, nk)
        perm = lambda g0, g1, g2: (g0, g1, g2)

    def wrap(f):
        return lambda g0, g1, g2: f(*perm(g0, g1, g2))

    a_spec = (pl.BlockSpec((tm, tk), wrap(lambda i, j, k: (i, k))) if ca == 1
              else pl.BlockSpec((tk, tm), wrap(lambda i, j, k: (k, i))))
    b_spec = (pl.BlockSpec((tk, tn), wrap(lambda i, j, k: (k, j))) if cb == 0
              else pl.BlockSpec((tn, tk), wrap(lambda i, j, k: (j, k))))
    ne, no = len(extra_in), len(outs)

    def body(*refs):
        a_ref, b_ref = refs[0], refs[1]
        ex, out = refs[2:2 + ne], refs[2 + ne:2 + ne + no]
        i, j, k = perm(pl.program_id(0), pl.program_id(1), pl.program_id(2))
        p = _dotb(a_ref[...], b_ref[...], ca, cb)
        if nk == 1:
            epi(p, ex, out, i, j)
        else:
            acc_ref = refs[-1]

            @pl.when(k == 0)
            def _():
                acc_ref[...] = p

            @pl.when(k > 0)
            def _():
                acc_ref[...] += p

            @pl.when(k == nk - 1)
            def _():
                epi(acc_ref[...], ex, out, i, j)

    return pl.pallas_call(
        body, grid=grid,
        in_specs=[a_spec, b_spec] + [pl.BlockSpec(bs, wrap(f)) for _, bs, f in extra_in],
        out_specs=[pl.BlockSpec(bs, wrap(f)) for _, bs, f in outs],
        out_shape=[s for s, _, _ in outs],
        scratch_shapes=[pltpu.VMEM((tm, tn), F32)] if nk > 1 else [],
        name=name, compiler_params=_params(3))(a, b, *[x for x, _, _ in extra_in])


def _mm_plain(name, a, b, ca, cb, out_dtype, tm=1024, tn=1024, tk=1024, scale=1.0, bias=None):
    M, N = a.shape[1 - ca], b.shape[1 - cb]
    tm, tn = min(tm, M), min(tn, N)
    extra = [] if bias is None else [(bias, (1, tn), lambda i, j, k: (0, j))]

    def epi(acc, ex, out, i, j):
        r = acc * scale if scale != 1.0 else acc
        if bias is not None:
            r = r + ex[0][...]
        out[0][...] = r.astype(out_dtype)

    return _matmul(name, a, b, ca, cb, tm, tn, tk, extra,
                   [(jax.ShapeDtypeStruct((M, N), out_dtype), (tm, tn), lambda i, j, k: (i, j))], epi)[0]


def _mm_residual(name, a, b, x, scale, bias=None, tk=1024):
    M, N = x.shape
    tm, tn = min(512, M), N
    extra = [(x, (tm, tn), lambda i, j, k: (i, j))]
    if bias is not None:
        extra.append((bias, (1, tn), lambda i, j, k: (0, j)))

    def epi(acc, ex, out, i, j):
        r = acc if bias is None else acc + ex[1][...]
        out[0][...] = ex[0][...] + scale * r

    return _matmul(name, a, b, 1, 0, tm, tn, tk, extra,
                   [(jax.ShapeDtypeStruct((M, N), F32), (tm, tn), lambda i, j, k: (i, j))], epi, order="ij")[0]


def _mm_rms_bwd(name, dproj, w_in, x, dy, nw, tk, out_scale):
    M, N = x.shape
    tm = min(512, M)
    extra = [(x, (tm, N), lambda i, j, k: (i, 0)), (dy, (tm, N), lambda i, j, k: (i, 0)),
             (nw, (1, N), lambda i, j, k: (0, 0))]

    def epi(acc, ex, out, i, j):
        dx, dw = _rms_bwd_tile(acc, ex[0][...], ex[1][...], ex[2][...])
        out[0][...] = dx
        out[1][...] = (dx * out_scale).astype(BF16)

        @pl.when(i == 0)
        def _():
            out[2][...] = dw

        @pl.when(i > 0)
        def _():
            out[2][...] += dw

    return _matmul(name, dproj, w_in, 1, 1, tm, N, tk, extra,
                   [(jax.ShapeDtypeStruct((M, N), F32), (tm, N), lambda i, j, k: (i, 0)),
                    (jax.ShapeDtypeStruct((M, N), BF16), (tm, N), lambda i, j, k: (i, 0)),
                    (jax.ShapeDtypeStruct((1, N), F32), (1, N), lambda i, j, k: (0, 0))], epi, order="ij")


def _peer_copies(pairs, send_sems, recv_sems):
    x, y, c, chips = _mesh_pos()
    n = len(pairs)
    return [pltpu.make_async_remote_copy(src_ref=src(chip), dst_ref=dst.at[j], send_sem=send_sems.at[n * j + k],
                                         recv_sem=recv_sems.at[n * j + k], device_id=(*chip, c), device_id_type=MESH)
            for j, chip in enumerate(chips) for k, (src, dst) in enumerate(pairs)]


def _ffn_gu(name, x, nw, ga, blk, exchange=None):
    T, D = x.shape
    tm = min(T, 512)
    rs = min(tm, 256)
    ni = T // tm

    def body(*refs):
        x_ref, nw_ref, wg0, wg1, wu0, wu1 = refs[:6]
        if exchange is None:
            h_ref, gu_ref, act_ref = refs[6:]
        else:
            bb_ref, sm_ref, h_ref, gu_ref, act_ref, rb_ref, rs_ref, send_sems, recv_sems = refs[6:]
            c = lax.axis_index("c")
            pairs = [(lambda chip: bb_ref.at[_half(PACK_ROWS, c)], rb_ref), (lambda chip: sm_ref, rs_ref)]

            @pl.when(pl.program_id(0) == 0)
            def _():
                for cp in _peer_copies(pairs, send_sems, recv_sems):
                    cp.start()

        for r in range(tm // rs):
            rows = pl.ds(r * rs, rs)
            xv = x_ref[rows, :]
            hv = (xv * lax.rsqrt(jnp.mean(xv * xv, axis=-1, keepdims=True) + EPS) * nw_ref[...]).astype(BF16)
            h_ref[rows, :] = hv
            for j, (wg_ref, wu_ref) in enumerate(((wg0, wu0), (wg1, wu1))):
                g = _dot(hv, wg_ref[0, 0], 1, 0)
                u = _dot(hv, wu_ref[0, 0], 1, 0)
                s = _sigmoid(g)
                gs = g * s
                gu_ref[0, j, rows, :] = (u * (s + gs * (1.0 - s))).astype(BF16)
                gu_ref[1, j, rows, :] = gs.astype(BF16)
                act_ref[j, rows, :] = (gs * u).astype(BF16)

        if exchange is not None:
            @pl.when(pl.program_id(0) == ni - 1)
            def _():
                for cp in _peer_copies(pairs, send_sems, recv_sems):
                    cp.wait_recv()
                    cp.wait_send()

    wspec = lambda q: pl.BlockSpec((1, 1, D, FF_BLK), lambda i: (q, blk, 0, 0), pipeline_mode=pl.Buffered(1))
    in_specs = [pl.BlockSpec((tm, D), lambda i: (i, 0)), pl.BlockSpec((1, D), lambda i: (0, 0)),
                wspec(0), wspec(1), wspec(2), wspec(3)]
    out_specs = [pl.BlockSpec((tm, D), lambda i: (i, 0)), pl.BlockSpec((2, 2, tm, FF_BLK), lambda i: (0, 0, i, 0)),
                 pl.BlockSpec((2, tm, FF_BLK), lambda i: (0, i, 0))]
    out_shape = [jax.ShapeDtypeStruct((T, D), BF16), jax.ShapeDtypeStruct((2, 2, T, FF_BLK), BF16),
                 jax.ShapeDtypeStruct((2, T, FF_BLK), BF16)]
    args, scratch = [x, nw, ga, ga, ga, ga], []
    if exchange is not None:
        big_b, small = exchange
        in_specs += [HBM, HBM]
        out_specs += [HBM, HBM]
        out_shape += [jax.ShapeDtypeStruct((3, big_b.shape[0] // 2, big_b.shape[1]), BF16),
                      jax.ShapeDtypeStruct((3,) + small.shape, F32)]
        scratch = [pltpu.SemaphoreType.DMA((6,)), pltpu.SemaphoreType.DMA((6,))]
        args += [big_b, small]
    return pl.pallas_call(body, grid=(ni,), in_specs=in_specs, out_specs=out_specs, out_shape=out_shape,
                          scratch_shapes=scratch, name=name, compiler_params=_params(1))(*args)


def _ffn_down(name, act, wd, x):
    T, D = x.shape
    tm = min(T, 512)

    def body(act_ref, wd_ref, x_ref, o_ref):
        acc = _dot(act_ref[0], wd_ref[0], 1, 0) + _dot(act_ref[1], wd_ref[1], 1, 0)
        o_ref[...] = x_ref[...] + 0.5 * acc

    return pl.pallas_call(
        body, grid=(T // tm,),
        in_specs=[pl.BlockSpec((2, tm, FF_BLK), lambda i: (0, i, 0)),
                  pl.BlockSpec((2, FF_BLK, D), lambda i: (0, 0, 0)),
                  pl.BlockSpec((tm, D), lambda i: (i, 0))],
        out_specs=pl.BlockSpec((tm, D), lambda i: (i, 0)),
        out_shape=jax.ShapeDtypeStruct((T, D), F32), name=name, compiler_params=_params(1))(act, wd, x)


def _ffn_dact(name, dyh, wd, gu):
    T, D = dyh.shape
    tm = min(T, 1024)
    rs = min(tm, 256)

    def body(dy_ref, wd_ref, gu_ref, o_ref):
        for r in range(tm // rs):
            rows = pl.ds(r * rs, rs)
            dact = _dot(dy_ref[rows, :], wd_ref[0], 1, 1)
            o_ref[0, 0, rows, :] = (dact * gu_ref[0, 0, rows, :].astype(F32)).astype(BF16)
            o_ref[1, 0, rows, :] = (dact * gu_ref[1, 0, rows, :].astype(F32)).astype(BF16)

    return pl.pallas_call(
        body, grid=(2, T // tm),
        in_specs=[pl.BlockSpec((tm, D), lambda j, i: (i, 0)),
                  pl.BlockSpec((1, FF_BLK, D), lambda j, i: (j, 0, 0)),
                  pl.BlockSpec((2, 1, tm, FF_BLK), lambda j, i: (0, j, i, 0))],
        out_specs=pl.BlockSpec((2, 1, tm, FF_BLK), lambda j, i: (0, j, i, 0)),
        out_shape=jax.ShapeDtypeStruct((2, 2, T, FF_BLK), BF16), name=name, compiler_params=_params(2))(dyh, wd, gu)


def _ffn_dwd(name, act, dyh):
    _, T, _ = act.shape
    D = dyh.shape[1]
    tk = min(T, 2048)
    nk = T // tk

    def body(a_ref, d_ref, o_ref, acc_ref):
        k = pl.program_id(1)
        p = _dot(a_ref[0], d_ref[...], 0, 0)

        @pl.when(k == 0)
        def _():
            acc_ref[...] = p

        @pl.when(k > 0)
        def _():
            acc_ref[...] += p

        @pl.when(k == nk - 1)
        def _():
            o_ref[0] = acc_ref[...].astype(BF16)

    return pl.pallas_call(
        body, grid=(2, nk),
        in_specs=[pl.BlockSpec((1, tk, FF_BLK), lambda j, k: (j, k, 0)), pl.BlockSpec((tk, D), lambda j, k: (k, 0))],
        out_specs=pl.BlockSpec((1, FF_BLK, D), lambda j, k: (j, 0, 0)),
        out_shape=jax.ShapeDtypeStruct((2, FF_BLK, D), BF16), scratch_shapes=[pltpu.VMEM((FF_BLK, D), F32)],
        name=name, compiler_params=_params(2))(act, dyh)


def _ffn_dwgu(name, h, dgu, pa, blk, exchange=None):
    T, D = h.shape
    tk = min(T, 2048)
    nk = T // tk

    def body(*refs):
        h_ref, d_ref, pa_in = refs[:3]
        if exchange is None:
            o_ref, acc_ref = refs[3:]
        else:
            cs_ref, o_ref, f_ref, acc_ref, send_sems, recv_sems = refs[3:]
            pairs = [(lambda chip: cs_ref.at[2 * chip[0] + chip[1]], f_ref)]

            @pl.when((pl.program_id(0) == 0) & (pl.program_id(1) == 0))
            def _():
                for cp in _peer_copies(pairs, send_sems, recv_sems):
                    cp.start()

        k = pl.program_id(1)
        p = _dot(h_ref[...], d_ref[0, 0], 0, 0)

        @pl.when(k == 0)
        def _():
            acc_ref[...] = p

        @pl.when(k > 0)
        def _():
            acc_ref[...] += p

        @pl.when(k == nk - 1)
        def _():
            o_ref[0, 0] = acc_ref[...].astype(BF16)

        if exchange is not None:
            @pl.when((pl.program_id(0) == 3) & (pl.program_id(1) == nk - 1))
            def _():
                for cp in _peer_copies(pairs, send_sems, recv_sems):
                    cp.wait_recv()
                    cp.wait_send()

    in_specs = [pl.BlockSpec((tk, D), lambda q, k: (k, 0)),
                pl.BlockSpec((1, 1, tk, FF_BLK), lambda q, k: (q // 2, q % 2, k, 0)), HBM]
    out_specs = [pl.BlockSpec((1, 1, D, FF_BLK), lambda q, k: (q, blk, 0, 0))]
    out_shape = [jax.ShapeDtypeStruct(pa.shape, BF16)]
    scratch, args = [pltpu.VMEM((D, FF_BLK), F32)], [h, dgu, pa]
    if exchange is not None:
        in_specs.append(HBM)
        out_specs.append(HBM)
        out_shape.append(jax.ShapeDtypeStruct((3,) + exchange.shape[1:], BF16))
        scratch += [pltpu.SemaphoreType.DMA((3,)), pltpu.SemaphoreType.DMA((3,))]
        args.append(exchange)
    out = pl.pallas_call(body, grid=(4, nk), in_specs=in_specs, out_specs=out_specs, out_shape=out_shape,
                         scratch_shapes=scratch, input_output_aliases={2: 0}, name=name,
                         compiler_params=_params(2))(*args)
    return out[0] if exchange is None else out


def _ffn_dx(name, dgu, ga, blk, x, dy, nw, out_scale, exchange=None):
    T, D = x.shape
    tm = min(T, 512)
    ni = T // tm

    def body(*refs):
        d_ref, w0, w1, w2, w3, x_ref, dy_ref, nw_ref = refs[:8]
        if exchange is None:
            dx_ref, dxb_ref, dnw_ref = refs[8:]
        else:
            n = len(exchange)
            dx_ref, dxb_ref, dnw_ref = refs[8 + n:11 + n]
            send_sems, recv_sems = refs[-2:]
            pairs = [(lambda chip, r=r: r.at[2 * chip[0] + chip[1]], f)
                     for r, f in zip(refs[8:8 + n], refs[11 + n:11 + 2 * n])]

            @pl.when(pl.program_id(0) == 0)
            def _():
                for cp in _peer_copies(pairs, send_sems, recv_sems):
                    cp.start()

            @pl.when(pl.program_id(0) == ni - 1)
            def _():
                for cp in _peer_copies(pairs, send_sems, recv_sems):
                    cp.wait_recv()
                    cp.wait_send()

        i = pl.program_id(0)
        acc = (_dot(d_ref[0, 0], w0[0, 0], 1, 1) + _dot(d_ref[0, 1], w1[0, 0], 1, 1)
               + _dot(d_ref[1, 0], w2[0, 0], 1, 1) + _dot(d_ref[1, 1], w3[0, 0], 1, 1))
        dx, dw = _rms_bwd_tile(acc, x_ref[...], dy_ref[...], nw_ref[...])
        dx_ref[...] = dx
        dxb_ref[...] = (dx * out_scale).astype(BF16)

        @pl.when(i == 0)
        def _():
            dnw_ref[...] = dw

        @pl.when(i > 0)
        def _():
            dnw_ref[...] += dw

    wspec = lambda q: pl.BlockSpec((1, 1, D, FF_BLK), lambda i: (q, blk, 0, 0), pipeline_mode=pl.Buffered(1))
    row = pl.BlockSpec((tm, D), lambda i: (i, 0))
    in_specs = [pl.BlockSpec((2, 2, tm, FF_BLK), lambda i: (0, 0, i, 0)), wspec(0), wspec(1), wspec(2), wspec(3),
                row, row, pl.BlockSpec((1, D), lambda i: (0, 0))]
    out_specs = [row, row, pl.BlockSpec((1, D), lambda i: (0, 0))]
    out_shape = [jax.ShapeDtypeStruct((T, D), F32), jax.ShapeDtypeStruct((T, D), BF16), jax.ShapeDtypeStruct((1, D), F32)]
    args, scratch = [dgu, ga, ga, ga, ga, x, dy, nw], []
    if exchange is not None:
        in_specs += [HBM] * len(exchange)
        out_specs += [HBM] * len(exchange)
        out_shape += [jax.ShapeDtypeStruct((3,) + cs.shape[1:], BF16) for cs in exchange]
        scratch = [pltpu.SemaphoreType.DMA((3 * len(exchange),)), pltpu.SemaphoreType.DMA((3 * len(exchange),))]
        args += list(exchange)
    return pl.pallas_call(body, grid=(ni,), in_specs=in_specs, out_specs=out_specs, out_shape=out_shape,
                          scratch_shapes=scratch, name=name, compiler_params=_params(1))(*args)


def _ffn_fwd(tag, x, nw, ga, blk, wd):
    h, gu, act = _ffn_gu(tag + "_gu", x, nw, ga, blk)
    return _ffn_down(tag + "_down", act, wd, x), (h, gu, act)


def _ffn_bwd(tag, dy, dyh, x, nw, ga, blk, wd, saved, pa, out_scale):
    h, gu, act = saved
    dgu = _ffn_dact(tag + "_dact", dyh, wd, gu)
    d_wd = _ffn_dwd(tag + "_dwd", act, dyh)
    pa = _ffn_dwgu(tag + "_dwgu", h, dgu, pa, blk)
    dx, dxb, d_nw = _ffn_dx(tag + "_dx", dgu, ga, blk, x, dy, nw, out_scale)
    return dx, dxb, d_nw, pa, d_wd


def _conv_shifts(cur, halo, first, sign):
    tt = cur.shape[0]
    halo = jnp.where(first, 0.0, halo)
    rid = lax.broadcasted_iota(jnp.int32, (8, cur.shape[1]), 0)
    out = [cur]
    for s in (1, 2, 3):
        if sign < 0:
            sh = pltpu.roll(cur, s, 0)
            edge = jnp.where(rid < s, pltpu.roll(halo, s, 0), sh[0:8])
            sh = jnp.concatenate([edge, sh[8:]], axis=0) if tt > 8 else edge
        else:
            sh = pltpu.roll(cur, tt - s, 0)
            edge = jnp.where(rid >= 8 - s, pltpu.roll(halo, 8 - s, 0), sh[tt - 8:])
            sh = jnp.concatenate([sh[:tt - 8], edge], axis=0) if tt > 8 else edge
        out.append(sh)
    return out


def _conv_taps(cur, halo, w, first, sign, shifts=None):
    shifts = _conv_shifts(cur, halo, first, sign) if shifts is None else shifts
    acc = w[3:4, :] * shifts[0]
    for s in (1, 2, 3):
        acc = acc + w[3 - s:4 - s, :] * shifts[s]
    return acc


def _gdn_prep(name, proj, wconv, gate_p):
    T = proj.shape[0]
    tt = min(T, 256)
    hb = tt // 8
    nch = tt // A_CHUNK

    def body(cur_ref, halo_ref, ba_ref, w_ref, gp_ref, qkv_ref, bg_ref, gc_ref):
        first = pl.program_id(0) == 0
        for c in range(24):
            cols = pl.ds(c * 128, 128)
            conv = _conv_taps(cur_ref[:, cols], halo_ref[:, cols], w_ref[:, cols], first, -1)
            y = conv * _sigmoid(conv)
            if c < 16:
                y = y * lax.rsqrt(jnp.sum(y * y, axis=-1, keepdims=True) + EPS)
                if c < 8:
                    y = y * (A_DK ** -0.5)
            qkv_ref[:, cols] = y
        ba = ba_ref[...]
        lane = lax.broadcasted_iota(jnp.int32, ba.shape, 1)
        zarg = ba + gp_ref[1:2, :]
        softplus = jnp.maximum(zarg, 0.0) + jnp.log(1.0 + jnp.exp(-jnp.abs(zarg)))
        bg = jnp.where(lane < 8, _sigmoid(ba), jnp.where(lane < 16, gp_ref[0:1, :] * softplus, 0.0))
        bg_ref[...] = bg
        tri = (lax.broadcasted_iota(jnp.int32, (A_CHUNK, A_CHUNK), 0)
               >= lax.broadcasted_iota(jnp.int32, (A_CHUNK, A_CHUNK), 1)).astype(F32)
        for c in range(nch):
            rows = pl.ds(c * A_CHUNK, A_CHUNK)
            gc_ref[rows, :] = _dotx(tri, bg[c * A_CHUNK:(c + 1) * A_CHUNK, :])

    return pl.pallas_call(
        body, grid=(T // tt,),
        in_specs=[pl.BlockSpec((tt, 3072), lambda i: (i, 0)),
                  pl.BlockSpec((8, 3072), lambda i: (jnp.maximum(i * hb - 1, 0), 0)),
                  pl.BlockSpec((tt, 128), lambda i: (i, 32)),
                  pl.BlockSpec((4, 3072), lambda i: (0, 0)),
                  pl.BlockSpec((2, 128), lambda i: (0, 0))],
        out_specs=[pl.BlockSpec((tt, 3072), lambda i: (i, 0)), pl.BlockSpec((tt, 128), lambda i: (i, 0)),
                   pl.BlockSpec((tt, 128), lambda i: (i, 0))],
        out_shape=[jax.ShapeDtypeStruct((T, 3072), F32), jax.ShapeDtypeStruct((T, 128), F32),
                   jax.ShapeDtypeStruct((T, 128), F32)],
        name=name, compiler_params=_params(1))(proj, proj, proj, wconv, gate_p)


def _chunk_masks():
    ri = lax.broadcasted_iota(jnp.int32, (A_CHUNK, A_CHUNK), 0)
    ci = lax.broadcasted_iota(jnp.int32, (A_CHUNK, A_CHUNK), 1)
    return ri >= ci, ri > ci, ri == ci


def _chunk_local(q, k, gcol, grow, bcol):
    incl, strict, _ = _chunk_masks()
    dec = jnp.where(incl, jnp.exp(jnp.where(incl, gcol - grow, 0.0)), 0.0)
    e = jnp.exp(gcol)
    glast = grow[:, A_CHUNK - 1:A_CHUNK]
    f = jnp.exp(glast - gcol)
    gl = jnp.exp(glast)
    kb = k * bcol
    lmat = jnp.where(strict, _dotb(kb, k, 1, 1) * dec, 0.0)
    amat = jnp.where(incl, _dotb(q, k, 1, 1) * dec, 0.0)
    return dec, e, f, gl, kb, lmat, amat


def _unit_lower_inverse(lmats):
    _, _, eye = _chunk_masks()
    ts = [jnp.where(eye, 1.0, 0.0) - lm for lm in lmats]
    lps = [_doth(lm, lm) for lm in lmats]
    for it in range(5):
        ts = [t + _doth(t, lp) for t, lp in zip(ts, lps)]
        if it < 4:
            lps = [_doth(lp, lp) for lp in lps]
    return ts


def _gate_columns(bgt, gct):
    sel = (lax.broadcasted_iota(jnp.int32, (16, 128), 0) == lax.broadcasted_iota(jnp.int32, (16, 128), 1)).astype(F32)
    g_rows = _dotx(sel, gct, 1, 1)
    hs = range(A_HEADS)
    return ([bgt[:, h:h + 1] for h in hs], [gct[:, 8 + h:9 + h] for h in hs], [g_rows[8 + h:9 + h, :] for h in hs])


def _gdn_delta_fwd(name, qkv, bg, gcum, a4=None):
    T = qkv.shape[0]
    tt = min(T, 512)
    nch = tt // A_CHUNK
    NC = T // A_CHUNK
    ni = T // tt
    wd, ng = 128 * A_HG, A_HEADS // A_HG
    assert ng == 1

    def body(*refs):
        q_ref, k_ref, v_ref, bg_ref, gc_ref = refs[:5]
        if a4 is None:
            o_ref, s_ref, t_ref, u_ref, w_ref, state = refs[5:]
        else:
            a4_ref, o_ref, s_ref, t_ref, u_ref, w_ref, ra_ref, state, send_sems, recv_sems = refs[5:]

            @pl.when(pl.program_id(1) == 0)
            def _():
                _later_blocks_start(a4_ref, ra_ref, send_sems, recv_sems)

        @pl.when(pl.program_id(1) == 0)
        def _():
            state[...] = jnp.zeros_like(state)

        def chunk(c, carry):
            rows = pl.ds(pl.multiple_of(c * A_CHUNK, A_CHUNK), A_CHUNK)
            hs = range(A_HG)
            cols = [pl.ds(h * 128, 128) for h in hs]
            q = [q_ref[rows, cols[h]] for h in hs]
            k = [k_ref[rows, cols[h]] for h in hs]
            v = [v_ref[rows, cols[h]] for h in hs]
            bcl, gcl, grw = _gate_columns(bg_ref[rows, :], gc_ref[rows, :])
            loc = [_chunk_local(q[h], k[h], gcl[h], grw[h], bcl[h]) for h in hs]
            e, f, gl, kb, amat = ([l[i] for l in loc] for i in (1, 2, 3, 4, 6))
            tinv = _unit_lower_inverse([l[5] for l in loc])
            u = [_doth(tinv[h], v[h] * bcl[h]) for h in hs]
            w = [_doth(tinv[h], kb[h] * e[h]) for h in hs]
            s = [state[h] for h in hs]
            vn = [u[h] - _dotb(w[h], s[h]) for h in hs]
            o_s = [_dotb(q[h] * e[h], s[h]) for h in hs]
            o_a = [_dotb(amat[h], vn[h]) for h in hs]
            s_new = [s[h] * gl[h] + _dotb(k[h] * f[h], vn[h], 0, 0) for h in hs]
            for h in hs:
                s_ref[h, c] = s[h].astype(BF16)
                t_ref[h, c] = tinv[h]
                u_ref[rows, cols[h]] = u[h]
                w_ref[rows, cols[h]] = w[h]
                o_ref[rows, cols[h]] = o_s[h] + o_a[h]
                state[h] = s_new[h]
            return carry

        lax.fori_loop(0, nch, chunk, 0)

        if a4 is not None:
            @pl.when(pl.program_id(1) == ni - 1)
            def _():
                x, y, c, chips = _mesh_pos()
                for j, chip in enumerate(chips):
                    cp = pltpu.make_async_remote_copy(src_ref=a4_ref.at[pl.ds(1, 3), c], dst_ref=ra_ref.at[j],
                                                      send_sem=send_sems.at[j], recv_sem=recv_sems.at[j],
                                                      device_id=(*chip, c), device_id_type=MESH)
                    cp.wait_recv()
                    cp.wait_send()

    hd = lambda col: pl.BlockSpec((tt, wd), lambda h, i: (i, col * ng + h))
    gate_spec = pl.BlockSpec((tt, 128), lambda h, i: (i, 0))
    in_specs = [hd(0), hd(1), hd(2), gate_spec, gate_spec]
    out_specs = [pl.BlockSpec((tt, wd), lambda h, i: (i, h)),
                 pl.BlockSpec((A_HG, nch, 128, 128), lambda h, i: (h, i, 0, 0)),
                 pl.BlockSpec((A_HG, nch, A_CHUNK, A_CHUNK), lambda h, i: (h, i, 0, 0)),
                 pl.BlockSpec((tt, wd), lambda h, i: (i, h)), pl.BlockSpec((tt, wd), lambda h, i: (i, h))]
    out_shape = [jax.ShapeDtypeStruct((T, 1024), F32), jax.ShapeDtypeStruct((A_HEADS, NC, 128, 128), BF16),
                 jax.ShapeDtypeStruct((A_HEADS, NC, A_CHUNK, A_CHUNK), F32),
                 jax.ShapeDtypeStruct((T, 1024), F32), jax.ShapeDtypeStruct((T, 1024), F32)]
    scratch = [pltpu.VMEM((A_HG, 128, 128), F32)]
    args = [qkv, qkv, qkv, bg, gcum]
    if a4 is not None:
        in_specs.append(HBM)
        out_specs.append(HBM)
        out_shape.append(jax.ShapeDtypeStruct((3, 3, A_HALF, FF_BLK), BF16))
        scratch += [pltpu.SemaphoreType.DMA((3,)), pltpu.SemaphoreType.DMA((3,))]
        args.append(a4)
    return pl.pallas_call(body, grid=(ng, ni), in_specs=in_specs, out_specs=out_specs, out_shape=out_shape,
                          scratch_shapes=scratch, name=name, compiler_params=_params(2))(*args)


def _grads_exchange(cs_ref, from_ref, send_sems, recv_sems):
    x, y, c, chips = _mesh_pos()
    return [pltpu.make_async_remote_copy(src_ref=cs_ref.at[2 * chip[0] + chip[1]], dst_ref=from_ref.at[j],
                                         send_sem=send_sems.at[j], recv_sem=recv_sems.at[j],
                                         device_id=(*chip, c), device_id_type=MESH)
            for j, chip in enumerate(chips)]


def _gdn_delta_bwd(name, qkv, bg, gcum, d_o, s_sv, t_sv, u_sv, w_sv, cs_early=None):
    T = qkv.shape[0]
    tt = min(T, 512)
    nch = tt // A_CHUNK
    ni = T // tt

    def body(*refs):
        q_ref, k_ref, v_ref, bg_ref, gc_ref, do_ref, s_ref, t_ref, u_ref, w_ref = refs[:10]
        if cs_early is None:
            dq_ref, dk_ref, dv_ref, dbg_ref, dstate = refs[10:]
        else:
            cs_ref, dq_ref, dk_ref, dv_ref, dbg_ref, from_ref, dstate, send_sems, recv_sems = refs[10:]

            @pl.when(pl.program_id(1) == 0)
            def _():
                for cp in _grads_exchange(cs_ref, from_ref, send_sems, recv_sems):
                    cp.start()

        @pl.when(pl.program_id(1) == 0)
        def _():
            dstate[...] = jnp.zeros_like(dstate)

        incl, strict, _ = _chunk_masks()
        upper = (lax.broadcasted_iota(jnp.int32, (A_CHUNK, A_CHUNK), 0)
                 <= lax.broadcasted_iota(jnp.int32, (A_CHUNK, A_CHUNK), 1)).astype(F32)
        last_row = lax.broadcasted_iota(jnp.int32, (A_CHUNK, 1), 0) == A_CHUNK - 1
        ones = jnp.ones((A_CHUNK, 128), F32)

        rsum = lambda x: jnp.sum(x, axis=1, keepdims=True)

        def chunk(cc, carry):
            c = nch - 1 - cc
            rows = pl.ds(pl.multiple_of(c * A_CHUNK, A_CHUNK), A_CHUNK)
            bcl, gcl, grw = _gate_columns(bg_ref[rows, :], gc_ref[rows, :])
            lane = lax.broadcasted_iota(jnp.int32, (A_CHUNK, 128), 1)
            dbg = jnp.zeros((A_CHUNK, 128), F32)
            for first in range(0, A_HG, 4):
                hs = range(first, first + 4)
                cols = {h: pl.ds(h * 128, 128) for h in hs}
                q = {h: q_ref[rows, cols[h]] for h in hs}
                k = {h: k_ref[rows, cols[h]] for h in hs}
                v = {h: v_ref[rows, cols[h]] for h in hs}
                do = {h: do_ref[rows, cols[h]] for h in hs}
                u = {h: u_ref[rows, cols[h]] for h in hs}
                w = {h: w_ref[rows, cols[h]] for h in hs}
                s = {h: s_ref[h, c] for h in hs}
                tinv = {h: t_ref[h, c] for h in hs}
                ds = {h: dstate[h] for h in hs}
                loc = {h: _chunk_local(q[h], k[h], gcl[h], grw[h], bcl[h]) for h in hs}
                dec, e, f, gl, kb, lmat, amat = ({h: loc[h][i] for h in hs} for i in range(7))
                qd = {h: q[h] * e[h] for h in hs}
                kd = {h: k[h] * f[h] for h in hs}
                ke = {h: kb[h] * e[h] for h in hs}
                vn = {h: u[h] - _dotb(w[h], s[h]) for h in hs}
                d_qd = {h: _dotb(do[h], s[h], 1, 1) for h in hs}
                d_a = {h: jnp.where(incl, _dotb(do[h], vn[h], 1, 1), 0.0) for h in hs}
                d_vn1 = {h: _dotb(amat[h], do[h], 0, 0) for h in hs}
                d_vn = {h: d_vn1[h] + _dotb(kd[h], ds[h]) for h in hs}
                d_kd = {h: _dotb(vn[h], ds[h], 1, 1) for h in hs}
                d_w = {h: -_dotb(d_vn[h], s[h], 1, 1) for h in hs}
                ds_q = {h: _dotb(qd[h], do[h], 0, 0) for h in hs}
                ds_w = {h: _dotb(w[h], d_vn[h], 0, 0) for h in hs}
                d_bv = {h: _doth(tinv[h], d_vn[h], 0, 0) for h in hs}
                d_ke = {h: _doth(tinv[h], d_w[h], 0, 0) for h in hs}
                d_l1 = {h: _dotb(d_bv[h], u[h], 1, 1) for h in hs}
                d_l = {h: -jnp.where(strict, d_l1[h] + _dotb(d_ke[h], w[h], 1, 1), 0.0) for h in hs}
                d_kk = {h: d_l[h] * dec[h] for h in hs}
                d_qk = {h: d_a[h] * dec[h] for h in hs}
                d_kb = {h: _dotb(d_kk[h], k[h]) for h in hs}
                dk1 = {h: _dotb(d_kk[h], kb[h], 0, 0) for h in hs}
                dk2 = {h: _dotb(d_qk[h], q[h], 0, 0) for h in hs}
                dq1 = {h: _dotb(d_qk[h], k[h]) for h in hs}
                m = {h: d_l[h] * lmat[h] + d_a[h] * amat[h] for h in hs}
                col_m = {h: _dotx(m[h], ones, 0, 0)[:, 0:1] for h in hs}
                d_gc = {}
                for h in hs:
                    d_gl = jnp.sum(jnp.sum(ds[h] * s[h].astype(F32), axis=1, keepdims=True), axis=0, keepdims=True)
                    r_kd = rsum(d_kd[h] * kd[h])
                    tail = jnp.sum(r_kd, axis=0, keepdims=True) + d_gl * gl[h]
                    d_gc[h] = (rsum(m[h]) - col_m[h] + rsum(d_qd[h] * qd[h]) - r_kd + rsum(d_ke[h] * ke[h])
                               + jnp.where(last_row, tail, 0.0))
                dg = {h: _dotx(upper, d_gc[h] * ones)[:, 0:1] for h in hs}
                for h in hs:
                    dstate[h] = gl[h] * ds[h] + ds_q[h] - ds_w[h]
                    dk_ref[rows, cols[h]] = (dk1[h] + dk2[h] + d_kd[h] * f[h] + d_ke[h] * (bcl[h] * e[h])
                                             + d_kb[h] * bcl[h])
                    dq_ref[rows, cols[h]] = dq1[h] + d_qd[h] * e[h]
                    dv_ref[rows, cols[h]] = d_bv[h] * bcl[h]
                    d_beta = rsum(d_ke[h] * k[h]) * e[h] + rsum(d_kb[h] * k[h]) + rsum(d_bv[h] * v[h])
                    dbg = jnp.where(lane == h, d_beta, jnp.where(lane == 8 + h, dg[h], dbg))
            dbg_ref[rows, :] = dbg
            return carry

        lax.fori_loop(0, nch, chunk, 0)

        if cs_early is not None:
            @pl.when(pl.program_id(1) == ni - 1)
            def _():
                for cp in _grads_exchange(cs_ref, from_ref, send_sems, recv_sems):
                    cp.wait_recv()
                    cp.wait_send()

    wd, ng = 128 * A_HG, A_HEADS // A_HG
    assert ng == 1
    rev = lambda i: ni - 1 - i
    hd = lambda col: pl.BlockSpec((tt, wd), lambda h, i: (rev(i), col * ng + h))
    hd1 = pl.BlockSpec((tt, wd), lambda h, i: (rev(i), h))
    gate_spec = pl.BlockSpec((tt, 128), lambda h, i: (rev(i), 0))
    in_specs = [hd(0), hd(1), hd(2), gate_spec, gate_spec, hd1,
                pl.BlockSpec((A_HG, nch, 128, 128), lambda h, i: (h, rev(i), 0, 0)),
                pl.BlockSpec((A_HG, nch, A_CHUNK, A_CHUNK), lambda h, i: (h, rev(i), 0, 0)), hd1, hd1]
    out_specs = [hd1, hd1, hd1, gate_spec]
    out_shape = [jax.ShapeDtypeStruct((T, 1024), F32)] * 3 + [jax.ShapeDtypeStruct((T, 128), F32)]
    scratch = [pltpu.VMEM((A_HG, 128, 128), F32)]
    args = [qkv, qkv, qkv, bg, gcum, d_o, s_sv, t_sv, u_sv, w_sv]
    if cs_early is not None:
        in_specs.append(HBM)
        out_specs.append(HBM)
        out_shape.append(jax.ShapeDtypeStruct((3,) + cs_early.shape[1:], BF16))
        scratch += [pltpu.SemaphoreType.DMA((3,)), pltpu.SemaphoreType.DMA((3,))]
        args.append(cs_early)
    return pl.pallas_call(body, grid=(ng, ni), in_specs=in_specs, out_specs=out_specs, out_shape=out_shape,
                          scratch_shapes=scratch, name=name, compiler_params=_params(2))(*args)


def _gdn_gate_fwd(name, o, proj, nw):
    T = o.shape[0]
    tt = min(T, 512)

    def body(o_ref, z_ref, nw_ref, y_ref):
        for h in range(A_HEADS):
            cols = pl.ds(h * 128, 128)
            ov, z = o_ref[:, cols], z_ref[:, cols]
            r = lax.rsqrt(jnp.mean(ov * ov, axis=-1, keepdims=True) + EPS)
            y_ref[:, cols] = (ov * r * nw_ref[...] * (z * _sigmoid(z))).astype(BF16)

    return pl.pallas_call(
        body, grid=(T // tt,),
        in_specs=[pl.BlockSpec((tt, 1024), lambda i: (i, 0)), pl.BlockSpec((tt, 1024), lambda i: (i, 3)),
                  pl.BlockSpec((1, 128), lambda i: (0, 0))],
        out_specs=pl.BlockSpec((tt, 1024), lambda i: (i, 0)),
        out_shape=jax.ShapeDtypeStruct((T, 1024), BF16), name=name, compiler_params=_params(1))(o, proj, nw)


def _gdn_gate_bwd(name, dy2, o, proj, nw):
    T = o.shape[0]
    tt = min(T, 512)

    def body(dy_ref, o_ref, z_ref, nw_ref, do_ref, dz_ref, dnw_ref):
        dnw = jnp.zeros((1, 128), F32)
        for h in range(A_HEADS):
            cols = pl.ds(h * 128, 128)
            dy, ov, z = dy_ref[:, cols], o_ref[:, cols], z_ref[:, cols]
            s = _sigmoid(z)
            sz = z * s
            r = lax.rsqrt(jnp.mean(ov * ov, axis=-1, keepdims=True) + EPS)
            xhat = ov * r
            dn = dy * sz
            dz_ref[:, cols] = dy * (xhat * nw_ref[...]) * (s + z * s * (1.0 - s))
            dxhat = dn * nw_ref[...]
            do_ref[:, cols] = r * (dxhat - xhat * jnp.mean(dxhat * xhat, axis=-1, keepdims=True))
            dnw = dnw + jnp.sum(dn * xhat, axis=0, keepdims=True)

        @pl.when(pl.program_id(0) == 0)
        def _():
            dnw_ref[...] = dnw

        @pl.when(pl.program_id(0) > 0)
        def _():
            dnw_ref[...] += dnw

    blk = lambda c: pl.BlockSpec((tt, 1024), lambda i: (i, c))
    return pl.pallas_call(
        body, grid=(T // tt,),
        in_specs=[blk(0), blk(0), blk(3), pl.BlockSpec((1, 128), lambda i: (0, 0))],
        out_specs=[blk(0), blk(0), pl.BlockSpec((1, 128), lambda i: (0, 0))],
        out_shape=[jax.ShapeDtypeStruct((T, 1024), F32), jax.ShapeDtypeStruct((T, 1024), F32),
                   jax.ShapeDtypeStruct((1, 128), F32)],
        name=name, compiler_params=_params(1))(dy2, o, proj, nw)


def _gdn_prep_bwd1(name, proj, wconv, gate_p, dq, dk, dv, dbg):
    T = proj.shape[0]
    tt = min(T, 256)
    hb = tt // 8

    def body(cur_ref, halo_ref, ba_ref, w_ref, gp_ref, dq_ref, dk_ref, dv_ref, dbg_ref,
             dc_ref, dw_ref, dba_ref, dgp_ref):
        first = pl.program_id(0) == 0
        rid = lax.broadcasted_iota(jnp.int32, (8, 128), 0)
        for c in range(24):
            cols = pl.ds(c * 128, 128)
            cur = cur_ref[:, cols]
            shifts = _conv_shifts(cur, halo_ref[:, cols], first, -1)
            conv = _conv_taps(cur, None, w_ref[:, cols], first, -1, shifts)
            s = _sigmoid(conv)
            y = conv * s
            if c < 16:
                dref = dq_ref if c < 8 else dk_ref
                dn = dref[:, pl.ds((c % 8) * 128, 128)]
                rinv = lax.rsqrt(jnp.sum(y * y, axis=-1, keepdims=True) + EPS)
                yhat = y * rinv
                dyv = rinv * (dn - yhat * jnp.sum(dn * yhat, axis=-1, keepdims=True))
                if c < 8:
                    dyv = dyv * (A_DK ** -0.5)
            else:
                dyv = dv_ref[:, pl.ds((c - 16) * 128, 128)]
            dc = dyv * (s + y * (1.0 - s))
            dc_ref[:, cols] = dc
            parts = [jnp.sum(dc * sh, axis=0, keepdims=True) for sh in shifts]
            dwc = jnp.concatenate(parts[::-1], axis=0)

            @pl.when(first)
            def _():
                dw_ref[:, cols] = dwc

            @pl.when(jnp.logical_not(first))
            def _():
                dw_ref[:, cols] += dwc

        ba = ba_ref[...]
        dbg = dbg_ref[...]
        lane = lax.broadcasted_iota(jnp.int32, ba.shape, 1)
        sb = _sigmoid(ba)
        zarg = ba + gp_ref[1:2, :]
        softplus = jnp.maximum(zarg, 0.0) + jnp.log(1.0 + jnp.exp(-jnp.abs(zarg)))
        d_b = dbg * sb * (1.0 - sb)
        d_a = dbg * gp_ref[0:1, :] * _sigmoid(zarg)
        dba_ref[...] = jnp.where(lane < 8, d_b, jnp.where(lane < 16, d_a, 0.0))
        g = gp_ref[0:1, :] * softplus
        in_a = (lane >= 8) & (lane < 16)
        sums = jnp.concatenate([jnp.sum(jnp.where(in_a, dbg * g, 0.0), axis=0, keepdims=True),
                                jnp.sum(jnp.where(in_a, d_a, 0.0), axis=0, keepdims=True)], axis=0)

        @pl.when(first)
        def _():
            dgp_ref[...] = sums

        @pl.when(jnp.logical_not(first))
        def _():
            dgp_ref[...] += sums

    row = lambda w, c=0: pl.BlockSpec((tt, w), lambda i: (i, c))
    return pl.pallas_call(
        body, grid=(T // tt,),
        in_specs=[row(3072), pl.BlockSpec((8, 3072), lambda i: (jnp.maximum(i * hb - 1, 0), 0)), row(128, 32),
                  pl.BlockSpec((4, 3072), lambda i: (0, 0)), pl.BlockSpec((2, 128), lambda i: (0, 0)),
                  row(1024), row(1024), row(1024), row(128)],
        out_specs=[row(3072), pl.BlockSpec((4, 3072), lambda i: (0, 0)), row(128),
                   pl.BlockSpec((2, 128), lambda i: (0, 0))],
        out_shape=[jax.ShapeDtypeStruct((T, 3072), F32), jax.ShapeDtypeStruct((4, 3072), F32),
                   jax.ShapeDtypeStruct((T, 128), F32), jax.ShapeDtypeStruct((2, 128), F32)],
        name=name, compiler_params=_params(1))(proj, proj, proj, wconv, gate_p, dq, dk, dv, dbg)


def _gdn_prep_bwd2(name, dc, wconv, dz, dba):
    T = dc.shape[0]
    tt = min(T, 256)
    hb = tt // 8
    ni = T // tt

    def body(cur_ref, halo_ref, w_ref, dz_ref, dba_ref, o_ref):
        last = pl.program_id(0) == ni - 1
        for c in range(24):
            cols = pl.ds(c * 128, 128)
            o_ref[:, cols] = _conv_taps(cur_ref[:, cols], halo_ref[:, cols], w_ref[:, cols], last, +1).astype(BF16)
        o_ref[:, pl.ds(3072, 1024)] = dz_ref[...].astype(BF16)
        o_ref[:, pl.ds(4096, 128)] = dba_ref[...].astype(BF16)

    return pl.pallas_call(
        body, grid=(ni,),
        in_specs=[pl.BlockSpec((tt, 3072), lambda i: (i, 0)),
                  pl.BlockSpec((8, 3072), lambda i: (jnp.minimum((i + 1) * hb, T // 8 - 1), 0)),
                  pl.BlockSpec((4, 3072), lambda i: (0, 0)),
                  pl.BlockSpec((tt, 1024), lambda i: (i, 0)), pl.BlockSpec((tt, 128), lambda i: (i, 0))],
        out_specs=pl.BlockSpec((tt, A_IN_PAD), lambda i: (i, 0)),
        out_shape=jax.ShapeDtypeStruct((T, A_IN_PAD), BF16), name=name, compiler_params=_params(1))(
            dc, dc, wconv, dz, dba)


def _gdn_fwd(x, nw, w_in, wconv, gate_p, out_nw, w_out, a4=None):
    h = _rms_fwd("a_rms", x, nw)
    proj = _mm_plain("a_proj", h, w_in, 1, 0, F32, tn=FF_BLK)
    qkv, bg, gcum = _gdn_prep("a_prep", proj, wconv, gate_p)
    o, s_sv, t_sv, u_sv, w_sv, *arrived = _gdn_delta_fwd("a_delta", qkv, bg, gcum, a4)
    o2 = _gdn_gate_fwd("a_gate", o, proj, out_nw)
    y = _mm_residual("a_out", o2, w_out, x, 1.0)
    return y, (h, proj, qkv, bg, gcum, o, s_sv, t_sv, u_sv, w_sv, o2), (arrived[0] if arrived else None)


def _gdn_bwd(dy, dyb, x, nw, w_in, wconv, gate_p, out_nw, w_out, saved, out_scale, cs_early=None):
    h, proj, qkv, bg, gcum, o, s_sv, t_sv, u_sv, w_sv, o2 = saved
    d_o2 = _mm_plain("a_dout", dyb, w_out, 1, 1, F32)
    d_wout = _mm_plain("a_dwout", o2, dyb, 0, 0, F32)
    d_o, d_z, d_outnw = _gdn_gate_bwd("a_dgate", d_o2, o, proj, out_nw)
    dq, dk, dv, dbg, *arrived = _gdn_delta_bwd("a_ddelta", qkv, bg, gcum, d_o, s_sv, t_sv, u_sv, w_sv, cs_early)
    dc, d_wconv, dba, dgp = _gdn_prep_bwd1("a_dprep1", proj, wconv, gate_p, dq, dk, dv, dbg)
    dproj = _gdn_prep_bwd2("a_dprep2", dc, wconv, d_z, dba)
    d_win = _mm_plain("a_dwin", h, dproj, 0, 0, F32, tn=FF_BLK, tk=2048)
    dx, dxb, d_nw = _mm_rms_bwd("a_dx", dproj, w_in, x, dy, nw, A_IN_PAD, out_scale)
    return dx, dxb, d_nw, d_win, d_wconv, dgp, d_outnw, d_wout, (arrived[0] if arrived else None)


def _swa_masks(n):
    qi = lax.broadcasted_iota(jnp.int32, (B_BLK, B_BLK), 0)
    kj = lax.broadcasted_iota(jnp.int32, (B_BLK, B_BLK), 1)
    return kj > qi + jnp.where(n > 0, 0, B_BLK), kj <= qi


def _swa_fwd(name, q, k, v, sinks):
    T = q.shape[1]
    tq = min(T, 1024)
    nbt = tq // B_BLK
    scale = B_HD ** -0.5
    G = B_HEADS // B_KV

    def body(q_ref, k_ref, v_ref, kh_ref, vh_ref, s_ref, o_ref, l_ref):
        first_blk = pl.program_id(1) * nbt

        def block(n, kp, vp):
            m_prev, m_cur = _swa_masks(first_blk + n)
            cur = pl.ds(pl.multiple_of(n * B_BLK, B_BLK), B_BLK)
            kc, vc = k_ref[0, cur, :], v_ref[0, cur, :]
            gs = range(G)
            rmax = lambda a: jnp.max(a, axis=1, keepdims=True)
            rsum = lambda a: jnp.sum(a, axis=1, keepdims=True)
            sink = [s_ref[g][:, 0:1] for g in gs]
            qb = [q_ref[g, cur, :] for g in gs]
            s_p = [jnp.where(m_prev, _dot(qb[g], kp, 1, 1) * scale, -jnp.inf) for g in gs]
            s_c = [jnp.where(m_cur, _dot(qb[g], kc, 1, 1) * scale, -jnp.inf) for g in gs]
            m = [jnp.maximum(jnp.maximum(rmax(s_p[g]), rmax(s_c[g])), sink[g]) for g in gs]
            p_p = [jnp.exp(s_p[g] - m[g]) for g in gs]
            p_c = [jnp.exp(s_c[g] - m[g]) for g in gs]
            den = [rsum(p_p[g]) + rsum(p_c[g]) + jnp.exp(sink[g] - m[g]) for g in gs]
            a_p = [_dotb(p_p[g], vp) for g in gs]
            a_c = [_dotb(p_c[g], vc) for g in gs]
            for g in gs:
                o_ref[g, cur, :] = ((a_p[g] + a_c[g]) / den[g]).astype(BF16)
                l_ref[g, cur, :] = m[g] + jnp.log(den[g])

        block(0, kh_ref[0], vh_ref[0])

        def rest(n, carry):
            prv = pl.ds(pl.multiple_of((n - 1) * B_BLK, B_BLK), B_BLK)
            block(n, k_ref[0, prv, :], v_ref[0, prv, :])
            return carry

        lax.fori_loop(1, nbt, rest, 0)

    qs = pl.BlockSpec((G, tq, B_HD), lambda kv, i: (kv, i, 0))
    ks = pl.BlockSpec((1, tq, B_HD), lambda kv, i: (kv, i, 0))
    halo = pl.BlockSpec((1, B_BLK, B_HD), lambda kv, i: (kv, jnp.maximum(i * nbt - 1, 0), 0))
    return pl.pallas_call(
        body, grid=(B_KV, T // tq),
        in_specs=[qs, ks, ks, halo, halo, pl.BlockSpec((G, 1, 128), lambda kv, i: (kv, 0, 0))],
        out_specs=[qs, pl.BlockSpec((G, tq, 1), lambda kv, i: (kv, i, 0))],
        out_shape=[jax.ShapeDtypeStruct((B_HEADS, T, B_HD), BF16), jax.ShapeDtypeStruct((B_HEADS, T, 1), F32)],
        name=name, compiler_params=_params(2))(q, k, v, k, v, sinks)


def _swa_bwd(name, q, k, v, sinks, o, lse, do):
    T = q.shape[1]
    tq = min(T, 1024)
    nbt, ni = tq // B_BLK, T // tq
    scale = B_HD ** -0.5
    G = B_HEADS // B_KV

    def body(q_ref, k_ref, v_ref, kh_ref, vh_ref, s_ref, o_ref, l_ref, do_ref, dq_ref, dk_ref, dv_ref, ds_ref,
             dk_halo, dv_halo):
        step = pl.program_id(1)
        first_blk = (ni - 1 - step) * nbt
        last = pl.ds(tq - B_BLK, B_BLK)
        dk_ref[...] = jnp.zeros_like(dk_ref)
        dv_ref[...] = jnp.zeros_like(dv_ref)

        @pl.when(step > 0)
        def _():
            dk_ref[0, last, :] = dk_halo[...]
            dv_ref[0, last, :] = dv_halo[...]

        def block(n, kp, vp, dsinks):
            m_prev, m_cur = _swa_masks(first_blk + n)
            cur = pl.ds(pl.multiple_of(n * B_BLK, B_BLK), B_BLK)
            kc, vc = k_ref[0, cur, :], v_ref[0, cur, :]
            gs = range(G)
            sink = [s_ref[g][:, 0:1] for g in gs]
            qb = [q_ref[g, cur, :] for g in gs]
            dob = [do_ref[g, cur, :] for g in gs]
            lse_b = [l_ref[g, cur, :] for g in gs]
            p_p = [jnp.where(m_prev, jnp.exp(_dot(qb[g], kp, 1, 1) * scale - lse_b[g]), 0.0) for g in gs]
            p_c = [jnp.where(m_cur, jnp.exp(_dot(qb[g], kc, 1, 1) * scale - lse_b[g]), 0.0) for g in gs]
            delta = [jnp.sum(dob[g].astype(F32) * o_ref[g, cur, :].astype(F32), axis=1, keepdims=True) for g in gs]
            ds_p = [p_p[g] * (_dot(dob[g], vp, 1, 1) - delta[g]) for g in gs]
            ds_c = [p_c[g] * (_dot(dob[g], vc, 1, 1) - delta[g]) for g in gs]
            dq_p = [_dotb(ds_p[g], kp) for g in gs]
            dq_c = [_dotb(ds_c[g], kc) for g in gs]
            dk_ps = [_dotb(ds_p[g], qb[g], 0, 0) for g in gs]
            dk_cs = [_dotb(ds_c[g], qb[g], 0, 0) for g in gs]
            dv_ps = [_dotb(p_p[g], dob[g], 0, 0) for g in gs]
            dv_cs = [_dotb(p_c[g], dob[g], 0, 0) for g in gs]
            for g in gs:
                dq_ref[g, cur, :] = (dq_p[g] + dq_c[g]) * scale
            out = tuple(dsinks[g] - jnp.sum(jnp.exp(sink[g] - lse_b[g]) * delta[g], axis=0, keepdims=True) for g in gs)
            total = lambda parts: (parts[0] + parts[1]) + (parts[2] + parts[3])
            dk_ref[0, cur, :] += total(dk_cs) * scale
            dv_ref[0, cur, :] += total(dv_cs)
            return total(dk_ps) * scale, total(dv_ps), out

        zeros = tuple(jnp.zeros((1, 1), F32) for _ in range(G))
        dk_p, dv_p, dsinks = block(0, kh_ref[0], vh_ref[0], zeros)
        dk_halo[...] = dk_p
        dv_halo[...] = dv_p

        def rest(n, dsinks):
            prv = pl.ds(pl.multiple_of((n - 1) * B_BLK, B_BLK), B_BLK)
            dk_p, dv_p, dsinks = block(n, k_ref[0, prv, :], v_ref[0, prv, :], dsinks)
            dk_ref[0, prv, :] += dk_p
            dv_ref[0, prv, :] += dv_p
            return dsinks

        dsinks = lax.fori_loop(1, nbt, rest, dsinks)
        for g in range(G):
            row = jnp.broadcast_to(dsinks[g], (1, 128))

            @pl.when(step == 0)
            def _():
                ds_ref[g] = row

            @pl.when(step > 0)
            def _():
                ds_ref[g] += row

    rev = lambda i: ni - 1 - i
    qs = pl.BlockSpec((G, tq, B_HD), lambda kv, i: (kv, rev(i), 0))
    ks = pl.BlockSpec((1, tq, B_HD), lambda kv, i: (kv, rev(i), 0))
    halo = pl.BlockSpec((1, B_BLK, B_HD), lambda kv, i: (kv, jnp.maximum(rev(i) * nbt - 1, 0), 0))
    ss = pl.BlockSpec((G, 1, 128), lambda kv, i: (kv, 0, 0))
    return pl.pallas_call(
        body, grid=(B_KV, ni),
        in_specs=[qs, ks, ks, halo, halo, ss, qs, pl.BlockSpec((G, tq, 1), lambda kv, i: (kv, rev(i), 0)), qs],
        out_specs=[qs, ks, ks, ss],
        out_shape=[jax.ShapeDtypeStruct((B_HEADS, T, B_HD), F32), jax.ShapeDtypeStruct((B_KV, T, B_HD), F32),
                   jax.ShapeDtypeStruct((B_KV, T, B_HD), F32), jax.ShapeDtypeStruct((B_HEADS, 1, 128), F32)],
        scratch_shapes=[pltpu.VMEM((B_BLK, B_HD), F32), pltpu.VMEM((B_BLK, B_HD), F32)],
        name=name, compiler_params=_params(2))(q, k, v, k, v, sinks, o, lse, do)


def _split_heads(a, n):
    T = a.shape[0]
    return a.reshape(T, n, B_HD).transpose(1, 0, 2)


def _merge_heads(a):
    n, T, _ = a.shape
    return a.transpose(1, 0, 2).reshape(T, n * B_HD)


def _swa_mixer_fwd(x, nw, w_in, b_in, sinks, w_out, b_out):
    h = _rms_fwd("b_rms", x, nw)
    proj = _mm_plain("b_proj", h, w_in, 1, 0, BF16, tn=768, bias=b_in)
    q, k, v = _split_heads(proj[:, :1024], B_HEADS), _split_heads(proj[:, 1024:1280], B_KV), _split_heads(proj[:, 1280:], B_KV)
    o, lse = _swa_fwd("b_attn", q, k, v, sinks)
    om = _merge_heads(o)
    y = _mm_residual("b_out", om, w_out, x, 1.0, bias=b_out)
    return y, (h, q, k, v, o, lse, om)


def _swa_mixer_bwd(dy, dyb, x, nw, w_in, sinks, w_out, saved, out_scale):
    h, q, k, v, o, lse, om = saved
    d_om = _mm_plain("b_dout", dyb, w_out, 1, 1, BF16)
    d_wout = _mm_plain("b_dwout", om, dyb, 0, 0, F32)
    d_bout = _colsum("b_dbout", dy)
    dq, dk, dv, dsinks = _swa_bwd("b_dattn", q, k, v, sinks, o, lse, _split_heads(d_om, B_HEADS))
    dproj = jnp.concatenate([_merge_heads(dq), _merge_heads(dk), _merge_heads(dv)], axis=1)
    d_bin = _colsum("b_dbin", dproj)
    d_win = _mm_plain("b_dwin", h, dproj, 0, 0, F32, tn=768)
    dx, dxb, d_nw = _mm_rms_bwd("b_dx", dproj, w_in, x, dy, nw, 1536, out_scale)
    return dx, dxb, d_nw, d_win, d_bin, dsinks[:, 0, 0], d_wout, d_bout


def _loss_head(name, x, tgt, fw, out_scale):
    T, D = x.shape
    tt = min(T, 512)

    def body(x_ref, t_ref, w_ref, dx_ref, dxb_ref, loss_ref, dw_ref):
        xv = x_ref[...]
        r = lax.rsqrt(jnp.mean(xv * xv, axis=-1, keepdims=True) + EPS)
        xhat = xv * r
        diff = xhat * w_ref[...] - t_ref[...]
        part = 0.5 * jnp.sum(jnp.mean(diff * diff, axis=-1, keepdims=True), axis=0, keepdims=True)
        dyv = diff * (1.0 / D)
        dxhat = dyv * w_ref[...]
        dx = r * (dxhat - xhat * jnp.mean(dxhat * xhat, axis=-1, keepdims=True))
        dx_ref[...] = dx
        dxb_ref[...] = (dx * out_scale).astype(BF16)
        dw = jnp.sum(dyv * xhat, axis=0, keepdims=True)
        lp = jnp.broadcast_to(part, (1, 128))

        @pl.when(pl.program_id(0) == 0)
        def _():
            loss_ref[...] = lp
            dw_ref[...] = dw

        @pl.when(pl.program_id(0) > 0)
        def _():
            loss_ref[...] += lp
            dw_ref[...] += dw

    row = pl.BlockSpec((tt, D), lambda i: (i, 0))
    return pl.pallas_call(
        body, grid=(T // tt,), in_specs=[row, row, pl.BlockSpec((1, D), lambda i: (0, 0))],
        out_specs=[row, row, pl.BlockSpec((1, 128), lambda i: (0, 0)), pl.BlockSpec((1, D), lambda i: (0, 0))],
        out_shape=[jax.ShapeDtypeStruct((T, D), F32), jax.ShapeDtypeStruct((T, D), BF16),
                   jax.ShapeDtypeStruct((1, 128), F32), jax.ShapeDtypeStruct((1, D), F32)],
        name=name, compiler_params=_params(1))(x, tgt, fw)


def _local_step(x, tgt, wts, comm=None):
    W = dict(wts)
    g = {}
    ga = W["ga"]
    a4 = comm["a4"] if comm else None
    n1, n2, nm = W["ffn1_norm"], W["ffn2_norm"], W["mix_norm"]
    h10, gu10, act10, *arrived = _ffn_gu("f10_gu", x, n1[0:1], ga, 0, (comm["big_b"], comm["small"]) if comm else None)
    if comm:
        W.update(comm["finish"](*arrived))
    wdn = W["w_down"]
    x1, sv1 = _ffn_down("f10_down", act10, wdn[0], x), (h10, gu10, act10)
    x2, sva, arrived = _gdn_fwd(x1, nm[0:1], W["a_w_in"], W["a_w_conv"], W["a_gate_p"], W["a_out_norm"], W["a_w_out"], a4)
    if a4 is not None:
        ga = _fill_a(1, a4, arrived, ga.reshape(4, 4, 2, A_HALF, FF_BLK)).reshape(ga.shape)
    x3, sv3 = _ffn_fwd("f20", x2, n2[0:1], ga, 2, wdn[2])
    x4, sv4 = _ffn_fwd("f11", x3, n1[1:2], ga, 1, wdn[1])
    x5, svb = _swa_mixer_fwd(x4, nm[1:2], W["b_w_in"], W["b_b_in"], W["b_sinks"], W["b_w_out"], W["b_b_out"])
    x6, sv6 = _ffn_fwd("f21", x5, n2[1:2], ga, 3, wdn[3])
    dx, dxb, loss_p, g["final_norm"] = _loss_head("loss_head", x6, tgt, W["final_norm"], 0.5)

    pa = jnp.zeros(ga.shape, BF16)
    dx, dxb, n21, pa, wd21 = _ffn_bwd("f21", dx, dxb, x5, n2[1:2], ga, 3, wdn[3], sv6, pa, 1.0)
    dx, dxb, nb, g["b_w_in"], g["b_b_in"], g["b_sinks"], g["b_w_out"], g["b_b_out"] = _swa_mixer_bwd(
        dx, dxb, x4, nm[1:2], W["b_w_in"], W["b_sinks"], W["b_w_out"], svb, 0.5)
    dx, dxb, n11, pa, wd11 = _ffn_bwd("f11", dx, dxb, x3, n1[1:2], ga, 1, wdn[1], sv4, pa, 0.5)
    dx, dxb, n20, pa, wd20 = _ffn_bwd("f20", dx, dxb, x2, n2[0:1], ga, 2, wdn[2], sv3, pa, 1.0)
    g["cs_early"] = _pair_sums_a("1", pa.reshape(4, 4, 2, A_HALF, FF_BLK), 1, 3) if a4 is not None else None
    dx, dxb, na, g["a_w_in"], g["a_w_conv"], g["a_gate_p"], g["a_out_norm"], g["a_w_out"], g["from_early"] = _gdn_bwd(
        dx, dxb, x1, nm[0:1], W["a_w_in"], W["a_w_conv"], W["a_gate_p"], W["a_out_norm"], W["a_w_out"], sva, 0.5,
        g["cs_early"])
    dgu10 = _ffn_dact("f10_dact", dxb, wdn[0], gu10)
    wd10 = _ffn_dwd("f10_dwd", act10, dxb)
    g["w_down"] = jnp.stack([wd10, wd11, wd20, wd21])
    if comm:
        p_b = comm["pack_b"](g)
        cs_b = _pair_sum("b", p_b, _pair_send("b", p_b))
        pa, from_b = _ffn_dwgu("f10_dwgu", h10, dgu10, pa, 0, cs_b)
        cs_late = _pair_sums_a("0", pa.reshape(4, 4, 2, A_HALF, FF_BLK), 0, 1)
        dx, dxb, n10, from_late = _ffn_dx("f10_dx", dgu10, ga, 0, x, dx, n1[0:1], 1.0, (cs_late,))
        g["late"] = ((cs_late, cs_b), (from_late, from_b))
    else:
        pa = _ffn_dwgu("f10_dwgu", h10, dgu10, pa, 0)
        dx, dxb, n10 = _ffn_dx("f10_dx", dgu10, ga, 0, x, dx, n1[0:1], 1.0)
        g["late"] = None
    g["ga"] = pa

    g["ffn1_norm"] = jnp.concatenate([n10, n11], axis=0)
    g["ffn2_norm"] = jnp.concatenate([n20, n21], axis=0)
    g["mix_norm"] = jnp.concatenate([na, nb], axis=0)
    return loss_p, dx, g


A_ROWS = 4 * D_MODEL
PACK = (("ffn1_w_down", 1408), ("ffn2_w_down", 1408), ("a_w_in", 1028), ("a_w_out", 256), ("b_w_in", 384),
        ("b_w_out", 256))
PACK_TILE = 16
PACK_USED = sum(-(-n // PACK_TILE) * PACK_TILE for _, n in PACK)
PACK_ROWS = 4864
assert PACK_USED <= PACK_ROWS
SMALL_SHARD = (8, 512)
MOVE_ROWS = {"a": 512, "b": 608}
SUM_ROWS = {"a": 256, "b": 304}


def _mesh_pos():
    x, y, c = lax.axis_index("x"), lax.axis_index("y"), lax.axis_index("c")
    return x, y, c, [(1 - x, y), (x, 1 - y), (1 - x, 1 - y)]


def _half(rows, c):
    return pl.ds(pl.multiple_of(c * (rows // 2), 16), rows // 2)


A_HALF = D_MODEL // 2


def _src_chip(j):
    x, y = lax.axis_index("x"), lax.axis_index("y")
    return jnp.where(j == 0, 2 * (1 - x) + y, jnp.where(j == 1, 2 * x + 1 - y, 2 * (1 - x) + 1 - y))


def _later_blocks_start(a4_ref, ra_ref, send_sems, recv_sems):
    x, y, c, chips = _mesh_pos()
    copies = [pltpu.make_async_remote_copy(src_ref=a4_ref.at[pl.ds(1, 3), c], dst_ref=ra_ref.at[j],
                                           send_sem=send_sems.at[j], recv_sem=recv_sems.at[j],
                                           device_id=(*chip, c), device_id_type=MESH)
              for j, chip in enumerate(chips)]
    for cp in copies:
        cp.start()
    return copies


def _fill_a(phase, a4, ra, ga=None):
    nb = 1 if phase == 0 else 3
    first = 0 if phase == 0 else 1
    steps = 3 * nb
    own_tiles = 2 * nb
    ra = ra.reshape(3, nb, A_HALF, FF_BLK)

    def body(*refs):
        if phase == 0:
            r_ref, own_ref, g_ref, send_sem, recv_sem, local_sems = refs
        else:
            r_ref, own_ref, _, g_ref, send_sem, recv_sem, local_sems = refs
        x, y, c, _ = _mesh_pos()
        s = pl.program_id(0)
        j, b = s // nb, s % nb
        dst = g_ref.at[_src_chip(j), first + b, c]
        keep = pltpu.make_async_copy(r_ref.at[0, 0], dst, local_sems.at[0])
        give = pltpu.make_async_remote_copy(src_ref=r_ref.at[0, 0], dst_ref=dst, send_sem=send_sem, recv_sem=recv_sem,
                                            device_id=(x, y, 1 - c), device_id_type=MESH)
        keep.start()
        give.start()

        @pl.when(s < own_tiles)
        def _():
            own = pltpu.make_async_copy(own_ref.at[0, 0], g_ref.at[2 * x + y, first + s // 2, s % 2], local_sems.at[1])
            own.start()
            own.wait()

        give.wait_send()
        keep.wait()

        @pl.when(s == steps - 1)
        def _():
            landed = g_ref.at[pl.ds(0, 3), pl.ds(0, nb), 0]
            pltpu.make_async_remote_copy(src_ref=landed, dst_ref=landed, send_sem=send_sem, recv_sem=recv_sem,
                                         device_id=(x, y, c), device_id_type=MESH).wait_recv()

    tile = (1, 1, A_HALF, FF_BLK)
    in_specs = [pl.BlockSpec(tile, lambda s: (s // nb, s % nb, 0, 0)),
                pl.BlockSpec(tile, lambda s: (first + jnp.minimum(s, own_tiles - 1) // 2, jnp.minimum(s, own_tiles - 1) % 2, 0, 0))]
    args = [ra, a4]
    if phase == 1:
        in_specs.append(HBM)
        args.append(ga)
    return pl.pallas_call(
        body, grid=(steps,), in_specs=in_specs, out_specs=HBM,
        out_shape=jax.ShapeDtypeStruct((4, 4, 2, A_HALF, FF_BLK), BF16),
        scratch_shapes=[pltpu.SemaphoreType.DMA, pltpu.SemaphoreType.DMA, pltpu.SemaphoreType.DMA((2,))],
        input_output_aliases={2: 0} if phase == 1 else {},
        name="fill_a%d" % phase, compiler_params=_params(1))(*args)


def _gather_chips(big_a4):
    def body(a_ref, ra_ref, send_sems, recv_sems):
        c = lax.axis_index("c")
        send = _peer_copies([(lambda chip: a_ref.at[0, c], ra_ref)], send_sems, recv_sems)
        for cp in send:
            cp.start()
        for cp in send:
            cp.wait_recv()
        for cp in send:
            cp.wait_send()

    return pl.pallas_call(
        body, name="gather_chips", in_specs=[HBM], out_specs=HBM,
        out_shape=jax.ShapeDtypeStruct((3, A_HALF, FF_BLK), BF16),
        scratch_shapes=[pltpu.SemaphoreType.DMA((3,)), pltpu.SemaphoreType.DMA((3,))])(big_a4)


def _gather_fill(tag, big, recv):
    rows_all, width = big.shape
    half, mv = rows_all // 2, MOVE_ROWS[tag]
    nt = half // mv
    own_tiles = rows_all // mv
    assert half % mv == 0 and own_tiles <= 3 * nt

    def body(recv_ref, big_ref, g_ref, send_sem, recv_sem, local_sems):
        x, y, c, chips = _mesh_pos()
        j, t = pl.program_id(0), pl.program_id(1)
        step = j * nt + t
        src_chip = jnp.where(j == 0, 2 * (1 - x) + y, jnp.where(j == 1, 2 * x + 1 - y, 2 * (1 - x) + 1 - y))
        rows = pl.ds(pl.multiple_of(c * half + t * mv, 16), mv)
        keep = pltpu.make_async_copy(recv_ref.at[0], g_ref.at[src_chip, rows], local_sems.at[0])
        give = pltpu.make_async_remote_copy(src_ref=recv_ref.at[0], dst_ref=g_ref.at[src_chip, rows],
                                            send_sem=send_sem, recv_sem=recv_sem,
                                            device_id=(x, y, 1 - c), device_id_type=MESH)
        keep.start()
        give.start()

        @pl.when(step < own_tiles)
        def _():
            own_rows = pl.ds(pl.multiple_of(step * mv, 16), mv)
            own = pltpu.make_async_copy(big_ref, g_ref.at[2 * x + y, own_rows], local_sems.at[1])
            own.start()
            own.wait()

        give.wait_send()
        keep.wait()

        @pl.when(step == 3 * nt - 1)
        def _():
            landed = g_ref.at[pl.ds(0, 3), pl.ds(0, half)]
            pltpu.make_async_remote_copy(src_ref=landed, dst_ref=landed, send_sem=send_sem, recv_sem=recv_sem,
                                         device_id=(x, y, c), device_id_type=MESH).wait_recv()

    return pl.pallas_call(
        body, grid=(3, nt),
        in_specs=[pl.BlockSpec((1, mv, width), lambda j, t: (j, t, 0)),
                  pl.BlockSpec((mv, width), lambda j, t: (jnp.minimum(j * nt + t, own_tiles - 1), 0))],
        out_specs=HBM, out_shape=jax.ShapeDtypeStruct((4, rows_all, width), BF16),
        scratch_shapes=[pltpu.SemaphoreType.DMA, pltpu.SemaphoreType.DMA, pltpu.SemaphoreType.DMA((2,))],
        name="gather_fill_" + tag, compiler_params=_params(2))(recv, big)


def _pair_send(tag, p):
    _, rows_all, width = p.shape
    half, mv = rows_all // 2, MOVE_ROWS[tag]
    nt = half // mv

    def body(p_ref, a_ref, send_sem, recv_sem):
        x, y, c, _ = _mesh_pos()
        s, t = pl.program_id(0), pl.program_id(1)
        rows = pl.ds(pl.multiple_of(t * mv, 16), mv)
        give = pltpu.make_async_remote_copy(src_ref=p_ref.at[0], dst_ref=a_ref.at[s, rows], send_sem=send_sem,
                                            recv_sem=recv_sem, device_id=(x, y, 1 - c), device_id_type=MESH)
        give.start()
        give.wait_send()

        @pl.when((s == 3) & (t == nt - 1))
        def _():
            pltpu.make_async_remote_copy(src_ref=a_ref, dst_ref=a_ref, send_sem=send_sem, recv_sem=recv_sem,
                                         device_id=(x, y, c), device_id_type=MESH).wait_recv()

    return pl.pallas_call(
        body, grid=(4, nt),
        in_specs=[pl.BlockSpec((1, mv, width), lambda s, t: (s, (1 - lax.axis_index("c")) * nt + t, 0))],
        out_specs=HBM, out_shape=jax.ShapeDtypeStruct((4, half, width), BF16),
        scratch_shapes=[pltpu.SemaphoreType.DMA, pltpu.SemaphoreType.DMA],
        name="pair_send_" + tag, compiler_params=_params(2))(p)


def _pair_sum(tag, p, a):
    _, half, width = a.shape
    sr = SUM_ROWS[tag]
    nt = half // sr
    assert half % sr == 0

    def body(p_ref, a_ref, o_ref):
        o_ref[...] = (p_ref[...].astype(F32) + a_ref[...].astype(F32)).astype(BF16)

    spec = pl.BlockSpec((1, sr, width), lambda s, t: (s, t, 0))
    return pl.pallas_call(
        body, grid=(4, nt),
        in_specs=[pl.BlockSpec((1, sr, width), lambda s, t: (s, lax.axis_index("c") * nt + t, 0)), spec],
        out_specs=spec, out_shape=jax.ShapeDtypeStruct((4, half, width), BF16),
        name="pair_sum_" + tag, compiler_params=_params(2))(p, a)


def _chip_exchange(cs_a, cs_b):
    def body(ca_ref, cb_ref, ba_ref, bb_ref, send_sems, recv_sems):
        x, y, c, chips = _mesh_pos()
        send = []
        for j, chip in enumerate(chips):
            for n, (src, dst) in enumerate(((ca_ref, ba_ref), (cb_ref, bb_ref))):
                send.append(pltpu.make_async_remote_copy(src_ref=src.at[2 * chip[0] + chip[1]], dst_ref=dst.at[j],
                                                         send_sem=send_sems.at[2 * j + n], recv_sem=recv_sems.at[2 * j + n],
                                                         device_id=(*chip, c), device_id_type=MESH))
        for cp in send:
            cp.start()
        for cp in send:
            cp.wait_recv()
        for cp in send:
            cp.wait_send()

    return pl.pallas_call(
        body, name="chip_exchange", in_specs=[HBM, HBM], out_specs=[HBM, HBM],
        out_shape=[jax.ShapeDtypeStruct((3,) + cs.shape[1:], BF16) for cs in (cs_a, cs_b)],
        scratch_shapes=[pltpu.SemaphoreType.DMA((6,)), pltpu.SemaphoreType.DMA((6,))])(cs_a, cs_b)


def _chip_sum(tag, cs, b):
    _, half, width = cs.shape
    sr = SUM_ROWS[tag]
    nt = half // sr

    def body(c_ref, b_ref, r_ref, buf, send_sems, recv_sem, local_sems):
        x, y, c, _ = _mesh_pos()
        t = pl.program_id(0)
        slot = lax.rem(t, 2)

        def copies(k, tile):
            rows = pl.ds(pl.multiple_of(c * half + tile * sr, 8), sr)
            keep = pltpu.make_async_copy(buf.at[k], r_ref.at[rows], local_sems.at[k])
            give = pltpu.make_async_remote_copy(src_ref=buf.at[k], dst_ref=r_ref.at[rows], send_sem=send_sems.at[k],
                                                recv_sem=recv_sem, device_id=(x, y, 1 - c), device_id_type=MESH)
            return keep, give

        @pl.when(t >= 2)
        def _():
            keep, give = copies(slot, t - 2)
            keep.wait()
            give.wait_send()

        buf[slot] = (c_ref[0].astype(F32) + b_ref[0].astype(F32)) + (b_ref[1].astype(F32) + b_ref[2].astype(F32))
        keep, give = copies(slot, t)
        keep.start()
        give.start()

        @pl.when(t == nt - 1)
        def _():
            for back in (1, 0):
                keep, give = copies(lax.rem(t - back, 2), t - back)
                keep.wait()
                give.wait_send()
            landed = r_ref.at[_half(2 * half, 1 - c)]
            pltpu.make_async_remote_copy(src_ref=landed, dst_ref=landed, send_sem=send_sems.at[0], recv_sem=recv_sem,
                                         device_id=(x, y, c), device_id_type=MESH).wait_recv()

    return pl.pallas_call(
        body, grid=(nt,),
        in_specs=[pl.BlockSpec((1, sr, width), lambda t: (2 * lax.axis_index("x") + lax.axis_index("y"), t, 0)),
                  pl.BlockSpec((3, sr, width), lambda t: (0, t, 0))],
        out_specs=HBM, out_shape=jax.ShapeDtypeStruct((2 * half, width), F32),
        scratch_shapes=[pltpu.VMEM((2, sr, width), F32), pltpu.SemaphoreType.DMA((2,)), pltpu.SemaphoreType.DMA,
                        pltpu.SemaphoreType.DMA((2,))],
        name="chip_sum_" + tag, compiler_params=_params(1))(cs, b)


def _pair_sums_a(tag, p5, b0, nb):
    tile5 = (1, 1, 1, A_HALF, FF_BLK)
    tile4 = (1, 1, A_HALF, FF_BLK)

    def send_body(p_ref, a_ref, send_sem, recv_sem):
        x, y, c, _ = _mesh_pos()
        s, t = pl.program_id(0), pl.program_id(1)
        give = pltpu.make_async_remote_copy(src_ref=p_ref.at[0, 0, 0], dst_ref=a_ref.at[s, t], send_sem=send_sem,
                                            recv_sem=recv_sem, device_id=(x, y, 1 - c), device_id_type=MESH)
        give.start()
        give.wait_send()

        @pl.when((s == 3) & (t == nb - 1))
        def _():
            pltpu.make_async_remote_copy(src_ref=a_ref, dst_ref=a_ref, send_sem=send_sem, recv_sem=recv_sem,
                                         device_id=(x, y, c), device_id_type=MESH).wait_recv()

    shape = jax.ShapeDtypeStruct((4, nb, A_HALF, FF_BLK), BF16)
    recv = pl.pallas_call(
        send_body, grid=(4, nb),
        in_specs=[pl.BlockSpec(tile5, lambda s, t: (s, b0 + t, 1 - lax.axis_index("c"), 0, 0))],
        out_specs=HBM, out_shape=shape, scratch_shapes=[pltpu.SemaphoreType.DMA, pltpu.SemaphoreType.DMA],
        name="pair_send_a" + tag, compiler_params=_params(2))(p5)

    def sum_body(p_ref, a_ref, o_ref):
        o_ref[0, 0] = (p_ref[0, 0, 0].astype(F32) + a_ref[0, 0].astype(F32)).astype(BF16)

    spec = pl.BlockSpec(tile4, lambda s, t: (s, t, 0, 0))
    return pl.pallas_call(
        sum_body, grid=(4, nb),
        in_specs=[pl.BlockSpec(tile5, lambda s, t: (s, b0 + t, lax.axis_index("c"), 0, 0)), spec],
        out_specs=spec, out_shape=shape, name="pair_sum_a" + tag, compiler_params=_params(2))(p5, recv)


def _chip_sum_a(tag, cs, frm, b0, r_prev=None):
    nb = cs.shape[1]
    sr = SUM_ROWS["a"]
    per = A_HALF // sr
    nt = nb * per

    def body(*refs):
        c_ref, b_ref = refs[:2]
        r_ref, buf, send_sems, recv_sem, local_sems = refs[-5:]
        x, y, c, _ = _mesh_pos()
        t = pl.program_id(0)
        slot = lax.rem(t, 2)

        def copies(k, tile):
            dst = r_ref.at[b0 + tile // per, c, pl.ds(pl.multiple_of(lax.rem(tile, per) * sr, 8), sr)]
            keep = pltpu.make_async_copy(buf.at[k], dst, local_sems.at[k])
            give = pltpu.make_async_remote_copy(src_ref=buf.at[k], dst_ref=dst, send_sem=send_sems.at[k],
                                                recv_sem=recv_sem, device_id=(x, y, 1 - c), device_id_type=MESH)
            return keep, give

        @pl.when(t >= 2)
        def _():
            keep, give = copies(slot, t - 2)
            keep.wait()
            give.wait_send()

        buf[slot] = ((c_ref[0, 0].astype(F32) + b_ref[0, 0].astype(F32))
                     + (b_ref[1, 0].astype(F32) + b_ref[2, 0].astype(F32)))
        keep, give = copies(slot, t)
        keep.start()
        give.start()

        @pl.when(t == nt - 1)
        def _():
            for back in (1, 0):
                keep, give = copies(lax.rem(t - back, 2), t - back)
                keep.wait()
                give.wait_send()
            landed = r_ref.at[pl.ds(b0, nb), 1 - c]
            pltpu.make_async_remote_copy(src_ref=landed, dst_ref=landed, send_sem=send_sems.at[0], recv_sem=recv_sem,
                                         device_id=(x, y, c), device_id_type=MESH).wait_recv()

    in_specs = [pl.BlockSpec((1, 1, sr, FF_BLK),
                             lambda t: (2 * lax.axis_index("x") + lax.axis_index("y"), t // per, t % per, 0)),
                pl.BlockSpec((3, 1, sr, FF_BLK), lambda t: (0, t // per, t % per, 0))]
    args = [cs, frm]
    if r_prev is not None:
        in_specs.append(HBM)
        args.append(r_prev)
    return pl.pallas_call(
        body, grid=(nt,), in_specs=in_specs, out_specs=HBM,
        out_shape=jax.ShapeDtypeStruct((4, 2, A_HALF, FF_BLK), F32),
        scratch_shapes=[pltpu.VMEM((2, sr, FF_BLK), F32), pltpu.SemaphoreType.DMA((2,)), pltpu.SemaphoreType.DMA,
                        pltpu.SemaphoreType.DMA((2,))],
        input_output_aliases={2: 0} if r_prev is not None else {},
        name="chip_sum_a" + tag, compiler_params=_params(1))(*args)


def _reduce_scatter(cs_early, from_early, late):
    (cs_late, cs_b), (from_late, from_b) = late
    red = _chip_sum_a("1", cs_early, from_early, 1)
    red = _chip_sum_a("0", cs_late, from_late, 0, red)
    return red.reshape(A_ROWS, FF_BLK), _chip_sum("b", cs_b, from_b)


SMALL_ROWS = 24


def _all_reduce_small(v):
    def body(v_ref, o_ref, all_ref, send_sems, recv_sems):
        x, y, c, _ = _mesh_pos()
        me = 4 * x + 2 * y + c
        all_ref[me] = v_ref[...]
        peers = [(x ^ ((k >> 2) & 1), y ^ ((k >> 1) & 1), c ^ (k & 1)) for k in range(1, 8)]
        idx = lambda p: 4 * p[0] + 2 * p[1] + p[2]
        send = [pltpu.make_async_remote_copy(src_ref=v_ref, dst_ref=all_ref.at[me], send_sem=send_sems.at[k],
                                             recv_sem=recv_sems.at[k], device_id=p, device_id_type=MESH)
                for k, p in enumerate(peers)]
        for cp in send:
            cp.start()
        for k, p in enumerate(peers):
            pltpu.make_async_remote_copy(src_ref=v_ref, dst_ref=all_ref.at[idx(p)], send_sem=send_sems.at[k],
                                         recv_sem=recv_sems.at[k], device_id=p, device_id_type=MESH).wait_recv()
        for cp in send:
            cp.wait_send()
        acc = all_ref[0]
        for d in range(1, 8):
            acc = acc + all_ref[d]
        o_ref[...] = acc

    vm = pl.BlockSpec(memory_space=pltpu.VMEM)
    return pl.pallas_call(
        body, name="all_reduce_small", in_specs=[vm], out_specs=vm,
        out_shape=jax.ShapeDtypeStruct((SMALL_ROWS, 1024), F32),
        scratch_shapes=[pltpu.VMEM((8, SMALL_ROWS, 1024), F32), pltpu.SemaphoreType.DMA((7,)),
                        pltpu.SemaphoreType.DMA((7,))],)(v)


def _adamw(name, w, g, m, v):
    rows, cols = w.shape
    tr = rows
    if rows * cols > 400_000:
        tr = max(t for t in range(8, rows, 8) if rows % t == 0 and t * cols <= 400_000)

    def body(w_ref, g_ref, m_ref, v_ref, d_ref, nm_ref, nv_ref):
        gv = g_ref[...]
        m_new = ADAM_B1 * m_ref[...] + (1.0 - ADAM_B1) * gv
        v_new = ADAM_B2 * v_ref[...] + (1.0 - ADAM_B2) * (gv * gv)
        m_hat = m_new / (1.0 - ADAM_B1 ** ADAM_STEP)
        v_hat = v_new / (1.0 - ADAM_B2 ** ADAM_STEP)
        d_ref[...] = -ADAM_LR * (m_hat / (jnp.sqrt(v_hat) + ADAM_EPS) + ADAM_WD * w_ref[...])
        nm_ref[...] = m_new
        nv_ref[...] = v_new

    spec = pl.BlockSpec((tr, cols), lambda i: (i, 0))
    sds = jax.ShapeDtypeStruct((rows, cols), F32)
    return pl.pallas_call(body, grid=(rows // tr,), in_specs=[spec] * 4, out_specs=[spec] * 3, out_shape=[sds] * 3,
                          name=name, compiler_params=_params(1))(w, g, m, v)


WEIGHTS = ("ffn1_norm", "ffn1_w_gu", "ffn1_w_down", "mix_norm", "ffn2_norm", "ffn2_w_gu", "ffn2_w_down",
           "a_w_in", "a_w_conv", "a_A_log", "a_dt_bias", "a_out_norm", "a_w_out",
           "b_w_in", "b_b_in", "b_sinks", "b_w_out", "b_b_out", "final_norm")
SMALL_SLOTS = {"ffn1_norm": (0, 2048), "mix_norm": (2048, 2048), "ffn2_norm": (4096, 2048), "final_norm": (6144, 1024),
               "a_A_log": (7168, 8), "a_dt_bias": (7296, 8), "a_out_norm": (7424, 128), "b_sinks": (7552, 16),
               "loss": (7680, 1)}
SMALL_SHARDED = {"a_w_conv": (8192, 8192, (4,), 3072), "b_b_in": (20480, 11264, (), 1536), "b_b_out": (22016, 11648, (), 1024)}
DEV_SMALL_ROWS = 12


def _pack_rows(parts):
    rows = []
    for p in parts:
        r = p.reshape(p.shape[0], -1, 1024)
        rows.append(jnp.pad(r, ((0, 0), (0, -r.shape[1] % PACK_TILE), (0, 0))))
    rows.append(jnp.zeros((parts[0].shape[0], PACK_ROWS - PACK_USED, 1024), parts[0].dtype))
    return jnp.concatenate(rows, axis=1)


def _place(vec, off, a):
    return lax.dynamic_update_slice(vec, a.reshape(-1).astype(F32), (off,))


def kernel(x, ffn1_norm, ffn1_w_gu, ffn1_w_down, mix_norm, ffn2_norm, ffn2_w_gu, ffn2_w_down, a_w_in, a_w_conv, a_A_log, a_dt_bias, a_out_norm, a_w_out, b_w_in, b_b_in, b_sinks, b_w_out, b_b_out, final_norm, loss_target, m_ffn1_norm, m_ffn1_w_gu, m_ffn1_w_down, m_mix_norm, m_ffn2_norm, m_ffn2_w_gu, m_ffn2_w_down, m_a_w_in, m_a_w_conv, m_a_A_log, m_a_dt_bias, m_a_out_norm, m_a_w_out, m_b_w_in, m_b_b_in, m_b_sinks, m_b_w_out, m_b_b_out, m_final_norm, v_ffn1_norm, v_ffn1_w_gu, v_ffn1_w_down, v_mix_norm, v_ffn2_norm, v_ffn2_w_gu, v_ffn2_w_down, v_a_w_in, v_a_w_conv, v_a_A_log, v_a_dt_bias, v_a_out_norm, v_a_w_out, v_b_w_in, v_b_b_in, v_b_sinks, v_b_w_out, v_b_b_out, v_final_norm):
    w = dict(zip(WEIGHTS, (ffn1_norm, ffn1_w_gu, ffn1_w_down, mix_norm, ffn2_norm, ffn2_w_gu, ffn2_w_down, a_w_in, a_w_conv,
                           a_A_log, a_dt_bias, a_out_norm, a_w_out, b_w_in, b_b_in, b_sinks, b_w_out, b_b_out, final_norm)))
    m = dict(zip(WEIGHTS, (m_ffn1_norm, m_ffn1_w_gu, m_ffn1_w_down, m_mix_norm, m_ffn2_norm, m_ffn2_w_gu, m_ffn2_w_down,
                           m_a_w_in, m_a_w_conv, m_a_A_log, m_a_dt_bias, m_a_out_norm, m_a_w_out, m_b_w_in, m_b_b_in,
                           m_b_sinks, m_b_w_out, m_b_b_out, m_final_norm)))
    v = dict(zip(WEIGHTS, (v_ffn1_norm, v_ffn1_w_gu, v_ffn1_w_down, v_mix_norm, v_ffn2_norm, v_ffn2_w_gu, v_ffn2_w_down,
                           v_a_w_in, v_a_w_conv, v_a_A_log, v_a_dt_bias, v_a_out_norm, v_a_w_out, v_b_w_in, v_b_b_in,
                           v_b_sinks, v_b_w_out, v_b_b_out, v_final_norm)))
    chip = 2 * lax.axis_index("x") + lax.axis_index("y")

    big_a4 = jnp.concatenate([w["ffn1_w_gu"], w["ffn2_w_gu"]], axis=0).astype(BF16).reshape(4, 2, A_HALF, FF_BLK)
    big_b = _pack_rows([w[n].astype(BF16).reshape(1, -1) for n, _ in PACK])[0]
    small = jnp.zeros((4096,), F32)
    small = _place(small, 0, w["a_w_conv"])
    small = _place(small, 3072, w["b_b_in"])
    small = _place(small, 3456, w["b_b_out"]).reshape(SMALL_SHARD)
    ga = _fill_a(0, big_a4, _gather_chips(big_a4)).reshape(4, 4, D_MODEL, FF_BLK)
    offs, o = {}, 0
    for n, r in PACK:
        offs[n] = (o, r)
        o += -(-r // PACK_TILE) * PACK_TILE

    def finish_weights(rb, rs):
        gb = _gather_fill("b", big_b, rb)
        blk = lambda n: gb[:, offs[n][0]:offs[n][0] + offs[n][1]]
        gsf = lax.dynamic_update_slice(jnp.zeros((4, 4096), F32), small.reshape(1, 4096), (chip, 0))
        for j, other in enumerate((chip ^ 2, chip ^ 1, chip ^ 3)):
            gsf = lax.dynamic_update_slice(gsf, rs[j].reshape(1, 4096), (other, 0))
        return {
            "w_down": gb[:, 0:2816].reshape(4, 4, 704, 1024).transpose(1, 0, 2, 3).reshape(4, 2, FF_BLK, 1024),
            "a_w_in": jnp.pad(blk("a_w_in").reshape(4, 1024, 1028).transpose(1, 0, 2).reshape(1024, A_IN_COLS),
                              ((0, 0), (0, A_IN_PAD - A_IN_COLS))),
            "a_w_out": blk("a_w_out").reshape(1024, 1024),
            "b_w_in": blk("b_w_in").reshape(4, 1024, 384).transpose(1, 0, 2).reshape(1024, 1536),
            "b_w_out": blk("b_w_out").reshape(1024, 1024),
            "a_w_conv": gsf[:, 0:3072].reshape(4, 4, 768).transpose(1, 0, 2).reshape(4, 3072),
            "b_b_in": gsf[:, 3072:3456].reshape(1, 1536),
            "b_b_out": gsf[:, 3456:3712].reshape(1, 1024)}

    def pack_b(g):
        down = g["w_down"].reshape(4, 4, 704, 1024).transpose(1, 0, 2, 3)
        parts = [down[:, 0:2], down[:, 2:4],
                 g["a_w_in"][:, :A_IN_COLS].reshape(1024, 4, 1028).transpose(1, 0, 2), g["a_w_out"].reshape(4, 256, 1024),
                 g["b_w_in"].reshape(1024, 4, 384).transpose(1, 0, 2), g["b_w_out"].reshape(4, 256, 1024)]
        return _pack_rows([a.astype(BF16).reshape(4, -1) for a in parts])

    W = {n: w[n] for n in ("ffn1_norm", "ffn2_norm", "mix_norm", "a_out_norm")}
    W["ga"] = ga
    W["a_gate_p"] = jnp.pad(jnp.concatenate([-jnp.exp(w["a_A_log"]), w["a_dt_bias"]], axis=0), ((0, 0), (8, 112)))
    W["b_sinks"] = jnp.broadcast_to(w["b_sinks"][0][:, None, None], (B_HEADS, 1, 128))
    W["final_norm"] = w["final_norm"][None]

    comm = {"a4": big_a4, "big_b": big_b, "small": small, "finish": finish_weights, "pack_b": pack_b}
    loss_p, dx, g = _local_step(x[0], loss_target[0], W, comm)

    red_a, red_b = _reduce_scatter(g["cs_early"], g["from_early"], g["late"])
    grads = {n: red_b[offs[n][0]:offs[n][0] + offs[n][1]].reshape(w[n].shape) for n, _ in PACK}
    grads["ffn1_w_gu"] = red_a[:A_ROWS // 2].reshape(w["ffn1_w_gu"].shape)
    grads["ffn2_w_gu"] = red_a[A_ROWS // 2:].reshape(w["ffn2_w_gu"].shape)

    sv = jnp.zeros((SMALL_ROWS * 1024,), F32)
    small_g = {"ffn1_norm": g["ffn1_norm"], "mix_norm": g["mix_norm"], "ffn2_norm": g["ffn2_norm"], "final_norm": g["final_norm"],
               "a_A_log": g["a_gate_p"][0, 8:16], "a_dt_bias": g["a_gate_p"][1, 8:16], "a_out_norm": g["a_out_norm"],
               "b_sinks": g["b_sinks"], "loss": loss_p[0, 0:1]}
    for n, (off, _) in SMALL_SLOTS.items():
        sv = _place(sv, off, small_g[n])
    for n, (off, _, _, _) in SMALL_SHARDED.items():
        sv = _place(sv, off, g[n])
    tot = _all_reduce_small(sv.reshape(SMALL_ROWS, 1024)).reshape(-1)
    for n, (off, size) in SMALL_SLOTS.items():
        if n != "loss":
            grads[n] = tot[off:off + size].reshape(w[n].shape)
    for n, (off, _, lead, last) in SMALL_SHARDED.items():
        full = tot[off:off + (lead[0] if lead else 1) * last].reshape(lead + (last,))
        width = last // 4
        grads[n] = lax.dynamic_slice_in_dim(full, chip * width, width, axis=-1).reshape(w[n].shape)
    loss = tot[SMALL_SLOTS["loss"][0]]

    delta, new_m, new_v = {}, {}, {}
    for n in ("ffn1_w_gu", "ffn2_w_gu") + tuple(n for n, _ in PACK):
        two_d = lambda a: a.reshape(-1, a.shape[-1])
        d, nm, nv = _adamw("adamw_" + n, two_d(w[n]), two_d(grads[n]), two_d(m[n]), two_d(v[n]))
        delta[n], new_m[n], new_v[n] = d.reshape(w[n].shape), nm.reshape(w[n].shape), nv.reshape(w[n].shape)

    def dev_small(src):
        vec = jnp.zeros((DEV_SMALL_ROWS * 1024,), F32)
        for n, (off, _) in SMALL_SLOTS.items():
            if n != "loss":
                vec = _place(vec, off, src[n])
        for n, (_, off, _, _) in SMALL_SHARDED.items():
            vec = _place(vec, off, src[n])
        return vec.reshape(DEV_SMALL_ROWS, 1024)

    sd, sm, svv = _adamw("adamw_small", dev_small(w), dev_small(grads), dev_small(m), dev_small(v))
    for n in WEIGHTS:
        if n in SMALL_SLOTS:
            off, size = SMALL_SLOTS[n]
        elif n in SMALL_SHARDED:
            off, size = SMALL_SHARDED[n][1], w[n].size
        else:
            continue
        for dst, src in ((delta, sd), (new_m, sm), (new_v, svv)):
            dst[n] = src.reshape(-1)[off:off + size].reshape(w[n].shape)

    return (loss, dx[None], *[grads[n] for n in WEIGHTS], *[delta[n] for n in WEIGHTS],
            *[new_m[n] for n in WEIGHTS], *[new_v[n] for n in WEIGHTS])
```

```python
import jax
import jax.numpy as jnp
from jax import lax
from jax.experimental import pallas as pl
from jax.experimental.pallas import tpu as pltpu

F32 = jnp.float32
BF16 = jnp.bfloat16

D_MODEL = 1024
EPS = 1e-6
FF_BLK = 1408
A_HEADS = 8
A_DK = 128
A_CHUNK = 64
A_HG = 8
A_IN_COLS = 4112
A_IN_PAD = 4224
B_HEADS = 16
B_KV = 4
B_HD = 64
B_BLK = 128
ADAM_LR, ADAM_B1, ADAM_B2, ADAM_EPS, ADAM_WD, ADAM_STEP = 0.001, 0.9, 0.999, 1e-08, 0.01, 10
MESH = pl.DeviceIdType.MESH
VMEM_LIMIT = 56 * 1024 * 1024
HBM = pl.BlockSpec(memory_space=pl.ANY)


def _params(n_axes):
    return pltpu.CompilerParams(dimension_semantics=("arbitrary",) * n_axes, vmem_limit_bytes=VMEM_LIMIT)


def _sigmoid(x):
    return 1.0 / (1.0 + jnp.exp(-x))


def _dot(a, b, ca, cb):
    return lax.dot_general(a, b, (((ca,), (cb,)), ((), ())), preferred_element_type=F32)


def _dotb(a, b, ca=1, cb=0):
    return _dot(a.astype(BF16), b.astype(BF16), ca, cb)


def _dotx(a, b, ca=1, cb=0):
    return lax.dot_general(a, b, (((ca,), (cb,)), ((), ())), preferred_element_type=F32,
                           precision=lax.Precision.HIGHEST)


def _doth(a, b, ca=1, cb=0):
    return lax.dot_general(a, b, (((ca,), (cb,)), ((), ())), preferred_element_type=F32,
                           precision=lax.Precision.HIGH)


def _rms_fwd(name, x, w):
    T, D = x.shape
    tt = min(T, 512)

    def body(x_ref, w_ref, h_ref):
        xv = x_ref[...]
        r = lax.rsqrt(jnp.mean(xv * xv, axis=-1, keepdims=True) + EPS)
        h_ref[...] = (xv * r * w_ref[...]).astype(BF16)

    return pl.pallas_call(
        body, grid=(T // tt,),
        in_specs=[pl.BlockSpec((tt, D), lambda i: (i, 0)), pl.BlockSpec((1, D), lambda i: (0, 0))],
        out_specs=pl.BlockSpec((tt, D), lambda i: (i, 0)),
        out_shape=jax.ShapeDtypeStruct((T, D), BF16), name=name, compiler_params=_params(1))(x, w)


def _rms_bwd_tile(dh, xv, dy, w):
    r = lax.rsqrt(jnp.mean(xv * xv, axis=-1, keepdims=True) + EPS)
    xhat = xv * r
    dxhat = dh * w
    dx = dy + r * (dxhat - xhat * jnp.mean(dxhat * xhat, axis=-1, keepdims=True))
    return dx, jnp.sum(dh * xhat, axis=0, keepdims=True)


def _colsum(name, a):
    T, N = a.shape
    tt = min(T, 512)

    def body(a_ref, o_ref):
        @pl.when(pl.program_id(0) == 0)
        def _():
            o_ref[...] = jnp.zeros_like(o_ref)
        o_ref[...] += jnp.sum(a_ref[...].astype(F32), axis=0, keepdims=True)

    return pl.pallas_call(
        body, grid=(T // tt,), in_specs=[pl.BlockSpec((tt, N), lambda i: (i, 0))],
        out_specs=pl.BlockSpec((1, N), lambda i: (0, 0)),
        out_shape=jax.ShapeDtypeStruct((1, N), F32), name=name, compiler_params=_params(1))(a)


def _matmul(name, a, b, ca, cb, tm, tn, tk, extra_in, outs, epi, order="ji"):
    M, K, N = a.shape[1 - ca], a.shape[ca], b.shape[1 - cb]
    tm, tn, tk = min(tm, M), min(tn, N), min(tk, K)
    assert M % tm == 0 and N % tn == 0 and K % tk == 0, (name, M, N, K, tm, tn, tk)
    ni, nj, nk = M // tm, N // tn, K // tk
    if order == "ji":
        grid = (nj, ni, nk)
        perm = lambda g0, g1, g2: (g1, g0, g2)
    else:
        grid = (ni, nj, nk)
        perm = lambda g0, g1, g2: (g0, g1, g2)

    def wrap(f):
        return lambda g0, g1, g2: f(*perm(g0, g1, g2))

    a_spec = (pl.BlockSpec((tm, tk), wrap(lambda i, j, k: (i, k))) if ca == 1
              else pl.BlockSpec((tk, tm), wrap(lambda i, j, k: (k, i))))
    b_spec = (pl.BlockSpec((tk, tn), wrap(lambda i, j, k: (k, j))) if cb == 0
              else pl.BlockSpec((tn, tk), wrap(lambda i, j, k: (j, k))))
    ne, no = len(extra_in), len(outs)

    def body(*refs):
        a_ref, b_ref = refs[0], refs[1]
        ex, out = refs[2:2 + ne], refs[2 + ne:2 + ne + no]
        i, j, k = perm(pl.program_id(0), pl.program_id(1), pl.program_id(2))
        p = _dotb(a_ref[...], b_ref[...], ca, cb)
        if nk == 1:
            epi(p, ex, out, i, j)
        else:
            acc_ref = refs[-1]

            @pl.when(k == 0)
            def _():
                acc_ref[...] = p

            @pl.when(k > 0)
            def _():
                acc_ref[...] += p

            @pl.when(k == nk - 1)
            def _():
                epi(acc_ref[...], ex, out, i, j)

    return pl.pallas_call(
        body, grid=grid,
        in_specs=[a_spec, b_spec] + [pl.BlockSpec(bs, wrap(f)) for _, bs, f in extra_in],
        out_specs=[pl.BlockSpec(bs, wrap(f)) for _, bs, f in outs],
        out_shape=[s for s, _, _ in outs],
        scratch_shapes=[pltpu.VMEM((tm, tn), F32)] if nk > 1 else [],
        name=name, compiler_params=_params(3))(a, b, *[x for x, _, _ in extra_in])


def _mm_plain(name, a, b, ca, cb, out_dtype, tm=1024, tn=1024, tk=1024, scale=1.0, bias=None):
    M, N = a.shape[1 - ca], b.shape[1 - cb]
    tm, tn = min(tm, M), min(tn, N)
    extra = [] if bias is None else [(bias, (1, tn), lambda i, j, k: (0, j))]

    def epi(acc, ex, out, i, j):
        r = acc * scale if scale != 1.0 else acc
        if bias is not None:
            r = r + ex[0][...]
        out[0][...] = r.astype(out_dtype)

    return _matmul(name, a, b, ca, cb, tm, tn, tk, extra,
                   [(jax.ShapeDtypeStruct((M, N), out_dtype), (tm, tn), lambda i, j, k: (i, j))], epi)[0]


def _mm_residual(name, a, b, x, scale, bias=None, tk=1024):
    M, N = x.shape
    tm, tn = min(512, M), N
    extra = [(x, (tm, tn), lambda i, j, k: (i, j))]
    if bias is not None:
        extra.append((bias, (1, tn), lambda i, j, k: (0, j)))

    def epi(acc, ex, out, i, j):
        r = acc if bias is None else acc + ex[1][...]
        out[0][...] = ex[0][...] + scale * r

    return _matmul(name, a, b, 1, 0, tm, tn, tk, extra,
                   [(jax.ShapeDtypeStruct((M, N), F32), (tm, tn), lambda i, j, k: (i, j))], epi, order="ij")[0]


def _mm_rms_bwd(name, dproj, w_in, x, dy, nw, tk, out_scale):
    M, N = x.shape
    tm = min(512, M)
    extra = [(x, (tm, N), lambda i, j, k: (i, 0)), (dy, (tm, N), lambda i, j, k: (i, 0)),
             (nw, (1, N), lambda i, j, k: (0, 0))]

    def epi(acc, ex, out, i, j):
        dx, dw = _rms_bwd_tile(acc, ex[0][...], ex[1][...], ex[2][...])
        out[0][...] = dx
        out[1][...] = (dx * out_scale).astype(BF16)

        @pl.when(i == 0)
        def _():
            out[2][...] = dw

        @pl.when(i > 0)
        def _():
            out[2][...] += dw

    return _matmul(name, dproj, w_in, 1, 1, tm, N, tk, extra,
                   [(jax.ShapeDtypeStruct((M, N), F32), (tm, N), lambda i, j, k: (i, 0)),
                    (jax.ShapeDtypeStruct((M, N), BF16), (tm, N), lambda i, j, k: (i, 0)),
                    (jax.ShapeDtypeStruct((1, N), F32), (1, N), lambda i, j, k: (0, 0))], epi, order="ij")


def _peer_copies(pairs, send_sems, recv_sems):
    x, y, c, chips = _mesh_pos()
    n = len(pairs)
    return [pltpu.make_async_remote_copy(src_ref=src(chip), dst_ref=dst.at[j], send_sem=send_sems.at[n * j + k],
                                         recv_sem=recv_sems.at[n * j + k], device_id=(*chip, c), device_id_type=MESH)
            for j, chip in enumerate(chips) for k, (src, dst) in enumerate(pairs)]


def _ffn_gu(name, x, nw, ga, blk, exchange=None):
    T, D = x.shape
    tm = min(T, 512)
    rs = min(tm, 256)
    ni = T // tm

    def body(*refs):
        x_ref, nw_ref, wg0, wg1, wu0, wu1 = refs[:6]
        if exchange is None:
            h_ref, gu_ref, act_ref = refs[6:]
        else:
            bb_ref, sm_ref, h_ref, gu_ref, act_ref, rb_ref, rs_ref, send_sems, recv_sems = refs[6:]
            c = lax.axis_index("c")
            pairs = [(lambda chip: bb_ref.at[_half(PACK_ROWS, c)], rb_ref), (lambda chip: sm_ref, rs_ref)]

            @pl.when(pl.program_id(0) == 0)
            def _():
                for cp in _peer_copies(pairs, send_sems, recv_sems):
                    cp.start()

        for r in range(tm // rs):
            rows = pl.ds(r * rs, rs)
            xv = x_ref[rows, :]
            hv = (xv * lax.rsqrt(jnp.mean(xv * xv, axis=-1, keepdims=True) + EPS) * nw_ref[...]).astype(BF16)
            h_ref[rows, :] = hv
            for j, (wg_ref, wu_ref) in enumerate(((wg0, wu0), (wg1, wu1))):
                g = _dot(hv, wg_ref[0, 0], 1, 0)
                u = _dot(hv, wu_ref[0, 0], 1, 0)
                s = _sigmoid(g)
                gs = g * s
                gu_ref[0, j, rows, :] = (u * (s + gs * (1.0 - s))).astype(BF16)
                gu_ref[1, j, rows, :] = gs.astype(BF16)
                act_ref[j, rows, :] = (gs * u).astype(BF16)

        if exchange is not None:
            @pl.when(pl.program_id(0) == ni - 1)
            def _():
                for cp in _peer_copies(pairs, send_sems, recv_sems):
                    cp.wait_recv()
                    cp.wait_send()

    wspec = lambda q: pl.BlockSpec((1, 1, D, FF_BLK), lambda i: (q, blk, 0, 0), pipeline_mode=pl.Buffered(1))
    in_specs = [pl.BlockSpec((tm, D), lambda i: (i, 0)), pl.BlockSpec((1, D), lambda i: (0, 0)),
                wspec(0), wspec(1), wspec(2), wspec(3)]
    out_specs = [pl.BlockSpec((tm, D), lambda i: (i, 0)), pl.BlockSpec((2, 2, tm, FF_BLK), lambda i: (0, 0, i, 0)),
                 pl.BlockSpec((2, tm, FF_BLK), lambda i: (0, i, 0))]
    out_shape = [jax.ShapeDtypeStruct((T, D), BF16), jax.ShapeDtypeStruct((2, 2, T, FF_BLK), BF16),
                 jax.ShapeDtypeStruct((2, T, FF_BLK), BF16)]
    args, scratch = [x, nw, ga, ga, ga, ga], []
    if exchange is not None:
        big_b, small = exchange
        in_specs += [HBM, HBM]
        out_specs += [HBM, HBM]
        out_shape += [jax.ShapeDtypeStruct((3, big_b.shape[0] // 2, big_b.shape[1]), BF16),
                      jax.ShapeDtypeStruct((3,) + small.shape, F32)]
        scratch = [pltpu.SemaphoreType.DMA((6,)), pltpu.SemaphoreType.DMA((6,))]
        args += [big_b, small]
    return pl.pallas_call(body, grid=(ni,), in_specs=in_specs, out_specs=out_specs, out_shape=out_shape,
                          scratch_shapes=scratch, name=name, compiler_params=_params(1))(*args)


def _ffn_down(name, act, wd, x):
    T, D = x.shape
    tm = min(T, 512)

    def body(act_ref, wd_ref, x_ref, o_ref):
        acc = _dot(act_ref[0], wd_ref[0], 1, 0) + _dot(act_ref[1], wd_ref[1], 1, 0)
        o_ref[...] = x_ref[...] + 0.5 * acc

    return pl.pallas_call(
        body, grid=(T // tm,),
        in_specs=[pl.BlockSpec((2, tm, FF_BLK), lambda i: (0, i, 0)),
                  pl.BlockSpec((2, FF_BLK, D), lambda i: (0, 0, 0)),
                  pl.BlockSpec((tm, D), lambda i: (i, 0))],
        out_specs=pl.BlockSpec((tm, D), lambda i: (i, 0)),
        out_shape=jax.ShapeDtypeStruct((T, D), F32), name=name, compiler_params=_params(1))(act, wd, x)


def _ffn_dact(name, dyh, wd, gu):
    T, D = dyh.shape
    tm = min(T, 1024)
    rs = min(tm, 256)

    def body(dy_ref, wd_ref, gu_ref, o_ref):
        for r in range(tm // rs):
            rows = pl.ds(r * rs, rs)
            dact = _dot(dy_ref[rows, :], wd_ref[0], 1, 1)
            o_ref[0, 0, rows, :] = (dact * gu_ref[0, 0, rows, :].astype(F32)).astype(BF16)
            o_ref[1, 0, rows, :] = (dact * gu_ref[1, 0, rows, :].astype(F32)).astype(BF16)

    return pl.pallas_call(
        body, grid=(2, T // tm),
        in_specs=[pl.BlockSpec((tm, D), lambda j, i: (i, 0)),
                  pl.BlockSpec((1, FF_BLK, D), lambda j, i: (j, 0, 0)),
                  pl.BlockSpec((2, 1, tm, FF_BLK), lambda j, i: (0, j, i, 0))],
        out_specs=pl.BlockSpec((2, 1, tm, FF_BLK), lambda j, i: (0, j, i, 0)),
        out_shape=jax.ShapeDtypeStruct((2, 2, T, FF_BLK), BF16), name=name, compiler_params=_params(2))(dyh, wd, gu)


def _ffn_dwd(name, act, dyh):
    _, T, _ = act.shape
    D = dyh.shape[1]
    tk = min(T, 2048)
    nk = T // tk

    def body(a_ref, d_ref, o_ref, acc_ref):
        k = pl.program_id(1)
        p = _dot(a_ref[0], d_ref[...], 0, 0)

        @pl.when(k == 0)
        def _():
            acc_ref[...] = p

        @pl.when(k > 0)
        def _():
            acc_ref[...] += p

        @pl.when(k == nk - 1)
        def _():
            o_ref[0] = acc_ref[...].astype(BF16)

    return pl.pallas_call(
        body, grid=(2, nk),
        in_specs=[pl.BlockSpec((1, tk, FF_BLK), lambda j, k: (j, k, 0)), pl.BlockSpec((tk, D), lambda j, k: (k, 0))],
        out_specs=pl.BlockSpec((1, FF_BLK, D), lambda j, k: (j, 0, 0)),
        out_shape=jax.ShapeDtypeStruct((2, FF_BLK, D), BF16), scratch_shapes=[pltpu.VMEM((FF_BLK, D), F32)],
        name=name, compiler_params=_params(2))(act, dyh)


def _ffn_dwgu(name, h, dgu, pa, blk, exchange=None):
    T, D = h.shape
    tk = min(T, 2048)
    nk = T // tk

    def body(*refs):
        h_ref, d_ref, pa_in = refs[:3]
        if exchange is None:
            o_ref, acc_ref = refs[3:]
        else:
            cs_ref, o_ref, f_ref, acc_ref, send_sems, recv_sems = refs[3:]
            pairs = [(lambda chip: cs_ref.at[2 * chip[0] + chip[1]], f_ref)]

            @pl.when((pl.program_id(0) == 0) & (pl.program_id(1) == 0))
            def _():
                for cp in _peer_copies(pairs, send_sems, recv_sems):
                    cp.start()

        k = pl.program_id(1)
        p = _dot(h_ref[...], d_ref[0, 0], 0, 0)

        @pl.when(k == 0)
        def _():
            acc_ref[...] = p

        @pl.when(k > 0)
        def _():
            acc_ref[...] += p

        @pl.when(k == nk - 1)
        def _():
            o_ref[0, 0] = acc_ref[...].astype(BF16)

        if exchange is not None:
            @pl.when((pl.program_id(0) == 3) & (pl.program_id(1) == nk - 1))
            def _():
                for cp in _peer_copies(pairs, send_sems, recv_sems):
                    cp.wait_recv()
                    cp.wait_send()

    in_specs = [pl.BlockSpec((tk, D), lambda q, k: (k, 0)),
                pl.BlockSpec((1, 1, tk, FF_BLK), lambda q, k: (q // 2, q % 2, k, 0)), HBM]
    out_specs = [pl.BlockSpec((1, 1, D, FF_BLK), lambda q, k: (q, blk, 0, 0))]
    out_shape = [jax.ShapeDtypeStruct(pa.shape, BF16)]
    scratch, args = [pltpu.VMEM((D, FF_BLK), F32)], [h, dgu, pa]
    if exchange is not None:
        in_specs.append(HBM)
        out_specs.append(HBM)
        out_shape.append(jax.ShapeDtypeStruct((3,) + exchange.shape[1:], BF16))
        scratch += [pltpu.SemaphoreType.DMA((3,)), pltpu.SemaphoreType.DMA((3,))]
        args.append(exchange)
    out = pl.pallas_call(body, grid=(4, nk), in_specs=in_specs, out_specs=out_specs, out_shape=out_shape,
                         scratch_shapes=scratch, input_output_aliases={2: 0}, name=name,
                         compiler_params=_params(2))(*args)
    return out[0] if exchange is None else out


def _ffn_dx(name, dgu, ga, blk, x, dy, nw, out_scale, exchange=None):
    T, D = x.shape
    tm = min(T, 512)
    ni = T // tm

    def body(*refs):
        d_ref, w0, w1, w2, w3, x_ref, dy_ref, nw_ref = refs[:8]
        if exchange is None:
            dx_ref, dxb_ref, dnw_ref = refs[8:]
        else:
            n = len(exchange)
            dx_ref, dxb_ref, dnw_ref = refs[8 + n:11 + n]
            send_sems, recv_sems = refs[-2:]
            pairs = [(lambda chip, r=r: r.at[2 * chip[0] + chip[1]], f)
                     for r, f in zip(refs[8:8 + n], refs[11 + n:11 + 2 * n])]

            @pl.when(pl.program_id(0) == 0)
            def _():
                for cp in _peer_copies(pairs, send_sems, recv_sems):
                    cp.start()

            @pl.when(pl.program_id(0) == ni - 1)
            def _():
                for cp in _peer_copies(pairs, send_sems, recv_sems):
                    cp.wait_recv()
                    cp.wait_send()

        i = pl.program_id(0)
        acc = (_dot(d_ref[0, 0], w0[0, 0], 1, 1) + _dot(d_ref[0, 1], w1[0, 0], 1, 1)
               + _dot(d_ref[1, 0], w2[0, 0], 1, 1) + _dot(d_ref[1, 1], w3[0, 0], 1, 1))
        dx, dw = _rms_bwd_tile(acc, x_ref[...], dy_ref[...], nw_ref[...])
        dx_ref[...] = dx
        dxb_ref[...] = (dx * out_scale).astype(BF16)

        @pl.when(i == 0)
        def _():
            dnw_ref[...] = dw

        @pl.when(i > 0)
        def _():
            dnw_ref[...] += dw

    wspec = lambda q: pl.BlockSpec((1, 1, D, FF_BLK), lambda i: (q, blk, 0, 0), pipeline_mode=pl.Buffered(1))
    row = pl.BlockSpec((tm, D), lambda i: (i, 0))
    in_specs = [pl.BlockSpec((2, 2, tm, FF_BLK), lambda i: (0, 0, i, 0)), wspec(0), wspec(1), wspec(2), wspec(3),
                row, row, pl.BlockSpec((1, D), lambda i: (0, 0))]
    out_specs = [row, row, pl.BlockSpec((1, D), lambda i: (0, 0))]
    out_shape = [jax.ShapeDtypeStruct((T, D), F32), jax.ShapeDtypeStruct((T, D), BF16), jax.ShapeDtypeStruct((1, D), F32)]
    args, scratch = [dgu, ga, ga, ga, ga, x, dy, nw], []
    if exchange is not None:
        in_specs += [HBM] * len(exchange)
        out_specs += [HBM] * len(exchange)
        out_shape += [jax.ShapeDtypeStruct((3,) + cs.shape[1:], BF16) for cs in exchange]
        scratch = [pltpu.SemaphoreType.DMA((3 * len(exchange),)), pltpu.SemaphoreType.DMA((3 * len(exchange),))]
        args += list(exchange)
    return pl.pallas_call(body, grid=(ni,), in_specs=in_specs, out_specs=out_specs, out_shape=out_shape,
                          scratch_shapes=scratch, name=name, compiler_params=_params(1))(*args)


def _ffn_fwd(tag, x, nw, ga, blk, wd):
    h, gu, act = _ffn_gu(tag + "_gu", x, nw, ga, blk)
    return _ffn_down(tag + "_down", act, wd, x), (h, gu, act)


def _ffn_bwd(tag, dy, dyh, x, nw, ga, blk, wd, saved, pa, out_scale):
    h, gu, act = saved
    dgu = _ffn_dact(tag + "_dact", dyh, wd, gu)
    d_wd = _ffn_dwd(tag + "_dwd", act, dyh)
    pa = _ffn_dwgu(tag + "_dwgu", h, dgu, pa, blk)
    dx, dxb, d_nw = _ffn_dx(tag + "_dx", dgu, ga, blk, x, dy, nw, out_scale)
    return dx, dxb, d_nw, pa, d_wd


def _conv_shifts(cur, halo, first, sign):
    tt = cur.shape[0]
    halo = jnp.where(first, 0.0, halo)
    rid = lax.broadcasted_iota(jnp.int32, (8, cur.shape[1]), 0)
    out = [cur]
    for s in (1, 2, 3):
        if sign < 0:
            sh = pltpu.roll(cur, s, 0)
            edge = jnp.where(rid < s, pltpu.roll(halo, s, 0), sh[0:8])
            sh = jnp.concatenate([edge, sh[8:]], axis=0) if tt > 8 else edge
        else:
            sh = pltpu.roll(cur, tt - s, 0)
            edge = jnp.where(rid >= 8 - s, pltpu.roll(halo, 8 - s, 0), sh[tt - 8:])
            sh = jnp.concatenate([sh[:tt - 8], edge], axis=0) if tt > 8 else edge
        out.append(sh)
    return out


def _conv_taps(cur, halo, w, first, sign, shifts=None):
    shifts = _conv_shifts(cur, halo, first, sign) if shifts is None else shifts
    acc = w[3:4, :] * shifts[0]
    for s in (1, 2, 3):
        acc = acc + w[3 - s:4 - s, :] * shifts[s]
    return acc


def _gdn_prep(name, proj, wconv, gate_p):
    T = proj.shape[0]
    tt = min(T, 256)
    hb = tt // 8
    nch = tt // A_CHUNK

    def body(cur_ref, halo_ref, ba_ref, w_ref, gp_ref, qkv_ref, bg_ref, gc_ref):
        first = pl.program_id(0) == 0
        for c in range(24):
            cols = pl.ds(c * 128, 128)
            conv = _conv_taps(cur_ref[:, cols], halo_ref[:, cols], w_ref[:, cols], first, -1)
            y = conv * _sigmoid(conv)
            if c < 16:
                y = y * lax.rsqrt(jnp.sum(y * y, axis=-1, keepdims=True) + EPS)
                if c < 8:
                    y = y * (A_DK ** -0.5)
            qkv_ref[:, cols] = y
        ba = ba_ref[...]
        lane = lax.broadcasted_iota(jnp.int32, ba.shape, 1)
        zarg = ba + gp_ref[1:2, :]
        softplus = jnp.maximum(zarg, 0.0) + jnp.log(1.0 + jnp.exp(-jnp.abs(zarg)))
        bg = jnp.where(lane < 8, _sigmoid(ba), jnp.where(lane < 16, gp_ref[0:1, :] * softplus, 0.0))
        bg_ref[...] = bg
        tri = (lax.broadcasted_iota(jnp.int32, (A_CHUNK, A_CHUNK), 0)
               >= lax.broadcasted_iota(jnp.int32, (A_CHUNK, A_CHUNK), 1)).astype(F32)
        for c in range(nch):
            rows = pl.ds(c * A_CHUNK, A_CHUNK)
            gc_ref[rows, :] = _dotx(tri, bg[c * A_CHUNK:(c + 1) * A_CHUNK, :])

    return pl.pallas_call(
        body, grid=(T // tt,),
        in_specs=[pl.BlockSpec((tt, 3072), lambda i: (i, 0)),
                  pl.BlockSpec((8, 3072), lambda i: (jnp.maximum(i * hb - 1, 0), 0)),
                  pl.BlockSpec((tt, 128), lambda i: (i, 32)),
                  pl.BlockSpec((4, 3072), lambda i: (0, 0)),
                  pl.BlockSpec((2, 128), lambda i: (0, 0))],
        out_specs=[pl.BlockSpec((tt, 3072), lambda i: (i, 0)), pl.BlockSpec((tt, 128), lambda i: (i, 0)),
                   pl.BlockSpec((tt, 128), lambda i: (i, 0))],
        out_shape=[jax.ShapeDtypeStruct((T, 3072), F32), jax.ShapeDtypeStruct((T, 128), F32),
                   jax.ShapeDtypeStruct((T, 128), F32)],
        name=name, compiler_params=_params(1))(proj, proj, proj, wconv, gate_p)


def _chunk_masks():
    ri = lax.broadcasted_iota(jnp.int32, (A_CHUNK, A_CHUNK), 0)
    ci = lax.broadcasted_iota(jnp.int32, (A_CHUNK, A_CHUNK), 1)
    return ri >= ci, ri > ci, ri == ci


def _chunk_local(q, k, gcol, grow, bcol):
    incl, strict, _ = _chunk_masks()
    dec = jnp.where(incl, jnp.exp(jnp.where(incl, gcol - grow, 0.0)), 0.0)
    e = jnp.exp(gcol)
    glast = grow[:, A_CHUNK - 1:A_CHUNK]
    f = jnp.exp(glast - gcol)
    gl = jnp.exp(glast)
    kb = k * bcol
    lmat = jnp.where(strict, _dotb(kb, k, 1, 1) * dec, 0.0)
    amat = jnp.where(incl, _dotb(q, k, 1, 1) * dec, 0.0)
    return dec, e, f, gl, kb, lmat, amat


def _unit_lower_inverse(lmats):
    _, _, eye = _chunk_masks()
    ts = [jnp.where(eye, 1.0, 0.0) - lm for lm in lmats]
    lps = [_doth(lm, lm) for lm in lmats]
    for it in range(5):
        ts = [t + _doth(t, lp) for t, lp in zip(ts, lps)]
        if it < 4:
            lps = [_doth(lp, lp) for lp in lps]
    return ts


def _gate_columns(bgt, gct):
    sel = (lax.broadcasted_iota(jnp.int32, (16, 128), 0) == lax.broadcasted_iota(jnp.int32, (16, 128), 1)).astype(F32)
    g_rows = _dotx(sel, gct, 1, 1)
    hs = range(A_HEADS)
    return ([bgt[:, h:h + 1] for h in hs], [gct[:, 8 + h:9 + h] for h in hs], [g_rows[8 + h:9 + h, :] for h in hs])


def _gdn_delta_fwd(name, qkv, bg, gcum, a4=None):
    T = qkv.shape[0]
    tt = min(T, 512)
    nch = tt // A_CHUNK
    NC = T // A_CHUNK
    ni = T // tt
    wd, ng = 128 * A_HG, A_HEADS // A_HG
    assert ng == 1

    def body(*refs):
        q_ref, k_ref, v_ref, bg_ref, gc_ref = refs[:5]
        if a4 is None:
            o_ref, s_ref, t_ref, u_ref, w_ref, state = refs[5:]
        else:
            a4_ref, o_ref, s_ref, t_ref, u_ref, w_ref, ra_ref, state, send_sems, recv_sems = refs[5:]

            @pl.when(pl.program_id(1) == 0)
            def _():
                _later_blocks_start(a4_ref, ra_ref, send_sems, recv_sems)

        @pl.when(pl.program_id(1) == 0)
        def _():
            state[...] = jnp.zeros_like(state)

        def chunk(c, carry):
            rows = pl.ds(pl.multiple_of(c * A_CHUNK, A_CHUNK), A_CHUNK)
            hs = range(A_HG)
            cols = [pl.ds(h * 128, 128) for h in hs]
            q = [q_ref[rows, cols[h]] for h in hs]
            k = [k_ref[rows, cols[h]] for h in hs]
            v = [v_ref[rows, cols[h]] for h in hs]
            bcl, gcl, grw = _gate_columns(bg_ref[rows, :], gc_ref[rows, :])
            loc = [_chunk_local(q[h], k[h], gcl[h], grw[h], bcl[h]) for h in hs]
            e, f, gl, kb, amat = ([l[i] for l in loc] for i in (1, 2, 3, 4, 6))
            tinv = _unit_lower_inverse([l[5] for l in loc])
            u = [_doth(tinv[h], v[h] * bcl[h]) for h in hs]
            w = [_doth(tinv[h], kb[h] * e[h]) for h in hs]
            s = [state[h] for h in hs]
            vn = [u[h] - _dotb(w[h], s[h]) for h in hs]
            o_s = [_dotb(q[h] * e[h], s[h]) for h in hs]
            o_a = [_dotb(amat[h], vn[h]) for h in hs]
            s_new = [s[h] * gl[h] + _dotb(k[h] * f[h], vn[h], 0, 0) for h in hs]
            for h in hs:
                s_ref[h, c] = s[h].astype(BF16)
                t_ref[h, c] = tinv[h]
                u_ref[rows, cols[h]] = u[h]
                w_ref[rows, cols[h]] = w[h]
                o_ref[rows, cols[h]] = o_s[h] + o_a[h]
                state[h] = s_new[h]
            return carry

        lax.fori_loop(0, nch, chunk, 0)

        if a4 is not None:
            @pl.when(pl.program_id(1) == ni - 1)
            def _():
                x, y, c, chips = _mesh_pos()
                for j, chip in enumerate(chips):
                    cp = pltpu.make_async_remote_copy(src_ref=a4_ref.at[pl.ds(1, 3), c], dst_ref=ra_ref.at[j],
                                                      send_sem=send_sems.at[j], recv_sem=recv_sems.at[j],
                                                      device_id=(*chip, c), device_id_type=MESH)
                    cp.wait_recv()
                    cp.wait_send()

    hd = lambda col: pl.BlockSpec((tt, wd), lambda h, i: (i, col * ng + h))
    gate_spec = pl.BlockSpec((tt, 128), lambda h, i: (i, 0))
    in_specs = [hd(0), hd(1), hd(2), gate_spec, gate_spec]
    out_specs = [pl.BlockSpec((tt, wd), lambda h, i: (i, h)),
                 pl.BlockSpec((A_HG, nch, 128, 128), lambda h, i: (h, i, 0, 0)),
                 pl.BlockSpec((A_HG, nch, A_CHUNK, A_CHUNK), lambda h, i: (h, i, 0, 0)),
                 pl.BlockSpec((tt, wd), lambda h, i: (i, h)), pl.BlockSpec((tt, wd), lambda h, i: (i, h))]
    out_shape = [jax.ShapeDtypeStruct((T, 1024), F32), jax.ShapeDtypeStruct((A_HEADS, NC, 128, 128), BF16),
                 jax.ShapeDtypeStruct((A_HEADS, NC, A_CHUNK, A_CHUNK), F32),
                 jax.ShapeDtypeStruct((T, 1024), F32), jax.ShapeDtypeStruct((T, 1024), F32)]
    scratch = [pltpu.VMEM((A_HG, 128, 128), F32)]
    args = [qkv, qkv, qkv, bg, gcum]
    if a4 is not None:
        in_specs.append(HBM)
        out_specs.append(HBM)
        out_shape.append(jax.ShapeDtypeStruct((3, 3, A_HALF, FF_BLK), BF16))
        scratch += [pltpu.SemaphoreType.DMA((3,)), pltpu.SemaphoreType.DMA((3,))]
        args.append(a4)
    return pl.pallas_call(body, grid=(ng, ni), in_specs=in_specs, out_specs=out_specs, out_shape=out_shape,
                          scratch_shapes=scratch, name=name, compiler_params=_params(2))(*args)


def _grads_exchange(cs_ref, from_ref, send_sems, recv_sems):
    x, y, c, chips = _mesh_pos()
    return [pltpu.make_async_remote_copy(src_ref=cs_ref.at[2 * chip[0] + chip[1]], dst_ref=from_ref.at[j],
                                         send_sem=send_sems.at[j], recv_sem=recv_sems.at[j],
                                         device_id=(*chip, c), device_id_type=MESH)
            for j, chip in enumerate(chips)]


def _gdn_delta_bwd(name, qkv, bg, gcum, d_o, s_sv, t_sv, u_sv, w_sv, cs_early=None):
    T = qkv.shape[0]
    tt = min(T, 512)
    nch = tt // A_CHUNK
    ni = T // tt

    def body(*refs):
        q_ref, k_ref, v_ref, bg_ref, gc_ref, do_ref, s_ref, t_ref, u_ref, w_ref = refs[:10]
        if cs_early is None:
            dq_ref, dk_ref, dv_ref, dbg_ref, dstate = refs[10:]
        else:
            cs_ref, dq_ref, dk_ref, dv_ref, dbg_ref, from_ref, dstate, send_sems, recv_sems = refs[10:]

            @pl.when(pl.program_id(1) == 0)
            def _():
                for cp in _grads_exchange(cs_ref, from_ref, send_sems, recv_sems):
                    cp.start()

        @pl.when(pl.program_id(1) == 0)
        def _():
            dstate[...] = jnp.zeros_like(dstate)

        incl, strict, _ = _chunk_masks()
        upper = (lax.broadcasted_iota(jnp.int32, (A_CHUNK, A_CHUNK), 0)
                 <= lax.broadcasted_iota(jnp.int32, (A_CHUNK, A_CHUNK), 1)).astype(F32)
        last_row = lax.broadcasted_iota(jnp.int32, (A_CHUNK, 1), 0) == A_CHUNK - 1
        ones = jnp.ones((A_CHUNK, 128), F32)

        rsum = lambda x: jnp.sum(x, axis=1, keepdims=True)

        def chunk(cc, carry):
            c = nch - 1 - cc
            rows = pl.ds(pl.multiple_of(c * A_CHUNK, A_CHUNK), A_CHUNK)
            bcl, gcl, grw = _gate_columns(bg_ref[rows, :], gc_ref[rows, :])
            lane = lax.broadcasted_iota(jnp.int32, (A_CHUNK, 128), 1)
            dbg = jnp.zeros((A_CHUNK, 128), F32)
            for first in range(0, A_HG, 4):
                hs = range(first, first + 4)
                cols = {h: pl.ds(h * 128, 128) for h in hs}
                q = {h: q_ref[rows, cols[h]] for h in hs}
                k = {h: k_ref[rows, cols[h]] for h in hs}
                v = {h: v_ref[rows, cols[h]] for h in hs}
                do = {h: do_ref[rows, cols[h]] for h in hs}
                u = {h: u_ref[rows, cols[h]] for h in hs}
                w = {h: w_ref[rows, cols[h]] for h in hs}
                s = {h: s_ref[h, c] for h in hs}
                tinv = {h: t_ref[h, c] for h in hs}
                ds = {h: dstate[h] for h in hs}
                loc = {h: _chunk_local(q[h], k[h], gcl[h], grw[h], bcl[h]) for h in hs}
                dec, e, f, gl, kb, lmat, amat = ({h: loc[h][i] for h in hs} for i in range(7))
                qd = {h: q[h] * e[h] for h in hs}
                kd = {h: k[h] * f[h] for h in hs}
                ke = {h: kb[h] * e[h] for h in hs}
                vn = {h: u[h] - _dotb(w[h], s[h]) for h in hs}
                d_qd = {h: _dotb(do[h], s[h], 1, 1) for h in hs}
                d_a = {h: jnp.where(incl, _dotb(do[h], vn[h], 1, 1), 0.0) for h in hs}
                d_vn1 = {h: _dotb(amat[h], do[h], 0, 0) for h in hs}
                d_vn = {h: d_vn1[h] + _dotb(kd[h], ds[h]) for h in hs}
                d_kd = {h: _dotb(vn[h], ds[h], 1, 1) for h in hs}
                d_w = {h: -_dotb(d_vn[h], s[h], 1, 1) for h in hs}
                ds_q = {h: _dotb(qd[h], do[h], 0, 0) for h in hs}
                ds_w = {h: _dotb(w[h], d_vn[h], 0, 0) for h in hs}
                d_bv = {h: _doth(tinv[h], d_vn[h], 0, 0) for h in hs}
                d_ke = {h: _doth(tinv[h], d_w[h], 0, 0) for h in hs}
                d_l1 = {h: _dotb(d_bv[h], u[h], 1, 1) for h in hs}
                d_l = {h: -jnp.where(strict, d_l1[h] + _dotb(d_ke[h], w[h], 1, 1), 0.0) for h in hs}
                d_kk = {h: d_l[h] * dec[h] for h in hs}
                d_qk = {h: d_a[h] * dec[h] for h in hs}
                d_kb = {h: _dotb(d_kk[h], k[h]) for h in hs}
                dk1 = {h: _dotb(d_kk[h], kb[h], 0, 0) for h in hs}
                dk2 = {h: _dotb(d_qk[h], q[h], 0, 0) for h in hs}
                dq1 = {h: _dotb(d_qk[h], k[h]) for h in hs}
                m = {h: d_l[h] * lmat[h] + d_a[h] * amat[h] for h in hs}
                col_m = {h: _dotx(m[h], ones, 0, 0)[:, 0:1] for h in hs}
                d_gc = {}
                for h in hs:
                    d_gl = jnp.sum(jnp.sum(ds[h] * s[h].astype(F32), axis=1, keepdims=True), axis=0, keepdims=True)
                    r_kd = rsum(d_kd[h] * kd[h])
                    tail = jnp.sum(r_kd, axis=0, keepdims=True) + d_gl * gl[h]
                    d_gc[h] = (rsum(m[h]) - col_m[h] + rsum(d_qd[h] * qd[h]) - r_kd + rsum(d_ke[h] * ke[h])
                               + jnp.where(last_row, tail, 0.0))
                dg = {h: _dotx(upper, d_gc[h] * ones)[:, 0:1] for h in hs}
                for h in hs:
                    dstate[h] = gl[h] * ds[h] + ds_q[h] - ds_w[h]
                    dk_ref[rows, cols[h]] = (dk1[h] + dk2[h] + d_kd[h] * f[h] + d_ke[h] * (bcl[h] * e[h])
                                             + d_kb[h] * bcl[h])
                    dq_ref[rows, cols[h]] = dq1[h] + d_qd[h] * e[h]
                    dv_ref[rows, cols[h]] = d_bv[h] * bcl[h]
                    d_beta = rsum(d_ke[h] * k[h]) * e[h] + rsum(d_kb[h] * k[h]) + rsum(d_bv[h] * v[h])
                    dbg = jnp.where(lane == h, d_beta, jnp.where(lane == 8 + h, dg[h], dbg))
            dbg_ref[rows, :] = dbg
            return carry

        lax.fori_loop(0, nch, chunk, 0)

        if cs_early is not None:
            @pl.when(pl.program_id(1) == ni - 1)
            def _():
                for cp in _grads_exchange(cs_ref, from_ref, send_sems, recv_sems):
                    cp.wait_recv()
                    cp.wait_send()

    wd, ng = 128 * A_HG, A_HEADS // A_HG
    assert ng == 1
    rev = lambda i: ni - 1 - i
    hd = lambda col: pl.BlockSpec((tt, wd), lambda h, i: (rev(i), col * ng + h))
    hd1 = pl.BlockSpec((tt, wd), lambda h, i: (rev(i), h))
    gate_spec = pl.BlockSpec((tt, 128), lambda h, i: (rev(i), 0))
    in_specs = [hd(0), hd(1), hd(2), gate_spec, gate_spec, hd1,
                pl.BlockSpec((A_HG, nch, 128, 128), lambda h, i: (h, rev(i), 0, 0)),
                pl.BlockSpec((A_HG, nch, A_CHUNK, A_CHUNK), lambda h, i: (h, rev(i), 0, 0)), hd1, hd1]
    out_specs = [hd1, hd1, hd1, gate_spec]
    out_shape = [jax.ShapeDtypeStruct((T, 1024), F32)] * 3 + [jax.ShapeDtypeStruct((T, 128), F32)]
    scratch = [pltpu.VMEM((A_HG, 128, 128), F32)]
    args = [qkv, qkv, qkv, bg, gcum, d_o, s_sv, t_sv, u_sv, w_sv]
    if cs_early is not None:
        in_specs.append(HBM)
        out_specs.append(HBM)
        out_shape.append(jax.ShapeDtypeStruct((3,) + cs_early.shape[1:], BF16))
        scratch += [pltpu.SemaphoreType.DMA((3,)), pltpu.SemaphoreType.DMA((3,))]
        args.append(cs_early)
    return pl.pallas_call(body, grid=(ng, ni), in_specs=in_specs, out_specs=out_specs, out_shape=out_shape,
                          scratch_shapes=scratch, name=name, compiler_params=_params(2))(*args)


def _gdn_gate_fwd(name, o, proj, nw):
    T = o.shape[0]
    tt = min(T, 512)

    def body(o_ref, z_ref, nw_ref, y_ref):
        for h in range(A_HEADS):
            cols = pl.ds(h * 128, 128)
            ov, z = o_ref[:, cols], z_ref[:, cols]
            r = lax.rsqrt(jnp.mean(ov * ov, axis=-1, keepdims=True) + EPS)
            y_ref[:, cols] = (ov * r * nw_ref[...] * (z * _sigmoid(z))).astype(BF16)

    return pl.pallas_call(
        body, grid=(T // tt,),
        in_specs=[pl.BlockSpec((tt, 1024), lambda i: (i, 0)), pl.BlockSpec((tt, 1024), lambda i: (i, 3)),
                  pl.BlockSpec((1, 128), lambda i: (0, 0))],
        out_specs=pl.BlockSpec((tt, 1024), lambda i: (i, 0)),
        out_shape=jax.ShapeDtypeStruct((T, 1024), BF16), name=name, compiler_params=_params(1))(o, proj, nw)


def _gdn_gate_bwd(name, dy2, o, proj, nw):
    T = o.shape[0]
    tt = min(T, 512)

    def body(dy_ref, o_ref, z_ref, nw_ref, do_ref, dz_ref, dnw_ref):
        dnw = jnp.zeros((1, 128), F32)
        for h in range(A_HEADS):
            cols = pl.ds(h * 128, 128)
            dy, ov, z = dy_ref[:, cols], o_ref[:, cols], z_ref[:, cols]
            s = _sigmoid(z)
            sz = z * s
            r = lax.rsqrt(jnp.mean(ov * ov, axis=-1, keepdims=True) + EPS)
            xhat = ov * r
            dn = dy * sz
            dz_ref[:, cols] = dy * (xhat * nw_ref[...]) * (s + z * s * (1.0 - s))
            dxhat = dn * nw_ref[...]
            do_ref[:, cols] = r * (dxhat - xhat * jnp.mean(dxhat * xhat, axis=-1, keepdims=True))
            dnw = dnw + jnp.sum(dn * xhat, axis=0, keepdims=True)

        @pl.when(pl.program_id(0) == 0)
        def _():
            dnw_ref[...] = dnw

        @pl.when(pl.program_id(0) > 0)
        def _():
            dnw_ref[...] += dnw

    blk = lambda c: pl.BlockSpec((tt, 1024), lambda i: (i, c))
    return pl.pallas_call(
        body, grid=(T // tt,),
        in_specs=[blk(0), blk(0), blk(3), pl.BlockSpec((1, 128), lambda i: (0, 0))],
        out_specs=[blk(0), blk(0), pl.BlockSpec((1, 128), lambda i: (0, 0))],
        out_shape=[jax.ShapeDtypeStruct((T, 1024), F32), jax.ShapeDtypeStruct((T, 1024), F32),
                   jax.ShapeDtypeStruct((1, 128), F32)],
        name=name, compiler_params=_params(1))(dy2, o, proj, nw)


def _gdn_prep_bwd1(name, proj, wconv, gate_p, dq, dk, dv, dbg):
    T = proj.shape[0]
    tt = min(T, 256)
    hb = tt // 8

    def body(cur_ref, halo_ref, ba_ref, w_ref, gp_ref, dq_ref, dk_ref, dv_ref, dbg_ref,
             dc_ref, dw_ref, dba_ref, dgp_ref):
        first = pl.program_id(0) == 0
        rid = lax.broadcasted_iota(jnp.int32, (8, 128), 0)
        for c in range(24):
            cols = pl.ds(c * 128, 128)
            cur = cur_ref[:, cols]
            shifts = _conv_shifts(cur, halo_ref[:, cols], first, -1)
            conv = _conv_taps(cur, None, w_ref[:, cols], first, -1, shifts)
            s = _sigmoid(conv)
            y = conv * s
            if c < 16:
                dref = dq_ref if c < 8 else dk_ref
                dn = dref[:, pl.ds((c % 8) * 128, 128)]
                rinv = lax.rsqrt(jnp.sum(y * y, axis=-1, keepdims=True) + EPS)
                yhat = y * rinv
                dyv = rinv * (dn - yhat * jnp.sum(dn * yhat, axis=-1, keepdims=True))
                if c < 8:
                    dyv = dyv * (A_DK ** -0.5)
            else:
                dyv = dv_ref[:, pl.ds((c - 16) * 128, 128)]
            dc = dyv * (s + y * (1.0 - s))
            dc_ref[:, cols] = dc
            parts = [jnp.sum(dc * sh, axis=0, keepdims=True) for sh in shifts]
            dwc = jnp.concatenate(parts[::-1], axis=0)

            @pl.when(first)
            def _():
                dw_ref[:, cols] = dwc

            @pl.when(jnp.logical_not(first))
            def _():
                dw_ref[:, cols] += dwc

        ba = ba_ref[...]
        dbg = dbg_ref[...]
        lane = lax.broadcasted_iota(jnp.int32, ba.shape, 1)
        sb = _sigmoid(ba)
        zarg = ba + gp_ref[1:2, :]
        softplus = jnp.maximum(zarg, 0.0) + jnp.log(1.0 + jnp.exp(-jnp.abs(zarg)))
        d_b = dbg * sb * (1.0 - sb)
        d_a = dbg * gp_ref[0:1, :] * _sigmoid(zarg)
        dba_ref[...] = jnp.where(lane < 8, d_b, jnp.where(lane < 16, d_a, 0.0))
        g = gp_ref[0:1, :] * softplus
        in_a = (lane >= 8) & (lane < 16)
        sums = jnp.concatenate([jnp.sum(jnp.where(in_a, dbg * g, 0.0), axis=0, keepdims=True),
                                jnp.sum(jnp.where(in_a, d_a, 0.0), axis=0, keepdims=True)], axis=0)

        @pl.when(first)
        def _():
            dgp_ref[...] = sums

        @pl.when(jnp.logical_not(first))
        def _():
            dgp_ref[...] += sums

    row = lambda w, c=0: pl.BlockSpec((tt, w), lambda i: (i, c))
    return pl.pallas_call(
        body, grid=(T // tt,),
        in_specs=[row(3072), pl.BlockSpec((8, 3072), lambda i: (jnp.maximum(i * hb - 1, 0), 0)), row(128, 32),
                  pl.BlockSpec((4, 3072), lambda i: (0, 0)), pl.BlockSpec((2, 128), lambda i: (0, 0)),
                  row(1024), row(1024), row(1024), row(128)],
        out_specs=[row(3072), pl.BlockSpec((4, 3072), lambda i: (0, 0)), row(128),
                   pl.BlockSpec((2, 128), lambda i: (0, 0))],
        out_shape=[jax.ShapeDtypeStruct((T, 3072), F32), jax.ShapeDtypeStruct((4, 3072), F32),
                   jax.ShapeDtypeStruct((T, 128), F32), jax.ShapeDtypeStruct((2, 128), F32)],
        name=name, compiler_params=_params(1))(proj, proj, proj, wconv, gate_p, dq, dk, dv, dbg)


def _gdn_prep_bwd2(name, dc, wconv, dz, dba):
    T = dc.shape[0]
    tt = min(T, 256)
    hb = tt // 8
    ni = T // tt

    def body(cur_ref, halo_ref, w_ref, dz_ref, dba_ref, o_ref):
        last = pl.program_id(0) == ni - 1
        for c in range(24):
            cols = pl.ds(c * 128, 128)
            o_ref[:, cols] = _conv_taps(cur_ref[:, cols], halo_ref[:, cols], w_ref[:, cols], last, +1).astype(BF16)
        o_ref[:, pl.ds(3072, 1024)] = dz_ref[...].astype(BF16)
        o_ref[:, pl.ds(4096, 128)] = dba_ref[...].astype(BF16)

    return pl.pallas_call(
        body, grid=(ni,),
        in_specs=[pl.BlockSpec((tt, 3072), lambda i: (i, 0)),
                  pl.BlockSpec((8, 3072), lambda i: (jnp.minimum((i + 1) * hb, T // 8 - 1), 0)),
                  pl.BlockSpec((4, 3072), lambda i: (0, 0)),
                  pl.BlockSpec((tt, 1024), lambda i: (i, 0)), pl.BlockSpec((tt, 128), lambda i: (i, 0))],
        out_specs=pl.BlockSpec((tt, A_IN_PAD), lambda i: (i, 0)),
        out_shape=jax.ShapeDtypeStruct((T, A_IN_PAD), BF16), name=name, compiler_params=_params(1))(
            dc, dc, wconv, dz, dba)


def _gdn_fwd(x, nw, w_in, wconv, gate_p, out_nw, w_out, a4=None):
    h = _rms_fwd("a_rms", x, nw)
    proj = _mm_plain("a_proj", h, w_in, 1, 0, F32, tn=FF_BLK)
    qkv, bg, gcum = _gdn_prep("a_prep", proj, wconv, gate_p)
    o, s_sv, t_sv, u_sv, w_sv, *arrived = _gdn_delta_fwd("a_delta", qkv, bg, gcum, a4)
    o2 = _gdn_gate_fwd("a_gate", o, proj, out_nw)
    y = _mm_residual("a_out", o2, w_out, x, 1.0)
    return y, (h, proj, qkv, bg, gcum, o, s_sv, t_sv, u_sv, w_sv, o2), (arrived[0] if arrived else None)


def _gdn_bwd(dy, dyb, x, nw, w_in, wconv, gate_p, out_nw, w_out, saved, out_scale, cs_early=None):
    h, proj, qkv, bg, gcum, o, s_sv, t_sv, u_sv, w_sv, o2 = saved
    d_o2 = _mm_plain("a_dout", dyb, w_out, 1, 1, F32)
    d_wout = _mm_plain("a_dwout", o2, dyb, 0, 0, F32)
    d_o, d_z, d_outnw = _gdn_gate_bwd("a_dgate", d_o2, o, proj, out_nw)
    dq, dk, dv, dbg, *arrived = _gdn_delta_bwd("a_ddelta", qkv, bg, gcum, d_o, s_sv, t_sv, u_sv, w_sv, cs_early)
    dc, d_wconv, dba, dgp = _gdn_prep_bwd1("a_dprep1", proj, wconv, gate_p, dq, dk, dv, dbg)
    dproj = _gdn_prep_bwd2("a_dprep2", dc, wconv, d_z, dba)
    d_win = _mm_plain("a_dwin", h, dproj, 0, 0, F32, tn=FF_BLK, tk=2048)
    dx, dxb, d_nw = _mm_rms_bwd("a_dx", dproj, w_in, x, dy, nw, A_IN_PAD, out_scale)
    return dx, dxb, d_nw, d_win, d_wconv, dgp, d_outnw, d_wout, (arrived[0] if arrived else None)


def _swa_masks(n):
    qi = lax.broadcasted_iota(jnp.int32, (B_BLK, B_BLK), 0)
    kj = lax.broadcasted_iota(jnp.int32, (B_BLK, B_BLK), 1)
    return kj > qi + jnp.where(n > 0, 0, B_BLK), kj <= qi


def _swa_fwd(name, q, k, v, sinks):
    T = q.shape[1]
    tq = min(T, 1024)
    nbt = tq // B_BLK
    scale = B_HD ** -0.5
    G = B_HEADS // B_KV

    def body(q_ref, k_ref, v_ref, kh_ref, vh_ref, s_ref, o_ref, l_ref):
        first_blk = pl.program_id(1) * nbt

        def block(n, kp, vp):
            m_prev, m_cur = _swa_masks(first_blk + n)
            cur = pl.ds(pl.multiple_of(n * B_BLK, B_BLK), B_BLK)
            kc, vc = k_ref[0, cur, :], v_ref[0, cur, :]
            gs = range(G)
            rmax = lambda a: jnp.max(a, axis=1, keepdims=True)
            rsum = lambda a: jnp.sum(a, axis=1, keepdims=True)
            sink = [s_ref[g][:, 0:1] for g in gs]
            qb = [q_ref[g, cur, :] for g in gs]
            s_p = [jnp.where(m_prev, _dot(qb[g], kp, 1, 1) * scale, -jnp.inf) for g in gs]
            s_c = [jnp.where(m_cur, _dot(qb[g], kc, 1, 1) * scale, -jnp.inf) for g in gs]
            m = [jnp.maximum(jnp.maximum(rmax(s_p[g]), rmax(s_c[g])), sink[g]) for g in gs]
            p_p = [jnp.exp(s_p[g] - m[g]) for g in gs]
            p_c = [jnp.exp(s_c[g] - m[g]) for g in gs]
            den = [rsum(p_p[g]) + rsum(p_c[g]) + jnp.exp(sink[g] - m[g]) for g in gs]
            a_p = [_dotb(p_p[g], vp) for g in gs]
            a_c = [_dotb(p_c[g], vc) for g in gs]
            for g in gs:
                o_ref[g, cur, :] = ((a_p[g] + a_c[g]) / den[g]).astype(BF16)
                l_ref[g, cur, :] = m[g] + jnp.log(den[g])

        block(0, kh_ref[0], vh_ref[0])

        def rest(n, carry):
            prv = pl.ds(pl.multiple_of((n - 1) * B_BLK, B_BLK), B_BLK)
            block(n, k_ref[0, prv, :], v_ref[0, prv, :])
            return carry

        lax.fori_loop(1, nbt, rest, 0)

    qs = pl.BlockSpec((G, tq, B_HD), lambda kv, i: (kv, i, 0))
    ks = pl.BlockSpec((1, tq, B_HD), lambda kv, i: (kv, i, 0))
    halo = pl.BlockSpec((1, B_BLK, B_HD), lambda kv, i: (kv, jnp.maximum(i * nbt - 1, 0), 0))
    return pl.pallas_call(
        body, grid=(B_KV, T // tq),
        in_specs=[qs, ks, ks, halo, halo, pl.BlockSpec((G, 1, 128), lambda kv, i: (kv, 0, 0))],
        out_specs=[qs, pl.BlockSpec((G, tq, 1), lambda kv, i: (kv, i, 0))],
        out_shape=[jax.ShapeDtypeStruct((B_HEADS, T, B_HD), BF16), jax.ShapeDtypeStruct((B_HEADS, T, 1), F32)],
        name=name, compiler_params=_params(2))(q, k, v, k, v, sinks)


def _swa_bwd(name, q, k, v, sinks, o, lse, do):
    T = q.shape[1]
    tq = min(T, 1024)
    nbt, ni = tq // B_BLK, T // tq
    scale = B_HD ** -0.5
    G = B_HEADS // B_KV

    def body(q_ref, k_ref, v_ref, kh_ref, vh_ref, s_ref, o_ref, l_ref, do_ref, dq_ref, dk_ref, dv_ref, ds_ref,
             dk_halo, dv_halo):
        step = pl.program_id(1)
        first_blk = (ni - 1 - step) * nbt
        last = pl.ds(tq - B_BLK, B_BLK)
        dk_ref[...] = jnp.zeros_like(dk_ref)
        dv_ref[...] = jnp.zeros_like(dv_ref)

        @pl.when(step > 0)
        def _():
            dk_ref[0, last, :] = dk_halo[...]
            dv_ref[0, last, :] = dv_halo[...]

        def block(n, kp, vp, dsinks):
            m_prev, m_cur = _swa_masks(first_blk + n)
            cur = pl.ds(pl.multiple_of(n * B_BLK, B_BLK), B_BLK)
            kc, vc = k_ref[0, cur, :], v_ref[0, cur, :]
            gs = range(G)
            sink = [s_ref[g][:, 0:1] for g in gs]
            qb = [q_ref[g, cur, :] for g in gs]
            dob = [do_ref[g, cur, :] for g in gs]
            lse_b = [l_ref[g, cur, :] for g in gs]
            p_p = [jnp.where(m_prev, jnp.exp(_dot(qb[g], kp, 1, 1) * scale - lse_b[g]), 0.0) for g in gs]
            p_c = [jnp.where(m_cur, jnp.exp(_dot(qb[g], kc, 1, 1) * scale - lse_b[g]), 0.0) for g in gs]
            delta = [jnp.sum(dob[g].astype(F32) * o_ref[g, cur, :].astype(F32), axis=1, keepdims=True) for g in gs]
            ds_p = [p_p[g] * (_dot(dob[g], vp, 1, 1) - delta[g]) for g in gs]
            ds_c = [p_c[g] * (_dot(dob[g], vc, 1, 1) - delta[g]) for g in gs]
            dq_p = [_dotb(ds_p[g], kp) for g in gs]
            dq_c = [_dotb(ds_c[g], kc) for g in gs]
            dk_ps = [_dotb(ds_p[g], qb[g], 0, 0) for g in gs]
            dk_cs = [_dotb(ds_c[g], qb[g], 0, 0) for g in gs]
            dv_ps = [_dotb(p_p[g], dob[g], 0, 0) for g in gs]
            dv_cs = [_dotb(p_c[g], dob[g], 0, 0) for g in gs]
            for g in gs:
                dq_ref[g, cur, :] = (dq_p[g] + dq_c[g]) * scale
            out = tuple(dsinks[g] - jnp.sum(jnp.exp(sink[g] - lse_b[g]) * delta[g], axis=0, keepdims=True) for g in gs)
            total = lambda parts: (parts[0] + parts[1]) + (parts[2] + parts[3])
            dk_ref[0, cur, :] += total(dk_cs) * scale
            dv_ref[0, cur, :] += total(dv_cs)
            return total(dk_ps) * scale, total(dv_ps), out

        zeros = tuple(jnp.zeros((1, 1), F32) for _ in range(G))
        dk_p, dv_p, dsinks = block(0, kh_ref[0], vh_ref[0], zeros)
        dk_halo[...] = dk_p
        dv_halo[...] = dv_p

        def rest(n, dsinks):
            prv = pl.ds(pl.multiple_of((n - 1) * B_BLK, B_BLK), B_BLK)
            dk_p, dv_p, dsinks = block(n, k_ref[0, prv, :], v_ref[0, prv, :], dsinks)
            dk_ref[0, prv, :] += dk_p
            dv_ref[0, prv, :] += dv_p
            return dsinks

        dsinks = lax.fori_loop(1, nbt, rest, dsinks)
        for g in range(G):
            row = jnp.broadcast_to(dsinks[g], (1, 128))

            @pl.when(step == 0)
            def _():
                ds_ref[g] = row

            @pl.when(step > 0)
            def _():
                ds_ref[g] += row

    rev = lambda i: ni - 1 - i
    qs = pl.BlockSpec((G, tq, B_HD), lambda kv, i: (kv, rev(i), 0))
    ks = pl.BlockSpec((1, tq, B_HD), lambda kv, i: (kv, rev(i), 0))
    halo = pl.BlockSpec((1, B_BLK, B_HD), lambda kv, i: (kv, jnp.maximum(rev(i) * nbt - 1, 0), 0))
    ss = pl.BlockSpec((G, 1, 128), lambda kv, i: (kv, 0, 0))
    return pl.pallas_call(
        body, grid=(B_KV, ni),
        in_specs=[qs, ks, ks, halo, halo, ss, qs, pl.BlockSpec((G, tq, 1), lambda kv, i: (kv, rev(i), 0)), qs],
        out_specs=[qs, ks, ks, ss],
        out_shape=[jax.ShapeDtypeStruct((B_HEADS, T, B_HD), F32), jax.ShapeDtypeStruct((B_KV, T, B_HD), F32),
                   jax.ShapeDtypeStruct((B_KV, T, B_HD), F32), jax.ShapeDtypeStruct((B_HEADS, 1, 128), F32)],
        scratch_shapes=[pltpu.VMEM((B_BLK, B_HD), F32), pltpu.VMEM((B_BLK, B_HD), F32)],
        name=name, compiler_params=_params(2))(q, k, v, k, v, sinks, o, lse, do)


def _split_heads(a, n):
    T = a.shape[0]
    return a.reshape(T, n, B_HD).transpose(1, 0, 2)


def _merge_heads(a):
    n, T, _ = a.shape
    return a.transpose(1, 0, 2).reshape(T, n * B_HD)


def _swa_mixer_fwd(x, nw, w_in, b_in, sinks, w_out, b_out):
    h = _rms_fwd("b_rms", x, nw)
    proj = _mm_plain("b_proj", h, w_in, 1, 0, BF16, tn=768, bias=b_in)
    q, k, v = _split_heads(proj[:, :1024], B_HEADS), _split_heads(proj[:, 1024:1280], B_KV), _split_heads(proj[:, 1280:], B_KV)
    o, lse = _swa_fwd("b_attn", q, k, v, sinks)
    om = _merge_heads(o)
    y = _mm_residual("b_out", om, w_out, x, 1.0, bias=b_out)
    return y, (h, q, k, v, o, lse, om)


def _swa_mixer_bwd(dy, dyb, x, nw, w_in, sinks, w_out, saved, out_scale):
    h, q, k, v, o, lse, om = saved
    d_om = _mm_plain("b_dout", dyb, w_out, 1, 1, BF16)
    d_wout = _mm_plain("b_dwout", om, dyb, 0, 0, F32)
    d_bout = _colsum("b_dbout", dy)
    dq, dk, dv, dsinks = _swa_bwd("b_dattn", q, k, v, sinks, o, lse, _split_heads(d_om, B_HEADS))
    dproj = jnp.concatenate([_merge_heads(dq), _merge_heads(dk), _merge_heads(dv)], axis=1)
    d_bin = _colsum("b_dbin", dproj)
    d_win = _mm_plain("b_dwin", h, dproj, 0, 0, F32, tn=768)
    dx, dxb, d_nw = _mm_rms_bwd("b_dx", dproj, w_in, x, dy, nw, 1536, out_scale)
    return dx, dxb, d_nw, d_win, d_bin, dsinks[:, 0, 0], d_wout, d_bout


def _loss_head(name, x, tgt, fw, out_scale):
    T, D = x.shape
    tt = min(T, 512)

    def body(x_ref, t_ref, w_ref, dx_ref, dxb_ref, loss_ref, dw_ref):
        xv = x_ref[...]
        r = lax.rsqrt(jnp.mean(xv * xv, axis=-1, keepdims=True) + EPS)
        xhat = xv * r
        diff = xhat * w_ref[...] - t_ref[...]
        part = 0.5 * jnp.sum(jnp.mean(diff * diff, axis=-1, keepdims=True), axis=0, keepdims=True)
        dyv = diff * (1.0 / D)
        dxhat = dyv * w_ref[...]
        dx = r * (dxhat - xhat * jnp.mean(dxhat * xhat, axis=-1, keepdims=True))
        dx_ref[...] = dx
        dxb_ref[...] = (dx * out_scale).astype(BF16)
        dw = jnp.sum(dyv * xhat, axis=0, keepdims=True)
        lp = jnp.broadcast_to(part, (1, 128))

        @pl.when(pl.program_id(0) == 0)
        def _():
            loss_ref[...] = lp
            dw_ref[...] = dw

        @pl.when(pl.program_id(0) > 0)
        def _():
            loss_ref[...] += lp
            dw_ref[...] += dw

    row = pl.BlockSpec((tt, D), lambda i: (i, 0))
    return pl.pallas_call(
        body, grid=(T // tt,), in_specs=[row, row, pl.BlockSpec((1, D), lambda i: (0, 0))],
        out_specs=[row, row, pl.BlockSpec((1, 128), lambda i: (0, 0)), pl.BlockSpec((1, D), lambda i: (0, 0))],
        out_shape=[jax.ShapeDtypeStruct((T, D), F32), jax.ShapeDtypeStruct((T, D), BF16),
                   jax.ShapeDtypeStruct((1, 128), F32), jax.ShapeDtypeStruct((1, D), F32)],
        name=name, compiler_params=_params(1))(x, tgt, fw)


def _local_step(x, tgt, wts, comm=None):
    W = dict(wts)
    g = {}
    ga = W["ga"]
    a4 = comm["a4"] if comm else None
    n1, n2, nm = W["ffn1_norm"], W["ffn2_norm"], W["mix_norm"]
    h10, gu10, act10, *arrived = _ffn_gu("f10_gu", x, n1[0:1], ga, 0, (comm["big_b"], comm["small"]) if comm else None)
    if comm:
        W.update(comm["finish"](*arrived))
    wdn = W["w_down"]
    x1, sv1 = _ffn_down("f10_down", act10, wdn[0], x), (h10, gu10, act10)
    x2, sva, arrived = _gdn_fwd(x1, nm[0:1], W["a_w_in"], W["a_w_conv"], W["a_gate_p"], W["a_out_norm"], W["a_w_out"], a4)
    if a4 is not None:
        ga = _fill_a(1, a4, arrived, ga.reshape(4, 4, 2, A_HALF, FF_BLK)).reshape(ga.shape)
    x3, sv3 = _ffn_fwd("f20", x2, n2[0:1], ga, 2, wdn[2])
    x4, sv4 = _ffn_fwd("f11", x3, n1[1:2], ga, 1, wdn[1])
    x5, svb = _swa_mixer_fwd(x4, nm[1:2], W["b_w_in"], W["b_b_in"], W["b_sinks"], W["b_w_out"], W["b_b_out"])
    x6, sv6 = _ffn_fwd("f21", x5, n2[1:2], ga, 3, wdn[3])
    dx, dxb, loss_p, g["final_norm"] = _loss_head("loss_head", x6, tgt, W["final_norm"], 0.5)

    pa = jnp.zeros(ga.shape, BF16)
    dx, dxb, n21, pa, wd21 = _ffn_bwd("f21", dx, dxb, x5, n2[1:2], ga, 3, wdn[3], sv6, pa, 1.0)
    dx, dxb, nb, g["b_w_in"], g["b_b_in"], g["b_sinks"], g["b_w_out"], g["b_b_out"] = _swa_mixer_bwd(
        dx, dxb, x4, nm[1:2], W["b_w_in"], W["b_sinks"], W["b_w_out"], svb, 0.5)
    dx, dxb, n11, pa, wd11 = _ffn_bwd("f11", dx, dxb, x3, n1[1:2], ga, 1, wdn[1], sv4, pa, 0.5)
    dx, dxb, n20, pa, wd20 = _ffn_bwd("f20", dx, dxb, x2, n2[0:1], ga, 2, wdn[2], sv3, pa, 1.0)
    g["cs_early"] = _pair_sums_a("1", pa.reshape(4, 4, 2, A_HALF, FF_BLK), 1, 3) if a4 is not None else None
    dx, dxb, na, g["a_w_in"], g["a_w_conv"], g["a_gate_p"], g["a_out_norm"], g["a_w_out"], g["from_early"] = _gdn_bwd(
        dx, dxb, x1, nm[0:1], W["a_w_in"], W["a_w_conv"], W["a_gate_p"], W["a_out_norm"], W["a_w_out"], sva, 0.5,
        g["cs_early"])
    dgu10 = _ffn_dact("f10_dact", dxb, wdn[0], gu10)
    wd10 = _ffn_dwd("f10_dwd", act10, dxb)
    g["w_down"] = jnp.stack([wd10, wd11, wd20, wd21])
    if comm:
        p_b = comm["pack_b"](g)
        cs_b = _pair_sum("b", p_b, _pair_send("b", p_b))
        pa, from_b1 = _ffn_dwgu("f10_dwgu", h10, dgu10, pa, 0, cs_b[:, :B_SPLIT])
        cs_late = _pair_sums_a("0", pa.reshape(4, 4, 2, A_HALF, FF_BLK), 0, 1)
        dx, dxb, n10, from_late, from_b2 = _ffn_dx("f10_dx", dgu10, ga, 0, x, dx, n1[0:1], 1.0,
                                                   (cs_late, cs_b[:, B_SPLIT:]))
        from_b = jnp.concatenate([from_b1, from_b2], axis=1)
        g["late"] = ((cs_late, cs_b), (from_late, from_b))
    else:
        pa = _ffn_dwgu("f10_dwgu", h10, dgu10, pa, 0)
        dx, dxb, n10 = _ffn_dx("f10_dx", dgu10, ga, 0, x, dx, n1[0:1], 1.0)
        g["late"] = None
    g["ga"] = pa

    g["ffn1_norm"] = jnp.concatenate([n10, n11], axis=0)
    g["ffn2_norm"] = jnp.concatenate([n20, n21], axis=0)
    g["mix_norm"] = jnp.concatenate([na, nb], axis=0)
    return loss_p, dx, g


A_ROWS = 4 * D_MODEL
PACK = (("ffn1_w_down", 1408), ("ffn2_w_down", 1408), ("a_w_in", 1028), ("a_w_out", 256), ("b_w_in", 384),
        ("b_w_out", 256))
PACK_TILE = 16
PACK_USED = sum(-(-n // PACK_TILE) * PACK_TILE for _, n in PACK)
PACK_ROWS = 4864
assert PACK_USED <= PACK_ROWS
B_SPLIT = 1408
SMALL_SHARD = (8, 512)
MOVE_ROWS = {"a": 512, "b": 608}
SUM_ROWS = {"a": 256, "b": 304}


def _mesh_pos():
    x, y, c = lax.axis_index("x"), lax.axis_index("y"), lax.axis_index("c")
    return x, y, c, [(1 - x, y), (x, 1 - y), (1 - x, 1 - y)]


def _half(rows, c):
    return pl.ds(pl.multiple_of(c * (rows // 2), 16), rows // 2)


A_HALF = D_MODEL // 2


def _src_chip(j):
    x, y = lax.axis_index("x"), lax.axis_index("y")
    return jnp.where(j == 0, 2 * (1 - x) + y, jnp.where(j == 1, 2 * x + 1 - y, 2 * (1 - x) + 1 - y))


def _later_blocks_start(a4_ref, ra_ref, send_sems, recv_sems):
    x, y, c, chips = _mesh_pos()
    copies = [pltpu.make_async_remote_copy(src_ref=a4_ref.at[pl.ds(1, 3), c], dst_ref=ra_ref.at[j],
                                           send_sem=send_sems.at[j], recv_sem=recv_sems.at[j],
                                           device_id=(*chip, c), device_id_type=MESH)
              for j, chip in enumerate(chips)]
    for cp in copies:
        cp.start()
    return copies


def _fill_a(phase, a4, ra, ga=None):
    nb = 1 if phase == 0 else 3
    first = 0 if phase == 0 else 1
    steps = 3 * nb
    own_tiles = 2 * nb
    ra = ra.reshape(3, nb, A_HALF, FF_BLK)

    def body(*refs):
        if phase == 0:
            r_ref, own_ref, g_ref, send_sem, recv_sem, local_sems = refs
        else:
            r_ref, own_ref, _, g_ref, send_sem, recv_sem, local_sems = refs
        x, y, c, _ = _mesh_pos()
        s = pl.program_id(0)
        j, b = s // nb, s % nb
        dst = g_ref.at[_src_chip(j), first + b, c]
        keep = pltpu.make_async_copy(r_ref.at[0, 0], dst, local_sems.at[0])
        give = pltpu.make_async_remote_copy(src_ref=r_ref.at[0, 0], dst_ref=dst, send_sem=send_sem, recv_sem=recv_sem,
                                            device_id=(x, y, 1 - c), device_id_type=MESH)
        keep.start()
        give.start()

        @pl.when(s < own_tiles)
        def _():
            own = pltpu.make_async_copy(own_ref.at[0, 0], g_ref.at[2 * x + y, first + s // 2, s % 2], local_sems.at[1])
            own.start()
            own.wait()

        give.wait_send()
        keep.wait()

        @pl.when(s == steps - 1)
        def _():
            landed = g_ref.at[pl.ds(0, 3), pl.ds(0, nb), 0]
            pltpu.make_async_remote_copy(src_ref=landed, dst_ref=landed, send_sem=send_sem, recv_sem=recv_sem,
                                         device_id=(x, y, c), device_id_type=MESH).wait_recv()

    tile = (1, 1, A_HALF, FF_BLK)
    in_specs = [pl.BlockSpec(tile, lambda s: (s // nb, s % nb, 0, 0)),
                pl.BlockSpec(tile, lambda s: (first + jnp.minimum(s, own_tiles - 1) // 2, jnp.minimum(s, own_tiles - 1) % 2, 0, 0))]
    args = [ra, a4]
    if phase == 1:
        in_specs.append(HBM)
        args.append(ga)
    return pl.pallas_call(
        body, grid=(steps,), in_specs=in_specs, out_specs=HBM,
        out_shape=jax.ShapeDtypeStruct((4, 4, 2, A_HALF, FF_BLK), BF16),
        scratch_shapes=[pltpu.SemaphoreType.DMA, pltpu.SemaphoreType.DMA, pltpu.SemaphoreType.DMA((2,))],
        input_output_aliases={2: 0} if phase == 1 else {},
        name="fill_a%d" % phase, compiler_params=_params(1))(*args)


def _gather_chips(big_a4):
    def body(a_ref, ra_ref, send_sems, recv_sems):
        c = lax.axis_index("c")
        send = _peer_copies([(lambda chip: a_ref.at[0, c], ra_ref)], send_sems, recv_sems)
        for cp in send:
            cp.start()
        for cp in send:
            cp.wait_recv()
        for cp in send:
            cp.wait_send()

    return pl.pallas_call(
        body, name="gather_chips", in_specs=[HBM], out_specs=HBM,
        out_shape=jax.ShapeDtypeStruct((3, A_HALF, FF_BLK), BF16),
        scratch_shapes=[pltpu.SemaphoreType.DMA((3,)), pltpu.SemaphoreType.DMA((3,))])(big_a4)


def _gather_fill(tag, big, recv):
    rows_all, width = big.shape
    half, mv = rows_all // 2, MOVE_ROWS[tag]
    nt = half // mv
    own_tiles = rows_all // mv
    assert half % mv == 0 and own_tiles <= 3 * nt

    def body(recv_ref, big_ref, g_ref, send_sem, recv_sem, local_sems):
        x, y, c, chips = _mesh_pos()
        j, t = pl.program_id(0), pl.program_id(1)
        step = j * nt + t
        src_chip = jnp.where(j == 0, 2 * (1 - x) + y, jnp.where(j == 1, 2 * x + 1 - y, 2 * (1 - x) + 1 - y))
        rows = pl.ds(pl.multiple_of(c * half + t * mv, 16), mv)
        keep = pltpu.make_async_copy(recv_ref.at[0], g_ref.at[src_chip, rows], local_sems.at[0])
        give = pltpu.make_async_remote_copy(src_ref=recv_ref.at[0], dst_ref=g_ref.at[src_chip, rows],
                                            send_sem=send_sem, recv_sem=recv_sem,
                                            device_id=(x, y, 1 - c), device_id_type=MESH)
        keep.start()
        give.start()

        @pl.when(step < own_tiles)
        def _():
            own_rows = pl.ds(pl.multiple_of(step * mv, 16), mv)
            own = pltpu.make_async_copy(big_ref, g_ref.at[2 * x + y, own_rows], local_sems.at[1])
            own.start()
            own.wait()

        give.wait_send()
        keep.wait()

        @pl.when(step == 3 * nt - 1)
        def _():
            landed = g_ref.at[pl.ds(0, 3), pl.ds(0, half)]
            pltpu.make_async_remote_copy(src_ref=landed, dst_ref=landed, send_sem=send_sem, recv_sem=recv_sem,
                                         device_id=(x, y, c), device_id_type=MESH).wait_recv()

    return pl.pallas_call(
        body, grid=(3, nt),
        in_specs=[pl.BlockSpec((1, mv, width), lambda j, t: (j, t, 0)),
                  pl.BlockSpec((mv, width), lambda j, t: (jnp.minimum(j * nt + t, own_tiles - 1), 0))],
        out_specs=HBM, out_shape=jax.ShapeDtypeStruct((4, rows_all, width), BF16),
        scratch_shapes=[pltpu.SemaphoreType.DMA, pltpu.SemaphoreType.DMA, pltpu.SemaphoreType.DMA((2,))],
        name="gather_fill_" + tag, compiler_params=_params(2))(recv, big)


def _pair_send(tag, p):
    _, rows_all, width = p.shape
    half, mv = rows_all // 2, MOVE_ROWS[tag]
    nt = half // mv

    def body(p_ref, a_ref, send_sem, recv_sem):
        x, y, c, _ = _mesh_pos()
        s, t = pl.program_id(0), pl.program_id(1)
        rows = pl.ds(pl.multiple_of(t * mv, 16), mv)
        give = pltpu.make_async_remote_copy(src_ref=p_ref.at[0], dst_ref=a_ref.at[s, rows], send_sem=send_sem,
                                            recv_sem=recv_sem, device_id=(x, y, 1 - c), device_id_type=MESH)
        give.start()
        give.wait_send()

        @pl.when((s == 3) & (t == nt - 1))
        def _():
            pltpu.make_async_remote_copy(src_ref=a_ref, dst_ref=a_ref, send_sem=send_sem, recv_sem=recv_sem,
                                         device_id=(x, y, c), device_id_type=MESH).wait_recv()

    return pl.pallas_call(
        body, grid=(4, nt),
        in_specs=[pl.BlockSpec((1, mv, width), lambda s, t: (s, (1 - lax.axis_index("c")) * nt + t, 0))],
        out_specs=HBM, out_shape=jax.ShapeDtypeStruct((4, half, width), BF16),
        scratch_shapes=[pltpu.SemaphoreType.DMA, pltpu.SemaphoreType.DMA],
        name="pair_send_" + tag, compiler_params=_params(2))(p)


def _pair_sum(tag, p, a):
    _, half, width = a.shape
    sr = SUM_ROWS[tag]
    nt = half // sr
    assert half % sr == 0

    def body(p_ref, a_ref, o_ref):
        o_ref[...] = (p_ref[...].astype(F32) + a_ref[...].astype(F32)).astype(BF16)

    spec = pl.BlockSpec((1, sr, width), lambda s, t: (s, t, 0))
    return pl.pallas_call(
        body, grid=(4, nt),
        in_specs=[pl.BlockSpec((1, sr, width), lambda s, t: (s, lax.axis_index("c") * nt + t, 0)), spec],
        out_specs=spec, out_shape=jax.ShapeDtypeStruct((4, half, width), BF16),
        name="pair_sum_" + tag, compiler_params=_params(2))(p, a)


def _chip_exchange(cs_a, cs_b):
    def body(ca_ref, cb_ref, ba_ref, bb_ref, send_sems, recv_sems):
        x, y, c, chips = _mesh_pos()
        send = []
        for j, chip in enumerate(chips):
            for n, (src, dst) in enumerate(((ca_ref, ba_ref), (cb_ref, bb_ref))):
                send.append(pltpu.make_async_remote_copy(src_ref=src.at[2 * chip[0] + chip[1]], dst_ref=dst.at[j],
                                                         send_sem=send_sems.at[2 * j + n], recv_sem=recv_sems.at[2 * j + n],
                                                         device_id=(*chip, c), device_id_type=MESH))
        for cp in send:
            cp.start()
        for cp in send:
            cp.wait_recv()
        for cp in send:
            cp.wait_send()

    return pl.pallas_call(
        body, name="chip_exchange", in_specs=[HBM, HBM], out_specs=[HBM, HBM],
        out_shape=[jax.ShapeDtypeStruct((3,) + cs.shape[1:], BF16) for cs in (cs_a, cs_b)],
        scratch_shapes=[pltpu.SemaphoreType.DMA((6,)), pltpu.SemaphoreType.DMA((6,))])(cs_a, cs_b)


def _chip_sum(tag, cs, b):
    _, half, width = cs.shape
    sr = SUM_ROWS[tag]
    nt = half // sr

    def body(c_ref, b_ref, r_ref, buf, send_sems, recv_sem, local_sems):
        x, y, c, _ = _mesh_pos()
        t = pl.program_id(0)
        slot = lax.rem(t, 2)

        def copies(k, tile):
            rows = pl.ds(pl.multiple_of(c * half + tile * sr, 8), sr)
            keep = pltpu.make_async_copy(buf.at[k], r_ref.at[rows], local_sems.at[k])
            give = pltpu.make_async_remote_copy(src_ref=buf.at[k], dst_ref=r_ref.at[rows], send_sem=send_sems.at[k],
                                                recv_sem=recv_sem, device_id=(x, y, 1 - c), device_id_type=MESH)
            return keep, give

        @pl.when(t >= 2)
        def _():
            keep, give = copies(slot, t - 2)
            keep.wait()
            give.wait_send()

        buf[slot] = (c_ref[0].astype(F32) + b_ref[0].astype(F32)) + (b_ref[1].astype(F32) + b_ref[2].astype(F32))
        keep, give = copies(slot, t)
        keep.start()
        give.start()

        @pl.when(t == nt - 1)
        def _():
            for back in (1, 0):
                keep, give = copies(lax.rem(t - back, 2), t - back)
                keep.wait()
                give.wait_send()
            landed = r_ref.at[_half(2 * half, 1 - c)]
            pltpu.make_async_remote_copy(src_ref=landed, dst_ref=landed, send_sem=send_sems.at[0], recv_sem=recv_sem,
                                         device_id=(x, y, c), device_id_type=MESH).wait_recv()

    return pl.pallas_call(
        body, grid=(nt,),
        in_specs=[pl.BlockSpec((1, sr, width), lambda t: (2 * lax.axis_index("x") + lax.axis_index("y"), t, 0)),
                  pl.BlockSpec((3, sr, width), lambda t: (0, t, 0))],
        out_specs=HBM, out_shape=jax.ShapeDtypeStruct((2 * half, width), F32),
        scratch_shapes=[pltpu.VMEM((2, sr, width), F32), pltpu.SemaphoreType.DMA((2,)), pltpu.SemaphoreType.DMA,
                        pltpu.SemaphoreType.DMA((2,))],
        name="chip_sum_" + tag, compiler_params=_params(1))(cs, b)


def _pair_sums_a(tag, p5, b0, nb):
    tile5 = (1, 1, 1, A_HALF, FF_BLK)
    tile4 = (1, 1, A_HALF, FF_BLK)

    def send_body(p_ref, a_ref, send_sem, recv_sem):
        x, y, c, _ = _mesh_pos()
        s, t = pl.program_id(0), pl.program_id(1)
        give = pltpu.make_async_remote_copy(src_ref=p_ref.at[0, 0, 0], dst_ref=a_ref.at[s, t], send_sem=send_sem,
                                            recv_sem=recv_sem, device_id=(x, y, 1 - c), device_id_type=MESH)
        give.start()
        give.wait_send()

        @pl.when((s == 3) & (t == nb - 1))
        def _():
            pltpu.make_async_remote_copy(src_ref=a_ref, dst_ref=a_ref, send_sem=send_sem, recv_sem=recv_sem,
                                         device_id=(x, y, c), device_id_type=MESH).wait_recv()

    shape = jax.ShapeDtypeStruct((4, nb, A_HALF, FF_BLK), BF16)
    recv = pl.pallas_call(
        send_body, grid=(4, nb),
        in_specs=[pl.BlockSpec(tile5, lambda s, t: (s, b0 + t, 1 - lax.axis_index("c"), 0, 0))],
        out_specs=HBM, out_shape=shape, scratch_shapes=[pltpu.SemaphoreType.DMA, pltpu.SemaphoreType.DMA],
        name="pair_send_a" + tag, compiler_params=_params(2))(p5)

    def sum_body(p_ref, a_ref, o_ref):
        o_ref[0, 0] = (p_ref[0, 0, 0].astype(F32) + a_ref[0, 0].astype(F32)).astype(BF16)

    spec = pl.BlockSpec(tile4, lambda s, t: (s, t, 0, 0))
    return pl.pallas_call(
        sum_body, grid=(4, nb),
        in_specs=[pl.BlockSpec(tile5, lambda s, t: (s, b0 + t, lax.axis_index("c"), 0, 0)), spec],
        out_specs=spec, out_shape=shape, name="pair_sum_a" + tag, compiler_params=_params(2))(p5, recv)


def _chip_sum_a(tag, cs, frm, b0, r_prev=None):
    nb = cs.shape[1]
    sr = SUM_ROWS["a"]
    per = A_HALF // sr
    nt = nb * per

    def body(*refs):
        c_ref, b_ref = refs[:2]
        r_ref, buf, send_sems, recv_sem, local_sems = refs[-5:]
        x, y, c, _ = _mesh_pos()
        t = pl.program_id(0)
        slot = lax.rem(t, 2)

        def copies(k, tile):
            dst = r_ref.at[b0 + tile // per, c, pl.ds(pl.multiple_of(lax.rem(tile, per) * sr, 8), sr)]
            keep = pltpu.make_async_copy(buf.at[k], dst, local_sems.at[k])
            give = pltpu.make_async_remote_copy(src_ref=buf.at[k], dst_ref=dst, send_sem=send_sems.at[k],
                                                recv_sem=recv_sem, device_id=(x, y, 1 - c), device_id_type=MESH)
            return keep, give

        @pl.when(t >= 2)
        def _():
            keep, give = copies(slot, t - 2)
            keep.wait()
            give.wait_send()

        buf[slot] = ((c_ref[0, 0].astype(F32) + b_ref[0, 0].astype(F32))
                     + (b_ref[1, 0].astype(F32) + b_ref[2, 0].astype(F32)))
        keep, give = copies(slot, t)
        keep.start()
        give.start()

        @pl.when(t == nt - 1)
        def _():
            for back in (1, 0):
                keep, give = copies(lax.rem(t - back, 2), t - back)
                keep.wait()
                give.wait_send()
            landed = r_ref.at[pl.ds(b0, nb), 1 - c]
            pltpu.make_async_remote_copy(src_ref=landed, dst_ref=landed, send_sem=send_sems.at[0], recv_sem=recv_sem,
                                         device_id=(x, y, c), device_id_type=MESH).wait_recv()

    in_specs = [pl.BlockSpec((1, 1, sr, FF_BLK),
                             lambda t: (2 * lax.axis_index("x") + lax.axis_index("y"), t // per, t % per, 0)),
                pl.BlockSpec((3, 1, sr, FF_BLK), lambda t: (0, t // per, t % per, 0))]
    args = [cs, frm]
    if r_prev is not None:
        in_specs.append(HBM)
        args.append(r_prev)
    return pl.pallas_call(
        body, grid=(nt,), in_specs=in_specs, out_specs=HBM,
        out_shape=jax.ShapeDtypeStruct((4, 2, A_HALF, FF_BLK), F32),
        scratch_shapes=[pltpu.VMEM((2, sr, FF_BLK), F32), pltpu.SemaphoreType.DMA((2,)), pltpu.SemaphoreType.DMA,
                        pltpu.SemaphoreType.DMA((2,))],
        input_output_aliases={2: 0} if r_prev is not None else {},
        name="chip_sum_a" + tag, compiler_params=_params(1))(*args)


def _reduce_scatter(cs_early, from_early, late):
    (cs_late, cs_b), (from_late, from_b) = late
    red = _chip_sum_a("1", cs_early, from_early, 1)
    red = _chip_sum_a("0", cs_late, from_late, 0, red)
    return red.reshape(A_ROWS, FF_BLK), _chip_sum("b", cs_b, from_b)


SMALL_ROWS = 24


def _all_reduce_small(v):
    def body(v_ref, o_ref, all_ref, send_sems, recv_sems):
        x, y, c, _ = _mesh_pos()
        me = 4 * x + 2 * y + c
        all_ref[me] = v_ref[...]
        peers = [(x ^ ((k >> 2) & 1), y ^ ((k >> 1) & 1), c ^ (k & 1)) for k in range(1, 8)]
        idx = lambda p: 4 * p[0] + 2 * p[1] + p[2]
        send = [pltpu.make_async_remote_copy(src_ref=v_ref, dst_ref=all_ref.at[me], send_sem=send_sems.at[k],
                                             recv_sem=recv_sems.at[k], device_id=p, device_id_type=MESH)
                for k, p in enumerate(peers)]
        for cp in send:
            cp.start()
        for k, p in enumerate(peers):
            pltpu.make_async_remote_copy(src_ref=v_ref, dst_ref=all_ref.at[idx(p)], send_sem=send_sems.at[k],
                                         recv_sem=recv_sems.at[k], device_id=p, device_id_type=MESH).wait_recv()
        for cp in send:
            cp.wait_send()
        acc = all_ref[0]
        for d in range(1, 8):
            acc = acc + all_ref[d]
        o_ref[...] = acc

    vm = pl.BlockSpec(memory_space=pltpu.VMEM)
    return pl.pallas_call(
        body, name="all_reduce_small", in_specs=[vm], out_specs=vm,
        out_shape=jax.ShapeDtypeStruct((SMALL_ROWS, 1024), F32),
        scratch_shapes=[pltpu.VMEM((8, SMALL_ROWS, 1024), F32), pltpu.SemaphoreType.DMA((7,)),
                        pltpu.SemaphoreType.DMA((7,))],)(v)


def _adamw(name, w, g, m, v):
    rows, cols = w.shape
    tr = rows
    if rows * cols > 400_000:
        tr = max(t for t in range(8, rows, 8) if rows % t == 0 and t * cols <= 400_000)

    def body(w_ref, g_ref, m_ref, v_ref, d_ref, nm_ref, nv_ref):
        gv = g_ref[...]
        m_new = ADAM_B1 * m_ref[...] + (1.0 - ADAM_B1) * gv
        v_new = ADAM_B2 * v_ref[...] + (1.0 - ADAM_B2) * (gv * gv)
        m_hat = m_new / (1.0 - ADAM_B1 ** ADAM_STEP)
        v_hat = v_new / (1.0 - ADAM_B2 ** ADAM_STEP)
        d_ref[...] = -ADAM_LR * (m_hat / (jnp.sqrt(v_hat) + ADAM_EPS) + ADAM_WD * w_ref[...])
        nm_ref[...] = m_new
        nv_ref[...] = v_new

    spec = pl.BlockSpec((tr, cols), lambda i: (i, 0))
    sds = jax.ShapeDtypeStruct((rows, cols), F32)
    return pl.pallas_call(body, grid=(rows // tr,), in_specs=[spec] * 4, out_specs=[spec] * 3, out_shape=[sds] * 3,
                          name=name, compiler_params=_params(1))(w, g, m, v)


WEIGHTS = ("ffn1_norm", "ffn1_w_gu", "ffn1_w_down", "mix_norm", "ffn2_norm", "ffn2_w_gu", "ffn2_w_down",
           "a_w_in", "a_w_conv", "a_A_log", "a_dt_bias", "a_out_norm", "a_w_out",
           "b_w_in", "b_b_in", "b_sinks", "b_w_out", "b_b_out", "final_norm")
SMALL_SLOTS = {"ffn1_norm": (0, 2048), "mix_norm": (2048, 2048), "ffn2_norm": (4096, 2048), "final_norm": (6144, 1024),
               "a_A_log": (7168, 8), "a_dt_bias": (7296, 8), "a_out_norm": (7424, 128), "b_sinks": (7552, 16),
               "loss": (7680, 1)}
SMALL_SHARDED = {"a_w_conv": (8192, 8192, (4,), 3072), "b_b_in": (20480, 11264, (), 1536), "b_b_out": (22016, 11648, (), 1024)}
DEV_SMALL_ROWS = 12


def _pack_rows(parts):
    rows = []
    for p in parts:
        r = p.reshape(p.shape[0], -1, 1024)
        rows.append(jnp.pad(r, ((0, 0), (0, -r.shape[1] % PACK_TILE), (0, 0))))
    rows.append(jnp.zeros((parts[0].shape[0], PACK_ROWS - PACK_USED, 1024), parts[0].dtype))
    return jnp.concatenate(rows, axis=1)


def _place(vec, off, a):
    return lax.dynamic_update_slice(vec, a.reshape(-1).astype(F32), (off,))


def kernel(x, ffn1_norm, ffn1_w_gu, ffn1_w_down, mix_norm, ffn2_norm, ffn2_w_gu, ffn2_w_down, a_w_in, a_w_conv, a_A_log, a_dt_bias, a_out_norm, a_w_out, b_w_in, b_b_in, b_sinks, b_w_out, b_b_out, final_norm, loss_target, m_ffn1_norm, m_ffn1_w_gu, m_ffn1_w_down, m_mix_norm, m_ffn2_norm, m_ffn2_w_gu, m_ffn2_w_down, m_a_w_in, m_a_w_conv, m_a_A_log, m_a_dt_bias, m_a_out_norm, m_a_w_out, m_b_w_in, m_b_b_in, m_b_sinks, m_b_w_out, m_b_b_out, m_final_norm, v_ffn1_norm, v_ffn1_w_gu, v_ffn1_w_down, v_mix_norm, v_ffn2_norm, v_ffn2_w_gu, v_ffn2_w_down, v_a_w_in, v_a_w_conv, v_a_A_log, v_a_dt_bias, v_a_out_norm, v_a_w_out, v_b_w_in, v_b_b_in, v_b_sinks, v_b_w_out, v_b_b_out, v_final_norm):
    w = dict(zip(WEIGHTS, (ffn1_norm, ffn1_w_gu, ffn1_w_down, mix_norm, ffn2_norm, ffn2_w_gu, ffn2_w_down, a_w_in, a_w_conv,
                           a_A_log, a_dt_bias, a_out_norm, a_w_out, b_w_in, b_b_in, b_sinks, b_w_out, b_b_out, final_norm)))
    m = dict(zip(WEIGHTS, (m_ffn1_norm, m_ffn1_w_gu, m_ffn1_w_down, m_mix_norm, m_ffn2_norm, m_ffn2_w_gu, m_ffn2_w_down,
                           m_a_w_in, m_a_w_conv, m_a_A_log, m_a_dt_bias, m_a_out_norm, m_a_w_out, m_b_w_in, m_b_b_in,
                           m_b_sinks, m_b_w_out, m_b_b_out, m_final_norm)))
    v = dict(zip(WEIGHTS, (v_ffn1_norm, v_ffn1_w_gu, v_ffn1_w_down, v_mix_norm, v_ffn2_norm, v_ffn2_w_gu, v_ffn2_w_down,
                           v_a_w_in, v_a_w_conv, v_a_A_log, v_a_dt_bias, v_a_out_norm, v_a_w_out, v_b_w_in, v_b_b_in,
                           v_b_sinks, v_b_w_out, v_b_b_out, v_final_norm)))
    chip = 2 * lax.axis_index("x") + lax.axis_index("y")

    big_a4 = jnp.concatenate([w["ffn1_w_gu"], w["ffn2_w_gu"]], axis=0).astype(BF16).reshape(4, 2, A_HALF, FF_BLK)
    big_b = _pack_rows([w[n].astype(BF16).reshape(1, -1) for n, _ in PACK])[0]
    small = jnp.zeros((4096,), F32)
    small = _place(small, 0, w["a_w_conv"])
    small = _place(small, 3072, w["b_b_in"])
    small = _place(small, 3456, w["b_b_out"]).reshape(SMALL_SHARD)
    ga = _fill_a(0, big_a4, _gather_chips(big_a4)).reshape(4, 4, D_MODEL, FF_BLK)
    offs, o = {}, 0
    for n, r in PACK:
        offs[n] = (o, r)
        o += -(-r // PACK_TILE) * PACK_TILE

    def finish_weights(rb, rs):
        gb = _gather_fill("b", big_b, rb)
        blk = lambda n: gb[:, offs[n][0]:offs[n][0] + offs[n][1]]
        gsf = lax.dynamic_update_slice(jnp.zeros((4, 4096), F32), small.reshape(1, 4096), (chip, 0))
        for j, other in enumerate((chip ^ 2, chip ^ 1, chip ^ 3)):
            gsf = lax.dynamic_update_slice(gsf, rs[j].reshape(1, 4096), (other, 0))
        return {
            "w_down": gb[:, 0:2816].reshape(4, 4, 704, 1024).transpose(1, 0, 2, 3).reshape(4, 2, FF_BLK, 1024),
            "a_w_in": jnp.pad(blk("a_w_in").reshape(4, 1024, 1028).transpose(1, 0, 2).reshape(1024, A_IN_COLS),
                              ((0, 0), (0, A_IN_PAD - A_IN_COLS))),
            "a_w_out": blk("a_w_out").reshape(1024, 1024),
            "b_w_in": blk("b_w_in").reshape(4, 1024, 384).transpose(1, 0, 2).reshape(1024, 1536),
            "b_w_out": blk("b_w_out").reshape(1024, 1024),
            "a_w_conv": gsf[:, 0:3072].reshape(4, 4, 768).transpose(1, 0, 2).reshape(4, 3072),
            "b_b_in": gsf[:, 3072:3456].reshape(1, 1536),
            "b_b_out": gsf[:, 3456:3712].reshape(1, 1024)}

    def pack_b(g):
        down = g["w_down"].reshape(4, 4, 704, 1024).transpose(1, 0, 2, 3)
        parts = [down[:, 0:2], down[:, 2:4],
                 g["a_w_in"][:, :A_IN_COLS].reshape(1024, 4, 1028).transpose(1, 0, 2), g["a_w_out"].reshape(4, 256, 1024),
                 g["b_w_in"].reshape(1024, 4, 384).transpose(1, 0, 2), g["b_w_out"].reshape(4, 256, 1024)]
        return _pack_rows([a.astype(BF16).reshape(4, -1) for a in parts])

    W = {n: w[n] for n in ("ffn1_norm", "ffn2_norm", "mix_norm", "a_out_norm")}
    W["ga"] = ga
    W["a_gate_p"] = jnp.pad(jnp.concatenate([-jnp.exp(w["a_A_log"]), w["a_dt_bias"]], axis=0), ((0, 0), (8, 112)))
    W["b_sinks"] = jnp.broadcast_to(w["b_sinks"][0][:, None, None], (B_HEADS, 1, 128))
    W["final_norm"] = w["final_norm"][None]

    comm = {"a4": big_a4, "big_b": big_b, "small": small, "finish": finish_weights, "pack_b": pack_b}
    loss_p, dx, g = _local_step(x[0], loss_target[0], W, comm)

    red_a, red_b = _reduce_scatter(g["cs_early"], g["from_early"], g["late"])
    grads = {n: red_b[offs[n][0]:offs[n][0] + offs[n][1]].reshape(w[n].shape) for n, _ in PACK}
    grads["ffn1_w_gu"] = red_a[:A_ROWS // 2].reshape(w["ffn1_w_gu"].shape)
    grads["ffn2_w_gu"] = red_a[A_ROWS // 2:].reshape(w["ffn2_w_gu"].shape)

    sv = jnp.zeros((SMALL_ROWS * 1024,), F32)
    small_g = {"ffn1_norm": g["ffn1_norm"], "mix_norm": g["mix_norm"], "ffn2_norm": g["ffn2_norm"], "final_norm": g["final_norm"],
               "a_A_log": g["a_gate_p"][0, 8:16], "a_dt_bias": g["a_gate_p"][1, 8:16], "a_out_norm": g["a_out_norm"],
               "b_sinks": g["b_sinks"], "loss": loss_p[0, 0:1]}
    for n, (off, _) in SMALL_SLOTS.items():
        sv = _place(sv, off, small_g[n])
    for n, (off, _, _, _) in SMALL_SHARDED.items():
        sv = _place(sv, off, g[n])
    tot = _all_reduce_small(sv.reshape(SMALL_ROWS, 1024)).reshape(-1)
    for n, (off, size) in SMALL_SLOTS.items():
        if n != "loss":
            grads[n] = tot[off:off + size].reshape(w[n].shape)
    for n, (off, _, lead, last) in SMALL_SHARDED.items():
        full = tot[off:off + (lead[0] if lead else 1) * last].reshape(lead + (last,))
        width = last // 4
        grads[n] = lax.dynamic_slice_in_dim(full, chip * width, width, axis=-1).reshape(w[n].shape)
    loss = tot[SMALL_SLOTS["loss"][0]]

    delta, new_m, new_v = {}, {}, {}
    for n in ("ffn1_w_gu", "ffn2_w_gu") + tuple(n for n, _ in PACK):
        two_d = lambda a: a.reshape(-1, a.shape[-1])
        d, nm, nv = _adamw("adamw_" + n, two_d(w[n]), two_d(grads[n]), two_d(m[n]), two_d(v[n]))
        delta[n], new_m[n], new_v[n] = d.reshape(w[n].shape), nm.reshape(w[n].shape), nv.reshape(w[n].shape)

    def dev_small(src):
        vec = jnp.zeros((DEV_SMALL_ROWS * 1024,), F32)
        for n, (off, _) in SMALL_SLOTS.items():
            if n != "loss":
                vec = _place(vec, off, src[n])
        for n, (_, off, _, _) in SMALL_SHARDED.items():
            vec = _place(vec, off, src[n])
        return vec.reshape(DEV_SMALL_ROWS, 1024)

    sd, sm, svv = _adamw("adamw_small", dev_small(w), dev_small(grads), dev_small(m), dev_small(v))
    for n in WEIGHTS:
        if n in SMALL_SLOTS:
            off, size = SMALL_SLOTS[n]
        elif n in SMALL_SHARDED:
            off, size = SMALL_SHARDED[n][1], w[n].size
        else:
            continue
        for dst, src in ((delta, sd), (new_m, sm), (new_v, svv)):
            dst[n] = src.reshape(-1)[off:off + size].reshape(w[n].shape)

    return (loss, dx[None], *[grads[n] for n in WEIGHTS], *[delta[n] for n in WEIGHTS],
            *[new_m[n] for n in WEIGHTS], *[new_v[n] for n in WEIGHTS])
```
